```python
import jax, jax.numpy as jnp
from jax import lax
import numpy as np

D_MODEL = 2048
BATCH = 8
SEQ = 2048
DEPTH = 2

N_MIXERS = 2
N_FOX = (DEPTH + 1) // 2
N_SWA = DEPTH // 2
Q_BLOCK = 128

FOX_HEADS = 16
FOX_HEAD_DIM = D_MODEL // FOX_HEADS
FOX_WIDTH = FOX_HEADS * FOX_HEAD_DIM

SWA_HEAD_DIM = 64
SWA_Q_HEADS = D_MODEL // SWA_HEAD_DIM
SWA_KV_HEADS = SWA_Q_HEADS // 8
SWA_GROUP = SWA_Q_HEADS // SWA_KV_HEADS
SWA_WINDOW = 128
ROPE_THETA = 500000.0
ROPE_DIM = SWA_HEAD_DIM // 4

D_FF = 5632
CONV_WIDTH = 3

DEEPNORM_ALPHA = (2.0 * DEPTH) ** 0.25
DEEPNORM_BETA = (8.0 * DEPTH) ** -0.25
LN_EPS = 1e-5
ADA_SCALE = 0.2
MAX_POS_OFFSET = 4096

kernel_name = "hybrid_fox_swa_sink_convffn_deepnorm_adaln"


def layer_norm(x, g, b):
    xf = x.astype(jnp.float32)
    mu = jnp.mean(xf, axis=-1, keepdims=True)
    var = jnp.mean(jnp.square(xf - mu), axis=-1, keepdims=True)
    y = (xf - mu) * lax.rsqrt(var + LN_EPS)
    return (y * g.astype(jnp.float32) + b.astype(jnp.float32)).astype(x.dtype)


def rope_partial(t, pos):
    inv_freq = ROPE_THETA ** (-jnp.arange(0, ROPE_DIM, 2, dtype=jnp.float32) / ROPE_DIM)
    ang = pos.astype(jnp.float32)[..., None] * inv_freq
    cos = jnp.cos(ang)[:, :, None, :]
    sin = jnp.sin(ang)[:, :, None, :]
    tr = t[..., :ROPE_DIM].astype(jnp.float32)
    t1, t2 = tr[..., :ROPE_DIM // 2], tr[..., ROPE_DIM // 2:]
    rot = jnp.concatenate([t1 * cos - t2 * sin, t2 * cos + t1 * sin], axis=-1)
    return jnp.concatenate([rot.astype(t.dtype), t[..., ROPE_DIM:]], axis=-1)


def fox_attention(h, w_in, b_f, w_o):
    B, S, _ = h.shape
    H, dh = FOX_HEADS, FOX_HEAD_DIM
    proj = h @ w_in
    q = proj[..., :FOX_WIDTH].reshape(B, S, H, dh)
    k = proj[..., FOX_WIDTH:2 * FOX_WIDTH].reshape(B, S, H, dh)
    v = proj[..., 2 * FOX_WIDTH:3 * FOX_WIDTH].reshape(B, S, H, dh)
    f_logit = proj[..., 3 * FOX_WIDTH:] + b_f
    log_f = jax.nn.log_sigmoid(f_logit.astype(jnp.float32))
    cum = jnp.cumsum(log_f, axis=1).transpose(0, 2, 1)
    nb = S // Q_BLOCK
    q_blocks = q.reshape(B, nb, Q_BLOCK, H, dh).transpose(1, 0, 2, 3, 4)
    cq_blocks = cum.reshape(B, H, nb, Q_BLOCK).transpose(2, 0, 1, 3)
    key_pos = jnp.arange(S)
    scale = FOX_HEAD_DIM ** -0.5

    def one_block(args):
        qb, cqb, bi = args
        s = jnp.einsum('bqhd,bkhd->bhqk', qb, k).astype(jnp.float32) * scale
        s = s + cqb[..., None] - cum[:, :, None, :]
        q_pos = bi * Q_BLOCK + jnp.arange(Q_BLOCK)
        causal = key_pos[None, :] <= q_pos[:, None]
        s = jnp.where(causal, s, -jnp.inf)
        p = jax.nn.softmax(s, axis=-1).astype(v.dtype)
        return jnp.einsum('bhqk,bkhd->bqhd', p, v)

    o = lax.map(one_block, (q_blocks, cq_blocks, jnp.arange(nb)))
    o = o.transpose(1, 0, 2, 3, 4).reshape(B, S, FOX_WIDTH)
    return o @ w_o


def swa_attention(h, pos, w_in, sinks, w_o):
    B, S, _ = h.shape
    Hq, Hk, G, dh = SWA_Q_HEADS, SWA_KV_HEADS, SWA_GROUP, SWA_HEAD_DIM
    proj = h @ w_in
    q = proj[..., :Hq * dh].reshape(B, S, Hq, dh)
    k = proj[..., Hq * dh:(Hq + Hk) * dh].reshape(B, S, Hk, dh)
    v = proj[..., (Hq + Hk) * dh:].reshape(B, S, Hk, dh)
    q = rope_partial(q, pos)
    k = rope_partial(k, pos)
    nb = S // Q_BLOCK
    qb = q.reshape(B, nb, Q_BLOCK, Hk, G, dh)

    def band(t):
        tb = t.reshape(B, nb, Q_BLOCK, Hk, dh)
        prev = jnp.concatenate([jnp.zeros_like(tb[:, :1]), tb[:, :-1]], axis=1)
        return jnp.concatenate([prev, tb], axis=2)

    kb, vb = band(k), band(v)
    s = jnp.einsum('bnqhgd,bnkhd->bnhgqk', qb, kb).astype(jnp.float32) * (dh ** -0.5)
    qi = jnp.arange(Q_BLOCK)[:, None]
    kj = jnp.arange(2 * Q_BLOCK)[None, :]
    rel = qi + Q_BLOCK - kj
    key_abs = (jnp.arange(nb) * Q_BLOCK)[:, None] - Q_BLOCK + kj
    mask = (rel >= 0)[None] & (rel < SWA_WINDOW)[None] & (key_abs[:, None, :] >= 0)
    s = jnp.where(mask[None, :, None, None], s, -jnp.inf)
    sink = jnp.broadcast_to(sinks.astype(jnp.float32).reshape(1, 1, Hk, G, 1, 1), s.shape[:-1] + (1,))
    p = jax.nn.softmax(jnp.concatenate([s, sink], axis=-1), axis=-1)[..., :-1].astype(v.dtype)
    o = jnp.einsum('bnhgqk,bnkhd->bnqhgd', p, vb).reshape(B, S, Hq * dh)
    return o @ w_o


def conv_ffn(h, w_up, conv_w, conv_b, w_down):
    S = h.shape[1]
    u = h @ w_up
    up = jnp.pad(u, ((0, 0), (CONV_WIDTH - 1, 0), (0, 0)))
    u = sum(up[:, j:j + S] * conv_w[j] for j in range(CONV_WIDTH)) + conv_b
    g, val = u[..., :D_FF], u[..., D_FF:]
    return (jax.nn.silu(g) * val) @ w_down


def _fwd_setup_inputs(seed: int = 0) -> dict:
    key = jax.random.key(seed)
    ks = jax.random.split(key, 20)
    f32 = jnp.float32
    n = lambda k, shape, s: (jax.random.normal(k, shape, f32) * s)
    D = D_MODEL
    x = n(ks[0], (BATCH, SEQ, D), 1.0)
    c = n(ks[1], (BATCH, D), 1.0)
    offset = jax.random.randint(ks[2], (BATCH, 1), 0, MAX_POS_OFFSET, dtype=jnp.int32)
    positions = (offset + jnp.arange(SEQ, dtype=jnp.int32)[None, :]).astype(jnp.int32)
    fox_w_in = n(ks[3], (N_FOX, D, 3 * FOX_WIDTH + FOX_HEADS), D ** -0.5)
    fox_b_f = n(ks[4], (N_FOX, FOX_HEADS), 0.1)
    fox_w_o = n(ks[5], (N_FOX, FOX_WIDTH, D), FOX_WIDTH ** -0.5 * DEEPNORM_BETA)
    swa_w_in = n(ks[6], (N_SWA, D, (SWA_Q_HEADS + 2 * SWA_KV_HEADS) * SWA_HEAD_DIM), D ** -0.5)
    swa_sinks = n(ks[7], (N_SWA, SWA_Q_HEADS), 0.5)
    swa_w_o = n(ks[8], (N_SWA, SWA_Q_HEADS * SWA_HEAD_DIM, D), (SWA_Q_HEADS * SWA_HEAD_DIM) ** -0.5 * DEEPNORM_BETA)
    ada_w = n(ks[9], (DEPTH, D, 6 * D), ADA_SCALE * D ** -0.5)
    ada_b = n(ks[10], (DEPTH, 6 * D), 0.02)
    ffn_w_up = n(ks[11], (DEPTH, D, 2 * D_FF), D ** -0.5)
    ffn_conv_w = n(ks[12], (DEPTH, CONV_WIDTH, 2 * D_FF), CONV_WIDTH ** -0.5)
    ffn_conv_b = n(ks[13], (DEPTH, 2 * D_FF), 0.02)
    ffn_w_down = n(ks[14], (DEPTH, D_FF, D), D_FF ** -0.5 * DEEPNORM_BETA)
    ln_mix_g = 1.0 + n(ks[15], (DEPTH, D), 0.02)
    ln_mix_b = n(ks[16], (DEPTH, D), 0.02)
    ln_ffn_g = 1.0 + n(ks[17], (DEPTH, D), 0.02)
    ln_ffn_b = n(ks[18], (DEPTH, D), 0.02)
    return {"x": x, "c": c, "positions": positions,
            "fox_w_in": fox_w_in, "fox_b_f": fox_b_f, "fox_w_o": fox_w_o,
            "swa_w_in": swa_w_in, "swa_sinks": swa_sinks, "swa_w_o": swa_w_o,
            "ada_w": ada_w, "ada_b": ada_b,
            "ffn_w_up": ffn_w_up, "ffn_conv_w": ffn_conv_w, "ffn_conv_b": ffn_conv_b, "ffn_w_down": ffn_w_down,
            "ln_mix_g": ln_mix_g, "ln_mix_b": ln_mix_b, "ln_ffn_g": ln_ffn_g, "ln_ffn_b": ln_ffn_b}


def _fwd_reference(x, c, positions, fox_w_in, fox_b_f, fox_w_o, swa_w_in, swa_sinks, swa_w_o,
              ada_w, ada_b, ffn_w_up, ffn_conv_w, ffn_conv_b, ffn_w_down,
              ln_mix_g, ln_mix_b, ln_ffn_g, ln_ffn_b):
    c_act = jax.nn.silu(c)
    for i in range(DEPTH):
        mod = c_act @ ada_w[i] + ada_b[i]
        sh1, sc1, g1, sh2, sc2, g2 = jnp.split(mod[:, None, :], 6, axis=-1)
        h = x * (1.0 + sc1) + sh1
        j = i // N_MIXERS
        if i % N_MIXERS == 0:
            y = fox_attention(h, fox_w_in[j], fox_b_f[j], fox_w_o[j])
        else:
            y = swa_attention(h, positions, swa_w_in[j], swa_sinks[j], swa_w_o[j])
        x = layer_norm(DEEPNORM_ALPHA * x + (1.0 + g1) * y, ln_mix_g[i], ln_mix_b[i])
        h = x * (1.0 + sc2) + sh2
        y = conv_ffn(h, ffn_w_up[i], ffn_conv_w[i], ffn_conv_b[i], ffn_w_down[i])
        x = layer_norm(DEEPNORM_ALPHA * x + (1.0 + g2) * y, ln_ffn_g[i], ln_ffn_b[i])
    return x


import jax as _jax
import jax.numpy as _jnp

TWIN_FORMAT = 'train_step'
FWD_PARAMS = ['x', 'c', 'positions', 'fox_w_in', 'fox_b_f', 'fox_w_o', 'swa_w_in', 'swa_sinks', 'swa_w_o', 'ada_w', 'ada_b', 'ffn_w_up', 'ffn_conv_w', 'ffn_conv_b', 'ffn_w_down', 'ln_mix_g', 'ln_mix_b', 'ln_ffn_g', 'ln_ffn_b']
TWIN_WEIGHTS = ['fox_w_in', 'fox_b_f', 'fox_w_o', 'swa_w_in', 'swa_sinks', 'swa_w_o', 'ada_w', 'ada_b', 'ffn_w_up', 'ffn_conv_w', 'ffn_conv_b', 'ffn_w_down', 'ln_mix_g', 'ln_mix_b', 'ln_ffn_g', 'ln_ffn_b']
TWIN_DIFF_INPUT = 'x'
TWIN_INPUTS = ['x', 'c', 'positions', 'fox_w_in', 'fox_b_f', 'fox_w_o', 'swa_w_in', 'swa_sinks', 'swa_w_o', 'ada_w', 'ada_b', 'ffn_w_up', 'ffn_conv_w', 'ffn_conv_b', 'ffn_w_down', 'ln_mix_g', 'ln_mix_b', 'ln_ffn_g', 'ln_ffn_b', 'loss_target', 'm_fox_w_in', 'm_fox_b_f', 'm_fox_w_o', 'm_swa_w_in', 'm_swa_sinks', 'm_swa_w_o', 'm_ada_w', 'm_ada_b', 'm_ffn_w_up', 'm_ffn_conv_w', 'm_ffn_conv_b', 'm_ffn_w_down', 'm_ln_mix_g', 'm_ln_mix_b', 'm_ln_ffn_g', 'm_ln_ffn_b', 'v_fox_w_in', 'v_fox_b_f', 'v_fox_w_o', 'v_swa_w_in', 'v_swa_sinks', 'v_swa_w_o', 'v_ada_w', 'v_ada_b', 'v_ffn_w_up', 'v_ffn_conv_w', 'v_ffn_conv_b', 'v_ffn_w_down', 'v_ln_mix_g', 'v_ln_mix_b', 'v_ln_ffn_g', 'v_ln_ffn_b']
TWIN_OUTPUTS = ['loss', 'grad_x', 'grad_fox_w_in', 'grad_fox_b_f', 'grad_fox_w_o', 'grad_swa_w_in', 'grad_swa_sinks', 'grad_swa_w_o', 'grad_ada_w', 'grad_ada_b', 'grad_ffn_w_up', 'grad_ffn_conv_w', 'grad_ffn_conv_b', 'grad_ffn_w_down', 'grad_ln_mix_g', 'grad_ln_mix_b', 'grad_ln_ffn_g', 'grad_ln_ffn_b', 'delta_fox_w_in', 'delta_fox_b_f', 'delta_fox_w_o', 'delta_swa_w_in', 'delta_swa_sinks', 'delta_swa_w_o', 'delta_ada_w', 'delta_ada_b', 'delta_ffn_w_up', 'delta_ffn_conv_w', 'delta_ffn_conv_b', 'delta_ffn_w_down', 'delta_ln_mix_g', 'delta_ln_mix_b', 'delta_ln_ffn_g', 'delta_ln_ffn_b', 'new_m_fox_w_in', 'new_m_fox_b_f', 'new_m_fox_w_o', 'new_m_swa_w_in', 'new_m_swa_sinks', 'new_m_swa_w_o', 'new_m_ada_w', 'new_m_ada_b', 'new_m_ffn_w_up', 'new_m_ffn_conv_w', 'new_m_ffn_conv_b', 'new_m_ffn_w_down', 'new_m_ln_mix_g', 'new_m_ln_mix_b', 'new_m_ln_ffn_g', 'new_m_ln_ffn_b', 'new_v_fox_w_in', 'new_v_fox_b_f', 'new_v_fox_w_o', 'new_v_swa_w_in', 'new_v_swa_sinks', 'new_v_swa_w_o', 'new_v_ada_w', 'new_v_ada_b', 'new_v_ffn_w_up', 'new_v_ffn_conv_w', 'new_v_ffn_conv_b', 'new_v_ffn_w_down', 'new_v_ln_mix_g', 'new_v_ln_mix_b', 'new_v_ln_ffn_g', 'new_v_ln_ffn_b']
TWIN_LEAF_KINDS = {'loss': 'loss', 'grad_x': 'grad_x', 'grad_fox_w_in': 'grad_w', 'grad_fox_b_f': 'grad_w', 'grad_fox_w_o': 'grad_w', 'grad_swa_w_in': 'grad_w', 'grad_swa_sinks': 'grad_w', 'grad_swa_w_o': 'grad_w', 'grad_ada_w': 'grad_w', 'grad_ada_b': 'grad_w', 'grad_ffn_w_up': 'grad_w', 'grad_ffn_conv_w': 'grad_w', 'grad_ffn_conv_b': 'grad_w', 'grad_ffn_w_down': 'grad_w', 'grad_ln_mix_g': 'grad_w', 'grad_ln_mix_b': 'grad_w', 'grad_ln_ffn_g': 'grad_w', 'grad_ln_ffn_b': 'grad_w', 'delta_fox_w_in': 'delta_w', 'delta_fox_b_f': 'delta_w', 'delta_fox_w_o': 'delta_w', 'delta_swa_w_in': 'delta_w', 'delta_swa_sinks': 'delta_w', 'delta_swa_w_o': 'delta_w', 'delta_ada_w': 'delta_w', 'delta_ada_b': 'delta_w', 'delta_ffn_w_up': 'delta_w', 'delta_ffn_conv_w': 'delta_w', 'delta_ffn_conv_b': 'delta_w', 'delta_ffn_w_down': 'delta_w', 'delta_ln_mix_g': 'delta_w', 'delta_ln_mix_b': 'delta_w', 'delta_ln_ffn_g': 'delta_w', 'delta_ln_ffn_b': 'delta_w', 'new_m_fox_w_in': 'new_m', 'new_m_fox_b_f': 'new_m', 'new_m_fox_w_o': 'new_m', 'new_m_swa_w_in': 'new_m', 'new_m_swa_sinks': 'new_m', 'new_m_swa_w_o': 'new_m', 'new_m_ada_w': 'new_m', 'new_m_ada_b': 'new_m', 'new_m_ffn_w_up': 'new_m', 'new_m_ffn_conv_w': 'new_m', 'new_m_ffn_conv_b': 'new_m', 'new_m_ffn_w_down': 'new_m', 'new_m_ln_mix_g': 'new_m', 'new_m_ln_mix_b': 'new_m', 'new_m_ln_ffn_g': 'new_m', 'new_m_ln_ffn_b': 'new_m', 'new_v_fox_w_in': 'new_v', 'new_v_fox_b_f': 'new_v', 'new_v_fox_w_o': 'new_v', 'new_v_swa_w_in': 'new_v', 'new_v_swa_sinks': 'new_v', 'new_v_swa_w_o': 'new_v', 'new_v_ada_w': 'new_v', 'new_v_ada_b': 'new_v', 'new_v_ffn_w_up': 'new_v', 'new_v_ffn_conv_w': 'new_v', 'new_v_ffn_conv_b': 'new_v', 'new_v_ffn_w_down': 'new_v', 'new_v_ln_mix_g': 'new_v', 'new_v_ln_mix_b': 'new_v', 'new_v_ln_ffn_g': 'new_v', 'new_v_ln_ffn_b': 'new_v'}


def _forward(args):
    return _fwd_reference(*[args[k] for k in FWD_PARAMS])


def _output_shape():
    out = _jax.eval_shape(lambda: _forward(_fwd_setup_inputs(0)))
    return out.shape, out.dtype

N_MICROBATCH = 1
ADAM_LR = 0.001
ADAM_B1 = 0.9
ADAM_B2 = 0.999
ADAM_EPS = 1e-08
ADAM_WD = 0.01
ADAM_STEP = 10
PER_EXAMPLE_BATCH_AXIS = {'x': 0, 'c': 0, 'positions': 0, 'loss_target': 0}
SHARED_INPUTS = []
_WEIGHT_DTYPES = {'fox_w_in': _jnp.float32, 'fox_b_f': _jnp.float32, 'fox_w_o': _jnp.float32, 'swa_w_in': _jnp.float32, 'swa_sinks': _jnp.float32, 'swa_w_o': _jnp.float32, 'ada_w': _jnp.float32, 'ada_b': _jnp.float32, 'ffn_w_up': _jnp.float32, 'ffn_conv_w': _jnp.float32, 'ffn_conv_b': _jnp.float32, 'ffn_w_down': _jnp.float32, 'ln_mix_g': _jnp.float32, 'ln_mix_b': _jnp.float32, 'ln_ffn_g': _jnp.float32, 'ln_ffn_b': _jnp.float32}
MOMENT_SCALE = {'fox_w_in': 1.175289e-02, 'fox_b_f': 6.051402e-02, 'fox_w_o': 3.190791e-02, 'swa_w_in': 7.399979e-03, 'swa_sinks': 3.958003e-03, 'swa_w_o': 1.170356e-02, 'ada_w': 1.263815e-02, 'ada_b': 2.642943e-02, 'ffn_w_up': 8.592772e-03, 'ffn_conv_w': 8.691374e-03, 'ffn_conv_b': 9.531665e-03, 'ffn_w_down': 2.807261e-02, 'ln_mix_g': 2.686606e-01, 'ln_mix_b': 1.461099e-01, 'ln_ffn_g': 5.665455e+00, 'ln_ffn_b': 2.954859e-01}


def _to_microbatches(a, axis):
    t = _jnp.moveaxis(a, axis, 0)
    t = t.reshape((N_MICROBATCH, t.shape[0] // N_MICROBATCH) + t.shape[1:])
    return _jnp.moveaxis(t, 1, axis + 1)


def setup_inputs(seed: int = 0) -> dict:
    inp = _fwd_setup_inputs(seed)
    key = _jax.random.fold_in(_jax.random.key(seed), 7919)
    shape, _ = _output_shape()
    out = dict(inp)
    out["loss_target"] = _jax.random.normal(_jax.random.fold_in(key, 0), shape, _jnp.float32)
    for i, name in enumerate(TWIN_WEIGHTS):
        w = inp[name].astype(_jnp.float32)
        if MOMENT_SCALE is None:
            s = _jnp.sqrt(_jnp.mean(_jnp.square(w)) + 1e-30)
        else:
            s = MOMENT_SCALE[name]
        km, kv = _jax.random.split(_jax.random.fold_in(key, i + 1))
        out[name] = w
        out["m_" + name] = s * _jax.random.normal(km, w.shape, _jnp.float32)
        out["v_" + name] = (s * s) * _jax.random.uniform(kv, w.shape, _jnp.float32, 0.5, 1.5)
    if N_MICROBATCH > 1:
        for name, axis in PER_EXAMPLE_BATCH_AXIS.items():
            out[name] = _to_microbatches(out[name], axis)
    return {'x': out['x'], 'c': out['c'], 'positions': out['positions'], 'fox_w_in': out['fox_w_in'], 'fox_b_f': out['fox_b_f'], 'fox_w_o': out['fox_w_o'], 'swa_w_in': out['swa_w_in'], 'swa_sinks': out['swa_sinks'], 'swa_w_o': out['swa_w_o'], 'ada_w': out['ada_w'], 'ada_b': out['ada_b'], 'ffn_w_up': out['ffn_w_up'], 'ffn_conv_w': out['ffn_conv_w'], 'ffn_conv_b': out['ffn_conv_b'], 'ffn_w_down': out['ffn_w_down'], 'ln_mix_g': out['ln_mix_g'], 'ln_mix_b': out['ln_mix_b'], 'ln_ffn_g': out['ln_ffn_g'], 'ln_ffn_b': out['ln_ffn_b'], 'loss_target': out['loss_target'], 'm_fox_w_in': out['m_fox_w_in'], 'm_fox_b_f': out['m_fox_b_f'], 'm_fox_w_o': out['m_fox_w_o'], 'm_swa_w_in': out['m_swa_w_in'], 'm_swa_sinks': out['m_swa_sinks'], 'm_swa_w_o': out['m_swa_w_o'], 'm_ada_w': out['m_ada_w'], 'm_ada_b': out['m_ada_b'], 'm_ffn_w_up': out['m_ffn_w_up'], 'm_ffn_conv_w': out['m_ffn_conv_w'], 'm_ffn_conv_b': out['m_ffn_conv_b'], 'm_ffn_w_down': out['m_ffn_w_down'], 'm_ln_mix_g': out['m_ln_mix_g'], 'm_ln_mix_b': out['m_ln_mix_b'], 'm_ln_ffn_g': out['m_ln_ffn_g'], 'm_ln_ffn_b': out['m_ln_ffn_b'], 'v_fox_w_in': out['v_fox_w_in'], 'v_fox_b_f': out['v_fox_b_f'], 'v_fox_w_o': out['v_fox_w_o'], 'v_swa_w_in': out['v_swa_w_in'], 'v_swa_sinks': out['v_swa_sinks'], 'v_swa_w_o': out['v_swa_w_o'], 'v_ada_w': out['v_ada_w'], 'v_ada_b': out['v_ada_b'], 'v_ffn_w_up': out['v_ffn_w_up'], 'v_ffn_conv_w': out['v_ffn_conv_w'], 'v_ffn_conv_b': out['v_ffn_conv_b'], 'v_ffn_w_down': out['v_ffn_w_down'], 'v_ln_mix_g': out['v_ln_mix_g'], 'v_ln_mix_b': out['v_ln_mix_b'], 'v_ln_ffn_g': out['v_ln_ffn_g'], 'v_ln_ffn_b': out['v_ln_ffn_b']}


def _loss(weights, diff, rest, loss_target):
    with _jax.named_scope("forward"):
        args = {**rest, TWIN_DIFF_INPUT: diff, **{k: w.astype(_WEIGHT_DTYPES[k]) for k, w in weights.items()}}
        y = _forward(args)
    with _jax.named_scope("loss_head"):
        err = _jnp.square(y.astype(_jnp.float32) - loss_target)
        return 0.5 * _jnp.sum(_jnp.mean(err, axis=-1)) if err.ndim else 0.5 * err


def _adamw(w, g, m, v):
    m = ADAM_B1 * m + (1.0 - ADAM_B1) * g
    v = ADAM_B2 * v + (1.0 - ADAM_B2) * _jnp.square(g)
    m_hat = m / (1.0 - ADAM_B1 ** ADAM_STEP)
    v_hat = v / (1.0 - ADAM_B2 ** ADAM_STEP)
    delta = -ADAM_LR * (m_hat / (_jnp.sqrt(v_hat) + ADAM_EPS) + ADAM_WD * w)
    return delta, m, v


def reference(x, c, positions, fox_w_in, fox_b_f, fox_w_o, swa_w_in, swa_sinks, swa_w_o, ada_w, ada_b, ffn_w_up, ffn_conv_w, ffn_conv_b, ffn_w_down, ln_mix_g, ln_mix_b, ln_ffn_g, ln_ffn_b, loss_target, m_fox_w_in, m_fox_b_f, m_fox_w_o, m_swa_w_in, m_swa_sinks, m_swa_w_o, m_ada_w, m_ada_b, m_ffn_w_up, m_ffn_conv_w, m_ffn_conv_b, m_ffn_w_down, m_ln_mix_g, m_ln_mix_b, m_ln_ffn_g, m_ln_ffn_b, v_fox_w_in, v_fox_b_f, v_fox_w_o, v_swa_w_in, v_swa_sinks, v_swa_w_o, v_ada_w, v_ada_b, v_ffn_w_up, v_ffn_conv_w, v_ffn_conv_b, v_ffn_w_down, v_ln_mix_g, v_ln_mix_b, v_ln_ffn_g, v_ln_ffn_b):
    given = dict(x=x, c=c, positions=positions, fox_w_in=fox_w_in, fox_b_f=fox_b_f, fox_w_o=fox_w_o, swa_w_in=swa_w_in, swa_sinks=swa_sinks, swa_w_o=swa_w_o, ada_w=ada_w, ada_b=ada_b, ffn_w_up=ffn_w_up, ffn_conv_w=ffn_conv_w, ffn_conv_b=ffn_conv_b, ffn_w_down=ffn_w_down, ln_mix_g=ln_mix_g, ln_mix_b=ln_mix_b, ln_ffn_g=ln_ffn_g, ln_ffn_b=ln_ffn_b, loss_target=loss_target, m_fox_w_in=m_fox_w_in, m_fox_b_f=m_fox_b_f, m_fox_w_o=m_fox_w_o, m_swa_w_in=m_swa_w_in, m_swa_sinks=m_swa_sinks, m_swa_w_o=m_swa_w_o, m_ada_w=m_ada_w, m_ada_b=m_ada_b, m_ffn_w_up=m_ffn_w_up, m_ffn_conv_w=m_ffn_conv_w, m_ffn_conv_b=m_ffn_conv_b, m_ffn_w_down=m_ffn_w_down, m_ln_mix_g=m_ln_mix_g, m_ln_mix_b=m_ln_mix_b, m_ln_ffn_g=m_ln_ffn_g, m_ln_ffn_b=m_ln_ffn_b, v_fox_w_in=v_fox_w_in, v_fox_b_f=v_fox_b_f, v_fox_w_o=v_fox_w_o, v_swa_w_in=v_swa_w_in, v_swa_sinks=v_swa_sinks, v_swa_w_o=v_swa_w_o, v_ada_w=v_ada_w, v_ada_b=v_ada_b, v_ffn_w_up=v_ffn_w_up, v_ffn_conv_w=v_ffn_conv_w, v_ffn_conv_b=v_ffn_conv_b, v_ffn_w_down=v_ffn_w_down, v_ln_mix_g=v_ln_mix_g, v_ln_mix_b=v_ln_mix_b, v_ln_ffn_g=v_ln_ffn_g, v_ln_ffn_b=v_ln_ffn_b)
    weights = {n: given[n] for n in TWIN_WEIGHTS}
    shared = {n: given[n] for n in SHARED_INPUTS}
    per_example = {n: given[n] for n in ['x', 'c', 'positions']}
    grad_fn = _jax.value_and_grad(_loss, argnums=(0, 1))

    def one_microbatch(ex, loss_target):
        ex = dict(ex)
        diff = ex.pop(TWIN_DIFF_INPUT)
        return grad_fn(weights, diff, {**shared, **ex}, loss_target)

    if N_MICROBATCH == 1:
        loss, (grad_w, grad_x) = one_microbatch(per_example, given["loss_target"])
    else:
        def body(carry, xs):
            loss_sum, grad_sum = carry
            l_k, (gw_k, gx_k) = one_microbatch(xs[0], xs[1])
            with _jax.named_scope("update"):
                return (loss_sum + l_k, _jax.tree.map(_jnp.add, grad_sum, gw_k)), gx_k

        init = (_jnp.zeros((), _jnp.float32), _jax.tree.map(_jnp.zeros_like, weights))
        (loss, grad_w), grad_x = _jax.lax.scan(body, init, (per_example, given["loss_target"]))
    with _jax.named_scope("update"):
        delta_w, new_m, new_v = {}, {}, {}
        for n in TWIN_WEIGHTS:
            delta_w[n], new_m[n], new_v[n] = _adamw(weights[n], grad_w[n], given["m_" + n], given["v_" + n])
    return (loss, grad_x, *[grad_w[n] for n in TWIN_WEIGHTS], *[delta_w[n] for n in TWIN_WEIGHTS],
            *[new_m[n] for n in TWIN_WEIGHTS], *[new_v[n] for n in TWIN_WEIGHTS])
```

```python
import functools
from typing import NamedTuple

import jax
import jax.numpy as jnp
from jax import lax
from jax.experimental import pallas as pl
from jax.experimental.pallas import tpu as pltpu

F32 = jnp.float32
BF16 = jnp.bfloat16
MESH = pl.DeviceIdType.MESH
HIGHEST = lax.Precision.HIGHEST

N_CHIPS = 4
N_DEV = 8
LANES = 128
VMEM_LIMIT = 56 * 1024 * 1024

DEPTH = 2
DEEPNORM_ALPHA = (2.0 * DEPTH) ** 0.25
LN_EPS = 1e-5
ROPE_THETA = 500000.0
ADAM_LR, ADAM_B1, ADAM_B2, ADAM_EPS, ADAM_WD, ADAM_STEP = 0.001, 0.9, 0.999, 1e-08, 0.01, 10
NEG = -1e30


class Dims(NamedTuple):
    S: int
    D: int
    FH: int
    QH: int
    KH: int
    F: int


PROD = Dims(S=2048, D=2048, FH=16, QH=32, KH=4, F=5632)
FDH = 128
SDH = 64
WIN = 128
ROPE_DIM = 16
FOX_TQ = 256


def _params(sem=None, vmem=VMEM_LIMIT):
    return pltpu.CompilerParams(dimension_semantics=sem, vmem_limit_bytes=vmem)


def _tile(n, pref, unit=LANES):
    if n <= pref:
        return n
    t = (pref // unit) * unit
    while t > 0:
        if n % t == 0:
            return t
        t -= unit
    return n


_DN = {"nn": (((1,), (0,)), ((), ())), "nt": (((1,), (1,)), ((), ())), "tn": (((0,), (0,)), ((), ()))}


def _mm(a, b, *, mode, out_dtype, name, out_groups=1, tm=1024, tn=1024, tk=2048):
    ga, ra, ca = a.shape
    gb, rb, cb = b.shape
    if mode == "nn":
        M, K, N = ra, ga * ca, gb * cb
        assert rb == K and ga == 1 or (rb == K)
    elif mode == "nt":
        M, K, N = ra, ga * ca, rb
        assert gb * cb == K
    else:
        K, M, N = ra, ga * ca, gb * cb
        assert rb == K
    go = out_groups
    if mode == "nn":
        tk = _tile(ca, tk); assert rb % tk == 0 and (ga == 1 or True)
        tn = _tile(min(cb, N // go), tn); tm = _tile(M, tm, 8)
    elif mode == "nt":
        tk = _tile(ca, tk); tk = _tile(cb, tk) if cb % tk else tk; assert ca % tk == 0 and cb % tk == 0
        tn = _tile(N // go, tn); tm = _tile(M, tm, 8)
    else:
        tk = _tile(K, tk, 8); tm = _tile(ca, tm); tn = _tile(min(cb, N // go), tn)
    assert (N // go) % tn == 0 and M % tm == 0 and K % tk == 0, (name, M, N, K, tm, tn, tk)
    nk = K // tk
    kpa = max(ca // tk, 1)
    kpb = max(cb // tk, 1)
    npb = max(cb // tn, 1)
    npo = (N // go) // tn
    mpa = max(ca // tm, 1)

    if mode == "nn":
        a_spec = pl.BlockSpec((1, tm, tk), lambda j, i, k: (k // kpa, i, k % kpa))
        b_spec = pl.BlockSpec((1, tk, tn), lambda j, i, k: (j // npb, k, j % npb))
    elif mode == "nt":
        a_spec = pl.BlockSpec((1, tm, tk), lambda j, i, k: (k // kpa, i, k % kpa))
        b_spec = pl.BlockSpec((1, tn, tk), lambda j, i, k: (k // kpb, j, k % kpb))
    else:
        a_spec = pl.BlockSpec((1, tk, tm), lambda j, i, k: (i // mpa, k, i % mpa))
        b_spec = pl.BlockSpec((1, tk, tn), lambda j, i, k: (j // npb, k, j % npb))
    o_spec = pl.BlockSpec((1, tm, tn), lambda j, i, k: (j // npo, i, j % npo))
    dn = _DN[mode]

    def body(a_ref, b_ref, o_ref, *acc):
        p = lax.dot_general(a_ref[0], b_ref[0], dn, preferred_element_type=F32)
        if nk == 1:
            o_ref[0] = p.astype(out_dtype)
        else:
            k = pl.program_id(2)

            @pl.when(k == 0)
            def _():
                acc[0][...] = p

            @pl.when(k > 0)
            def _():
                acc[0][...] += p

            @pl.when(k == nk - 1)
            def _():
                o_ref[0] = acc[0][...].astype(out_dtype)

    return pl.pallas_call(
        body, name=name, grid=(N // tn, M // tm, nk),
        in_specs=[a_spec, b_spec], out_specs=o_spec,
        out_shape=jax.ShapeDtypeStruct((go, M, N // go), out_dtype),
        scratch_shapes=[pltpu.VMEM((tm, tn), F32)] if nk > 1 else [],
        compiler_params=_params(("parallel", "parallel", "arbitrary")),
    )(a, b)


def _rows(tr, d):
    return pl.BlockSpec((tr, d), lambda i: (i, 0))


def _vec(d):
    return pl.BlockSpec((1, d), lambda i: (0, 0))


def _modulate(x, sc, sh, name):
    S, D = x.shape
    tr = min(256, S)

    def body(x_ref, sc_ref, sh_ref, h_ref):
        h_ref[...] = (x_ref[...] * (1.0 + sc_ref[...]) + sh_ref[...]).astype(BF16)

    return pl.pallas_call(
        body, name=name, grid=(S // tr,), in_specs=[_rows(tr, D), _vec(D), _vec(D)], out_specs=_rows(tr, D),
        out_shape=jax.ShapeDtypeStruct((S, D), BF16), compiler_params=_params(("parallel",)),
    )(x, sc, sh)


def _ln_fwd(x, y, gate, gamma, beta, sc, sh, name):
    S, D = x.shape
    tr = min(256, S)
    emit_h = sc is not None

    def body(*refs):
        if emit_h:
            x_ref, y_ref, g_ref, ga_ref, be_ref, sc_ref, sh_ref, xo_ref, xh_ref, rs_ref, h_ref = refs
        else:
            x_ref, y_ref, g_ref, ga_ref, be_ref, xo_ref, xh_ref, rs_ref = refs
        z = DEEPNORM_ALPHA * x_ref[...] + (1.0 + g_ref[...]) * y_ref[...]
        mu = jnp.mean(z, axis=-1, keepdims=True)
        zc = z - mu
        var = jnp.mean(zc * zc, axis=-1, keepdims=True)
        rstd = lax.rsqrt(var + LN_EPS)
        xh = zc * rstd
        xo = xh * ga_ref[...] + be_ref[...]
        xo_ref[...] = xo
        xh_ref[...] = xh
        rs_ref[...] = rstd
        if emit_h:
            h_ref[...] = (xo * (1.0 + sc_ref[...]) + sh_ref[...]).astype(BF16)

    ins = [x, y, gate, gamma, beta] + ([sc, sh] if emit_h else [])
    in_specs = [_rows(tr, D), _rows(tr, D)] + [_vec(D)] * (len(ins) - 2)
    out_shape = [jax.ShapeDtypeStruct((S, D), F32), jax.ShapeDtypeStruct((S, D), F32), jax.ShapeDtypeStruct((S, 1), F32)]
    out_specs = [_rows(tr, D), _rows(tr, D), _rows(tr, 1)]
    if emit_h:
        out_shape.append(jax.ShapeDtypeStruct((S, D), BF16))
        out_specs.append(_rows(tr, D))
    return pl.pallas_call(
        body, name=name, grid=(S // tr,), in_specs=in_specs, out_specs=out_specs, out_shape=out_shape,
        compiler_params=_params(("parallel",)),
    )(*ins)


def _loss_head(xf, tgt, name):
    S, D = xf.shape
    tr = min(256, S)

    def body(x_ref, t_ref, dx_ref, l_ref):
        e = x_ref[...] - t_ref[...]
        dx_ref[...] = e * (1.0 / D)

        @pl.when(pl.program_id(0) == 0)
        def _():
            l_ref[...] = jnp.zeros_like(l_ref)

        l_ref[...] += jnp.sum(e * e, axis=0, keepdims=True)

    return pl.pallas_call(
        body, name=name, grid=(S // tr,), in_specs=[_rows(tr, D), _rows(tr, D)],
        out_specs=[_rows(tr, D), _vec(D)],
        out_shape=[jax.ShapeDtypeStruct((S, D), F32), jax.ShapeDtypeStruct((1, D), F32)],
        compiler_params=_params(("arbitrary",)),
    )(xf, tgt)


def _ln_bwd(dxo, xh, rstd, gamma, y, gate, name):
    S, D = dxo.shape
    tr = min(256, S)

    def body(dx_ref, xh_ref, rs_ref, ga_ref, y_ref, g_ref, dres_ref, dy_ref, dga_ref, dbe_ref, dg_ref):
        dxo_ = dx_ref[...]
        xh_ = xh_ref[...]
        dxh = dxo_ * ga_ref[...]
        m1 = jnp.mean(dxh, axis=-1, keepdims=True)
        m2 = jnp.mean(dxh * xh_, axis=-1, keepdims=True)
        dz = rs_ref[...] * (dxh - m1 - xh_ * m2)
        dres_ref[...] = DEEPNORM_ALPHA * dz
        dy_ref[...] = ((1.0 + g_ref[...]) * dz).astype(BF16)

        @pl.when(pl.program_id(0) == 0)
        def _():
            dga_ref[...] = jnp.zeros_like(dga_ref)
            dbe_ref[...] = jnp.zeros_like(dbe_ref)
            dg_ref[...] = jnp.zeros_like(dg_ref)

        dga_ref[...] += jnp.sum(dxo_ * xh_, axis=0, keepdims=True)
        dbe_ref[...] += jnp.sum(dxo_, axis=0, keepdims=True)
        dg_ref[...] += jnp.sum(dz * y_ref[...], axis=0, keepdims=True)

    return pl.pallas_call(
        body, name=name, grid=(S // tr,),
        in_specs=[_rows(tr, D), _rows(tr, D), _rows(tr, 1), _vec(D), _rows(tr, D), _vec(D)],
        out_specs=[_rows(tr, D), _rows(tr, D), _vec(D), _vec(D), _vec(D)],
        out_shape=[jax.ShapeDtypeStruct((S, D), F32), jax.ShapeDtypeStruct((S, D), BF16)] + [jax.ShapeDtypeStruct((1, D), F32)] * 3,
        compiler_params=_params(("arbitrary",)),
    )(dxo, xh, rstd, gamma, y, gate)


def _mod_bwd(dh, x, sc, dres, name):
    S, D = x.shape
    tr = min(256, S)

    def body(dh_ref, x_ref, sc_ref, dr_ref, dx_ref, dsc_ref, dsh_ref):
        dh_ = dh_ref[...]
        dx_ref[...] = dr_ref[...] + dh_ * (1.0 + sc_ref[...])

        @pl.when(pl.program_id(0) == 0)
        def _():
            dsc_ref[...] = jnp.zeros_like(dsc_ref)
            dsh_ref[...] = jnp.zeros_like(dsh_ref)

        dsc_ref[...] += jnp.sum(dh_ * x_ref[...], axis=0, keepdims=True)
        dsh_ref[...] += jnp.sum(dh_, axis=0, keepdims=True)

    return pl.pallas_call(
        body, name=name, grid=(S // tr,),
        in_specs=[_rows(tr, D), _rows(tr, D), _vec(D), _rows(tr, D)],
        out_specs=[_rows(tr, D), _vec(D), _vec(D)],
        out_shape=[jax.ShapeDtypeStruct((S, D), F32), jax.ShapeDtypeStruct((1, D), F32), jax.ShapeDtypeStruct((1, D), F32)],
        compiler_params=_params(("arbitrary",)),
    )(dh, x, sc, dres)


def _log_sigmoid(z):
    return jnp.minimum(z, 0.0) - jnp.log(1.0 + jnp.exp(-jnp.abs(z)))


def _fox_gate_fwd(proj, b_f, n_heads, name):
    S, PW = proj.shape
    blk = min(256, S)
    last = PW // LANES - 1

    def body(fl_ref, b_ref, cum_ref):
        r = lax.broadcasted_iota(jnp.int32, (blk, blk), 0)
        c = lax.broadcasted_iota(jnp.int32, (blk, blk), 1)
        tril = (c <= r).astype(F32)
        carry = jnp.zeros((1, LANES), F32)
        for i in range(S // blk):
            lf = _log_sigmoid(fl_ref[i * blk:(i + 1) * blk, :] + b_ref[...])
            cum_ref[i * blk:(i + 1) * blk, :] = jnp.dot(tril, lf, preferred_element_type=F32, precision=HIGHEST) + carry
            carry = carry + jnp.sum(lf, axis=0, keepdims=True)

    return pl.pallas_call(
        body, name=name, grid=(1,),
        in_specs=[pl.BlockSpec((S, LANES), lambda i: (0, last)), pl.BlockSpec((1, LANES), lambda i: (0, 0))],
        out_specs=pl.BlockSpec((S, LANES), lambda i: (0, 0)),
        out_shape=jax.ShapeDtypeStruct((S, LANES), F32), compiler_params=_params(("arbitrary",)),
    )(proj, b_f)


def _fox_gate_bwd(dcum, proj, b_f, n_heads, name):
    S, PW = proj.shape
    blk = min(256, S)
    last = PW // LANES - 1
    nb = S // blk

    def body(dc_ref, fl_ref, b_ref, dfl_ref, db_ref):
        r = lax.broadcasted_iota(jnp.int32, (blk, blk), 0)
        c = lax.broadcasted_iota(jnp.int32, (blk, blk), 1)
        triu = (c >= r).astype(F32)
        lane = lax.broadcasted_iota(jnp.int32, (blk, LANES), 1)
        carry = jnp.zeros((1, LANES), F32)
        dbs = jnp.zeros((1, LANES), F32)
        for i in reversed(range(nb)):
            dc = dc_ref[i * blk:(i + 1) * blk, :]
            dlf = jnp.dot(triu, dc, preferred_element_type=F32, precision=HIGHEST) + carry
            carry = carry + jnp.sum(dc, axis=0, keepdims=True)
            z = fl_ref[i * blk:(i + 1) * blk, :] + b_ref[...]
            e = jnp.exp(-jnp.abs(z))
            sig_neg = jnp.where(z >= 0, e / (1.0 + e), 1.0 / (1.0 + e))
            dfl = jnp.where(lane < n_heads, dlf * sig_neg, 0.0)
            dfl_ref[i * blk:(i + 1) * blk, :] = dfl.astype(BF16)
            dbs = dbs + jnp.sum(dfl, axis=0, keepdims=True)
        db_ref[...] = dbs

    return pl.pallas_call(
        body, name=name, grid=(1,),
        in_specs=[pl.BlockSpec((S, LANES), lambda i: (0, 0)), pl.BlockSpec((S, LANES), lambda i: (0, last)),
                  pl.BlockSpec((1, LANES), lambda i: (0, 0))],
        out_specs=[pl.BlockSpec((S, LANES), lambda i: (0, 0)), pl.BlockSpec((1, LANES), lambda i: (0, 0))],
        out_shape=[jax.ShapeDtypeStruct((S, LANES), BF16), jax.ShapeDtypeStruct((1, LANES), F32)],
        compiler_params=_params(("arbitrary",)),
    )(dcum, proj, b_f)


def _fox_scores(q_ref, kb_ref, cq_ref, ck_ref, qi, tq, scale):
    kk = (qi + 1) * tq
    rows = slice(qi * tq, (qi + 1) * tq)
    qb = q_ref[rows, :].astype(BF16)
    s = lax.dot_general(qb, kb_ref[0:kk, :], _DN["nt"], preferred_element_type=F32) * scale
    s = s + (cq_ref[0, rows, :] - ck_ref[0, :, 0:kk])
    r = lax.broadcasted_iota(jnp.int32, (tq, kk), 0) + qi * tq
    c = lax.broadcasted_iota(jnp.int32, (tq, kk), 1)
    mask = c <= r
    return jnp.where(mask, s, NEG), mask, qb


def _fox_fwd(proj, cq, ck, n_heads, name):
    S = proj.shape[0]
    H = n_heads
    tq = min(FOX_TQ, S)
    nq = S // tq
    scale = FDH ** -0.5

    def body(q_ref, k_ref, v_ref, cq_ref, ck_ref, o_ref, lse_ref, kb_ref, vb_ref):
        kb_ref[...] = k_ref[...].astype(BF16)
        vb_ref[...] = v_ref[...].astype(BF16)
        for qi in range(nq):
            kk = (qi + 1) * tq
            rows = slice(qi * tq, (qi + 1) * tq)
            s, _, _ = _fox_scores(q_ref, kb_ref, cq_ref, ck_ref, qi, tq, scale)
            m = jnp.max(s, axis=-1, keepdims=True)
            p = jnp.exp(s - m)
            l = jnp.sum(p, axis=-1, keepdims=True)
            p = p * (1.0 / l)
            o_ref[rows, :] = jnp.dot(p.astype(BF16), vb_ref[0:kk, :], preferred_element_type=F32).astype(BF16)
            lse_ref[0, rows, :] = m + jnp.log(l)

    col = lambda off: pl.BlockSpec((S, FDH), lambda h: (0, h + off))
    stat_c = pl.BlockSpec((1, S, 1), lambda h: (h, 0, 0))
    stat_r = pl.BlockSpec((1, 1, S), lambda h: (h, 0, 0))
    return pl.pallas_call(
        body, name=name, grid=(H,),
        in_specs=[col(0), col(H), col(2 * H), stat_c, stat_r],
        out_specs=[col(0), stat_c],
        out_shape=[jax.ShapeDtypeStruct((S, H * FDH), BF16), jax.ShapeDtypeStruct((H, S, 1), F32)],
        scratch_shapes=[pltpu.VMEM((S, FDH), BF16), pltpu.VMEM((S, FDH), BF16)],
        compiler_params=_params(("parallel",)),
    )(proj, proj, proj, cq, ck)


def _fox_bwd(proj, cq, ck, lse, do, n_heads, name):
    S = proj.shape[0]
    H = n_heads
    tq = min(FOX_TQ, S)
    nq = S // tq
    scale = FDH ** -0.5

    def body(q_ref, k_ref, v_ref, cq_ref, ck_ref, lse_ref, do_ref, dq_ref, dk_ref, dv_ref, dcq_ref, dck_ref,
             kb_ref, vb_ref, dka_ref, dva_ref):
        kb_ref[...] = k_ref[...].astype(BF16)
        vb_ref[...] = v_ref[...].astype(BF16)
        dka_ref[...] = jnp.zeros_like(dka_ref)
        dva_ref[...] = jnp.zeros_like(dva_ref)
        dck_ref[...] = jnp.zeros_like(dck_ref)
        for qi in range(nq):
            kk = (qi + 1) * tq
            rows = slice(qi * tq, (qi + 1) * tq)
            s, mask, qb = _fox_scores(q_ref, kb_ref, cq_ref, ck_ref, qi, tq, scale)
            p = jnp.where(mask, jnp.exp(s - lse_ref[0, rows, :]), 0.0)
            dob = do_ref[rows, :]
            dp = lax.dot_general(dob, vb_ref[0:kk, :], _DN["nt"], preferred_element_type=F32)
            delta = jnp.sum(p * dp, axis=-1, keepdims=True)
            ds = p * (dp - delta)
            dcq_ref[0, rows, :] = jnp.sum(ds, axis=-1, keepdims=True)
            dck_ref[0, :, 0:kk] -= jnp.sum(ds, axis=0, keepdims=True)
            dsb = (ds * scale).astype(BF16)
            dq_ref[rows, :] = jnp.dot(dsb, kb_ref[0:kk, :], preferred_element_type=F32).astype(BF16)
            dka_ref[0:kk, :] += lax.dot_general(dsb, qb, _DN["tn"], preferred_element_type=F32)
            dva_ref[0:kk, :] += lax.dot_general(p.astype(BF16), dob, _DN["tn"], preferred_element_type=F32)
        dk_ref[...] = dka_ref[...].astype(BF16)
        dv_ref[...] = dva_ref[...].astype(BF16)

    col = lambda off: pl.BlockSpec((S, FDH), lambda h: (0, h + off))
    stat_c = pl.BlockSpec((1, S, 1), lambda h: (h, 0, 0))
    stat_r = pl.BlockSpec((1, 1, S), lambda h: (h, 0, 0))
    wide = jax.ShapeDtypeStruct((S, H * FDH), BF16)
    return pl.pallas_call(
        body, name=name, grid=(H,),
        in_specs=[col(0), col(H), col(2 * H), stat_c, stat_r, stat_c, col(0)],
        out_specs=[col(0), col(0), col(0), stat_c, stat_r],
        out_shape=[wide, wide, wide, jax.ShapeDtypeStruct((H, S, 1), F32), jax.ShapeDtypeStruct((H, 1, S), F32)],
        scratch_shapes=[pltpu.VMEM((S, FDH), BF16), pltpu.VMEM((S, FDH), BF16), pltpu.VMEM((S, FDH), F32), pltpu.VMEM((S, FDH), F32)],
        compiler_params=_params(("parallel",)),
    )(proj, proj, proj, cq, ck, lse, do)


def _rope_tables(pos, sign):
    inv = ROPE_THETA ** (-jnp.arange(0, ROPE_DIM, 2, dtype=F32) / ROPE_DIM)
    ang = pos.astype(F32)[:, None] * inv
    cos, sin = jnp.cos(ang), sign * jnp.sin(ang)
    l64 = jnp.arange(LANES) % SDH
    idx = l64 % (ROPE_DIM // 2)
    c = jnp.where(l64 < ROPE_DIM, cos[:, idx], 1.0)
    sa = jnp.where(l64 < ROPE_DIM // 2, -sin[:, idx], 0.0)
    sb = jnp.where((l64 >= ROPE_DIM // 2) & (l64 < ROPE_DIM), sin[:, idx], 0.0)
    rot = jnp.stack([c, sa, sb])
    ident = jnp.stack([jnp.ones_like(c), jnp.zeros_like(c), jnp.zeros_like(c)])
    return jnp.stack([rot, ident]).astype(F32)


def _rope(xin, tabs, n_rot, out_dtype, name):
    S, W = xin.shape

    def body(x_ref, t_ref, o_ref):
        xv = x_ref[...]
        o = xv * t_ref[0, 0] + pltpu.roll(xv, LANES - ROPE_DIM // 2, 1) * t_ref[0, 1] + pltpu.roll(xv, ROPE_DIM // 2, 1) * t_ref[0, 2]
        o_ref[...] = o.astype(out_dtype)

    return pl.pallas_call(
        body, name=name, grid=(W // LANES,),
        in_specs=[pl.BlockSpec((S, LANES), lambda j: (0, j)),
                  pl.BlockSpec((1, 3, S, LANES), lambda j: (jnp.where(j < n_rot, 0, 1), 0, 0, 0))],
        out_specs=pl.BlockSpec((S, LANES), lambda j: (0, j)),
        out_shape=jax.ShapeDtypeStruct((S, W), out_dtype), compiler_params=_params(("parallel",)),
    )(xin, tabs)


def _swa_probs(q_ref, k_ref, sink, n, scale):
    st = pl.multiple_of(jnp.maximum(n - 1, 0) * WIN, WIN)
    q0 = pl.multiple_of(n * WIN, WIN)
    qb = q_ref[0, pl.ds(q0, WIN), :]
    kb = k_ref[0, pl.ds(st, 2 * WIN), :]
    s = lax.dot_general(qb, kb, _DN["nt"], preferred_element_type=F32) * scale
    qa = q0 + lax.broadcasted_iota(jnp.int32, (WIN, 2 * WIN), 0)
    ka = st + lax.broadcasted_iota(jnp.int32, (WIN, 2 * WIN), 1)
    valid = (ka <= qa) & (qa - ka < WIN)
    s = jnp.where(valid, s, NEG)
    m = jnp.maximum(jnp.max(s, axis=-1, keepdims=True), sink)
    e = jnp.where(valid, jnp.exp(s - m), 0.0)
    es = jnp.exp(sink - m)
    inv = 1.0 / (jnp.sum(e, axis=-1, keepdims=True) + es)
    return e * inv, es * inv, st, q0, qb, kb


def _swa_fwd(q, k, v, sinks, name):
    QH, S, _ = q.shape
    grp = QH // k.shape[0]
    scale = SDH ** -0.5

    def body(q_ref, k_ref, v_ref, s_ref, o_ref):
        sink = s_ref[0]

        def step(n, carry):
            p, _, st, q0, _, _ = _swa_probs(q_ref, k_ref, sink, n, scale)
            vb = v_ref[0, pl.ds(st, 2 * WIN), :]
            o_ref[0, pl.ds(q0, WIN), :] = jnp.dot(p.astype(BF16), vb, preferred_element_type=F32).astype(BF16)
            return carry

        lax.fori_loop(0, S // WIN, step, 0)

    hq = pl.BlockSpec((1, S, SDH), lambda h: (h, 0, 0))
    hk = pl.BlockSpec((1, S, SDH), lambda h: (h // grp, 0, 0))
    return pl.pallas_call(
        body, name=name, grid=(QH,),
        in_specs=[hq, hk, hk, pl.BlockSpec((1, 1, 1), lambda h: (h, 0, 0))], out_specs=hq,
        out_shape=jax.ShapeDtypeStruct((QH, S, SDH), BF16), compiler_params=_params(("parallel",)),
    )(q, k, v, sinks)


def _swa_bwd(q, k, v, sinks, do, name):
    QH, S, _ = q.shape
    KH = k.shape[0]
    grp = QH // KH
    scale = SDH ** -0.5

    def body(q_ref, k_ref, v_ref, s_ref, do_ref, dq_ref, dk_ref, dv_ref, dsk_ref):
        sink = s_ref[0]

        @pl.when(pl.program_id(0) % grp == 0)
        def _():
            dk_ref[...] = jnp.zeros_like(dk_ref)
            dv_ref[...] = jnp.zeros_like(dv_ref)

        def step(n, acc):
            p, ps, st, q0, qb, kb = _swa_probs(q_ref, k_ref, sink, n, scale)
            vb = v_ref[0, pl.ds(st, 2 * WIN), :]
            dob = do_ref[0, pl.ds(q0, WIN), :]
            dp = lax.dot_general(dob, vb, _DN["nt"], preferred_element_type=F32)
            delta = jnp.sum(p * dp, axis=-1, keepdims=True)
            ds = p * (dp - delta)
            dsb = (ds * scale).astype(BF16)
            dq_ref[0, pl.ds(q0, WIN), :] = jnp.dot(dsb, kb, preferred_element_type=F32)
            dk_ref[0, pl.ds(st, 2 * WIN), :] += lax.dot_general(dsb, qb, _DN["tn"], preferred_element_type=F32)
            dv_ref[0, pl.ds(st, 2 * WIN), :] += lax.dot_general(p.astype(BF16), dob, _DN["tn"], preferred_element_type=F32)
            return acc - jnp.sum(ps * delta, axis=0, keepdims=True)

        dsk_ref[0] = lax.fori_loop(0, S // WIN, step, jnp.zeros((1, 1), F32))

    hq = pl.BlockSpec((1, S, SDH), lambda h: (h, 0, 0))
    hk = pl.BlockSpec((1, S, SDH), lambda h: (h // grp, 0, 0))
    one = pl.BlockSpec((1, 1, 1), lambda h: (h, 0, 0))
    return pl.pallas_call(
        body, name=name, grid=(QH,),
        in_specs=[hq, hk, hk, one, hq], out_specs=[hq, hk, hk, one],
        out_shape=[jax.ShapeDtypeStruct((QH, S, SDH), F32), jax.ShapeDtypeStruct((KH, S, SDH), F32),
                   jax.ShapeDtypeStruct((KH, S, SDH), F32), jax.ShapeDtypeStruct((QH, 1, 1), F32)],
        compiler_params=_params(("arbitrary",)),
    )(q, k, v, sinks, do)


def _shift_down(u, k):
    row = lax.broadcasted_iota(jnp.int32, u.shape, 0)
    return jnp.where(row >= k, pltpu.roll(u, k, 0), 0.0)


def _shift_up(u, k):
    n = u.shape[0]
    row = lax.broadcasted_iota(jnp.int32, u.shape, 0)
    return jnp.where(row < n - k, pltpu.roll(u, n - k, 0), 0.0)


def _conv3(u, w_ref, b_ref):
    return w_ref[0:1, :] * _shift_down(u, 2) + w_ref[1:2, :] * _shift_down(u, 1) + w_ref[2:3, :] * u + b_ref[...]


def _conv_gate(u, cw, cb, name):
    S, F2 = u.shape
    Fh = F2 // 2
    tc = _tile(Fh, 256)
    nf = Fh // tc

    def body(ug_ref, uv_ref, wg_ref, wv_ref, bg_ref, bv_ref, a_ref):
        g = _conv3(ug_ref[...], wg_ref, bg_ref)
        val = _conv3(uv_ref[...], wv_ref, bv_ref)
        a_ref[...] = (g * (1.0 / (1.0 + jnp.exp(-g))) * val).astype(BF16)

    blk = lambda r, off: pl.BlockSpec((r, tc), lambda j: (0, j + off))
    return pl.pallas_call(
        body, name=name, grid=(nf,),
        in_specs=[blk(S, 0), blk(S, nf), blk(3, 0), blk(3, nf), blk(1, 0), blk(1, nf)], out_specs=blk(S, 0),
        out_shape=jax.ShapeDtypeStruct((S, Fh), BF16), compiler_params=_params(("parallel",)),
    )(u, u, cw, cw, cb, cb)


def _conv_gate_bwd(u, da, cw, cb, name):
    S, F2 = u.shape
    Fh = F2 // 2
    tc = _tile(Fh, 256)
    nf = Fh // tc

    def half(dx, uu, w_ref, du_ref, dw_ref, db_ref):
        du = w_ref[2:3, :] * dx + w_ref[1:2, :] * _shift_up(dx, 1) + w_ref[0:1, :] * _shift_up(dx, 2)
        du_ref[...] = du.astype(BF16)
        dw_ref[0:1, :] = jnp.sum(dx * _shift_down(uu, 2), axis=0, keepdims=True)
        dw_ref[1:2, :] = jnp.sum(dx * _shift_down(uu, 1), axis=0, keepdims=True)
        dw_ref[2:3, :] = jnp.sum(dx * uu, axis=0, keepdims=True)
        db_ref[...] = jnp.sum(dx, axis=0, keepdims=True)

    def body(ug_ref, uv_ref, da_ref, wg_ref, wv_ref, bg_ref, bv_ref, dug_ref, duv_ref, dwg_ref, dwv_ref, dbg_ref, dbv_ref):
        ug = ug_ref[...]
        uv = uv_ref[...]
        g = _conv3(ug, wg_ref, bg_ref)
        val = _conv3(uv, wv_ref, bv_ref)
        sig = 1.0 / (1.0 + jnp.exp(-g))
        da_ = da_ref[...]
        dg = da_ * val * (sig * (1.0 + g * (1.0 - sig)))
        dval = da_ * (g * sig)
        half(dg, ug, wg_ref, dug_ref, dwg_ref, dbg_ref)
        half(dval, uv, wv_ref, duv_ref, dwv_ref, dbv_ref)

    blk = lambda r, off: pl.BlockSpec((r, tc), lambda j: (0, j + off))
    return pl.pallas_call(
        body, name=name, grid=(nf,),
        in_specs=[blk(S, 0), blk(S, nf), blk(S, 0), blk(3, 0), blk(3, nf), blk(1, 0), blk(1, nf)],
        out_specs=[blk(S, 0), blk(S, 0), blk(3, 0), blk(3, 0), blk(1, 0), blk(1, 0)],
        out_shape=[jax.ShapeDtypeStruct((S, Fh), BF16), jax.ShapeDtypeStruct((S, Fh), BF16),
                   jax.ShapeDtypeStruct((3, Fh), F32), jax.ShapeDtypeStruct((3, Fh), F32),
                   jax.ShapeDtypeStruct((1, Fh), F32), jax.ShapeDtypeStruct((1, Fh), F32)],
        compiler_params=_params(("parallel",)),
    )(u, u, da, cw, cw, cb, cb)


def _local_step(dm, x, tgt, pos, mod, W, sp):
    S, D, FH, QH, KH, Fh = dm
    m = [[mod[i:i + 1, j * D:(j + 1) * D] for j in range(6)] for i in range(DEPTH)]
    sv = []
    xs = x
    h = _modulate(xs, m[0][1], m[0][0], "mod_in")
    save = {}
    for i in range(DEPTH):
        sh1, sc1, g1, sh2, sc2, g2 = m[i]
        L = {}
        L["x_in"], L["h1"] = xs, h
        if i == 0:
            proj = _mm(h[None], W["fox_in"], mode="nn", out_dtype=F32, name="fox_proj", tn=896)[0]
            cum = _fox_gate_fwd(proj, sp["fox_b_f"], FH, "fox_gate")
            cq = cum[:, :FH].T[:, :, None]
            ck = cum[:, :FH].T[:, None, :]
            o, lse = _fox_fwd(proj, cq, ck, FH, "fox_attn")
            L.update(proj=proj, cq=cq, ck=ck, lse=lse, o=o)
            y = _mm(o[None], W["fox_o"], mode="nn", out_dtype=F32, name="fox_out")[0]
        else:
            proj = _mm(h[None], W["swa_in"], mode="nn", out_dtype=F32, name="swa_proj", tn=640)[0]
            tabs = _rope_tables(pos, 1.0)
            n_rot = (QH + KH) * SDH // LANES
            pr = _rope(proj, tabs, n_rot, BF16, "swa_rope")
            qh = pr[:, :QH * SDH].reshape(S, QH, SDH).transpose(1, 0, 2)
            kh = pr[:, QH * SDH:(QH + KH) * SDH].reshape(S, KH, SDH).transpose(1, 0, 2)
            vh = pr[:, (QH + KH) * SDH:].reshape(S, KH, SDH).transpose(1, 0, 2)
            oh = _swa_fwd(qh, kh, vh, sp["sinks"], "swa_attn")
            o = oh.transpose(1, 0, 2).reshape(S, QH * SDH)
            L.update(qh=qh, kh=kh, vh=vh, o=o)
            y = _mm(o[None], W["swa_o"], mode="nn", out_dtype=F32, name="swa_out")[0]
        L["y1"] = y
        x1, L["xh1"], L["rs1"], h2 = _ln_fwd(xs, y, g1, sp["ln_mix_g"][i], sp["ln_mix_b"][i], sc2, sh2, f"ln_mix{i}")
        L["x1"], L["h2"] = x1, h2
        u = _mm(h2[None], W["up"][i], mode="nn", out_dtype=F32, name=f"ffn_up{i}", tm=512, tn=1408)[0]
        a = _conv_gate(u, sp["conv_w"][i], sp["conv_b"][i], f"ffn_gate{i}")
        y2 = _mm(a[None], W["down"][i], mode="nn", out_dtype=F32, name=f"ffn_down{i}", tk=1408)[0]
        L.update(u=u, a=a, y2=y2)
        if i + 1 < DEPTH:
            xs, L["xh2"], L["rs2"], h = _ln_fwd(x1, y2, g2, sp["ln_ffn_g"][i], sp["ln_ffn_b"][i], m[i + 1][1], m[i + 1][0], f"ln_ffn{i}")
        else:
            xs, L["xh2"], L["rs2"] = _ln_fwd(x1, y2, g2, sp["ln_ffn_g"][i], sp["ln_ffn_b"][i], None, None, f"ln_ffn{i}")
        sv.append(L)

    dx, loss_cols = _loss_head(xs, tgt, "loss_head")

    gw = {"up": [None] * DEPTH, "down": [None] * DEPTH}
    gs = {k: [None] * DEPTH for k in ("conv_w", "conv_b", "ln_mix_g", "ln_mix_b", "ln_ffn_g", "ln_ffn_b")}
    dmod = [None] * DEPTH
    for i in reversed(range(DEPTH)):
        sh1, sc1, g1, sh2, sc2, g2 = m[i]
        L = sv[i]
        dres, dy, gs["ln_ffn_g"][i], gs["ln_ffn_b"][i], dg2 = _ln_bwd(dx, L["xh2"], L["rs2"], sp["ln_ffn_g"][i], L["y2"], g2, f"ln_ffn_bwd{i}")
        da = _mm(dy[None], W["down"][i], mode="nt", out_dtype=F32, name=f"ffn_down_dx{i}", tm=512, tn=1408)[0]
        gw["down"][i] = _mm(L["a"][None], dy[None], mode="tn", out_dtype=BF16, name=f"ffn_down_dw{i}", tm=1408)
        dug, duv, dwg, dwv, dbg, dbv = _conv_gate_bwd(L["u"], da, sp["conv_w"][i], sp["conv_b"][i], f"ffn_gate_bwd{i}")
        du = jnp.concatenate([dug, duv], axis=1)
        gs["conv_w"][i] = jnp.concatenate([dwg, dwv], axis=1)
        gs["conv_b"][i] = jnp.concatenate([dbg, dbv], axis=1)
        dh2 = _mm(du[None], W["up"][i], mode="nt", out_dtype=F32, name=f"ffn_up_dx{i}", tk=1408)[0]
        gw["up"][i] = _mm(L["h2"][None], du[None], mode="tn", out_dtype=BF16, name=f"ffn_up_dw{i}", out_groups=N_CHIPS, tn=1408)
        dx, dsc2, dsh2 = _mod_bwd(dh2, L["x1"], sc2, dres, f"mod_ffn_bwd{i}")
        dres, dy, gs["ln_mix_g"][i], gs["ln_mix_b"][i], dg1 = _ln_bwd(dx, L["xh1"], L["rs1"], sp["ln_mix_g"][i], L["y1"], g1, f"ln_mix_bwd{i}")
        if i == 0:
            do = _mm(dy[None], W["fox_o"], mode="nt", out_dtype=BF16, name="fox_out_dx")[0]
            gw["fox_o"] = _mm(L["o"][None], dy[None], mode="tn", out_dtype=BF16, name="fox_out_dw")
            dq, dk, dv, dcq, dck = _fox_bwd(L["proj"], L["cq"], L["ck"], L["lse"], do, FH, "fox_attn_bwd")
            dcum = dcq[:, :, 0].T + dck[:, 0, :].T
            dcum = jnp.pad(dcum, ((0, 0), (0, LANES - FH)))
            dfl, db_f = _fox_gate_bwd(dcum, L["proj"], sp["fox_b_f"], FH, "fox_gate_bwd")
            gs["fox_b_f"] = db_f
            dproj = jnp.concatenate([dq, dk, dv, dfl], axis=1)
            dh1 = _mm(dproj[None], W["fox_in"], mode="nt", out_dtype=F32, name="fox_proj_dx", tk=896)[0]
            gw["fox_in"] = _mm(L["h1"][None], dproj[None], mode="tn", out_dtype=BF16, name="fox_proj_dw", tn=896)
        else:
            do = _mm(dy[None], W["swa_o"], mode="nt", out_dtype=BF16, name="swa_out_dx")[0]
            gw["swa_o"] = _mm(L["o"][None], dy[None], mode="tn", out_dtype=BF16, name="swa_out_dw")
            doh = do.reshape(S, QH, SDH).transpose(1, 0, 2)
            dqh, dkh, dvh, dsk = _swa_bwd(L["qh"], L["kh"], L["vh"], sp["sinks"], doh, "swa_attn_bwd")
            gs["sinks"] = dsk
            dpr = jnp.concatenate([dqh.transpose(1, 0, 2).reshape(S, QH * SDH), dkh.transpose(1, 0, 2).reshape(S, KH * SDH),
                                   dvh.transpose(1, 0, 2).reshape(S, KH * SDH)], axis=1)
            n_rot = (QH + KH) * SDH // LANES
            dproj = _rope(dpr, _rope_tables(pos, -1.0), n_rot, BF16, "swa_rope_bwd")
            dh1 = _mm(dproj[None], W["swa_in"], mode="nt", out_dtype=F32, name="swa_proj_dx", tk=640)[0]
            gw["swa_in"] = _mm(L["h1"][None], dproj[None], mode="tn", out_dtype=BF16, name="swa_proj_dw", out_groups=N_CHIPS, tn=640)
        dx, dsc1, dsh1 = _mod_bwd(dh1, L["x_in"], sc1, dres, f"mod_mix_bwd{i}")
        dmod[i] = jnp.concatenate([dsh1, dsc1, dg1, dsh2, dsc2, dg2], axis=1)
    return loss_cols, dx, gw, gs, jnp.concatenate(dmod, axis=0)


def _place():
    x, y, c = lax.axis_index("x"), lax.axis_index("y"), lax.axis_index("c")
    chips = [(1 - x, y), (x, 1 - y), (1 - x, 1 - y)]
    return x, y, c, chips


def _remote(src, dst, send, recv, to):
    return pltpu.make_async_remote_copy(src_ref=src, dst_ref=dst, send_sem=send, recv_sem=recv, device_id=to, device_id_type=MESH)


def _allgather_small(v, name):
    m_per, n = v.shape

    def body(x_ref, out_ref, send_sems, recv_sems, local_sem):
        x, y, c, chips = _place()
        me, sibling = (x, y, c), (x, y, 1 - c)

        def rows(px, py, pc):
            return out_ref.at[pl.ds((4 * px + 2 * py + pc) * m_per, m_per), :]

        def copy(k, block, to, src=None):
            return _remote(rows(*block) if src is None else src, rows(*block), send_sems.at[k], recv_sems.at[k], to)

        mine = pltpu.make_async_copy(x_ref, rows(*me), local_sem)
        mine.start()
        first = [copy(0, me, sibling, src=x_ref)]
        first += [copy(1 + j, me, (*chip, c), src=x_ref) for j, chip in enumerate(chips)]
        for cp in first:
            cp.start()
        passed = [copy(4 + j, (*chip, c), sibling) for j, chip in enumerate(chips)]
        for j, chip in enumerate(chips):
            copy(1 + j, (*chip, c), me).wait_recv()
            passed[j].start()
        copy(0, sibling, me).wait_recv()
        for j, chip in enumerate(chips):
            copy(4 + j, (*chip, 1 - c), me).wait_recv()
        for cp in first + passed:
            cp.wait_send()
        mine.wait()

    return pl.pallas_call(
        body, name=name, out_shape=jax.ShapeDtypeStruct((N_DEV * m_per, n), v.dtype),
        in_specs=[pl.BlockSpec(memory_space=pltpu.VMEM)], out_specs=pl.BlockSpec(memory_space=pltpu.VMEM),
        scratch_shapes=[pltpu.SemaphoreType.DMA((7,)), pltpu.SemaphoreType.DMA((7,)), pltpu.SemaphoreType.DMA],
        compiler_params=pltpu.CompilerParams(vmem_limit_bytes=VMEM_LIMIT),
    )(v)


def _any_specs(n):
    return [pl.BlockSpec(memory_space=pl.ANY)] * n


def _gather_weights(shards, name):
    n = len(shards)

    def body(*refs):
        srcs, outs = refs[:n], refs[n:2 * n]
        send, recv, loc = refs[2 * n:]
        x, y, c, chips = _place()
        s = 2 * x + y
        sib = (x, y, 1 - c)
        started, locs = [], []
        for t in range(n):
            rh = shards[t].shape[0] // 2
            half = pl.ds(c * rh, rh)
            cp = pltpu.make_async_copy(srcs[t], outs[t].at[s], loc.at[t])
            cp.start()
            locs.append(cp)
            for j, chip in enumerate(chips):
                cp = _remote(srcs[t].at[half], outs[t].at[s, half], send.at[6 * t + j], recv.at[6 * t + j], (*chip, c))
                cp.start()
                started.append(cp)
        for t in range(n):
            rh = shards[t].shape[0] // 2
            half = pl.ds(c * rh, rh)
            for j, chip in enumerate(chips):
                sj = 2 * chip[0] + chip[1]
                blk = outs[t].at[sj, half]
                _remote(blk, blk, send.at[6 * t + j], recv.at[6 * t + j], (*chip, c)).wait_recv()
                cp = _remote(blk, blk, send.at[6 * t + 3 + j], recv.at[6 * t + 3 + j], sib)
                cp.start()
                started.append(cp)
        for t in range(n):
            rh = shards[t].shape[0] // 2
            other = pl.ds((1 - c) * rh, rh)
            for j, chip in enumerate(chips):
                sj = 2 * chip[0] + chip[1]
                blk = outs[t].at[sj, other]
                _remote(blk, blk, send.at[6 * t + 3 + j], recv.at[6 * t + 3 + j], sib).wait_recv()
        for cp in started:
            cp.wait_send()
        for cp in locs:
            cp.wait()

    return pl.pallas_call(
        body, name=name, out_shape=[jax.ShapeDtypeStruct((N_CHIPS,) + w.shape, w.dtype) for w in shards],
        in_specs=_any_specs(n), out_specs=_any_specs(n),
        scratch_shapes=[pltpu.SemaphoreType.DMA((6 * n,)), pltpu.SemaphoreType.DMA((6 * n,)), pltpu.SemaphoreType.DMA((n,))],
    )(*shards)


def _swap_halves(gs, name):
    n = len(gs)

    def body(*refs):
        srcs, outs = refs[:n], refs[n:2 * n]
        send, recv = refs[2 * n:]
        x, y, c, _ = _place()
        cps = []
        for t in range(n):
            rh = gs[t].shape[1] // 2
            cp = _remote(srcs[t].at[:, pl.ds((1 - c) * rh, rh), :], outs[t], send.at[t], recv.at[t], (x, y, 1 - c))
            cp.start()
            cps.append(cp)
        for cp in cps:
            cp.wait()

    return pl.pallas_call(
        body, name=name,
        out_shape=[jax.ShapeDtypeStruct((g.shape[0], g.shape[1] // 2, g.shape[2]), g.dtype) for g in gs],
        in_specs=_any_specs(n), out_specs=_any_specs(n),
        scratch_shapes=[pltpu.SemaphoreType.DMA((n,)), pltpu.SemaphoreType.DMA((n,))],
    )(*gs)


def _scatter_to_owner(ps, name):
    n = len(ps)

    def body(*refs):
        srcs, outs = refs[:n], refs[n:2 * n]
        send, recv, loc = refs[2 * n:]
        x, y, c, chips = _place()
        s = 2 * x + y
        cps, locs = [], []
        for t in range(n):
            cp = pltpu.make_async_copy(srcs[t].at[s], outs[t].at[s], loc.at[t])
            cp.start()
            locs.append(cp)
            for j, chip in enumerate(chips):
                sj = 2 * chip[0] + chip[1]
                cp = _remote(srcs[t].at[sj], outs[t].at[s], send.at[3 * t + j], recv.at[3 * t + j], (*chip, c))
                cp.start()
                cps.append(cp)
        for t in range(n):
            for j, chip in enumerate(chips):
                sj = 2 * chip[0] + chip[1]
                _remote(outs[t].at[sj], outs[t].at[sj], send.at[3 * t + j], recv.at[3 * t + j], (*chip, c)).wait_recv()
        for cp in cps:
            cp.wait_send()
        for cp in locs:
            cp.wait()

    return pl.pallas_call(
        body, name=name, out_shape=[jax.ShapeDtypeStruct(p.shape, p.dtype) for p in ps],
        in_specs=_any_specs(n), out_specs=_any_specs(n),
        scratch_shapes=[pltpu.SemaphoreType.DMA((3 * n,)), pltpu.SemaphoreType.DMA((3 * n,)), pltpu.SemaphoreType.DMA((n,))],
    )(*ps)


def _join_halves(hs, name):
    n = len(hs)

    def body(*refs):
        srcs, outs = refs[:n], refs[n:2 * n]
        send, recv, loc = refs[2 * n:]
        x, y, c, _ = _place()
        cps = []
        for t in range(n):
            rh = hs[t].shape[0]
            mine = outs[t].at[pl.ds(c * rh, rh), :]
            cp = pltpu.make_async_copy(srcs[t], mine, loc.at[t])
            cp.start()
            cps.append(cp)
            cp = _remote(srcs[t], mine, send.at[t], recv.at[t], (x, y, 1 - c))
            cp.start()
            cps.append(cp)
        for t in range(n):
            rh = hs[t].shape[0]
            theirs = outs[t].at[pl.ds((1 - c) * rh, rh), :]
            _remote(theirs, theirs, send.at[t], recv.at[t], (x, y, 1 - c)).wait_recv()
        for t in range(n):
            cps[2 * t + 1].wait_send()
            cps[2 * t].wait()

    return pl.pallas_call(
        body, name=name, out_shape=[jax.ShapeDtypeStruct((2 * h.shape[0], h.shape[1]), h.dtype) for h in hs],
        in_specs=_any_specs(n), out_specs=_any_specs(n),
        scratch_shapes=[pltpu.SemaphoreType.DMA((n,)), pltpu.SemaphoreType.DMA((n,)), pltpu.SemaphoreType.DMA((n,))],
    )(*hs)


def _row_tile(r, pref=256):
    return _tile(r, pref, 16)


def _cast_bf16(w, layer, name):
    _, R, C = w.shape
    tr = _row_tile(R)

    def body(w_ref, o_ref):
        o_ref[...] = w_ref[...].astype(BF16)

    return pl.pallas_call(
        body, name=name, grid=(R // tr,), in_specs=[pl.BlockSpec((None, tr, C), lambda i: (layer, i, 0))],
        out_specs=pl.BlockSpec((tr, C), lambda i: (i, 0)), out_shape=jax.ShapeDtypeStruct((R, C), BF16),
        compiler_params=_params(("parallel",)),
    )(w)


def _add_sibling(g, got, c, name):
    G, R, C = g.shape
    rh = R // 2
    tr = _row_tile(rh)
    nb = rh // tr

    def body(c_ref, g_ref, o_ref, p_ref):
        p_ref[...] = (g_ref[...].astype(F32) + o_ref[...].astype(F32)).astype(BF16)

    return pl.pallas_call(
        body, name=name,
        grid_spec=pltpu.PrefetchScalarGridSpec(
            num_scalar_prefetch=1, grid=(G, nb),
            in_specs=[pl.BlockSpec((1, tr, C), lambda s, i, c_ref: (s, c_ref[0] * nb + i, 0)),
                      pl.BlockSpec((1, tr, C), lambda s, i, c_ref: (s, i, 0))],
            out_specs=pl.BlockSpec((1, tr, C), lambda s, i, c_ref: (s, i, 0))),
        out_shape=jax.ShapeDtypeStruct((G, rh, C), BF16), compiler_params=_params(("parallel", "parallel")),
    )(jnp.reshape(c, (1,)).astype(jnp.int32), g, got)


def _sum_chips(q, name):
    G, rh, C = q.shape
    tr = _row_tile(rh)

    def body(q_ref, o_ref):
        acc = q_ref[0].astype(F32)
        for j in range(1, G):
            acc = acc + q_ref[j].astype(F32)
        o_ref[...] = acc

    return pl.pallas_call(
        body, name=name, grid=(rh // tr,), in_specs=[pl.BlockSpec((G, tr, C), lambda i: (0, i, 0))],
        out_specs=pl.BlockSpec((tr, C), lambda i: (i, 0)), out_shape=jax.ShapeDtypeStruct((rh, C), F32),
        compiler_params=_params(("parallel",)),
    )(q)


def _adam_math(w, g, m, v):
    m = ADAM_B1 * m + (1.0 - ADAM_B1) * g
    v = ADAM_B2 * v + (1.0 - ADAM_B2) * (g * g)
    m_hat = m / (1.0 - ADAM_B1 ** ADAM_STEP)
    v_hat = v / (1.0 - ADAM_B2 ** ADAM_STEP)
    delta = -ADAM_LR * (m_hat / (jnp.sqrt(v_hat) + ADAM_EPS) + ADAM_WD * w)
    return delta, m, v


def _adamw(w, g, m, v, layer, name):
    _, R, C = w.shape
    tr = _tile(R, 128, 8)

    def body(w_ref, g_ref, m_ref, v_ref, d_ref, mo_ref, vo_ref):
        d_ref[...], mo_ref[...], vo_ref[...] = _adam_math(w_ref[...], g_ref[...], m_ref[...], v_ref[...])

    lay = pl.BlockSpec((None, tr, C), lambda i: (layer, i, 0))
    flat = pl.BlockSpec((tr, C), lambda i: (i, 0))
    return pl.pallas_call(
        body, name=name, grid=(R // tr,), in_specs=[lay, flat, lay, lay], out_specs=[flat] * 3,
        out_shape=[jax.ShapeDtypeStruct((R, C), F32)] * 3, compiler_params=_params(("parallel",)),
    )(w, g, m, v)


def _silu_rows(c_row, extra, name):
    D = c_row.shape[1]
    E = extra.shape[1]

    def body(c_ref, e_ref, o_ref):
        o_ref[...] = jnp.zeros_like(o_ref)
        cv = c_ref[...]
        o_ref[0:1, 0:D] = cv * (1.0 / (1.0 + jnp.exp(-cv)))
        o_ref[0:1, D:D + E] = e_ref[...]

    return pl.pallas_call(body, name=name, out_shape=jax.ShapeDtypeStruct((8, D + E), F32))(c_row, extra)


def _ada_fwd(cact, ada_w, ada_b, layer, chip, name):
    _, D, NC = ada_w.shape
    tn = _tile(NC, 1024)
    nj = NC // tn

    def body(idx_ref, c_ref, w_ref, b_ref, o_ref):
        acc = jnp.dot(c_ref[...].astype(BF16), w_ref[0].astype(BF16), preferred_element_type=F32)
        o_ref[...] = acc + b_ref[pl.ds(idx_ref[0], 1), :]

    return pl.pallas_call(
        body, name=name,
        grid_spec=pltpu.PrefetchScalarGridSpec(
            num_scalar_prefetch=1, grid=(nj,),
            in_specs=[pl.BlockSpec((8, D), lambda j, idx: (0, 0)),
                      pl.BlockSpec((1, D, tn), lambda j, idx: (idx[0], 0, j)),
                      pl.BlockSpec((DEPTH, tn), lambda j, idx: (0, idx[1] * nj + j))],
            out_specs=pl.BlockSpec((8, tn), lambda j, idx: (0, j))),
        out_shape=jax.ShapeDtypeStruct((8, NC), F32), compiler_params=_params(("parallel",)),
    )(jnp.stack([layer, chip]).astype(jnp.int32), cact, ada_w, ada_b)


def _ada_grad_adamw(cact_t, dmod, w, m, v, layer, name):
    _, D, NC = w.shape
    tr = _tile(D, 128, 8)

    def body(c_ref, d_ref, w_ref, m_ref, v_ref, g_ref, dl_ref, mo_ref, vo_ref):
        g = jnp.dot(c_ref[...], d_ref[...], preferred_element_type=F32, precision=HIGHEST)
        g_ref[...] = g
        dl_ref[...], mo_ref[...], vo_ref[...] = _adam_math(w_ref[...], g, m_ref[...], v_ref[...])

    lay = pl.BlockSpec((None, tr, NC), lambda i: (layer, i, 0))
    flat = pl.BlockSpec((tr, NC), lambda i: (i, 0))
    return pl.pallas_call(
        body, name=name, grid=(D // tr,),
        in_specs=[pl.BlockSpec((tr, N_DEV), lambda i: (i, 0)), pl.BlockSpec((N_DEV, NC), lambda i: (0, 0)), lay, lay, lay],
        out_specs=[flat] * 4, out_shape=[jax.ShapeDtypeStruct((D, NC), F32)] * 4, compiler_params=_params(("parallel",)),
    )(cact_t, dmod, w, m, v)


def _sum_devices(gathered, name):
    n, R, C = gathered.shape

    def body(g_ref, o_ref):
        acc = g_ref[0]
        for j in range(1, n):
            acc = acc + g_ref[j]
        o_ref[...] = acc

    return pl.pallas_call(body, name=name, out_shape=jax.ShapeDtypeStruct((R, C), F32),
                          compiler_params=pltpu.CompilerParams(vmem_limit_bytes=VMEM_LIMIT))(gathered)


def _adamw_small(w, g, m, v, name):
    def body(w_ref, g_ref, m_ref, v_ref, d_ref, mo_ref, vo_ref):
        d_ref[...], mo_ref[...], vo_ref[...] = _adam_math(w_ref[...], g_ref[...], m_ref[...], v_ref[...])

    return pl.pallas_call(body, name=name, out_shape=[jax.ShapeDtypeStruct(w.shape, F32)] * 3)(w, g, m, v)


def _pad_rows(flat, unit=8 * LANES):
    n = flat.shape[0]
    total = -(-n // unit) * unit
    return jnp.pad(flat, (0, total - n)).reshape(total // LANES, LANES)


def _pad_lanes(v2d):
    return jnp.pad(v2d.reshape(1, -1), ((0, 0), (0, LANES - v2d.size)))


def _step(dm, a):
    S, D, FH, QH, KH, Fh = dm
    ix, iy, ic = lax.axis_index("x"), lax.axis_index("y"), lax.axis_index("c")
    chip = 2 * ix + iy
    dev = 2 * chip + ic
    F2c = a["ffn_w_up"].shape[2]
    NC = a["ada_w"].shape[2]
    PW = 3 * D + LANES
    fox_cols = a["fox_w_in"].shape[2]

    cw_flat = a["ffn_conv_w"].reshape(1, -1)
    e0 = _silu_rows(a["c"], cw_flat, "silu_c")
    width = e0.shape[1]
    pad = -width % LANES
    g0 = _allgather_small(jnp.pad(e0, ((0, 0), (0, pad))), "gather_cond").reshape(N_DEV, 8, width + pad)[:, 0, :]
    cact = g0[:, :D]
    conv_w = g0[0::2, D:D + cw_flat.shape[1]].reshape(N_CHIPS, DEPTH, 3, F2c).transpose(1, 2, 0, 3).reshape(DEPTH, 3, N_CHIPS * F2c)
    rows = _ada_fwd(cact, a["ada_w"], a["ada_b"], ic, chip, "ada_proj")
    g1 = _allgather_small(rows, "gather_mod").reshape(N_CHIPS, DEPTH, 8, NC)
    mod = lax.dynamic_index_in_dim(g1, dev, axis=2, keepdims=False).transpose(1, 0, 2).reshape(DEPTH, N_CHIPS * NC)

    names = ["fox_w_in", "fox_w_o", "swa_w_in", "swa_w_o", "ffn_w_up", "ffn_w_up", "ffn_w_down", "ffn_w_down"]
    layers = [0, 0, 0, 0, 0, 1, 0, 1]
    shards = [_cast_bf16(a[nm], l, f"cast_{nm}{l}") for nm, l in zip(names, layers)]
    full = _gather_weights(shards, "gather_weights")
    fox_in = jnp.pad(full[0].transpose(1, 0, 2).reshape(D, N_CHIPS * fox_cols), ((0, 0), (0, PW - N_CHIPS * fox_cols)))
    W = {"fox_in": fox_in[None], "fox_o": full[1].reshape(1, D, D), "swa_in": full[2], "swa_o": full[3].reshape(1, D, D),
         "up": [full[4], full[5]], "down": [full[6].reshape(1, Fh, D), full[7].reshape(1, Fh, D)]}
    sp = {"fox_b_f": _pad_lanes(a["fox_b_f"]), "sinks": a["swa_sinks"].reshape(QH, 1, 1),
          "conv_w": [conv_w[i] for i in range(DEPTH)], "conv_b": [a["ffn_conv_b"][i:i + 1] for i in range(DEPTH)]}
    for nm in ("ln_mix_g", "ln_mix_b", "ln_ffn_g", "ln_ffn_b"):
        sp[nm] = [a[nm][i:i + 1] for i in range(DEPTH)]

    loss_cols, grad_x, gw, gs, dmod = _local_step(dm, a["x"][0], a["loss_target"][0], a["positions"][0], mod, W, sp)
    loss = lax.psum(0.5 / D * jnp.sum(loss_cols), ("x", "y", "c"))

    g_fox_in = gw["fox_in"][0][:, :N_CHIPS * fox_cols].reshape(D, N_CHIPS, fox_cols).transpose(1, 0, 2)
    contrib = [g_fox_in, gw["fox_o"].reshape(N_CHIPS, D // N_CHIPS, D), gw["swa_in"], gw["swa_o"].reshape(N_CHIPS, D // N_CHIPS, D),
               gw["up"][0], gw["up"][1], gw["down"][0].reshape(N_CHIPS, Fh // N_CHIPS, D), gw["down"][1].reshape(N_CHIPS, Fh // N_CHIPS, D)]
    got = _swap_halves(contrib, "grads_to_sibling")
    part = [_add_sibling(g, o, ic, f"add_sibling{t}") for t, (g, o) in enumerate(zip(contrib, got))]
    landed = _scatter_to_owner(part, "grads_to_owner")
    halves = [_sum_chips(q, f"sum_chips{t}") for t, q in enumerate(landed)]
    grads = _join_halves(halves, "grads_join")

    out = {"loss": loss, "grad_x": grad_x[None]}
    upd = [_adamw(a[nm], g, a["m_" + nm], a["v_" + nm], l, f"adamw_{nm}{l}") for nm, l, g in zip(names, layers, grads)]
    for nm in ("fox_w_in", "fox_w_o", "swa_w_in", "swa_w_o", "ffn_w_up", "ffn_w_down"):
        idx = [t for t, n2 in enumerate(names) if n2 == nm]
        out["grad_" + nm] = jnp.stack([grads[t] for t in idx])
        for k, pre in enumerate(("delta_", "new_m_", "new_v_")):
            out[pre + nm] = jnp.stack([upd[t][k] for t in idx])

    pieces = [dmod.reshape(-1), gs["fox_b_f"].reshape(-1), _pad_lanes(gs["sinks"]).reshape(-1),
              jnp.stack(gs["conv_w"]).reshape(-1), jnp.stack(gs["conv_b"]).reshape(-1)]
    pieces += [jnp.stack(gs[nm]).reshape(-1) for nm in ("ln_mix_g", "ln_mix_b", "ln_ffn_g", "ln_ffn_b")]
    sizes = [p.shape[0] for p in pieces]
    packed = _pad_rows(jnp.concatenate(pieces))
    allp = _allgather_small(packed, "gather_small").reshape(N_DEV, packed.shape[0], LANES)
    tot = _sum_devices(allp, "sum_small").reshape(-1)
    offs = [sum(sizes[:k]) for k in range(len(sizes))]
    take = lambda k: tot[offs[k]:offs[k] + sizes[k]]
    g_small = {"ada_b": take(0).reshape(DEPTH, -1), "fox_b_f": take(1)[:FH].reshape(1, FH), "swa_sinks": take(2)[:QH].reshape(1, QH),
               "ffn_conv_w": lax.dynamic_slice_in_dim(take(3).reshape(DEPTH, 3, N_CHIPS * F2c), chip * F2c, F2c, axis=2),
               "ffn_conv_b": take(4).reshape(DEPTH, -1)}
    for k, nm in enumerate(("ln_mix_g", "ln_mix_b", "ln_ffn_g", "ln_ffn_b")):
        g_small[nm] = take(5 + k).reshape(DEPTH, D)
    small = list(g_small)
    pack = lambda pre: _pad_rows(jnp.concatenate([(a[pre + nm] if pre else a[nm]).reshape(-1) for nm in small]))
    gp = _pad_rows(jnp.concatenate([g_small[nm].reshape(-1) for nm in small]))
    ds_, ms_, vs_ = _adamw_small(pack(""), gp, pack("m_"), pack("v_"), "adamw_small")
    off = 0
    for nm in small:
        n_el = a[nm].size
        out["grad_" + nm] = g_small[nm]
        for pre, arr in (("delta_", ds_), ("new_m_", ms_), ("new_v_", vs_)):
            out[pre + nm] = arr.reshape(-1)[off:off + n_el].reshape(a[nm].shape)
        off += n_el

    dmod_all = allp.reshape(N_DEV, -1)[:, :DEPTH * N_CHIPS * NC].reshape(N_DEV, DEPTH, N_CHIPS * NC)
    dmod_mine = lax.dynamic_slice_in_dim(dmod_all, chip * NC, NC, axis=2)
    ada = [_ada_grad_adamw(cact.T, dmod_mine[:, i, :], a["ada_w"], a["m_ada_w"], a["v_ada_w"], i, f"ada_grad{i}") for i in range(DEPTH)]
    for k, pre in enumerate(("grad_", "delta_", "new_m_", "new_v_")):
        out[pre + "ada_w"] = jnp.stack([ada[i][k] for i in range(DEPTH)])
    return out


_WEIGHTS = ["fox_w_in", "fox_b_f", "fox_w_o", "swa_w_in", "swa_sinks", "swa_w_o", "ada_w", "ada_b", "ffn_w_up", "ffn_conv_w",
            "ffn_conv_b", "ffn_w_down", "ln_mix_g", "ln_mix_b", "ln_ffn_g", "ln_ffn_b"]
_INPUTS = (["x", "c", "positions"] + _WEIGHTS + ["loss_target"] + ["m_" + w for w in _WEIGHTS] + ["v_" + w for w in _WEIGHTS])


def kernel(x, c, positions, fox_w_in, fox_b_f, fox_w_o, swa_w_in, swa_sinks, swa_w_o, ada_w, ada_b, ffn_w_up, ffn_conv_w, ffn_conv_b, ffn_w_down, ln_mix_g, ln_mix_b, ln_ffn_g, ln_ffn_b, loss_target, m_fox_w_in, m_fox_b_f, m_fox_w_o, m_swa_w_in, m_swa_sinks, m_swa_w_o, m_ada_w, m_ada_b, m_ffn_w_up, m_ffn_conv_w, m_ffn_conv_b, m_ffn_w_down, m_ln_mix_g, m_ln_mix_b, m_ln_ffn_g, m_ln_ffn_b, v_fox_w_in, v_fox_b_f, v_fox_w_o, v_swa_w_in, v_swa_sinks, v_swa_w_o, v_ada_w, v_ada_b, v_ffn_w_up, v_ffn_conv_w, v_ffn_conv_b, v_ffn_w_down, v_ln_mix_g, v_ln_mix_b, v_ln_ffn_g, v_ln_ffn_b):
    args = (x, c, positions, fox_w_in, fox_b_f, fox_w_o, swa_w_in, swa_sinks, swa_w_o, ada_w, ada_b, ffn_w_up, ffn_conv_w, ffn_conv_b, ffn_w_down, ln_mix_g, ln_mix_b, ln_ffn_g, ln_ffn_b, loss_target, m_fox_w_in, m_fox_b_f, m_fox_w_o, m_swa_w_in, m_swa_sinks, m_swa_w_o, m_ada_w, m_ada_b, m_ffn_w_up, m_ffn_conv_w, m_ffn_conv_b, m_ffn_w_down, m_ln_mix_g, m_ln_mix_b, m_ln_ffn_g, m_ln_ffn_b, v_fox_w_in, v_fox_b_f, v_fox_w_o, v_swa_w_in, v_swa_sinks, v_swa_w_o, v_ada_w, v_ada_b, v_ffn_w_up, v_ffn_conv_w, v_ffn_conv_b, v_ffn_w_down, v_ln_mix_g, v_ln_mix_b, v_ln_ffn_g, v_ln_ffn_b)
    out = _step(PROD, dict(zip(_INPUTS, args)))
    order = ["loss", "grad_x"] + [p + w for p in ("grad_", "delta_", "new_m_", "new_v_") for w in _WEIGHTS]
    return tuple(out[k] for k in order)
```

```python
import functools
from typing import NamedTuple

import jax
import jax.numpy as jnp
from jax import lax
from jax.experimental import pallas as pl
from jax.experimental.pallas import tpu as pltpu

F32 = jnp.float32
BF16 = jnp.bfloat16
MESH = pl.DeviceIdType.MESH
HIGHEST = lax.Precision.HIGHEST

N_CHIPS = 4
N_DEV = 8
LANES = 128
VMEM_LIMIT = 56 * 1024 * 1024

DEPTH = 2
DEEPNORM_ALPHA = (2.0 * DEPTH) ** 0.25
LN_EPS = 1e-5
ROPE_THETA = 500000.0
ADAM_LR, ADAM_B1, ADAM_B2, ADAM_EPS, ADAM_WD, ADAM_STEP = 0.001, 0.9, 0.999, 1e-08, 0.01, 10
NEG = -1e30


class Dims(NamedTuple):
    S: int
    D: int
    FH: int
    QH: int
    KH: int
    F: int


PROD = Dims(S=2048, D=2048, FH=16, QH=32, KH=4, F=5632)
FDH = 128
SDH = 64
WIN = 128
ROPE_DIM = 16
FOX_TQ = 256


def _params(sem=None, vmem=VMEM_LIMIT):
    return pltpu.CompilerParams(dimension_semantics=sem, vmem_limit_bytes=vmem)


def _tile(n, pref, unit=LANES):
    if n <= pref:
        return n
    t = (pref // unit) * unit
    while t > 0:
        if n % t == 0:
            return t
        t -= unit
    return n


_DN = {"nn": (((1,), (0,)), ((), ())), "nt": (((1,), (1,)), ((), ())), "tn": (((0,), (0,)), ((), ()))}


def _mm(a, b, *, mode, out_dtype, name, out_groups=1, tm=1024, tn=1024, tk=2048):
    ga, ra, ca = a.shape
    gb, rb, cb = b.shape
    if mode == "nn":
        M, K, N = ra, ga * ca, gb * cb
        assert rb == K and ga == 1 or (rb == K)
    elif mode == "nt":
        M, K, N = ra, ga * ca, rb
        assert gb * cb == K
    else:
        K, M, N = ra, ga * ca, gb * cb
        assert rb == K
    go = out_groups
    if mode == "nn":
        tk = _tile(ca, tk); assert rb % tk == 0 and (ga == 1 or True)
        tn = _tile(min(cb, N // go), tn); tm = _tile(M, tm, 8)
    elif mode == "nt":
        tk = _tile(ca, tk); tk = _tile(cb, tk) if cb % tk else tk; assert ca % tk == 0 and cb % tk == 0
        tn = _tile(N // go, tn); tm = _tile(M, tm, 8)
    else:
        tk = _tile(K, tk, 8); tm = _tile(ca, tm); tn = _tile(min(cb, N // go), tn)
    assert (N // go) % tn == 0 and M % tm == 0 and K % tk == 0, (name, M, N, K, tm, tn, tk)
    nk = K // tk
    kpa = max(ca // tk, 1)
    kpb = max(cb // tk, 1)
    npb = max(cb // tn, 1)
    npo = (N // go) // tn
    mpa = max(ca // tm, 1)

    if mode == "nn":
        a_spec = pl.BlockSpec((1, tm, tk), lambda j, i, k: (k // kpa, i, k % kpa))
        b_spec = pl.BlockSpec((1, tk, tn), lambda j, i, k: (j // npb, k, j % npb))
    elif mode == "nt":
        a_spec = pl.BlockSpec((1, tm, tk), lambda j, i, k: (k // kpa, i, k % kpa))
        b_spec = pl.BlockSpec((1, tn, tk), lambda j, i, k: (k // kpb, j, k % kpb))
    else:
        a_spec = pl.BlockSpec((1, tk, tm), lambda j, i, k: (i // mpa, k, i % mpa))
        b_spec = pl.BlockSpec((1, tk, tn), lambda j, i, k: (j // npb, k, j % npb))
    o_spec = pl.BlockSpec((1, tm, tn), lambda j, i, k: (j // npo, i, j % npo))
    dn = _DN[mode]

    def body(a_ref, b_ref, o_ref, *acc):
        p = lax.dot_general(a_ref[0], b_ref[0], dn, preferred_element_type=F32)
        if nk == 1:
            o_ref[0] = p.astype(out_dtype)
        else:
            k = pl.program_id(2)

            @pl.when(k == 0)
            def _():
                acc[0][...] = p

            @pl.when(k > 0)
            def _():
                acc[0][...] += p

            @pl.when(k == nk - 1)
            def _():
                o_ref[0] = acc[0][...].astype(out_dtype)

    return pl.pallas_call(
        body, name=name, grid=(N // tn, M // tm, nk),
        in_specs=[a_spec, b_spec], out_specs=o_spec,
        out_shape=jax.ShapeDtypeStruct((go, M, N // go), out_dtype),
        scratch_shapes=[pltpu.VMEM((tm, tn), F32)] if nk > 1 else [],
        compiler_params=_params(("parallel", "parallel", "arbitrary")),
    )(a, b)


def _rows(tr, d):
    return pl.BlockSpec((tr, d), lambda i: (i, 0))


def _vec(d):
    return pl.BlockSpec((1, d), lambda i: (0, 0))


def _modulate(x, sc, sh, name):
    S, D = x.shape
    tr = min(256, S)

    def body(x_ref, sc_ref, sh_ref, h_ref):
        h_ref[...] = (x_ref[...] * (1.0 + sc_ref[...]) + sh_ref[...]).astype(BF16)

    return pl.pallas_call(
        body, name=name, grid=(S // tr,), in_specs=[_rows(tr, D), _vec(D), _vec(D)], out_specs=_rows(tr, D),
        out_shape=jax.ShapeDtypeStruct((S, D), BF16), compiler_params=_params(("parallel",)),
    )(x, sc, sh)


def _ln_fwd(x, y, gate, gamma, beta, sc, sh, name):
    S, D = x.shape
    tr = min(256, S)
    emit_h = sc is not None

    def body(*refs):
        if emit_h:
            x_ref, y_ref, g_ref, ga_ref, be_ref, sc_ref, sh_ref, xo_ref, xh_ref, rs_ref, h_ref = refs
        else:
            x_ref, y_ref, g_ref, ga_ref, be_ref, xo_ref, xh_ref, rs_ref = refs
        z = DEEPNORM_ALPHA * x_ref[...] + (1.0 + g_ref[...]) * y_ref[...]
        mu = jnp.mean(z, axis=-1, keepdims=True)
        zc = z - mu
        var = jnp.mean(zc * zc, axis=-1, keepdims=True)
        rstd = lax.rsqrt(var + LN_EPS)
        xh = zc * rstd
        xo = xh * ga_ref[...] + be_ref[...]
        xo_ref[...] = xo
        xh_ref[...] = xh
        rs_ref[...] = rstd
        if emit_h:
            h_ref[...] = (xo * (1.0 + sc_ref[...]) + sh_ref[...]).astype(BF16)

    ins = [x, y, gate, gamma, beta] + ([sc, sh] if emit_h else [])
    in_specs = [_rows(tr, D), _rows(tr, D)] + [_vec(D)] * (len(ins) - 2)
    out_shape = [jax.ShapeDtypeStruct((S, D), F32), jax.ShapeDtypeStruct((S, D), F32), jax.ShapeDtypeStruct((S, 1), F32)]
    out_specs = [_rows(tr, D), _rows(tr, D), _rows(tr, 1)]
    if emit_h:
        out_shape.append(jax.ShapeDtypeStruct((S, D), BF16))
        out_specs.append(_rows(tr, D))
    return pl.pallas_call(
        body, name=name, grid=(S // tr,), in_specs=in_specs, out_specs=out_specs, out_shape=out_shape,
        compiler_params=_params(("parallel",)),
    )(*ins)


def _loss_head(xf, tgt, name):
    S, D = xf.shape
    tr = min(256, S)

    def body(x_ref, t_ref, dx_ref, l_ref):
        e = x_ref[...] - t_ref[...]
        dx_ref[...] = e * (1.0 / D)

        @pl.when(pl.program_id(0) == 0)
        def _():
            l_ref[...] = jnp.zeros_like(l_ref)

        l_ref[...] += jnp.sum(e * e, axis=0, keepdims=True)

    return pl.pallas_call(
        body, name=name, grid=(S // tr,), in_specs=[_rows(tr, D), _rows(tr, D)],
        out_specs=[_rows(tr, D), _vec(D)],
        out_shape=[jax.ShapeDtypeStruct((S, D), F32), jax.ShapeDtypeStruct((1, D), F32)],
        compiler_params=_params(("arbitrary",)),
    )(xf, tgt)


def _ln_bwd(dxo, xh, rstd, gamma, y, gate, name):
    S, D = dxo.shape
    tr = min(256, S)

    def body(dx_ref, xh_ref, rs_ref, ga_ref, y_ref, g_ref, dres_ref, dy_ref, dga_ref, dbe_ref, dg_ref):
        dxo_ = dx_ref[...]
        xh_ = xh_ref[...]
        dxh = dxo_ * ga_ref[...]
        m1 = jnp.mean(dxh, axis=-1, keepdims=True)
        m2 = jnp.mean(dxh * xh_, axis=-1, keepdims=True)
        dz = rs_ref[...] * (dxh - m1 - xh_ * m2)
        dres_ref[...] = DEEPNORM_ALPHA * dz
        dy_ref[...] = ((1.0 + g_ref[...]) * dz).astype(BF16)

        @pl.when(pl.program_id(0) == 0)
        def _():
            dga_ref[...] = jnp.zeros_like(dga_ref)
            dbe_ref[...] = jnp.zeros_like(dbe_ref)
            dg_ref[...] = jnp.zeros_like(dg_ref)

        dga_ref[...] += jnp.sum(dxo_ * xh_, axis=0, keepdims=True)
        dbe_ref[...] += jnp.sum(dxo_, axis=0, keepdims=True)
        dg_ref[...] += jnp.sum(dz * y_ref[...], axis=0, keepdims=True)

    return pl.pallas_call(
        body, name=name, grid=(S // tr,),
        in_specs=[_rows(tr, D), _rows(tr, D), _rows(tr, 1), _vec(D), _rows(tr, D), _vec(D)],
        out_specs=[_rows(tr, D), _rows(tr, D), _vec(D), _vec(D), _vec(D)],
        out_shape=[jax.ShapeDtypeStruct((S, D), F32), jax.ShapeDtypeStruct((S, D), BF16)] + [jax.ShapeDtypeStruct((1, D), F32)] * 3,
        compiler_params=_params(("arbitrary",)),
    )(dxo, xh, rstd, gamma, y, gate)


def _mod_bwd(dh, x, sc, dres, name):
    S, D = x.shape
    tr = min(256, S)

    def body(dh_ref, x_ref, sc_ref, dr_ref, dx_ref, dsc_ref, dsh_ref):
        dh_ = dh_ref[...]
        dx_ref[...] = dr_ref[...] + dh_ * (1.0 + sc_ref[...])

        @pl.when(pl.program_id(0) == 0)
        def _():
            dsc_ref[...] = jnp.zeros_like(dsc_ref)
            dsh_ref[...] = jnp.zeros_like(dsh_ref)

        dsc_ref[...] += jnp.sum(dh_ * x_ref[...], axis=0, keepdims=True)
        dsh_ref[...] += jnp.sum(dh_, axis=0, keepdims=True)

    return pl.pallas_call(
        body, name=name, grid=(S // tr,),
        in_specs=[_rows(tr, D), _rows(tr, D), _vec(D), _rows(tr, D)],
        out_specs=[_rows(tr, D), _vec(D), _vec(D)],
        out_shape=[jax.ShapeDtypeStruct((S, D), F32), jax.ShapeDtypeStruct((1, D), F32), jax.ShapeDtypeStruct((1, D), F32)],
        compiler_params=_params(("arbitrary",)),
    )(dh, x, sc, dres)


def _log_sigmoid(z):
    return jnp.minimum(z, 0.0) - jnp.log(1.0 + jnp.exp(-jnp.abs(z)))


def _fox_gate_fwd(proj, b_f, n_heads, name):
    S, PW = proj.shape
    blk = min(256, S)
    last = PW // LANES - 1

    def body(fl_ref, b_ref, cum_ref):
        r = lax.broadcasted_iota(jnp.int32, (blk, blk), 0)
        c = lax.broadcasted_iota(jnp.int32, (blk, blk), 1)
        tril = (c <= r).astype(F32)
        carry = jnp.zeros((1, LANES), F32)
        for i in range(S // blk):
            lf = _log_sigmoid(fl_ref[i * blk:(i + 1) * blk, :] + b_ref[...])
            cum_ref[i * blk:(i + 1) * blk, :] = jnp.dot(tril, lf, preferred_element_type=F32, precision=HIGHEST) + carry
            carry = carry + jnp.sum(lf, axis=0, keepdims=True)

    return pl.pallas_call(
        body, name=name, grid=(1,),
        in_specs=[pl.BlockSpec((S, LANES), lambda i: (0, last)), pl.BlockSpec((1, LANES), lambda i: (0, 0))],
        out_specs=pl.BlockSpec((S, LANES), lambda i: (0, 0)),
        out_shape=jax.ShapeDtypeStruct((S, LANES), F32), compiler_params=_params(("arbitrary",)),
    )(proj, b_f)


def _fox_gate_bwd(dcum, proj, b_f, n_heads, name):
    S, PW = proj.shape
    blk = min(256, S)
    last = PW // LANES - 1
    nb = S // blk

    def body(dc_ref, fl_ref, b_ref, dfl_ref, db_ref):
        r = lax.broadcasted_iota(jnp.int32, (blk, blk), 0)
        c = lax.broadcasted_iota(jnp.int32, (blk, blk), 1)
        triu = (c >= r).astype(F32)
        lane = lax.broadcasted_iota(jnp.int32, (blk, LANES), 1)
        carry = jnp.zeros((1, LANES), F32)
        dbs = jnp.zeros((1, LANES), F32)
        for i in reversed(range(nb)):
            dc = dc_ref[i * blk:(i + 1) * blk, :]
            dlf = jnp.dot(triu, dc, preferred_element_type=F32, precision=HIGHEST) + carry
            carry = carry + jnp.sum(dc, axis=0, keepdims=True)
            z = fl_ref[i * blk:(i + 1) * blk, :] + b_ref[...]
            e = jnp.exp(-jnp.abs(z))
            sig_neg = jnp.where(z >= 0, e / (1.0 + e), 1.0 / (1.0 + e))
            dfl = jnp.where(lane < n_heads, dlf * sig_neg, 0.0)
            dfl_ref[i * blk:(i + 1) * blk, :] = dfl.astype(BF16)
            dbs = dbs + jnp.sum(dfl, axis=0, keepdims=True)
        db_ref[...] = dbs

    return pl.pallas_call(
        body, name=name, grid=(1,),
        in_specs=[pl.BlockSpec((S, LANES), lambda i: (0, 0)), pl.BlockSpec((S, LANES), lambda i: (0, last)),
                  pl.BlockSpec((1, LANES), lambda i: (0, 0))],
        out_specs=[pl.BlockSpec((S, LANES), lambda i: (0, 0)), pl.BlockSpec((1, LANES), lambda i: (0, 0))],
        out_shape=[jax.ShapeDtypeStruct((S, LANES), BF16), jax.ShapeDtypeStruct((1, LANES), F32)],
        compiler_params=_params(("arbitrary",)),
    )(dcum, proj, b_f)


def _fox_scores(q_ref, kb_ref, cq_ref, ck_ref, qi, tq, scale):
    kk = (qi + 1) * tq
    rows = slice(qi * tq, (qi + 1) * tq)
    qb = q_ref[rows, :].astype(BF16)
    s = lax.dot_general(qb, kb_ref[0:kk, :], _DN["nt"], preferred_element_type=F32) * scale
    s = s + (cq_ref[0, rows, :] - ck_ref[0, :, 0:kk])
    r = lax.broadcasted_iota(jnp.int32, (tq, kk), 0) + qi * tq
    c = lax.broadcasted_iota(jnp.int32, (tq, kk), 1)
    mask = c <= r
    return jnp.where(mask, s, NEG), mask, qb


def _fox_fwd(proj, cq, ck, n_heads, name):
    S = proj.shape[0]
    H = n_heads
    tq = min(FOX_TQ, S)
    nq = S // tq
    scale = FDH ** -0.5

    def body(q_ref, k_ref, v_ref, cq_ref, ck_ref, o_ref, lse_ref, kb_ref, vb_ref):
        kb_ref[...] = k_ref[...].astype(BF16)
        vb_ref[...] = v_ref[...].astype(BF16)
        for qi in range(nq):
            kk = (qi + 1) * tq
            rows = slice(qi * tq, (qi + 1) * tq)
            s, _, _ = _fox_scores(q_ref, kb_ref, cq_ref, ck_ref, qi, tq, scale)
            m = jnp.max(s, axis=-1, keepdims=True)
            p = jnp.exp(s - m)
            l = jnp.sum(p, axis=-1, keepdims=True)
            p = p * (1.0 / l)
            o_ref[rows, :] = jnp.dot(p.astype(BF16), vb_ref[0:kk, :], preferred_element_type=F32).astype(BF16)
            lse_ref[0, rows, :] = m + jnp.log(l)

    col = lambda off: pl.BlockSpec((S, FDH), lambda h: (0, h + off))
    stat_c = pl.BlockSpec((1, S, 1), lambda h: (h, 0, 0))
    stat_r = pl.BlockSpec((1, 1, S), lambda h: (h, 0, 0))
    return pl.pallas_call(
        body, name=name, grid=(H,),
        in_specs=[col(0), col(H), col(2 * H), stat_c, stat_r],
        out_specs=[col(0), stat_c],
        out_shape=[jax.ShapeDtypeStruct((S, H * FDH), BF16), jax.ShapeDtypeStruct((H, S, 1), F32)],
        scratch_shapes=[pltpu.VMEM((S, FDH), BF16), pltpu.VMEM((S, FDH), BF16)],
        compiler_params=_params(("parallel",)),
    )(proj, proj, proj, cq, ck)


def _fox_bwd(proj, cq, ck, lse, do, n_heads, name):
    S = proj.shape[0]
    H = n_heads
    tq = min(FOX_TQ, S)
    nq = S // tq
    scale = FDH ** -0.5

    def body(q_ref, k_ref, v_ref, cq_ref, ck_ref, lse_ref, do_ref, dq_ref, dk_ref, dv_ref, dcq_ref, dck_ref,
             kb_ref, vb_ref, dka_ref, dva_ref):
        kb_ref[...] = k_ref[...].astype(BF16)
        vb_ref[...] = v_ref[...].astype(BF16)
        dka_ref[...] = jnp.zeros_like(dka_ref)
        dva_ref[...] = jnp.zeros_like(dva_ref)
        dck_ref[...] = jnp.zeros_like(dck_ref)
        for qi in range(nq):
            kk = (qi + 1) * tq
            rows = slice(qi * tq, (qi + 1) * tq)
            s, mask, qb = _fox_scores(q_ref, kb_ref, cq_ref, ck_ref, qi, tq, scale)
            p = jnp.where(mask, jnp.exp(s - lse_ref[0, rows, :]), 0.0)
            dob = do_ref[rows, :]
            dp = lax.dot_general(dob, vb_ref[0:kk, :], _DN["nt"], preferred_element_type=F32)
            delta = jnp.sum(p * dp, axis=-1, keepdims=True)
            ds = p * (dp - delta)
            dcq_ref[0, rows, :] = jnp.sum(ds, axis=-1, keepdims=True)
            dck_ref[0, :, 0:kk] -= jnp.sum(ds, axis=0, keepdims=True)
            dsb = (ds * scale).astype(BF16)
            dq_ref[rows, :] = jnp.dot(dsb, kb_ref[0:kk, :], preferred_element_type=F32).astype(BF16)
            dka_ref[0:kk, :] += lax.dot_general(dsb, qb, _DN["tn"], preferred_element_type=F32)
            dva_ref[0:kk, :] += lax.dot_general(p.astype(BF16), dob, _DN["tn"], preferred_element_type=F32)
        dk_ref[...] = dka_ref[...].astype(BF16)
        dv_ref[...] = dva_ref[...].astype(BF16)

    col = lambda off: pl.BlockSpec((S, FDH), lambda h: (0, h + off))
    stat_c = pl.BlockSpec((1, S, 1), lambda h: (h, 0, 0))
    stat_r = pl.BlockSpec((1, 1, S), lambda h: (h, 0, 0))
    wide = jax.ShapeDtypeStruct((S, H * FDH), BF16)
    return pl.pallas_call(
        body, name=name, grid=(H,),
        in_specs=[col(0), col(H), col(2 * H), stat_c, stat_r, stat_c, col(0)],
        out_specs=[col(0), col(0), col(0), stat_c, stat_r],
        out_shape=[wide, wide, wide, jax.ShapeDtypeStruct((H, S, 1), F32), jax.ShapeDtypeStruct((H, 1, S), F32)],
        scratch_shapes=[pltpu.VMEM((S, FDH), BF16), pltpu.VMEM((S, FDH), BF16), pltpu.VMEM((S, FDH), F32), pltpu.VMEM((S, FDH), F32)],
        compiler_params=_params(("parallel",)),
    )(proj, proj, proj, cq, ck, lse, do)


def _rope_tables(pos, sign):
    inv = ROPE_THETA ** (-jnp.arange(0, ROPE_DIM, 2, dtype=F32) / ROPE_DIM)
    ang = pos.astype(F32)[:, None] * inv
    cos, sin = jnp.cos(ang), sign * jnp.sin(ang)
    l64 = jnp.arange(LANES) % SDH
    idx = l64 % (ROPE_DIM // 2)
    c = jnp.where(l64 < ROPE_DIM, cos[:, idx], 1.0)
    sa = jnp.where(l64 < ROPE_DIM // 2, -sin[:, idx], 0.0)
    sb = jnp.where((l64 >= ROPE_DIM // 2) & (l64 < ROPE_DIM), sin[:, idx], 0.0)
    rot = jnp.stack([c, sa, sb])
    ident = jnp.stack([jnp.ones_like(c), jnp.zeros_like(c), jnp.zeros_like(c)])
    return jnp.stack([rot, ident]).astype(F32)


def _rope(xin, tabs, n_rot, out_dtype, name):
    S, W = xin.shape

    def body(x_ref, t_ref, o_ref):
        xv = x_ref[...]
        o = xv * t_ref[0, 0] + pltpu.roll(xv, LANES - ROPE_DIM // 2, 1) * t_ref[0, 1] + pltpu.roll(xv, ROPE_DIM // 2, 1) * t_ref[0, 2]
        o_ref[...] = o.astype(out_dtype)

    return pl.pallas_call(
        body, name=name, grid=(W // LANES,),
        in_specs=[pl.BlockSpec((S, LANES), lambda j: (0, j)),
                  pl.BlockSpec((1, 3, S, LANES), lambda j: (jnp.where(j < n_rot, 0, 1), 0, 0, 0))],
        out_specs=pl.BlockSpec((S, LANES), lambda j: (0, j)),
        out_shape=jax.ShapeDtypeStruct((S, W), out_dtype), compiler_params=_params(("parallel",)),
    )(xin, tabs)


def _swa_probs(q_ref, k_ref, sk_ref, n, scale):
    st = pl.multiple_of(jnp.maximum(n - 1, 0) * WIN, WIN)
    qb = q_ref[0, 0]
    kb = k_ref[0, pl.ds(st, 2 * WIN), :]
    gm = qb.shape[0]
    s = lax.dot_general(qb, kb, _DN["nt"], preferred_element_type=F32) * scale
    qa = n * WIN + (lax.broadcasted_iota(jnp.int32, (gm, 2 * WIN), 0) & (WIN - 1))
    ka = st + lax.broadcasted_iota(jnp.int32, (gm, 2 * WIN), 1)
    valid = (ka <= qa) & (qa - ka < WIN)
    s = jnp.where(valid, s, NEG)
    sink = sk_ref[0]
    m = jnp.maximum(jnp.max(s, axis=-1, keepdims=True), sink)
    e = jnp.where(valid, jnp.exp(s - m), 0.0)
    es = jnp.exp(sink - m)
    inv = 1.0 / (jnp.sum(e, axis=-1, keepdims=True) + es)
    return e * inv, es * inv, st, qb, kb


def _swa_specs(S, gm):
    blk = pl.BlockSpec((1, 1, gm, SDH), lambda g, n: (g, n, 0, 0))
    kv = pl.BlockSpec((1, S, SDH), lambda g, n: (g, 0, 0))
    col = pl.BlockSpec((1, gm, 1), lambda g, n: (g, 0, 0))
    return blk, kv, col


def _swa_fwd(q, k, v, sinks, name):
    KH, nb, gm, _ = q.shape
    S = k.shape[1]
    scale = SDH ** -0.5

    def body(q_ref, k_ref, v_ref, sk_ref, o_ref):
        p, _, st, _, _ = _swa_probs(q_ref, k_ref, sk_ref, pl.program_id(1), scale)
        vb = v_ref[0, pl.ds(st, 2 * WIN), :]
        o_ref[0, 0] = jnp.dot(p.astype(BF16), vb, preferred_element_type=F32).astype(BF16)

    blk, kv, col = _swa_specs(S, gm)
    return pl.pallas_call(
        body, name=name, grid=(KH, nb), in_specs=[blk, kv, kv, col], out_specs=blk,
        out_shape=jax.ShapeDtypeStruct(q.shape, BF16), compiler_params=_params(("parallel", "parallel")),
    )(q, k, v, sinks)


def _swa_bwd(q, k, v, sinks, do, name):
    KH, nb, gm, _ = q.shape
    S = k.shape[1]
    scale = SDH ** -0.5

    def body(q_ref, k_ref, v_ref, sk_ref, do_ref, dq_ref, dk_ref, dv_ref, dsk_ref):
        n = pl.program_id(1)

        @pl.when(n == 0)
        def _():
            dk_ref[...] = jnp.zeros_like(dk_ref)
            dv_ref[...] = jnp.zeros_like(dv_ref)
            dsk_ref[...] = jnp.zeros_like(dsk_ref)

        p, ps, st, qb, kb = _swa_probs(q_ref, k_ref, sk_ref, n, scale)
        vb = v_ref[0, pl.ds(st, 2 * WIN), :]
        dob = do_ref[0, 0]
        dp = lax.dot_general(dob, vb, _DN["nt"], preferred_element_type=F32)
        delta = jnp.sum(p * dp, axis=-1, keepdims=True)
        ds = p * (dp - delta)
        dsb = (ds * scale).astype(BF16)
        dq_ref[0, 0] = jnp.dot(dsb, kb, preferred_element_type=F32)
        dk_ref[0, pl.ds(st, 2 * WIN), :] += lax.dot_general(dsb, qb, _DN["tn"], preferred_element_type=F32)
        dv_ref[0, pl.ds(st, 2 * WIN), :] += lax.dot_general(p.astype(BF16), dob, _DN["tn"], preferred_element_type=F32)
        dsk_ref[0] -= ps * delta

    blk, kv, col = _swa_specs(S, gm)
    return pl.pallas_call(
        body, name=name, grid=(KH, nb), in_specs=[blk, kv, kv, col, blk], out_specs=[blk, kv, kv, col],
        out_shape=[jax.ShapeDtypeStruct(q.shape, F32), jax.ShapeDtypeStruct(k.shape, F32),
                   jax.ShapeDtypeStruct(k.shape, F32), jax.ShapeDtypeStruct(sinks.shape, F32)],
        compiler_params=_params(("parallel", "arbitrary")),
    )(q, k, v, sinks, do)


def _shift_down(u, k):
    row = lax.broadcasted_iota(jnp.int32, u.shape, 0)
    return jnp.where(row >= k, pltpu.roll(u, k, 0), 0.0)


def _shift_up(u, k):
    n = u.shape[0]
    row = lax.broadcasted_iota(jnp.int32, u.shape, 0)
    return jnp.where(row < n - k, pltpu.roll(u, n - k, 0), 0.0)


def _conv3(u, w_ref, b_ref):
    return w_ref[0:1, :] * _shift_down(u, 2) + w_ref[1:2, :] * _shift_down(u, 1) + w_ref[2:3, :] * u + b_ref[...]


def _conv_gate(u, cw, cb, name):
    S, F2 = u.shape
    Fh = F2 // 2
    tc = _tile(Fh, 256)
    nf = Fh // tc

    def body(ug_ref, uv_ref, wg_ref, wv_ref, bg_ref, bv_ref, a_ref):
        g = _conv3(ug_ref[...], wg_ref, bg_ref)
        val = _conv3(uv_ref[...], wv_ref, bv_ref)
        a_ref[...] = (g * (1.0 / (1.0 + jnp.exp(-g))) * val).astype(BF16)

    blk = lambda r, off: pl.BlockSpec((r, tc), lambda j: (0, j + off))
    return pl.pallas_call(
        body, name=name, grid=(nf,),
        in_specs=[blk(S, 0), blk(S, nf), blk(3, 0), blk(3, nf), blk(1, 0), blk(1, nf)], out_specs=blk(S, 0),
        out_shape=jax.ShapeDtypeStruct((S, Fh), BF16), compiler_params=_params(("parallel",)),
    )(u, u, cw, cw, cb, cb)


def _conv_gate_bwd(u, da, cw, cb, name):
    S, F2 = u.shape
    Fh = F2 // 2
    tc = _tile(Fh, 256)
    nf = Fh // tc

    def half(dx, uu, w_ref, du_ref, dw_ref, db_ref):
        du = w_ref[2:3, :] * dx + w_ref[1:2, :] * _shift_up(dx, 1) + w_ref[0:1, :] * _shift_up(dx, 2)
        du_ref[...] = du.astype(BF16)
        dw_ref[0:1, :] = jnp.sum(dx * _shift_down(uu, 2), axis=0, keepdims=True)
        dw_ref[1:2, :] = jnp.sum(dx * _shift_down(uu, 1), axis=0, keepdims=True)
        dw_ref[2:3, :] = jnp.sum(dx * uu, axis=0, keepdims=True)
        db_ref[...] = jnp.sum(dx, axis=0, keepdims=True)

    def body(ug_ref, uv_ref, da_ref, wg_ref, wv_ref, bg_ref, bv_ref, dug_ref, duv_ref, dwg_ref, dwv_ref, dbg_ref, dbv_ref):
        ug = ug_ref[...]
        uv = uv_ref[...]
        g = _conv3(ug, wg_ref, bg_ref)
        val = _conv3(uv, wv_ref, bv_ref)
        sig = 1.0 / (1.0 + jnp.exp(-g))
        da_ = da_ref[...]
        dg = da_ * val * (sig * (1.0 + g * (1.0 - sig)))
        dval = da_ * (g * sig)
        half(dg, ug, wg_ref, dug_ref, dwg_ref, dbg_ref)
        half(dval, uv, wv_ref, duv_ref, dwv_ref, dbv_ref)

    blk = lambda r, off: pl.BlockSpec((r, tc), lambda j: (0, j + off))
    return pl.pallas_call(
        body, name=name, grid=(nf,),
        in_specs=[blk(S, 0), blk(S, nf), blk(S, 0), blk(3, 0), blk(3, nf), blk(1, 0), blk(1, nf)],
        out_specs=[blk(S, 0), blk(S, 0), blk(3, 0), blk(3, 0), blk(1, 0), blk(1, 0)],
        out_shape=[jax.ShapeDtypeStruct((S, Fh), BF16), jax.ShapeDtypeStruct((S, Fh), BF16),
                   jax.ShapeDtypeStruct((3, Fh), F32), jax.ShapeDtypeStruct((3, Fh), F32),
                   jax.ShapeDtypeStruct((1, Fh), F32), jax.ShapeDtypeStruct((1, Fh), F32)],
        compiler_params=_params(("parallel",)),
    )(u, u, da, cw, cw, cb, cb)


def _to_groups(t, kh):
    S, width = t.shape
    g = width // SDH // kh
    return t.reshape(S // WIN, WIN, kh, g, SDH).transpose(2, 0, 3, 1, 4).reshape(kh, S // WIN, g * WIN, SDH)


def _from_groups(t):
    kh, nb, gm, _ = t.shape
    g = gm // WIN
    return t.reshape(kh, nb, g, WIN, SDH).transpose(1, 3, 0, 2, 4).reshape(nb * WIN, kh * g * SDH)


def _local_step(dm, x, tgt, pos, mod, W, sp):
    S, D, FH, QH, KH, Fh = dm
    m = [[mod[i:i + 1, j * D:(j + 1) * D] for j in range(6)] for i in range(DEPTH)]
    sv = []
    xs = x
    h = _modulate(xs, m[0][1], m[0][0], "mod_in")
    save = {}
    for i in range(DEPTH):
        sh1, sc1, g1, sh2, sc2, g2 = m[i]
        L = {}
        L["x_in"], L["h1"] = xs, h
        if i == 0:
            proj = _mm(h[None], W["fox_in"], mode="nn", out_dtype=F32, name="fox_proj", tn=896)[0]
            cum = _fox_gate_fwd(proj, sp["fox_b_f"], FH, "fox_gate")
            cq = cum[:, :FH].T[:, :, None]
            ck = cum[:, :FH].T[:, None, :]
            o, lse = _fox_fwd(proj, cq, ck, FH, "fox_attn")
            L.update(proj=proj, cq=cq, ck=ck, lse=lse, o=o)
            y = _mm(o[None], W["fox_o"], mode="nn", out_dtype=F32, name="fox_out")[0]
        else:
            proj = _mm(h[None], W["swa_in"], mode="nn", out_dtype=F32, name="swa_proj", tn=640)[0]
            tabs = _rope_tables(pos, 1.0)
            n_rot = (QH + KH) * SDH // LANES
            pr = _rope(proj, tabs, n_rot, BF16, "swa_rope")
            qh = _to_groups(pr[:, :QH * SDH], KH)
            kh = pr[:, QH * SDH:(QH + KH) * SDH].reshape(S, KH, SDH).transpose(1, 0, 2)
            vh = pr[:, (QH + KH) * SDH:].reshape(S, KH, SDH).transpose(1, 0, 2)
            oh = _swa_fwd(qh, kh, vh, sp["sinks"], "swa_attn")
            o = _from_groups(oh)
            L.update(qh=qh, kh=kh, vh=vh, o=o)
            y = _mm(o[None], W["swa_o"], mode="nn", out_dtype=F32, name="swa_out")[0]
        L["y1"] = y
        x1, L["xh1"], L["rs1"], h2 = _ln_fwd(xs, y, g1, sp["ln_mix_g"][i], sp["ln_mix_b"][i], sc2, sh2, f"ln_mix{i}")
        L["x1"], L["h2"] = x1, h2
        u = _mm(h2[None], W["up"][i], mode="nn", out_dtype=F32, name=f"ffn_up{i}", tm=512, tn=1408)[0]
        a = _conv_gate(u, sp["conv_w"][i], sp["conv_b"][i], f"ffn_gate{i}")
        y2 = _mm(a[None], W["down"][i], mode="nn", out_dtype=F32, name=f"ffn_down{i}", tk=1408)[0]
        L.update(u=u, a=a, y2=y2)
        if i + 1 < DEPTH:
            xs, L["xh2"], L["rs2"], h = _ln_fwd(x1, y2, g2, sp["ln_ffn_g"][i], sp["ln_ffn_b"][i], m[i + 1][1], m[i + 1][0], f"ln_ffn{i}")
        else:
            xs, L["xh2"], L["rs2"] = _ln_fwd(x1, y2, g2, sp["ln_ffn_g"][i], sp["ln_ffn_b"][i], None, None, f"ln_ffn{i}")
        sv.append(L)

    dx, loss_cols = _loss_head(xs, tgt, "loss_head")

    gw = {"up": [None] * DEPTH, "down": [None] * DEPTH}
    gs = {k: [None] * DEPTH for k in ("conv_w", "conv_b", "ln_mix_g", "ln_mix_b", "ln_ffn_g", "ln_ffn_b")}
    dmod = [None] * DEPTH
    for i in reversed(range(DEPTH)):
        sh1, sc1, g1, sh2, sc2, g2 = m[i]
        L = sv[i]
        dres, dy, gs["ln_ffn_g"][i], gs["ln_ffn_b"][i], dg2 = _ln_bwd(dx, L["xh2"], L["rs2"], sp["ln_ffn_g"][i], L["y2"], g2, f"ln_ffn_bwd{i}")
        da = _mm(dy[None], W["down"][i], mode="nt", out_dtype=F32, name=f"ffn_down_dx{i}", tm=512, tn=1408)[0]
        gw["down"][i] = _mm(L["a"][None], dy[None], mode="tn", out_dtype=BF16, name=f"ffn_down_dw{i}", tm=1408)
        dug, duv, dwg, dwv, dbg, dbv = _conv_gate_bwd(L["u"], da, sp["conv_w"][i], sp["conv_b"][i], f"ffn_gate_bwd{i}")
        du = jnp.concatenate([dug, duv], axis=1)
        gs["conv_w"][i] = jnp.concatenate([dwg, dwv], axis=1)
        gs["conv_b"][i] = jnp.concatenate([dbg, dbv], axis=1)
        dh2 = _mm(du[None], W["up"][i], mode="nt", out_dtype=F32, name=f"ffn_up_dx{i}", tk=1408)[0]
        gw["up"][i] = _mm(L["h2"][None], du[None], mode="tn", out_dtype=BF16, name=f"ffn_up_dw{i}", out_groups=N_CHIPS, tn=1408)
        dx, dsc2, dsh2 = _mod_bwd(dh2, L["x1"], sc2, dres, f"mod_ffn_bwd{i}")
        dres, dy, gs["ln_mix_g"][i], gs["ln_mix_b"][i], dg1 = _ln_bwd(dx, L["xh1"], L["rs1"], sp["ln_mix_g"][i], L["y1"], g1, f"ln_mix_bwd{i}")
        if i == 0:
            do = _mm(dy[None], W["fox_o"], mode="nt", out_dtype=BF16, name="fox_out_dx")[0]
            gw["fox_o"] = _mm(L["o"][None], dy[None], mode="tn", out_dtype=BF16, name="fox_out_dw")
            dq, dk, dv, dcq, dck = _fox_bwd(L["proj"], L["cq"], L["ck"], L["lse"], do, FH, "fox_attn_bwd")
            dcum = dcq[:, :, 0].T + dck[:, 0, :].T
            dcum = jnp.pad(dcum, ((0, 0), (0, LANES - FH)))
            dfl, db_f = _fox_gate_bwd(dcum, L["proj"], sp["fox_b_f"], FH, "fox_gate_bwd")
            gs["fox_b_f"] = db_f
            dproj = jnp.concatenate([dq, dk, dv, dfl], axis=1)
            dh1 = _mm(dproj[None], W["fox_in"], mode="nt", out_dtype=F32, name="fox_proj_dx", tk=896)[0]
            gw["fox_in"] = _mm(L["h1"][None], dproj[None], mode="tn", out_dtype=BF16, name="fox_proj_dw", tn=896)
        else:
            do = _mm(dy[None], W["swa_o"], mode="nt", out_dtype=BF16, name="swa_out_dx")[0]
            gw["swa_o"] = _mm(L["o"][None], dy[None], mode="tn", out_dtype=BF16, name="swa_out_dw")
            dqh, dkh, dvh, dsk = _swa_bwd(L["qh"], L["kh"], L["vh"], sp["sinks"], _to_groups(do, KH), "swa_attn_bwd")
            gs["sinks"] = jnp.sum(dsk.reshape(QH, WIN), axis=1)
            dpr = jnp.concatenate([_from_groups(dqh), dkh.transpose(1, 0, 2).reshape(S, KH * SDH),
                                   dvh.transpose(1, 0, 2).reshape(S, KH * SDH)], axis=1)
            n_rot = (QH + KH) * SDH // LANES
            dproj = _rope(dpr, _rope_tables(pos, -1.0), n_rot, BF16, "swa_rope_bwd")
            dh1 = _mm(dproj[None], W["swa_in"], mode="nt", out_dtype=F32, name="swa_proj_dx", tk=640)[0]
            gw["swa_in"] = _mm(L["h1"][None], dproj[None], mode="tn", out_dtype=BF16, name="swa_proj_dw", out_groups=N_CHIPS, tn=640)
        dx, dsc1, dsh1 = _mod_bwd(dh1, L["x_in"], sc1, dres, f"mod_mix_bwd{i}")
        dmod[i] = jnp.concatenate([dsh1, dsc1, dg1, dsh2, dsc2, dg2], axis=1)
    return loss_cols, dx, gw, gs, jnp.concatenate(dmod, axis=0)


def _place():
    x, y, c = lax.axis_index("x"), lax.axis_index("y"), lax.axis_index("c")
    chips = [(1 - x, y), (x, 1 - y), (1 - x, 1 - y)]
    return x, y, c, chips


def _remote(src, dst, send, recv, to):
    return pltpu.make_async_remote_copy(src_ref=src, dst_ref=dst, send_sem=send, recv_sem=recv, device_id=to, device_id_type=MESH)


def _allgather_small(v, name):
    m_per, n = v.shape

    def body(x_ref, out_ref, send_sems, recv_sems, local_sem):
        x, y, c, chips = _place()
        me, sibling = (x, y, c), (x, y, 1 - c)

        def rows(px, py, pc):
            return out_ref.at[pl.ds((4 * px + 2 * py + pc) * m_per, m_per), :]

        def copy(k, block, to, src=None):
            return _remote(rows(*block) if src is None else src, rows(*block), send_sems.at[k], recv_sems.at[k], to)

        mine = pltpu.make_async_copy(x_ref, rows(*me), local_sem)
        mine.start()
        first = [copy(0, me, sibling, src=x_ref)]
        first += [copy(1 + j, me, (*chip, c), src=x_ref) for j, chip in enumerate(chips)]
        for cp in first:
            cp.start()
        passed = [copy(4 + j, (*chip, c), sibling) for j, chip in enumerate(chips)]
        for j, chip in enumerate(chips):
            copy(1 + j, (*chip, c), me).wait_recv()
            passed[j].start()
        copy(0, sibling, me).wait_recv()
        for j, chip in enumerate(chips):
            copy(4 + j, (*chip, 1 - c), me).wait_recv()
        for cp in first + passed:
            cp.wait_send()
        mine.wait()

    return pl.pallas_call(
        body, name=name, out_shape=jax.ShapeDtypeStruct((N_DEV * m_per, n), v.dtype),
        in_specs=[pl.BlockSpec(memory_space=pltpu.VMEM)], out_specs=pl.BlockSpec(memory_space=pltpu.VMEM),
        scratch_shapes=[pltpu.SemaphoreType.DMA((7,)), pltpu.SemaphoreType.DMA((7,)), pltpu.SemaphoreType.DMA],
        compiler_params=pltpu.CompilerParams(vmem_limit_bytes=VMEM_LIMIT),
    )(v)


def _any_specs(n):
    return [pl.BlockSpec(memory_space=pl.ANY)] * n


def _gather_weights(slots, name):
    n = len(slots)

    def body(*refs):
        outs = refs[n:2 * n]
        send, recv = refs[2 * n:]
        x, y, c, chips = _place()
        s = 2 * x + y
        sib = (x, y, 1 - c)
        started = []
        for t in range(n):
            rh = slots[t].shape[1] // 2
            mine = outs[t].at[s, pl.ds(c * rh, rh)]
            for j, chip in enumerate(chips):
                cp = _remote(mine, mine, send.at[6 * t + j], recv.at[6 * t + j], (*chip, c))
                cp.start()
                started.append(cp)
        for t in range(n):
            rh = slots[t].shape[1] // 2
            half = pl.ds(c * rh, rh)
            for j, chip in enumerate(chips):
                sj = 2 * chip[0] + chip[1]
                blk = outs[t].at[sj, half]
                _remote(blk, blk, send.at[6 * t + j], recv.at[6 * t + j], (*chip, c)).wait_recv()
                cp = _remote(blk, blk, send.at[6 * t + 3 + j], recv.at[6 * t + 3 + j], sib)
                cp.start()
                started.append(cp)
        for t in range(n):
            rh = slots[t].shape[1] // 2
            other = pl.ds((1 - c) * rh, rh)
            for j, chip in enumerate(chips):
                sj = 2 * chip[0] + chip[1]
                blk = outs[t].at[sj, other]
                _remote(blk, blk, send.at[6 * t + 3 + j], recv.at[6 * t + 3 + j], sib).wait_recv()
        for cp in started:
            cp.wait_send()

    return pl.pallas_call(
        body, name=name, out_shape=[jax.ShapeDtypeStruct(w.shape, w.dtype) for w in slots],
        in_specs=_any_specs(n), out_specs=_any_specs(n), input_output_aliases={t: t for t in range(n)},
        scratch_shapes=[pltpu.SemaphoreType.DMA((6 * n,)), pltpu.SemaphoreType.DMA((6 * n,))],
    )(*slots)


def _swap_halves(gs, name):
    n = len(gs)

    def body(*refs):
        srcs, outs = refs[:n], refs[n:2 * n]
        send, recv = refs[2 * n:]
        x, y, c, _ = _place()
        cps = []
        for t in range(n):
            rh = gs[t].shape[1] // 2
            cp = _remote(srcs[t].at[:, pl.ds((1 - c) * rh, rh), :], outs[t], send.at[t], recv.at[t], (x, y, 1 - c))
            cp.start()
            cps.append(cp)
        for cp in cps:
            cp.wait()

    return pl.pallas_call(
        body, name=name,
        out_shape=[jax.ShapeDtypeStruct((g.shape[0], g.shape[1] // 2, g.shape[2]), g.dtype) for g in gs],
        in_specs=_any_specs(n), out_specs=_any_specs(n),
        scratch_shapes=[pltpu.SemaphoreType.DMA((n,)), pltpu.SemaphoreType.DMA((n,))],
    )(*gs)


def _scatter_to_owner(ps, name):
    n = len(ps)

    def body(*refs):
        srcs, outs = refs[:n], refs[n:2 * n]
        send, recv, loc = refs[2 * n:]
        x, y, c, chips = _place()
        s = 2 * x + y
        cps, locs = [], []
        for t in range(n):
            cp = pltpu.make_async_copy(srcs[t].at[s], outs[t].at[s], loc.at[t])
            cp.start()
            locs.append(cp)
            for j, chip in enumerate(chips):
                sj = 2 * chip[0] + chip[1]
                cp = _remote(srcs[t].at[sj], outs[t].at[s], send.at[3 * t + j], recv.at[3 * t + j], (*chip, c))
                cp.start()
                cps.append(cp)
        for t in range(n):
            for j, chip in enumerate(chips):
                sj = 2 * chip[0] + chip[1]
                _remote(outs[t].at[sj], outs[t].at[sj], send.at[3 * t + j], recv.at[3 * t + j], (*chip, c)).wait_recv()
        for cp in cps:
            cp.wait_send()
        for cp in locs:
            cp.wait()

    return pl.pallas_call(
        body, name=name, out_shape=[jax.ShapeDtypeStruct(p.shape, p.dtype) for p in ps],
        in_specs=_any_specs(n), out_specs=_any_specs(n),
        scratch_shapes=[pltpu.SemaphoreType.DMA((3 * n,)), pltpu.SemaphoreType.DMA((3 * n,)), pltpu.SemaphoreType.DMA((n,))],
    )(*ps)


def _join_halves(gs, name):
    n = len(gs)

    def body(*refs):
        outs = refs[n:2 * n]
        send, recv = refs[2 * n:]
        x, y, c, _ = _place()
        cps = []
        for t in range(n):
            rh = gs[t].shape[0] // 2
            mine = outs[t].at[pl.ds(c * rh, rh), :]
            cp = _remote(mine, mine, send.at[t], recv.at[t], (x, y, 1 - c))
            cp.start()
            cps.append(cp)
        for t in range(n):
            rh = gs[t].shape[0] // 2
            theirs = outs[t].at[pl.ds((1 - c) * rh, rh), :]
            _remote(theirs, theirs, send.at[t], recv.at[t], (x, y, 1 - c)).wait_recv()
        for cp in cps:
            cp.wait_send()

    return pl.pallas_call(
        body, name=name, out_shape=[jax.ShapeDtypeStruct(g.shape, g.dtype) for g in gs],
        in_specs=_any_specs(n), out_specs=_any_specs(n), input_output_aliases={t: t for t in range(n)},
        scratch_shapes=[pltpu.SemaphoreType.DMA((n,)), pltpu.SemaphoreType.DMA((n,))],
    )(*gs)


def _row_tile(r, pref=256):
    return _tile(r, pref, 16)


def _cast_bf16(w, layer, chip, name):
    _, R, C = w.shape
    tr = _row_tile(R)

    def body(s_ref, w_ref, o_ref):
        o_ref[...] = w_ref[...].astype(BF16)

    return pl.pallas_call(
        body, name=name,
        grid_spec=pltpu.PrefetchScalarGridSpec(
            num_scalar_prefetch=1, grid=(R // tr,),
            in_specs=[pl.BlockSpec((None, tr, C), lambda i, s: (layer, i, 0))],
            out_specs=pl.BlockSpec((None, tr, C), lambda i, s: (s[0], i, 0))),
        out_shape=jax.ShapeDtypeStruct((N_CHIPS, R, C), BF16), compiler_params=_params(("parallel",)),
    )(jnp.reshape(chip, (1,)).astype(jnp.int32), w)


def _add_sibling(g, got, c, name):
    G, R, C = g.shape
    rh = R // 2
    tr = _row_tile(rh)
    nb = rh // tr

    def body(c_ref, g_ref, o_ref, p_ref):
        p_ref[...] = (g_ref[...].astype(F32) + o_ref[...].astype(F32)).astype(BF16)

    return pl.pallas_call(
        body, name=name,
        grid_spec=pltpu.PrefetchScalarGridSpec(
            num_scalar_prefetch=1, grid=(G, nb),
            in_specs=[pl.BlockSpec((1, tr, C), lambda s, i, c_ref: (s, c_ref[0] * nb + i, 0)),
                      pl.BlockSpec((1, tr, C), lambda s, i, c_ref: (s, i, 0))],
            out_specs=pl.BlockSpec((1, tr, C), lambda s, i, c_ref: (s, i, 0))),
        out_shape=jax.ShapeDtypeStruct((G, rh, C), BF16), compiler_params=_params(("parallel", "parallel")),
    )(jnp.reshape(c, (1,)).astype(jnp.int32), g, got)


def _sum_chips(q, c, name):
    G, rh, C = q.shape
    tr = _row_tile(rh)
    nb = rh // tr

    def body(c_ref, q_ref, o_ref):
        acc = q_ref[0].astype(F32)
        for j in range(1, G):
            acc = acc + q_ref[j].astype(F32)
        o_ref[...] = acc

    return pl.pallas_call(
        body, name=name,
        grid_spec=pltpu.PrefetchScalarGridSpec(
            num_scalar_prefetch=1, grid=(nb,),
            in_specs=[pl.BlockSpec((G, tr, C), lambda i, c_ref: (0, i, 0))],
            out_specs=pl.BlockSpec((tr, C), lambda i, c_ref: (c_ref[0] * nb + i, 0))),
        out_shape=jax.ShapeDtypeStruct((2 * rh, C), F32), compiler_params=_params(("parallel",)),
    )(jnp.reshape(c, (1,)).astype(jnp.int32), q)


def _adam_math(w, g, m, v):
    m = ADAM_B1 * m + (1.0 - ADAM_B1) * g
    v = ADAM_B2 * v + (1.0 - ADAM_B2) * (g * g)
    m_hat = m / (1.0 - ADAM_B1 ** ADAM_STEP)
    v_hat = v / (1.0 - ADAM_B2 ** ADAM_STEP)
    delta = -ADAM_LR * (m_hat / (jnp.sqrt(v_hat) + ADAM_EPS) + ADAM_WD * w)
    return delta, m, v


def _adamw(w, g, m, v, layer, prev, name):
    L, R, C = w.shape
    tr = _tile(R, 128, 8)
    n_prev = len(prev)

    def body(w_ref, g_ref, m_ref, v_ref, *rest):
        go_ref, d_ref, mo_ref, vo_ref = rest[n_prev:]
        gv = g_ref[...]
        go_ref[...] = gv
        d_ref[...], mo_ref[...], vo_ref[...] = _adam_math(w_ref[...], gv, m_ref[...], v_ref[...])

    lay = pl.BlockSpec((None, tr, C), lambda i: (layer, i, 0))
    flat = pl.BlockSpec((tr, C), lambda i: (i, 0))
    return pl.pallas_call(
        body, name=name, grid=(R // tr,), in_specs=[lay, flat, lay, lay] + _any_specs(n_prev), out_specs=[lay] * 4,
        out_shape=[jax.ShapeDtypeStruct((L, R, C), F32)] * 4, input_output_aliases={4 + k: k for k in range(n_prev)},
        compiler_params=_params(("parallel",)),
    )(w, g, m, v, *prev)


def _cond_rows(c_row, cw, name):
    D = c_row.shape[1]
    nr, fc = cw.shape

    def body(c_ref, e_ref, o_ref):
        o_ref[...] = jnp.zeros_like(o_ref)
        cv = c_ref[...]
        o_ref[0:1, 0:D] = cv * (1.0 / (1.0 + jnp.exp(-cv)))
        o_ref[8:8 + nr, 0:fc] = e_ref[...]

    return pl.pallas_call(body, name=name, out_shape=jax.ShapeDtypeStruct((16, max(D, fc)), F32))(c_row, cw)


def _ada_fwd(cact, ada_w, ada_b, layer, chip, name):
    _, D, NC = ada_w.shape
    tn = _tile(NC, 1024)
    nj = NC // tn

    def body(idx_ref, c_ref, w_ref, b_ref, o_ref):
        acc = jnp.dot(c_ref[...].astype(BF16), w_ref[0].astype(BF16), preferred_element_type=F32)
        o_ref[...] = acc + b_ref[pl.ds(idx_ref[0], 1), :]

    return pl.pallas_call(
        body, name=name,
        grid_spec=pltpu.PrefetchScalarGridSpec(
            num_scalar_prefetch=1, grid=(nj,),
            in_specs=[pl.BlockSpec((8, D), lambda j, idx: (0, 0)),
                      pl.BlockSpec((1, D, tn), lambda j, idx: (idx[0], 0, j)),
                      pl.BlockSpec((DEPTH, tn), lambda j, idx: (0, idx[1] * nj + j))],
            out_specs=pl.BlockSpec((8, tn), lambda j, idx: (0, j))),
        out_shape=jax.ShapeDtypeStruct((8, NC), F32), compiler_params=_params(("parallel",)),
    )(jnp.stack([layer, chip]).astype(jnp.int32), cact, ada_w, ada_b)


def _ada_grad_adamw(cact_t, dmod, w, m, v, name):
    L, D, NC = w.shape
    tr = _tile(D, 128, 8)

    def body(c_ref, d_ref, w_ref, m_ref, v_ref, g_ref, dl_ref, mo_ref, vo_ref):
        g = jnp.dot(c_ref[...], d_ref[...], preferred_element_type=F32, precision=HIGHEST)
        g_ref[...] = g
        dl_ref[...], mo_ref[...], vo_ref[...] = _adam_math(w_ref[...], g, m_ref[...], v_ref[...])

    lay = pl.BlockSpec((None, tr, NC), lambda l, i: (l, i, 0))
    return pl.pallas_call(
        body, name=name, grid=(L, D // tr),
        in_specs=[pl.BlockSpec((tr, N_DEV), lambda l, i: (i, 0)), pl.BlockSpec((None, N_DEV, NC), lambda l, i: (l, 0, 0)), lay, lay, lay],
        out_specs=[lay] * 4, out_shape=[jax.ShapeDtypeStruct((L, D, NC), F32)] * 4,
        compiler_params=_params(("parallel", "parallel")),
    )(cact_t, dmod, w, m, v)


def _sum_devices(gathered, name):
    n, R, C = gathered.shape

    def body(g_ref, o_ref):
        acc = g_ref[0]
        for j in range(1, n):
            acc = acc + g_ref[j]
        o_ref[...] = acc

    return pl.pallas_call(body, name=name, out_shape=jax.ShapeDtypeStruct((R, C), F32),
                          compiler_params=pltpu.CompilerParams(vmem_limit_bytes=VMEM_LIMIT))(gathered)


def _adamw_small(w, g, m, v, name):
    def body(w_ref, g_ref, m_ref, v_ref, d_ref, mo_ref, vo_ref):
        d_ref[...], mo_ref[...], vo_ref[...] = _adam_math(w_ref[...], g_ref[...], m_ref[...], v_ref[...])

    return pl.pallas_call(body, name=name, out_shape=[jax.ShapeDtypeStruct(w.shape, F32)] * 3)(w, g, m, v)


def _pad_rows(flat, unit=8 * LANES):
    n = flat.shape[0]
    total = -(-n // unit) * unit
    return jnp.pad(flat, (0, total - n)).reshape(total // LANES, LANES)


def _pad_lanes(v2d):
    return jnp.pad(v2d.reshape(1, -1), ((0, 0), (0, LANES - v2d.size)))


def _step(dm, a):
    S, D, FH, QH, KH, Fh = dm
    ix, iy, ic = lax.axis_index("x"), lax.axis_index("y"), lax.axis_index("c")
    chip = 2 * ix + iy
    dev = 2 * chip + ic
    F2c = a["ffn_w_up"].shape[2]
    NC = a["ada_w"].shape[2]
    PW = 3 * D + LANES
    fox_cols = a["fox_w_in"].shape[2]

    e0 = _cond_rows(a["c"], a["ffn_conv_w"].reshape(DEPTH * 3, F2c), "silu_c")
    g0 = _allgather_small(e0, "gather_cond").reshape(N_DEV, 16, e0.shape[1])
    cact = g0[:, 0, :D]
    conv_w = g0[0::2, 8:8 + DEPTH * 3, :F2c].transpose(1, 0, 2).reshape(DEPTH, 3, N_CHIPS * F2c)
    rows = _ada_fwd(cact, a["ada_w"], a["ada_b"], ic, chip, "ada_proj")
    g1 = _allgather_small(rows, "gather_mod").reshape(N_CHIPS, DEPTH, 8, NC)
    mod = lax.dynamic_index_in_dim(g1, dev, axis=2, keepdims=False).transpose(1, 0, 2).reshape(DEPTH, N_CHIPS * NC)

    names = ["fox_w_in", "fox_w_o", "swa_w_in", "swa_w_o", "ffn_w_up", "ffn_w_up", "ffn_w_down", "ffn_w_down"]
    layers = [0, 0, 0, 0, 0, 1, 0, 1]
    slots = [_cast_bf16(a[nm], l, chip, f"cast_{nm}{l}") for nm, l in zip(names, layers)]
    full = _gather_weights(slots, "gather_weights")
    fox_in = jnp.pad(full[0].transpose(1, 0, 2).reshape(D, N_CHIPS * fox_cols), ((0, 0), (0, PW - N_CHIPS * fox_cols)))
    W = {"fox_in": fox_in[None], "fox_o": full[1].reshape(1, D, D), "swa_in": full[2], "swa_o": full[3].reshape(1, D, D),
         "up": [full[4], full[5]], "down": [full[6].reshape(1, Fh, D), full[7].reshape(1, Fh, D)]}
    sp = {"fox_b_f": _pad_lanes(a["fox_b_f"]), "sinks": jnp.repeat(a["swa_sinks"].reshape(KH, QH // KH), WIN, axis=1)[:, :, None],
          "conv_w": [conv_w[i] for i in range(DEPTH)], "conv_b": [a["ffn_conv_b"][i:i + 1] for i in range(DEPTH)]}
    for nm in ("ln_mix_g", "ln_mix_b", "ln_ffn_g", "ln_ffn_b"):
        sp[nm] = [a[nm][i:i + 1] for i in range(DEPTH)]

    loss_cols, grad_x, gw, gs, dmod = _local_step(dm, a["x"][0], a["loss_target"][0], a["positions"][0], mod, W, sp)
    loss = lax.psum(0.5 / D * jnp.sum(loss_cols), ("x", "y", "c"))

    g_fox_in = gw["fox_in"][0][:, :N_CHIPS * fox_cols].reshape(D, N_CHIPS, fox_cols).transpose(1, 0, 2)
    contrib = [g_fox_in, gw["fox_o"].reshape(N_CHIPS, D // N_CHIPS, D), gw["swa_in"], gw["swa_o"].reshape(N_CHIPS, D // N_CHIPS, D),
               gw["up"][0], gw["up"][1], gw["down"][0].reshape(N_CHIPS, Fh // N_CHIPS, D), gw["down"][1].reshape(N_CHIPS, Fh // N_CHIPS, D)]
    got = _swap_halves(contrib, "grads_to_sibling")
    part = [_add_sibling(g, o, ic, f"add_sibling{t}") for t, (g, o) in enumerate(zip(contrib, got))]
    landed = _scatter_to_owner(part, "grads_to_owner")
    halves = [_sum_chips(q, ic, f"sum_chips{t}") for t, q in enumerate(landed)]
    grads = _join_halves(halves, "grads_join")

    out = {"loss": loss, "grad_x": grad_x[None]}
    upd = {}
    for nm, l, g in zip(names, layers, grads):
        upd[nm] = _adamw(a[nm], g, a["m_" + nm], a["v_" + nm], l, upd.get(nm, ()), f"adamw_{nm}{l}")
    for nm, res in upd.items():
        for pre, arr in zip(("grad_", "delta_", "new_m_", "new_v_"), res):
            out[pre + nm] = arr

    pieces = [dmod.reshape(-1), gs["fox_b_f"].reshape(-1), _pad_lanes(gs["sinks"]).reshape(-1),
              jnp.stack(gs["conv_w"]).reshape(-1), jnp.stack(gs["conv_b"]).reshape(-1)]
    pieces += [jnp.stack(gs[nm]).reshape(-1) for nm in ("ln_mix_g", "ln_mix_b", "ln_ffn_g", "ln_ffn_b")]
    sizes = [p.shape[0] for p in pieces]
    packed = _pad_rows(jnp.concatenate(pieces))
    allp = _allgather_small(packed, "gather_small").reshape(N_DEV, packed.shape[0], LANES)
    tot = _sum_devices(allp, "sum_small").reshape(-1)
    offs = [sum(sizes[:k]) for k in range(len(sizes))]
    take = lambda k: tot[offs[k]:offs[k] + sizes[k]]
    g_small = {"ada_b": take(0).reshape(DEPTH, -1), "fox_b_f": take(1)[:FH].reshape(1, FH), "swa_sinks": take(2)[:QH].reshape(1, QH),
               "ffn_conv_w": lax.dynamic_slice_in_dim(take(3).reshape(DEPTH, 3, N_CHIPS * F2c), chip * F2c, F2c, axis=2),
               "ffn_conv_b": take(4).reshape(DEPTH, -1)}
    for k, nm in enumerate(("ln_mix_g", "ln_mix_b", "ln_ffn_g", "ln_ffn_b")):
        g_small[nm] = take(5 + k).reshape(DEPTH, D)
    small = list(g_small)
    pack = lambda pre: _pad_rows(jnp.concatenate([(a[pre + nm] if pre else a[nm]).reshape(-1) for nm in small]))
    gp = _pad_rows(jnp.concatenate([g_small[nm].reshape(-1) for nm in small]))
    ds_, ms_, vs_ = _adamw_small(pack(""), gp, pack("m_"), pack("v_"), "adamw_small")
    off = 0
    for nm in small:
        n_el = a[nm].size
        out["grad_" + nm] = g_small[nm]
        for pre, arr in (("delta_", ds_), ("new_m_", ms_), ("new_v_", vs_)):
            out[pre + nm] = arr.reshape(-1)[off:off + n_el].reshape(a[nm].shape)
        off += n_el

    dmod_all = allp.reshape(N_DEV, -1)[:, :DEPTH * N_CHIPS * NC].reshape(N_DEV, DEPTH, N_CHIPS * NC)
    dmod_mine = lax.dynamic_slice_in_dim(dmod_all, chip * NC, NC, axis=2).transpose(1, 0, 2)
    ada = _ada_grad_adamw(cact.T, dmod_mine, a["ada_w"], a["m_ada_w"], a["v_ada_w"], "ada_grad")
    for pre, arr in zip(("grad_", "delta_", "new_m_", "new_v_"), ada):
        out[pre + "ada_w"] = arr
    return out


_WEIGHTS = ["fox_w_in", "fox_b_f", "fox_w_o", "swa_w_in", "swa_sinks", "swa_w_o", "ada_w", "ada_b", "ffn_w_up", "ffn_conv_w",
            "ffn_conv_b", "ffn_w_down", "ln_mix_g", "ln_mix_b", "ln_ffn_g", "ln_ffn_b"]
_INPUTS = (["x", "c", "positions"] + _WEIGHTS + ["loss_target"] + ["m_" + w for w in _WEIGHTS] + ["v_" + w for w in _WEIGHTS])


def kernel(x, c, positions, fox_w_in, fox_b_f, fox_w_o, swa_w_in, swa_sinks, swa_w_o, ada_w, ada_b, ffn_w_up, ffn_conv_w, ffn_conv_b, ffn_w_down, ln_mix_g, ln_mix_b, ln_ffn_g, ln_ffn_b, loss_target, m_fox_w_in, m_fox_b_f, m_fox_w_o, m_swa_w_in, m_swa_sinks, m_swa_w_o, m_ada_w, m_ada_b, m_ffn_w_up, m_ffn_conv_w, m_ffn_conv_b, m_ffn_w_down, m_ln_mix_g, m_ln_mix_b, m_ln_ffn_g, m_ln_ffn_b, v_fox_w_in, v_fox_b_f, v_fox_w_o, v_swa_w_in, v_swa_sinks, v_swa_w_o, v_ada_w, v_ada_b, v_ffn_w_up, v_ffn_conv_w, v_ffn_conv_b, v_ffn_w_down, v_ln_mix_g, v_ln_mix_b, v_ln_ffn_g, v_ln_ffn_b):
    args = (x, c, positions, fox_w_in, fox_b_f, fox_w_o, swa_w_in, swa_sinks, swa_w_o, ada_w, ada_b, ffn_w_up, ffn_conv_w, ffn_conv_b, ffn_w_down, ln_mix_g, ln_mix_b, ln_ffn_g, ln_ffn_b, loss_target, m_fox_w_in, m_fox_b_f, m_fox_w_o, m_swa_w_in, m_swa_sinks, m_swa_w_o, m_ada_w, m_ada_b, m_ffn_w_up, m_ffn_conv_w, m_ffn_conv_b, m_ffn_w_down, m_ln_mix_g, m_ln_mix_b, m_ln_ffn_g, m_ln_ffn_b, v_fox_w_in, v_fox_b_f, v_fox_w_o, v_swa_w_in, v_swa_sinks, v_swa_w_o, v_ada_w, v_ada_b, v_ffn_w_up, v_ffn_conv_w, v_ffn_conv_b, v_ffn_w_down, v_ln_mix_g, v_ln_mix_b, v_ln_ffn_g, v_ln_ffn_b)
    out = _step(PROD, dict(zip(_INPUTS, args)))
    order = ["loss", "grad_x"] + [p + w for p in ("grad_", "delta_", "new_m_", "new_v_") for w in _WEIGHTS]
    return tuple(out[k] for k in order)
```

```python
import functools
from typing import NamedTuple

import jax
import jax.numpy as jnp
from jax import lax
from jax.experimental import pallas as pl
from jax.experimental.pallas import tpu as pltpu

F32 = jnp.float32
BF16 = jnp.bfloat16
MESH = pl.DeviceIdType.MESH
HIGHEST = lax.Precision.HIGHEST

N_CHIPS = 4
N_DEV = 8
LANES = 128
VMEM_LIMIT = 56 * 1024 * 1024

DEPTH = 2
DEEPNORM_ALPHA = (2.0 * DEPTH) ** 0.25
LN_EPS = 1e-5
ROPE_THETA = 500000.0
ADAM_LR, ADAM_B1, ADAM_B2, ADAM_EPS, ADAM_WD, ADAM_STEP = 0.001, 0.9, 0.999, 1e-08, 0.01, 10
NEG = -1e30


class Dims(NamedTuple):
    S: int
    D: int
    FH: int
    QH: int
    KH: int
    F: int


PROD = Dims(S=2048, D=2048, FH=16, QH=32, KH=4, F=5632)
FDH = 128
SDH = 64
WIN = 128
ROPE_DIM = 16
FOX_TQ = 256


def _params(sem=None, vmem=VMEM_LIMIT):
    return pltpu.CompilerParams(dimension_semantics=sem, vmem_limit_bytes=vmem)


def _tile(n, pref, unit=LANES):
    if n <= pref:
        return n
    t = (pref // unit) * unit
    while t > 0:
        if n % t == 0:
            return t
        t -= unit
    return n


class Comm:
    def __init__(self, args, out_shapes, aliases, n_sem, start, finish):
        self.args, self.out_shapes, self.aliases, self.n_sem = list(args), list(out_shapes), dict(aliases), n_sem
        self.start, self.finish = start, finish
        self.results = None


def _place():
    x, y, c = lax.axis_index("x"), lax.axis_index("y"), lax.axis_index("c")
    chips = [(1 - x, y), (x, 1 - y), (1 - x, 1 - y)]
    return x, y, c, chips


def _remote(src, dst, send, recv, to):
    return pltpu.make_async_remote_copy(src_ref=src, dst_ref=dst, send_sem=send, recv_sem=recv, device_id=to, device_id_type=MESH)


def _any_specs(n):
    return [pl.BlockSpec(memory_space=pl.ANY)] * n


def _call(body, *, name, grid, in_specs, out_specs, out_shape, args, sem, scratch_shapes=(), comm=None):
    in_specs, out_specs, out_shape, scratch_shapes = list(in_specs), list(out_specs), list(out_shape), list(scratch_shapes)
    if comm is None:
        return pl.pallas_call(body, name=name, grid=grid, in_specs=in_specs, out_specs=out_specs, out_shape=out_shape,
                              scratch_shapes=scratch_shapes, compiler_params=_params(sem))(*args)
    n_in, n_out, nc_in, nc_out, n_scr = len(in_specs), len(out_specs), len(comm.args), len(comm.out_shapes), len(scratch_shapes)

    def wrapped(*refs):
        ins, refs = refs[:n_in], refs[n_in:]
        cin, refs = refs[:nc_in], refs[nc_in:]
        outs, refs = refs[:n_out], refs[n_out:]
        cout, refs = refs[:nc_out], refs[nc_out:]
        scratch, (send, recv) = refs[:n_scr], refs[n_scr:]
        ids = [pl.program_id(k) for k in range(len(grid))]
        first = functools.reduce(jnp.logical_and, [i == 0 for i in ids])
        last = functools.reduce(jnp.logical_and, [i == g - 1 for i, g in zip(ids, grid)])

        @pl.when(first)
        def _():
            comm.start(cin, cout, send, recv)

        body(*ins, *outs, *scratch)

        @pl.when(last)
        def _():
            comm.finish(cin, cout, send, recv)

    res = pl.pallas_call(
        wrapped, name=name, grid=grid, in_specs=in_specs + _any_specs(nc_in), out_specs=out_specs + _any_specs(nc_out),
        out_shape=out_shape + comm.out_shapes,
        scratch_shapes=scratch_shapes + [pltpu.SemaphoreType.DMA((comm.n_sem,)), pltpu.SemaphoreType.DMA((comm.n_sem,))],
        input_output_aliases={n_in + a: n_out + o for a, o in comm.aliases.items()},
        compiler_params=_params(("arbitrary",) * len(grid)),
    )(*args, *comm.args)
    comm.results = list(res[n_out:])
    return list(res[:n_out])


def _run_comm(comm, name):
    nc_in, nc_out = len(comm.args), len(comm.out_shapes)

    def body(*refs):
        cin, cout, (send, recv) = refs[:nc_in], refs[nc_in:nc_in + nc_out], refs[nc_in + nc_out:]
        comm.start(cin, cout, send, recv)
        comm.finish(cin, cout, send, recv)

    res = pl.pallas_call(
        body, name=name, in_specs=_any_specs(nc_in), out_specs=_any_specs(nc_out), out_shape=comm.out_shapes,
        scratch_shapes=[pltpu.SemaphoreType.DMA((comm.n_sem,)), pltpu.SemaphoreType.DMA((comm.n_sem,))],
        input_output_aliases=comm.aliases,
    )(*comm.args)
    comm.results = list(res)


def _gather_comm(slots, chunks):
    n = len(slots)

    def rows(t, who):
        rh = slots[t].shape[1] // 2
        k, nch = chunks[t]
        rc = rh // nch
        return pl.ds(who * rh + k * rc, rc)

    def start(args, outs, send, recv):
        x, y, c, chips = _place()
        s = 2 * x + y
        for t in range(n):
            mine = outs[t].at[s, rows(t, c)]
            for j, chip in enumerate(chips):
                _remote(mine, mine, send.at[6 * t + j], recv.at[6 * t + j], (*chip, c)).start()

    def finish(args, outs, send, recv):
        x, y, c, chips = _place()
        s = 2 * x + y
        sib = (x, y, 1 - c)
        for t in range(n):
            for j, chip in enumerate(chips):
                blk = outs[t].at[2 * chip[0] + chip[1], rows(t, c)]
                _remote(blk, blk, send.at[6 * t + j], recv.at[6 * t + j], (*chip, c)).wait_recv()
                _remote(blk, blk, send.at[6 * t + 3 + j], recv.at[6 * t + 3 + j], sib).start()
        for t in range(n):
            for j, chip in enumerate(chips):
                blk = outs[t].at[2 * chip[0] + chip[1], rows(t, 1 - c)]
                _remote(blk, blk, send.at[6 * t + 3 + j], recv.at[6 * t + 3 + j], sib).wait_recv()
        for t in range(n):
            mine = outs[t].at[s, rows(t, c)]
            for j, chip in enumerate(chips):
                _remote(mine, mine, send.at[6 * t + j], recv.at[6 * t + j], (*chip, c)).wait_send()
                blk = outs[t].at[2 * chip[0] + chip[1], rows(t, c)]
                _remote(blk, blk, send.at[6 * t + 3 + j], recv.at[6 * t + 3 + j], sib).wait_send()

    shapes = [jax.ShapeDtypeStruct(w.shape, w.dtype) for w in slots]
    return Comm(slots, shapes, {t: t for t in range(n)}, 6 * n, start, finish)


def _scatter_comm(parts, landed, chunks):
    n = len(parts)
    prev = [t for t in range(n) if landed[t] is not None]

    def rows(t):
        k, nch = chunks[t]
        rc = parts[t].shape[1] // nch
        return pl.ds(k * rc, rc)

    def start(args, outs, send, recv):
        x, y, c, chips = _place()
        s = 2 * x + y
        for t in range(n):
            for j, chip in enumerate(chips):
                _remote(args[t].at[2 * chip[0] + chip[1], rows(t)], outs[t].at[s, rows(t)],
                        send.at[3 * t + j], recv.at[3 * t + j], (*chip, c)).start()

    def finish(args, outs, send, recv):
        x, y, c, chips = _place()
        for t in range(n):
            for j, chip in enumerate(chips):
                blk = outs[t].at[2 * chip[0] + chip[1], rows(t)]
                _remote(blk, blk, send.at[3 * t + j], recv.at[3 * t + j], (*chip, c)).wait_recv()
        for t in range(n):
            for j, chip in enumerate(chips):
                src = args[t].at[2 * chip[0] + chip[1], rows(t)]
                _remote(src, src, send.at[3 * t + j], recv.at[3 * t + j], (*chip, c)).wait_send()

    shapes = [jax.ShapeDtypeStruct(p.shape, p.dtype) for p in parts]
    return Comm(list(parts) + [landed[t] for t in prev], shapes, {n + i: t for i, t in enumerate(prev)}, 3 * n, start, finish)


_DN = {"nn": (((1,), (0,)), ((), ())), "nt": (((1,), (1,)), ((), ())), "tn": (((0,), (0,)), ((), ()))}


def _mm(a, b, *, mode, out_dtype, name, out_groups=1, tm=1024, tn=1024, tk=2048, comm=None):
    ga, ra, ca = a.shape
    gb, rb, cb = b.shape
    if mode == "nn":
        M, K, N = ra, ga * ca, gb * cb
        assert rb == K and ga == 1 or (rb == K)
    elif mode == "nt":
        M, K, N = ra, ga * ca, rb
        assert gb * cb == K
    else:
        K, M, N = ra, ga * ca, gb * cb
        assert rb == K
    go = out_groups
    if mode == "nn":
        tk = _tile(ca, tk); assert rb % tk == 0 and (ga == 1 or True)
        tn = _tile(min(cb, N // go), tn); tm = _tile(M, tm, 8)
    elif mode == "nt":
        tk = _tile(ca, tk); tk = _tile(cb, tk) if cb % tk else tk; assert ca % tk == 0 and cb % tk == 0
        tn = _tile(N // go, tn); tm = _tile(M, tm, 8)
    else:
        tk = _tile(K, tk, 8); tm = _tile(ca, tm); tn = _tile(min(cb, N // go), tn)
    assert (N // go) % tn == 0 and M % tm == 0 and K % tk == 0, (name, M, N, K, tm, tn, tk)
    nk = K // tk
    kpa = max(ca // tk, 1)
    kpb = max(cb // tk, 1)
    npb = max(cb // tn, 1)
    npo = (N // go) // tn
    mpa = max(ca // tm, 1)

    if mode == "nn":
        a_spec = pl.BlockSpec((1, tm, tk), lambda j, i, k: (k // kpa, i, k % kpa))
        b_spec = pl.BlockSpec((1, tk, tn), lambda j, i, k: (j // npb, k, j % npb))
    elif mode == "nt":
        a_spec = pl.BlockSpec((1, tm, tk), lambda j, i, k: (k // kpa, i, k % kpa))
        b_spec = pl.BlockSpec((1, tn, tk), lambda j, i, k: (k // kpb, j, k % kpb))
    else:
        a_spec = pl.BlockSpec((1, tk, tm), lambda j, i, k: (i // mpa, k, i % mpa))
        b_spec = pl.BlockSpec((1, tk, tn), lambda j, i, k: (j // npb, k, j % npb))
    o_spec = pl.BlockSpec((1, tm, tn), lambda j, i, k: (j // npo, i, j % npo))
    dn = _DN[mode]

    def body(a_ref, b_ref, o_ref, *acc):
        p = lax.dot_general(a_ref[0], b_ref[0], dn, preferred_element_type=F32)
        if nk == 1:
            o_ref[0] = p.astype(out_dtype)
        else:
            k = pl.program_id(2)

            @pl.when(k == 0)
            def _():
                acc[0][...] = p

            @pl.when(k > 0)
            def _():
                acc[0][...] += p

            @pl.when(k == nk - 1)
            def _():
                o_ref[0] = acc[0][...].astype(out_dtype)

    return _call(
        body, name=name, grid=(N // tn, M // tm, nk), in_specs=[a_spec, b_spec], out_specs=[o_spec],
        out_shape=[jax.ShapeDtypeStruct((go, M, N // go), out_dtype)],
        scratch_shapes=[pltpu.VMEM((tm, tn), F32)] if nk > 1 else [],
        sem=("parallel", "parallel", "arbitrary"), args=(a, b), comm=comm)[0]


def _rows(tr, d):
    return pl.BlockSpec((tr, d), lambda i: (i, 0))


def _vec(d):
    return pl.BlockSpec((1, d), lambda i: (0, 0))


def _modulate(x, sc, sh, name):
    S, D = x.shape
    tr = min(256, S)

    def body(x_ref, sc_ref, sh_ref, h_ref):
        h_ref[...] = (x_ref[...] * (1.0 + sc_ref[...]) + sh_ref[...]).astype(BF16)

    return pl.pallas_call(
        body, name=name, grid=(S // tr,), in_specs=[_rows(tr, D), _vec(D), _vec(D)], out_specs=_rows(tr, D),
        out_shape=jax.ShapeDtypeStruct((S, D), BF16), compiler_params=_params(("parallel",)),
    )(x, sc, sh)


def _ln_fwd(x, y, gate, gamma, beta, sc, sh, name):
    S, D = x.shape
    tr = min(256, S)
    emit_h = sc is not None

    def body(*refs):
        if emit_h:
            x_ref, y_ref, g_ref, ga_ref, be_ref, sc_ref, sh_ref, xo_ref, xh_ref, rs_ref, h_ref = refs
        else:
            x_ref, y_ref, g_ref, ga_ref, be_ref, xo_ref, xh_ref, rs_ref = refs
        z = DEEPNORM_ALPHA * x_ref[...] + (1.0 + g_ref[...]) * y_ref[...]
        mu = jnp.mean(z, axis=-1, keepdims=True)
        zc = z - mu
        var = jnp.mean(zc * zc, axis=-1, keepdims=True)
        rstd = lax.rsqrt(var + LN_EPS)
        xh = zc * rstd
        xo = xh * ga_ref[...] + be_ref[...]
        xo_ref[...] = xo
        xh_ref[...] = xh
        rs_ref[...] = rstd
        if emit_h:
            h_ref[...] = (xo * (1.0 + sc_ref[...]) + sh_ref[...]).astype(BF16)

    ins = [x, y, gate, gamma, beta] + ([sc, sh] if emit_h else [])
    in_specs = [_rows(tr, D), _rows(tr, D)] + [_vec(D)] * (len(ins) - 2)
    out_shape = [jax.ShapeDtypeStruct((S, D), F32), jax.ShapeDtypeStruct((S, D), F32), jax.ShapeDtypeStruct((S, 1), F32)]
    out_specs = [_rows(tr, D), _rows(tr, D), _rows(tr, 1)]
    if emit_h:
        out_shape.append(jax.ShapeDtypeStruct((S, D), BF16))
        out_specs.append(_rows(tr, D))
    return pl.pallas_call(
        body, name=name, grid=(S // tr,), in_specs=in_specs, out_specs=out_specs, out_shape=out_shape,
        compiler_params=_params(("parallel",)),
    )(*ins)


def _loss_head(xf, tgt, name):
    S, D = xf.shape
    tr = min(256, S)

    def body(x_ref, t_ref, dx_ref, l_ref):
        e = x_ref[...] - t_ref[...]
        dx_ref[...] = e * (1.0 / D)

        @pl.when(pl.program_id(0) == 0)
        def _():
            l_ref[...] = jnp.zeros_like(l_ref)

        l_ref[...] += jnp.sum(e * e, axis=0, keepdims=True)

    return pl.pallas_call(
        body, name=name, grid=(S // tr,), in_specs=[_rows(tr, D), _rows(tr, D)],
        out_specs=[_rows(tr, D), _vec(D)],
        out_shape=[jax.ShapeDtypeStruct((S, D), F32), jax.ShapeDtypeStruct((1, D), F32)],
        compiler_params=_params(("arbitrary",)),
    )(xf, tgt)


def _ln_bwd(dxo, xh, rstd, gamma, y, gate, name):
    S, D = dxo.shape
    tr = min(256, S)

    def body(dx_ref, xh_ref, rs_ref, ga_ref, y_ref, g_ref, dres_ref, dy_ref, dga_ref, dbe_ref, dg_ref):
        dxo_ = dx_ref[...]
        xh_ = xh_ref[...]
        dxh = dxo_ * ga_ref[...]
        m1 = jnp.mean(dxh, axis=-1, keepdims=True)
        m2 = jnp.mean(dxh * xh_, axis=-1, keepdims=True)
        dz = rs_ref[...] * (dxh - m1 - xh_ * m2)
        dres_ref[...] = DEEPNORM_ALPHA * dz
        dy_ref[...] = ((1.0 + g_ref[...]) * dz).astype(BF16)

        @pl.when(pl.program_id(0) == 0)
        def _():
            dga_ref[...] = jnp.zeros_like(dga_ref)
            dbe_ref[...] = jnp.zeros_like(dbe_ref)
            dg_ref[...] = jnp.zeros_like(dg_ref)

        dga_ref[...] += jnp.sum(dxo_ * xh_, axis=0, keepdims=True)
        dbe_ref[...] += jnp.sum(dxo_, axis=0, keepdims=True)
        dg_ref[...] += jnp.sum(dz * y_ref[...], axis=0, keepdims=True)

    return pl.pallas_call(
        body, name=name, grid=(S // tr,),
        in_specs=[_rows(tr, D), _rows(tr, D), _rows(tr, 1), _vec(D), _rows(tr, D), _vec(D)],
        out_specs=[_rows(tr, D), _rows(tr, D), _vec(D), _vec(D), _vec(D)],
        out_shape=[jax.ShapeDtypeStruct((S, D), F32), jax.ShapeDtypeStruct((S, D), BF16)] + [jax.ShapeDtypeStruct((1, D), F32)] * 3,
        compiler_params=_params(("arbitrary",)),
    )(dxo, xh, rstd, gamma, y, gate)


def _mod_bwd(dh, x, sc, dres, name):
    S, D = x.shape
    tr = min(256, S)

    def body(dh_ref, x_ref, sc_ref, dr_ref, dx_ref, dsc_ref, dsh_ref):
        dh_ = dh_ref[...]
        dx_ref[...] = dr_ref[...] + dh_ * (1.0 + sc_ref[...])

        @pl.when(pl.program_id(0) == 0)
        def _():
            dsc_ref[...] = jnp.zeros_like(dsc_ref)
            dsh_ref[...] = jnp.zeros_like(dsh_ref)

        dsc_ref[...] += jnp.sum(dh_ * x_ref[...], axis=0, keepdims=True)
        dsh_ref[...] += jnp.sum(dh_, axis=0, keepdims=True)

    return pl.pallas_call(
        body, name=name, grid=(S // tr,),
        in_specs=[_rows(tr, D), _rows(tr, D), _vec(D), _rows(tr, D)],
        out_specs=[_rows(tr, D), _vec(D), _vec(D)],
        out_shape=[jax.ShapeDtypeStruct((S, D), F32), jax.ShapeDtypeStruct((1, D), F32), jax.ShapeDtypeStruct((1, D), F32)],
        compiler_params=_params(("arbitrary",)),
    )(dh, x, sc, dres)


def _log_sigmoid(z):
    return jnp.minimum(z, 0.0) - jnp.log(1.0 + jnp.exp(-jnp.abs(z)))


def _fox_gate_fwd(proj, b_f, n_heads, name):
    S, PW = proj.shape
    blk = min(256, S)
    last = PW // LANES - 1

    def body(fl_ref, b_ref, cum_ref):
        r = lax.broadcasted_iota(jnp.int32, (blk, blk), 0)
        c = lax.broadcasted_iota(jnp.int32, (blk, blk), 1)
        tril = (c <= r).astype(F32)
        carry = jnp.zeros((1, LANES), F32)
        for i in range(S // blk):
            lf = _log_sigmoid(fl_ref[i * blk:(i + 1) * blk, :] + b_ref[...])
            cum_ref[i * blk:(i + 1) * blk, :] = jnp.dot(tril, lf, preferred_element_type=F32, precision=HIGHEST) + carry
            carry = carry + jnp.sum(lf, axis=0, keepdims=True)

    return pl.pallas_call(
        body, name=name, grid=(1,),
        in_specs=[pl.BlockSpec((S, LANES), lambda i: (0, last)), pl.BlockSpec((1, LANES), lambda i: (0, 0))],
        out_specs=pl.BlockSpec((S, LANES), lambda i: (0, 0)),
        out_shape=jax.ShapeDtypeStruct((S, LANES), F32), compiler_params=_params(("arbitrary",)),
    )(proj, b_f)


def _fox_gate_bwd(dcum, proj, b_f, n_heads, name):
    S, PW = proj.shape
    blk = min(256, S)
    last = PW // LANES - 1
    nb = S // blk

    def body(dc_ref, fl_ref, b_ref, dfl_ref, db_ref):
        r = lax.broadcasted_iota(jnp.int32, (blk, blk), 0)
        c = lax.broadcasted_iota(jnp.int32, (blk, blk), 1)
        triu = (c >= r).astype(F32)
        lane = lax.broadcasted_iota(jnp.int32, (blk, LANES), 1)
        carry = jnp.zeros((1, LANES), F32)
        dbs = jnp.zeros((1, LANES), F32)
        for i in reversed(range(nb)):
            dc = dc_ref[i * blk:(i + 1) * blk, :]
            dlf = jnp.dot(triu, dc, preferred_element_type=F32, precision=HIGHEST) + carry
            carry = carry + jnp.sum(dc, axis=0, keepdims=True)
            z = fl_ref[i * blk:(i + 1) * blk, :] + b_ref[...]
            e = jnp.exp(-jnp.abs(z))
            sig_neg = jnp.where(z >= 0, e / (1.0 + e), 1.0 / (1.0 + e))
            dfl = jnp.where(lane < n_heads, dlf * sig_neg, 0.0)
            dfl_ref[i * blk:(i + 1) * blk, :] = dfl.astype(BF16)
            dbs = dbs + jnp.sum(dfl, axis=0, keepdims=True)
        db_ref[...] = dbs

    return pl.pallas_call(
        body, name=name, grid=(1,),
        in_specs=[pl.BlockSpec((S, LANES), lambda i: (0, 0)), pl.BlockSpec((S, LANES), lambda i: (0, last)),
                  pl.BlockSpec((1, LANES), lambda i: (0, 0))],
        out_specs=[pl.BlockSpec((S, LANES), lambda i: (0, 0)), pl.BlockSpec((1, LANES), lambda i: (0, 0))],
        out_shape=[jax.ShapeDtypeStruct((S, LANES), BF16), jax.ShapeDtypeStruct((1, LANES), F32)],
        compiler_params=_params(("arbitrary",)),
    )(dcum, proj, b_f)


def _fox_scores(q_ref, kb_ref, cq_ref, ck_ref, qi, tq, scale):
    kk = (qi + 1) * tq
    rows = slice(qi * tq, (qi + 1) * tq)
    qb = q_ref[rows, :].astype(BF16)
    s = lax.dot_general(qb, kb_ref[0:kk, :], _DN["nt"], preferred_element_type=F32) * scale
    s = s + (cq_ref[0, rows, :] - ck_ref[0, :, 0:kk])
    r = lax.broadcasted_iota(jnp.int32, (tq, kk), 0) + qi * tq
    c = lax.broadcasted_iota(jnp.int32, (tq, kk), 1)
    mask = c <= r
    return jnp.where(mask, s, NEG), mask, qb


def _fox_fwd(proj, cq, ck, n_heads, name, comm=None):
    S = proj.shape[0]
    H = n_heads
    tq = min(FOX_TQ, S)
    nq = S // tq
    scale = FDH ** -0.5

    def body(q_ref, k_ref, v_ref, cq_ref, ck_ref, o_ref, lse_ref, kb_ref, vb_ref):
        kb_ref[...] = k_ref[...].astype(BF16)
        vb_ref[...] = v_ref[...].astype(BF16)
        for qi in range(nq):
            kk = (qi + 1) * tq
            rows = slice(qi * tq, (qi + 1) * tq)
            s, _, _ = _fox_scores(q_ref, kb_ref, cq_ref, ck_ref, qi, tq, scale)
            m = jnp.max(s, axis=-1, keepdims=True)
            p = jnp.exp(s - m)
            l = jnp.sum(p, axis=-1, keepdims=True)
            p = p * (1.0 / l)
            o_ref[rows, :] = jnp.dot(p.astype(BF16), vb_ref[0:kk, :], preferred_element_type=F32).astype(BF16)
            lse_ref[0, rows, :] = m + jnp.log(l)

    col = lambda off: pl.BlockSpec((S, FDH), lambda h: (0, h + off))
    stat_c = pl.BlockSpec((1, S, 1), lambda h: (h, 0, 0))
    stat_r = pl.BlockSpec((1, 1, S), lambda h: (h, 0, 0))
    return _call(
        body, name=name, grid=(H,),
        in_specs=[col(0), col(H), col(2 * H), stat_c, stat_r],
        out_specs=[col(0), stat_c],
        out_shape=[jax.ShapeDtypeStruct((S, H * FDH), BF16), jax.ShapeDtypeStruct((H, S, 1), F32)],
        scratch_shapes=[pltpu.VMEM((S, FDH), BF16), pltpu.VMEM((S, FDH), BF16)],
        sem=("parallel",), args=(proj, proj, proj, cq, ck), comm=comm)


def _fox_bwd(proj, cq, ck, lse, do, n_heads, name, comm=None):
    S = proj.shape[0]
    H = n_heads
    tq = min(FOX_TQ, S)
    nq = S // tq
    scale = FDH ** -0.5

    def body(q_ref, k_ref, v_ref, cq_ref, ck_ref, lse_ref, do_ref, dq_ref, dk_ref, dv_ref, dcq_ref, dck_ref,
             kb_ref, vb_ref, dka_ref, dva_ref):
        kb_ref[...] = k_ref[...].astype(BF16)
        vb_ref[...] = v_ref[...].astype(BF16)
        dka_ref[...] = jnp.zeros_like(dka_ref)
        dva_ref[...] = jnp.zeros_like(dva_ref)
        dck_ref[...] = jnp.zeros_like(dck_ref)
        for qi in range(nq):
            kk = (qi + 1) * tq
            rows = slice(qi * tq, (qi + 1) * tq)
            s, mask, qb = _fox_scores(q_ref, kb_ref, cq_ref, ck_ref, qi, tq, scale)
            p = jnp.where(mask, jnp.exp(s - lse_ref[0, rows, :]), 0.0)
            dob = do_ref[rows, :]
            dp = lax.dot_general(dob, vb_ref[0:kk, :], _DN["nt"], preferred_element_type=F32)
            delta = jnp.sum(p * dp, axis=-1, keepdims=True)
            ds = p * (dp - delta)
            dcq_ref[0, rows, :] = jnp.sum(ds, axis=-1, keepdims=True)
            dck_ref[0, :, 0:kk] -= jnp.sum(ds, axis=0, keepdims=True)
            dsb = (ds * scale).astype(BF16)
            dq_ref[rows, :] = jnp.dot(dsb, kb_ref[0:kk, :], preferred_element_type=F32).astype(BF16)
            dka_ref[0:kk, :] += lax.dot_general(dsb, qb, _DN["tn"], preferred_element_type=F32)
            dva_ref[0:kk, :] += lax.dot_general(p.astype(BF16), dob, _DN["tn"], preferred_element_type=F32)
        dk_ref[...] = dka_ref[...].astype(BF16)
        dv_ref[...] = dva_ref[...].astype(BF16)

    col = lambda off: pl.BlockSpec((S, FDH), lambda h: (0, h + off))
    stat_c = pl.BlockSpec((1, S, 1), lambda h: (h, 0, 0))
    stat_r = pl.BlockSpec((1, 1, S), lambda h: (h, 0, 0))
    wide = jax.ShapeDtypeStruct((S, H * FDH), BF16)
    return _call(
        body, name=name, grid=(H,),
        in_specs=[col(0), col(H), col(2 * H), stat_c, stat_r, stat_c, col(0)],
        out_specs=[col(0), col(0), col(0), stat_c, stat_r],
        out_shape=[wide, wide, wide, jax.ShapeDtypeStruct((H, S, 1), F32), jax.ShapeDtypeStruct((H, 1, S), F32)],
        scratch_shapes=[pltpu.VMEM((S, FDH), BF16), pltpu.VMEM((S, FDH), BF16), pltpu.VMEM((S, FDH), F32), pltpu.VMEM((S, FDH), F32)],
        sem=("parallel",), args=(proj, proj, proj, cq, ck, lse, do), comm=comm)


def _rope_tables(pos, sign):
    inv = ROPE_THETA ** (-jnp.arange(0, ROPE_DIM, 2, dtype=F32) / ROPE_DIM)
    ang = pos.astype(F32)[:, None] * inv
    cos, sin = jnp.cos(ang), sign * jnp.sin(ang)
    l64 = jnp.arange(LANES) % SDH
    idx = l64 % (ROPE_DIM // 2)
    c = jnp.where(l64 < ROPE_DIM, cos[:, idx], 1.0)
    sa = jnp.where(l64 < ROPE_DIM // 2, -sin[:, idx], 0.0)
    sb = jnp.where((l64 >= ROPE_DIM // 2) & (l64 < ROPE_DIM), sin[:, idx], 0.0)
    rot = jnp.stack([c, sa, sb])
    ident = jnp.stack([jnp.ones_like(c), jnp.zeros_like(c), jnp.zeros_like(c)])
    return jnp.stack([rot, ident]).astype(F32)


def _rope(xin, tabs, n_rot, out_dtype, name):
    S, W = xin.shape

    def body(x_ref, t_ref, o_ref):
        xv = x_ref[...]
        o = xv * t_ref[0, 0] + pltpu.roll(xv, LANES - ROPE_DIM // 2, 1) * t_ref[0, 1] + pltpu.roll(xv, ROPE_DIM // 2, 1) * t_ref[0, 2]
        o_ref[...] = o.astype(out_dtype)

    return pl.pallas_call(
        body, name=name, grid=(W // LANES,),
        in_specs=[pl.BlockSpec((S, LANES), lambda j: (0, j)),
                  pl.BlockSpec((1, 3, S, LANES), lambda j: (jnp.where(j < n_rot, 0, 1), 0, 0, 0))],
        out_specs=pl.BlockSpec((S, LANES), lambda j: (0, j)),
        out_shape=jax.ShapeDtypeStruct((S, W), out_dtype), compiler_params=_params(("parallel",)),
    )(xin, tabs)


def _swa_probs(q_ref, k_ref, sk_ref, n, scale):
    st = pl.multiple_of(jnp.maximum(n - 1, 0) * WIN, WIN)
    qb = q_ref[0, 0]
    kb = k_ref[0, pl.ds(st, 2 * WIN), :]
    gm = qb.shape[0]
    s = lax.dot_general(qb, kb, _DN["nt"], preferred_element_type=F32) * scale
    qa = n * WIN + (lax.broadcasted_iota(jnp.int32, (gm, 2 * WIN), 0) & (WIN - 1))
    ka = st + lax.broadcasted_iota(jnp.int32, (gm, 2 * WIN), 1)
    valid = (ka <= qa) & (qa - ka < WIN)
    s = jnp.where(valid, s, NEG)
    sink = sk_ref[0]
    m = jnp.maximum(jnp.max(s, axis=-1, keepdims=True), sink)
    e = jnp.where(valid, jnp.exp(s - m), 0.0)
    es = jnp.exp(sink - m)
    inv = 1.0 / (jnp.sum(e, axis=-1, keepdims=True) + es)
    return e * inv, es * inv, st, qb, kb


def _swa_specs(S, gm):
    blk = pl.BlockSpec((1, 1, gm, SDH), lambda g, n: (g, n, 0, 0))
    kv = pl.BlockSpec((1, S, SDH), lambda g, n: (g, 0, 0))
    col = pl.BlockSpec((1, gm, 1), lambda g, n: (g, 0, 0))
    return blk, kv, col


def _swa_fwd(q, k, v, sinks, name, comm=None):
    KH, nb, gm, _ = q.shape
    S = k.shape[1]
    scale = SDH ** -0.5

    def body(q_ref, k_ref, v_ref, sk_ref, o_ref):
        p, _, st, _, _ = _swa_probs(q_ref, k_ref, sk_ref, pl.program_id(1), scale)
        vb = v_ref[0, pl.ds(st, 2 * WIN), :]
        o_ref[0, 0] = jnp.dot(p.astype(BF16), vb, preferred_element_type=F32).astype(BF16)

    blk, kv, col = _swa_specs(S, gm)
    return _call(
        body, name=name, grid=(KH, nb), in_specs=[blk, kv, kv, col], out_specs=[blk],
        out_shape=[jax.ShapeDtypeStruct(q.shape, BF16)], sem=("parallel", "parallel"), args=(q, k, v, sinks), comm=comm)[0]


def _swa_bwd(q, k, v, sinks, do, name, comm=None):
    KH, nb, gm, _ = q.shape
    S = k.shape[1]
    scale = SDH ** -0.5

    def body(q_ref, k_ref, v_ref, sk_ref, do_ref, dq_ref, dk_ref, dv_ref, dsk_ref):
        n = pl.program_id(1)

        @pl.when(n == 0)
        def _():
            dk_ref[...] = jnp.zeros_like(dk_ref)
            dv_ref[...] = jnp.zeros_like(dv_ref)
            dsk_ref[...] = jnp.zeros_like(dsk_ref)

        p, ps, st, qb, kb = _swa_probs(q_ref, k_ref, sk_ref, n, scale)
        vb = v_ref[0, pl.ds(st, 2 * WIN), :]
        dob = do_ref[0, 0]
        dp = lax.dot_general(dob, vb, _DN["nt"], preferred_element_type=F32)
        delta = jnp.sum(p * dp, axis=-1, keepdims=True)
        ds = p * (dp - delta)
        dsb = (ds * scale).astype(BF16)
        dq_ref[0, 0] = jnp.dot(dsb, kb, preferred_element_type=F32)
        dk_ref[0, pl.ds(st, 2 * WIN), :] += lax.dot_general(dsb, qb, _DN["tn"], preferred_element_type=F32)
        dv_ref[0, pl.ds(st, 2 * WIN), :] += lax.dot_general(p.astype(BF16), dob, _DN["tn"], preferred_element_type=F32)
        dsk_ref[0] -= ps * delta

    blk, kv, col = _swa_specs(S, gm)
    return _call(
        body, name=name, grid=(KH, nb), in_specs=[blk, kv, kv, col, blk], out_specs=[blk, kv, kv, col],
        out_shape=[jax.ShapeDtypeStruct(q.shape, F32), jax.ShapeDtypeStruct(k.shape, F32),
                   jax.ShapeDtypeStruct(k.shape, F32), jax.ShapeDtypeStruct(sinks.shape, F32)],
        sem=("parallel", "arbitrary"), args=(q, k, v, sinks, do), comm=comm)


def _shift_down(u, k):
    row = lax.broadcasted_iota(jnp.int32, u.shape, 0)
    return jnp.where(row >= k, pltpu.roll(u, k, 0), 0.0)


def _shift_up(u, k):
    n = u.shape[0]
    row = lax.broadcasted_iota(jnp.int32, u.shape, 0)
    return jnp.where(row < n - k, pltpu.roll(u, n - k, 0), 0.0)


def _conv3(u, w_ref, b_ref):
    return w_ref[0:1, :] * _shift_down(u, 2) + w_ref[1:2, :] * _shift_down(u, 1) + w_ref[2:3, :] * u + b_ref[...]


def _conv_gate(u, cw, cb, name, comm=None):
    S, F2 = u.shape
    Fh = F2 // 2
    tc = _tile(Fh, 256)
    nf = Fh // tc

    def body(ug_ref, uv_ref, wg_ref, wv_ref, bg_ref, bv_ref, a_ref):
        g = _conv3(ug_ref[...], wg_ref, bg_ref)
        val = _conv3(uv_ref[...], wv_ref, bv_ref)
        a_ref[...] = (g * (1.0 / (1.0 + jnp.exp(-g))) * val).astype(BF16)

    blk = lambda r, off: pl.BlockSpec((r, tc), lambda j: (0, j + off))
    return _call(
        body, name=name, grid=(nf,),
        in_specs=[blk(S, 0), blk(S, nf), blk(3, 0), blk(3, nf), blk(1, 0), blk(1, nf)], out_specs=[blk(S, 0)],
        out_shape=[jax.ShapeDtypeStruct((S, Fh), BF16)], sem=("parallel",), args=(u, u, cw, cw, cb, cb), comm=comm)[0]


def _conv_gate_bwd(u, da, cw, cb, name, comm=None):
    S, F2 = u.shape
    Fh = F2 // 2
    tc = _tile(Fh, 256)
    nf = Fh // tc

    def half(h, dx, uu, w_ref, du_ref, dw_ref, db_ref):
        du = w_ref[2:3, :] * dx + w_ref[1:2, :] * _shift_up(dx, 1) + w_ref[0:1, :] * _shift_up(dx, 2)
        du_ref[h] = du.astype(BF16)
        dw_ref[h, 0:1, :] = jnp.sum(dx * _shift_down(uu, 2), axis=0, keepdims=True)
        dw_ref[h, 1:2, :] = jnp.sum(dx * _shift_down(uu, 1), axis=0, keepdims=True)
        dw_ref[h, 2:3, :] = jnp.sum(dx * uu, axis=0, keepdims=True)
        db_ref[h] = jnp.sum(dx, axis=0, keepdims=True)

    def body(ug_ref, uv_ref, da_ref, wg_ref, wv_ref, bg_ref, bv_ref, du_ref, dw_ref, db_ref):
        ug = ug_ref[...]
        uv = uv_ref[...]
        g = _conv3(ug, wg_ref, bg_ref)
        val = _conv3(uv, wv_ref, bv_ref)
        sig = 1.0 / (1.0 + jnp.exp(-g))
        da_ = da_ref[...]
        dg = da_ * val * (sig * (1.0 + g * (1.0 - sig)))
        dval = da_ * (g * sig)
        half(0, dg, ug, wg_ref, du_ref, dw_ref, db_ref)
        half(1, dval, uv, wv_ref, du_ref, dw_ref, db_ref)

    blk = lambda r, off: pl.BlockSpec((r, tc), lambda j: (0, j + off))
    both = lambda r: pl.BlockSpec((2, r, tc), lambda j: (0, 0, j))
    return _call(
        body, name=name, grid=(nf,),
        in_specs=[blk(S, 0), blk(S, nf), blk(S, 0), blk(3, 0), blk(3, nf), blk(1, 0), blk(1, nf)],
        out_specs=[both(S), both(3), both(1)],
        out_shape=[jax.ShapeDtypeStruct((2, S, Fh), BF16), jax.ShapeDtypeStruct((2, 3, Fh), F32), jax.ShapeDtypeStruct((2, 1, Fh), F32)],
        sem=("parallel",), args=(u, u, da, cw, cw, cb, cb), comm=comm)


def _to_groups(t, kh):
    S, width = t.shape
    g = width // SDH // kh
    return t.reshape(S // WIN, WIN, kh, g, SDH).transpose(2, 0, 3, 1, 4).reshape(kh, S // WIN, g * WIN, SDH)


def _from_groups(t):
    kh, nb, gm, _ = t.shape
    g = gm // WIN
    return t.reshape(kh, nb, g, WIN, SDH).transpose(1, 3, 0, 2, 4).reshape(nb * WIN, kh * g * SDH)


class LocalWeights:
    def __init__(self, weights):
        self.weights, self.grads = weights, {}

    def w(self, name):
        return self.weights[name]

    def carry(self, stage):
        return None

    def carried(self, stage, comm):
        pass

    def grad(self, name, g):
        self.grads[name] = g


def _local_step(dm, x, tgt, pos, mod, sp, pp):
    S, D, FH, QH, KH, Fh = dm
    m = [[mod[i:i + 1, j * D:(j + 1) * D] for j in range(6)] for i in range(DEPTH)]

    def run(fn, *args, name, **kw):
        comm = pp.carry(name)
        out = fn(*args, name=name, comm=comm, **kw)
        if comm is not None:
            pp.carried(name, comm)
        return out

    sv = []
    xs = x
    h = _modulate(xs, m[0][1], m[0][0], "mod_in")
    for i in range(DEPTH):
        sh1, sc1, g1, sh2, sc2, g2 = m[i]
        L = {}
        L["x_in"], L["h1"] = xs, h
        if i == 0:
            proj = run(_mm, h[None], pp.w("fox_w_in"), mode="nn", out_dtype=F32, name="fox_proj", tn=896)[0]
            cum = _fox_gate_fwd(proj, sp["fox_b_f"], FH, "fox_gate")
            cq = cum[:, :FH].T[:, :, None]
            ck = cum[:, :FH].T[:, None, :]
            o, lse = run(_fox_fwd, proj, cq, ck, FH, name="fox_attn")
            L.update(proj=proj, cq=cq, ck=ck, lse=lse, o=o)
            y = run(_mm, o[None], pp.w("fox_w_o"), mode="nn", out_dtype=F32, name="fox_out")[0]
        else:
            proj = run(_mm, h[None], pp.w("swa_w_in"), mode="nn", out_dtype=F32, name="swa_proj", tn=640)[0]
            tabs = _rope_tables(pos, 1.0)
            n_rot = (QH + KH) * SDH // LANES
            pr = _rope(proj, tabs, n_rot, BF16, "swa_rope")
            qh = _to_groups(pr[:, :QH * SDH], KH)
            kh = pr[:, QH * SDH:(QH + KH) * SDH].reshape(S, KH, SDH).transpose(1, 0, 2)
            vh = pr[:, (QH + KH) * SDH:].reshape(S, KH, SDH).transpose(1, 0, 2)
            oh = run(_swa_fwd, qh, kh, vh, sp["sinks"], name="swa_attn")
            o = _from_groups(oh)
            L.update(qh=qh, kh=kh, vh=vh, o=o)
            y = run(_mm, o[None], pp.w("swa_w_o"), mode="nn", out_dtype=F32, name="swa_out")[0]
        L["y1"] = y
        x1, L["xh1"], L["rs1"], h2 = _ln_fwd(xs, y, g1, sp["ln_mix_g"][i], sp["ln_mix_b"][i], sc2, sh2, f"ln_mix{i}")
        L["x1"], L["h2"] = x1, h2
        u = run(_mm, h2[None], pp.w(f"ffn_w_up{i}"), mode="nn", out_dtype=F32, name=f"ffn_up{i}", tm=512, tn=1408)[0]
        a = run(_conv_gate, u, sp["conv_w"][i], sp["conv_b"][i], name=f"ffn_gate{i}")
        y2 = run(_mm, a[None], pp.w(f"ffn_w_down{i}"), mode="nn", out_dtype=F32, name=f"ffn_down{i}", tk=1408)[0]
        L.update(u=u, a=a, y2=y2)
        if i + 1 < DEPTH:
            xs, L["xh2"], L["rs2"], h = _ln_fwd(x1, y2, g2, sp["ln_ffn_g"][i], sp["ln_ffn_b"][i], m[i + 1][1], m[i + 1][0], f"ln_ffn{i}")
        else:
            xs, L["xh2"], L["rs2"] = _ln_fwd(x1, y2, g2, sp["ln_ffn_g"][i], sp["ln_ffn_b"][i], None, None, f"ln_ffn{i}")
        sv.append(L)

    dx, loss_cols = _loss_head(xs, tgt, "loss_head")

    gs = {k: [None] * DEPTH for k in ("conv_w", "conv_b", "ln_mix_g", "ln_mix_b", "ln_ffn_g", "ln_ffn_b")}
    dmod = [None] * DEPTH
    for i in reversed(range(DEPTH)):
        sh1, sc1, g1, sh2, sc2, g2 = m[i]
        L = sv[i]
        dres, dy, gs["ln_ffn_g"][i], gs["ln_ffn_b"][i], dg2 = _ln_bwd(dx, L["xh2"], L["rs2"], sp["ln_ffn_g"][i], L["y2"], g2, f"ln_ffn_bwd{i}")
        da = run(_mm, dy[None], pp.w(f"ffn_w_down{i}"), mode="nt", out_dtype=F32, name=f"ffn_down_dx{i}", tm=512, tn=1408)[0]
        pp.grad(f"ffn_w_down{i}", run(_mm, L["a"][None], dy[None], mode="tn", out_dtype=BF16, name=f"ffn_down_dw{i}", tm=1408))
        du, dcw, dcb = run(_conv_gate_bwd, L["u"], da, sp["conv_w"][i], sp["conv_b"][i], name=f"ffn_gate_bwd{i}")
        gs["conv_w"][i] = dcw.transpose(1, 0, 2).reshape(3, 2 * Fh)
        gs["conv_b"][i] = dcb.transpose(1, 0, 2).reshape(1, 2 * Fh)
        dh2 = run(_mm, du, pp.w(f"ffn_w_up{i}"), mode="nt", out_dtype=F32, name=f"ffn_up_dx{i}", tk=1408)[0]
        pp.grad(f"ffn_w_up{i}", run(_mm, L["h2"][None], du, mode="tn", out_dtype=BF16, name=f"ffn_up_dw{i}", out_groups=N_CHIPS, tn=1408))
        dx, dsc2, dsh2 = _mod_bwd(dh2, L["x1"], sc2, dres, f"mod_ffn_bwd{i}")
        dres, dy, gs["ln_mix_g"][i], gs["ln_mix_b"][i], dg1 = _ln_bwd(dx, L["xh1"], L["rs1"], sp["ln_mix_g"][i], L["y1"], g1, f"ln_mix_bwd{i}")
        if i == 0:
            do = run(_mm, dy[None], pp.w("fox_w_o"), mode="nt", out_dtype=BF16, name="fox_out_dx")[0]
            pp.grad("fox_w_o", run(_mm, L["o"][None], dy[None], mode="tn", out_dtype=BF16, name="fox_out_dw"))
            dq, dk, dv, dcq, dck = run(_fox_bwd, L["proj"], L["cq"], L["ck"], L["lse"], do, FH, name="fox_attn_bwd")
            dcum = dcq[:, :, 0].T + dck[:, 0, :].T
            dcum = jnp.pad(dcum, ((0, 0), (0, LANES - FH)))
            dfl, db_f = _fox_gate_bwd(dcum, L["proj"], sp["fox_b_f"], FH, "fox_gate_bwd")
            gs["fox_b_f"] = db_f
            dproj = jnp.concatenate([dq, dk, dv, dfl], axis=1)
            dh1 = run(_mm, dproj[None], pp.w("fox_w_in"), mode="nt", out_dtype=F32, name="fox_proj_dx", tk=896)[0]
            pp.grad("fox_w_in", run(_mm, L["h1"][None], dproj[None], mode="tn", out_dtype=BF16, name="fox_proj_dw", tn=896))
        else:
            do = run(_mm, dy[None], pp.w("swa_w_o"), mode="nt", out_dtype=BF16, name="swa_out_dx")[0]
            pp.grad("swa_w_o", run(_mm, L["o"][None], dy[None], mode="tn", out_dtype=BF16, name="swa_out_dw"))
            dqh, dkh, dvh, dsk = run(_swa_bwd, L["qh"], L["kh"], L["vh"], sp["sinks"], _to_groups(do, KH), name="swa_attn_bwd")
            gs["sinks"] = jnp.sum(dsk.reshape(QH, WIN), axis=1)
            dpr = jnp.concatenate([_from_groups(dqh), dkh.transpose(1, 0, 2).reshape(S, KH * SDH),
                                   dvh.transpose(1, 0, 2).reshape(S, KH * SDH)], axis=1)
            n_rot = (QH + KH) * SDH // LANES
            dproj = _rope(dpr, _rope_tables(pos, -1.0), n_rot, BF16, "swa_rope_bwd")
            dh1 = run(_mm, dproj[None], pp.w("swa_w_in"), mode="nt", out_dtype=F32, name="swa_proj_dx", tk=640)[0]
            pp.grad("swa_w_in", run(_mm, L["h1"][None], dproj[None], mode="tn", out_dtype=BF16, name="swa_proj_dw", out_groups=N_CHIPS, tn=640))
        dx, dsc1, dsh1 = _mod_bwd(dh1, L["x_in"], sc1, dres, f"mod_mix_bwd{i}")
        dmod[i] = jnp.concatenate([dsh1, dsc1, dg1, dsh2, dsc2, dg2], axis=1)
    return loss_cols, dx, gs, jnp.concatenate(dmod, axis=0)


def _allgather_small(v, name):
    m_per, n = v.shape

    def body(x_ref, out_ref, send_sems, recv_sems, local_sem):
        x, y, c, chips = _place()
        me, sibling = (x, y, c), (x, y, 1 - c)

        def rows(px, py, pc):
            return out_ref.at[pl.ds((4 * px + 2 * py + pc) * m_per, m_per), :]

        def copy(k, block, to, src=None):
            return _remote(rows(*block) if src is None else src, rows(*block), send_sems.at[k], recv_sems.at[k], to)

        mine = pltpu.make_async_copy(x_ref, rows(*me), local_sem)
        mine.start()
        first = [copy(0, me, sibling, src=x_ref)]
        first += [copy(1 + j, me, (*chip, c), src=x_ref) for j, chip in enumerate(chips)]
        for cp in first:
            cp.start()
        passed = [copy(4 + j, (*chip, c), sibling) for j, chip in enumerate(chips)]
        for j, chip in enumerate(chips):
            copy(1 + j, (*chip, c), me).wait_recv()
            passed[j].start()
        copy(0, sibling, me).wait_recv()
        for j, chip in enumerate(chips):
            copy(4 + j, (*chip, 1 - c), me).wait_recv()
        for cp in first + passed:
            cp.wait_send()
        mine.wait()

    return pl.pallas_call(
        body, name=name, out_shape=jax.ShapeDtypeStruct((N_DEV * m_per, n), v.dtype),
        in_specs=[pl.BlockSpec(memory_space=pltpu.VMEM)], out_specs=pl.BlockSpec(memory_space=pltpu.VMEM),
        scratch_shapes=[pltpu.SemaphoreType.DMA((7,)), pltpu.SemaphoreType.DMA((7,)), pltpu.SemaphoreType.DMA],
        compiler_params=pltpu.CompilerParams(vmem_limit_bytes=VMEM_LIMIT),
    )(v)


def _swap_halves(gs, name):
    n = len(gs)

    def body(*refs):
        srcs, outs = refs[:n], refs[n:2 * n]
        send, recv = refs[2 * n:]
        x, y, c, _ = _place()
        cps = []
        for t in range(n):
            rh = gs[t].shape[1] // 2
            cp = _remote(srcs[t].at[:, pl.ds((1 - c) * rh, rh), :], outs[t], send.at[t], recv.at[t], (x, y, 1 - c))
            cp.start()
            cps.append(cp)
        for cp in cps:
            cp.wait()

    return pl.pallas_call(
        body, name=name,
        out_shape=[jax.ShapeDtypeStruct((g.shape[0], g.shape[1] // 2, g.shape[2]), g.dtype) for g in gs],
        in_specs=_any_specs(n), out_specs=_any_specs(n),
        scratch_shapes=[pltpu.SemaphoreType.DMA((n,)), pltpu.SemaphoreType.DMA((n,))],
    )(*gs)


def _join_halves(gs, name):
    n = len(gs)

    def body(*refs):
        outs = refs[n:2 * n]
        send, recv = refs[2 * n:]
        x, y, c, _ = _place()
        cps = []
        for t in range(n):
            rh = gs[t].shape[0] // 2
            mine = outs[t].at[pl.ds(c * rh, rh), :]
            cp = _remote(mine, mine, send.at[t], recv.at[t], (x, y, 1 - c))
            cp.start()
            cps.append(cp)
        for t in range(n):
            rh = gs[t].shape[0] // 2
            theirs = outs[t].at[pl.ds((1 - c) * rh, rh), :]
            _remote(theirs, theirs, send.at[t], recv.at[t], (x, y, 1 - c)).wait_recv()
        for cp in cps:
            cp.wait_send()

    return pl.pallas_call(
        body, name=name, out_shape=[jax.ShapeDtypeStruct(g.shape, g.dtype) for g in gs],
        in_specs=_any_specs(n), out_specs=_any_specs(n), input_output_aliases={t: t for t in range(n)},
        scratch_shapes=[pltpu.SemaphoreType.DMA((n,)), pltpu.SemaphoreType.DMA((n,))],
    )(*gs)


def _row_tile(r, pref=256):
    return _tile(r, pref, 16)


def _cast_bf16(w, layer, chip, name):
    _, R, C = w.shape
    tr = _row_tile(R)

    def body(s_ref, w_ref, o_ref):
        o_ref[...] = w_ref[...].astype(BF16)

    return pl.pallas_call(
        body, name=name,
        grid_spec=pltpu.PrefetchScalarGridSpec(
            num_scalar_prefetch=1, grid=(R // tr,),
            in_specs=[pl.BlockSpec((None, tr, C), lambda i, s: (layer, i, 0))],
            out_specs=pl.BlockSpec((None, tr, C), lambda i, s: (s[0], i, 0))),
        out_shape=jax.ShapeDtypeStruct((N_CHIPS, R, C), BF16), compiler_params=_params(("parallel",)),
    )(jnp.reshape(chip, (1,)).astype(jnp.int32), w)


def _add_sibling(g, got, c, name):
    G, R, C = g.shape
    rh = R // 2
    tr = _row_tile(rh)
    nb = rh // tr

    def body(c_ref, g_ref, o_ref, p_ref):
        p_ref[...] = (g_ref[...].astype(F32) + o_ref[...].astype(F32)).astype(BF16)

    return pl.pallas_call(
        body, name=name,
        grid_spec=pltpu.PrefetchScalarGridSpec(
            num_scalar_prefetch=1, grid=(G, nb),
            in_specs=[pl.BlockSpec((1, tr, C), lambda s, i, c_ref: (s, c_ref[0] * nb + i, 0)),
                      pl.BlockSpec((1, tr, C), lambda s, i, c_ref: (s, i, 0))],
            out_specs=pl.BlockSpec((1, tr, C), lambda s, i, c_ref: (s, i, 0))),
        out_shape=jax.ShapeDtypeStruct((G, rh, C), BF16), compiler_params=_params(("parallel", "parallel")),
    )(jnp.reshape(c, (1,)).astype(jnp.int32), g, got)


def _sum_chips(part, landed, chip, c, name):
    G, rh, C = part.shape
    tr = _row_tile(rh)
    nb = rh // tr

    def body(p_ref, own_ref, *rest):
        acc = own_ref[...].astype(F32)
        for ref in rest[:G - 1]:
            acc = acc + ref[...].astype(F32)
        rest[G - 1][...] = acc

    slot = lambda k: pl.BlockSpec((None, tr, C), lambda i, p: ((p[0] + k) % G, i, 0))
    return pl.pallas_call(
        body, name=name,
        grid_spec=pltpu.PrefetchScalarGridSpec(
            num_scalar_prefetch=1, grid=(nb,), in_specs=[slot(k) for k in range(G)],
            out_specs=pl.BlockSpec((tr, C), lambda i, p: (p[1] * nb + i, 0))),
        out_shape=jax.ShapeDtypeStruct((2 * rh, C), F32), compiler_params=_params(("parallel",)),
    )(jnp.stack([chip, c]).astype(jnp.int32), part, *([landed] * (G - 1)))


def _adam_math(w, g, m, v):
    m = ADAM_B1 * m + (1.0 - ADAM_B1) * g
    v = ADAM_B2 * v + (1.0 - ADAM_B2) * (g * g)
    m_hat = m / (1.0 - ADAM_B1 ** ADAM_STEP)
    v_hat = v / (1.0 - ADAM_B2 ** ADAM_STEP)
    delta = -ADAM_LR * (m_hat / (jnp.sqrt(v_hat) + ADAM_EPS) + ADAM_WD * w)
    return delta, m, v


def _adamw(w, g, m, v, layer, prev, name):
    L, R, C = w.shape
    tr = _tile(R, 128, 8)
    n_prev = len(prev)

    def body(w_ref, g_ref, m_ref, v_ref, *rest):
        go_ref, d_ref, mo_ref, vo_ref = rest[n_prev:]
        gv = g_ref[...]
        go_ref[...] = gv
        d_ref[...], mo_ref[...], vo_ref[...] = _adam_math(w_ref[...], gv, m_ref[...], v_ref[...])

    lay = pl.BlockSpec((None, tr, C), lambda i: (layer, i, 0))
    flat = pl.BlockSpec((tr, C), lambda i: (i, 0))
    return pl.pallas_call(
        body, name=name, grid=(R // tr,), in_specs=[lay, flat, lay, lay] + _any_specs(n_prev), out_specs=[lay] * 4,
        out_shape=[jax.ShapeDtypeStruct((L, R, C), F32)] * 4, input_output_aliases={4 + k: k for k in range(n_prev)},
        compiler_params=_params(("parallel",)),
    )(w, g, m, v, *prev)


def _cond_rows(c_row, cw, name):
    D = c_row.shape[1]
    nr, fc = cw.shape

    def body(c_ref, e_ref, o_ref):
        o_ref[...] = jnp.zeros_like(o_ref)
        cv = c_ref[...]
        o_ref[0:1, 0:D] = cv * (1.0 / (1.0 + jnp.exp(-cv)))
        o_ref[8:8 + nr, 0:fc] = e_ref[...]

    return pl.pallas_call(body, name=name, out_shape=jax.ShapeDtypeStruct((16, max(D, fc)), F32))(c_row, cw)


def _ada_fwd(cact, ada_w, ada_b, layer, chip, name):
    _, D, NC = ada_w.shape
    tn = _tile(NC, 1024)
    nj = NC // tn

    def body(idx_ref, c_ref, w_ref, b_ref, o_ref):
        acc = jnp.dot(c_ref[...].astype(BF16), w_ref[0].astype(BF16), preferred_element_type=F32)
        o_ref[...] = acc + b_ref[pl.ds(idx_ref[0], 1), :]

    return pl.pallas_call(
        body, name=name,
        grid_spec=pltpu.PrefetchScalarGridSpec(
            num_scalar_prefetch=1, grid=(nj,),
            in_specs=[pl.BlockSpec((8, D), lambda j, idx: (0, 0)),
                      pl.BlockSpec((1, D, tn), lambda j, idx: (idx[0], 0, j)),
                      pl.BlockSpec((DEPTH, tn), lambda j, idx: (0, idx[1] * nj + j))],
            out_specs=pl.BlockSpec((8, tn), lambda j, idx: (0, j))),
        out_shape=jax.ShapeDtypeStruct((8, NC), F32), compiler_params=_params(("parallel",)),
    )(jnp.stack([layer, chip]).astype(jnp.int32), cact, ada_w, ada_b)


def _ada_grad_adamw(cact_t, dmod, w, m, v, name):
    L, D, NC = w.shape
    tr = _tile(D, 128, 8)

    def body(c_ref, d_ref, w_ref, m_ref, v_ref, g_ref, dl_ref, mo_ref, vo_ref):
        g = jnp.dot(c_ref[...], d_ref[...], preferred_element_type=F32, precision=HIGHEST)
        g_ref[...] = g
        dl_ref[...], mo_ref[...], vo_ref[...] = _adam_math(w_ref[...], g, m_ref[...], v_ref[...])

    lay = pl.BlockSpec((None, tr, NC), lambda l, i: (l, i, 0))
    return pl.pallas_call(
        body, name=name, grid=(L, D // tr),
        in_specs=[pl.BlockSpec((tr, N_DEV), lambda l, i: (i, 0)), pl.BlockSpec((None, N_DEV, NC), lambda l, i: (l, 0, 0)), lay, lay, lay],
        out_specs=[lay] * 4, out_shape=[jax.ShapeDtypeStruct((L, D, NC), F32)] * 4,
        compiler_params=_params(("parallel", "parallel")),
    )(cact_t, dmod, w, m, v)


def _sum_devices(gathered, name):
    n, R, C = gathered.shape

    def body(g_ref, o_ref):
        acc = g_ref[0]
        for j in range(1, n):
            acc = acc + g_ref[j]
        o_ref[...] = acc

    return pl.pallas_call(body, name=name, out_shape=jax.ShapeDtypeStruct((R, C), F32),
                          compiler_params=pltpu.CompilerParams(vmem_limit_bytes=VMEM_LIMIT))(gathered)


def _adamw_small(w, g, m, v, name):
    def body(w_ref, g_ref, m_ref, v_ref, d_ref, mo_ref, vo_ref):
        d_ref[...], mo_ref[...], vo_ref[...] = _adam_math(w_ref[...], g_ref[...], m_ref[...], v_ref[...])

    return pl.pallas_call(body, name=name, out_shape=[jax.ShapeDtypeStruct(w.shape, F32)] * 3)(w, g, m, v)


def _pad_rows(flat, unit=8 * LANES):
    n = flat.shape[0]
    total = -(-n // unit) * unit
    return jnp.pad(flat, (0, total - n)).reshape(total // LANES, LANES)


def _pad_lanes(v2d):
    return jnp.pad(v2d.reshape(1, -1), ((0, 0), (0, LANES - v2d.size)))


GATHER_PLAN = {
    "fox_proj": [("ffn_w_up0", 0, 2)],
    "fox_attn": [("ffn_w_up0", 1, 2)],
    "ffn_up0": [("ffn_w_down0", 0, 1)],
    "ffn_gate0": [("swa_w_in", 0, 1), ("swa_w_o", 0, 1)],
    "ffn_down0": [("ffn_w_up1", 0, 2)],
    "swa_attn": [("ffn_w_up1", 1, 2)],
    "ffn_up1": [("ffn_w_down1", 0, 1)],
}
SCATTER_PLAN = {
    "ffn_gate_bwd1": [("ffn_w_down1", 0, 1)],
    "swa_attn_bwd": [("ffn_w_up1", 0, 1)],
    "ffn_down_dx0": [("swa_w_o", 0, 1)],
    "ffn_gate_bwd0": [("swa_w_in", 0, 1)],
    "ffn_up_dx0": [("ffn_w_down0", 0, 1)],
    "fox_attn_bwd": [("ffn_w_up0", 0, 2)],
    "fox_proj_dx": [("ffn_w_up0", 1, 2)],
}


class Exchanges:
    def __init__(self, dm, slots, chip, c):
        self.dm, self.slots, self.chip, self.c = dm, dict(slots), chip, c
        self.part, self.landed, self.sent, self.views = {}, {}, set(), {}

    def gather_now(self, keys, name):
        comm = _gather_comm([self.slots[k] for k in keys], [(0, 1)] * len(keys))
        _run_comm(comm, name)
        self.slots.update(zip(keys, comm.results))

    def w(self, key):
        if key not in self.views:
            S, D, FH, QH, KH, Fh = self.dm
            full = self.slots[key]
            if key == "fox_w_in":
                cols = full.shape[2]
                full = jnp.pad(full.transpose(1, 0, 2).reshape(D, N_CHIPS * cols), ((0, 0), (0, 3 * D + LANES - N_CHIPS * cols)))[None]
            elif key in ("fox_w_o", "swa_w_o"):
                full = full.reshape(1, D, D)
            elif key.startswith("ffn_w_down"):
                full = full.reshape(1, Fh, D)
            self.views[key] = full
        return self.views[key]

    def carry(self, stage):
        if stage in GATHER_PLAN:
            items = GATHER_PLAN[stage]
            return _gather_comm([self.slots[k] for k, _, _ in items], [(i, n) for _, i, n in items])
        if stage in SCATTER_PLAN:
            items = SCATTER_PLAN[stage]
            return _scatter_comm([self.part[k] for k, _, _ in items], [self.landed.get(k) for k, _, _ in items], [(i, n) for _, i, n in items])
        return None

    def carried(self, stage, comm):
        if stage in GATHER_PLAN:
            self.slots.update(zip([k for k, _, _ in GATHER_PLAN[stage]], comm.results))
        else:
            for (k, i, n), res in zip(SCATTER_PLAN[stage], comm.results):
                self.landed[k] = res
                if i == n - 1:
                    self.sent.add(k)

    def grad(self, key, g):
        S, D, FH, QH, KH, Fh = self.dm
        if key == "fox_w_in":
            cols = self.slots[key].shape[2]
            g = g[0][:, :N_CHIPS * cols].reshape(D, N_CHIPS, cols).transpose(1, 0, 2)
        elif key in ("fox_w_o", "swa_w_o"):
            g = g.reshape(N_CHIPS, D // N_CHIPS, D)
        elif key.startswith("ffn_w_down"):
            g = g.reshape(N_CHIPS, Fh // N_CHIPS, D)
        got = _swap_halves([g], f"swap_{key}")[0]
        self.part[key] = _add_sibling(g, got, self.c, f"add_sibling_{key}")

    def finish(self):
        rest = [k for k in self.part if k not in self.sent]
        if rest:
            comm = _scatter_comm([self.part[k] for k in rest], [None] * len(rest), [(0, 1)] * len(rest))
            _run_comm(comm, "grads_to_owner")
            self.landed.update(zip(rest, comm.results))
        keys = list(self.part)
        halves = [_sum_chips(self.part[k], self.landed[k], self.chip, self.c, f"sum_chips_{k}") for k in keys]
        return dict(zip(keys, _join_halves(halves, "grads_join")))


def _step(dm, a):
    S, D, FH, QH, KH, Fh = dm
    ix, iy, ic = lax.axis_index("x"), lax.axis_index("y"), lax.axis_index("c")
    chip = 2 * ix + iy
    dev = 2 * chip + ic
    F2c = a["ffn_w_up"].shape[2]
    NC = a["ada_w"].shape[2]
    PW = 3 * D + LANES
    fox_cols = a["fox_w_in"].shape[2]

    e0 = _cond_rows(a["c"], a["ffn_conv_w"].reshape(DEPTH * 3, F2c), "silu_c")
    g0 = _allgather_small(e0, "gather_cond").reshape(N_DEV, 16, e0.shape[1])
    cact = g0[:, 0, :D]
    conv_w = g0[0::2, 8:8 + DEPTH * 3, :F2c].transpose(1, 0, 2).reshape(DEPTH, 3, N_CHIPS * F2c)
    rows = _ada_fwd(cact, a["ada_w"], a["ada_b"], ic, chip, "ada_proj")
    g1 = _allgather_small(rows, "gather_mod").reshape(N_CHIPS, DEPTH, 8, NC)
    mod = lax.dynamic_index_in_dim(g1, dev, axis=2, keepdims=False).transpose(1, 0, 2).reshape(DEPTH, N_CHIPS * NC)

    names = ["fox_w_in", "fox_w_o", "swa_w_in", "swa_w_o", "ffn_w_up", "ffn_w_up", "ffn_w_down", "ffn_w_down"]
    layers = [0, 0, 0, 0, 0, 1, 0, 1]
    keys = ["fox_w_in", "fox_w_o", "swa_w_in", "swa_w_o", "ffn_w_up0", "ffn_w_up1", "ffn_w_down0", "ffn_w_down1"]
    slots = {k: _cast_bf16(a[nm], l, chip, f"cast_{k}") for k, nm, l in zip(keys, names, layers)}
    pp = Exchanges(dm, slots, chip, ic)
    pp.gather_now(["fox_w_in", "fox_w_o"], "gather_fox")
    sp = {"fox_b_f": _pad_lanes(a["fox_b_f"]), "sinks": jnp.repeat(a["swa_sinks"].reshape(KH, QH // KH), WIN, axis=1)[:, :, None],
          "conv_w": [conv_w[i] for i in range(DEPTH)], "conv_b": [a["ffn_conv_b"][i:i + 1] for i in range(DEPTH)]}
    for nm in ("ln_mix_g", "ln_mix_b", "ln_ffn_g", "ln_ffn_b"):
        sp[nm] = [a[nm][i:i + 1] for i in range(DEPTH)]

    loss_cols, grad_x, gs, dmod = _local_step(dm, a["x"][0], a["loss_target"][0], a["positions"][0], mod, sp, pp)
    loss = lax.psum(0.5 / D * jnp.sum(loss_cols), ("x", "y", "c"))
    grads = pp.finish()

    out = {"loss": loss, "grad_x": grad_x[None]}
    upd = {}
    for k, nm, l in zip(keys, names, layers):
        upd[nm] = _adamw(a[nm], grads[k], a["m_" + nm], a["v_" + nm], l, upd.get(nm, ()), f"adamw_{k}")
    for nm, res in upd.items():
        for pre, arr in zip(("grad_", "delta_", "new_m_", "new_v_"), res):
            out[pre + nm] = arr

    pieces = [dmod.reshape(-1), gs["fox_b_f"].reshape(-1), _pad_lanes(gs["sinks"]).reshape(-1),
              jnp.stack(gs["conv_w"]).reshape(-1), jnp.stack(gs["conv_b"]).reshape(-1)]
    pieces += [jnp.stack(gs[nm]).reshape(-1) for nm in ("ln_mix_g", "ln_mix_b", "ln_ffn_g", "ln_ffn_b")]
    sizes = [p.shape[0] for p in pieces]
    packed = _pad_rows(jnp.concatenate(pieces))
    allp = _allgather_small(packed, "gather_small").reshape(N_DEV, packed.shape[0], LANES)
    tot = _sum_devices(allp, "sum_small").reshape(-1)
    offs = [sum(sizes[:k]) for k in range(len(sizes))]
    take = lambda k: tot[offs[k]:offs[k] + sizes[k]]
    g_small = {"ada_b": take(0).reshape(DEPTH, -1), "fox_b_f": take(1)[:FH].reshape(1, FH), "swa_sinks": take(2)[:QH].reshape(1, QH),
               "ffn_conv_w": lax.dynamic_slice_in_dim(take(3).reshape(DEPTH, 3, N_CHIPS * F2c), chip * F2c, F2c, axis=2),
               "ffn_conv_b": take(4).reshape(DEPTH, -1)}
    for k, nm in enumerate(("ln_mix_g", "ln_mix_b", "ln_ffn_g", "ln_ffn_b")):
        g_small[nm] = take(5 + k).reshape(DEPTH, D)
    small = list(g_small)
    pack = lambda pre: _pad_rows(jnp.concatenate([(a[pre + nm] if pre else a[nm]).reshape(-1) for nm in small]))
    gp = _pad_rows(jnp.concatenate([g_small[nm].reshape(-1) for nm in small]))
    ds_, ms_, vs_ = _adamw_small(pack(""), gp, pack("m_"), pack("v_"), "adamw_small")
    off = 0
    for nm in small:
        n_el = a[nm].size
        out["grad_" + nm] = g_small[nm]
        for pre, arr in (("delta_", ds_), ("new_m_", ms_), ("new_v_", vs_)):
            out[pre + nm] = arr.reshape(-1)[off:off + n_el].reshape(a[nm].shape)
        off += n_el

    dmod_all = allp.reshape(N_DEV, -1)[:, :DEPTH * N_CHIPS * NC].reshape(N_DEV, DEPTH, N_CHIPS * NC)
    dmod_mine = lax.dynamic_slice_in_dim(dmod_all, chip * NC, NC, axis=2).transpose(1, 0, 2)
    ada = _ada_grad_adamw(cact.T, dmod_mine, a["ada_w"], a["m_ada_w"], a["v_ada_w"], "ada_grad")
    for pre, arr in zip(("grad_", "delta_", "new_m_", "new_v_"), ada):
        out[pre + "ada_w"] = arr
    return out


_WEIGHTS = ["fox_w_in", "fox_b_f", "fox_w_o", "swa_w_in", "swa_sinks", "swa_w_o", "ada_w", "ada_b", "ffn_w_up", "ffn_conv_w",
            "ffn_conv_b", "ffn_w_down", "ln_mix_g", "ln_mix_b", "ln_ffn_g", "ln_ffn_b"]
_INPUTS = (["x", "c", "positions"] + _WEIGHTS + ["loss_target"] + ["m_" + w for w in _WEIGHTS] + ["v_" + w for w in _WEIGHTS])


def kernel(x, c, positions, fox_w_in, fox_b_f, fox_w_o, swa_w_in, swa_sinks, swa_w_o, ada_w, ada_b, ffn_w_up, ffn_conv_w, ffn_conv_b, ffn_w_down, ln_mix_g, ln_mix_b, ln_ffn_g, ln_ffn_b, loss_target, m_fox_w_in, m_fox_b_f, m_fox_w_o, m_swa_w_in, m_swa_sinks, m_swa_w_o, m_ada_w, m_ada_b, m_ffn_w_up, m_ffn_conv_w, m_ffn_conv_b, m_ffn_w_down, m_ln_mix_g, m_ln_mix_b, m_ln_ffn_g, m_ln_ffn_b, v_fox_w_in, v_fox_b_f, v_fox_w_o, v_swa_w_in, v_swa_sinks, v_swa_w_o, v_ada_w, v_ada_b, v_ffn_w_up, v_ffn_conv_w, v_ffn_conv_b, v_ffn_w_down, v_ln_mix_g, v_ln_mix_b, v_ln_ffn_g, v_ln_ffn_b):
    args = (x, c, positions, fox_w_in, fox_b_f, fox_w_o, swa_w_in, swa_sinks, swa_w_o, ada_w, ada_b, ffn_w_up, ffn_conv_w, ffn_conv_b, ffn_w_down, ln_mix_g, ln_mix_b, ln_ffn_g, ln_ffn_b, loss_target, m_fox_w_in, m_fox_b_f, m_fox_w_o, m_swa_w_in, m_swa_sinks, m_swa_w_o, m_ada_w, m_ada_b, m_ffn_w_up, m_ffn_conv_w, m_ffn_conv_b, m_ffn_w_down, m_ln_mix_g, m_ln_mix_b, m_ln_ffn_g, m_ln_ffn_b, v_fox_w_in, v_fox_b_f, v_fox_w_o, v_swa_w_in, v_swa_sinks, v_swa_w_o, v_ada_w, v_ada_b, v_ffn_w_up, v_ffn_conv_w, v_ffn_conv_b, v_ffn_w_down, v_ln_mix_g, v_ln_mix_b, v_ln_ffn_g, v_ln_ffn_b)
    out = _step(PROD, dict(zip(_INPUTS, args)))
    order = ["loss", "grad_x"] + [p + w for p in ("grad_", "delta_", "new_m_", "new_v_") for w in _WEIGHTS]
    return tuple(out[k] for k in order)
```

```python
import functools
from typing import NamedTuple

import jax
import jax.numpy as jnp
from jax import lax
from jax.experimental import pallas as pl
from jax.experimental.pallas import tpu as pltpu

F32 = jnp.float32
BF16 = jnp.bfloat16
MESH = pl.DeviceIdType.MESH
HIGHEST = lax.Precision.HIGHEST

N_CHIPS = 4
N_DEV = 8
LANES = 128
VMEM_LIMIT = 56 * 1024 * 1024

DEPTH = 2
DEEPNORM_ALPHA = (2.0 * DEPTH) ** 0.25
LN_EPS = 1e-5
ROPE_THETA = 500000.0
ADAM_LR, ADAM_B1, ADAM_B2, ADAM_EPS, ADAM_WD, ADAM_STEP = 0.001, 0.9, 0.999, 1e-08, 0.01, 10
NEG = -1e30


class Dims(NamedTuple):
    S: int
    D: int
    FH: int
    QH: int
    KH: int
    F: int


PROD = Dims(S=2048, D=2048, FH=16, QH=32, KH=4, F=5632)
FDH = 128
SDH = 64
WIN = 128
ROPE_DIM = 16
FOX_TQ = 256


def _params(sem=None, vmem=VMEM_LIMIT):
    return pltpu.CompilerParams(dimension_semantics=sem, vmem_limit_bytes=vmem)


def _tile(n, pref, unit=LANES):
    if n <= pref:
        return n
    t = (pref // unit) * unit
    while t > 0:
        if n % t == 0:
            return t
        t -= unit
    return n


class Comm:
    def __init__(self, args, out_shapes, aliases, n_sem, start, finish, members=()):
        self.args, self.out_shapes, self.aliases, self.n_sem = list(args), list(out_shapes), dict(aliases), n_sem
        self.start, self.finish = start, finish
        self.members = members
        self.results = None

    def set_results(self, res):
        self.results = list(res)
        for cm, o0 in self.members:
            cm.set_results(self.results[o0:o0 + len(cm.out_shapes)])


class _SemView:
    def __init__(self, sems, first):
        self.sems, self.first = sems, first

    @property
    def at(self):
        return self

    def __getitem__(self, k):
        return self.sems.at[self.first + k]


def _merge(comms):
    comms = [cm for cm in comms if cm is not None]
    if len(comms) < 2:
        return comms[0] if comms else None
    args, shapes, aliases, spans, n_sem = [], [], {}, [], 0
    for cm in comms:
        spans.append((len(args), len(shapes), n_sem))
        aliases.update({len(args) + a: len(shapes) + o for a, o in cm.aliases.items()})
        args += cm.args
        shapes += cm.out_shapes
        n_sem += cm.n_sem

    def each(step):
        def run(ar, ou, send, recv):
            for cm, (a0, o0, s0) in zip(comms, spans):
                getattr(cm, step)(ar[a0:a0 + len(cm.args)], ou[o0:o0 + len(cm.out_shapes)], _SemView(send, s0), _SemView(recv, s0))
        return run

    return Comm(args, shapes, aliases, n_sem, each("start"), each("finish"), [(cm, o0) for cm, (_, o0, _) in zip(comms, spans)])


def _place():
    x, y, c = lax.axis_index("x"), lax.axis_index("y"), lax.axis_index("c")
    chips = [(1 - x, y), (x, 1 - y), (1 - x, 1 - y)]
    return x, y, c, chips


def _remote(src, dst, send, recv, to):
    return pltpu.make_async_remote_copy(src_ref=src, dst_ref=dst, send_sem=send, recv_sem=recv, device_id=to, device_id_type=MESH)


def _any_specs(n):
    return [pl.BlockSpec(memory_space=pl.ANY)] * n


def _call(body, *, name, grid, in_specs, out_specs, out_shape, args, sem, scratch_shapes=(), aliases=None, comm=None):
    in_specs, out_specs, out_shape, scratch_shapes = list(in_specs), list(out_specs), list(out_shape), list(scratch_shapes)
    aliases = dict(aliases or {})
    if comm is None:
        return pl.pallas_call(body, name=name, grid=grid, in_specs=in_specs, out_specs=out_specs, out_shape=out_shape,
                              scratch_shapes=scratch_shapes, input_output_aliases=aliases, compiler_params=_params(sem))(*args)
    n_in, n_out, nc_in, nc_out, n_scr = len(in_specs), len(out_specs), len(comm.args), len(comm.out_shapes), len(scratch_shapes)

    def wrapped(*refs):
        ins, refs = refs[:n_in], refs[n_in:]
        cin, refs = refs[:nc_in], refs[nc_in:]
        outs, refs = refs[:n_out], refs[n_out:]
        cout, refs = refs[:nc_out], refs[nc_out:]
        scratch, (send, recv) = refs[:n_scr], refs[n_scr:]
        ids = [pl.program_id(k) for k in range(len(grid))]
        first = functools.reduce(jnp.logical_and, [i == 0 for i in ids])
        last = functools.reduce(jnp.logical_and, [i == g - 1 for i, g in zip(ids, grid)])

        @pl.when(first)
        def _():
            comm.start(cin, cout, send, recv)

        body(*ins, *outs, *scratch)

        @pl.when(last)
        def _():
            comm.finish(cin, cout, send, recv)

    res = pl.pallas_call(
        wrapped, name=name, grid=grid, in_specs=in_specs + _any_specs(nc_in), out_specs=out_specs + _any_specs(nc_out),
        out_shape=out_shape + comm.out_shapes,
        scratch_shapes=scratch_shapes + [pltpu.SemaphoreType.DMA((comm.n_sem,)), pltpu.SemaphoreType.DMA((comm.n_sem,))],
        input_output_aliases={**aliases, **{n_in + a: n_out + o for a, o in comm.aliases.items()}},
        compiler_params=_params(("arbitrary",) * len(grid)),
    )(*args, *comm.args)
    comm.set_results(res[n_out:])
    return list(res[:n_out])


def _run_comm(comm, name):
    nc_in, nc_out = len(comm.args), len(comm.out_shapes)

    def body(*refs):
        cin, cout, (send, recv) = refs[:nc_in], refs[nc_in:nc_in + nc_out], refs[nc_in + nc_out:]
        comm.start(cin, cout, send, recv)
        comm.finish(cin, cout, send, recv)

    res = pl.pallas_call(
        body, name=name, in_specs=_any_specs(nc_in), out_specs=_any_specs(nc_out), out_shape=comm.out_shapes,
        scratch_shapes=[pltpu.SemaphoreType.DMA((comm.n_sem,)), pltpu.SemaphoreType.DMA((comm.n_sem,))],
        input_output_aliases=comm.aliases,
    )(*comm.args)
    comm.set_results(res)


def _gather_comm(slots, chunks):
    n = len(slots)

    def rows(t, who):
        rh = slots[t].shape[1] // 2
        k, nch = chunks[t]
        rc = rh // nch
        return pl.ds(who * rh + k * rc, rc)

    def start(args, outs, send, recv):
        x, y, c, chips = _place()
        s = 2 * x + y
        for t in range(n):
            mine = outs[t].at[s, rows(t, c)]
            for j, chip in enumerate(chips):
                _remote(mine, mine, send.at[6 * t + j], recv.at[6 * t + j], (*chip, c)).start()

    def finish(args, outs, send, recv):
        x, y, c, chips = _place()
        s = 2 * x + y
        sib = (x, y, 1 - c)
        for t in range(n):
            for j, chip in enumerate(chips):
                blk = outs[t].at[2 * chip[0] + chip[1], rows(t, c)]
                _remote(blk, blk, send.at[6 * t + j], recv.at[6 * t + j], (*chip, c)).wait_recv()
                _remote(blk, blk, send.at[6 * t + 3 + j], recv.at[6 * t + 3 + j], sib).start()
        for t in range(n):
            for j, chip in enumerate(chips):
                blk = outs[t].at[2 * chip[0] + chip[1], rows(t, 1 - c)]
                _remote(blk, blk, send.at[6 * t + 3 + j], recv.at[6 * t + 3 + j], sib).wait_recv()
        for t in range(n):
            mine = outs[t].at[s, rows(t, c)]
            for j, chip in enumerate(chips):
                _remote(mine, mine, send.at[6 * t + j], recv.at[6 * t + j], (*chip, c)).wait_send()
                blk = outs[t].at[2 * chip[0] + chip[1], rows(t, c)]
                _remote(blk, blk, send.at[6 * t + 3 + j], recv.at[6 * t + 3 + j], sib).wait_send()

    shapes = [jax.ShapeDtypeStruct(w.shape, w.dtype) for w in slots]
    return Comm(slots, shapes, {t: t for t in range(n)}, 6 * n, start, finish)


def _scatter_comm(parts, landed, chunks):
    n = len(parts)
    prev = [t for t in range(n) if landed[t] is not None]

    def rows(t):
        k, nch = chunks[t]
        rc = parts[t].shape[1] // nch
        return pl.ds(k * rc, rc)

    def start(args, outs, send, recv):
        x, y, c, chips = _place()
        s = 2 * x + y
        for t in range(n):
            for j, chip in enumerate(chips):
                _remote(args[t].at[2 * chip[0] + chip[1], rows(t)], outs[t].at[s, rows(t)],
                        send.at[3 * t + j], recv.at[3 * t + j], (*chip, c)).start()

    def finish(args, outs, send, recv):
        x, y, c, chips = _place()
        for t in range(n):
            for j, chip in enumerate(chips):
                blk = outs[t].at[2 * chip[0] + chip[1], rows(t)]
                _remote(blk, blk, send.at[3 * t + j], recv.at[3 * t + j], (*chip, c)).wait_recv()
        for t in range(n):
            for j, chip in enumerate(chips):
                src = args[t].at[2 * chip[0] + chip[1], rows(t)]
                _remote(src, src, send.at[3 * t + j], recv.at[3 * t + j], (*chip, c)).wait_send()

    shapes = [jax.ShapeDtypeStruct(p.shape, p.dtype) for p in parts]
    return Comm(list(parts) + [landed[t] for t in prev], shapes, {n + i: t for i, t in enumerate(prev)}, 3 * n, start, finish)


def _swap_comm(gs):
    n = len(gs)

    def copy(args, outs, send, recv, t):
        _, _, c, _ = _place()
        rh = gs[t].shape[1] // 2
        x, y = lax.axis_index("x"), lax.axis_index("y")
        return _remote(args[t].at[:, pl.ds((1 - c) * rh, rh), :], outs[t], send.at[t], recv.at[t], (x, y, 1 - c))

    def start(args, outs, send, recv):
        for t in range(n):
            copy(args, outs, send, recv, t).start()

    def finish(args, outs, send, recv):
        for t in range(n):
            copy(args, outs, send, recv, t).wait()

    shapes = [jax.ShapeDtypeStruct((g.shape[0], g.shape[1] // 2, g.shape[2]), g.dtype) for g in gs]
    return Comm(gs, shapes, {}, n, start, finish)


def _join_comm(gs):
    n = len(gs)

    def half(outs, t, who):
        rh = gs[t].shape[0] // 2
        return outs[t].at[pl.ds(who * rh, rh), :]

    def start(args, outs, send, recv):
        x, y, c, _ = _place()
        for t in range(n):
            _remote(half(outs, t, c), half(outs, t, c), send.at[t], recv.at[t], (x, y, 1 - c)).start()

    def finish(args, outs, send, recv):
        x, y, c, _ = _place()
        for t in range(n):
            _remote(half(outs, t, 1 - c), half(outs, t, 1 - c), send.at[t], recv.at[t], (x, y, 1 - c)).wait_recv()
        for t in range(n):
            _remote(half(outs, t, c), half(outs, t, c), send.at[t], recv.at[t], (x, y, 1 - c)).wait_send()

    shapes = [jax.ShapeDtypeStruct(g.shape, g.dtype) for g in gs]
    return Comm(gs, shapes, {t: t for t in range(n)}, n, start, finish)


_DN = {"nn": (((1,), (0,)), ((), ())), "nt": (((1,), (1,)), ((), ())), "tn": (((0,), (0,)), ((), ()))}


def _mm(a, b, *, mode, out_dtype, name, out_groups=1, tm=1024, tn=1024, tk=2048, comm=None):
    ga, ra, ca = a.shape
    gb, rb, cb = b.shape
    if mode == "nn":
        M, K, N = ra, ga * ca, gb * cb
        assert rb == K and ga == 1 or (rb == K)
    elif mode == "nt":
        M, K, N = ra, ga * ca, rb
        assert gb * cb == K
    else:
        K, M, N = ra, ga * ca, gb * cb
        assert rb == K
    go = out_groups
    if mode == "nn":
        tk = _tile(ca, tk); assert rb % tk == 0 and (ga == 1 or True)
        tn = _tile(min(cb, N // go), tn); tm = _tile(M, tm, 8)
    elif mode == "nt":
        tk = _tile(ca, tk); tk = _tile(cb, tk) if cb % tk else tk; assert ca % tk == 0 and cb % tk == 0
        tn = _tile(N // go, tn); tm = _tile(M, tm, 8)
    else:
        tk = _tile(K, tk, 8); tm = _tile(ca, tm); tn = _tile(min(cb, N // go), tn)
    assert (N // go) % tn == 0 and M % tm == 0 and K % tk == 0, (name, M, N, K, tm, tn, tk)
    nk = K // tk
    kpa = max(ca // tk, 1)
    kpb = max(cb // tk, 1)
    npb = max(cb // tn, 1)
    npo = (N // go) // tn
    mpa = max(ca // tm, 1)

    if mode == "nn":
        a_spec = pl.BlockSpec((1, tm, tk), lambda j, i, k: (k // kpa, i, k % kpa))
        b_spec = pl.BlockSpec((1, tk, tn), lambda j, i, k: (j // npb, k, j % npb))
    elif mode == "nt":
        a_spec = pl.BlockSpec((1, tm, tk), lambda j, i, k: (k // kpa, i, k % kpa))
        b_spec = pl.BlockSpec((1, tn, tk), lambda j, i, k: (k // kpb, j, k % kpb))
    else:
        a_spec = pl.BlockSpec((1, tk, tm), lambda j, i, k: (i // mpa, k, i % mpa))
        b_spec = pl.BlockSpec((1, tk, tn), lambda j, i, k: (j // npb, k, j % npb))
    o_spec = pl.BlockSpec((1, tm, tn), lambda j, i, k: (j // npo, i, j % npo))
    dn = _DN[mode]

    def body(a_ref, b_ref, o_ref, *acc):
        p = lax.dot_general(a_ref[0], b_ref[0], dn, preferred_element_type=F32)
        if nk == 1:
            o_ref[0] = p.astype(out_dtype)
        else:
            k = pl.program_id(2)

            @pl.when(k == 0)
            def _():
                acc[0][...] = p

            @pl.when(k > 0)
            def _():
                acc[0][...] += p

            @pl.when(k == nk - 1)
            def _():
                o_ref[0] = acc[0][...].astype(out_dtype)

    return _call(
        body, name=name, grid=(N // tn, M // tm, nk), in_specs=[a_spec, b_spec], out_specs=[o_spec],
        out_shape=[jax.ShapeDtypeStruct((go, M, N // go), out_dtype)],
        scratch_shapes=[pltpu.VMEM((tm, tn), F32)] if nk > 1 else [],
        sem=("parallel", "parallel", "arbitrary"), args=(a, b), comm=comm)[0]


def _rows(tr, d):
    return pl.BlockSpec((tr, d), lambda i: (i, 0))


def _vec(d):
    return pl.BlockSpec((1, d), lambda i: (0, 0))


def _modulate(x, sc, sh, name):
    S, D = x.shape
    tr = min(256, S)

    def body(x_ref, sc_ref, sh_ref, h_ref):
        h_ref[...] = (x_ref[...] * (1.0 + sc_ref[...]) + sh_ref[...]).astype(BF16)

    return pl.pallas_call(
        body, name=name, grid=(S // tr,), in_specs=[_rows(tr, D), _vec(D), _vec(D)], out_specs=_rows(tr, D),
        out_shape=jax.ShapeDtypeStruct((S, D), BF16), compiler_params=_params(("parallel",)),
    )(x, sc, sh)


def _ln_fwd(x, y, gate, gamma, beta, sc, sh, name):
    S, D = x.shape
    tr = min(256, S)
    emit_h = sc is not None

    def body(*refs):
        if emit_h:
            x_ref, y_ref, g_ref, ga_ref, be_ref, sc_ref, sh_ref, xo_ref, xh_ref, rs_ref, h_ref = refs
        else:
            x_ref, y_ref, g_ref, ga_ref, be_ref, xo_ref, xh_ref, rs_ref = refs
        z = DEEPNORM_ALPHA * x_ref[...] + (1.0 + g_ref[...]) * y_ref[...]
        mu = jnp.mean(z, axis=-1, keepdims=True)
        zc = z - mu
        var = jnp.mean(zc * zc, axis=-1, keepdims=True)
        rstd = lax.rsqrt(var + LN_EPS)
        xh = zc * rstd
        xo = xh * ga_ref[...] + be_ref[...]
        xo_ref[...] = xo
        xh_ref[...] = xh
        rs_ref[...] = rstd
        if emit_h:
            h_ref[...] = (xo * (1.0 + sc_ref[...]) + sh_ref[...]).astype(BF16)

    ins = [x, y, gate, gamma, beta] + ([sc, sh] if emit_h else [])
    in_specs = [_rows(tr, D), _rows(tr, D)] + [_vec(D)] * (len(ins) - 2)
    out_shape = [jax.ShapeDtypeStruct((S, D), F32), jax.ShapeDtypeStruct((S, D), F32), jax.ShapeDtypeStruct((S, 1), F32)]
    out_specs = [_rows(tr, D), _rows(tr, D), _rows(tr, 1)]
    if emit_h:
        out_shape.append(jax.ShapeDtypeStruct((S, D), BF16))
        out_specs.append(_rows(tr, D))
    return pl.pallas_call(
        body, name=name, grid=(S // tr,), in_specs=in_specs, out_specs=out_specs, out_shape=out_shape,
        compiler_params=_params(("parallel",)),
    )(*ins)


def _loss_head(xf, tgt, name):
    S, D = xf.shape
    tr = min(256, S)

    def body(x_ref, t_ref, dx_ref, l_ref):
        e = x_ref[...] - t_ref[...]
        dx_ref[...] = e * (1.0 / D)

        @pl.when(pl.program_id(0) == 0)
        def _():
            l_ref[...] = jnp.zeros_like(l_ref)

        l_ref[...] += jnp.sum(e * e, axis=0, keepdims=True)

    return pl.pallas_call(
        body, name=name, grid=(S // tr,), in_specs=[_rows(tr, D), _rows(tr, D)],
        out_specs=[_rows(tr, D), _vec(D)],
        out_shape=[jax.ShapeDtypeStruct((S, D), F32), jax.ShapeDtypeStruct((1, D), F32)],
        compiler_params=_params(("arbitrary",)),
    )(xf, tgt)


def _ln_bwd(dxo, xh, rstd, gamma, y, gate, name):
    S, D = dxo.shape
    tr = min(256, S)

    def body(dx_ref, xh_ref, rs_ref, ga_ref, y_ref, g_ref, dres_ref, dy_ref, dga_ref, dbe_ref, dg_ref):
        dxo_ = dx_ref[...]
        xh_ = xh_ref[...]
        dxh = dxo_ * ga_ref[...]
        m1 = jnp.mean(dxh, axis=-1, keepdims=True)
        m2 = jnp.mean(dxh * xh_, axis=-1, keepdims=True)
        dz = rs_ref[...] * (dxh - m1 - xh_ * m2)
        dres_ref[...] = DEEPNORM_ALPHA * dz
        dy_ref[...] = ((1.0 + g_ref[...]) * dz).astype(BF16)

        @pl.when(pl.program_id(0) == 0)
        def _():
            dga_ref[...] = jnp.zeros_like(dga_ref)
            dbe_ref[...] = jnp.zeros_like(dbe_ref)
            dg_ref[...] = jnp.zeros_like(dg_ref)

        dga_ref[...] += jnp.sum(dxo_ * xh_, axis=0, keepdims=True)
        dbe_ref[...] += jnp.sum(dxo_, axis=0, keepdims=True)
        dg_ref[...] += jnp.sum(dz * y_ref[...], axis=0, keepdims=True)

    return pl.pallas_call(
        body, name=name, grid=(S // tr,),
        in_specs=[_rows(tr, D), _rows(tr, D), _rows(tr, 1), _vec(D), _rows(tr, D), _vec(D)],
        out_specs=[_rows(tr, D), _rows(tr, D), _vec(D), _vec(D), _vec(D)],
        out_shape=[jax.ShapeDtypeStruct((S, D), F32), jax.ShapeDtypeStruct((S, D), BF16)] + [jax.ShapeDtypeStruct((1, D), F32)] * 3,
        compiler_params=_params(("arbitrary",)),
    )(dxo, xh, rstd, gamma, y, gate)


def _mod_bwd(dh, x, sc, dres, name):
    S, D = x.shape
    tr = min(256, S)

    def body(dh_ref, x_ref, sc_ref, dr_ref, dx_ref, dsc_ref, dsh_ref):
        dh_ = dh_ref[...]
        dx_ref[...] = dr_ref[...] + dh_ * (1.0 + sc_ref[...])

        @pl.when(pl.program_id(0) == 0)
        def _():
            dsc_ref[...] = jnp.zeros_like(dsc_ref)
            dsh_ref[...] = jnp.zeros_like(dsh_ref)

        dsc_ref[...] += jnp.sum(dh_ * x_ref[...], axis=0, keepdims=True)
        dsh_ref[...] += jnp.sum(dh_, axis=0, keepdims=True)

    return pl.pallas_call(
        body, name=name, grid=(S // tr,),
        in_specs=[_rows(tr, D), _rows(tr, D), _vec(D), _rows(tr, D)],
        out_specs=[_rows(tr, D), _vec(D), _vec(D)],
        out_shape=[jax.ShapeDtypeStruct((S, D), F32), jax.ShapeDtypeStruct((1, D), F32), jax.ShapeDtypeStruct((1, D), F32)],
        compiler_params=_params(("arbitrary",)),
    )(dh, x, sc, dres)


def _log_sigmoid(z):
    return jnp.minimum(z, 0.0) - jnp.log(1.0 + jnp.exp(-jnp.abs(z)))


def _fox_gate_fwd(proj, b_f, n_heads, name):
    S, PW = proj.shape
    blk = min(256, S)
    last = PW // LANES - 1

    def body(fl_ref, b_ref, cum_ref):
        r = lax.broadcasted_iota(jnp.int32, (blk, blk), 0)
        c = lax.broadcasted_iota(jnp.int32, (blk, blk), 1)
        tril = (c <= r).astype(F32)
        carry = jnp.zeros((1, LANES), F32)
        for i in range(S // blk):
            lf = _log_sigmoid(fl_ref[i * blk:(i + 1) * blk, :] + b_ref[...])
            cum_ref[i * blk:(i + 1) * blk, :] = jnp.dot(tril, lf, preferred_element_type=F32, precision=HIGHEST) + carry
            carry = carry + jnp.sum(lf, axis=0, keepdims=True)

    return pl.pallas_call(
        body, name=name, grid=(1,),
        in_specs=[pl.BlockSpec((S, LANES), lambda i: (0, last)), pl.BlockSpec((1, LANES), lambda i: (0, 0))],
        out_specs=pl.BlockSpec((S, LANES), lambda i: (0, 0)),
        out_shape=jax.ShapeDtypeStruct((S, LANES), F32), compiler_params=_params(("arbitrary",)),
    )(proj, b_f)


def _fox_gate_bwd(dcum, proj, b_f, n_heads, name):
    S, PW = proj.shape
    blk = min(256, S)
    last = PW // LANES - 1
    nb = S // blk

    def body(dc_ref, fl_ref, b_ref, dfl_ref, db_ref):
        r = lax.broadcasted_iota(jnp.int32, (blk, blk), 0)
        c = lax.broadcasted_iota(jnp.int32, (blk, blk), 1)
        triu = (c >= r).astype(F32)
        lane = lax.broadcasted_iota(jnp.int32, (blk, LANES), 1)
        carry = jnp.zeros((1, LANES), F32)
        dbs = jnp.zeros((1, LANES), F32)
        for i in reversed(range(nb)):
            dc = dc_ref[i * blk:(i + 1) * blk, :]
            dlf = jnp.dot(triu, dc, preferred_element_type=F32, precision=HIGHEST) + carry
            carry = carry + jnp.sum(dc, axis=0, keepdims=True)
            z = fl_ref[i * blk:(i + 1) * blk, :] + b_ref[...]
            e = jnp.exp(-jnp.abs(z))
            sig_neg = jnp.where(z >= 0, e / (1.0 + e), 1.0 / (1.0 + e))
            dfl = jnp.where(lane < n_heads, dlf * sig_neg, 0.0)
            dfl_ref[i * blk:(i + 1) * blk, :] = dfl.astype(BF16)
            dbs = dbs + jnp.sum(dfl, axis=0, keepdims=True)
        db_ref[...] = dbs

    return pl.pallas_call(
        body, name=name, grid=(1,),
        in_specs=[pl.BlockSpec((S, LANES), lambda i: (0, 0)), pl.BlockSpec((S, LANES), lambda i: (0, last)),
                  pl.BlockSpec((1, LANES), lambda i: (0, 0))],
        out_specs=[pl.BlockSpec((S, LANES), lambda i: (0, 0)), pl.BlockSpec((1, LANES), lambda i: (0, 0))],
        out_shape=[jax.ShapeDtypeStruct((S, LANES), BF16), jax.ShapeDtypeStruct((1, LANES), F32)],
        compiler_params=_params(("arbitrary",)),
    )(dcum, proj, b_f)


def _fox_scores(q_ref, kb_ref, cq_ref, ck_ref, qi, tq, scale):
    kk = (qi + 1) * tq
    rows = slice(qi * tq, (qi + 1) * tq)
    qb = q_ref[rows, :].astype(BF16)
    s = lax.dot_general(qb, kb_ref[0:kk, :], _DN["nt"], preferred_element_type=F32) * scale
    s = s + (cq_ref[0, rows, :] - ck_ref[0, :, 0:kk])
    r = lax.broadcasted_iota(jnp.int32, (tq, kk), 0) + qi * tq
    c = lax.broadcasted_iota(jnp.int32, (tq, kk), 1)
    mask = c <= r
    return jnp.where(mask, s, NEG), mask, qb


def _fox_fwd(proj, cq, ck, n_heads, name, comm=None):
    S = proj.shape[0]
    H = n_heads
    tq = min(FOX_TQ, S)
    nq = S // tq
    scale = FDH ** -0.5

    def body(q_ref, k_ref, v_ref, cq_ref, ck_ref, o_ref, lse_ref, kb_ref, vb_ref):
        kb_ref[...] = k_ref[...].astype(BF16)
        vb_ref[...] = v_ref[...].astype(BF16)
        for qi in range(nq):
            kk = (qi + 1) * tq
            rows = slice(qi * tq, (qi + 1) * tq)
            s, _, _ = _fox_scores(q_ref, kb_ref, cq_ref, ck_ref, qi, tq, scale)
            m = jnp.max(s, axis=-1, keepdims=True)
            p = jnp.exp(s - m)
            l = jnp.sum(p, axis=-1, keepdims=True)
            p = p * (1.0 / l)
            o_ref[rows, :] = jnp.dot(p.astype(BF16), vb_ref[0:kk, :], preferred_element_type=F32).astype(BF16)
            lse_ref[0, rows, :] = m + jnp.log(l)

    col = lambda off: pl.BlockSpec((S, FDH), lambda h: (0, h + off))
    stat_c = pl.BlockSpec((1, S, 1), lambda h: (h, 0, 0))
    stat_r = pl.BlockSpec((1, 1, S), lambda h: (h, 0, 0))
    return _call(
        body, name=name, grid=(H,),
        in_specs=[col(0), col(H), col(2 * H), stat_c, stat_r],
        out_specs=[col(0), stat_c],
        out_shape=[jax.ShapeDtypeStruct((S, H * FDH), BF16), jax.ShapeDtypeStruct((H, S, 1), F32)],
        scratch_shapes=[pltpu.VMEM((S, FDH), BF16), pltpu.VMEM((S, FDH), BF16)],
        sem=("parallel",), args=(proj, proj, proj, cq, ck), comm=comm)


def _fox_bwd(proj, cq, ck, lse, do, n_heads, name, comm=None):
    S = proj.shape[0]
    H = n_heads
    tq = min(FOX_TQ, S)
    nq = S // tq
    scale = FDH ** -0.5

    def body(q_ref, k_ref, v_ref, cq_ref, ck_ref, lse_ref, do_ref, dq_ref, dk_ref, dv_ref, dcq_ref, dck_ref,
             kb_ref, vb_ref, dka_ref, dva_ref):
        kb_ref[...] = k_ref[...].astype(BF16)
        vb_ref[...] = v_ref[...].astype(BF16)
        dka_ref[...] = jnp.zeros_like(dka_ref)
        dva_ref[...] = jnp.zeros_like(dva_ref)
        dck_ref[...] = jnp.zeros_like(dck_ref)
        for qi in range(nq):
            kk = (qi + 1) * tq
            rows = slice(qi * tq, (qi + 1) * tq)
            s, mask, qb = _fox_scores(q_ref, kb_ref, cq_ref, ck_ref, qi, tq, scale)
            p = jnp.where(mask, jnp.exp(s - lse_ref[0, rows, :]), 0.0)
            dob = do_ref[rows, :]
            dp = lax.dot_general(dob, vb_ref[0:kk, :], _DN["nt"], preferred_element_type=F32)
            delta = jnp.sum(p * dp, axis=-1, keepdims=True)
            ds = p * (dp - delta)
            dcq_ref[0, rows, :] = jnp.sum(ds, axis=-1, keepdims=True)
            dck_ref[0, :, 0:kk] -= jnp.sum(ds, axis=0, keepdims=True)
            dsb = (ds * scale).astype(BF16)
            dq_ref[rows, :] = jnp.dot(dsb, kb_ref[0:kk, :], preferred_element_type=F32).astype(BF16)
            dka_ref[0:kk, :] += lax.dot_general(dsb, qb, _DN["tn"], preferred_element_type=F32)
            dva_ref[0:kk, :] += lax.dot_general(p.astype(BF16), dob, _DN["tn"], preferred_element_type=F32)
        dk_ref[...] = dka_ref[...].astype(BF16)
        dv_ref[...] = dva_ref[...].astype(BF16)

    col = lambda off: pl.BlockSpec((S, FDH), lambda h: (0, h + off))
    stat_c = pl.BlockSpec((1, S, 1), lambda h: (h, 0, 0))
    stat_r = pl.BlockSpec((1, 1, S), lambda h: (h, 0, 0))
    wide = jax.ShapeDtypeStruct((S, H * FDH), BF16)
    return _call(
        body, name=name, grid=(H,),
        in_specs=[col(0), col(H), col(2 * H), stat_c, stat_r, stat_c, col(0)],
        out_specs=[col(0), col(0), col(0), stat_c, stat_r],
        out_shape=[wide, wide, wide, jax.ShapeDtypeStruct((H, S, 1), F32), jax.ShapeDtypeStruct((H, 1, S), F32)],
        scratch_shapes=[pltpu.VMEM((S, FDH), BF16), pltpu.VMEM((S, FDH), BF16), pltpu.VMEM((S, FDH), F32), pltpu.VMEM((S, FDH), F32)],
        sem=("parallel",), args=(proj, proj, proj, cq, ck, lse, do), comm=comm)


def _rope_tables(pos, sign):
    inv = ROPE_THETA ** (-jnp.arange(0, ROPE_DIM, 2, dtype=F32) / ROPE_DIM)
    ang = pos.astype(F32)[:, None] * inv
    cos, sin = jnp.cos(ang), sign * jnp.sin(ang)
    l64 = jnp.arange(LANES) % SDH
    idx = l64 % (ROPE_DIM // 2)
    c = jnp.where(l64 < ROPE_DIM, cos[:, idx], 1.0)
    sa = jnp.where(l64 < ROPE_DIM // 2, -sin[:, idx], 0.0)
    sb = jnp.where((l64 >= ROPE_DIM // 2) & (l64 < ROPE_DIM), sin[:, idx], 0.0)
    rot = jnp.stack([c, sa, sb])
    ident = jnp.stack([jnp.ones_like(c), jnp.zeros_like(c), jnp.zeros_like(c)])
    return jnp.stack([rot, ident]).astype(F32)


def _rope(xin, tabs, n_rot, out_dtype, name):
    S, W = xin.shape

    def body(x_ref, t_ref, o_ref):
        xv = x_ref[...]
        o = xv * t_ref[0, 0] + pltpu.roll(xv, LANES - ROPE_DIM // 2, 1) * t_ref[0, 1] + pltpu.roll(xv, ROPE_DIM // 2, 1) * t_ref[0, 2]
        o_ref[...] = o.astype(out_dtype)

    return pl.pallas_call(
        body, name=name, grid=(W // LANES,),
        in_specs=[pl.BlockSpec((S, LANES), lambda j: (0, j)),
                  pl.BlockSpec((1, 3, S, LANES), lambda j: (jnp.where(j < n_rot, 0, 1), 0, 0, 0))],
        out_specs=pl.BlockSpec((S, LANES), lambda j: (0, j)),
        out_shape=jax.ShapeDtypeStruct((S, W), out_dtype), compiler_params=_params(("parallel",)),
    )(xin, tabs)


def _swa_probs(q_ref, k_ref, sk_ref, n, scale):
    st = pl.multiple_of(jnp.maximum(n - 1, 0) * WIN, WIN)
    qb = q_ref[0, 0]
    kb = k_ref[0, pl.ds(st, 2 * WIN), :]
    gm = qb.shape[0]
    s = lax.dot_general(qb, kb, _DN["nt"], preferred_element_type=F32) * scale
    qa = n * WIN + (lax.broadcasted_iota(jnp.int32, (gm, 2 * WIN), 0) & (WIN - 1))
    ka = st + lax.broadcasted_iota(jnp.int32, (gm, 2 * WIN), 1)
    valid = (ka <= qa) & (qa - ka < WIN)
    s = jnp.where(valid, s, NEG)
    sink = sk_ref[0]
    m = jnp.maximum(jnp.max(s, axis=-1, keepdims=True), sink)
    e = jnp.where(valid, jnp.exp(s - m), 0.0)
    es = jnp.exp(sink - m)
    inv = 1.0 / (jnp.sum(e, axis=-1, keepdims=True) + es)
    return e * inv, es * inv, st, qb, kb


def _swa_specs(S, gm):
    blk = pl.BlockSpec((1, 1, gm, SDH), lambda g, n: (g, n, 0, 0))
    kv = pl.BlockSpec((1, S, SDH), lambda g, n: (g, 0, 0))
    col = pl.BlockSpec((1, gm, 1), lambda g, n: (g, 0, 0))
    return blk, kv, col


def _swa_fwd(q, k, v, sinks, name, comm=None):
    KH, nb, gm, _ = q.shape
    S = k.shape[1]
    scale = SDH ** -0.5

    def body(q_ref, k_ref, v_ref, sk_ref, o_ref):
        p, _, st, _, _ = _swa_probs(q_ref, k_ref, sk_ref, pl.program_id(1), scale)
        vb = v_ref[0, pl.ds(st, 2 * WIN), :]
        o_ref[0, 0] = jnp.dot(p.astype(BF16), vb, preferred_element_type=F32).astype(BF16)

    blk, kv, col = _swa_specs(S, gm)
    return _call(
        body, name=name, grid=(KH, nb), in_specs=[blk, kv, kv, col], out_specs=[blk],
        out_shape=[jax.ShapeDtypeStruct(q.shape, BF16)], sem=("parallel", "parallel"), args=(q, k, v, sinks), comm=comm)[0]


def _swa_bwd(q, k, v, sinks, do, name, comm=None):
    KH, nb, gm, _ = q.shape
    S = k.shape[1]
    scale = SDH ** -0.5

    def body(q_ref, k_ref, v_ref, sk_ref, do_ref, dq_ref, dk_ref, dv_ref, dsk_ref):
        n = pl.program_id(1)

        @pl.when(n == 0)
        def _():
            dk_ref[...] = jnp.zeros_like(dk_ref)
            dv_ref[...] = jnp.zeros_like(dv_ref)
            dsk_ref[...] = jnp.zeros_like(dsk_ref)

        p, ps, st, qb, kb = _swa_probs(q_ref, k_ref, sk_ref, n, scale)
        vb = v_ref[0, pl.ds(st, 2 * WIN), :]
        dob = do_ref[0, 0]
        dp = lax.dot_general(dob, vb, _DN["nt"], preferred_element_type=F32)
        delta = jnp.sum(p * dp, axis=-1, keepdims=True)
        ds = p * (dp - delta)
        dsb = (ds * scale).astype(BF16)
        dq_ref[0, 0] = jnp.dot(dsb, kb, preferred_element_type=F32)
        dk_ref[0, pl.ds(st, 2 * WIN), :] += lax.dot_general(dsb, qb, _DN["tn"], preferred_element_type=F32)
        dv_ref[0, pl.ds(st, 2 * WIN), :] += lax.dot_general(p.astype(BF16), dob, _DN["tn"], preferred_element_type=F32)
        dsk_ref[0] -= ps * delta

    blk, kv, col = _swa_specs(S, gm)
    return _call(
        body, name=name, grid=(KH, nb), in_specs=[blk, kv, kv, col, blk], out_specs=[blk, kv, kv, col],
        out_shape=[jax.ShapeDtypeStruct(q.shape, F32), jax.ShapeDtypeStruct(k.shape, F32),
                   jax.ShapeDtypeStruct(k.shape, F32), jax.ShapeDtypeStruct(sinks.shape, F32)],
        sem=("parallel", "arbitrary"), args=(q, k, v, sinks, do), comm=comm)


def _shift_down(u, k):
    row = lax.broadcasted_iota(jnp.int32, u.shape, 0)
    return jnp.where(row >= k, pltpu.roll(u, k, 0), 0.0)


def _shift_up(u, k):
    n = u.shape[0]
    row = lax.broadcasted_iota(jnp.int32, u.shape, 0)
    return jnp.where(row < n - k, pltpu.roll(u, n - k, 0), 0.0)


def _conv3(u, w_ref, b_ref):
    return w_ref[0:1, :] * _shift_down(u, 2) + w_ref[1:2, :] * _shift_down(u, 1) + w_ref[2:3, :] * u + b_ref[...]


def _conv_gate(u, cw, cb, name, comm=None):
    S, F2 = u.shape
    Fh = F2 // 2
    tc = _tile(Fh, 256)
    nf = Fh // tc

    def body(ug_ref, uv_ref, wg_ref, wv_ref, bg_ref, bv_ref, a_ref):
        g = _conv3(ug_ref[...], wg_ref, bg_ref)
        val = _conv3(uv_ref[...], wv_ref, bv_ref)
        a_ref[...] = (g * (1.0 / (1.0 + jnp.exp(-g))) * val).astype(BF16)

    blk = lambda r, off: pl.BlockSpec((r, tc), lambda j: (0, j + off))
    return _call(
        body, name=name, grid=(nf,),
        in_specs=[blk(S, 0), blk(S, nf), blk(3, 0), blk(3, nf), blk(1, 0), blk(1, nf)], out_specs=[blk(S, 0)],
        out_shape=[jax.ShapeDtypeStruct((S, Fh), BF16)], sem=("parallel",), args=(u, u, cw, cw, cb, cb), comm=comm)[0]


def _conv_gate_bwd(u, da, cw, cb, name, comm=None):
    S, F2 = u.shape
    Fh = F2 // 2
    tc = _tile(Fh, 256)
    nf = Fh // tc

    def half(h, dx, uu, w_ref, du_ref, dw_ref, db_ref):
        du = w_ref[2:3, :] * dx + w_ref[1:2, :] * _shift_up(dx, 1) + w_ref[0:1, :] * _shift_up(dx, 2)
        du_ref[h] = du.astype(BF16)
        dw_ref[h, 0:1, :] = jnp.sum(dx * _shift_down(uu, 2), axis=0, keepdims=True)
        dw_ref[h, 1:2, :] = jnp.sum(dx * _shift_down(uu, 1), axis=0, keepdims=True)
        dw_ref[h, 2:3, :] = jnp.sum(dx * uu, axis=0, keepdims=True)
        db_ref[h] = jnp.sum(dx, axis=0, keepdims=True)

    def body(ug_ref, uv_ref, da_ref, wg_ref, wv_ref, bg_ref, bv_ref, du_ref, dw_ref, db_ref):
        ug = ug_ref[...]
        uv = uv_ref[...]
        g = _conv3(ug, wg_ref, bg_ref)
        val = _conv3(uv, wv_ref, bv_ref)
        sig = 1.0 / (1.0 + jnp.exp(-g))
        da_ = da_ref[...]
        dg = da_ * val * (sig * (1.0 + g * (1.0 - sig)))
        dval = da_ * (g * sig)
        half(0, dg, ug, wg_ref, du_ref, dw_ref, db_ref)
        half(1, dval, uv, wv_ref, du_ref, dw_ref, db_ref)

    blk = lambda r, off: pl.BlockSpec((r, tc), lambda j: (0, j + off))
    both = lambda r: pl.BlockSpec((2, r, tc), lambda j: (0, 0, j))
    return _call(
        body, name=name, grid=(nf,),
        in_specs=[blk(S, 0), blk(S, nf), blk(S, 0), blk(3, 0), blk(3, nf), blk(1, 0), blk(1, nf)],
        out_specs=[both(S), both(3), both(1)],
        out_shape=[jax.ShapeDtypeStruct((2, S, Fh), BF16), jax.ShapeDtypeStruct((2, 3, Fh), F32), jax.ShapeDtypeStruct((2, 1, Fh), F32)],
        sem=("parallel",), args=(u, u, da, cw, cw, cb, cb), comm=comm)


def _to_groups(t, kh):
    S, width = t.shape
    g = width // SDH // kh
    return t.reshape(S // WIN, WIN, kh, g, SDH).transpose(2, 0, 3, 1, 4).reshape(kh, S // WIN, g * WIN, SDH)


def _from_groups(t):
    kh, nb, gm, _ = t.shape
    g = gm // WIN
    return t.reshape(kh, nb, g, WIN, SDH).transpose(1, 3, 0, 2, 4).reshape(nb * WIN, kh * g * SDH)


class LocalWeights:
    def __init__(self, weights):
        self.weights, self.grads = weights, {}

    def w(self, name):
        return self.weights[name]

    def carry(self, stage):
        return None

    def carried(self, stage, comm):
        pass

    def grad(self, name, g):
        self.grads[name] = g


def _local_step(dm, x, tgt, pos, mod, sp, pp):
    S, D, FH, QH, KH, Fh = dm
    m = [[mod[i:i + 1, j * D:(j + 1) * D] for j in range(6)] for i in range(DEPTH)]

    def run(fn, *args, name, **kw):
        comm = pp.carry(name)
        out = fn(*args, name=name, comm=comm, **kw)
        if comm is not None:
            pp.carried(name, comm)
        return out

    sv = []
    xs = x
    h = _modulate(xs, m[0][1], m[0][0], "mod_in")
    for i in range(DEPTH):
        sh1, sc1, g1, sh2, sc2, g2 = m[i]
        L = {}
        L["x_in"], L["h1"] = xs, h
        if i == 0:
            proj = run(_mm, h[None], pp.w("fox_w_in"), mode="nn", out_dtype=F32, name="fox_proj", tn=896)[0]
            cum = _fox_gate_fwd(proj, sp["fox_b_f"], FH, "fox_gate")
            cq = cum[:, :FH].T[:, :, None]
            ck = cum[:, :FH].T[:, None, :]
            o, lse = run(_fox_fwd, proj, cq, ck, FH, name="fox_attn")
            L.update(proj=proj, cq=cq, ck=ck, lse=lse, o=o)
            y = run(_mm, o[None], pp.w("fox_w_o"), mode="nn", out_dtype=F32, name="fox_out")[0]
        else:
            proj = run(_mm, h[None], pp.w("swa_w_in"), mode="nn", out_dtype=F32, name="swa_proj", tn=640)[0]
            tabs = _rope_tables(pos, 1.0)
            n_rot = (QH + KH) * SDH // LANES
            pr = _rope(proj, tabs, n_rot, BF16, "swa_rope")
            qh = _to_groups(pr[:, :QH * SDH], KH)
            kh = pr[:, QH * SDH:(QH + KH) * SDH].reshape(S, KH, SDH).transpose(1, 0, 2)
            vh = pr[:, (QH + KH) * SDH:].reshape(S, KH, SDH).transpose(1, 0, 2)
            oh = run(_swa_fwd, qh, kh, vh, sp["sinks"], name="swa_attn")
            o = _from_groups(oh)
            L.update(qh=qh, kh=kh, vh=vh, o=o)
            y = run(_mm, o[None], pp.w("swa_w_o"), mode="nn", out_dtype=F32, name="swa_out")[0]
        L["y1"] = y
        x1, L["xh1"], L["rs1"], h2 = _ln_fwd(xs, y, g1, sp["ln_mix_g"][i], sp["ln_mix_b"][i], sc2, sh2, f"ln_mix{i}")
        L["x1"], L["h2"] = x1, h2
        u = run(_mm, h2[None], pp.w(f"ffn_w_up{i}"), mode="nn", out_dtype=F32, name=f"ffn_up{i}", tm=512, tn=1408)[0]
        a = run(_conv_gate, u, sp["conv_w"][i], sp["conv_b"][i], name=f"ffn_gate{i}")
        y2 = run(_mm, a[None], pp.w(f"ffn_w_down{i}"), mode="nn", out_dtype=F32, name=f"ffn_down{i}", tk=1408)[0]
        L.update(u=u, a=a, y2=y2)
        if i + 1 < DEPTH:
            xs, L["xh2"], L["rs2"], h = _ln_fwd(x1, y2, g2, sp["ln_ffn_g"][i], sp["ln_ffn_b"][i], m[i + 1][1], m[i + 1][0], f"ln_ffn{i}")
        else:
            xs, L["xh2"], L["rs2"] = _ln_fwd(x1, y2, g2, sp["ln_ffn_g"][i], sp["ln_ffn_b"][i], None, None, f"ln_ffn{i}")
        sv.append(L)

    dx, loss_cols = _loss_head(xs, tgt, "loss_head")

    gs = {k: [None] * DEPTH for k in ("conv_w", "conv_b", "ln_mix_g", "ln_mix_b", "ln_ffn_g", "ln_ffn_b")}
    dmod = [None] * DEPTH
    for i in reversed(range(DEPTH)):
        sh1, sc1, g1, sh2, sc2, g2 = m[i]
        L = sv[i]
        dres, dy, gs["ln_ffn_g"][i], gs["ln_ffn_b"][i], dg2 = _ln_bwd(dx, L["xh2"], L["rs2"], sp["ln_ffn_g"][i], L["y2"], g2, f"ln_ffn_bwd{i}")
        da = run(_mm, dy[None], pp.w(f"ffn_w_down{i}"), mode="nt", out_dtype=F32, name=f"ffn_down_dx{i}", tm=512, tn=1408)[0]
        pp.grad(f"ffn_w_down{i}", run(_mm, L["a"][None], dy[None], mode="tn", out_dtype=BF16, name=f"ffn_down_dw{i}", tm=1408))
        du, dcw, dcb = run(_conv_gate_bwd, L["u"], da, sp["conv_w"][i], sp["conv_b"][i], name=f"ffn_gate_bwd{i}")
        gs["conv_w"][i] = dcw.transpose(1, 0, 2).reshape(3, 2 * Fh)
        gs["conv_b"][i] = dcb.transpose(1, 0, 2).reshape(1, 2 * Fh)
        dh2 = run(_mm, du, pp.w(f"ffn_w_up{i}"), mode="nt", out_dtype=F32, name=f"ffn_up_dx{i}", tk=1408)[0]
        pp.grad(f"ffn_w_up{i}", run(_mm, L["h2"][None], du, mode="tn", out_dtype=BF16, name=f"ffn_up_dw{i}", out_groups=N_CHIPS, tn=1408))
        dx, dsc2, dsh2 = _mod_bwd(dh2, L["x1"], sc2, dres, f"mod_ffn_bwd{i}")
        dres, dy, gs["ln_mix_g"][i], gs["ln_mix_b"][i], dg1 = _ln_bwd(dx, L["xh1"], L["rs1"], sp["ln_mix_g"][i], L["y1"], g1, f"ln_mix_bwd{i}")
        if i == 0:
            do = run(_mm, dy[None], pp.w("fox_w_o"), mode="nt", out_dtype=BF16, name="fox_out_dx")[0]
            pp.grad("fox_w_o", run(_mm, L["o"][None], dy[None], mode="tn", out_dtype=BF16, name="fox_out_dw"))
            dq, dk, dv, dcq, dck = run(_fox_bwd, L["proj"], L["cq"], L["ck"], L["lse"], do, FH, name="fox_attn_bwd")
            dcum = dcq[:, :, 0].T + dck[:, 0, :].T
            dcum = jnp.pad(dcum, ((0, 0), (0, LANES - FH)))
            dfl, db_f = _fox_gate_bwd(dcum, L["proj"], sp["fox_b_f"], FH, "fox_gate_bwd")
            gs["fox_b_f"] = db_f
            dproj = jnp.concatenate([dq, dk, dv, dfl], axis=1)
            dh1 = run(_mm, dproj[None], pp.w("fox_w_in"), mode="nt", out_dtype=F32, name="fox_proj_dx", tk=896)[0]
            pp.grad("fox_w_in", run(_mm, L["h1"][None], dproj[None], mode="tn", out_dtype=BF16, name="fox_proj_dw", tn=896))
        else:
            do = run(_mm, dy[None], pp.w("swa_w_o"), mode="nt", out_dtype=BF16, name="swa_out_dx")[0]
            pp.grad("swa_w_o", run(_mm, L["o"][None], dy[None], mode="tn", out_dtype=BF16, name="swa_out_dw"))
            dqh, dkh, dvh, dsk = run(_swa_bwd, L["qh"], L["kh"], L["vh"], sp["sinks"], _to_groups(do, KH), name="swa_attn_bwd")
            gs["sinks"] = jnp.sum(dsk.reshape(QH, WIN), axis=1)
            dpr = jnp.concatenate([_from_groups(dqh), dkh.transpose(1, 0, 2).reshape(S, KH * SDH),
                                   dvh.transpose(1, 0, 2).reshape(S, KH * SDH)], axis=1)
            n_rot = (QH + KH) * SDH // LANES
            dproj = _rope(dpr, _rope_tables(pos, -1.0), n_rot, BF16, "swa_rope_bwd")
            dh1 = run(_mm, dproj[None], pp.w("swa_w_in"), mode="nt", out_dtype=F32, name="swa_proj_dx", tk=640)[0]
            pp.grad("swa_w_in", run(_mm, L["h1"][None], dproj[None], mode="tn", out_dtype=BF16, name="swa_proj_dw", out_groups=N_CHIPS, tn=640))
        dx, dsc1, dsh1 = _mod_bwd(dh1, L["x_in"], sc1, dres, f"mod_mix_bwd{i}")
        dmod[i] = jnp.concatenate([dsh1, dsc1, dg1, dsh2, dsc2, dg2], axis=1)
    return loss_cols, dx, gs, jnp.concatenate(dmod, axis=0)


def _allgather_small(v, name):
    m_per, n = v.shape

    def body(x_ref, out_ref, send_sems, recv_sems, local_sem):
        x, y, c, chips = _place()
        me, sibling = (x, y, c), (x, y, 1 - c)

        def rows(px, py, pc):
            return out_ref.at[pl.ds((4 * px + 2 * py + pc) * m_per, m_per), :]

        def copy(k, block, to, src=None):
            return _remote(rows(*block) if src is None else src, rows(*block), send_sems.at[k], recv_sems.at[k], to)

        mine = pltpu.make_async_copy(x_ref, rows(*me), local_sem)
        mine.start()
        first = [copy(0, me, sibling, src=x_ref)]
        first += [copy(1 + j, me, (*chip, c), src=x_ref) for j, chip in enumerate(chips)]
        for cp in first:
            cp.start()
        passed = [copy(4 + j, (*chip, c), sibling) for j, chip in enumerate(chips)]
        for j, chip in enumerate(chips):
            copy(1 + j, (*chip, c), me).wait_recv()
            passed[j].start()
        copy(0, sibling, me).wait_recv()
        for j, chip in enumerate(chips):
            copy(4 + j, (*chip, 1 - c), me).wait_recv()
        for cp in first + passed:
            cp.wait_send()
        mine.wait()

    return pl.pallas_call(
        body, name=name, out_shape=jax.ShapeDtypeStruct((N_DEV * m_per, n), v.dtype),
        in_specs=[pl.BlockSpec(memory_space=pltpu.VMEM)], out_specs=pl.BlockSpec(memory_space=pltpu.VMEM),
        scratch_shapes=[pltpu.SemaphoreType.DMA((7,)), pltpu.SemaphoreType.DMA((7,)), pltpu.SemaphoreType.DMA],
        compiler_params=pltpu.CompilerParams(vmem_limit_bytes=VMEM_LIMIT),
    )(v)


def _row_tile(r, pref=256):
    return _tile(r, pref, 16)


def _cast_bf16(w, layer, chip, name):
    _, R, C = w.shape
    tr = _row_tile(R)

    def body(s_ref, w_ref, o_ref):
        o_ref[...] = w_ref[...].astype(BF16)

    return pl.pallas_call(
        body, name=name,
        grid_spec=pltpu.PrefetchScalarGridSpec(
            num_scalar_prefetch=1, grid=(R // tr,),
            in_specs=[pl.BlockSpec((None, tr, C), lambda i, s: (layer, i, 0))],
            out_specs=pl.BlockSpec((None, tr, C), lambda i, s: (s[0], i, 0))),
        out_shape=jax.ShapeDtypeStruct((N_CHIPS, R, C), BF16), compiler_params=_params(("parallel",)),
    )(jnp.reshape(chip, (1,)).astype(jnp.int32), w)


def _add_sibling(g, got, c, name):
    G, R, C = g.shape
    rh = R // 2
    tr = _row_tile(rh)
    nb = rh // tr

    def body(c_ref, g_ref, o_ref, p_ref):
        p_ref[...] = (g_ref[...].astype(F32) + o_ref[...].astype(F32)).astype(BF16)

    return pl.pallas_call(
        body, name=name,
        grid_spec=pltpu.PrefetchScalarGridSpec(
            num_scalar_prefetch=1, grid=(G, nb),
            in_specs=[pl.BlockSpec((1, tr, C), lambda s, i, c_ref: (s, c_ref[0] * nb + i, 0)),
                      pl.BlockSpec((1, tr, C), lambda s, i, c_ref: (s, i, 0))],
            out_specs=pl.BlockSpec((1, tr, C), lambda s, i, c_ref: (s, i, 0))),
        out_shape=jax.ShapeDtypeStruct((G, rh, C), BF16), compiler_params=_params(("parallel", "parallel")),
    )(jnp.reshape(c, (1,)).astype(jnp.int32), g, got)


def _sum_chips(part, landed, chip, c, name):
    G, rh, C = part.shape
    tr = _row_tile(rh)
    nb = rh // tr

    def body(p_ref, own_ref, *rest):
        acc = own_ref[...].astype(F32)
        for ref in rest[:G - 1]:
            acc = acc + ref[...].astype(F32)
        rest[G - 1][...] = acc

    slot = lambda k: pl.BlockSpec((None, tr, C), lambda i, p: ((p[0] + k) % G, i, 0))
    return pl.pallas_call(
        body, name=name,
        grid_spec=pltpu.PrefetchScalarGridSpec(
            num_scalar_prefetch=1, grid=(nb,), in_specs=[slot(k) for k in range(G)],
            out_specs=pl.BlockSpec((tr, C), lambda i, p: (p[1] * nb + i, 0))),
        out_shape=jax.ShapeDtypeStruct((2 * rh, C), F32), compiler_params=_params(("parallel",)),
    )(jnp.stack([chip, c]).astype(jnp.int32), part, *([landed] * (G - 1)))


def _adam_math(w, g, m, v):
    m = ADAM_B1 * m + (1.0 - ADAM_B1) * g
    v = ADAM_B2 * v + (1.0 - ADAM_B2) * (g * g)
    m_hat = m / (1.0 - ADAM_B1 ** ADAM_STEP)
    v_hat = v / (1.0 - ADAM_B2 ** ADAM_STEP)
    delta = -ADAM_LR * (m_hat / (jnp.sqrt(v_hat) + ADAM_EPS) + ADAM_WD * w)
    return delta, m, v


def _adamw(w, g, m, v, layer, prev, name, comm=None):
    L, R, C = w.shape
    tr = _tile(R, 128, 8)
    n_prev = len(prev)

    def body(w_ref, g_ref, m_ref, v_ref, *rest):
        go_ref, d_ref, mo_ref, vo_ref = rest[n_prev:]
        gv = g_ref[...]
        go_ref[...] = gv
        d_ref[...], mo_ref[...], vo_ref[...] = _adam_math(w_ref[...], gv, m_ref[...], v_ref[...])

    lay = pl.BlockSpec((None, tr, C), lambda i: (layer, i, 0))
    flat = pl.BlockSpec((tr, C), lambda i: (i, 0))
    return _call(
        body, name=name, grid=(R // tr,), in_specs=[lay, flat, lay, lay] + _any_specs(n_prev), out_specs=[lay] * 4,
        out_shape=[jax.ShapeDtypeStruct((L, R, C), F32)] * 4, aliases={4 + k: k for k in range(n_prev)},
        sem=("parallel",), args=(w, g, m, v, *prev), comm=comm)


def _cond_rows(c_row, cw, name):
    D = c_row.shape[1]
    nr, fc = cw.shape

    def body(c_ref, e_ref, o_ref):
        o_ref[...] = jnp.zeros_like(o_ref)
        cv = c_ref[...]
        o_ref[0:1, 0:D] = cv * (1.0 / (1.0 + jnp.exp(-cv)))
        o_ref[8:8 + nr, 0:fc] = e_ref[...]

    return pl.pallas_call(body, name=name, out_shape=jax.ShapeDtypeStruct((16, max(D, fc)), F32))(c_row, cw)


def _ada_fwd(cact, ada_w, ada_b, layer, chip, name):
    _, D, NC = ada_w.shape
    tn = _tile(NC, 1024)
    nj = NC // tn

    def body(idx_ref, c_ref, w_ref, b_ref, o_ref):
        acc = jnp.dot(c_ref[...].astype(BF16), w_ref[0].astype(BF16), preferred_element_type=F32)
        o_ref[...] = acc + b_ref[pl.ds(idx_ref[0], 1), :]

    return pl.pallas_call(
        body, name=name,
        grid_spec=pltpu.PrefetchScalarGridSpec(
            num_scalar_prefetch=1, grid=(nj,),
            in_specs=[pl.BlockSpec((8, D), lambda j, idx: (0, 0)),
                      pl.BlockSpec((1, D, tn), lambda j, idx: (idx[0], 0, j)),
                      pl.BlockSpec((DEPTH, tn), lambda j, idx: (0, idx[1] * nj + j))],
            out_specs=pl.BlockSpec((8, tn), lambda j, idx: (0, j))),
        out_shape=jax.ShapeDtypeStruct((8, NC), F32), compiler_params=_params(("parallel",)),
    )(jnp.stack([layer, chip]).astype(jnp.int32), cact, ada_w, ada_b)


def _ada_grad_adamw(cact_t, dmod, w, m, v, name, comm=None):
    L, D, NC = w.shape
    tr = _tile(D, 128, 8)

    def body(c_ref, d_ref, w_ref, m_ref, v_ref, g_ref, dl_ref, mo_ref, vo_ref):
        g = jnp.dot(c_ref[...], d_ref[...], preferred_element_type=F32, precision=HIGHEST)
        g_ref[...] = g
        dl_ref[...], mo_ref[...], vo_ref[...] = _adam_math(w_ref[...], g, m_ref[...], v_ref[...])

    lay = pl.BlockSpec((None, tr, NC), lambda l, i: (l, i, 0))
    return _call(
        body, name=name, grid=(L, D // tr),
        in_specs=[pl.BlockSpec((tr, N_DEV), lambda l, i: (i, 0)), pl.BlockSpec((None, N_DEV, NC), lambda l, i: (l, 0, 0)), lay, lay, lay],
        out_specs=[lay] * 4, out_shape=[jax.ShapeDtypeStruct((L, D, NC), F32)] * 4,
        sem=("parallel", "parallel"), args=(cact_t, dmod, w, m, v), comm=comm)


def _sum_devices(gathered, name):
    n, R, C = gathered.shape

    def body(g_ref, o_ref):
        acc = g_ref[0]
        for j in range(1, n):
            acc = acc + g_ref[j]
        o_ref[...] = acc

    return pl.pallas_call(body, name=name, out_shape=jax.ShapeDtypeStruct((R, C), F32),
                          compiler_params=pltpu.CompilerParams(vmem_limit_bytes=VMEM_LIMIT))(gathered)


def _adamw_small(w, g, m, v, name):
    def body(w_ref, g_ref, m_ref, v_ref, d_ref, mo_ref, vo_ref):
        d_ref[...], mo_ref[...], vo_ref[...] = _adam_math(w_ref[...], g_ref[...], m_ref[...], v_ref[...])

    return pl.pallas_call(body, name=name, out_shape=[jax.ShapeDtypeStruct(w.shape, F32)] * 3)(w, g, m, v)


def _pad_rows(flat, unit=8 * LANES):
    n = flat.shape[0]
    total = -(-n // unit) * unit
    return jnp.pad(flat, (0, total - n)).reshape(total // LANES, LANES)


def _pad_lanes(v2d):
    return jnp.pad(v2d.reshape(1, -1), ((0, 0), (0, LANES - v2d.size)))


PLAN = {
    "fox_proj": [("gather", "ffn_w_up0", 0, 2)],
    "fox_attn": [("gather", "ffn_w_up0", 1, 2)],
    "ffn_up0": [("gather", "ffn_w_down0", 0, 1)],
    "ffn_gate0": [("gather", "swa_w_in", 0, 1), ("gather", "swa_w_o", 0, 1)],
    "ffn_down0": [("gather", "ffn_w_up1", 0, 2)],
    "swa_attn": [("gather", "ffn_w_up1", 1, 2)],
    "ffn_up1": [("gather", "ffn_w_down1", 0, 1)],
    "ffn_gate_bwd1": [("swap", "ffn_w_down1")],
    "ffn_up_dx1": [("scatter", "ffn_w_down1", 0, 1)],
    "swa_out_dx": [("swap", "ffn_w_up1")],
    "swa_attn_bwd": [("scatter", "ffn_w_up1", 0, 1), ("swap", "swa_w_o")],
    "swa_proj_dx": [("scatter", "swa_w_o", 0, 1)],
    "ffn_down_dx0": [("swap", "swa_w_in")],
    "ffn_down_dw0": [("scatter", "swa_w_in", 0, 1)],
    "ffn_gate_bwd0": [("swap", "ffn_w_down0")],
    "ffn_up_dx0": [("scatter", "ffn_w_down0", 0, 1)],
    "fox_out_dx": [("swap", "ffn_w_up0")],
    "fox_attn_bwd": [("scatter", "ffn_w_up0", 0, 2), ("swap", "fox_w_o")],
    "fox_proj_dx": [("scatter", "ffn_w_up0", 1, 2)],
    "fox_proj_dw": [("scatter", "fox_w_o", 0, 1)],
    "ada_grad": [("swap", "fox_w_in"), ("join", "all but fox_w_in")],
    "adamw_ffn_w_up0": [("scatter", "fox_w_in", 0, 2)],
    "adamw_ffn_w_up1": [("scatter", "fox_w_in", 1, 2)],
}


class Exchanges:
    def __init__(self, dm, slots, chip, c):
        self.dm, self.slots, self.chip, self.c = dm, dict(slots), chip, c
        self.raw, self.part, self.landed, self.grads, self.views, self.pending = {}, {}, {}, {}, {}, {}

    def gather_now(self, keys, name):
        comm = _gather_comm([self.slots[k] for k in keys], [(0, 1)] * len(keys))
        _run_comm(comm, name)
        self.slots.update(zip(keys, comm.results))

    def w(self, key):
        if key not in self.views:
            S, D, FH, QH, KH, Fh = self.dm
            full = self.slots[key]
            if key == "fox_w_in":
                cols = full.shape[2]
                full = jnp.pad(full.transpose(1, 0, 2).reshape(D, N_CHIPS * cols), ((0, 0), (0, 3 * D + LANES - N_CHIPS * cols)))[None]
            elif key in ("fox_w_o", "swa_w_o"):
                full = full.reshape(1, D, D)
            elif key.startswith("ffn_w_down"):
                full = full.reshape(1, Fh, D)
            self.views[key] = full
        return self.views[key]

    def carry(self, stage):
        todo = []
        for kind, key, *chunk in PLAN.get(stage, ()):
            if kind == "gather":
                todo.append((kind, [key], _gather_comm([self.slots[key]], [tuple(chunk)])))
            elif kind == "swap":
                todo.append((kind, [key], _swap_comm([self.raw[key]])))
            elif kind == "scatter":
                todo.append((kind, [(key, *chunk)], _scatter_comm([self.part[key]], [self.landed.get(key)], [tuple(chunk)])))
            else:
                keys = [k for k in self.landed if k != "fox_w_in"]
                halves = [_sum_chips(self.part[k], self.landed[k], self.chip, self.c, f"sum_chips_{k}") for k in keys]
                todo.append((kind, keys, _join_comm(halves)))
        self.pending[stage] = todo
        return _merge([cm for _, _, cm in todo])

    def carried(self, stage, comm):
        for kind, keys, cm in self.pending.pop(stage):
            if kind == "gather":
                self.slots[keys[0]] = cm.results[0]
            elif kind == "swap":
                self.part[keys[0]] = _add_sibling(self.raw[keys[0]], cm.results[0], self.c, f"add_sibling_{keys[0]}")
            elif kind == "scatter":
                self.landed[keys[0][0]] = cm.results[0]
            else:
                self.grads.update(zip(keys, cm.results))

    def grad(self, key, g):
        S, D, FH, QH, KH, Fh = self.dm
        if key == "fox_w_in":
            cols = self.slots[key].shape[2]
            g = g[0][:, :N_CHIPS * cols].reshape(D, N_CHIPS, cols).transpose(1, 0, 2)
        elif key in ("fox_w_o", "swa_w_o"):
            g = g.reshape(N_CHIPS, D // N_CHIPS, D)
        elif key.startswith("ffn_w_down"):
            g = g.reshape(N_CHIPS, Fh // N_CHIPS, D)
        self.raw[key] = g

    def finish(self):
        key = "fox_w_in"
        comm = _join_comm([_sum_chips(self.part[key], self.landed[key], self.chip, self.c, f"sum_chips_{key}")])
        _run_comm(comm, "join_fox_w_in")
        self.grads[key] = comm.results[0]
        return self.grads


def _step(dm, a):
    S, D, FH, QH, KH, Fh = dm
    ix, iy, ic = lax.axis_index("x"), lax.axis_index("y"), lax.axis_index("c")
    chip = 2 * ix + iy
    dev = 2 * chip + ic
    F2c = a["ffn_w_up"].shape[2]
    NC = a["ada_w"].shape[2]
    PW = 3 * D + LANES
    fox_cols = a["fox_w_in"].shape[2]

    e0 = _cond_rows(a["c"], a["ffn_conv_w"].reshape(DEPTH * 3, F2c), "silu_c")
    g0 = _allgather_small(e0, "gather_cond").reshape(N_DEV, 16, e0.shape[1])
    cact = g0[:, 0, :D]
    conv_w = g0[0::2, 8:8 + DEPTH * 3, :F2c].transpose(1, 0, 2).reshape(DEPTH, 3, N_CHIPS * F2c)
    rows = _ada_fwd(cact, a["ada_w"], a["ada_b"], ic, chip, "ada_proj")
    g1 = _allgather_small(rows, "gather_mod").reshape(N_CHIPS, DEPTH, 8, NC)
    mod = lax.dynamic_index_in_dim(g1, dev, axis=2, keepdims=False).transpose(1, 0, 2).reshape(DEPTH, N_CHIPS * NC)

    names = ["fox_w_in", "fox_w_o", "swa_w_in", "swa_w_o", "ffn_w_up", "ffn_w_up", "ffn_w_down", "ffn_w_down"]
    layers = [0, 0, 0, 0, 0, 1, 0, 1]
    keys = ["fox_w_in", "fox_w_o", "swa_w_in", "swa_w_o", "ffn_w_up0", "ffn_w_up1", "ffn_w_down0", "ffn_w_down1"]
    slots = {k: _cast_bf16(a[nm], l, chip, f"cast_{k}") for k, nm, l in zip(keys, names, layers)}
    pp = Exchanges(dm, slots, chip, ic)
    pp.gather_now(["fox_w_in", "fox_w_o"], "gather_fox")
    sp = {"fox_b_f": _pad_lanes(a["fox_b_f"]), "sinks": jnp.repeat(a["swa_sinks"].reshape(KH, QH // KH), WIN, axis=1)[:, :, None],
          "conv_w": [conv_w[i] for i in range(DEPTH)], "conv_b": [a["ffn_conv_b"][i:i + 1] for i in range(DEPTH)]}
    for nm in ("ln_mix_g", "ln_mix_b", "ln_ffn_g", "ln_ffn_b"):
        sp[nm] = [a[nm][i:i + 1] for i in range(DEPTH)]

    loss_cols, grad_x, gs, dmod = _local_step(dm, a["x"][0], a["loss_target"][0], a["positions"][0], mod, sp, pp)
    loss = lax.psum(0.5 / D * jnp.sum(loss_cols), ("x", "y", "c"))
    out = {"loss": loss, "grad_x": grad_x[None]}

    def run(fn, *args, name, **kw):
        comm = pp.carry(name)
        res = fn(*args, name=name, comm=comm, **kw)
        if comm is not None:
            pp.carried(name, comm)
        return res

    pieces = [dmod.reshape(-1), gs["fox_b_f"].reshape(-1), _pad_lanes(gs["sinks"]).reshape(-1),
              jnp.stack(gs["conv_w"]).reshape(-1), jnp.stack(gs["conv_b"]).reshape(-1)]
    pieces += [jnp.stack(gs[nm]).reshape(-1) for nm in ("ln_mix_g", "ln_mix_b", "ln_ffn_g", "ln_ffn_b")]
    sizes = [p.shape[0] for p in pieces]
    packed = _pad_rows(jnp.concatenate(pieces))
    allp = _allgather_small(packed, "gather_small").reshape(N_DEV, packed.shape[0], LANES)
    tot = _sum_devices(allp, "sum_small").reshape(-1)
    offs = [sum(sizes[:k]) for k in range(len(sizes))]
    take = lambda k: tot[offs[k]:offs[k] + sizes[k]]
    g_small = {"ada_b": take(0).reshape(DEPTH, -1), "fox_b_f": take(1)[:FH].reshape(1, FH), "swa_sinks": take(2)[:QH].reshape(1, QH),
               "ffn_conv_w": lax.dynamic_slice_in_dim(take(3).reshape(DEPTH, 3, N_CHIPS * F2c), chip * F2c, F2c, axis=2),
               "ffn_conv_b": take(4).reshape(DEPTH, -1)}
    for k, nm in enumerate(("ln_mix_g", "ln_mix_b", "ln_ffn_g", "ln_ffn_b")):
        g_small[nm] = take(5 + k).reshape(DEPTH, D)
    small = list(g_small)
    pack = lambda pre: _pad_rows(jnp.concatenate([(a[pre + nm] if pre else a[nm]).reshape(-1) for nm in small]))
    gp = _pad_rows(jnp.concatenate([g_small[nm].reshape(-1) for nm in small]))
    ds_, ms_, vs_ = _adamw_small(pack(""), gp, pack("m_"), pack("v_"), "adamw_small")
    off = 0
    for nm in small:
        n_el = a[nm].size
        out["grad_" + nm] = g_small[nm]
        for pre, arr in (("delta_", ds_), ("new_m_", ms_), ("new_v_", vs_)):
            out[pre + nm] = arr.reshape(-1)[off:off + n_el].reshape(a[nm].shape)
        off += n_el

    dmod_all = allp.reshape(N_DEV, -1)[:, :DEPTH * N_CHIPS * NC].reshape(N_DEV, DEPTH, N_CHIPS * NC)
    dmod_mine = lax.dynamic_slice_in_dim(dmod_all, chip * NC, NC, axis=2).transpose(1, 0, 2)
    ada = run(_ada_grad_adamw, cact.T, dmod_mine, a["ada_w"], a["m_ada_w"], a["v_ada_w"], name="ada_grad")
    for pre, arr in zip(("grad_", "delta_", "new_m_", "new_v_"), ada):
        out[pre + "ada_w"] = arr

    upd = {}
    order = [4, 5, 6, 7, 1, 2, 3]
    for t in order:
        k, nm, l = keys[t], names[t], layers[t]
        upd[nm] = run(_adamw, a[nm], pp.grads[k], a["m_" + nm], a["v_" + nm], l, upd.get(nm, ()), name=f"adamw_{k}")
    grads = pp.finish()
    upd["fox_w_in"] = _adamw(a["fox_w_in"], grads["fox_w_in"], a["m_fox_w_in"], a["v_fox_w_in"], 0, (), "adamw_fox_w_in")
    for nm, res in upd.items():
        for pre, arr in zip(("grad_", "delta_", "new_m_", "new_v_"), res):
            out[pre + nm] = arr
    return out


_WEIGHTS = ["fox_w_in", "fox_b_f", "fox_w_o", "swa_w_in", "swa_sinks", "swa_w_o", "ada_w", "ada_b", "ffn_w_up", "ffn_conv_w",
            "ffn_conv_b", "ffn_w_down", "ln_mix_g", "ln_mix_b", "ln_ffn_g", "ln_ffn_b"]
_INPUTS = (["x", "c", "positions"] + _WEIGHTS + ["loss_target"] + ["m_" + w for w in _WEIGHTS] + ["v_" + w for w in _WEIGHTS])


def kernel(x, c, positions, fox_w_in, fox_b_f, fox_w_o, swa_w_in, swa_sinks, swa_w_o, ada_w, ada_b, ffn_w_up, ffn_conv_w, ffn_conv_b, ffn_w_down, ln_mix_g, ln_mix_b, ln_ffn_g, ln_ffn_b, loss_target, m_fox_w_in, m_fox_b_f, m_fox_w_o, m_swa_w_in, m_swa_sinks, m_swa_w_o, m_ada_w, m_ada_b, m_ffn_w_up, m_ffn_conv_w, m_ffn_conv_b, m_ffn_w_down, m_ln_mix_g, m_ln_mix_b, m_ln_ffn_g, m_ln_ffn_b, v_fox_w_in, v_fox_b_f, v_fox_w_o, v_swa_w_in, v_swa_sinks, v_swa_w_o, v_ada_w, v_ada_b, v_ffn_w_up, v_ffn_conv_w, v_ffn_conv_b, v_ffn_w_down, v_ln_mix_g, v_ln_mix_b, v_ln_ffn_g, v_ln_ffn_b):
    args = (x, c, positions, fox_w_in, fox_b_f, fox_w_o, swa_w_in, swa_sinks, swa_w_o, ada_w, ada_b, ffn_w_up, ffn_conv_w, ffn_conv_b, ffn_w_down, ln_mix_g, ln_mix_b, ln_ffn_g, ln_ffn_b, loss_target, m_fox_w_in, m_fox_b_f, m_fox_w_o, m_swa_w_in, m_swa_sinks, m_swa_w_o, m_ada_w, m_ada_b, m_ffn_w_up, m_ffn_conv_w, m_ffn_conv_b, m_ffn_w_down, m_ln_mix_g, m_ln_mix_b, m_ln_ffn_g, m_ln_ffn_b, v_fox_w_in, v_fox_b_f, v_fox_w_o, v_swa_w_in, v_swa_sinks, v_swa_w_o, v_ada_w, v_ada_b, v_ffn_w_up, v_ffn_conv_w, v_ffn_conv_b, v_ffn_w_down, v_ln_mix_g, v_ln_mix_b, v_ln_ffn_g, v_ln_ffn_b)
    out = _step(PROD, dict(zip(_INPUTS, args)))
    order = ["loss", "grad_x"] + [p + w for p in ("grad_", "delta_", "new_m_", "new_v_") for w in _WEIGHTS]
    return tuple(out[k] for k in order)
```

```python
import functools
from typing import NamedTuple

import jax
import jax.numpy as jnp
from jax import lax
from jax.experimental import pallas as pl
from jax.experimental.pallas import tpu as pltpu

F32 = jnp.float32
BF16 = jnp.bfloat16
MESH = pl.DeviceIdType.MESH
HIGHEST = lax.Precision.HIGHEST

N_CHIPS = 4
N_DEV = 8
LANES = 128
VMEM_LIMIT = 56 * 1024 * 1024

DEPTH = 2
DEEPNORM_ALPHA = (2.0 * DEPTH) ** 0.25
LN_EPS = 1e-5
ROPE_THETA = 500000.0
ADAM_LR, ADAM_B1, ADAM_B2, ADAM_EPS, ADAM_WD, ADAM_STEP = 0.001, 0.9, 0.999, 1e-08, 0.01, 10
NEG = -1e30


class Dims(NamedTuple):
    S: int
    D: int
    FH: int
    QH: int
    KH: int
    F: int


PROD = Dims(S=2048, D=2048, FH=16, QH=32, KH=4, F=5632)
FDH = 128
SDH = 64
WIN = 128
ROPE_DIM = 16
FOX_TQ = 256


def _params(sem=None, vmem=VMEM_LIMIT):
    return pltpu.CompilerParams(dimension_semantics=sem, vmem_limit_bytes=vmem)


def _tile(n, pref, unit=LANES):
    if n <= pref:
        return n
    t = (pref // unit) * unit
    while t > 0:
        if n % t == 0:
            return t
        t -= unit
    return n


class Comm:
    def __init__(self, args, out_shapes, aliases, n_sem, start, finish, members=()):
        self.args, self.out_shapes, self.aliases, self.n_sem = list(args), list(out_shapes), dict(aliases), n_sem
        self.start, self.finish = start, finish
        self.members = members
        self.results = None

    def set_results(self, res):
        self.results = list(res)
        for cm, o0 in self.members:
            cm.set_results(self.results[o0:o0 + len(cm.out_shapes)])


class _SemView:
    def __init__(self, sems, first):
        self.sems, self.first = sems, first

    @property
    def at(self):
        return self

    def __getitem__(self, k):
        return self.sems.at[self.first + k]


def _merge(comms):
    comms = [cm for cm in comms if cm is not None]
    if len(comms) < 2:
        return comms[0] if comms else None
    args, shapes, aliases, spans, n_sem = [], [], {}, [], 0
    for cm in comms:
        spans.append((len(args), len(shapes), n_sem))
        aliases.update({len(args) + a: len(shapes) + o for a, o in cm.aliases.items()})
        args += cm.args
        shapes += cm.out_shapes
        n_sem += cm.n_sem

    def each(step):
        def run(ar, ou, send, recv):
            for cm, (a0, o0, s0) in zip(comms, spans):
                getattr(cm, step)(ar[a0:a0 + len(cm.args)], ou[o0:o0 + len(cm.out_shapes)], _SemView(send, s0), _SemView(recv, s0))
        return run

    return Comm(args, shapes, aliases, n_sem, each("start"), each("finish"), [(cm, o0) for cm, (_, o0, _) in zip(comms, spans)])


def _place():
    x, y, c = lax.axis_index("x"), lax.axis_index("y"), lax.axis_index("c")
    chips = [(1 - x, y), (x, 1 - y), (1 - x, 1 - y)]
    return x, y, c, chips


def _remote(src, dst, send, recv, to):
    return pltpu.make_async_remote_copy(src_ref=src, dst_ref=dst, send_sem=send, recv_sem=recv, device_id=to, device_id_type=MESH)


def _any_specs(n):
    return [pl.BlockSpec(memory_space=pl.ANY)] * n


def _call(body, *, name, grid, in_specs, out_specs, out_shape, args, sem, scratch_shapes=(), aliases=None, comm=None):
    in_specs, out_specs, out_shape, scratch_shapes = list(in_specs), list(out_specs), list(out_shape), list(scratch_shapes)
    aliases = dict(aliases or {})
    if comm is None:
        return pl.pallas_call(body, name=name, grid=grid, in_specs=in_specs, out_specs=out_specs, out_shape=out_shape,
                              scratch_shapes=scratch_shapes, input_output_aliases=aliases, compiler_params=_params(sem))(*args)
    n_in, n_out, nc_in, nc_out, n_scr = len(in_specs), len(out_specs), len(comm.args), len(comm.out_shapes), len(scratch_shapes)

    def wrapped(*refs):
        ins, refs = refs[:n_in], refs[n_in:]
        cin, refs = refs[:nc_in], refs[nc_in:]
        outs, refs = refs[:n_out], refs[n_out:]
        cout, refs = refs[:nc_out], refs[nc_out:]
        scratch, (send, recv) = refs[:n_scr], refs[n_scr:]
        ids = [pl.program_id(k) for k in range(len(grid))]
        first = functools.reduce(jnp.logical_and, [i == 0 for i in ids])
        last = functools.reduce(jnp.logical_and, [i == g - 1 for i, g in zip(ids, grid)])

        @pl.when(first)
        def _():
            comm.start(cin, cout, send, recv)

        body(*ins, *outs, *scratch)

        @pl.when(last)
        def _():
            comm.finish(cin, cout, send, recv)

    res = pl.pallas_call(
        wrapped, name=name, grid=grid, in_specs=in_specs + _any_specs(nc_in), out_specs=out_specs + _any_specs(nc_out),
        out_shape=out_shape + comm.out_shapes,
        scratch_shapes=scratch_shapes + [pltpu.SemaphoreType.DMA((comm.n_sem,)), pltpu.SemaphoreType.DMA((comm.n_sem,))],
        input_output_aliases={**aliases, **{n_in + a: n_out + o for a, o in comm.aliases.items()}},
        compiler_params=_params(("arbitrary",) * len(grid)),
    )(*args, *comm.args)
    comm.set_results(res[n_out:])
    return list(res[:n_out])


def _run_comm(comm, name):
    nc_in, nc_out = len(comm.args), len(comm.out_shapes)

    def body(*refs):
        cin, cout, (send, recv) = refs[:nc_in], refs[nc_in:nc_in + nc_out], refs[nc_in + nc_out:]
        comm.start(cin, cout, send, recv)
        comm.finish(cin, cout, send, recv)

    res = pl.pallas_call(
        body, name=name, in_specs=_any_specs(nc_in), out_specs=_any_specs(nc_out), out_shape=comm.out_shapes,
        scratch_shapes=[pltpu.SemaphoreType.DMA((comm.n_sem,)), pltpu.SemaphoreType.DMA((comm.n_sem,))],
        input_output_aliases=comm.aliases,
    )(*comm.args)
    comm.set_results(res)


def _gather_comm(slots, chunks):
    n = len(slots)

    def rows(t, who):
        rh = slots[t].shape[1] // 2
        k, nch = chunks[t]
        rc = rh // nch
        return pl.ds(who * rh + k * rc, rc)

    def start(args, outs, send, recv):
        x, y, c, chips = _place()
        s = 2 * x + y
        for t in range(n):
            mine = outs[t].at[s, rows(t, c)]
            for j, chip in enumerate(chips):
                _remote(mine, mine, send.at[6 * t + j], recv.at[6 * t + j], (*chip, c)).start()

    def finish(args, outs, send, recv):
        x, y, c, chips = _place()
        s = 2 * x + y
        sib = (x, y, 1 - c)
        for t in range(n):
            for j, chip in enumerate(chips):
                blk = outs[t].at[2 * chip[0] + chip[1], rows(t, c)]
                _remote(blk, blk, send.at[6 * t + j], recv.at[6 * t + j], (*chip, c)).wait_recv()
                _remote(blk, blk, send.at[6 * t + 3 + j], recv.at[6 * t + 3 + j], sib).start()
        for t in range(n):
            for j, chip in enumerate(chips):
                blk = outs[t].at[2 * chip[0] + chip[1], rows(t, 1 - c)]
                _remote(blk, blk, send.at[6 * t + 3 + j], recv.at[6 * t + 3 + j], sib).wait_recv()
        for t in range(n):
            mine = outs[t].at[s, rows(t, c)]
            for j, chip in enumerate(chips):
                _remote(mine, mine, send.at[6 * t + j], recv.at[6 * t + j], (*chip, c)).wait_send()
                blk = outs[t].at[2 * chip[0] + chip[1], rows(t, c)]
                _remote(blk, blk, send.at[6 * t + 3 + j], recv.at[6 * t + 3 + j], sib).wait_send()

    shapes = [jax.ShapeDtypeStruct(w.shape, w.dtype) for w in slots]
    return Comm(slots, shapes, {t: t for t in range(n)}, 6 * n, start, finish)


def _scatter_comm(parts, landed, chunks):
    n = len(parts)
    prev = [t for t in range(n) if landed[t] is not None]

    def rows(t):
        k, nch = chunks[t]
        rc = parts[t].shape[1] // nch
        return pl.ds(k * rc, rc)

    def start(args, outs, send, recv):
        x, y, c, chips = _place()
        s = 2 * x + y
        for t in range(n):
            for j, chip in enumerate(chips):
                _remote(args[t].at[2 * chip[0] + chip[1], rows(t)], outs[t].at[s, rows(t)],
                        send.at[3 * t + j], recv.at[3 * t + j], (*chip, c)).start()

    def finish(args, outs, send, recv):
        x, y, c, chips = _place()
        for t in range(n):
            for j, chip in enumerate(chips):
                blk = outs[t].at[2 * chip[0] + chip[1], rows(t)]
                _remote(blk, blk, send.at[3 * t + j], recv.at[3 * t + j], (*chip, c)).wait_recv()
        for t in range(n):
            for j, chip in enumerate(chips):
                src = args[t].at[2 * chip[0] + chip[1], rows(t)]
                _remote(src, src, send.at[3 * t + j], recv.at[3 * t + j], (*chip, c)).wait_send()

    shapes = [jax.ShapeDtypeStruct(p.shape, p.dtype) for p in parts]
    return Comm(list(parts) + [landed[t] for t in prev], shapes, {n + i: t for i, t in enumerate(prev)}, 3 * n, start, finish)


def _swap_comm(gs):
    n = len(gs)

    def copy(args, outs, send, recv, t):
        _, _, c, _ = _place()
        rh = gs[t].shape[1] // 2
        x, y = lax.axis_index("x"), lax.axis_index("y")
        return _remote(args[t].at[:, pl.ds((1 - c) * rh, rh), :], outs[t], send.at[t], recv.at[t], (x, y, 1 - c))

    def start(args, outs, send, recv):
        for t in range(n):
            copy(args, outs, send, recv, t).start()

    def finish(args, outs, send, recv):
        for t in range(n):
            copy(args, outs, send, recv, t).wait()

    shapes = [jax.ShapeDtypeStruct((g.shape[0], g.shape[1] // 2, g.shape[2]), g.dtype) for g in gs]
    return Comm(gs, shapes, {}, n, start, finish)


def _join_comm(gs):
    n = len(gs)

    def half(outs, t, who):
        rh = gs[t].shape[0] // 2
        return outs[t].at[pl.ds(who * rh, rh), :]

    def start(args, outs, send, recv):
        x, y, c, _ = _place()
        for t in range(n):
            _remote(half(outs, t, c), half(outs, t, c), send.at[t], recv.at[t], (x, y, 1 - c)).start()

    def finish(args, outs, send, recv):
        x, y, c, _ = _place()
        for t in range(n):
            _remote(half(outs, t, 1 - c), half(outs, t, 1 - c), send.at[t], recv.at[t], (x, y, 1 - c)).wait_recv()
        for t in range(n):
            _remote(half(outs, t, c), half(outs, t, c), send.at[t], recv.at[t], (x, y, 1 - c)).wait_send()

    shapes = [jax.ShapeDtypeStruct(g.shape, g.dtype) for g in gs]
    return Comm(gs, shapes, {t: t for t in range(n)}, n, start, finish)


_DN = {"nn": (((1,), (0,)), ((), ())), "nt": (((1,), (1,)), ((), ())), "tn": (((0,), (0,)), ((), ()))}


def _mm(a, b, *, mode, out_dtype, name, out_groups=1, tm=1024, tn=1024, tk=2048, comm=None):
    ga, ra, ca = a.shape
    gb, rb, cb = b.shape
    if mode == "nn":
        M, K, N = ra, ga * ca, gb * cb
        assert rb == K and ga == 1 or (rb == K)
    elif mode == "nt":
        M, K, N = ra, ga * ca, rb
        assert gb * cb == K
    else:
        K, M, N = ra, ga * ca, gb * cb
        assert rb == K
    go = out_groups
    if mode == "nn":
        tk = _tile(ca, tk); assert rb % tk == 0 and (ga == 1 or True)
        tn = _tile(min(cb, N // go), tn); tm = _tile(M, tm, 8)
    elif mode == "nt":
        tk = _tile(ca, tk); tk = _tile(cb, tk) if cb % tk else tk; assert ca % tk == 0 and cb % tk == 0
        tn = _tile(N // go, tn); tm = _tile(M, tm, 8)
    else:
        tk = _tile(K, tk, 8); tm = _tile(ca, tm); tn = _tile(min(cb, N // go), tn)
    assert (N // go) % tn == 0 and M % tm == 0 and K % tk == 0, (name, M, N, K, tm, tn, tk)
    nk = K // tk
    kpa = max(ca // tk, 1)
    kpb = max(cb // tk, 1)
    npb = max(cb // tn, 1)
    npo = (N // go) // tn
    mpa = max(ca // tm, 1)

    if mode == "nn":
        a_spec = pl.BlockSpec((1, tm, tk), lambda j, i, k: (k // kpa, i, k % kpa))
        b_spec = pl.BlockSpec((1, tk, tn), lambda j, i, k: (j // npb, k, j % npb))
    elif mode == "nt":
        a_spec = pl.BlockSpec((1, tm, tk), lambda j, i, k: (k // kpa, i, k % kpa))
        b_spec = pl.BlockSpec((1, tn, tk), lambda j, i, k: (k // kpb, j, k % kpb))
    else:
        a_spec = pl.BlockSpec((1, tk, tm), lambda j, i, k: (i // mpa, k, i % mpa))
        b_spec = pl.BlockSpec((1, tk, tn), lambda j, i, k: (j // npb, k, j % npb))
    o_spec = pl.BlockSpec((1, tm, tn), lambda j, i, k: (j // npo, i, j % npo))
    dn = _DN[mode]

    def body(a_ref, b_ref, o_ref, *acc):
        p = lax.dot_general(a_ref[0], b_ref[0], dn, preferred_element_type=F32)
        if nk == 1:
            o_ref[0] = p.astype(out_dtype)
        else:
            k = pl.program_id(2)

            @pl.when(k == 0)
            def _():
                acc[0][...] = p

            @pl.when(k > 0)
            def _():
                acc[0][...] += p

            @pl.when(k == nk - 1)
            def _():
                o_ref[0] = acc[0][...].astype(out_dtype)

    return _call(
        body, name=name, grid=(N // tn, M // tm, nk), in_specs=[a_spec, b_spec], out_specs=[o_spec],
        out_shape=[jax.ShapeDtypeStruct((go, M, N // go), out_dtype)],
        scratch_shapes=[pltpu.VMEM((tm, tn), F32)] if nk > 1 else [],
        sem=("parallel", "parallel", "arbitrary"), args=(a, b), comm=comm)[0]


def _rows(tr, d):
    return pl.BlockSpec((tr, d), lambda i: (i, 0))


def _vec(d):
    return pl.BlockSpec((1, d), lambda i: (0, 0))


def _modulate(x, sc, sh, name):
    S, D = x.shape
    tr = min(256, S)

    def body(x_ref, sc_ref, sh_ref, h_ref):
        h_ref[...] = (x_ref[...] * (1.0 + sc_ref[...]) + sh_ref[...]).astype(BF16)

    return pl.pallas_call(
        body, name=name, grid=(S // tr,), in_specs=[_rows(tr, D), _vec(D), _vec(D)], out_specs=_rows(tr, D),
        out_shape=jax.ShapeDtypeStruct((S, D), BF16), compiler_params=_params(("parallel",)),
    )(x, sc, sh)


def _ln_fwd(x, y, gate, gamma, beta, sc, sh, name):
    S, D = x.shape
    tr = min(256, S)
    emit_h = sc is not None

    def body(*refs):
        if emit_h:
            x_ref, y_ref, g_ref, ga_ref, be_ref, sc_ref, sh_ref, xo_ref, xh_ref, rs_ref, h_ref = refs
        else:
            x_ref, y_ref, g_ref, ga_ref, be_ref, xo_ref, xh_ref, rs_ref = refs
        z = DEEPNORM_ALPHA * x_ref[...] + (1.0 + g_ref[...]) * y_ref[...]
        mu = jnp.mean(z, axis=-1, keepdims=True)
        zc = z - mu
        var = jnp.mean(zc * zc, axis=-1, keepdims=True)
        rstd = lax.rsqrt(var + LN_EPS)
        xh = zc * rstd
        xo = xh * ga_ref[...] + be_ref[...]
        xo_ref[...] = xo
        xh_ref[...] = xh
        rs_ref[...] = rstd
        if emit_h:
            h_ref[...] = (xo * (1.0 + sc_ref[...]) + sh_ref[...]).astype(BF16)

    ins = [x, y, gate, gamma, beta] + ([sc, sh] if emit_h else [])
    in_specs = [_rows(tr, D), _rows(tr, D)] + [_vec(D)] * (len(ins) - 2)
    out_shape = [jax.ShapeDtypeStruct((S, D), F32), jax.ShapeDtypeStruct((S, D), F32), jax.ShapeDtypeStruct((S, 1), F32)]
    out_specs = [_rows(tr, D), _rows(tr, D), _rows(tr, 1)]
    if emit_h:
        out_shape.append(jax.ShapeDtypeStruct((S, D), BF16))
        out_specs.append(_rows(tr, D))
    return pl.pallas_call(
        body, name=name, grid=(S // tr,), in_specs=in_specs, out_specs=out_specs, out_shape=out_shape,
        compiler_params=_params(("parallel",)),
    )(*ins)


def _loss_head(xf, tgt, name):
    S, D = xf.shape
    tr = min(256, S)

    def body(x_ref, t_ref, dx_ref, l_ref):
        e = x_ref[...] - t_ref[...]
        dx_ref[...] = e * (1.0 / D)

        @pl.when(pl.program_id(0) == 0)
        def _():
            l_ref[...] = jnp.zeros_like(l_ref)

        l_ref[...] += jnp.sum(e * e, axis=0, keepdims=True)

    return pl.pallas_call(
        body, name=name, grid=(S // tr,), in_specs=[_rows(tr, D), _rows(tr, D)],
        out_specs=[_rows(tr, D), _vec(D)],
        out_shape=[jax.ShapeDtypeStruct((S, D), F32), jax.ShapeDtypeStruct((1, D), F32)],
        compiler_params=_params(("arbitrary",)),
    )(xf, tgt)


def _ln_bwd(dxo, xh, rstd, gamma, y, gate, name):
    S, D = dxo.shape
    tr = min(256, S)

    def body(dx_ref, xh_ref, rs_ref, ga_ref, y_ref, g_ref, dres_ref, dy_ref, dga_ref, dbe_ref, dg_ref):
        dxo_ = dx_ref[...]
        xh_ = xh_ref[...]
        dxh = dxo_ * ga_ref[...]
        m1 = jnp.mean(dxh, axis=-1, keepdims=True)
        m2 = jnp.mean(dxh * xh_, axis=-1, keepdims=True)
        dz = rs_ref[...] * (dxh - m1 - xh_ * m2)
        dres_ref[...] = DEEPNORM_ALPHA * dz
        dy_ref[...] = ((1.0 + g_ref[...]) * dz).astype(BF16)

        @pl.when(pl.program_id(0) == 0)
        def _():
            dga_ref[...] = jnp.zeros_like(dga_ref)
            dbe_ref[...] = jnp.zeros_like(dbe_ref)
            dg_ref[...] = jnp.zeros_like(dg_ref)

        dga_ref[...] += jnp.sum(dxo_ * xh_, axis=0, keepdims=True)
        dbe_ref[...] += jnp.sum(dxo_, axis=0, keepdims=True)
        dg_ref[...] += jnp.sum(dz * y_ref[...], axis=0, keepdims=True)

    return pl.pallas_call(
        body, name=name, grid=(S // tr,),
        in_specs=[_rows(tr, D), _rows(tr, D), _rows(tr, 1), _vec(D), _rows(tr, D), _vec(D)],
        out_specs=[_rows(tr, D), _rows(tr, D), _vec(D), _vec(D), _vec(D)],
        out_shape=[jax.ShapeDtypeStruct((S, D), F32), jax.ShapeDtypeStruct((S, D), BF16)] + [jax.ShapeDtypeStruct((1, D), F32)] * 3,
        compiler_params=_params(("arbitrary",)),
    )(dxo, xh, rstd, gamma, y, gate)


def _mod_bwd(dh, x, sc, dres, name):
    S, D = x.shape
    tr = min(256, S)

    def body(dh_ref, x_ref, sc_ref, dr_ref, dx_ref, dsc_ref, dsh_ref):
        dh_ = dh_ref[...]
        dx_ref[...] = dr_ref[...] + dh_ * (1.0 + sc_ref[...])

        @pl.when(pl.program_id(0) == 0)
        def _():
            dsc_ref[...] = jnp.zeros_like(dsc_ref)
            dsh_ref[...] = jnp.zeros_like(dsh_ref)

        dsc_ref[...] += jnp.sum(dh_ * x_ref[...], axis=0, keepdims=True)
        dsh_ref[...] += jnp.sum(dh_, axis=0, keepdims=True)

    return pl.pallas_call(
        body, name=name, grid=(S // tr,),
        in_specs=[_rows(tr, D), _rows(tr, D), _vec(D), _rows(tr, D)],
        out_specs=[_rows(tr, D), _vec(D), _vec(D)],
        out_shape=[jax.ShapeDtypeStruct((S, D), F32), jax.ShapeDtypeStruct((1, D), F32), jax.ShapeDtypeStruct((1, D), F32)],
        compiler_params=_params(("arbitrary",)),
    )(dh, x, sc, dres)


def _log_sigmoid(z):
    return jnp.minimum(z, 0.0) - jnp.log(1.0 + jnp.exp(-jnp.abs(z)))


def _fox_gate_fwd(proj, b_f, n_heads, name):
    S, PW = proj.shape
    blk = min(256, S)
    last = PW // LANES - 1

    def body(fl_ref, b_ref, cum_ref):
        r = lax.broadcasted_iota(jnp.int32, (blk, blk), 0)
        c = lax.broadcasted_iota(jnp.int32, (blk, blk), 1)
        tril = (c <= r).astype(F32)
        carry = jnp.zeros((1, LANES), F32)
        for i in range(S // blk):
            lf = _log_sigmoid(fl_ref[i * blk:(i + 1) * blk, :] + b_ref[...])
            cum_ref[i * blk:(i + 1) * blk, :] = jnp.dot(tril, lf, preferred_element_type=F32, precision=HIGHEST) + carry
            carry = carry + jnp.sum(lf, axis=0, keepdims=True)

    return pl.pallas_call(
        body, name=name, grid=(1,),
        in_specs=[pl.BlockSpec((S, LANES), lambda i: (0, last)), pl.BlockSpec((1, LANES), lambda i: (0, 0))],
        out_specs=pl.BlockSpec((S, LANES), lambda i: (0, 0)),
        out_shape=jax.ShapeDtypeStruct((S, LANES), F32), compiler_params=_params(("arbitrary",)),
    )(proj, b_f)


def _fox_gate_bwd(dcum, proj, b_f, n_heads, name):
    S, PW = proj.shape
    blk = min(256, S)
    last = PW // LANES - 1
    nb = S // blk

    def body(dc_ref, fl_ref, b_ref, dfl_ref, db_ref):
        r = lax.broadcasted_iota(jnp.int32, (blk, blk), 0)
        c = lax.broadcasted_iota(jnp.int32, (blk, blk), 1)
        triu = (c >= r).astype(F32)
        lane = lax.broadcasted_iota(jnp.int32, (blk, LANES), 1)
        carry = jnp.zeros((1, LANES), F32)
        dbs = jnp.zeros((1, LANES), F32)
        for i in reversed(range(nb)):
            dc = dc_ref[i * blk:(i + 1) * blk, :]
            dlf = jnp.dot(triu, dc, preferred_element_type=F32, precision=HIGHEST) + carry
            carry = carry + jnp.sum(dc, axis=0, keepdims=True)
            z = fl_ref[i * blk:(i + 1) * blk, :] + b_ref[...]
            e = jnp.exp(-jnp.abs(z))
            sig_neg = jnp.where(z >= 0, e / (1.0 + e), 1.0 / (1.0 + e))
            dfl = jnp.where(lane < n_heads, dlf * sig_neg, 0.0)
            dfl_ref[i * blk:(i + 1) * blk, :] = dfl.astype(BF16)
            dbs = dbs + jnp.sum(dfl, axis=0, keepdims=True)
        db_ref[...] = dbs

    return pl.pallas_call(
        body, name=name, grid=(1,),
        in_specs=[pl.BlockSpec((S, LANES), lambda i: (0, 0)), pl.BlockSpec((S, LANES), lambda i: (0, last)),
                  pl.BlockSpec((1, LANES), lambda i: (0, 0))],
        out_specs=[pl.BlockSpec((S, LANES), lambda i: (0, 0)), pl.BlockSpec((1, LANES), lambda i: (0, 0))],
        out_shape=[jax.ShapeDtypeStruct((S, LANES), BF16), jax.ShapeDtypeStruct((1, LANES), F32)],
        compiler_params=_params(("arbitrary",)),
    )(dcum, proj, b_f)


def _fox_scores(q_ref, kb_ref, cq_ref, ck_ref, qi, tq, scale):
    kk = (qi + 1) * tq
    rows = slice(qi * tq, (qi + 1) * tq)
    qb = q_ref[rows, :].astype(BF16)
    s = lax.dot_general(qb, kb_ref[0:kk, :], _DN["nt"], preferred_element_type=F32) * scale
    s = s + (cq_ref[0, rows, :] - ck_ref[0, :, 0:kk])
    r = lax.broadcasted_iota(jnp.int32, (tq, kk), 0) + qi * tq
    c = lax.broadcasted_iota(jnp.int32, (tq, kk), 1)
    mask = c <= r
    return jnp.where(mask, s, NEG), mask, qb


def _fox_fwd(proj, cq, ck, n_heads, name, comm=None):
    S = proj.shape[0]
    H = n_heads
    tq = min(FOX_TQ, S)
    nq = S // tq
    scale = FDH ** -0.5

    def body(q_ref, k_ref, v_ref, cq_ref, ck_ref, o_ref, lse_ref, kb_ref, vb_ref):
        kb_ref[...] = k_ref[...].astype(BF16)
        vb_ref[...] = v_ref[...].astype(BF16)
        for qi in range(nq):
            kk = (qi + 1) * tq
            rows = slice(qi * tq, (qi + 1) * tq)
            s, _, _ = _fox_scores(q_ref, kb_ref, cq_ref, ck_ref, qi, tq, scale)
            m = jnp.max(s, axis=-1, keepdims=True)
            p = jnp.exp(s - m)
            l = jnp.sum(p, axis=-1, keepdims=True)
            p = p * (1.0 / l)
            o_ref[rows, :] = jnp.dot(p.astype(BF16), vb_ref[0:kk, :], preferred_element_type=F32).astype(BF16)
            lse_ref[0, rows, :] = m + jnp.log(l)

    col = lambda off: pl.BlockSpec((S, FDH), lambda h: (0, h + off))
    stat_c = pl.BlockSpec((1, S, 1), lambda h: (h, 0, 0))
    stat_r = pl.BlockSpec((1, 1, S), lambda h: (h, 0, 0))
    return _call(
        body, name=name, grid=(H,),
        in_specs=[col(0), col(H), col(2 * H), stat_c, stat_r],
        out_specs=[col(0), stat_c],
        out_shape=[jax.ShapeDtypeStruct((S, H * FDH), BF16), jax.ShapeDtypeStruct((H, S, 1), F32)],
        scratch_shapes=[pltpu.VMEM((S, FDH), BF16), pltpu.VMEM((S, FDH), BF16)],
        sem=("parallel",), args=(proj, proj, proj, cq, ck), comm=comm)


def _fox_bwd(proj, cq, ck, lse, do, n_heads, name, comm=None):
    S = proj.shape[0]
    H = n_heads
    tq = min(FOX_TQ, S)
    nq = S // tq
    scale = FDH ** -0.5

    def body(q_ref, k_ref, v_ref, cq_ref, ck_ref, lse_ref, do_ref, dq_ref, dk_ref, dv_ref, dcq_ref, dck_ref,
             kb_ref, vb_ref, dka_ref, dva_ref):
        kb_ref[...] = k_ref[...].astype(BF16)
        vb_ref[...] = v_ref[...].astype(BF16)
        dka_ref[...] = jnp.zeros_like(dka_ref)
        dva_ref[...] = jnp.zeros_like(dva_ref)
        dck_ref[...] = jnp.zeros_like(dck_ref)
        for qi in range(nq):
            kk = (qi + 1) * tq
            rows = slice(qi * tq, (qi + 1) * tq)
            s, mask, qb = _fox_scores(q_ref, kb_ref, cq_ref, ck_ref, qi, tq, scale)
            p = jnp.where(mask, jnp.exp(s - lse_ref[0, rows, :]), 0.0)
            dob = do_ref[rows, :]
            dp = lax.dot_general(dob, vb_ref[0:kk, :], _DN["nt"], preferred_element_type=F32)
            delta = jnp.sum(p * dp, axis=-1, keepdims=True)
            ds = p * (dp - delta)
            dcq_ref[0, rows, :] = jnp.sum(ds, axis=-1, keepdims=True)
            dck_ref[0, :, 0:kk] -= jnp.sum(ds, axis=0, keepdims=True)
            dsb = (ds * scale).astype(BF16)
            dq_ref[rows, :] = jnp.dot(dsb, kb_ref[0:kk, :], preferred_element_type=F32).astype(BF16)
            dka_ref[0:kk, :] += lax.dot_general(dsb, qb, _DN["tn"], preferred_element_type=F32)
            dva_ref[0:kk, :] += lax.dot_general(p.astype(BF16), dob, _DN["tn"], preferred_element_type=F32)
        dk_ref[...] = dka_ref[...].astype(BF16)
        dv_ref[...] = dva_ref[...].astype(BF16)

    col = lambda off: pl.BlockSpec((S, FDH), lambda h: (0, h + off))
    stat_c = pl.BlockSpec((1, S, 1), lambda h: (h, 0, 0))
    stat_r = pl.BlockSpec((1, 1, S), lambda h: (h, 0, 0))
    wide = jax.ShapeDtypeStruct((S, H * FDH), BF16)
    return _call(
        body, name=name, grid=(H,),
        in_specs=[col(0), col(H), col(2 * H), stat_c, stat_r, stat_c, col(0)],
        out_specs=[col(0), col(0), col(0), stat_c, stat_r],
        out_shape=[wide, wide, wide, jax.ShapeDtypeStruct((H, S, 1), F32), jax.ShapeDtypeStruct((H, 1, S), F32)],
        scratch_shapes=[pltpu.VMEM((S, FDH), BF16), pltpu.VMEM((S, FDH), BF16), pltpu.VMEM((S, FDH), F32), pltpu.VMEM((S, FDH), F32)],
        sem=("parallel",), args=(proj, proj, proj, cq, ck, lse, do), comm=comm)


def _rope_tables(pos, sign):
    inv = ROPE_THETA ** (-jnp.arange(0, ROPE_DIM, 2, dtype=F32) / ROPE_DIM)
    ang = pos.astype(F32)[:, None] * inv
    cos, sin = jnp.cos(ang), sign * jnp.sin(ang)
    l64 = jnp.arange(LANES) % SDH
    idx = l64 % (ROPE_DIM // 2)
    c = jnp.where(l64 < ROPE_DIM, cos[:, idx], 1.0)
    sa = jnp.where(l64 < ROPE_DIM // 2, -sin[:, idx], 0.0)
    sb = jnp.where((l64 >= ROPE_DIM // 2) & (l64 < ROPE_DIM), sin[:, idx], 0.0)
    rot = jnp.stack([c, sa, sb])
    ident = jnp.stack([jnp.ones_like(c), jnp.zeros_like(c), jnp.zeros_like(c)])
    return jnp.stack([rot, ident]).astype(F32)


def _rope(xin, tabs, n_rot, out_dtype, name):
    S, W = xin.shape

    def body(x_ref, t_ref, o_ref):
        xv = x_ref[...]
        o = xv * t_ref[0, 0] + pltpu.roll(xv, LANES - ROPE_DIM // 2, 1) * t_ref[0, 1] + pltpu.roll(xv, ROPE_DIM // 2, 1) * t_ref[0, 2]
        o_ref[...] = o.astype(out_dtype)

    return pl.pallas_call(
        body, name=name, grid=(W // LANES,),
        in_specs=[pl.BlockSpec((S, LANES), lambda j: (0, j)),
                  pl.BlockSpec((1, 3, S, LANES), lambda j: (jnp.where(j < n_rot, 0, 1), 0, 0, 0))],
        out_specs=pl.BlockSpec((S, LANES), lambda j: (0, j)),
        out_shape=jax.ShapeDtypeStruct((S, W), out_dtype), compiler_params=_params(("parallel",)),
    )(xin, tabs)


def _swa_probs(q_ref, k_ref, sk_ref, n, scale):
    st = pl.multiple_of(jnp.maximum(n - 1, 0) * WIN, WIN)
    qb = q_ref[0, 0]
    kb = k_ref[0, pl.ds(st, 2 * WIN), :]
    gm = qb.shape[0]
    s = lax.dot_general(qb, kb, _DN["nt"], preferred_element_type=F32) * scale
    qa = n * WIN + (lax.broadcasted_iota(jnp.int32, (gm, 2 * WIN), 0) & (WIN - 1))
    ka = st + lax.broadcasted_iota(jnp.int32, (gm, 2 * WIN), 1)
    valid = (ka <= qa) & (qa - ka < WIN)
    s = jnp.where(valid, s, NEG)
    sink = sk_ref[0]
    m = jnp.maximum(jnp.max(s, axis=-1, keepdims=True), sink)
    e = jnp.where(valid, jnp.exp(s - m), 0.0)
    es = jnp.exp(sink - m)
    inv = 1.0 / (jnp.sum(e, axis=-1, keepdims=True) + es)
    return e * inv, es * inv, st, qb, kb


def _swa_specs(S, gm):
    blk = pl.BlockSpec((1, 1, gm, SDH), lambda g, n: (g, n, 0, 0))
    kv = pl.BlockSpec((1, S, SDH), lambda g, n: (g, 0, 0))
    col = pl.BlockSpec((1, gm, 1), lambda g, n: (g, 0, 0))
    return blk, kv, col


def _swa_fwd(q, k, v, sinks, name, comm=None):
    KH, nb, gm, _ = q.shape
    S = k.shape[1]
    scale = SDH ** -0.5

    def body(q_ref, k_ref, v_ref, sk_ref, o_ref):
        p, _, st, _, _ = _swa_probs(q_ref, k_ref, sk_ref, pl.program_id(1), scale)
        vb = v_ref[0, pl.ds(st, 2 * WIN), :]
        o_ref[0, 0] = jnp.dot(p.astype(BF16), vb, preferred_element_type=F32).astype(BF16)

    blk, kv, col = _swa_specs(S, gm)
    return _call(
        body, name=name, grid=(KH, nb), in_specs=[blk, kv, kv, col], out_specs=[blk],
        out_shape=[jax.ShapeDtypeStruct(q.shape, BF16)], sem=("parallel", "parallel"), args=(q, k, v, sinks), comm=comm)[0]


def _swa_bwd(q, k, v, sinks, do, name, comm=None):
    KH, nb, gm, _ = q.shape
    S = k.shape[1]
    scale = SDH ** -0.5

    def body(q_ref, k_ref, v_ref, sk_ref, do_ref, dq_ref, dk_ref, dv_ref, dsk_ref):
        n = pl.program_id(1)

        @pl.when(n == 0)
        def _():
            dk_ref[...] = jnp.zeros_like(dk_ref)
            dv_ref[...] = jnp.zeros_like(dv_ref)
            dsk_ref[...] = jnp.zeros_like(dsk_ref)

        p, ps, st, qb, kb = _swa_probs(q_ref, k_ref, sk_ref, n, scale)
        vb = v_ref[0, pl.ds(st, 2 * WIN), :]
        dob = do_ref[0, 0]
        dp = lax.dot_general(dob, vb, _DN["nt"], preferred_element_type=F32)
        delta = jnp.sum(p * dp, axis=-1, keepdims=True)
        ds = p * (dp - delta)
        dsb = (ds * scale).astype(BF16)
        dq_ref[0, 0] = jnp.dot(dsb, kb, preferred_element_type=F32)
        dk_ref[0, pl.ds(st, 2 * WIN), :] += lax.dot_general(dsb, qb, _DN["tn"], preferred_element_type=F32)
        dv_ref[0, pl.ds(st, 2 * WIN), :] += lax.dot_general(p.astype(BF16), dob, _DN["tn"], preferred_element_type=F32)
        dsk_ref[0] -= ps * delta

    blk, kv, col = _swa_specs(S, gm)
    return _call(
        body, name=name, grid=(KH, nb), in_specs=[blk, kv, kv, col, blk], out_specs=[blk, kv, kv, col],
        out_shape=[jax.ShapeDtypeStruct(q.shape, F32), jax.ShapeDtypeStruct(k.shape, F32),
                   jax.ShapeDtypeStruct(k.shape, F32), jax.ShapeDtypeStruct(sinks.shape, F32)],
        sem=("parallel", "arbitrary"), args=(q, k, v, sinks, do), comm=comm)


def _shift_down(u, k):
    row = lax.broadcasted_iota(jnp.int32, u.shape, 0)
    return jnp.where(row >= k, pltpu.roll(u, k, 0), 0.0)


def _shift_up(u, k):
    n = u.shape[0]
    row = lax.broadcasted_iota(jnp.int32, u.shape, 0)
    return jnp.where(row < n - k, pltpu.roll(u, n - k, 0), 0.0)


def _conv3(u, w_ref, b_ref):
    return w_ref[0:1, :] * _shift_down(u, 2) + w_ref[1:2, :] * _shift_down(u, 1) + w_ref[2:3, :] * u + b_ref[...]


def _conv_gate(u, cw, cb, name, comm=None):
    S, F2 = u.shape
    Fh = F2 // 2
    tc = _tile(Fh, 256)
    nf = Fh // tc

    def body(ug_ref, uv_ref, wg_ref, wv_ref, bg_ref, bv_ref, a_ref):
        g = _conv3(ug_ref[...], wg_ref, bg_ref)
        val = _conv3(uv_ref[...], wv_ref, bv_ref)
        a_ref[...] = (g * (1.0 / (1.0 + jnp.exp(-g))) * val).astype(BF16)

    blk = lambda r, off: pl.BlockSpec((r, tc), lambda j: (0, j + off))
    return _call(
        body, name=name, grid=(nf,),
        in_specs=[blk(S, 0), blk(S, nf), blk(3, 0), blk(3, nf), blk(1, 0), blk(1, nf)], out_specs=[blk(S, 0)],
        out_shape=[jax.ShapeDtypeStruct((S, Fh), BF16)], sem=("parallel",), args=(u, u, cw, cw, cb, cb), comm=comm)[0]


def _conv_gate_bwd(u, da, cw, cb, name, comm=None):
    S, F2 = u.shape
    Fh = F2 // 2
    tc = _tile(Fh, 256)
    nf = Fh // tc

    def half(h, dx, uu, w_ref, du_ref, dw_ref, db_ref):
        du = w_ref[2:3, :] * dx + w_ref[1:2, :] * _shift_up(dx, 1) + w_ref[0:1, :] * _shift_up(dx, 2)
        du_ref[h] = du.astype(BF16)
        dw_ref[h, 0:1, :] = jnp.sum(dx * _shift_down(uu, 2), axis=0, keepdims=True)
        dw_ref[h, 1:2, :] = jnp.sum(dx * _shift_down(uu, 1), axis=0, keepdims=True)
        dw_ref[h, 2:3, :] = jnp.sum(dx * uu, axis=0, keepdims=True)
        db_ref[h] = jnp.sum(dx, axis=0, keepdims=True)

    def body(ug_ref, uv_ref, da_ref, wg_ref, wv_ref, bg_ref, bv_ref, du_ref, dw_ref, db_ref):
        ug = ug_ref[...]
        uv = uv_ref[...]
        g = _conv3(ug, wg_ref, bg_ref)
        val = _conv3(uv, wv_ref, bv_ref)
        sig = 1.0 / (1.0 + jnp.exp(-g))
        da_ = da_ref[...]
        dg = da_ * val * (sig * (1.0 + g * (1.0 - sig)))
        dval = da_ * (g * sig)
        half(0, dg, ug, wg_ref, du_ref, dw_ref, db_ref)
        half(1, dval, uv, wv_ref, du_ref, dw_ref, db_ref)

    blk = lambda r, off: pl.BlockSpec((r, tc), lambda j: (0, j + off))
    both = lambda r: pl.BlockSpec((2, r, tc), lambda j: (0, 0, j))
    return _call(
        body, name=name, grid=(nf,),
        in_specs=[blk(S, 0), blk(S, nf), blk(S, 0), blk(3, 0), blk(3, nf), blk(1, 0), blk(1, nf)],
        out_specs=[both(S), both(3), both(1)],
        out_shape=[jax.ShapeDtypeStruct((2, S, Fh), BF16), jax.ShapeDtypeStruct((2, 3, Fh), F32), jax.ShapeDtypeStruct((2, 1, Fh), F32)],
        sem=("parallel",), args=(u, u, da, cw, cw, cb, cb), comm=comm)


def _to_groups(t, kh):
    S, width = t.shape
    g = width // SDH // kh
    return t.reshape(S // WIN, WIN, kh, g, SDH).transpose(2, 0, 3, 1, 4).reshape(kh, S // WIN, g * WIN, SDH)


def _from_groups(t):
    kh, nb, gm, _ = t.shape
    g = gm // WIN
    return t.reshape(kh, nb, g, WIN, SDH).transpose(1, 3, 0, 2, 4).reshape(nb * WIN, kh * g * SDH)


class LocalWeights:
    def __init__(self, weights):
        self.weights, self.grads = weights, {}

    def w(self, name):
        return self.weights[name]

    def carry(self, stage):
        return None

    def carried(self, stage, comm):
        pass

    def grad(self, name, g):
        self.grads[name] = g


def _local_step(dm, x, tgt, pos, mod, sp, pp):
    S, D, FH, QH, KH, Fh = dm
    m = [[mod[i:i + 1, j * D:(j + 1) * D] for j in range(6)] for i in range(DEPTH)]

    def run(fn, *args, name, **kw):
        comm = pp.carry(name)
        out = fn(*args, name=name, comm=comm, **kw)
        if comm is not None:
            pp.carried(name, comm)
        return out

    sv = []
    xs = x
    h = _modulate(xs, m[0][1], m[0][0], "mod_in")
    for i in range(DEPTH):
        sh1, sc1, g1, sh2, sc2, g2 = m[i]
        L = {}
        L["x_in"], L["h1"] = xs, h
        if i == 0:
            proj = run(_mm, h[None], pp.w("fox_w_in"), mode="nn", out_dtype=F32, name="fox_proj", tn=896)[0]
            cum = _fox_gate_fwd(proj, sp["fox_b_f"], FH, "fox_gate")
            cq = cum[:, :FH].T[:, :, None]
            ck = cum[:, :FH].T[:, None, :]
            o, lse = run(_fox_fwd, proj, cq, ck, FH, name="fox_attn")
            L.update(proj=proj, cq=cq, ck=ck, lse=lse, o=o)
            y = run(_mm, o[None], pp.w("fox_w_o"), mode="nn", out_dtype=F32, name="fox_out")[0]
        else:
            proj = run(_mm, h[None], pp.w("swa_w_in"), mode="nn", out_dtype=F32, name="swa_proj", tn=640)[0]
            tabs = _rope_tables(pos, 1.0)
            n_rot = (QH + KH) * SDH // LANES
            pr = _rope(proj, tabs, n_rot, BF16, "swa_rope")
            qh = _to_groups(pr[:, :QH * SDH], KH)
            kh = pr[:, QH * SDH:(QH + KH) * SDH].reshape(S, KH, SDH).transpose(1, 0, 2)
            vh = pr[:, (QH + KH) * SDH:].reshape(S, KH, SDH).transpose(1, 0, 2)
            oh = run(_swa_fwd, qh, kh, vh, sp["sinks"], name="swa_attn")
            o = _from_groups(oh)
            L.update(qh=qh, kh=kh, vh=vh, o=o)
            y = run(_mm, o[None], pp.w("swa_w_o"), mode="nn", out_dtype=F32, name="swa_out")[0]
        L["y1"] = y
        x1, L["xh1"], L["rs1"], h2 = _ln_fwd(xs, y, g1, sp["ln_mix_g"][i], sp["ln_mix_b"][i], sc2, sh2, f"ln_mix{i}")
        L["x1"], L["h2"] = x1, h2
        u = run(_mm, h2[None], pp.w(f"ffn_w_up{i}"), mode="nn", out_dtype=F32, name=f"ffn_up{i}", tm=512, tn=1408)[0]
        a = run(_conv_gate, u, sp["conv_w"][i], sp["conv_b"][i], name=f"ffn_gate{i}")
        y2 = run(_mm, a[None], pp.w(f"ffn_w_down{i}"), mode="nn", out_dtype=F32, name=f"ffn_down{i}", tk=1408)[0]
        L.update(u=u, a=a, y2=y2)
        if i + 1 < DEPTH:
            xs, L["xh2"], L["rs2"], h = _ln_fwd(x1, y2, g2, sp["ln_ffn_g"][i], sp["ln_ffn_b"][i], m[i + 1][1], m[i + 1][0], f"ln_ffn{i}")
        else:
            xs, L["xh2"], L["rs2"] = _ln_fwd(x1, y2, g2, sp["ln_ffn_g"][i], sp["ln_ffn_b"][i], None, None, f"ln_ffn{i}")
        sv.append(L)

    dx, loss_cols = _loss_head(xs, tgt, "loss_head")

    gs = {k: [None] * DEPTH for k in ("conv_w", "conv_b", "ln_mix_g", "ln_mix_b", "ln_ffn_g", "ln_ffn_b")}
    dmod = [None] * DEPTH
    for i in reversed(range(DEPTH)):
        sh1, sc1, g1, sh2, sc2, g2 = m[i]
        L = sv[i]
        dres, dy, gs["ln_ffn_g"][i], gs["ln_ffn_b"][i], dg2 = _ln_bwd(dx, L["xh2"], L["rs2"], sp["ln_ffn_g"][i], L["y2"], g2, f"ln_ffn_bwd{i}")
        da = run(_mm, dy[None], pp.w(f"ffn_w_down{i}"), mode="nt", out_dtype=F32, name=f"ffn_down_dx{i}", tm=512, tn=1408)[0]
        pp.grad(f"ffn_w_down{i}", run(_mm, L["a"][None], dy[None], mode="tn", out_dtype=BF16, name=f"ffn_down_dw{i}", tm=1408))
        du, dcw, dcb = run(_conv_gate_bwd, L["u"], da, sp["conv_w"][i], sp["conv_b"][i], name=f"ffn_gate_bwd{i}")
        gs["conv_w"][i] = dcw.transpose(1, 0, 2).reshape(3, 2 * Fh)
        gs["conv_b"][i] = dcb.transpose(1, 0, 2).reshape(1, 2 * Fh)
        dh2 = run(_mm, du, pp.w(f"ffn_w_up{i}"), mode="nt", out_dtype=F32, name=f"ffn_up_dx{i}", tk=1408)[0]
        pp.grad(f"ffn_w_up{i}", run(_mm, L["h2"][None], du, mode="tn", out_dtype=BF16, name=f"ffn_up_dw{i}", out_groups=N_CHIPS, tn=1408))
        dx, dsc2, dsh2 = _mod_bwd(dh2, L["x1"], sc2, dres, f"mod_ffn_bwd{i}")
        dres, dy, gs["ln_mix_g"][i], gs["ln_mix_b"][i], dg1 = _ln_bwd(dx, L["xh1"], L["rs1"], sp["ln_mix_g"][i], L["y1"], g1, f"ln_mix_bwd{i}")
        if i == 0:
            do = run(_mm, dy[None], pp.w("fox_w_o"), mode="nt", out_dtype=BF16, name="fox_out_dx")[0]
            pp.grad("fox_w_o", run(_mm, L["o"][None], dy[None], mode="tn", out_dtype=BF16, name="fox_out_dw"))
            dq, dk, dv, dcq, dck = run(_fox_bwd, L["proj"], L["cq"], L["ck"], L["lse"], do, FH, name="fox_attn_bwd")
            dcum = dcq[:, :, 0].T + dck[:, 0, :].T
            dcum = jnp.pad(dcum, ((0, 0), (0, LANES - FH)))
            dfl, db_f = _fox_gate_bwd(dcum, L["proj"], sp["fox_b_f"], FH, "fox_gate_bwd")
            gs["fox_b_f"] = db_f
            dproj = jnp.concatenate([dq, dk, dv, dfl], axis=1)
            pp.grad("fox_w_in", run(_mm, L["h1"][None], dproj[None], mode="tn", out_dtype=BF16, name="fox_proj_dw", tn=896))
            dh1 = run(_mm, dproj[None], pp.w("fox_w_in"), mode="nt", out_dtype=F32, name="fox_proj_dx", tk=896)[0]
        else:
            do = run(_mm, dy[None], pp.w("swa_w_o"), mode="nt", out_dtype=BF16, name="swa_out_dx")[0]
            pp.grad("swa_w_o", run(_mm, L["o"][None], dy[None], mode="tn", out_dtype=BF16, name="swa_out_dw"))
            dqh, dkh, dvh, dsk = run(_swa_bwd, L["qh"], L["kh"], L["vh"], sp["sinks"], _to_groups(do, KH), name="swa_attn_bwd")
            gs["sinks"] = jnp.sum(dsk.reshape(QH, WIN), axis=1)
            dpr = jnp.concatenate([_from_groups(dqh), dkh.transpose(1, 0, 2).reshape(S, KH * SDH),
                                   dvh.transpose(1, 0, 2).reshape(S, KH * SDH)], axis=1)
            n_rot = (QH + KH) * SDH // LANES
            dproj = _rope(dpr, _rope_tables(pos, -1.0), n_rot, BF16, "swa_rope_bwd")
            dh1 = run(_mm, dproj[None], pp.w("swa_w_in"), mode="nt", out_dtype=F32, name="swa_proj_dx", tk=640)[0]
            pp.grad("swa_w_in", run(_mm, L["h1"][None], dproj[None], mode="tn", out_dtype=BF16, name="swa_proj_dw", out_groups=N_CHIPS, tn=640))
        dx, dsc1, dsh1 = _mod_bwd(dh1, L["x_in"], sc1, dres, f"mod_mix_bwd{i}")
        dmod[i] = jnp.concatenate([dsh1, dsc1, dg1, dsh2, dsc2, dg2], axis=1)
    return loss_cols, dx, gs, jnp.concatenate(dmod, axis=0)


def _allgather_small(v, name):
    m_per, n = v.shape

    def body(x_ref, out_ref, send_sems, recv_sems, local_sem):
        x, y, c, chips = _place()
        me, sibling = (x, y, c), (x, y, 1 - c)

        def rows(px, py, pc):
            return out_ref.at[pl.ds((4 * px + 2 * py + pc) * m_per, m_per), :]

        def copy(k, block, to, src=None):
            return _remote(rows(*block) if src is None else src, rows(*block), send_sems.at[k], recv_sems.at[k], to)

        mine = pltpu.make_async_copy(x_ref, rows(*me), local_sem)
        mine.start()
        first = [copy(0, me, sibling, src=x_ref)]
        first += [copy(1 + j, me, (*chip, c), src=x_ref) for j, chip in enumerate(chips)]
        for cp in first:
            cp.start()
        passed = [copy(4 + j, (*chip, c), sibling) for j, chip in enumerate(chips)]
        for j, chip in enumerate(chips):
            copy(1 + j, (*chip, c), me).wait_recv()
            passed[j].start()
        copy(0, sibling, me).wait_recv()
        for j, chip in enumerate(chips):
            copy(4 + j, (*chip, 1 - c), me).wait_recv()
        for cp in first + passed:
            cp.wait_send()
        mine.wait()

    return pl.pallas_call(
        body, name=name, out_shape=jax.ShapeDtypeStruct((N_DEV * m_per, n), v.dtype),
        in_specs=[pl.BlockSpec(memory_space=pltpu.VMEM)], out_specs=pl.BlockSpec(memory_space=pltpu.VMEM),
        scratch_shapes=[pltpu.SemaphoreType.DMA((7,)), pltpu.SemaphoreType.DMA((7,)), pltpu.SemaphoreType.DMA],
        compiler_params=pltpu.CompilerParams(vmem_limit_bytes=VMEM_LIMIT),
    )(v)


def _row_tile(r, pref=256):
    return _tile(r, pref, 16)


def _cast_bf16(w, layer, chip, name):
    _, R, C = w.shape
    tr = _row_tile(R)

    def body(s_ref, w_ref, o_ref):
        o_ref[...] = w_ref[...].astype(BF16)

    return pl.pallas_call(
        body, name=name,
        grid_spec=pltpu.PrefetchScalarGridSpec(
            num_scalar_prefetch=1, grid=(R // tr,),
            in_specs=[pl.BlockSpec((None, tr, C), lambda i, s: (layer, i, 0))],
            out_specs=pl.BlockSpec((None, tr, C), lambda i, s: (s[0], i, 0))),
        out_shape=jax.ShapeDtypeStruct((N_CHIPS, R, C), BF16), compiler_params=_params(("parallel",)),
    )(jnp.reshape(chip, (1,)).astype(jnp.int32), w)


def _add_sibling(g, got, c, name):
    G, R, C = g.shape
    rh = R // 2
    tr = _row_tile(rh)
    nb = rh // tr

    def body(c_ref, g_ref, o_ref, p_ref):
        p_ref[...] = (g_ref[...].astype(F32) + o_ref[...].astype(F32)).astype(BF16)

    return pl.pallas_call(
        body, name=name,
        grid_spec=pltpu.PrefetchScalarGridSpec(
            num_scalar_prefetch=1, grid=(G, nb),
            in_specs=[pl.BlockSpec((1, tr, C), lambda s, i, c_ref: (s, c_ref[0] * nb + i, 0)),
                      pl.BlockSpec((1, tr, C), lambda s, i, c_ref: (s, i, 0))],
            out_specs=pl.BlockSpec((1, tr, C), lambda s, i, c_ref: (s, i, 0))),
        out_shape=jax.ShapeDtypeStruct((G, rh, C), BF16), compiler_params=_params(("parallel", "parallel")),
    )(jnp.reshape(c, (1,)).astype(jnp.int32), g, got)


def _sum_chips(part, landed, chip, c, name):
    G, rh, C = part.shape
    tr = _row_tile(rh)
    nb = rh // tr

    def body(p_ref, own_ref, *rest):
        acc = own_ref[...].astype(F32)
        for ref in rest[:G - 1]:
            acc = acc + ref[...].astype(F32)
        rest[G - 1][...] = acc

    slot = lambda k: pl.BlockSpec((None, tr, C), lambda i, p: ((p[0] + k) % G, i, 0))
    return pl.pallas_call(
        body, name=name,
        grid_spec=pltpu.PrefetchScalarGridSpec(
            num_scalar_prefetch=1, grid=(nb,), in_specs=[slot(k) for k in range(G)],
            out_specs=pl.BlockSpec((tr, C), lambda i, p: (p[1] * nb + i, 0))),
        out_shape=jax.ShapeDtypeStruct((2 * rh, C), F32), compiler_params=_params(("parallel",)),
    )(jnp.stack([chip, c]).astype(jnp.int32), part, *([landed] * (G - 1)))


def _adam_math(w, g, m, v):
    m = ADAM_B1 * m + (1.0 - ADAM_B1) * g
    v = ADAM_B2 * v + (1.0 - ADAM_B2) * (g * g)
    m_hat = m / (1.0 - ADAM_B1 ** ADAM_STEP)
    v_hat = v / (1.0 - ADAM_B2 ** ADAM_STEP)
    delta = -ADAM_LR * (m_hat / (jnp.sqrt(v_hat) + ADAM_EPS) + ADAM_WD * w)
    return delta, m, v


def _adamw(w, g, m, v, layer, prev, name, by_cols=False):
    L, R, C = w.shape
    tr = R if by_cols else _tile(R, 128, 8)
    tc = _tile(C, 256) if by_cols else C
    n_prev = len(prev)

    def body(w_ref, g_ref, m_ref, v_ref, *rest):
        go_ref, d_ref, mo_ref, vo_ref = rest[n_prev:]
        gv = g_ref[...]
        go_ref[...] = gv
        d_ref[...], mo_ref[...], vo_ref[...] = _adam_math(w_ref[...], gv, m_ref[...], v_ref[...])

    lay = pl.BlockSpec((None, tr, tc), lambda i: (layer, i // (C // tc), i % (C // tc)))
    flat = pl.BlockSpec((tr, tc), lambda i: (i // (C // tc), i % (C // tc)))
    return _call(
        body, name=name, grid=((R // tr) * (C // tc),), in_specs=[lay, flat, lay, lay] + _any_specs(n_prev), out_specs=[lay] * 4,
        out_shape=[jax.ShapeDtypeStruct((L, R, C), F32)] * 4, aliases={4 + k: k for k in range(n_prev)},
        sem=("parallel",), args=(w, g, m, v, *prev))


def _cond_rows(c_row, cw, name):
    D = c_row.shape[1]
    nr, fc = cw.shape

    def body(c_ref, e_ref, o_ref):
        o_ref[...] = jnp.zeros_like(o_ref)
        cv = c_ref[...]
        o_ref[0:1, 0:D] = cv * (1.0 / (1.0 + jnp.exp(-cv)))
        o_ref[8:8 + nr, 0:fc] = e_ref[...]

    return pl.pallas_call(body, name=name, out_shape=jax.ShapeDtypeStruct((16, max(D, fc)), F32))(c_row, cw)


def _ada_fwd(cact, ada_w, ada_b, layer, chip, name):
    _, D, NC = ada_w.shape
    tn = _tile(NC, 1024)
    nj = NC // tn

    def body(idx_ref, c_ref, w_ref, b_ref, o_ref):
        acc = jnp.dot(c_ref[...].astype(BF16), w_ref[0].astype(BF16), preferred_element_type=F32)
        o_ref[...] = acc + b_ref[pl.ds(idx_ref[0], 1), :]

    return pl.pallas_call(
        body, name=name,
        grid_spec=pltpu.PrefetchScalarGridSpec(
            num_scalar_prefetch=1, grid=(nj,),
            in_specs=[pl.BlockSpec((8, D), lambda j, idx: (0, 0)),
                      pl.BlockSpec((1, D, tn), lambda j, idx: (idx[0], 0, j)),
                      pl.BlockSpec((DEPTH, tn), lambda j, idx: (0, idx[1] * nj + j))],
            out_specs=pl.BlockSpec((8, tn), lambda j, idx: (0, j))),
        out_shape=jax.ShapeDtypeStruct((8, NC), F32), compiler_params=_params(("parallel",)),
    )(jnp.stack([layer, chip]).astype(jnp.int32), cact, ada_w, ada_b)


def _ada_grad_adamw(cact_t, dmod, w, m, v, name, comm=None):
    L, D, NC = w.shape
    tr = _tile(D, 128, 8)

    def body(c_ref, d_ref, w_ref, m_ref, v_ref, g_ref, dl_ref, mo_ref, vo_ref):
        g = jnp.dot(c_ref[...], d_ref[...], preferred_element_type=F32, precision=HIGHEST)
        g_ref[...] = g
        dl_ref[...], mo_ref[...], vo_ref[...] = _adam_math(w_ref[...], g, m_ref[...], v_ref[...])

    lay = pl.BlockSpec((None, tr, NC), lambda l, i: (l, i, 0))
    return _call(
        body, name=name, grid=(L, D // tr),
        in_specs=[pl.BlockSpec((tr, N_DEV), lambda l, i: (i, 0)), pl.BlockSpec((None, N_DEV, NC), lambda l, i: (l, 0, 0)), lay, lay, lay],
        out_specs=[lay] * 4, out_shape=[jax.ShapeDtypeStruct((L, D, NC), F32)] * 4,
        sem=("parallel", "parallel"), args=(cact_t, dmod, w, m, v), comm=comm)


def _sum_devices(gathered, name):
    n, R, C = gathered.shape

    def body(g_ref, o_ref):
        acc = g_ref[0]
        for j in range(1, n):
            acc = acc + g_ref[j]
        o_ref[...] = acc

    return pl.pallas_call(body, name=name, out_shape=jax.ShapeDtypeStruct((R, C), F32),
                          compiler_params=pltpu.CompilerParams(vmem_limit_bytes=VMEM_LIMIT))(gathered)


def _adamw_small(w, g, m, v, name):
    def body(w_ref, g_ref, m_ref, v_ref, d_ref, mo_ref, vo_ref):
        d_ref[...], mo_ref[...], vo_ref[...] = _adam_math(w_ref[...], g_ref[...], m_ref[...], v_ref[...])

    return pl.pallas_call(body, name=name, out_shape=[jax.ShapeDtypeStruct(w.shape, F32)] * 3)(w, g, m, v)


def _pad_rows(flat, unit=8 * LANES):
    n = flat.shape[0]
    total = -(-n // unit) * unit
    return jnp.pad(flat, (0, total - n)).reshape(total // LANES, LANES)


def _pad_lanes(v2d):
    return jnp.pad(v2d.reshape(1, -1), ((0, 0), (0, LANES - v2d.size)))


PLAN = {
    "fox_proj": [("gather", "ffn_w_up0", 0, 2)],
    "fox_attn": [("gather", "ffn_w_up0", 1, 2)],
    "ffn_up0": [("gather", "ffn_w_down0", 0, 1)],
    "ffn_gate0": [("gather", "swa_w_in", 0, 1), ("gather", "swa_w_o", 0, 1)],
    "ffn_down0": [("gather", "ffn_w_up1", 0, 2)],
    "swa_attn": [("gather", "ffn_w_up1", 1, 2)],
    "ffn_up1": [("gather", "ffn_w_down1", 0, 1)],
    "ffn_gate_bwd1": [("swap", "ffn_w_down1")],
    "ffn_up_dx1": [("scatter", "ffn_w_down1", 0, 1)],
    "swa_out_dx": [("swap", "ffn_w_up1")],
    "swa_attn_bwd": [("scatter", "ffn_w_up1", 0, 1), ("swap", "swa_w_o")],
    "swa_proj_dx": [("scatter", "swa_w_o", 0, 1)],
    "ffn_down_dx0": [("swap", "swa_w_in")],
    "ffn_down_dw0": [("scatter", "swa_w_in", 0, 1)],
    "ffn_gate_bwd0": [("swap", "ffn_w_down0")],
    "ffn_up_dx0": [("scatter", "ffn_w_down0", 0, 1)],
    "fox_out_dx": [("swap", "ffn_w_up0")],
    "fox_attn_bwd": [("scatter", "ffn_w_up0", 0, 2), ("swap", "fox_w_o")],
    "fox_proj_dw": [("scatter", "fox_w_o", 0, 1)],
    "fox_proj_dx": [("scatter", "ffn_w_up0", 1, 2), ("swap", "fox_w_in")],
}


class Exchanges:
    def __init__(self, dm, slots, chip, c):
        self.dm, self.slots, self.chip, self.c = dm, dict(slots), chip, c
        self.raw, self.part, self.landed, self.grads, self.views, self.pending = {}, {}, {}, {}, {}, {}

    def gather_now(self, keys, name):
        comm = _gather_comm([self.slots[k] for k in keys], [(0, 1)] * len(keys))
        _run_comm(comm, name)
        self.slots.update(zip(keys, comm.results))

    def w(self, key):
        if key not in self.views:
            S, D, FH, QH, KH, Fh = self.dm
            full = self.slots[key]
            if key == "fox_w_in":
                cols = full.shape[2]
                full = jnp.pad(full.transpose(1, 0, 2).reshape(D, N_CHIPS * cols), ((0, 0), (0, 3 * D + LANES - N_CHIPS * cols)))[None]
            elif key in ("fox_w_o", "swa_w_o"):
                full = full.reshape(1, D, D)
            elif key.startswith("ffn_w_down"):
                full = full.reshape(1, Fh, D)
            self.views[key] = full
        return self.views[key]

    def carry(self, stage):
        todo = []
        for kind, key, *chunk in PLAN.get(stage, ()):
            if kind == "gather":
                todo.append((kind, [key], _gather_comm([self.slots[key]], [tuple(chunk)])))
            elif kind == "swap":
                todo.append((kind, [key], _swap_comm([self.raw[key]])))
            elif kind == "scatter":
                todo.append((kind, [(key, *chunk)], _scatter_comm([self.part[key]], [self.landed.get(key)], [tuple(chunk)])))
        self.pending[stage] = todo
        return _merge([cm for _, _, cm in todo])

    def carried(self, stage, comm):
        for kind, keys, cm in self.pending.pop(stage):
            if kind == "gather":
                self.slots[keys[0]] = cm.results[0]
            elif kind == "swap":
                self.part[keys[0]] = _add_sibling(self.raw[keys[0]], cm.results[0], self.c, f"add_sibling_{keys[0]}")
            else:
                self.landed[keys[0][0]] = cm.results[0]

    def grad(self, key, g):
        S, D, FH, QH, KH, Fh = self.dm
        if key == "fox_w_in":
            cols = self.slots[key].shape[2]
            g = g[0][:, :N_CHIPS * cols].reshape(D, N_CHIPS, cols).transpose(1, 0, 2)
        elif key in ("fox_w_o", "swa_w_o"):
            g = g.reshape(N_CHIPS, D // N_CHIPS, D)
        elif key.startswith("ffn_w_down"):
            g = g.reshape(N_CHIPS, Fh // N_CHIPS, D)
        self.raw[key] = g

    def finish(self):
        last = "fox_w_in"
        keys = list(self.landed)
        halves = [_sum_chips(self.part[k], self.landed[k], self.chip, self.c, f"sum_chips_{k}") for k in keys]
        send, join = _scatter_comm([self.part[last]], [None], [(0, 1)]), _join_comm(halves)
        _run_comm(_merge([send, join]), "grads_tail")
        grads = dict(zip(keys, join.results))
        join = _join_comm([_sum_chips(self.part[last], send.results[0], self.chip, self.c, f"sum_chips_{last}")])
        _run_comm(join, "grads_join_last")
        grads[last] = join.results[0]
        return grads


def _step(dm, a):
    S, D, FH, QH, KH, Fh = dm
    ix, iy, ic = lax.axis_index("x"), lax.axis_index("y"), lax.axis_index("c")
    chip = 2 * ix + iy
    dev = 2 * chip + ic
    F2c = a["ffn_w_up"].shape[2]
    NC = a["ada_w"].shape[2]
    PW = 3 * D + LANES
    fox_cols = a["fox_w_in"].shape[2]

    e0 = _cond_rows(a["c"], a["ffn_conv_w"].reshape(DEPTH * 3, F2c), "silu_c")
    g0 = _allgather_small(e0, "gather_cond").reshape(N_DEV, 16, e0.shape[1])
    cact = g0[:, 0, :D]
    conv_w = g0[0::2, 8:8 + DEPTH * 3, :F2c].transpose(1, 0, 2).reshape(DEPTH, 3, N_CHIPS * F2c)
    rows = _ada_fwd(cact, a["ada_w"], a["ada_b"], ic, chip, "ada_proj")
    g1 = _allgather_small(rows, "gather_mod").reshape(N_CHIPS, DEPTH, 8, NC)
    mod = lax.dynamic_index_in_dim(g1, dev, axis=2, keepdims=False).transpose(1, 0, 2).reshape(DEPTH, N_CHIPS * NC)

    names = ["fox_w_in", "fox_w_o", "swa_w_in", "swa_w_o", "ffn_w_up", "ffn_w_up", "ffn_w_down", "ffn_w_down"]
    layers = [0, 0, 0, 0, 0, 1, 0, 1]
    keys = ["fox_w_in", "fox_w_o", "swa_w_in", "swa_w_o", "ffn_w_up0", "ffn_w_up1", "ffn_w_down0", "ffn_w_down1"]
    slots = {k: _cast_bf16(a[nm], l, chip, f"cast_{k}") for k, nm, l in zip(keys, names, layers)}
    pp = Exchanges(dm, slots, chip, ic)
    pp.gather_now(["fox_w_in", "fox_w_o"], "gather_fox")
    sp = {"fox_b_f": _pad_lanes(a["fox_b_f"]), "sinks": jnp.repeat(a["swa_sinks"].reshape(KH, QH // KH), WIN, axis=1)[:, :, None],
          "conv_w": [conv_w[i] for i in range(DEPTH)], "conv_b": [a["ffn_conv_b"][i:i + 1] for i in range(DEPTH)]}
    for nm in ("ln_mix_g", "ln_mix_b", "ln_ffn_g", "ln_ffn_b"):
        sp[nm] = [a[nm][i:i + 1] for i in range(DEPTH)]

    loss_cols, grad_x, gs, dmod = _local_step(dm, a["x"][0], a["loss_target"][0], a["positions"][0], mod, sp, pp)
    loss = lax.psum(0.5 / D * jnp.sum(loss_cols), ("x", "y", "c"))
    out = {"loss": loss, "grad_x": grad_x[None]}

    def run(fn, *args, name, **kw):
        comm = pp.carry(name)
        res = fn(*args, name=name, comm=comm, **kw)
        if comm is not None:
            pp.carried(name, comm)
        return res

    pieces = [dmod.reshape(-1), gs["fox_b_f"].reshape(-1), _pad_lanes(gs["sinks"]).reshape(-1),
              jnp.stack(gs["conv_w"]).reshape(-1), jnp.stack(gs["conv_b"]).reshape(-1)]
    pieces += [jnp.stack(gs[nm]).reshape(-1) for nm in ("ln_mix_g", "ln_mix_b", "ln_ffn_g", "ln_ffn_b")]
    sizes = [p.shape[0] for p in pieces]
    packed = _pad_rows(jnp.concatenate(pieces))
    allp = _allgather_small(packed, "gather_small").reshape(N_DEV, packed.shape[0], LANES)
    tot = _sum_devices(allp, "sum_small").reshape(-1)
    offs = [sum(sizes[:k]) for k in range(len(sizes))]
    take = lambda k: tot[offs[k]:offs[k] + sizes[k]]
    g_small = {"ada_b": take(0).reshape(DEPTH, -1), "fox_b_f": take(1)[:FH].reshape(1, FH), "swa_sinks": take(2)[:QH].reshape(1, QH),
               "ffn_conv_w": lax.dynamic_slice_in_dim(take(3).reshape(DEPTH, 3, N_CHIPS * F2c), chip * F2c, F2c, axis=2),
               "ffn_conv_b": take(4).reshape(DEPTH, -1)}
    for k, nm in enumerate(("ln_mix_g", "ln_mix_b", "ln_ffn_g", "ln_ffn_b")):
        g_small[nm] = take(5 + k).reshape(DEPTH, D)
    small = list(g_small)
    pack = lambda pre: _pad_rows(jnp.concatenate([(a[pre + nm] if pre else a[nm]).reshape(-1) for nm in small]))
    gp = _pad_rows(jnp.concatenate([g_small[nm].reshape(-1) for nm in small]))
    ds_, ms_, vs_ = _adamw_small(pack(""), gp, pack("m_"), pack("v_"), "adamw_small")
    off = 0
    for nm in small:
        n_el = a[nm].size
        out["grad_" + nm] = g_small[nm]
        for pre, arr in (("delta_", ds_), ("new_m_", ms_), ("new_v_", vs_)):
            out[pre + nm] = arr.reshape(-1)[off:off + n_el].reshape(a[nm].shape)
        off += n_el

    dmod_all = allp.reshape(N_DEV, -1)[:, :DEPTH * N_CHIPS * NC].reshape(N_DEV, DEPTH, N_CHIPS * NC)
    dmod_mine = lax.dynamic_slice_in_dim(dmod_all, chip * NC, NC, axis=2).transpose(1, 0, 2)
    ada = _ada_grad_adamw(cact.T, dmod_mine, a["ada_w"], a["m_ada_w"], a["v_ada_w"], "ada_grad")
    for pre, arr in zip(("grad_", "delta_", "new_m_", "new_v_"), ada):
        out[pre + "ada_w"] = arr

    grads = pp.finish()
    upd = {}
    for k, nm, l in zip(keys[1:], names[1:], layers[1:]):
        upd[nm] = _adamw(a[nm], grads[k], a["m_" + nm], a["v_" + nm], l, upd.get(nm, ()), f"adamw_{k}")
    tview = lambda t: jnp.swapaxes(t, 1, 2)
    res = _adamw(tview(a["fox_w_in"]), grads["fox_w_in"].T, tview(a["m_fox_w_in"]), tview(a["v_fox_w_in"]), 0, (), "adamw_fox_w_in", by_cols=True)
    upd["fox_w_in"] = [tview(r) for r in res]
    for nm, res in upd.items():
        for pre, arr in zip(("grad_", "delta_", "new_m_", "new_v_"), res):
            out[pre + nm] = arr
    return out


_WEIGHTS = ["fox_w_in", "fox_b_f", "fox_w_o", "swa_w_in", "swa_sinks", "swa_w_o", "ada_w", "ada_b", "ffn_w_up", "ffn_conv_w",
            "ffn_conv_b", "ffn_w_down", "ln_mix_g", "ln_mix_b", "ln_ffn_g", "ln_ffn_b"]
_INPUTS = (["x", "c", "positions"] + _WEIGHTS + ["loss_target"] + ["m_" + w for w in _WEIGHTS] + ["v_" + w for w in _WEIGHTS])


def kernel(x, c, positions, fox_w_in, fox_b_f, fox_w_o, swa_w_in, swa_sinks, swa_w_o, ada_w, ada_b, ffn_w_up, ffn_conv_w, ffn_conv_b, ffn_w_down, ln_mix_g, ln_mix_b, ln_ffn_g, ln_ffn_b, loss_target, m_fox_w_in, m_fox_b_f, m_fox_w_o, m_swa_w_in, m_swa_sinks, m_swa_w_o, m_ada_w, m_ada_b, m_ffn_w_up, m_ffn_conv_w, m_ffn_conv_b, m_ffn_w_down, m_ln_mix_g, m_ln_mix_b, m_ln_ffn_g, m_ln_ffn_b, v_fox_w_in, v_fox_b_f, v_fox_w_o, v_swa_w_in, v_swa_sinks, v_swa_w_o, v_ada_w, v_ada_b, v_ffn_w_up, v_ffn_conv_w, v_ffn_conv_b, v_ffn_w_down, v_ln_mix_g, v_ln_mix_b, v_ln_ffn_g, v_ln_ffn_b):
    args = (x, c, positions, fox_w_in, fox_b_f, fox_w_o, swa_w_in, swa_sinks, swa_w_o, ada_w, ada_b, ffn_w_up, ffn_conv_w, ffn_conv_b, ffn_w_down, ln_mix_g, ln_mix_b, ln_ffn_g, ln_ffn_b, loss_target, m_fox_w_in, m_fox_b_f, m_fox_w_o, m_swa_w_in, m_swa_sinks, m_swa_w_o, m_ada_w, m_ada_b, m_ffn_w_up, m_ffn_conv_w, m_ffn_conv_b, m_ffn_w_down, m_ln_mix_g, m_ln_mix_b, m_ln_ffn_g, m_ln_ffn_b, v_fox_w_in, v_fox_b_f, v_fox_w_o, v_swa_w_in, v_swa_sinks, v_swa_w_o, v_ada_w, v_ada_b, v_ffn_w_up, v_ffn_conv_w, v_ffn_conv_b, v_ffn_w_down, v_ln_mix_g, v_ln_mix_b, v_ln_ffn_g, v_ln_ffn_b)
    out = _step(PROD, dict(zip(_INPUTS, args)))
    order = ["loss", "grad_x"] + [p + w for p in ("grad_", "delta_", "new_m_", "new_v_") for w in _WEIGHTS]
    return tuple(out[k] for k in order)
```

```python
import functools
from typing import NamedTuple

import jax
import jax.numpy as jnp
from jax import lax
from jax.experimental import pallas as pl
from jax.experimental.pallas import tpu as pltpu

F32 = jnp.float32
BF16 = jnp.bfloat16
MESH = pl.DeviceIdType.MESH
HIGHEST = lax.Precision.HIGHEST

N_CHIPS = 4
N_DEV = 8
LANES = 128
VMEM_LIMIT = 56 * 1024 * 1024

DEPTH = 2
DEEPNORM_ALPHA = (2.0 * DEPTH) ** 0.25
LN_EPS = 1e-5
ROPE_THETA = 500000.0
ADAM_LR, ADAM_B1, ADAM_B2, ADAM_EPS, ADAM_WD, ADAM_STEP = 0.001, 0.9, 0.999, 1e-08, 0.01, 10
NEG = -1e30


class Dims(NamedTuple):
    S: int
    D: int
    FH: int
    QH: int
    KH: int
    F: int


PROD = Dims(S=2048, D=2048, FH=16, QH=32, KH=4, F=5632)
FDH = 128
SDH = 64
WIN = 128
ROPE_DIM = 16
FOX_TQ = 256


def _params(sem=None, vmem=VMEM_LIMIT):
    return pltpu.CompilerParams(dimension_semantics=sem, vmem_limit_bytes=vmem)


def _tile(n, pref, unit=LANES):
    if n <= pref:
        return n
    t = (pref // unit) * unit
    while t > 0:
        if n % t == 0:
            return t
        t -= unit
    return n


class Comm:
    def __init__(self, args, out_shapes, aliases, n_sem, start, finish, members=()):
        self.args, self.out_shapes, self.aliases, self.n_sem = list(args), list(out_shapes), dict(aliases), n_sem
        self.start, self.finish = start, finish
        self.members = members
        self.results = None

    def set_results(self, res):
        self.results = list(res)
        for cm, o0 in self.members:
            cm.set_results(self.results[o0:o0 + len(cm.out_shapes)])


class _SemView:
    def __init__(self, sems, first):
        self.sems, self.first = sems, first

    @property
    def at(self):
        return self

    def __getitem__(self, k):
        return self.sems.at[self.first + k]


def _merge(comms):
    comms = [cm for cm in comms if cm is not None]
    if len(comms) < 2:
        return comms[0] if comms else None
    args, shapes, aliases, spans, n_sem = [], [], {}, [], 0
    for cm in comms:
        spans.append((len(args), len(shapes), n_sem))
        aliases.update({len(args) + a: len(shapes) + o for a, o in cm.aliases.items()})
        args += cm.args
        shapes += cm.out_shapes
        n_sem += cm.n_sem

    def each(step):
        def run(ar, ou, send, recv):
            for cm, (a0, o0, s0) in zip(comms, spans):
                getattr(cm, step)(ar[a0:a0 + len(cm.args)], ou[o0:o0 + len(cm.out_shapes)], _SemView(send, s0), _SemView(recv, s0))
        return run

    return Comm(args, shapes, aliases, n_sem, each("start"), each("finish"), [(cm, o0) for cm, (_, o0, _) in zip(comms, spans)])


def _place():
    x, y, c = lax.axis_index("x"), lax.axis_index("y"), lax.axis_index("c")
    chips = [(1 - x, y), (x, 1 - y), (1 - x, 1 - y)]
    return x, y, c, chips


def _remote(src, dst, send, recv, to):
    return pltpu.make_async_remote_copy(src_ref=src, dst_ref=dst, send_sem=send, recv_sem=recv, device_id=to, device_id_type=MESH)


def _any_specs(n):
    return [pl.BlockSpec(memory_space=pl.ANY)] * n


def _call(body, *, name, grid, in_specs, out_specs, out_shape, args, sem, scratch_shapes=(), aliases=None, comm=None):
    in_specs, out_specs, out_shape, scratch_shapes = list(in_specs), list(out_specs), list(out_shape), list(scratch_shapes)
    aliases = dict(aliases or {})
    if comm is None:
        return pl.pallas_call(body, name=name, grid=grid, in_specs=in_specs, out_specs=out_specs, out_shape=out_shape,
                              scratch_shapes=scratch_shapes, input_output_aliases=aliases, compiler_params=_params(sem))(*args)
    n_in, n_out, nc_in, nc_out, n_scr = len(in_specs), len(out_specs), len(comm.args), len(comm.out_shapes), len(scratch_shapes)

    def wrapped(*refs):
        ins, refs = refs[:n_in], refs[n_in:]
        cin, refs = refs[:nc_in], refs[nc_in:]
        outs, refs = refs[:n_out], refs[n_out:]
        cout, refs = refs[:nc_out], refs[nc_out:]
        scratch, (send, recv) = refs[:n_scr], refs[n_scr:]
        ids = [pl.program_id(k) for k in range(len(grid))]
        first = functools.reduce(jnp.logical_and, [i == 0 for i in ids])
        last = functools.reduce(jnp.logical_and, [i == g - 1 for i, g in zip(ids, grid)])

        @pl.when(first)
        def _():
            comm.start(cin, cout, send, recv)

        body(*ins, *outs, *scratch)

        @pl.when(last)
        def _():
            comm.finish(cin, cout, send, recv)

    res = pl.pallas_call(
        wrapped, name=name, grid=grid, in_specs=in_specs + _any_specs(nc_in), out_specs=out_specs + _any_specs(nc_out),
        out_shape=out_shape + comm.out_shapes,
        scratch_shapes=scratch_shapes + [pltpu.SemaphoreType.DMA((comm.n_sem,)), pltpu.SemaphoreType.DMA((comm.n_sem,))],
        input_output_aliases={**aliases, **{n_in + a: n_out + o for a, o in comm.aliases.items()}},
        compiler_params=_params(("arbitrary",) * len(grid)),
    )(*args, *comm.args)
    comm.set_results(res[n_out:])
    return list(res[:n_out])


def _run_comm(comm, name):
    nc_in, nc_out = len(comm.args), len(comm.out_shapes)

    def body(*refs):
        cin, cout, (send, recv) = refs[:nc_in], refs[nc_in:nc_in + nc_out], refs[nc_in + nc_out:]
        comm.start(cin, cout, send, recv)
        comm.finish(cin, cout, send, recv)

    res = pl.pallas_call(
        body, name=name, in_specs=_any_specs(nc_in), out_specs=_any_specs(nc_out), out_shape=comm.out_shapes,
        scratch_shapes=[pltpu.SemaphoreType.DMA((comm.n_sem,)), pltpu.SemaphoreType.DMA((comm.n_sem,))],
        input_output_aliases=comm.aliases,
    )(*comm.args)
    comm.set_results(res)


def _gather_comm(slots, chunks):
    n = len(slots)

    def rows(t, who):
        rh = slots[t].shape[1] // 2
        lo, hi, nch = chunks[t]
        rc = rh // nch
        return pl.ds(who * rh + lo * rc, (hi - lo) * rc)

    def start(args, outs, send, recv):
        x, y, c, chips = _place()
        s = 2 * x + y
        for t in range(n):
            mine = outs[t].at[s, rows(t, c)]
            for j, chip in enumerate(chips):
                _remote(mine, mine, send.at[6 * t + j], recv.at[6 * t + j], (*chip, c)).start()

    def finish(args, outs, send, recv):
        x, y, c, chips = _place()
        s = 2 * x + y
        sib = (x, y, 1 - c)
        for t in range(n):
            for j, chip in enumerate(chips):
                blk = outs[t].at[2 * chip[0] + chip[1], rows(t, c)]
                _remote(blk, blk, send.at[6 * t + j], recv.at[6 * t + j], (*chip, c)).wait_recv()
                _remote(blk, blk, send.at[6 * t + 3 + j], recv.at[6 * t + 3 + j], sib).start()
        for t in range(n):
            for j, chip in enumerate(chips):
                blk = outs[t].at[2 * chip[0] + chip[1], rows(t, 1 - c)]
                _remote(blk, blk, send.at[6 * t + 3 + j], recv.at[6 * t + 3 + j], sib).wait_recv()
        for t in range(n):
            mine = outs[t].at[s, rows(t, c)]
            for j, chip in enumerate(chips):
                _remote(mine, mine, send.at[6 * t + j], recv.at[6 * t + j], (*chip, c)).wait_send()
                blk = outs[t].at[2 * chip[0] + chip[1], rows(t, c)]
                _remote(blk, blk, send.at[6 * t + 3 + j], recv.at[6 * t + 3 + j], sib).wait_send()

    shapes = [jax.ShapeDtypeStruct(w.shape, w.dtype) for w in slots]
    return Comm(slots, shapes, {t: t for t in range(n)}, 6 * n, start, finish)


def _scatter_comm(parts, landed, chunks):
    n = len(parts)
    prev = [t for t in range(n) if landed[t] is not None]

    def rows(t):
        lo, hi, nch = chunks[t]
        rc = parts[t].shape[1] // nch
        return pl.ds(lo * rc, (hi - lo) * rc)

    def start(args, outs, send, recv):
        x, y, c, chips = _place()
        s = 2 * x + y
        for t in range(n):
            for j, chip in enumerate(chips):
                _remote(args[t].at[2 * chip[0] + chip[1], rows(t)], outs[t].at[s, rows(t)],
                        send.at[3 * t + j], recv.at[3 * t + j], (*chip, c)).start()

    def finish(args, outs, send, recv):
        x, y, c, chips = _place()
        for t in range(n):
            for j, chip in enumerate(chips):
                blk = outs[t].at[2 * chip[0] + chip[1], rows(t)]
                _remote(blk, blk, send.at[3 * t + j], recv.at[3 * t + j], (*chip, c)).wait_recv()
        for t in range(n):
            for j, chip in enumerate(chips):
                src = args[t].at[2 * chip[0] + chip[1], rows(t)]
                _remote(src, src, send.at[3 * t + j], recv.at[3 * t + j], (*chip, c)).wait_send()

    shapes = [jax.ShapeDtypeStruct(p.shape, p.dtype) for p in parts]
    return Comm(list(parts) + [landed[t] for t in prev], shapes, {n + i: t for i, t in enumerate(prev)}, 3 * n, start, finish)


def _swap_comm(gs):
    n = len(gs)

    def copy(args, outs, send, recv, t):
        _, _, c, _ = _place()
        rh = gs[t].shape[1] // 2
        x, y = lax.axis_index("x"), lax.axis_index("y")
        return _remote(args[t].at[:, pl.ds((1 - c) * rh, rh), :], outs[t], send.at[t], recv.at[t], (x, y, 1 - c))

    def start(args, outs, send, recv):
        for t in range(n):
            copy(args, outs, send, recv, t).start()

    def finish(args, outs, send, recv):
        for t in range(n):
            copy(args, outs, send, recv, t).wait()

    shapes = [jax.ShapeDtypeStruct((g.shape[0], g.shape[1] // 2, g.shape[2]), g.dtype) for g in gs]
    return Comm(gs, shapes, {}, n, start, finish)


def _join_comm(gs):
    n = len(gs)

    def half(outs, t, who):
        rh = gs[t].shape[0] // 2
        return outs[t].at[pl.ds(who * rh, rh), :]

    def start(args, outs, send, recv):
        x, y, c, _ = _place()
        for t in range(n):
            _remote(half(outs, t, c), half(outs, t, c), send.at[t], recv.at[t], (x, y, 1 - c)).start()

    def finish(args, outs, send, recv):
        x, y, c, _ = _place()
        for t in range(n):
            _remote(half(outs, t, 1 - c), half(outs, t, 1 - c), send.at[t], recv.at[t], (x, y, 1 - c)).wait_recv()
        for t in range(n):
            _remote(half(outs, t, c), half(outs, t, c), send.at[t], recv.at[t], (x, y, 1 - c)).wait_send()

    shapes = [jax.ShapeDtypeStruct(g.shape, g.dtype) for g in gs]
    return Comm(gs, shapes, {t: t for t in range(n)}, n, start, finish)


_DN = {"nn": (((1,), (0,)), ((), ())), "nt": (((1,), (1,)), ((), ())), "tn": (((0,), (0,)), ((), ()))}


def _mm(a, b, *, mode, out_dtype, name, out_groups=1, tm=1024, tn=1024, tk=2048, comm=None):
    ga, ra, ca = a.shape
    gb, rb, cb = b.shape
    if mode == "nn":
        M, K, N = ra, ga * ca, gb * cb
        assert rb == K and ga == 1 or (rb == K)
    elif mode == "nt":
        M, K, N = ra, ga * ca, rb
        assert gb * cb == K
    else:
        K, M, N = ra, ga * ca, gb * cb
        assert rb == K
    go = out_groups
    if mode == "nn":
        tk = _tile(ca, tk); assert rb % tk == 0 and (ga == 1 or True)
        tn = _tile(min(cb, N // go), tn); tm = _tile(M, tm, 8)
    elif mode == "nt":
        tk = _tile(ca, tk); tk = _tile(cb, tk) if cb % tk else tk; assert ca % tk == 0 and cb % tk == 0
        tn = _tile(N // go, tn); tm = _tile(M, tm, 8)
    else:
        tk = _tile(K, tk, 8); tm = _tile(ca, tm); tn = _tile(min(cb, N // go), tn)
    assert (N // go) % tn == 0 and M % tm == 0 and K % tk == 0, (name, M, N, K, tm, tn, tk)
    nk = K // tk
    kpa = max(ca // tk, 1)
    kpb = max(cb // tk, 1)
    npb = max(cb // tn, 1)
    npo = (N // go) // tn
    mpa = max(ca // tm, 1)

    if mode == "nn":
        a_spec = pl.BlockSpec((1, tm, tk), lambda j, i, k: (k // kpa, i, k % kpa))
        b_spec = pl.BlockSpec((1, tk, tn), lambda j, i, k: (j // npb, k, j % npb))
    elif mode == "nt":
        a_spec = pl.BlockSpec((1, tm, tk), lambda j, i, k: (k // kpa, i, k % kpa))
        b_spec = pl.BlockSpec((1, tn, tk), lambda j, i, k: (k // kpb, j, k % kpb))
    else:
        a_spec = pl.BlockSpec((1, tk, tm), lambda j, i, k: (i // mpa, k, i % mpa))
        b_spec = pl.BlockSpec((1, tk, tn), lambda j, i, k: (j // npb, k, j % npb))
    o_spec = pl.BlockSpec((1, tm, tn), lambda j, i, k: (j // npo, i, j % npo))
    dn = _DN[mode]

    def body(a_ref, b_ref, o_ref, *acc):
        p = lax.dot_general(a_ref[0], b_ref[0], dn, preferred_element_type=F32)
        if nk == 1:
            o_ref[0] = p.astype(out_dtype)
        else:
            k = pl.program_id(2)

            @pl.when(k == 0)
            def _():
                acc[0][...] = p

            @pl.when(k > 0)
            def _():
                acc[0][...] += p

            @pl.when(k == nk - 1)
            def _():
                o_ref[0] = acc[0][...].astype(out_dtype)

    return _call(
        body, name=name, grid=(N // tn, M // tm, nk), in_specs=[a_spec, b_spec], out_specs=[o_spec],
        out_shape=[jax.ShapeDtypeStruct((go, M, N // go), out_dtype)],
        scratch_shapes=[pltpu.VMEM((tm, tn), F32)] if nk > 1 else [],
        sem=("parallel", "parallel", "arbitrary"), args=(a, b), comm=comm)[0]


def _rows(tr, d):
    return pl.BlockSpec((tr, d), lambda i: (i, 0))


def _vec(d):
    return pl.BlockSpec((1, d), lambda i: (0, 0))


def _modulate(x, sc, sh, name):
    S, D = x.shape
    tr = min(256, S)

    def body(x_ref, sc_ref, sh_ref, h_ref):
        h_ref[...] = (x_ref[...] * (1.0 + sc_ref[...]) + sh_ref[...]).astype(BF16)

    return pl.pallas_call(
        body, name=name, grid=(S // tr,), in_specs=[_rows(tr, D), _vec(D), _vec(D)], out_specs=_rows(tr, D),
        out_shape=jax.ShapeDtypeStruct((S, D), BF16), compiler_params=_params(("parallel",)),
    )(x, sc, sh)


def _ln_fwd(x, y, gate, gamma, beta, sc, sh, name, comm=None):
    S, D = x.shape
    tr = min(256, S)
    emit_h = sc is not None

    def body(*refs):
        if emit_h:
            x_ref, y_ref, g_ref, ga_ref, be_ref, sc_ref, sh_ref, xo_ref, xh_ref, rs_ref, h_ref = refs
        else:
            x_ref, y_ref, g_ref, ga_ref, be_ref, xo_ref, xh_ref, rs_ref = refs
        z = DEEPNORM_ALPHA * x_ref[...] + (1.0 + g_ref[...]) * y_ref[...]
        mu = jnp.mean(z, axis=-1, keepdims=True)
        zc = z - mu
        var = jnp.mean(zc * zc, axis=-1, keepdims=True)
        rstd = lax.rsqrt(var + LN_EPS)
        xh = zc * rstd
        xo = xh * ga_ref[...] + be_ref[...]
        xo_ref[...] = xo
        xh_ref[...] = xh
        rs_ref[...] = rstd
        if emit_h:
            h_ref[...] = (xo * (1.0 + sc_ref[...]) + sh_ref[...]).astype(BF16)

    ins = [x, y, gate, gamma, beta] + ([sc, sh] if emit_h else [])
    in_specs = [_rows(tr, D), _rows(tr, D)] + [_vec(D)] * (len(ins) - 2)
    out_shape = [jax.ShapeDtypeStruct((S, D), F32), jax.ShapeDtypeStruct((S, D), F32), jax.ShapeDtypeStruct((S, 1), F32)]
    out_specs = [_rows(tr, D), _rows(tr, D), _rows(tr, 1)]
    if emit_h:
        out_shape.append(jax.ShapeDtypeStruct((S, D), BF16))
        out_specs.append(_rows(tr, D))
    return _call(body, name=name, grid=(S // tr,), in_specs=in_specs, out_specs=out_specs, out_shape=out_shape,
                 sem=("parallel",), args=ins, comm=comm)


def _loss_head(xf, tgt, name):
    S, D = xf.shape
    tr = min(256, S)

    def body(x_ref, t_ref, dx_ref, l_ref):
        e = x_ref[...] - t_ref[...]
        dx_ref[...] = e * (1.0 / D)

        @pl.when(pl.program_id(0) == 0)
        def _():
            l_ref[...] = jnp.zeros_like(l_ref)

        l_ref[...] += jnp.sum(e * e, axis=0, keepdims=True)

    return pl.pallas_call(
        body, name=name, grid=(S // tr,), in_specs=[_rows(tr, D), _rows(tr, D)],
        out_specs=[_rows(tr, D), _vec(D)],
        out_shape=[jax.ShapeDtypeStruct((S, D), F32), jax.ShapeDtypeStruct((1, D), F32)],
        compiler_params=_params(("arbitrary",)),
    )(xf, tgt)


def _ln_bwd(dxo, xh, rstd, gamma, y, gate, name):
    S, D = dxo.shape
    tr = min(256, S)

    def body(dx_ref, xh_ref, rs_ref, ga_ref, y_ref, g_ref, dres_ref, dy_ref, dga_ref, dbe_ref, dg_ref):
        dxo_ = dx_ref[...]
        xh_ = xh_ref[...]
        dxh = dxo_ * ga_ref[...]
        m1 = jnp.mean(dxh, axis=-1, keepdims=True)
        m2 = jnp.mean(dxh * xh_, axis=-1, keepdims=True)
        dz = rs_ref[...] * (dxh - m1 - xh_ * m2)
        dres_ref[...] = DEEPNORM_ALPHA * dz
        dy_ref[...] = ((1.0 + g_ref[...]) * dz).astype(BF16)

        @pl.when(pl.program_id(0) == 0)
        def _():
            dga_ref[...] = jnp.zeros_like(dga_ref)
            dbe_ref[...] = jnp.zeros_like(dbe_ref)
            dg_ref[...] = jnp.zeros_like(dg_ref)

        dga_ref[...] += jnp.sum(dxo_ * xh_, axis=0, keepdims=True)
        dbe_ref[...] += jnp.sum(dxo_, axis=0, keepdims=True)
        dg_ref[...] += jnp.sum(dz * y_ref[...], axis=0, keepdims=True)

    return pl.pallas_call(
        body, name=name, grid=(S // tr,),
        in_specs=[_rows(tr, D), _rows(tr, D), _rows(tr, 1), _vec(D), _rows(tr, D), _vec(D)],
        out_specs=[_rows(tr, D), _rows(tr, D), _vec(D), _vec(D), _vec(D)],
        out_shape=[jax.ShapeDtypeStruct((S, D), F32), jax.ShapeDtypeStruct((S, D), BF16)] + [jax.ShapeDtypeStruct((1, D), F32)] * 3,
        compiler_params=_params(("arbitrary",)),
    )(dxo, xh, rstd, gamma, y, gate)


def _mod_bwd(dh, x, sc, dres, name):
    S, D = x.shape
    tr = min(256, S)

    def body(dh_ref, x_ref, sc_ref, dr_ref, dx_ref, dsc_ref, dsh_ref):
        dh_ = dh_ref[...]
        dx_ref[...] = dr_ref[...] + dh_ * (1.0 + sc_ref[...])

        @pl.when(pl.program_id(0) == 0)
        def _():
            dsc_ref[...] = jnp.zeros_like(dsc_ref)
            dsh_ref[...] = jnp.zeros_like(dsh_ref)

        dsc_ref[...] += jnp.sum(dh_ * x_ref[...], axis=0, keepdims=True)
        dsh_ref[...] += jnp.sum(dh_, axis=0, keepdims=True)

    return pl.pallas_call(
        body, name=name, grid=(S // tr,),
        in_specs=[_rows(tr, D), _rows(tr, D), _vec(D), _rows(tr, D)],
        out_specs=[_rows(tr, D), _vec(D), _vec(D)],
        out_shape=[jax.ShapeDtypeStruct((S, D), F32), jax.ShapeDtypeStruct((1, D), F32), jax.ShapeDtypeStruct((1, D), F32)],
        compiler_params=_params(("arbitrary",)),
    )(dh, x, sc, dres)


def _log_sigmoid(z):
    return jnp.minimum(z, 0.0) - jnp.log(1.0 + jnp.exp(-jnp.abs(z)))


def _fox_gate_fwd(proj, b_f, n_heads, name):
    S, PW = proj.shape
    blk = min(256, S)
    last = PW // LANES - 1

    def body(fl_ref, b_ref, cum_ref):
        r = lax.broadcasted_iota(jnp.int32, (blk, blk), 0)
        c = lax.broadcasted_iota(jnp.int32, (blk, blk), 1)
        tril = (c <= r).astype(F32)
        carry = jnp.zeros((1, LANES), F32)
        for i in range(S // blk):
            lf = _log_sigmoid(fl_ref[i * blk:(i + 1) * blk, :] + b_ref[...])
            cum_ref[i * blk:(i + 1) * blk, :] = jnp.dot(tril, lf, preferred_element_type=F32, precision=HIGHEST) + carry
            carry = carry + jnp.sum(lf, axis=0, keepdims=True)

    return pl.pallas_call(
        body, name=name, grid=(1,),
        in_specs=[pl.BlockSpec((S, LANES), lambda i: (0, last)), pl.BlockSpec((1, LANES), lambda i: (0, 0))],
        out_specs=pl.BlockSpec((S, LANES), lambda i: (0, 0)),
        out_shape=jax.ShapeDtypeStruct((S, LANES), F32), compiler_params=_params(("arbitrary",)),
    )(proj, b_f)


def _fox_gate_bwd(dcum, proj, b_f, n_heads, name):
    S, PW = proj.shape
    blk = min(256, S)
    last = PW // LANES - 1
    nb = S // blk

    def body(dc_ref, fl_ref, b_ref, dfl_ref, db_ref):
        r = lax.broadcasted_iota(jnp.int32, (blk, blk), 0)
        c = lax.broadcasted_iota(jnp.int32, (blk, blk), 1)
        triu = (c >= r).astype(F32)
        lane = lax.broadcasted_iota(jnp.int32, (blk, LANES), 1)
        carry = jnp.zeros((1, LANES), F32)
        dbs = jnp.zeros((1, LANES), F32)
        for i in reversed(range(nb)):
            dc = dc_ref[i * blk:(i + 1) * blk, :]
            dlf = jnp.dot(triu, dc, preferred_element_type=F32, precision=HIGHEST) + carry
            carry = carry + jnp.sum(dc, axis=0, keepdims=True)
            z = fl_ref[i * blk:(i + 1) * blk, :] + b_ref[...]
            e = jnp.exp(-jnp.abs(z))
            sig_neg = jnp.where(z >= 0, e / (1.0 + e), 1.0 / (1.0 + e))
            dfl = jnp.where(lane < n_heads, dlf * sig_neg, 0.0)
            dfl_ref[i * blk:(i + 1) * blk, :] = dfl.astype(BF16)
            dbs = dbs + jnp.sum(dfl, axis=0, keepdims=True)
        db_ref[...] = dbs

    return pl.pallas_call(
        body, name=name, grid=(1,),
        in_specs=[pl.BlockSpec((S, LANES), lambda i: (0, 0)), pl.BlockSpec((S, LANES), lambda i: (0, last)),
                  pl.BlockSpec((1, LANES), lambda i: (0, 0))],
        out_specs=[pl.BlockSpec((S, LANES), lambda i: (0, 0)), pl.BlockSpec((1, LANES), lambda i: (0, 0))],
        out_shape=[jax.ShapeDtypeStruct((S, LANES), BF16), jax.ShapeDtypeStruct((1, LANES), F32)],
        compiler_params=_params(("arbitrary",)),
    )(dcum, proj, b_f)


def _fox_scores(q_ref, kb_ref, cq_ref, ck_ref, qi, tq, scale):
    kk = (qi + 1) * tq
    rows = slice(qi * tq, (qi + 1) * tq)
    qb = q_ref[rows, :].astype(BF16)
    s = lax.dot_general(qb, kb_ref[0:kk, :], _DN["nt"], preferred_element_type=F32) * scale
    s = s + (cq_ref[0, rows, :] - ck_ref[0, :, 0:kk])
    r = lax.broadcasted_iota(jnp.int32, (tq, kk), 0) + qi * tq
    c = lax.broadcasted_iota(jnp.int32, (tq, kk), 1)
    mask = c <= r
    return jnp.where(mask, s, NEG), mask, qb


def _fox_fwd(proj, cq, ck, n_heads, name, comm=None):
    S = proj.shape[0]
    H = n_heads
    tq = min(FOX_TQ, S)
    nq = S // tq
    scale = FDH ** -0.5

    def body(q_ref, k_ref, v_ref, cq_ref, ck_ref, o_ref, lse_ref, kb_ref, vb_ref):
        kb_ref[...] = k_ref[...].astype(BF16)
        vb_ref[...] = v_ref[...].astype(BF16)
        for qi in range(nq):
            kk = (qi + 1) * tq
            rows = slice(qi * tq, (qi + 1) * tq)
            s, _, _ = _fox_scores(q_ref, kb_ref, cq_ref, ck_ref, qi, tq, scale)
            m = jnp.max(s, axis=-1, keepdims=True)
            p = jnp.exp(s - m)
            l = jnp.sum(p, axis=-1, keepdims=True)
            p = p * (1.0 / l)
            o_ref[rows, :] = jnp.dot(p.astype(BF16), vb_ref[0:kk, :], preferred_element_type=F32).astype(BF16)
            lse_ref[0, rows, :] = m + jnp.log(l)

    col = lambda off: pl.BlockSpec((S, FDH), lambda h: (0, h + off))
    stat_c = pl.BlockSpec((1, S, 1), lambda h: (h, 0, 0))
    stat_r = pl.BlockSpec((1, 1, S), lambda h: (h, 0, 0))
    return _call(
        body, name=name, grid=(H,),
        in_specs=[col(0), col(H), col(2 * H), stat_c, stat_r],
        out_specs=[col(0), stat_c],
        out_shape=[jax.ShapeDtypeStruct((S, H * FDH), BF16), jax.ShapeDtypeStruct((H, S, 1), F32)],
        scratch_shapes=[pltpu.VMEM((S, FDH), BF16), pltpu.VMEM((S, FDH), BF16)],
        sem=("parallel",), args=(proj, proj, proj, cq, ck), comm=comm)


def _fox_bwd(proj, cq, ck, lse, do, n_heads, name, comm=None):
    S = proj.shape[0]
    H = n_heads
    tq = min(FOX_TQ, S)
    nq = S // tq
    scale = FDH ** -0.5

    def body(q_ref, k_ref, v_ref, cq_ref, ck_ref, lse_ref, do_ref, dq_ref, dk_ref, dv_ref, dcq_ref, dck_ref,
             kb_ref, vb_ref, dka_ref, dva_ref):
        kb_ref[...] = k_ref[...].astype(BF16)
        vb_ref[...] = v_ref[...].astype(BF16)
        dka_ref[...] = jnp.zeros_like(dka_ref)
        dva_ref[...] = jnp.zeros_like(dva_ref)
        dck_ref[...] = jnp.zeros_like(dck_ref)
        for qi in range(nq):
            kk = (qi + 1) * tq
            rows = slice(qi * tq, (qi + 1) * tq)
            s, mask, qb = _fox_scores(q_ref, kb_ref, cq_ref, ck_ref, qi, tq, scale)
            p = jnp.where(mask, jnp.exp(s - lse_ref[0, rows, :]), 0.0)
            dob = do_ref[rows, :]
            dp = lax.dot_general(dob, vb_ref[0:kk, :], _DN["nt"], preferred_element_type=F32)
            delta = jnp.sum(p * dp, axis=-1, keepdims=True)
            ds = p * (dp - delta)
            dcq_ref[0, rows, :] = jnp.sum(ds, axis=-1, keepdims=True)
            dck_ref[0, :, 0:kk] -= jnp.sum(ds, axis=0, keepdims=True)
            dsb = (ds * scale).astype(BF16)
            dq_ref[rows, :] = jnp.dot(dsb, kb_ref[0:kk, :], preferred_element_type=F32).astype(BF16)
            dka_ref[0:kk, :] += lax.dot_general(dsb, qb, _DN["tn"], preferred_element_type=F32)
            dva_ref[0:kk, :] += lax.dot_general(p.astype(BF16), dob, _DN["tn"], preferred_element_type=F32)
        dk_ref[...] = dka_ref[...].astype(BF16)
        dv_ref[...] = dva_ref[...].astype(BF16)

    col = lambda off: pl.BlockSpec((S, FDH), lambda h: (0, h + off))
    stat_c = pl.BlockSpec((1, S, 1), lambda h: (h, 0, 0))
    stat_r = pl.BlockSpec((1, 1, S), lambda h: (h, 0, 0))
    wide = jax.ShapeDtypeStruct((S, H * FDH), BF16)
    return _call(
        body, name=name, grid=(H,),
        in_specs=[col(0), col(H), col(2 * H), stat_c, stat_r, stat_c, col(0)],
        out_specs=[col(0), col(0), col(0), stat_c, stat_r],
        out_shape=[wide, wide, wide, jax.ShapeDtypeStruct((H, S, 1), F32), jax.ShapeDtypeStruct((H, 1, S), F32)],
        scratch_shapes=[pltpu.VMEM((S, FDH), BF16), pltpu.VMEM((S, FDH), BF16), pltpu.VMEM((S, FDH), F32), pltpu.VMEM((S, FDH), F32)],
        sem=("parallel",), args=(proj, proj, proj, cq, ck, lse, do), comm=comm)


def _rope_tables(pos, sign):
    inv = ROPE_THETA ** (-jnp.arange(0, ROPE_DIM, 2, dtype=F32) / ROPE_DIM)
    ang = pos.astype(F32)[:, None] * inv
    cos, sin = jnp.cos(ang), sign * jnp.sin(ang)
    l64 = jnp.arange(LANES) % SDH
    idx = l64 % (ROPE_DIM // 2)
    c = jnp.where(l64 < ROPE_DIM, cos[:, idx], 1.0)
    sa = jnp.where(l64 < ROPE_DIM // 2, -sin[:, idx], 0.0)
    sb = jnp.where((l64 >= ROPE_DIM // 2) & (l64 < ROPE_DIM), sin[:, idx], 0.0)
    rot = jnp.stack([c, sa, sb])
    ident = jnp.stack([jnp.ones_like(c), jnp.zeros_like(c), jnp.zeros_like(c)])
    return jnp.stack([rot, ident]).astype(F32)


def _rope(xin, tabs, n_rot, out_dtype, name, comm=None):
    S, W = xin.shape

    def body(x_ref, t_ref, o_ref):
        xv = x_ref[...]
        o = xv * t_ref[0, 0] + pltpu.roll(xv, LANES - ROPE_DIM // 2, 1) * t_ref[0, 1] + pltpu.roll(xv, ROPE_DIM // 2, 1) * t_ref[0, 2]
        o_ref[...] = o.astype(out_dtype)

    return _call(
        body, name=name, grid=(W // LANES,),
        in_specs=[pl.BlockSpec((S, LANES), lambda j: (0, j)),
                  pl.BlockSpec((1, 3, S, LANES), lambda j: (jnp.where(j < n_rot, 0, 1), 0, 0, 0))],
        out_specs=[pl.BlockSpec((S, LANES), lambda j: (0, j))],
        out_shape=[jax.ShapeDtypeStruct((S, W), out_dtype)], sem=("parallel",), args=(xin, tabs), comm=comm)[0]


def _swa_probs(q_ref, k_ref, sk_ref, n, scale):
    st = pl.multiple_of(jnp.maximum(n - 1, 0) * WIN, WIN)
    qb = q_ref[0, 0]
    kb = k_ref[0, pl.ds(st, 2 * WIN), :]
    gm = qb.shape[0]
    s = lax.dot_general(qb, kb, _DN["nt"], preferred_element_type=F32) * scale
    qa = n * WIN + (lax.broadcasted_iota(jnp.int32, (gm, 2 * WIN), 0) & (WIN - 1))
    ka = st + lax.broadcasted_iota(jnp.int32, (gm, 2 * WIN), 1)
    valid = (ka <= qa) & (qa - ka < WIN)
    s = jnp.where(valid, s, NEG)
    sink = sk_ref[0]
    m = jnp.maximum(jnp.max(s, axis=-1, keepdims=True), sink)
    e = jnp.where(valid, jnp.exp(s - m), 0.0)
    es = jnp.exp(sink - m)
    inv = 1.0 / (jnp.sum(e, axis=-1, keepdims=True) + es)
    return e * inv, es * inv, st, qb, kb


def _swa_specs(S, gm):
    blk = pl.BlockSpec((1, 1, gm, SDH), lambda g, n: (g, n, 0, 0))
    kv = pl.BlockSpec((1, S, SDH), lambda g, n: (g, 0, 0))
    col = pl.BlockSpec((1, gm, 1), lambda g, n: (g, 0, 0))
    return blk, kv, col


def _swa_fwd(q, k, v, sinks, name, comm=None):
    KH, nb, gm, _ = q.shape
    S = k.shape[1]
    scale = SDH ** -0.5

    def body(q_ref, k_ref, v_ref, sk_ref, o_ref):
        p, _, st, _, _ = _swa_probs(q_ref, k_ref, sk_ref, pl.program_id(1), scale)
        vb = v_ref[0, pl.ds(st, 2 * WIN), :]
        o_ref[0, 0] = jnp.dot(p.astype(BF16), vb, preferred_element_type=F32).astype(BF16)

    blk, kv, col = _swa_specs(S, gm)
    return _call(
        body, name=name, grid=(KH, nb), in_specs=[blk, kv, kv, col], out_specs=[blk],
        out_shape=[jax.ShapeDtypeStruct(q.shape, BF16)], sem=("parallel", "parallel"), args=(q, k, v, sinks), comm=comm)[0]


def _swa_bwd(q, k, v, sinks, do, name, comm=None):
    KH, nb, gm, _ = q.shape
    S = k.shape[1]
    scale = SDH ** -0.5

    def body(q_ref, k_ref, v_ref, sk_ref, do_ref, dq_ref, dk_ref, dv_ref, dsk_ref):
        n = pl.program_id(1)

        @pl.when(n == 0)
        def _():
            dk_ref[...] = jnp.zeros_like(dk_ref)
            dv_ref[...] = jnp.zeros_like(dv_ref)
            dsk_ref[...] = jnp.zeros_like(dsk_ref)

        p, ps, st, qb, kb = _swa_probs(q_ref, k_ref, sk_ref, n, scale)
        vb = v_ref[0, pl.ds(st, 2 * WIN), :]
        dob = do_ref[0, 0]
        dp = lax.dot_general(dob, vb, _DN["nt"], preferred_element_type=F32)
        delta = jnp.sum(p * dp, axis=-1, keepdims=True)
        ds = p * (dp - delta)
        dsb = (ds * scale).astype(BF16)
        dq_ref[0, 0] = jnp.dot(dsb, kb, preferred_element_type=F32)
        dk_ref[0, pl.ds(st, 2 * WIN), :] += lax.dot_general(dsb, qb, _DN["tn"], preferred_element_type=F32)
        dv_ref[0, pl.ds(st, 2 * WIN), :] += lax.dot_general(p.astype(BF16), dob, _DN["tn"], preferred_element_type=F32)
        dsk_ref[0] -= ps * delta

    blk, kv, col = _swa_specs(S, gm)
    return _call(
        body, name=name, grid=(KH, nb), in_specs=[blk, kv, kv, col, blk], out_specs=[blk, kv, kv, col],
        out_shape=[jax.ShapeDtypeStruct(q.shape, F32), jax.ShapeDtypeStruct(k.shape, F32),
                   jax.ShapeDtypeStruct(k.shape, F32), jax.ShapeDtypeStruct(sinks.shape, F32)],
        sem=("parallel", "arbitrary"), args=(q, k, v, sinks, do), comm=comm)


def _shift_down(u, k):
    row = lax.broadcasted_iota(jnp.int32, u.shape, 0)
    return jnp.where(row >= k, pltpu.roll(u, k, 0), 0.0)


def _shift_up(u, k):
    n = u.shape[0]
    row = lax.broadcasted_iota(jnp.int32, u.shape, 0)
    return jnp.where(row < n - k, pltpu.roll(u, n - k, 0), 0.0)


def _conv3(u, w_ref, b_ref):
    return w_ref[0:1, :] * _shift_down(u, 2) + w_ref[1:2, :] * _shift_down(u, 1) + w_ref[2:3, :] * u + b_ref[...]


def _conv_gate(u, cw, cb, name, comm=None):
    S, F2 = u.shape
    Fh = F2 // 2
    tc = _tile(Fh, 256)
    nf = Fh // tc

    def body(ug_ref, uv_ref, wg_ref, wv_ref, bg_ref, bv_ref, a_ref):
        g = _conv3(ug_ref[...], wg_ref, bg_ref)
        val = _conv3(uv_ref[...], wv_ref, bv_ref)
        a_ref[...] = (g * (1.0 / (1.0 + jnp.exp(-g))) * val).astype(BF16)

    blk = lambda r, off: pl.BlockSpec((r, tc), lambda j: (0, j + off))
    return _call(
        body, name=name, grid=(nf,),
        in_specs=[blk(S, 0), blk(S, nf), blk(3, 0), blk(3, nf), blk(1, 0), blk(1, nf)], out_specs=[blk(S, 0)],
        out_shape=[jax.ShapeDtypeStruct((S, Fh), BF16)], sem=("parallel",), args=(u, u, cw, cw, cb, cb), comm=comm)[0]


def _conv_gate_bwd(u, da, cw, cb, name, comm=None):
    S, F2 = u.shape
    Fh = F2 // 2
    tc = _tile(Fh, 256)
    nf = Fh // tc

    def half(h, dx, uu, w_ref, du_ref, dw_ref, db_ref):
        du = w_ref[2:3, :] * dx + w_ref[1:2, :] * _shift_up(dx, 1) + w_ref[0:1, :] * _shift_up(dx, 2)
        du_ref[h] = du.astype(BF16)
        dw_ref[h, 0:1, :] = jnp.sum(dx * _shift_down(uu, 2), axis=0, keepdims=True)
        dw_ref[h, 1:2, :] = jnp.sum(dx * _shift_down(uu, 1), axis=0, keepdims=True)
        dw_ref[h, 2:3, :] = jnp.sum(dx * uu, axis=0, keepdims=True)
        db_ref[h] = jnp.sum(dx, axis=0, keepdims=True)

    def body(ug_ref, uv_ref, da_ref, wg_ref, wv_ref, bg_ref, bv_ref, du_ref, dw_ref, db_ref):
        ug = ug_ref[...]
        uv = uv_ref[...]
        g = _conv3(ug, wg_ref, bg_ref)
        val = _conv3(uv, wv_ref, bv_ref)
        sig = 1.0 / (1.0 + jnp.exp(-g))
        da_ = da_ref[...]
        dg = da_ * val * (sig * (1.0 + g * (1.0 - sig)))
        dval = da_ * (g * sig)
        half(0, dg, ug, wg_ref, du_ref, dw_ref, db_ref)
        half(1, dval, uv, wv_ref, du_ref, dw_ref, db_ref)

    blk = lambda r, off: pl.BlockSpec((r, tc), lambda j: (0, j + off))
    both = lambda r: pl.BlockSpec((2, r, tc), lambda j: (0, 0, j))
    return _call(
        body, name=name, grid=(nf,),
        in_specs=[blk(S, 0), blk(S, nf), blk(S, 0), blk(3, 0), blk(3, nf), blk(1, 0), blk(1, nf)],
        out_specs=[both(S), both(3), both(1)],
        out_shape=[jax.ShapeDtypeStruct((2, S, Fh), BF16), jax.ShapeDtypeStruct((2, 3, Fh), F32), jax.ShapeDtypeStruct((2, 1, Fh), F32)],
        sem=("parallel",), args=(u, u, da, cw, cw, cb, cb), comm=comm)


def _to_groups(t, kh):
    S, width = t.shape
    g = width // SDH // kh
    return t.reshape(S // WIN, WIN, kh, g, SDH).transpose(2, 0, 3, 1, 4).reshape(kh, S // WIN, g * WIN, SDH)


def _from_groups(t):
    kh, nb, gm, _ = t.shape
    g = gm // WIN
    return t.reshape(kh, nb, g, WIN, SDH).transpose(1, 3, 0, 2, 4).reshape(nb * WIN, kh * g * SDH)


class LocalWeights:
    def __init__(self, weights):
        self.weights, self.grads = weights, {}

    def w(self, name):
        return self.weights[name]

    def carry(self, stage):
        return None

    def carried(self, stage, comm):
        pass

    def grad(self, name, g):
        self.grads[name] = g


def _local_step(dm, x, tgt, pos, mod, sp, pp):
    S, D, FH, QH, KH, Fh = dm
    m = [[mod[i:i + 1, j * D:(j + 1) * D] for j in range(6)] for i in range(DEPTH)]

    def run(fn, *args, name, **kw):
        comm = pp.carry(name)
        out = fn(*args, name=name, comm=comm, **kw)
        if comm is not None:
            pp.carried(name, comm)
        return out

    sv = []
    xs = x
    h = _modulate(xs, m[0][1], m[0][0], "mod_in")
    for i in range(DEPTH):
        sh1, sc1, g1, sh2, sc2, g2 = m[i]
        L = {}
        L["x_in"], L["h1"] = xs, h
        if i == 0:
            proj = run(_mm, h[None], pp.w("fox_w_in"), mode="nn", out_dtype=F32, name="fox_proj", tn=896)[0]
            cum = _fox_gate_fwd(proj, sp["fox_b_f"], FH, "fox_gate")
            cq = cum[:, :FH].T[:, :, None]
            ck = cum[:, :FH].T[:, None, :]
            o, lse = run(_fox_fwd, proj, cq, ck, FH, name="fox_attn")
            L.update(proj=proj, cq=cq, ck=ck, lse=lse, o=o)
            y = run(_mm, o[None], pp.w("fox_w_o"), mode="nn", out_dtype=F32, name="fox_out")[0]
        else:
            proj = run(_mm, h[None], pp.w("swa_w_in"), mode="nn", out_dtype=F32, name="swa_proj", tn=640)[0]
            tabs = _rope_tables(pos, 1.0)
            n_rot = (QH + KH) * SDH // LANES
            pr = run(_rope, proj, tabs, n_rot, BF16, name="swa_rope")
            qh = _to_groups(pr[:, :QH * SDH], KH)
            kh = pr[:, QH * SDH:(QH + KH) * SDH].reshape(S, KH, SDH).transpose(1, 0, 2)
            vh = pr[:, (QH + KH) * SDH:].reshape(S, KH, SDH).transpose(1, 0, 2)
            oh = run(_swa_fwd, qh, kh, vh, sp["sinks"], name="swa_attn")
            o = _from_groups(oh)
            L.update(qh=qh, kh=kh, vh=vh, o=o)
            y = run(_mm, o[None], pp.w("swa_w_o"), mode="nn", out_dtype=F32, name="swa_out")[0]
        L["y1"] = y
        x1, L["xh1"], L["rs1"], h2 = run(_ln_fwd, xs, y, g1, sp["ln_mix_g"][i], sp["ln_mix_b"][i], sc2, sh2, name=f"ln_mix{i}")
        L["x1"], L["h2"] = x1, h2
        u = run(_mm, h2[None], pp.w(f"ffn_w_up{i}"), mode="nn", out_dtype=F32, name=f"ffn_up{i}", tm=512, tn=1408)[0]
        a = run(_conv_gate, u, sp["conv_w"][i], sp["conv_b"][i], name=f"ffn_gate{i}")
        y2 = run(_mm, a[None], pp.w(f"ffn_w_down{i}"), mode="nn", out_dtype=F32, name=f"ffn_down{i}", tk=1408)[0]
        L.update(u=u, a=a, y2=y2)
        if i + 1 < DEPTH:
            xs, L["xh2"], L["rs2"], h = run(_ln_fwd, x1, y2, g2, sp["ln_ffn_g"][i], sp["ln_ffn_b"][i], m[i + 1][1], m[i + 1][0], name=f"ln_ffn{i}")
        else:
            xs, L["xh2"], L["rs2"] = run(_ln_fwd, x1, y2, g2, sp["ln_ffn_g"][i], sp["ln_ffn_b"][i], None, None, name=f"ln_ffn{i}")
        sv.append(L)

    dx, loss_cols = _loss_head(xs, tgt, "loss_head")

    gs = {k: [None] * DEPTH for k in ("conv_w", "conv_b", "ln_mix_g", "ln_mix_b", "ln_ffn_g", "ln_ffn_b")}
    dmod = [None] * DEPTH
    for i in reversed(range(DEPTH)):
        sh1, sc1, g1, sh2, sc2, g2 = m[i]
        L = sv[i]
        dres, dy, gs["ln_ffn_g"][i], gs["ln_ffn_b"][i], dg2 = _ln_bwd(dx, L["xh2"], L["rs2"], sp["ln_ffn_g"][i], L["y2"], g2, f"ln_ffn_bwd{i}")
        da = run(_mm, dy[None], pp.w(f"ffn_w_down{i}"), mode="nt", out_dtype=F32, name=f"ffn_down_dx{i}", tm=512, tn=1408)[0]
        pp.grad(f"ffn_w_down{i}", run(_mm, L["a"][None], dy[None], mode="tn", out_dtype=BF16, name=f"ffn_down_dw{i}", tm=1408))
        du, dcw, dcb = run(_conv_gate_bwd, L["u"], da, sp["conv_w"][i], sp["conv_b"][i], name=f"ffn_gate_bwd{i}")
        gs["conv_w"][i] = dcw.transpose(1, 0, 2).reshape(3, 2 * Fh)
        gs["conv_b"][i] = dcb.transpose(1, 0, 2).reshape(1, 2 * Fh)
        dh2 = run(_mm, du, pp.w(f"ffn_w_up{i}"), mode="nt", out_dtype=F32, name=f"ffn_up_dx{i}", tk=1408)[0]
        pp.grad(f"ffn_w_up{i}", run(_mm, L["h2"][None], du, mode="tn", out_dtype=BF16, name=f"ffn_up_dw{i}", out_groups=N_CHIPS, tn=1408))
        dx, dsc2, dsh2 = _mod_bwd(dh2, L["x1"], sc2, dres, f"mod_ffn_bwd{i}")
        dres, dy, gs["ln_mix_g"][i], gs["ln_mix_b"][i], dg1 = _ln_bwd(dx, L["xh1"], L["rs1"], sp["ln_mix_g"][i], L["y1"], g1, f"ln_mix_bwd{i}")
        if i == 0:
            do = run(_mm, dy[None], pp.w("fox_w_o"), mode="nt", out_dtype=BF16, name="fox_out_dx")[0]
            pp.grad("fox_w_o", run(_mm, L["o"][None], dy[None], mode="tn", out_dtype=BF16, name="fox_out_dw"))
            dq, dk, dv, dcq, dck = run(_fox_bwd, L["proj"], L["cq"], L["ck"], L["lse"], do, FH, name="fox_attn_bwd")
            dcum = dcq[:, :, 0].T + dck[:, 0, :].T
            dcum = jnp.pad(dcum, ((0, 0), (0, LANES - FH)))
            dfl, db_f = _fox_gate_bwd(dcum, L["proj"], sp["fox_b_f"], FH, "fox_gate_bwd")
            gs["fox_b_f"] = db_f
            dproj = jnp.concatenate([dq, dk, dv, dfl], axis=1)
            pp.grad("fox_w_in", run(_mm, L["h1"][None], dproj[None], mode="tn", out_dtype=BF16, name="fox_proj_dw", tn=896))
            dh1 = run(_mm, dproj[None], pp.w("fox_w_in"), mode="nt", out_dtype=F32, name="fox_proj_dx", tk=896)[0]
        else:
            do = run(_mm, dy[None], pp.w("swa_w_o"), mode="nt", out_dtype=BF16, name="swa_out_dx")[0]
            pp.grad("swa_w_o", run(_mm, L["o"][None], dy[None], mode="tn", out_dtype=BF16, name="swa_out_dw"))
            dqh, dkh, dvh, dsk = run(_swa_bwd, L["qh"], L["kh"], L["vh"], sp["sinks"], _to_groups(do, KH), name="swa_attn_bwd")
            gs["sinks"] = jnp.sum(dsk.reshape(QH, WIN), axis=1)
            dpr = jnp.concatenate([_from_groups(dqh), dkh.transpose(1, 0, 2).reshape(S, KH * SDH),
                                   dvh.transpose(1, 0, 2).reshape(S, KH * SDH)], axis=1)
            n_rot = (QH + KH) * SDH // LANES
            dproj = _rope(dpr, _rope_tables(pos, -1.0), n_rot, BF16, "swa_rope_bwd")
            dh1 = run(_mm, dproj[None], pp.w("swa_w_in"), mode="nt", out_dtype=F32, name="swa_proj_dx", tk=640)[0]
            pp.grad("swa_w_in", run(_mm, L["h1"][None], dproj[None], mode="tn", out_dtype=BF16, name="swa_proj_dw", out_groups=N_CHIPS, tn=640))
        dx, dsc1, dsh1 = _mod_bwd(dh1, L["x_in"], sc1, dres, f"mod_mix_bwd{i}")
        dmod[i] = jnp.concatenate([dsh1, dsc1, dg1, dsh2, dsc2, dg2], axis=1)
    return loss_cols, dx, gs, jnp.concatenate(dmod, axis=0)


def _allgather_small(v, name):
    m_per, n = v.shape

    def body(x_ref, out_ref, send_sems, recv_sems, local_sem):
        x, y, c, chips = _place()
        me, sibling = (x, y, c), (x, y, 1 - c)

        def rows(px, py, pc):
            return out_ref.at[pl.ds((4 * px + 2 * py + pc) * m_per, m_per), :]

        def copy(k, block, to, src=None):
            return _remote(rows(*block) if src is None else src, rows(*block), send_sems.at[k], recv_sems.at[k], to)

        mine = pltpu.make_async_copy(x_ref, rows(*me), local_sem)
        mine.start()
        first = [copy(0, me, sibling, src=x_ref)]
        first += [copy(1 + j, me, (*chip, c), src=x_ref) for j, chip in enumerate(chips)]
        for cp in first:
            cp.start()
        passed = [copy(4 + j, (*chip, c), sibling) for j, chip in enumerate(chips)]
        for j, chip in enumerate(chips):
            copy(1 + j, (*chip, c), me).wait_recv()
            passed[j].start()
        copy(0, sibling, me).wait_recv()
        for j, chip in enumerate(chips):
            copy(4 + j, (*chip, 1 - c), me).wait_recv()
        for cp in first + passed:
            cp.wait_send()
        mine.wait()

    return pl.pallas_call(
        body, name=name, out_shape=jax.ShapeDtypeStruct((N_DEV * m_per, n), v.dtype),
        in_specs=[pl.BlockSpec(memory_space=pltpu.VMEM)], out_specs=pl.BlockSpec(memory_space=pltpu.VMEM),
        scratch_shapes=[pltpu.SemaphoreType.DMA((7,)), pltpu.SemaphoreType.DMA((7,)), pltpu.SemaphoreType.DMA],
        compiler_params=pltpu.CompilerParams(vmem_limit_bytes=VMEM_LIMIT),
    )(v)


def _row_tile(r, pref=256):
    return _tile(r, pref, 16)


def _cast_bf16(w, layer, chip, name):
    _, R, C = w.shape
    tr = _row_tile(R)

    def body(s_ref, w_ref, o_ref):
        o_ref[...] = w_ref[...].astype(BF16)

    return pl.pallas_call(
        body, name=name,
        grid_spec=pltpu.PrefetchScalarGridSpec(
            num_scalar_prefetch=1, grid=(R // tr,),
            in_specs=[pl.BlockSpec((None, tr, C), lambda i, s: (layer, i, 0))],
            out_specs=pl.BlockSpec((None, tr, C), lambda i, s: (s[0], i, 0))),
        out_shape=jax.ShapeDtypeStruct((N_CHIPS, R, C), BF16), compiler_params=_params(("parallel",)),
    )(jnp.reshape(chip, (1,)).astype(jnp.int32), w)


def _add_sibling(g, got, c, name):
    G, R, C = g.shape
    rh = R // 2
    tr = _row_tile(rh)
    nb = rh // tr

    def body(c_ref, g_ref, o_ref, p_ref):
        p_ref[...] = (g_ref[...].astype(F32) + o_ref[...].astype(F32)).astype(BF16)

    return pl.pallas_call(
        body, name=name,
        grid_spec=pltpu.PrefetchScalarGridSpec(
            num_scalar_prefetch=1, grid=(G, nb),
            in_specs=[pl.BlockSpec((1, tr, C), lambda s, i, c_ref: (s, c_ref[0] * nb + i, 0)),
                      pl.BlockSpec((1, tr, C), lambda s, i, c_ref: (s, i, 0))],
            out_specs=pl.BlockSpec((1, tr, C), lambda s, i, c_ref: (s, i, 0))),
        out_shape=jax.ShapeDtypeStruct((G, rh, C), BF16), compiler_params=_params(("parallel", "parallel")),
    )(jnp.reshape(c, (1,)).astype(jnp.int32), g, got)


def _sum_chips(part, landed, chip, c, name):
    G, rh, C = part.shape
    tr = _row_tile(rh)
    nb = rh // tr

    def body(p_ref, own_ref, *rest):
        acc = own_ref[...].astype(F32)
        for ref in rest[:G - 1]:
            acc = acc + ref[...].astype(F32)
        rest[G - 1][...] = acc

    slot = lambda k: pl.BlockSpec((None, tr, C), lambda i, p: ((p[0] + k) % G, i, 0))
    return pl.pallas_call(
        body, name=name,
        grid_spec=pltpu.PrefetchScalarGridSpec(
            num_scalar_prefetch=1, grid=(nb,), in_specs=[slot(k) for k in range(G)],
            out_specs=pl.BlockSpec((tr, C), lambda i, p: (p[1] * nb + i, 0))),
        out_shape=jax.ShapeDtypeStruct((2 * rh, C), F32), compiler_params=_params(("parallel",)),
    )(jnp.stack([chip, c]).astype(jnp.int32), part, *([landed] * (G - 1)))


def _adam_math(w, g, m, v):
    m = ADAM_B1 * m + (1.0 - ADAM_B1) * g
    v = ADAM_B2 * v + (1.0 - ADAM_B2) * (g * g)
    m_hat = m / (1.0 - ADAM_B1 ** ADAM_STEP)
    v_hat = v / (1.0 - ADAM_B2 ** ADAM_STEP)
    delta = -ADAM_LR * (m_hat / (jnp.sqrt(v_hat) + ADAM_EPS) + ADAM_WD * w)
    return delta, m, v


def _adamw(w, g, m, v, layer, prev, name, by_cols=False):
    L, R, C = w.shape
    tr = R if by_cols else _tile(R, 128, 8)
    tc = _tile(C, 256) if by_cols else C
    n_prev = len(prev)

    def body(w_ref, g_ref, m_ref, v_ref, *rest):
        go_ref, d_ref, mo_ref, vo_ref = rest[n_prev:]
        gv = g_ref[...]
        go_ref[...] = gv
        d_ref[...], mo_ref[...], vo_ref[...] = _adam_math(w_ref[...], gv, m_ref[...], v_ref[...])

    lay = pl.BlockSpec((None, tr, tc), lambda i: (layer, i // (C // tc), i % (C // tc)))
    flat = pl.BlockSpec((tr, tc), lambda i: (i // (C // tc), i % (C // tc)))
    return _call(
        body, name=name, grid=((R // tr) * (C // tc),), in_specs=[lay, flat, lay, lay] + _any_specs(n_prev), out_specs=[lay] * 4,
        out_shape=[jax.ShapeDtypeStruct((L, R, C), F32)] * 4, aliases={4 + k: k for k in range(n_prev)},
        sem=("parallel",), args=(w, g, m, v, *prev))


def _cond_rows(c_row, cw, name):
    D = c_row.shape[1]
    nr, fc = cw.shape

    def body(c_ref, e_ref, o_ref):
        o_ref[...] = jnp.zeros_like(o_ref)
        cv = c_ref[...]
        o_ref[0:1, 0:D] = cv * (1.0 / (1.0 + jnp.exp(-cv)))
        o_ref[8:8 + nr, 0:fc] = e_ref[...]

    return pl.pallas_call(body, name=name, out_shape=jax.ShapeDtypeStruct((16, max(D, fc)), F32))(c_row, cw)


def _ada_fwd(cact, ada_w, ada_b, layer, chip, name):
    _, D, NC = ada_w.shape
    tn = _tile(NC, 1024)
    nj = NC // tn

    def body(idx_ref, c_ref, w_ref, b_ref, o_ref):
        acc = jnp.dot(c_ref[...].astype(BF16), w_ref[0].astype(BF16), preferred_element_type=F32)
        o_ref[...] = acc + b_ref[pl.ds(idx_ref[0], 1), :]

    return pl.pallas_call(
        body, name=name,
        grid_spec=pltpu.PrefetchScalarGridSpec(
            num_scalar_prefetch=1, grid=(nj,),
            in_specs=[pl.BlockSpec((8, D), lambda j, idx: (0, 0)),
                      pl.BlockSpec((1, D, tn), lambda j, idx: (idx[0], 0, j)),
                      pl.BlockSpec((DEPTH, tn), lambda j, idx: (0, idx[1] * nj + j))],
            out_specs=pl.BlockSpec((8, tn), lambda j, idx: (0, j))),
        out_shape=jax.ShapeDtypeStruct((8, NC), F32), compiler_params=_params(("parallel",)),
    )(jnp.stack([layer, chip]).astype(jnp.int32), cact, ada_w, ada_b)


def _ada_grad_adamw(cact_t, dmod, w, m, v, name, comm=None):
    L, D, NC = w.shape
    tr = _tile(D, 128, 8)

    def body(c_ref, d_ref, w_ref, m_ref, v_ref, g_ref, dl_ref, mo_ref, vo_ref):
        g = jnp.dot(c_ref[...], d_ref[...], preferred_element_type=F32, precision=HIGHEST)
        g_ref[...] = g
        dl_ref[...], mo_ref[...], vo_ref[...] = _adam_math(w_ref[...], g, m_ref[...], v_ref[...])

    lay = pl.BlockSpec((None, tr, NC), lambda l, i: (l, i, 0))
    return _call(
        body, name=name, grid=(L, D // tr),
        in_specs=[pl.BlockSpec((tr, N_DEV), lambda l, i: (i, 0)), pl.BlockSpec((None, N_DEV, NC), lambda l, i: (l, 0, 0)), lay, lay, lay],
        out_specs=[lay] * 4, out_shape=[jax.ShapeDtypeStruct((L, D, NC), F32)] * 4,
        sem=("parallel", "parallel"), args=(cact_t, dmod, w, m, v), comm=comm)


def _sum_devices(gathered, name):
    n, R, C = gathered.shape

    def body(g_ref, o_ref):
        acc = g_ref[0]
        for j in range(1, n):
            acc = acc + g_ref[j]
        o_ref[...] = acc

    return pl.pallas_call(body, name=name, out_shape=jax.ShapeDtypeStruct((R, C), F32),
                          compiler_params=pltpu.CompilerParams(vmem_limit_bytes=VMEM_LIMIT))(gathered)


def _adamw_small(w, g, m, v, name):
    def body(w_ref, g_ref, m_ref, v_ref, d_ref, mo_ref, vo_ref):
        d_ref[...], mo_ref[...], vo_ref[...] = _adam_math(w_ref[...], g_ref[...], m_ref[...], v_ref[...])

    return pl.pallas_call(body, name=name, out_shape=[jax.ShapeDtypeStruct(w.shape, F32)] * 3)(w, g, m, v)


def _pad_rows(flat, unit=8 * LANES):
    n = flat.shape[0]
    total = -(-n // unit) * unit
    return jnp.pad(flat, (0, total - n)).reshape(total // LANES, LANES)


def _pad_lanes(v2d):
    return jnp.pad(v2d.reshape(1, -1), ((0, 0), (0, LANES - v2d.size)))


PLAN = {
    "fox_proj": [("gather", "ffn_w_up0", 0, 2, 8)],
    "fox_attn": [("gather", "ffn_w_up0", 2, 6, 8)],
    "fox_out": [("gather", "ffn_w_up0", 6, 7, 8)],
    "ln_mix0": [("gather", "ffn_w_up0", 7, 8, 8)],
    "ffn_up0": [("gather", "ffn_w_down0", 0, 1, 1)],
    "ffn_gate0": [("gather", "swa_w_in", 0, 1, 1)],
    "ffn_down0": [("gather", "swa_w_o", 0, 1, 1), ("gather", "ffn_w_up1", 0, 1, 8)],
    "ln_ffn0": [("gather", "ffn_w_up1", 1, 2, 8)],
    "swa_proj": [("gather", "ffn_w_up1", 2, 3, 8)],
    "swa_rope": [("gather", "ffn_w_up1", 3, 4, 8)],
    "swa_attn": [("gather", "ffn_w_up1", 4, 8, 8)],
    "ffn_up1": [("gather", "ffn_w_down1", 0, 1, 1)],
    "ffn_gate_bwd1": [("swap", "ffn_w_down1")],
    "ffn_up_dx1": [("scatter", "ffn_w_down1", 0, 1, 1)],
    "swa_out_dx": [("swap", "ffn_w_up1")],
    "swa_attn_bwd": [("scatter", "ffn_w_up1", 0, 1, 1), ("swap", "swa_w_o")],
    "swa_proj_dx": [("scatter", "swa_w_o", 0, 1, 1)],
    "ffn_down_dx0": [("swap", "swa_w_in")],
    "ffn_down_dw0": [("scatter", "swa_w_in", 0, 1, 1)],
    "ffn_gate_bwd0": [("swap", "ffn_w_down0")],
    "ffn_up_dx0": [("scatter", "ffn_w_down0", 0, 1, 1)],
    "fox_out_dx": [("swap", "ffn_w_up0")],
    "fox_attn_bwd": [("scatter", "ffn_w_up0", 0, 1, 2), ("swap", "fox_w_o")],
    "fox_proj_dw": [("scatter", "fox_w_o", 0, 1, 1)],
    "fox_proj_dx": [("scatter", "ffn_w_up0", 1, 2, 2), ("swap", "fox_w_in")],
}


class Exchanges:
    def __init__(self, dm, slots, chip, c):
        self.dm, self.slots, self.chip, self.c = dm, dict(slots), chip, c
        self.raw, self.part, self.landed, self.grads, self.views, self.pending = {}, {}, {}, {}, {}, {}

    def gather_now(self, keys, name):
        comm = _gather_comm([self.slots[k] for k in keys], [(0, 1, 1)] * len(keys))
        _run_comm(comm, name)
        self.slots.update(zip(keys, comm.results))

    def w(self, key):
        if key not in self.views:
            S, D, FH, QH, KH, Fh = self.dm
            full = self.slots[key]
            if key == "fox_w_in":
                cols = full.shape[2]
                full = jnp.pad(full.transpose(1, 0, 2).reshape(D, N_CHIPS * cols), ((0, 0), (0, 3 * D + LANES - N_CHIPS * cols)))[None]
            elif key in ("fox_w_o", "swa_w_o"):
                full = full.reshape(1, D, D)
            elif key.startswith("ffn_w_down"):
                full = full.reshape(1, Fh, D)
            self.views[key] = full
        return self.views[key]

    def carry(self, stage):
        todo = []
        for kind, key, *chunk in PLAN.get(stage, ()):
            if kind == "gather":
                todo.append((kind, [key], _gather_comm([self.slots[key]], [tuple(chunk)])))
            elif kind == "swap":
                todo.append((kind, [key], _swap_comm([self.raw[key]])))
            elif kind == "scatter":
                todo.append((kind, [(key, *chunk)], _scatter_comm([self.part[key]], [self.landed.get(key)], [tuple(chunk)])))
        self.pending[stage] = todo
        return _merge([cm for _, _, cm in todo])

    def carried(self, stage, comm):
        for kind, keys, cm in self.pending.pop(stage):
            if kind == "gather":
                self.slots[keys[0]] = cm.results[0]
            elif kind == "swap":
                self.part[keys[0]] = _add_sibling(self.raw[keys[0]], cm.results[0], self.c, f"add_sibling_{keys[0]}")
            else:
                self.landed[keys[0][0]] = cm.results[0]

    def grad(self, key, g):
        S, D, FH, QH, KH, Fh = self.dm
        if key == "fox_w_in":
            cols = self.slots[key].shape[2]
            g = g[0][:, :N_CHIPS * cols].reshape(D, N_CHIPS, cols).transpose(1, 0, 2)
        elif key in ("fox_w_o", "swa_w_o"):
            g = g.reshape(N_CHIPS, D // N_CHIPS, D)
        elif key.startswith("ffn_w_down"):
            g = g.reshape(N_CHIPS, Fh // N_CHIPS, D)
        self.raw[key] = g

    def finish(self):
        last = "fox_w_in"
        keys = list(self.landed)
        halves = [_sum_chips(self.part[k], self.landed[k], self.chip, self.c, f"sum_chips_{k}") for k in keys]
        send, join = _scatter_comm([self.part[last]], [None], [(0, 1, 1)]), _join_comm(halves)
        _run_comm(_merge([send, join]), "grads_tail")
        grads = dict(zip(keys, join.results))
        join = _join_comm([_sum_chips(self.part[last], send.results[0], self.chip, self.c, f"sum_chips_{last}")])
        _run_comm(join, "grads_join_last")
        grads[last] = join.results[0]
        return grads


def _step(dm, a):
    S, D, FH, QH, KH, Fh = dm
    ix, iy, ic = lax.axis_index("x"), lax.axis_index("y"), lax.axis_index("c")
    chip = 2 * ix + iy
    dev = 2 * chip + ic
    F2c = a["ffn_w_up"].shape[2]
    NC = a["ada_w"].shape[2]
    PW = 3 * D + LANES
    fox_cols = a["fox_w_in"].shape[2]

    e0 = _cond_rows(a["c"], a["ffn_conv_w"].reshape(DEPTH * 3, F2c), "silu_c")
    g0 = _allgather_small(e0, "gather_cond").reshape(N_DEV, 16, e0.shape[1])
    cact = g0[:, 0, :D]
    conv_w = g0[0::2, 8:8 + DEPTH * 3, :F2c].transpose(1, 0, 2).reshape(DEPTH, 3, N_CHIPS * F2c)
    rows = _ada_fwd(cact, a["ada_w"], a["ada_b"], ic, chip, "ada_proj")
    g1 = _allgather_small(rows, "gather_mod").reshape(N_CHIPS, DEPTH, 8, NC)
    mod = lax.dynamic_index_in_dim(g1, dev, axis=2, keepdims=False).transpose(1, 0, 2).reshape(DEPTH, N_CHIPS * NC)

    names = ["fox_w_in", "fox_w_o", "swa_w_in", "swa_w_o", "ffn_w_up", "ffn_w_up", "ffn_w_down", "ffn_w_down"]
    layers = [0, 0, 0, 0, 0, 1, 0, 1]
    keys = ["fox_w_in", "fox_w_o", "swa_w_in", "swa_w_o", "ffn_w_up0", "ffn_w_up1", "ffn_w_down0", "ffn_w_down1"]
    slots = {k: _cast_bf16(a[nm], l, chip, f"cast_{k}") for k, nm, l in zip(keys, names, layers)}
    pp = Exchanges(dm, slots, chip, ic)
    pp.gather_now(["fox_w_in", "fox_w_o"], "gather_fox")
    sp = {"fox_b_f": _pad_lanes(a["fox_b_f"]), "sinks": jnp.repeat(a["swa_sinks"].reshape(KH, QH // KH), WIN, axis=1)[:, :, None],
          "conv_w": [conv_w[i] for i in range(DEPTH)], "conv_b": [a["ffn_conv_b"][i:i + 1] for i in range(DEPTH)]}
    for nm in ("ln_mix_g", "ln_mix_b", "ln_ffn_g", "ln_ffn_b"):
        sp[nm] = [a[nm][i:i + 1] for i in range(DEPTH)]

    loss_cols, grad_x, gs, dmod = _local_step(dm, a["x"][0], a["loss_target"][0], a["positions"][0], mod, sp, pp)
    loss = lax.psum(0.5 / D * jnp.sum(loss_cols), ("x", "y", "c"))
    out = {"loss": loss, "grad_x": grad_x[None]}

    def run(fn, *args, name, **kw):
        comm = pp.carry(name)
        res = fn(*args, name=name, comm=comm, **kw)
        if comm is not None:
            pp.carried(name, comm)
        return res

    pieces = [dmod.reshape(-1), gs["fox_b_f"].reshape(-1), _pad_lanes(gs["sinks"]).reshape(-1),
              jnp.stack(gs["conv_w"]).reshape(-1), jnp.stack(gs["conv_b"]).reshape(-1)]
    pieces += [jnp.stack(gs[nm]).reshape(-1) for nm in ("ln_mix_g", "ln_mix_b", "ln_ffn_g", "ln_ffn_b")]
    sizes = [p.shape[0] for p in pieces]
    packed = _pad_rows(jnp.concatenate(pieces))
    allp = _allgather_small(packed, "gather_small").reshape(N_DEV, packed.shape[0], LANES)
    tot = _sum_devices(allp, "sum_small").reshape(-1)
    offs = [sum(sizes[:k]) for k in range(len(sizes))]
    take = lambda k: tot[offs[k]:offs[k] + sizes[k]]
    g_small = {"ada_b": take(0).reshape(DEPTH, -1), "fox_b_f": take(1)[:FH].reshape(1, FH), "swa_sinks": take(2)[:QH].reshape(1, QH),
               "ffn_conv_w": lax.dynamic_slice_in_dim(take(3).reshape(DEPTH, 3, N_CHIPS * F2c), chip * F2c, F2c, axis=2),
               "ffn_conv_b": take(4).reshape(DEPTH, -1)}
    for k, nm in enumerate(("ln_mix_g", "ln_mix_b", "ln_ffn_g", "ln_ffn_b")):
        g_small[nm] = take(5 + k).reshape(DEPTH, D)
    small = list(g_small)
    pack = lambda pre: _pad_rows(jnp.concatenate([(a[pre + nm] if pre else a[nm]).reshape(-1) for nm in small]))
    gp = _pad_rows(jnp.concatenate([g_small[nm].reshape(-1) for nm in small]))
    ds_, ms_, vs_ = _adamw_small(pack(""), gp, pack("m_"), pack("v_"), "adamw_small")
    off = 0
    for nm in small:
        n_el = a[nm].size
        out["grad_" + nm] = g_small[nm]
        for pre, arr in (("delta_", ds_), ("new_m_", ms_), ("new_v_", vs_)):
            out[pre + nm] = arr.reshape(-1)[off:off + n_el].reshape(a[nm].shape)
        off += n_el

    dmod_all = allp.reshape(N_DEV, -1)[:, :DEPTH * N_CHIPS * NC].reshape(N_DEV, DEPTH, N_CHIPS * NC)
    dmod_mine = lax.dynamic_slice_in_dim(dmod_all, chip * NC, NC, axis=2).transpose(1, 0, 2)
    ada = _ada_grad_adamw(cact.T, dmod_mine, a["ada_w"], a["m_ada_w"], a["v_ada_w"], "ada_grad")
    for pre, arr in zip(("grad_", "delta_", "new_m_", "new_v_"), ada):
        out[pre + "ada_w"] = arr

    grads = pp.finish()
    upd = {}
    for k, nm, l in zip(keys[1:], names[1:], layers[1:]):
        upd[nm] = _adamw(a[nm], grads[k], a["m_" + nm], a["v_" + nm], l, upd.get(nm, ()), f"adamw_{k}")
    tview = lambda t: jnp.swapaxes(t, 1, 2)
    res = _adamw(tview(a["fox_w_in"]), grads["fox_w_in"].T, tview(a["m_fox_w_in"]), tview(a["v_fox_w_in"]), 0, (), "adamw_fox_w_in", by_cols=True)
    upd["fox_w_in"] = [tview(r) for r in res]
    for nm, res in upd.items():
        for pre, arr in zip(("grad_", "delta_", "new_m_", "new_v_"), res):
            out[pre + nm] = arr
    return out


_WEIGHTS = ["fox_w_in", "fox_b_f", "fox_w_o", "swa_w_in", "swa_sinks", "swa_w_o", "ada_w", "ada_b", "ffn_w_up", "ffn_conv_w",
            "ffn_conv_b", "ffn_w_down", "ln_mix_g", "ln_mix_b", "ln_ffn_g", "ln_ffn_b"]
_INPUTS = (["x", "c", "positions"] + _WEIGHTS + ["loss_target"] + ["m_" + w for w in _WEIGHTS] + ["v_" + w for w in _WEIGHTS])


def kernel(x, c, positions, fox_w_in, fox_b_f, fox_w_o, swa_w_in, swa_sinks, swa_w_o, ada_w, ada_b, ffn_w_up, ffn_conv_w, ffn_conv_b, ffn_w_down, ln_mix_g, ln_mix_b, ln_ffn_g, ln_ffn_b, loss_target, m_fox_w_in, m_fox_b_f, m_fox_w_o, m_swa_w_in, m_swa_sinks, m_swa_w_o, m_ada_w, m_ada_b, m_ffn_w_up, m_ffn_conv_w, m_ffn_conv_b, m_ffn_w_down, m_ln_mix_g, m_ln_mix_b, m_ln_ffn_g, m_ln_ffn_b, v_fox_w_in, v_fox_b_f, v_fox_w_o, v_swa_w_in, v_swa_sinks, v_swa_w_o, v_ada_w, v_ada_b, v_ffn_w_up, v_ffn_conv_w, v_ffn_conv_b, v_ffn_w_down, v_ln_mix_g, v_ln_mix_b, v_ln_ffn_g, v_ln_ffn_b):
    args = (x, c, positions, fox_w_in, fox_b_f, fox_w_o, swa_w_in, swa_sinks, swa_w_o, ada_w, ada_b, ffn_w_up, ffn_conv_w, ffn_conv_b, ffn_w_down, ln_mix_g, ln_mix_b, ln_ffn_g, ln_ffn_b, loss_target, m_fox_w_in, m_fox_b_f, m_fox_w_o, m_swa_w_in, m_swa_sinks, m_swa_w_o, m_ada_w, m_ada_b, m_ffn_w_up, m_ffn_conv_w, m_ffn_conv_b, m_ffn_w_down, m_ln_mix_g, m_ln_mix_b, m_ln_ffn_g, m_ln_ffn_b, v_fox_w_in, v_fox_b_f, v_fox_w_o, v_swa_w_in, v_swa_sinks, v_swa_w_o, v_ada_w, v_ada_b, v_ffn_w_up, v_ffn_conv_w, v_ffn_conv_b, v_ffn_w_down, v_ln_mix_g, v_ln_mix_b, v_ln_ffn_g, v_ln_ffn_b)
    out = _step(PROD, dict(zip(_INPUTS, args)))
    order = ["loss", "grad_x"] + [p + w for p in ("grad_", "delta_", "new_m_", "new_v_") for w in _WEIGHTS]
    return tuple(out[k] for k in order)
```

```python
import functools
from typing import NamedTuple

import jax
import jax.numpy as jnp
from jax import lax
from jax.experimental import pallas as pl
from jax.experimental.pallas import tpu as pltpu

F32 = jnp.float32
BF16 = jnp.bfloat16
MESH = pl.DeviceIdType.MESH
HIGHEST = lax.Precision.HIGHEST

N_CHIPS = 4
N_DEV = 8
LANES = 128
VMEM_LIMIT = 56 * 1024 * 1024

DEPTH = 2
DEEPNORM_ALPHA = (2.0 * DEPTH) ** 0.25
LN_EPS = 1e-5
ROPE_THETA = 500000.0
ADAM_LR, ADAM_B1, ADAM_B2, ADAM_EPS, ADAM_WD, ADAM_STEP = 0.001, 0.9, 0.999, 1e-08, 0.01, 10
NEG = -1e30


class Dims(NamedTuple):
    S: int
    D: int
    FH: int
    QH: int
    KH: int
    F: int


PROD = Dims(S=2048, D=2048, FH=16, QH=32, KH=4, F=5632)
FDH = 128
SDH = 64
WIN = 128
ROPE_DIM = 16
FOX_TQ = 256


def _params(sem=None, vmem=VMEM_LIMIT):
    return pltpu.CompilerParams(dimension_semantics=sem, vmem_limit_bytes=vmem)


def _tile(n, pref, unit=LANES):
    if n <= pref:
        return n
    t = (pref // unit) * unit
    while t > 0:
        if n % t == 0:
            return t
        t -= unit
    return n


class Comm:
    def __init__(self, args, out_shapes, aliases, n_sem, start, finish, members=()):
        self.args, self.out_shapes, self.aliases, self.n_sem = list(args), list(out_shapes), dict(aliases), n_sem
        self.start, self.finish = start, finish
        self.members = members
        self.results = None

    def set_results(self, res):
        self.results = list(res)
        for cm, o0 in self.members:
            cm.set_results(self.results[o0:o0 + len(cm.out_shapes)])


class _SemView:
    def __init__(self, sems, first):
        self.sems, self.first = sems, first

    @property
    def at(self):
        return self

    def __getitem__(self, k):
        return self.sems.at[self.first + k]


def _merge(comms):
    comms = [cm for cm in comms if cm is not None]
    if len(comms) < 2:
        return comms[0] if comms else None
    args, shapes, aliases, spans, n_sem = [], [], {}, [], 0
    for cm in comms:
        spans.append((len(args), len(shapes), n_sem))
        aliases.update({len(args) + a: len(shapes) + o for a, o in cm.aliases.items()})
        args += cm.args
        shapes += cm.out_shapes
        n_sem += cm.n_sem

    def each(step):
        def run(ar, ou, send, recv):
            for cm, (a0, o0, s0) in zip(comms, spans):
                getattr(cm, step)(ar[a0:a0 + len(cm.args)], ou[o0:o0 + len(cm.out_shapes)], _SemView(send, s0), _SemView(recv, s0))
        return run

    return Comm(args, shapes, aliases, n_sem, each("start"), each("finish"), [(cm, o0) for cm, (_, o0, _) in zip(comms, spans)])


def _place():
    x, y, c = lax.axis_index("x"), lax.axis_index("y"), lax.axis_index("c")
    chips = [(1 - x, y), (x, 1 - y), (1 - x, 1 - y)]
    return x, y, c, chips


def _remote(src, dst, send, recv, to):
    return pltpu.make_async_remote_copy(src_ref=src, dst_ref=dst, send_sem=send, recv_sem=recv, device_id=to, device_id_type=MESH)


def _any_specs(n):
    return [pl.BlockSpec(memory_space=pl.ANY)] * n


def _call(body, *, name, grid, in_specs, out_specs, out_shape, args, sem, scratch_shapes=(), aliases=None, comm=None):
    in_specs, out_specs, out_shape, scratch_shapes = list(in_specs), list(out_specs), list(out_shape), list(scratch_shapes)
    aliases = dict(aliases or {})
    if comm is None:
        return pl.pallas_call(body, name=name, grid=grid, in_specs=in_specs, out_specs=out_specs, out_shape=out_shape,
                              scratch_shapes=scratch_shapes, input_output_aliases=aliases, compiler_params=_params(sem))(*args)
    n_in, n_out, nc_in, nc_out, n_scr = len(in_specs), len(out_specs), len(comm.args), len(comm.out_shapes), len(scratch_shapes)

    def wrapped(*refs):
        ins, refs = refs[:n_in], refs[n_in:]
        cin, refs = refs[:nc_in], refs[nc_in:]
        outs, refs = refs[:n_out], refs[n_out:]
        cout, refs = refs[:nc_out], refs[nc_out:]
        scratch, (send, recv) = refs[:n_scr], refs[n_scr:]
        ids = [pl.program_id(k) for k in range(len(grid))]
        first = functools.reduce(jnp.logical_and, [i == 0 for i in ids])
        last = functools.reduce(jnp.logical_and, [i == g - 1 for i, g in zip(ids, grid)])

        @pl.when(first)
        def _():
            comm.start(cin, cout, send, recv)

        body(*ins, *outs, *scratch)

        @pl.when(last)
        def _():
            comm.finish(cin, cout, send, recv)

    res = pl.pallas_call(
        wrapped, name=name, grid=grid, in_specs=in_specs + _any_specs(nc_in), out_specs=out_specs + _any_specs(nc_out),
        out_shape=out_shape + comm.out_shapes,
        scratch_shapes=scratch_shapes + [pltpu.SemaphoreType.DMA((comm.n_sem,)), pltpu.SemaphoreType.DMA((comm.n_sem,))],
        input_output_aliases={**aliases, **{n_in + a: n_out + o for a, o in comm.aliases.items()}},
        compiler_params=_params(("arbitrary",) * len(grid)),
    )(*args, *comm.args)
    comm.set_results(res[n_out:])
    return list(res[:n_out])


def _run_comm(comm, name):
    nc_in, nc_out = len(comm.args), len(comm.out_shapes)

    def body(*refs):
        cin, cout, (send, recv) = refs[:nc_in], refs[nc_in:nc_in + nc_out], refs[nc_in + nc_out:]
        comm.start(cin, cout, send, recv)
        comm.finish(cin, cout, send, recv)

    res = pl.pallas_call(
        body, name=name, in_specs=_any_specs(nc_in), out_specs=_any_specs(nc_out), out_shape=comm.out_shapes,
        scratch_shapes=[pltpu.SemaphoreType.DMA((comm.n_sem,)), pltpu.SemaphoreType.DMA((comm.n_sem,))],
        input_output_aliases=comm.aliases,
    )(*comm.args)
    comm.set_results(res)


def _gather_comm(slots, chunks):
    n = len(slots)

    def rows(t, who):
        rh = slots[t].shape[1] // 2
        lo, hi, nch = chunks[t]
        rc = rh // nch
        return pl.ds(who * rh + lo * rc, (hi - lo) * rc)

    def start(args, outs, send, recv):
        x, y, c, chips = _place()
        s = 2 * x + y
        for t in range(n):
            mine = outs[t].at[s, rows(t, c)]
            for j, chip in enumerate(chips):
                _remote(mine, mine, send.at[6 * t + j], recv.at[6 * t + j], (*chip, c)).start()

    def finish(args, outs, send, recv):
        x, y, c, chips = _place()
        s = 2 * x + y
        sib = (x, y, 1 - c)
        for t in range(n):
            for j, chip in enumerate(chips):
                blk = outs[t].at[2 * chip[0] + chip[1], rows(t, c)]
                _remote(blk, blk, send.at[6 * t + j], recv.at[6 * t + j], (*chip, c)).wait_recv()
                _remote(blk, blk, send.at[6 * t + 3 + j], recv.at[6 * t + 3 + j], sib).start()
        for t in range(n):
            for j, chip in enumerate(chips):
                blk = outs[t].at[2 * chip[0] + chip[1], rows(t, 1 - c)]
                _remote(blk, blk, send.at[6 * t + 3 + j], recv.at[6 * t + 3 + j], sib).wait_recv()
        for t in range(n):
            mine = outs[t].at[s, rows(t, c)]
            for j, chip in enumerate(chips):
                _remote(mine, mine, send.at[6 * t + j], recv.at[6 * t + j], (*chip, c)).wait_send()
                blk = outs[t].at[2 * chip[0] + chip[1], rows(t, c)]
                _remote(blk, blk, send.at[6 * t + 3 + j], recv.at[6 * t + 3 + j], sib).wait_send()

    shapes = [jax.ShapeDtypeStruct(w.shape, w.dtype) for w in slots]
    return Comm(slots, shapes, {t: t for t in range(n)}, 6 * n, start, finish)


def _scatter_comm(parts, landed, chunks):
    n = len(parts)
    prev = [t for t in range(n) if landed[t] is not None]

    def rows(t):
        lo, hi, nch = chunks[t]
        rc = parts[t].shape[1] // nch
        return pl.ds(lo * rc, (hi - lo) * rc)

    def start(args, outs, send, recv):
        x, y, c, chips = _place()
        s = 2 * x + y
        for t in range(n):
            for j, chip in enumerate(chips):
                _remote(args[t].at[2 * chip[0] + chip[1], rows(t)], outs[t].at[s, rows(t)],
                        send.at[3 * t + j], recv.at[3 * t + j], (*chip, c)).start()

    def finish(args, outs, send, recv):
        x, y, c, chips = _place()
        for t in range(n):
            for j, chip in enumerate(chips):
                blk = outs[t].at[2 * chip[0] + chip[1], rows(t)]
                _remote(blk, blk, send.at[3 * t + j], recv.at[3 * t + j], (*chip, c)).wait_recv()
        for t in range(n):
            for j, chip in enumerate(chips):
                src = args[t].at[2 * chip[0] + chip[1], rows(t)]
                _remote(src, src, send.at[3 * t + j], recv.at[3 * t + j], (*chip, c)).wait_send()

    shapes = [jax.ShapeDtypeStruct(p.shape, p.dtype) for p in parts]
    return Comm(list(parts) + [landed[t] for t in prev], shapes, {n + i: t for i, t in enumerate(prev)}, 3 * n, start, finish)


def _swap_comm(gs):
    n = len(gs)

    def copy(args, outs, send, recv, t):
        _, _, c, _ = _place()
        rh = gs[t].shape[1] // 2
        x, y = lax.axis_index("x"), lax.axis_index("y")
        return _remote(args[t].at[:, pl.ds((1 - c) * rh, rh), :], outs[t], send.at[t], recv.at[t], (x, y, 1 - c))

    def start(args, outs, send, recv):
        for t in range(n):
            copy(args, outs, send, recv, t).start()

    def finish(args, outs, send, recv):
        for t in range(n):
            copy(args, outs, send, recv, t).wait()

    shapes = [jax.ShapeDtypeStruct((g.shape[0], g.shape[1] // 2, g.shape[2]), g.dtype) for g in gs]
    return Comm(gs, shapes, {}, n, start, finish)


def _join_comm(gs):
    n = len(gs)

    def half(outs, t, who):
        rh = gs[t].shape[0] // 2
        return outs[t].at[pl.ds(who * rh, rh), :]

    def start(args, outs, send, recv):
        x, y, c, _ = _place()
        for t in range(n):
            _remote(half(outs, t, c), half(outs, t, c), send.at[t], recv.at[t], (x, y, 1 - c)).start()

    def finish(args, outs, send, recv):
        x, y, c, _ = _place()
        for t in range(n):
            _remote(half(outs, t, 1 - c), half(outs, t, 1 - c), send.at[t], recv.at[t], (x, y, 1 - c)).wait_recv()
        for t in range(n):
            _remote(half(outs, t, c), half(outs, t, c), send.at[t], recv.at[t], (x, y, 1 - c)).wait_send()

    shapes = [jax.ShapeDtypeStruct(g.shape, g.dtype) for g in gs]
    return Comm(gs, shapes, {t: t for t in range(n)}, n, start, finish)


_DN = {"nn": (((1,), (0,)), ((), ())), "nt": (((1,), (1,)), ((), ())), "tn": (((0,), (0,)), ((), ()))}


def _mm(a, b, *, mode, out_dtype, name, out_groups=1, tm=1024, tn=1024, tk=2048, comm=None):
    ga, ra, ca = a.shape
    gb, rb, cb = b.shape
    if mode == "nn":
        M, K, N = ra, ga * ca, gb * cb
        assert rb == K and ga == 1 or (rb == K)
    elif mode == "nt":
        M, K, N = ra, ga * ca, rb
        assert gb * cb == K
    else:
        K, M, N = ra, ga * ca, gb * cb
        assert rb == K
    go = out_groups
    if mode == "nn":
        tk = _tile(ca, tk); assert rb % tk == 0 and (ga == 1 or True)
        tn = _tile(min(cb, N // go), tn); tm = _tile(M, tm, 8)
    elif mode == "nt":
        tk = _tile(ca, tk); tk = _tile(cb, tk) if cb % tk else tk; assert ca % tk == 0 and cb % tk == 0
        tn = _tile(N // go, tn); tm = _tile(M, tm, 8)
    else:
        tk = _tile(K, tk, 8); tm = _tile(ca, tm); tn = _tile(min(cb, N // go), tn)
    assert (N // go) % tn == 0 and M % tm == 0 and K % tk == 0, (name, M, N, K, tm, tn, tk)
    nk = K // tk
    kpa = max(ca // tk, 1)
    kpb = max(cb // tk, 1)
    npb = max(cb // tn, 1)
    npo = (N // go) // tn
    mpa = max(ca // tm, 1)

    if mode == "nn":
        a_spec = pl.BlockSpec((1, tm, tk), lambda j, i, k: (k // kpa, i, k % kpa))
        b_spec = pl.BlockSpec((1, tk, tn), lambda j, i, k: (j // npb, k, j % npb))
    elif mode == "nt":
        a_spec = pl.BlockSpec((1, tm, tk), lambda j, i, k: (k // kpa, i, k % kpa))
        b_spec = pl.BlockSpec((1, tn, tk), lambda j, i, k: (k // kpb, j, k % kpb))
    else:
        a_spec = pl.BlockSpec((1, tk, tm), lambda j, i, k: (i // mpa, k, i % mpa))
        b_spec = pl.BlockSpec((1, tk, tn), lambda j, i, k: (j // npb, k, j % npb))
    o_spec = pl.BlockSpec((1, tm, tn), lambda j, i, k: (j // npo, i, j % npo))
    dn = _DN[mode]

    def body(a_ref, b_ref, o_ref, *acc):
        p = lax.dot_general(a_ref[0], b_ref[0], dn, preferred_element_type=F32)
        if nk == 1:
            o_ref[0] = p.astype(out_dtype)
        else:
            k = pl.program_id(2)

            @pl.when(k == 0)
            def _():
                acc[0][...] = p

            @pl.when(k > 0)
            def _():
                acc[0][...] += p

            @pl.when(k == nk - 1)
            def _():
                o_ref[0] = acc[0][...].astype(out_dtype)

    return _call(
        body, name=name, grid=(N // tn, M // tm, nk), in_specs=[a_spec, b_spec], out_specs=[o_spec],
        out_shape=[jax.ShapeDtypeStruct((go, M, N // go), out_dtype)],
        scratch_shapes=[pltpu.VMEM((tm, tn), F32)] if nk > 1 else [],
        sem=("parallel", "parallel", "arbitrary"), args=(a, b), comm=comm)[0]


def _rows(tr, d):
    return pl.BlockSpec((tr, d), lambda i: (i, 0))


def _vec(d):
    return pl.BlockSpec((1, d), lambda i: (0, 0))


def _modulate(x, sc, sh, name):
    S, D = x.shape
    tr = min(256, S)

    def body(x_ref, sc_ref, sh_ref, h_ref):
        h_ref[...] = (x_ref[...] * (1.0 + sc_ref[...]) + sh_ref[...]).astype(BF16)

    return pl.pallas_call(
        body, name=name, grid=(S // tr,), in_specs=[_rows(tr, D), _vec(D), _vec(D)], out_specs=_rows(tr, D),
        out_shape=jax.ShapeDtypeStruct((S, D), BF16), compiler_params=_params(("parallel",)),
    )(x, sc, sh)


def _ln_fwd(x, y, gate, gamma, beta, sc, sh, name, comm=None):
    S, D = x.shape
    tr = min(256, S)
    emit_h = sc is not None

    def body(*refs):
        if emit_h:
            x_ref, y_ref, g_ref, ga_ref, be_ref, sc_ref, sh_ref, xo_ref, xh_ref, rs_ref, h_ref = refs
        else:
            x_ref, y_ref, g_ref, ga_ref, be_ref, xo_ref, xh_ref, rs_ref = refs
        z = DEEPNORM_ALPHA * x_ref[...] + (1.0 + g_ref[...]) * y_ref[...]
        mu = jnp.mean(z, axis=-1, keepdims=True)
        zc = z - mu
        var = jnp.mean(zc * zc, axis=-1, keepdims=True)
        rstd = lax.rsqrt(var + LN_EPS)
        xh = zc * rstd
        xo = xh * ga_ref[...] + be_ref[...]
        xo_ref[...] = xo
        xh_ref[...] = xh
        rs_ref[...] = rstd
        if emit_h:
            h_ref[...] = (xo * (1.0 + sc_ref[...]) + sh_ref[...]).astype(BF16)

    ins = [x, y, gate, gamma, beta] + ([sc, sh] if emit_h else [])
    in_specs = [_rows(tr, D), _rows(tr, D)] + [_vec(D)] * (len(ins) - 2)
    out_shape = [jax.ShapeDtypeStruct((S, D), F32), jax.ShapeDtypeStruct((S, D), F32), jax.ShapeDtypeStruct((S, 1), F32)]
    out_specs = [_rows(tr, D), _rows(tr, D), _rows(tr, 1)]
    if emit_h:
        out_shape.append(jax.ShapeDtypeStruct((S, D), BF16))
        out_specs.append(_rows(tr, D))
    return _call(body, name=name, grid=(S // tr,), in_specs=in_specs, out_specs=out_specs, out_shape=out_shape,
                 sem=("parallel",), args=ins, comm=comm)


def _loss_head(xf, tgt, name):
    S, D = xf.shape
    tr = min(256, S)

    def body(x_ref, t_ref, dx_ref, l_ref):
        e = x_ref[...] - t_ref[...]
        dx_ref[...] = e * (1.0 / D)

        @pl.when(pl.program_id(0) == 0)
        def _():
            l_ref[...] = jnp.zeros_like(l_ref)

        l_ref[...] += jnp.sum(e * e, axis=0, keepdims=True)

    return pl.pallas_call(
        body, name=name, grid=(S // tr,), in_specs=[_rows(tr, D), _rows(tr, D)],
        out_specs=[_rows(tr, D), _vec(D)],
        out_shape=[jax.ShapeDtypeStruct((S, D), F32), jax.ShapeDtypeStruct((1, D), F32)],
        compiler_params=_params(("arbitrary",)),
    )(xf, tgt)


def _ln_bwd(dxo, xh, rstd, gamma, y, gate, name):
    S, D = dxo.shape
    tr = min(256, S)

    def body(dx_ref, xh_ref, rs_ref, ga_ref, y_ref, g_ref, dres_ref, dy_ref, dga_ref, dbe_ref, dg_ref):
        dxo_ = dx_ref[...]
        xh_ = xh_ref[...]
        dxh = dxo_ * ga_ref[...]
        m1 = jnp.mean(dxh, axis=-1, keepdims=True)
        m2 = jnp.mean(dxh * xh_, axis=-1, keepdims=True)
        dz = rs_ref[...] * (dxh - m1 - xh_ * m2)
        dres_ref[...] = DEEPNORM_ALPHA * dz
        dy_ref[...] = ((1.0 + g_ref[...]) * dz).astype(BF16)

        @pl.when(pl.program_id(0) == 0)
        def _():
            dga_ref[...] = jnp.zeros_like(dga_ref)
            dbe_ref[...] = jnp.zeros_like(dbe_ref)
            dg_ref[...] = jnp.zeros_like(dg_ref)

        dga_ref[...] += jnp.sum(dxo_ * xh_, axis=0, keepdims=True)
        dbe_ref[...] += jnp.sum(dxo_, axis=0, keepdims=True)
        dg_ref[...] += jnp.sum(dz * y_ref[...], axis=0, keepdims=True)

    return pl.pallas_call(
        body, name=name, grid=(S // tr,),
        in_specs=[_rows(tr, D), _rows(tr, D), _rows(tr, 1), _vec(D), _rows(tr, D), _vec(D)],
        out_specs=[_rows(tr, D), _rows(tr, D), _vec(D), _vec(D), _vec(D)],
        out_shape=[jax.ShapeDtypeStruct((S, D), F32), jax.ShapeDtypeStruct((S, D), BF16)] + [jax.ShapeDtypeStruct((1, D), F32)] * 3,
        compiler_params=_params(("arbitrary",)),
    )(dxo, xh, rstd, gamma, y, gate)


def _mod_bwd(dh, x, sc, dres, name):
    S, D = x.shape
    tr = min(256, S)

    def body(dh_ref, x_ref, sc_ref, dr_ref, dx_ref, dsc_ref, dsh_ref):
        dh_ = dh_ref[...]
        dx_ref[...] = dr_ref[...] + dh_ * (1.0 + sc_ref[...])

        @pl.when(pl.program_id(0) == 0)
        def _():
            dsc_ref[...] = jnp.zeros_like(dsc_ref)
            dsh_ref[...] = jnp.zeros_like(dsh_ref)

        dsc_ref[...] += jnp.sum(dh_ * x_ref[...], axis=0, keepdims=True)
        dsh_ref[...] += jnp.sum(dh_, axis=0, keepdims=True)

    return pl.pallas_call(
        body, name=name, grid=(S // tr,),
        in_specs=[_rows(tr, D), _rows(tr, D), _vec(D), _rows(tr, D)],
        out_specs=[_rows(tr, D), _vec(D), _vec(D)],
        out_shape=[jax.ShapeDtypeStruct((S, D), F32), jax.ShapeDtypeStruct((1, D), F32), jax.ShapeDtypeStruct((1, D), F32)],
        compiler_params=_params(("arbitrary",)),
    )(dh, x, sc, dres)


def _log_sigmoid(z):
    return jnp.minimum(z, 0.0) - jnp.log(1.0 + jnp.exp(-jnp.abs(z)))


def _fox_gate_fwd(proj, b_f, n_heads, name):
    S, PW = proj.shape
    blk = min(256, S)
    last = PW // LANES - 1

    def body(fl_ref, b_ref, cum_ref):
        r = lax.broadcasted_iota(jnp.int32, (blk, blk), 0)
        c = lax.broadcasted_iota(jnp.int32, (blk, blk), 1)
        tril = (c <= r).astype(F32)
        carry = jnp.zeros((1, LANES), F32)
        for i in range(S // blk):
            lf = _log_sigmoid(fl_ref[i * blk:(i + 1) * blk, :] + b_ref[...])
            cum_ref[i * blk:(i + 1) * blk, :] = jnp.dot(tril, lf, preferred_element_type=F32, precision=HIGHEST) + carry
            carry = carry + jnp.sum(lf, axis=0, keepdims=True)

    return pl.pallas_call(
        body, name=name, grid=(1,),
        in_specs=[pl.BlockSpec((S, LANES), lambda i: (0, last)), pl.BlockSpec((1, LANES), lambda i: (0, 0))],
        out_specs=pl.BlockSpec((S, LANES), lambda i: (0, 0)),
        out_shape=jax.ShapeDtypeStruct((S, LANES), F32), compiler_params=_params(("arbitrary",)),
    )(proj, b_f)


def _fox_gate_bwd(dcum, proj, b_f, n_heads, name):
    S, PW = proj.shape
    blk = min(256, S)
    last = PW // LANES - 1
    nb = S // blk

    def body(dc_ref, fl_ref, b_ref, dfl_ref, db_ref):
        r = lax.broadcasted_iota(jnp.int32, (blk, blk), 0)
        c = lax.broadcasted_iota(jnp.int32, (blk, blk), 1)
        triu = (c >= r).astype(F32)
        lane = lax.broadcasted_iota(jnp.int32, (blk, LANES), 1)
        carry = jnp.zeros((1, LANES), F32)
        dbs = jnp.zeros((1, LANES), F32)
        for i in reversed(range(nb)):
            dc = dc_ref[i * blk:(i + 1) * blk, :]
            dlf = jnp.dot(triu, dc, preferred_element_type=F32, precision=HIGHEST) + carry
            carry = carry + jnp.sum(dc, axis=0, keepdims=True)
            z = fl_ref[i * blk:(i + 1) * blk, :] + b_ref[...]
            e = jnp.exp(-jnp.abs(z))
            sig_neg = jnp.where(z >= 0, e / (1.0 + e), 1.0 / (1.0 + e))
            dfl = jnp.where(lane < n_heads, dlf * sig_neg, 0.0)
            dfl_ref[i * blk:(i + 1) * blk, :] = dfl.astype(BF16)
            dbs = dbs + jnp.sum(dfl, axis=0, keepdims=True)
        db_ref[...] = dbs

    return pl.pallas_call(
        body, name=name, grid=(1,),
        in_specs=[pl.BlockSpec((S, LANES), lambda i: (0, 0)), pl.BlockSpec((S, LANES), lambda i: (0, last)),
                  pl.BlockSpec((1, LANES), lambda i: (0, 0))],
        out_specs=[pl.BlockSpec((S, LANES), lambda i: (0, 0)), pl.BlockSpec((1, LANES), lambda i: (0, 0))],
        out_shape=[jax.ShapeDtypeStruct((S, LANES), BF16), jax.ShapeDtypeStruct((1, LANES), F32)],
        compiler_params=_params(("arbitrary",)),
    )(dcum, proj, b_f)


def _fox_scores(q_ref, kb_ref, cq_ref, ck_ref, qi, tq, scale):
    kk = (qi + 1) * tq
    rows = slice(qi * tq, (qi + 1) * tq)
    qb = q_ref[rows, :].astype(BF16)
    s = lax.dot_general(qb, kb_ref[0:kk, :], _DN["nt"], preferred_element_type=F32) * scale
    s = s + (cq_ref[0, rows, :] - ck_ref[0, :, 0:kk])
    r = lax.broadcasted_iota(jnp.int32, (tq, kk), 0) + qi * tq
    c = lax.broadcasted_iota(jnp.int32, (tq, kk), 1)
    mask = c <= r
    return jnp.where(mask, s, NEG), mask, qb


def _fox_fwd(proj, cq, ck, n_heads, name, comm=None):
    S = proj.shape[0]
    H = n_heads
    tq = min(FOX_TQ, S)
    nq = S // tq
    scale = FDH ** -0.5

    def body(q_ref, k_ref, v_ref, cq_ref, ck_ref, o_ref, lse_ref, kb_ref, vb_ref):
        kb_ref[...] = k_ref[...].astype(BF16)
        vb_ref[...] = v_ref[...].astype(BF16)
        for qi in range(nq):
            kk = (qi + 1) * tq
            rows = slice(qi * tq, (qi + 1) * tq)
            s, _, _ = _fox_scores(q_ref, kb_ref, cq_ref, ck_ref, qi, tq, scale)
            m = jnp.max(s, axis=-1, keepdims=True)
            p = jnp.exp(s - m)
            l = jnp.sum(p, axis=-1, keepdims=True)
            p = p * (1.0 / l)
            o_ref[rows, :] = jnp.dot(p.astype(BF16), vb_ref[0:kk, :], preferred_element_type=F32).astype(BF16)
            lse_ref[0, rows, :] = m + jnp.log(l)

    col = lambda off: pl.BlockSpec((S, FDH), lambda h: (0, h + off))
    stat_c = pl.BlockSpec((1, S, 1), lambda h: (h, 0, 0))
    stat_r = pl.BlockSpec((1, 1, S), lambda h: (h, 0, 0))
    return _call(
        body, name=name, grid=(H,),
        in_specs=[col(0), col(H), col(2 * H), stat_c, stat_r],
        out_specs=[col(0), stat_c],
        out_shape=[jax.ShapeDtypeStruct((S, H * FDH), BF16), jax.ShapeDtypeStruct((H, S, 1), F32)],
        scratch_shapes=[pltpu.VMEM((S, FDH), BF16), pltpu.VMEM((S, FDH), BF16)],
        sem=("parallel",), args=(proj, proj, proj, cq, ck), comm=comm)


def _fox_bwd(proj, cq, ck, lse, do, n_heads, name, comm=None):
    S = proj.shape[0]
    H = n_heads
    tq = min(FOX_TQ, S)
    nq = S // tq
    scale = FDH ** -0.5

    def body(q_ref, k_ref, v_ref, cq_ref, ck_ref, lse_ref, do_ref, dq_ref, dk_ref, dv_ref, dcq_ref, dck_ref,
             kb_ref, vb_ref, dka_ref, dva_ref):
        kb_ref[...] = k_ref[...].astype(BF16)
        vb_ref[...] = v_ref[...].astype(BF16)
        dka_ref[...] = jnp.zeros_like(dka_ref)
        dva_ref[...] = jnp.zeros_like(dva_ref)
        dck_ref[...] = jnp.zeros_like(dck_ref)
        for qi in range(nq):
            kk = (qi + 1) * tq
            rows = slice(qi * tq, (qi + 1) * tq)
            s, mask, qb = _fox_scores(q_ref, kb_ref, cq_ref, ck_ref, qi, tq, scale)
            p = jnp.where(mask, jnp.exp(s - lse_ref[0, rows, :]), 0.0)
            dob = do_ref[rows, :]
            dp = lax.dot_general(dob, vb_ref[0:kk, :], _DN["nt"], preferred_element_type=F32)
            delta = jnp.sum(p * dp, axis=-1, keepdims=True)
            ds = p * (dp - delta)
            dcq_ref[0, rows, :] = jnp.sum(ds, axis=-1, keepdims=True)
            dck_ref[0, :, 0:kk] -= jnp.sum(ds, axis=0, keepdims=True)
            dsb = (ds * scale).astype(BF16)
            dq_ref[rows, :] = jnp.dot(dsb, kb_ref[0:kk, :], preferred_element_type=F32).astype(BF16)
            dka_ref[0:kk, :] += lax.dot_general(dsb, qb, _DN["tn"], preferred_element_type=F32)
            dva_ref[0:kk, :] += lax.dot_general(p.astype(BF16), dob, _DN["tn"], preferred_element_type=F32)
        dk_ref[...] = dka_ref[...].astype(BF16)
        dv_ref[...] = dva_ref[...].astype(BF16)

    col = lambda off: pl.BlockSpec((S, FDH), lambda h: (0, h + off))
    stat_c = pl.BlockSpec((1, S, 1), lambda h: (h, 0, 0))
    stat_r = pl.BlockSpec((1, 1, S), lambda h: (h, 0, 0))
    wide = jax.ShapeDtypeStruct((S, H * FDH), BF16)
    return _call(
        body, name=name, grid=(H,),
        in_specs=[col(0), col(H), col(2 * H), stat_c, stat_r, stat_c, col(0)],
        out_specs=[col(0), col(0), col(0), stat_c, stat_r],
        out_shape=[wide, wide, wide, jax.ShapeDtypeStruct((H, S, 1), F32), jax.ShapeDtypeStruct((H, 1, S), F32)],
        scratch_shapes=[pltpu.VMEM((S, FDH), BF16), pltpu.VMEM((S, FDH), BF16), pltpu.VMEM((S, FDH), F32), pltpu.VMEM((S, FDH), F32)],
        sem=("parallel",), args=(proj, proj, proj, cq, ck, lse, do), comm=comm)


def _rope_tables(pos, sign):
    inv = ROPE_THETA ** (-jnp.arange(0, ROPE_DIM, 2, dtype=F32) / ROPE_DIM)
    ang = pos.astype(F32)[:, None] * inv
    cos, sin = jnp.cos(ang), sign * jnp.sin(ang)
    l64 = jnp.arange(LANES) % SDH
    idx = l64 % (ROPE_DIM // 2)
    c = jnp.where(l64 < ROPE_DIM, cos[:, idx], 1.0)
    sa = jnp.where(l64 < ROPE_DIM // 2, -sin[:, idx], 0.0)
    sb = jnp.where((l64 >= ROPE_DIM // 2) & (l64 < ROPE_DIM), sin[:, idx], 0.0)
    rot = jnp.stack([c, sa, sb])
    ident = jnp.stack([jnp.ones_like(c), jnp.zeros_like(c), jnp.zeros_like(c)])
    return jnp.stack([rot, ident]).astype(F32)


def _rope(xin, tabs, n_rot, out_dtype, name, comm=None):
    S, W = xin.shape

    def body(x_ref, t_ref, o_ref):
        xv = x_ref[...]
        o = xv * t_ref[0, 0] + pltpu.roll(xv, LANES - ROPE_DIM // 2, 1) * t_ref[0, 1] + pltpu.roll(xv, ROPE_DIM // 2, 1) * t_ref[0, 2]
        o_ref[...] = o.astype(out_dtype)

    return _call(
        body, name=name, grid=(W // LANES,),
        in_specs=[pl.BlockSpec((S, LANES), lambda j: (0, j)),
                  pl.BlockSpec((1, 3, S, LANES), lambda j: (jnp.where(j < n_rot, 0, 1), 0, 0, 0))],
        out_specs=[pl.BlockSpec((S, LANES), lambda j: (0, j))],
        out_shape=[jax.ShapeDtypeStruct((S, W), out_dtype)], sem=("parallel",), args=(xin, tabs), comm=comm)[0]


SWA_PER_STEP = 2


def _swa_bias():
    r = jnp.arange(WIN)[:, None]
    c = jnp.arange(2 * WIN)[None, :]
    first = c <= r
    later = (c > r) & (c <= r + WIN)
    return jnp.where(jnp.stack([first, later]), 0.0, NEG).astype(F32)


def _swa_probs(q_ref, k_ref, sk_ref, b_ref, n, j, scale):
    st = pl.multiple_of(jnp.maximum(n - 1, 0) * WIN, WIN)
    qb = q_ref[0, j]
    kb = k_ref[0, pl.ds(st, 2 * WIN), :]
    gm = qb.shape[0]
    s = lax.dot_general(qb, kb, _DN["nt"], preferred_element_type=F32) * scale
    s = (s.reshape(gm // WIN, WIN, 2 * WIN) + b_ref[jnp.minimum(n, 1)][None]).reshape(gm, 2 * WIN)
    sink = sk_ref[0]
    m = jnp.maximum(jnp.max(s, axis=-1, keepdims=True), sink)
    e = jnp.exp(s - m)
    es = jnp.exp(sink - m)
    inv = 1.0 / (jnp.sum(e, axis=-1, keepdims=True) + es)
    return e * inv, es * inv, st, qb, kb


def _swa_specs(S, gm):
    blk = pl.BlockSpec((1, SWA_PER_STEP, gm, SDH), lambda g, n: (g, n, 0, 0))
    kv = pl.BlockSpec((1, S, SDH), lambda g, n: (g, 0, 0))
    col = pl.BlockSpec((1, gm, 1), lambda g, n: (g, 0, 0))
    bias = pl.BlockSpec((2, WIN, 2 * WIN), lambda g, n: (0, 0, 0))
    return blk, kv, col, bias


def _swa_fwd(q, k, v, sinks, name, comm=None):
    KH, nb, gm, _ = q.shape
    S = k.shape[1]
    scale = SDH ** -0.5

    def body(q_ref, k_ref, v_ref, sk_ref, b_ref, o_ref):
        for j in range(SWA_PER_STEP):
            p, _, st, _, _ = _swa_probs(q_ref, k_ref, sk_ref, b_ref, pl.program_id(1) * SWA_PER_STEP + j, j, scale)
            vb = v_ref[0, pl.ds(st, 2 * WIN), :]
            o_ref[0, j] = jnp.dot(p.astype(BF16), vb, preferred_element_type=F32).astype(BF16)

    blk, kv, col, bias = _swa_specs(S, gm)
    return _call(
        body, name=name, grid=(KH, nb // SWA_PER_STEP), in_specs=[blk, kv, kv, col, bias], out_specs=[blk],
        out_shape=[jax.ShapeDtypeStruct(q.shape, BF16)], sem=("parallel", "parallel"), args=(q, k, v, sinks, _swa_bias()), comm=comm)[0]


def _swa_bwd(q, k, v, sinks, do, name, comm=None):
    KH, nb, gm, _ = q.shape
    S = k.shape[1]
    scale = SDH ** -0.5

    def body(q_ref, k_ref, v_ref, sk_ref, b_ref, do_ref, dq_ref, dk_ref, dv_ref, dsk_ref):
        @pl.when(pl.program_id(1) == 0)
        def _():
            dk_ref[...] = jnp.zeros_like(dk_ref)
            dv_ref[...] = jnp.zeros_like(dv_ref)
            dsk_ref[...] = jnp.zeros_like(dsk_ref)

        blocks = []
        for j in range(SWA_PER_STEP):
            p, ps, st, qb, kb = _swa_probs(q_ref, k_ref, sk_ref, b_ref, pl.program_id(1) * SWA_PER_STEP + j, j, scale)
            vb = v_ref[0, pl.ds(st, 2 * WIN), :]
            dob = do_ref[0, j]
            dp = lax.dot_general(dob, vb, _DN["nt"], preferred_element_type=F32)
            delta = jnp.sum(p * dp, axis=-1, keepdims=True)
            dsb = (p * (dp - delta) * scale).astype(BF16)
            dq_ref[0, j] = jnp.dot(dsb, kb, preferred_element_type=F32)
            blocks.append((st, lax.dot_general(dsb, qb, _DN["tn"], preferred_element_type=F32),
                           lax.dot_general(p.astype(BF16), dob, _DN["tn"], preferred_element_type=F32), ps * delta))
        for st, dk, dv, dsk in blocks:
            dk_ref[0, pl.ds(st, 2 * WIN), :] += dk
            dv_ref[0, pl.ds(st, 2 * WIN), :] += dv
            dsk_ref[0] -= dsk

    blk, kv, col, bias = _swa_specs(S, gm)
    return _call(
        body, name=name, grid=(KH, nb // SWA_PER_STEP), in_specs=[blk, kv, kv, col, bias, blk], out_specs=[blk, kv, kv, col],
        out_shape=[jax.ShapeDtypeStruct(q.shape, F32), jax.ShapeDtypeStruct(k.shape, F32),
                   jax.ShapeDtypeStruct(k.shape, F32), jax.ShapeDtypeStruct(sinks.shape, F32)],
        sem=("parallel", "arbitrary"), args=(q, k, v, sinks, _swa_bias(), do), comm=comm)


def _shift_down(u, k):
    row = lax.broadcasted_iota(jnp.int32, u.shape, 0)
    return jnp.where(row >= k, pltpu.roll(u, k, 0), 0.0)


def _shift_up(u, k):
    n = u.shape[0]
    row = lax.broadcasted_iota(jnp.int32, u.shape, 0)
    return jnp.where(row < n - k, pltpu.roll(u, n - k, 0), 0.0)


def _conv3(u, w_ref, b_ref):
    return w_ref[0:1, :] * _shift_down(u, 2) + w_ref[1:2, :] * _shift_down(u, 1) + w_ref[2:3, :] * u + b_ref[...]


def _conv_gate(u, cw, cb, name, comm=None):
    S, F2 = u.shape
    Fh = F2 // 2
    tc = _tile(Fh, 256)
    nf = Fh // tc

    def body(ug_ref, uv_ref, wg_ref, wv_ref, bg_ref, bv_ref, a_ref):
        g = _conv3(ug_ref[...], wg_ref, bg_ref)
        val = _conv3(uv_ref[...], wv_ref, bv_ref)
        a_ref[...] = (g * (1.0 / (1.0 + jnp.exp(-g))) * val).astype(BF16)

    blk = lambda r, off: pl.BlockSpec((r, tc), lambda j: (0, j + off))
    return _call(
        body, name=name, grid=(nf,),
        in_specs=[blk(S, 0), blk(S, nf), blk(3, 0), blk(3, nf), blk(1, 0), blk(1, nf)], out_specs=[blk(S, 0)],
        out_shape=[jax.ShapeDtypeStruct((S, Fh), BF16)], sem=("parallel",), args=(u, u, cw, cw, cb, cb), comm=comm)[0]


def _conv_gate_bwd(u, da, cw, cb, name, comm=None):
    S, F2 = u.shape
    Fh = F2 // 2
    tc = _tile(Fh, 256)
    nf = Fh // tc

    def half(h, dx, uu, w_ref, du_ref, dw_ref, db_ref):
        up1, up2 = _shift_up(dx, 1), _shift_up(dx, 2)
        du = w_ref[2:3, :] * dx + w_ref[1:2, :] * up1 + w_ref[0:1, :] * up2
        du_ref[h] = du.astype(BF16)
        dw_ref[h, 0:1, :] = jnp.sum(up2 * uu, axis=0, keepdims=True)
        dw_ref[h, 1:2, :] = jnp.sum(up1 * uu, axis=0, keepdims=True)
        dw_ref[h, 2:3, :] = jnp.sum(dx * uu, axis=0, keepdims=True)
        db_ref[h] = jnp.sum(dx, axis=0, keepdims=True)

    def body(ug_ref, uv_ref, da_ref, wg_ref, wv_ref, bg_ref, bv_ref, du_ref, dw_ref, db_ref):
        ug = ug_ref[...]
        uv = uv_ref[...]
        g = _conv3(ug, wg_ref, bg_ref)
        val = _conv3(uv, wv_ref, bv_ref)
        sig = 1.0 / (1.0 + jnp.exp(-g))
        da_ = da_ref[...]
        dg = da_ * val * (sig * (1.0 + g * (1.0 - sig)))
        dval = da_ * (g * sig)
        half(0, dg, ug, wg_ref, du_ref, dw_ref, db_ref)
        half(1, dval, uv, wv_ref, du_ref, dw_ref, db_ref)

    blk = lambda r, off: pl.BlockSpec((r, tc), lambda j: (0, j + off))
    both = lambda r: pl.BlockSpec((2, r, tc), lambda j: (0, 0, j))
    return _call(
        body, name=name, grid=(nf,),
        in_specs=[blk(S, 0), blk(S, nf), blk(S, 0), blk(3, 0), blk(3, nf), blk(1, 0), blk(1, nf)],
        out_specs=[both(S), both(3), both(1)],
        out_shape=[jax.ShapeDtypeStruct((2, S, Fh), BF16), jax.ShapeDtypeStruct((2, 3, Fh), F32), jax.ShapeDtypeStruct((2, 1, Fh), F32)],
        sem=("parallel",), args=(u, u, da, cw, cw, cb, cb), comm=comm)


def _to_groups(t, kh):
    S, width = t.shape
    g = width // SDH // kh
    return t.reshape(S // WIN, WIN, kh, g, SDH).transpose(2, 0, 3, 1, 4).reshape(kh, S // WIN, g * WIN, SDH)


def _from_groups(t):
    kh, nb, gm, _ = t.shape
    g = gm // WIN
    return t.reshape(kh, nb, g, WIN, SDH).transpose(1, 3, 0, 2, 4).reshape(nb * WIN, kh * g * SDH)


class LocalWeights:
    def __init__(self, weights):
        self.weights, self.grads = weights, {}

    def w(self, name):
        return self.weights[name]

    def carry(self, stage):
        return None

    def carried(self, stage, comm):
        pass

    def grad(self, name, g):
        self.grads[name] = g


def _local_step(dm, x, tgt, pos, mod, sp, pp):
    S, D, FH, QH, KH, Fh = dm
    m = [[mod[i:i + 1, j * D:(j + 1) * D] for j in range(6)] for i in range(DEPTH)]

    def run(fn, *args, name, **kw):
        comm = pp.carry(name)
        out = fn(*args, name=name, comm=comm, **kw)
        if comm is not None:
            pp.carried(name, comm)
        return out

    sv = []
    xs = x
    h = _modulate(xs, m[0][1], m[0][0], "mod_in")
    for i in range(DEPTH):
        sh1, sc1, g1, sh2, sc2, g2 = m[i]
        L = {}
        L["x_in"], L["h1"] = xs, h
        if i == 0:
            proj = run(_mm, h[None], pp.w("fox_w_in"), mode="nn", out_dtype=F32, name="fox_proj", tn=896)[0]
            cum = _fox_gate_fwd(proj, sp["fox_b_f"], FH, "fox_gate")
            cq = cum[:, :FH].T[:, :, None]
            ck = cum[:, :FH].T[:, None, :]
            o, lse = run(_fox_fwd, proj, cq, ck, FH, name="fox_attn")
            L.update(proj=proj, cq=cq, ck=ck, lse=lse, o=o)
            y = run(_mm, o[None], pp.w("fox_w_o"), mode="nn", out_dtype=F32, name="fox_out")[0]
        else:
            proj = run(_mm, h[None], pp.w("swa_w_in"), mode="nn", out_dtype=F32, name="swa_proj", tn=640)[0]
            tabs = _rope_tables(pos, 1.0)
            n_rot = (QH + KH) * SDH // LANES
            pr = run(_rope, proj, tabs, n_rot, BF16, name="swa_rope")
            qh = _to_groups(pr[:, :QH * SDH], KH)
            kh = pr[:, QH * SDH:(QH + KH) * SDH].reshape(S, KH, SDH).transpose(1, 0, 2)
            vh = pr[:, (QH + KH) * SDH:].reshape(S, KH, SDH).transpose(1, 0, 2)
            oh = run(_swa_fwd, qh, kh, vh, sp["sinks"], name="swa_attn")
            o = _from_groups(oh)
            L.update(qh=qh, kh=kh, vh=vh, o=o)
            y = run(_mm, o[None], pp.w("swa_w_o"), mode="nn", out_dtype=F32, name="swa_out")[0]
        L["y1"] = y
        x1, L["xh1"], L["rs1"], h2 = run(_ln_fwd, xs, y, g1, sp["ln_mix_g"][i], sp["ln_mix_b"][i], sc2, sh2, name=f"ln_mix{i}")
        L["x1"], L["h2"] = x1, h2
        u = run(_mm, h2[None], pp.w(f"ffn_w_up{i}"), mode="nn", out_dtype=F32, name=f"ffn_up{i}", tm=512, tn=1408)[0]
        a = run(_conv_gate, u, sp["conv_w"][i], sp["conv_b"][i], name=f"ffn_gate{i}")
        y2 = run(_mm, a[None], pp.w(f"ffn_w_down{i}"), mode="nn", out_dtype=F32, name=f"ffn_down{i}", tk=1408)[0]
        L.update(u=u, a=a, y2=y2)
        if i + 1 < DEPTH:
            xs, L["xh2"], L["rs2"], h = run(_ln_fwd, x1, y2, g2, sp["ln_ffn_g"][i], sp["ln_ffn_b"][i], m[i + 1][1], m[i + 1][0], name=f"ln_ffn{i}")
        else:
            xs, L["xh2"], L["rs2"] = run(_ln_fwd, x1, y2, g2, sp["ln_ffn_g"][i], sp["ln_ffn_b"][i], None, None, name=f"ln_ffn{i}")
        sv.append(L)

    dx, loss_cols = _loss_head(xs, tgt, "loss_head")

    gs = {k: [None] * DEPTH for k in ("conv_w", "conv_b", "ln_mix_g", "ln_mix_b", "ln_ffn_g", "ln_ffn_b")}
    dmod = [None] * DEPTH
    for i in reversed(range(DEPTH)):
        sh1, sc1, g1, sh2, sc2, g2 = m[i]
        L = sv[i]
        dres, dy, gs["ln_ffn_g"][i], gs["ln_ffn_b"][i], dg2 = _ln_bwd(dx, L["xh2"], L["rs2"], sp["ln_ffn_g"][i], L["y2"], g2, f"ln_ffn_bwd{i}")
        da = run(_mm, dy[None], pp.w(f"ffn_w_down{i}"), mode="nt", out_dtype=F32, name=f"ffn_down_dx{i}", tm=512, tn=1408)[0]
        pp.grad(f"ffn_w_down{i}", run(_mm, L["a"][None], dy[None], mode="tn", out_dtype=BF16, name=f"ffn_down_dw{i}", tm=1408))
        du, dcw, dcb = run(_conv_gate_bwd, L["u"], da, sp["conv_w"][i], sp["conv_b"][i], name=f"ffn_gate_bwd{i}")
        gs["conv_w"][i] = dcw.transpose(1, 0, 2).reshape(3, 2 * Fh)
        gs["conv_b"][i] = dcb.transpose(1, 0, 2).reshape(1, 2 * Fh)
        dh2 = run(_mm, du, pp.w(f"ffn_w_up{i}"), mode="nt", out_dtype=F32, name=f"ffn_up_dx{i}", tk=1408)[0]
        pp.grad(f"ffn_w_up{i}", run(_mm, L["h2"][None], du, mode="tn", out_dtype=BF16, name=f"ffn_up_dw{i}", out_groups=N_CHIPS, tn=1408))
        dx, dsc2, dsh2 = _mod_bwd(dh2, L["x1"], sc2, dres, f"mod_ffn_bwd{i}")
        dres, dy, gs["ln_mix_g"][i], gs["ln_mix_b"][i], dg1 = _ln_bwd(dx, L["xh1"], L["rs1"], sp["ln_mix_g"][i], L["y1"], g1, f"ln_mix_bwd{i}")
        if i == 0:
            do = run(_mm, dy[None], pp.w("fox_w_o"), mode="nt", out_dtype=BF16, name="fox_out_dx")[0]
            pp.grad("fox_w_o", run(_mm, L["o"][None], dy[None], mode="tn", out_dtype=BF16, name="fox_out_dw"))
            dq, dk, dv, dcq, dck = run(_fox_bwd, L["proj"], L["cq"], L["ck"], L["lse"], do, FH, name="fox_attn_bwd")
            dcum = dcq[:, :, 0].T + dck[:, 0, :].T
            dcum = jnp.pad(dcum, ((0, 0), (0, LANES - FH)))
            dfl, db_f = _fox_gate_bwd(dcum, L["proj"], sp["fox_b_f"], FH, "fox_gate_bwd")
            gs["fox_b_f"] = db_f
            dproj = jnp.concatenate([dq, dk, dv, dfl], axis=1)
            pp.grad("fox_w_in", run(_mm, L["h1"][None], dproj[None], mode="tn", out_dtype=BF16, name="fox_proj_dw", tn=896))
            dh1 = run(_mm, dproj[None], pp.w("fox_w_in"), mode="nt", out_dtype=F32, name="fox_proj_dx", tk=896)[0]
        else:
            do = run(_mm, dy[None], pp.w("swa_w_o"), mode="nt", out_dtype=BF16, name="swa_out_dx")[0]
            pp.grad("swa_w_o", run(_mm, L["o"][None], dy[None], mode="tn", out_dtype=BF16, name="swa_out_dw"))
            dqh, dkh, dvh, dsk = run(_swa_bwd, L["qh"], L["kh"], L["vh"], sp["sinks"], _to_groups(do, KH), name="swa_attn_bwd")
            gs["sinks"] = jnp.sum(dsk.reshape(QH, WIN), axis=1)
            dpr = jnp.concatenate([_from_groups(dqh), dkh.transpose(1, 0, 2).reshape(S, KH * SDH),
                                   dvh.transpose(1, 0, 2).reshape(S, KH * SDH)], axis=1)
            n_rot = (QH + KH) * SDH // LANES
            dproj = _rope(dpr, _rope_tables(pos, -1.0), n_rot, BF16, "swa_rope_bwd")
            dh1 = run(_mm, dproj[None], pp.w("swa_w_in"), mode="nt", out_dtype=F32, name="swa_proj_dx", tk=640)[0]
            pp.grad("swa_w_in", run(_mm, L["h1"][None], dproj[None], mode="tn", out_dtype=BF16, name="swa_proj_dw", out_groups=N_CHIPS, tn=640))
        dx, dsc1, dsh1 = _mod_bwd(dh1, L["x_in"], sc1, dres, f"mod_mix_bwd{i}")
        dmod[i] = jnp.concatenate([dsh1, dsc1, dg1, dsh2, dsc2, dg2], axis=1)
    return loss_cols, dx, gs, jnp.concatenate(dmod, axis=0)


def _allgather_small(v, name):
    m_per, n = v.shape

    def body(x_ref, out_ref, send_sems, recv_sems, local_sem):
        x, y, c, chips = _place()
        me, sibling = (x, y, c), (x, y, 1 - c)

        def rows(px, py, pc):
            return out_ref.at[pl.ds((4 * px + 2 * py + pc) * m_per, m_per), :]

        def copy(k, block, to, src=None):
            return _remote(rows(*block) if src is None else src, rows(*block), send_sems.at[k], recv_sems.at[k], to)

        mine = pltpu.make_async_copy(x_ref, rows(*me), local_sem)
        mine.start()
        first = [copy(0, me, sibling, src=x_ref)]
        first += [copy(1 + j, me, (*chip, c), src=x_ref) for j, chip in enumerate(chips)]
        for cp in first:
            cp.start()
        passed = [copy(4 + j, (*chip, c), sibling) for j, chip in enumerate(chips)]
        for j, chip in enumerate(chips):
            copy(1 + j, (*chip, c), me).wait_recv()
            passed[j].start()
        copy(0, sibling, me).wait_recv()
        for j, chip in enumerate(chips):
            copy(4 + j, (*chip, 1 - c), me).wait_recv()
        for cp in first + passed:
            cp.wait_send()
        mine.wait()

    return pl.pallas_call(
        body, name=name, out_shape=jax.ShapeDtypeStruct((N_DEV * m_per, n), v.dtype),
        in_specs=[pl.BlockSpec(memory_space=pltpu.VMEM)], out_specs=pl.BlockSpec(memory_space=pltpu.VMEM),
        scratch_shapes=[pltpu.SemaphoreType.DMA((7,)), pltpu.SemaphoreType.DMA((7,)), pltpu.SemaphoreType.DMA],
        compiler_params=pltpu.CompilerParams(vmem_limit_bytes=VMEM_LIMIT),
    )(v)


def _row_tile(r, pref=256):
    return _tile(r, pref, 16)


def _cast_bf16(w, layer, chip, name):
    _, R, C = w.shape
    tr = _row_tile(R)

    def body(s_ref, w_ref, o_ref):
        o_ref[...] = w_ref[...].astype(BF16)

    return pl.pallas_call(
        body, name=name,
        grid_spec=pltpu.PrefetchScalarGridSpec(
            num_scalar_prefetch=1, grid=(R // tr,),
            in_specs=[pl.BlockSpec((None, tr, C), lambda i, s: (layer, i, 0))],
            out_specs=pl.BlockSpec((None, tr, C), lambda i, s: (s[0], i, 0))),
        out_shape=jax.ShapeDtypeStruct((N_CHIPS, R, C), BF16), compiler_params=_params(("parallel",)),
    )(jnp.reshape(chip, (1,)).astype(jnp.int32), w)


def _add_sibling(g, got, c, name):
    G, R, C = g.shape
    rh = R // 2
    tr = _row_tile(rh)
    nb = rh // tr

    def body(c_ref, g_ref, o_ref, p_ref):
        p_ref[...] = (g_ref[...].astype(F32) + o_ref[...].astype(F32)).astype(BF16)

    return pl.pallas_call(
        body, name=name,
        grid_spec=pltpu.PrefetchScalarGridSpec(
            num_scalar_prefetch=1, grid=(G, nb),
            in_specs=[pl.BlockSpec((1, tr, C), lambda s, i, c_ref: (s, c_ref[0] * nb + i, 0)),
                      pl.BlockSpec((1, tr, C), lambda s, i, c_ref: (s, i, 0))],
            out_specs=pl.BlockSpec((1, tr, C), lambda s, i, c_ref: (s, i, 0))),
        out_shape=jax.ShapeDtypeStruct((G, rh, C), BF16), compiler_params=_params(("parallel", "parallel")),
    )(jnp.reshape(c, (1,)).astype(jnp.int32), g, got)


def _sum_chips(part, landed, chip, c, name):
    G, rh, C = part.shape
    tr = _row_tile(rh)
    nb = rh // tr

    def body(p_ref, own_ref, *rest):
        acc = own_ref[...].astype(F32)
        for ref in rest[:G - 1]:
            acc = acc + ref[...].astype(F32)
        rest[G - 1][...] = acc

    slot = lambda k: pl.BlockSpec((None, tr, C), lambda i, p: ((p[0] + k) % G, i, 0))
    return pl.pallas_call(
        body, name=name,
        grid_spec=pltpu.PrefetchScalarGridSpec(
            num_scalar_prefetch=1, grid=(nb,), in_specs=[slot(k) for k in range(G)],
            out_specs=pl.BlockSpec((tr, C), lambda i, p: (p[1] * nb + i, 0))),
        out_shape=jax.ShapeDtypeStruct((2 * rh, C), F32), compiler_params=_params(("parallel",)),
    )(jnp.stack([chip, c]).astype(jnp.int32), part, *([landed] * (G - 1)))


def _adam_math(w, g, m, v):
    m = ADAM_B1 * m + (1.0 - ADAM_B1) * g
    v = ADAM_B2 * v + (1.0 - ADAM_B2) * (g * g)
    m_hat = m / (1.0 - ADAM_B1 ** ADAM_STEP)
    v_hat = v / (1.0 - ADAM_B2 ** ADAM_STEP)
    delta = -ADAM_LR * (m_hat / (jnp.sqrt(v_hat) + ADAM_EPS) + ADAM_WD * w)
    return delta, m, v


def _adamw(w, g, m, v, layer, prev, name, by_cols=False):
    L, R, C = w.shape
    tr = R if by_cols else _tile(R, 128, 8)
    tc = _tile(C, 256) if by_cols else C
    n_prev = len(prev)

    def body(w_ref, g_ref, m_ref, v_ref, *rest):
        go_ref, d_ref, mo_ref, vo_ref = rest[n_prev:]
        gv = g_ref[...]
        go_ref[...] = gv
        d_ref[...], mo_ref[...], vo_ref[...] = _adam_math(w_ref[...], gv, m_ref[...], v_ref[...])

    lay = pl.BlockSpec((None, tr, tc), lambda i: (layer, i // (C // tc), i % (C // tc)))
    flat = pl.BlockSpec((tr, tc), lambda i: (i // (C // tc), i % (C // tc)))
    return _call(
        body, name=name, grid=((R // tr) * (C // tc),), in_specs=[lay, flat, lay, lay] + _any_specs(n_prev), out_specs=[lay] * 4,
        out_shape=[jax.ShapeDtypeStruct((L, R, C), F32)] * 4, aliases={4 + k: k for k in range(n_prev)},
        sem=("parallel",), args=(w, g, m, v, *prev))


def _cond_rows(c_row, cw, name):
    D = c_row.shape[1]
    nr, fc = cw.shape

    def body(c_ref, e_ref, o_ref):
        o_ref[...] = jnp.zeros_like(o_ref)
        cv = c_ref[...]
        o_ref[0:1, 0:D] = cv * (1.0 / (1.0 + jnp.exp(-cv)))
        o_ref[8:8 + nr, 0:fc] = e_ref[...]

    return pl.pallas_call(body, name=name, out_shape=jax.ShapeDtypeStruct((16, max(D, fc)), F32))(c_row, cw)


def _ada_fwd(cact, ada_w, ada_b, layer, chip, name):
    _, D, NC = ada_w.shape
    tn = _tile(NC, 1024)
    nj = NC // tn

    def body(idx_ref, c_ref, w_ref, b_ref, o_ref):
        acc = jnp.dot(c_ref[...].astype(BF16), w_ref[0].astype(BF16), preferred_element_type=F32)
        o_ref[...] = acc + b_ref[pl.ds(idx_ref[0], 1), :]

    return pl.pallas_call(
        body, name=name,
        grid_spec=pltpu.PrefetchScalarGridSpec(
            num_scalar_prefetch=1, grid=(nj,),
            in_specs=[pl.BlockSpec((8, D), lambda j, idx: (0, 0)),
                      pl.BlockSpec((1, D, tn), lambda j, idx: (idx[0], 0, j)),
                      pl.BlockSpec((DEPTH, tn), lambda j, idx: (0, idx[1] * nj + j))],
            out_specs=pl.BlockSpec((8, tn), lambda j, idx: (0, j))),
        out_shape=jax.ShapeDtypeStruct((8, NC), F32), compiler_params=_params(("parallel",)),
    )(jnp.stack([layer, chip]).astype(jnp.int32), cact, ada_w, ada_b)


def _ada_grad_adamw(cact_t, dmod, w, m, v, name, comm=None):
    L, D, NC = w.shape
    tr = _tile(D, 128, 8)

    def body(c_ref, d_ref, w_ref, m_ref, v_ref, g_ref, dl_ref, mo_ref, vo_ref):
        g = jnp.dot(c_ref[...], d_ref[...], preferred_element_type=F32, precision=HIGHEST)
        g_ref[...] = g
        dl_ref[...], mo_ref[...], vo_ref[...] = _adam_math(w_ref[...], g, m_ref[...], v_ref[...])

    lay = pl.BlockSpec((None, tr, NC), lambda l, i: (l, i, 0))
    return _call(
        body, name=name, grid=(L, D // tr),
        in_specs=[pl.BlockSpec((tr, N_DEV), lambda l, i: (i, 0)), pl.BlockSpec((None, N_DEV, NC), lambda l, i: (l, 0, 0)), lay, lay, lay],
        out_specs=[lay] * 4, out_shape=[jax.ShapeDtypeStruct((L, D, NC), F32)] * 4,
        sem=("parallel", "parallel"), args=(cact_t, dmod, w, m, v), comm=comm)


def _sum_devices(gathered, name):
    n, R, C = gathered.shape

    def body(g_ref, o_ref):
        acc = g_ref[0]
        for j in range(1, n):
            acc = acc + g_ref[j]
        o_ref[...] = acc

    return pl.pallas_call(body, name=name, out_shape=jax.ShapeDtypeStruct((R, C), F32),
                          compiler_params=pltpu.CompilerParams(vmem_limit_bytes=VMEM_LIMIT))(gathered)


def _adamw_small(w, g, m, v, name):
    def body(w_ref, g_ref, m_ref, v_ref, d_ref, mo_ref, vo_ref):
        d_ref[...], mo_ref[...], vo_ref[...] = _adam_math(w_ref[...], g_ref[...], m_ref[...], v_ref[...])

    return pl.pallas_call(body, name=name, out_shape=[jax.ShapeDtypeStruct(w.shape, F32)] * 3)(w, g, m, v)


def _pad_rows(flat, unit=8 * LANES):
    n = flat.shape[0]
    total = -(-n // unit) * unit
    return jnp.pad(flat, (0, total - n)).reshape(total // LANES, LANES)


def _pad_lanes(v2d):
    return jnp.pad(v2d.reshape(1, -1), ((0, 0), (0, LANES - v2d.size)))


PLAN = {
    "fox_proj": [("gather", "ffn_w_up0", 0, 2, 8)],
    "fox_attn": [("gather", "ffn_w_up0", 2, 6, 8)],
    "fox_out": [("gather", "ffn_w_up0", 6, 7, 8)],
    "ln_mix0": [("gather", "ffn_w_up0", 7, 8, 8)],
    "ffn_up0": [("gather", "ffn_w_down0", 0, 1, 1)],
    "ffn_gate0": [("gather", "swa_w_in", 0, 1, 1)],
    "ffn_down0": [("gather", "swa_w_o", 0, 1, 1), ("gather", "ffn_w_up1", 0, 1, 8)],
    "ln_ffn0": [("gather", "ffn_w_up1", 1, 2, 8)],
    "swa_proj": [("gather", "ffn_w_up1", 2, 3, 8)],
    "swa_rope": [("gather", "ffn_w_up1", 3, 4, 8)],
    "swa_attn": [("gather", "ffn_w_up1", 4, 8, 8)],
    "ffn_up1": [("gather", "ffn_w_down1", 0, 1, 1)],
    "ffn_gate_bwd1": [("swap", "ffn_w_down1")],
    "ffn_up_dx1": [("scatter", "ffn_w_down1", 0, 1, 1)],
    "swa_out_dx": [("swap", "ffn_w_up1")],
    "swa_attn_bwd": [("scatter", "ffn_w_up1", 0, 1, 1), ("swap", "swa_w_o")],
    "swa_proj_dx": [("scatter", "swa_w_o", 0, 1, 1)],
    "ffn_down_dx0": [("swap", "swa_w_in")],
    "ffn_down_dw0": [("scatter", "swa_w_in", 0, 1, 1)],
    "ffn_gate_bwd0": [("swap", "ffn_w_down0")],
    "ffn_up_dx0": [("scatter", "ffn_w_down0", 0, 1, 1)],
    "fox_out_dx": [("swap", "ffn_w_up0")],
    "fox_attn_bwd": [("scatter", "ffn_w_up0", 0, 1, 2), ("swap", "fox_w_o")],
    "fox_proj_dw": [("scatter", "fox_w_o", 0, 1, 1)],
    "fox_proj_dx": [("scatter", "ffn_w_up0", 1, 2, 2), ("swap", "fox_w_in")],
}


class Exchanges:
    def __init__(self, dm, slots, chip, c):
        self.dm, self.slots, self.chip, self.c = dm, dict(slots), chip, c
        self.raw, self.part, self.landed, self.grads, self.views, self.pending = {}, {}, {}, {}, {}, {}

    def gather_now(self, keys, name):
        comm = _gather_comm([self.slots[k] for k in keys], [(0, 1, 1)] * len(keys))
        _run_comm(comm, name)
        self.slots.update(zip(keys, comm.results))

    def w(self, key):
        if key not in self.views:
            S, D, FH, QH, KH, Fh = self.dm
            full = self.slots[key]
            if key == "fox_w_in":
                cols = full.shape[2]
                full = jnp.pad(full.transpose(1, 0, 2).reshape(D, N_CHIPS * cols), ((0, 0), (0, 3 * D + LANES - N_CHIPS * cols)))[None]
            elif key in ("fox_w_o", "swa_w_o"):
                full = full.reshape(1, D, D)
            elif key.startswith("ffn_w_down"):
                full = full.reshape(1, Fh, D)
            self.views[key] = full
        return self.views[key]

    def carry(self, stage):
        todo = []
        for kind, key, *chunk in PLAN.get(stage, ()):
            if kind == "gather":
                todo.append((kind, [key], _gather_comm([self.slots[key]], [tuple(chunk)])))
            elif kind == "swap":
                todo.append((kind, [key], _swap_comm([self.raw[key]])))
            elif kind == "scatter":
                todo.append((kind, [(key, *chunk)], _scatter_comm([self.part[key]], [self.landed.get(key)], [tuple(chunk)])))
        self.pending[stage] = todo
        return _merge([cm for _, _, cm in todo])

    def carried(self, stage, comm):
        for kind, keys, cm in self.pending.pop(stage):
            if kind == "gather":
                self.slots[keys[0]] = cm.results[0]
            elif kind == "swap":
                self.part[keys[0]] = _add_sibling(self.raw[keys[0]], cm.results[0], self.c, f"add_sibling_{keys[0]}")
            else:
                self.landed[keys[0][0]] = cm.results[0]

    def grad(self, key, g):
        S, D, FH, QH, KH, Fh = self.dm
        if key == "fox_w_in":
            cols = self.slots[key].shape[2]
            g = g[0][:, :N_CHIPS * cols].reshape(D, N_CHIPS, cols).transpose(1, 0, 2)
        elif key in ("fox_w_o", "swa_w_o"):
            g = g.reshape(N_CHIPS, D // N_CHIPS, D)
        elif key.startswith("ffn_w_down"):
            g = g.reshape(N_CHIPS, Fh // N_CHIPS, D)
        self.raw[key] = g

    def finish(self):
        last = "fox_w_in"
        keys = list(self.landed)
        halves = [_sum_chips(self.part[k], self.landed[k], self.chip, self.c, f"sum_chips_{k}") for k in keys]
        send, join = _scatter_comm([self.part[last]], [None], [(0, 1, 1)]), _join_comm(halves)
        _run_comm(_merge([send, join]), "grads_tail")
        grads = dict(zip(keys, join.results))
        join = _join_comm([_sum_chips(self.part[last], send.results[0], self.chip, self.c, f"sum_chips_{last}")])
        _run_comm(join, "grads_join_last")
        grads[last] = join.results[0]
        return grads


def _step(dm, a):
    S, D, FH, QH, KH, Fh = dm
    ix, iy, ic = lax.axis_index("x"), lax.axis_index("y"), lax.axis_index("c")
    chip = 2 * ix + iy
    dev = 2 * chip + ic
    F2c = a["ffn_w_up"].shape[2]
    NC = a["ada_w"].shape[2]
    PW = 3 * D + LANES
    fox_cols = a["fox_w_in"].shape[2]

    e0 = _cond_rows(a["c"], a["ffn_conv_w"].reshape(DEPTH * 3, F2c), "silu_c")
    g0 = _allgather_small(e0, "gather_cond").reshape(N_DEV, 16, e0.shape[1])
    cact = g0[:, 0, :D]
    conv_w = g0[0::2, 8:8 + DEPTH * 3, :F2c].transpose(1, 0, 2).reshape(DEPTH, 3, N_CHIPS * F2c)
    rows = _ada_fwd(cact, a["ada_w"], a["ada_b"], ic, chip, "ada_proj")
    g1 = _allgather_small(rows, "gather_mod").reshape(N_CHIPS, DEPTH, 8, NC)
    mod = lax.dynamic_index_in_dim(g1, dev, axis=2, keepdims=False).transpose(1, 0, 2).reshape(DEPTH, N_CHIPS * NC)

    names = ["fox_w_in", "fox_w_o", "swa_w_in", "swa_w_o", "ffn_w_up", "ffn_w_up", "ffn_w_down", "ffn_w_down"]
    layers = [0, 0, 0, 0, 0, 1, 0, 1]
    keys = ["fox_w_in", "fox_w_o", "swa_w_in", "swa_w_o", "ffn_w_up0", "ffn_w_up1", "ffn_w_down0", "ffn_w_down1"]
    slots = {k: _cast_bf16(a[nm], l, chip, f"cast_{k}") for k, nm, l in zip(keys, names, layers)}
    pp = Exchanges(dm, slots, chip, ic)
    pp.gather_now(["fox_w_in", "fox_w_o"], "gather_fox")
    sp = {"fox_b_f": _pad_lanes(a["fox_b_f"]), "sinks": jnp.repeat(a["swa_sinks"].reshape(KH, QH // KH), WIN, axis=1)[:, :, None],
          "conv_w": [conv_w[i] for i in range(DEPTH)], "conv_b": [a["ffn_conv_b"][i:i + 1] for i in range(DEPTH)]}
    for nm in ("ln_mix_g", "ln_mix_b", "ln_ffn_g", "ln_ffn_b"):
        sp[nm] = [a[nm][i:i + 1] for i in range(DEPTH)]

    loss_cols, grad_x, gs, dmod = _local_step(dm, a["x"][0], a["loss_target"][0], a["positions"][0], mod, sp, pp)
    loss = lax.psum(0.5 / D * jnp.sum(loss_cols), ("x", "y", "c"))
    out = {"loss": loss, "grad_x": grad_x[None]}

    def run(fn, *args, name, **kw):
        comm = pp.carry(name)
        res = fn(*args, name=name, comm=comm, **kw)
        if comm is not None:
            pp.carried(name, comm)
        return res

    pieces = [dmod.reshape(-1), gs["fox_b_f"].reshape(-1), _pad_lanes(gs["sinks"]).reshape(-1),
              jnp.stack(gs["conv_w"]).reshape(-1), jnp.stack(gs["conv_b"]).reshape(-1)]
    pieces += [jnp.stack(gs[nm]).reshape(-1) for nm in ("ln_mix_g", "ln_mix_b", "ln_ffn_g", "ln_ffn_b")]
    sizes = [p.shape[0] for p in pieces]
    packed = _pad_rows(jnp.concatenate(pieces))
    allp = _allgather_small(packed, "gather_small").reshape(N_DEV, packed.shape[0], LANES)
    tot = _sum_devices(allp, "sum_small").reshape(-1)
    offs = [sum(sizes[:k]) for k in range(len(sizes))]
    take = lambda k: tot[offs[k]:offs[k] + sizes[k]]
    g_small = {"ada_b": take(0).reshape(DEPTH, -1), "fox_b_f": take(1)[:FH].reshape(1, FH), "swa_sinks": take(2)[:QH].reshape(1, QH),
               "ffn_conv_w": lax.dynamic_slice_in_dim(take(3).reshape(DEPTH, 3, N_CHIPS * F2c), chip * F2c, F2c, axis=2),
               "ffn_conv_b": take(4).reshape(DEPTH, -1)}
    for k, nm in enumerate(("ln_mix_g", "ln_mix_b", "ln_ffn_g", "ln_ffn_b")):
        g_small[nm] = take(5 + k).reshape(DEPTH, D)
    small = list(g_small)
    pack = lambda pre: _pad_rows(jnp.concatenate([(a[pre + nm] if pre else a[nm]).reshape(-1) for nm in small]))
    gp = _pad_rows(jnp.concatenate([g_small[nm].reshape(-1) for nm in small]))
    ds_, ms_, vs_ = _adamw_small(pack(""), gp, pack("m_"), pack("v_"), "adamw_small")
    off = 0
    for nm in small:
        n_el = a[nm].size
        out["grad_" + nm] = g_small[nm]
        for pre, arr in (("delta_", ds_), ("new_m_", ms_), ("new_v_", vs_)):
            out[pre + nm] = arr.reshape(-1)[off:off + n_el].reshape(a[nm].shape)
        off += n_el

    dmod_all = allp.reshape(N_DEV, -1)[:, :DEPTH * N_CHIPS * NC].reshape(N_DEV, DEPTH, N_CHIPS * NC)
    dmod_mine = lax.dynamic_slice_in_dim(dmod_all, chip * NC, NC, axis=2).transpose(1, 0, 2)
    ada = _ada_grad_adamw(cact.T, dmod_mine, a["ada_w"], a["m_ada_w"], a["v_ada_w"], "ada_grad")
    for pre, arr in zip(("grad_", "delta_", "new_m_", "new_v_"), ada):
        out[pre + "ada_w"] = arr

    grads = pp.finish()
    upd = {}
    for k, nm, l in zip(keys[1:], names[1:], layers[1:]):
        upd[nm] = _adamw(a[nm], grads[k], a["m_" + nm], a["v_" + nm], l, upd.get(nm, ()), f"adamw_{k}")
    tview = lambda t: jnp.swapaxes(t, 1, 2)
    res = _adamw(tview(a["fox_w_in"]), grads["fox_w_in"].T, tview(a["m_fox_w_in"]), tview(a["v_fox_w_in"]), 0, (), "adamw_fox_w_in", by_cols=True)
    upd["fox_w_in"] = [tview(r) for r in res]
    for nm, res in upd.items():
        for pre, arr in zip(("grad_", "delta_", "new_m_", "new_v_"), res):
            out[pre + nm] = arr
    return out


_WEIGHTS = ["fox_w_in", "fox_b_f", "fox_w_o", "swa_w_in", "swa_sinks", "swa_w_o", "ada_w", "ada_b", "ffn_w_up", "ffn_conv_w",
            "ffn_conv_b", "ffn_w_down", "ln_mix_g", "ln_mix_b", "ln_ffn_g", "ln_ffn_b"]
_INPUTS = (["x", "c", "positions"] + _WEIGHTS + ["loss_target"] + ["m_" + w for w in _WEIGHTS] + ["v_" + w for w in _WEIGHTS])


def kernel(x, c, positions, fox_w_in, fox_b_f, fox_w_o, swa_w_in, swa_sinks, swa_w_o, ada_w, ada_b, ffn_w_up, ffn_conv_w, ffn_conv_b, ffn_w_down, ln_mix_g, ln_mix_b, ln_ffn_g, ln_ffn_b, loss_target, m_fox_w_in, m_fox_b_f, m_fox_w_o, m_swa_w_in, m_swa_sinks, m_swa_w_o, m_ada_w, m_ada_b, m_ffn_w_up, m_ffn_conv_w, m_ffn_conv_b, m_ffn_w_down, m_ln_mix_g, m_ln_mix_b, m_ln_ffn_g, m_ln_ffn_b, v_fox_w_in, v_fox_b_f, v_fox_w_o, v_swa_w_in, v_swa_sinks, v_swa_w_o, v_ada_w, v_ada_b, v_ffn_w_up, v_ffn_conv_w, v_ffn_conv_b, v_ffn_w_down, v_ln_mix_g, v_ln_mix_b, v_ln_ffn_g, v_ln_ffn_b):
    args = (x, c, positions, fox_w_in, fox_b_f, fox_w_o, swa_w_in, swa_sinks, swa_w_o, ada_w, ada_b, ffn_w_up, ffn_conv_w, ffn_conv_b, ffn_w_down, ln_mix_g, ln_mix_b, ln_ffn_g, ln_ffn_b, loss_target, m_fox_w_in, m_fox_b_f, m_fox_w_o, m_swa_w_in, m_swa_sinks, m_swa_w_o, m_ada_w, m_ada_b, m_ffn_w_up, m_ffn_conv_w, m_ffn_conv_b, m_ffn_w_down, m_ln_mix_g, m_ln_mix_b, m_ln_ffn_g, m_ln_ffn_b, v_fox_w_in, v_fox_b_f, v_fox_w_o, v_swa_w_in, v_swa_sinks, v_swa_w_o, v_ada_w, v_ada_b, v_ffn_w_up, v_ffn_conv_w, v_ffn_conv_b, v_ffn_w_down, v_ln_mix_g, v_ln_mix_b, v_ln_ffn_g, v_ln_ffn_b)
    out = _step(PROD, dict(zip(_INPUTS, args)))
    order = ["loss", "grad_x"] + [p + w for p in ("grad_", "delta_", "new_m_", "new_v_") for w in _WEIGHTS]
    return tuple(out[k] for k in order)
```

```python
import functools
from typing import NamedTuple

import jax
import jax.numpy as jnp
from jax import lax
from jax.experimental import pallas as pl
from jax.experimental.pallas import tpu as pltpu

F32 = jnp.float32
BF16 = jnp.bfloat16
MESH = pl.DeviceIdType.MESH
HIGHEST = lax.Precision.HIGHEST

N_CHIPS = 4
N_DEV = 8
LANES = 128
VMEM_LIMIT = 56 * 1024 * 1024

DEPTH = 2
DEEPNORM_ALPHA = (2.0 * DEPTH) ** 0.25
LN_EPS = 1e-5
ROPE_THETA = 500000.0
ADAM_LR, ADAM_B1, ADAM_B2, ADAM_EPS, ADAM_WD, ADAM_STEP = 0.001, 0.9, 0.999, 1e-08, 0.01, 10
NEG = -1e30


class Dims(NamedTuple):
    S: int
    D: int
    FH: int
    QH: int
    KH: int
    F: int


PROD = Dims(S=2048, D=2048, FH=16, QH=32, KH=4, F=5632)
FDH = 128
SDH = 64
WIN = 128
ROPE_DIM = 16
FOX_TQ = 256


def _params(sem=None, vmem=VMEM_LIMIT):
    return pltpu.CompilerParams(dimension_semantics=sem, vmem_limit_bytes=vmem)


def _tile(n, pref, unit=LANES):
    if n <= pref:
        return n
    t = (pref // unit) * unit
    while t > 0:
        if n % t == 0:
            return t
        t -= unit
    return n


class Comm:
    def __init__(self, args, out_shapes, aliases, n_sem, start, finish, members=()):
        self.args, self.out_shapes, self.aliases, self.n_sem = list(args), list(out_shapes), dict(aliases), n_sem
        self.start, self.finish = start, finish
        self.members = members
        self.results = None

    def set_results(self, res):
        self.results = list(res)
        for cm, o0 in self.members:
            cm.set_results(self.results[o0:o0 + len(cm.out_shapes)])


class _SemView:
    def __init__(self, sems, first):
        self.sems, self.first = sems, first

    @property
    def at(self):
        return self

    def __getitem__(self, k):
        return self.sems.at[self.first + k]


def _merge(comms):
    comms = [cm for cm in comms if cm is not None]
    if len(comms) < 2:
        return comms[0] if comms else None
    args, shapes, aliases, spans, n_sem = [], [], {}, [], 0
    for cm in comms:
        spans.append((len(args), len(shapes), n_sem))
        aliases.update({len(args) + a: len(shapes) + o for a, o in cm.aliases.items()})
        args += cm.args
        shapes += cm.out_shapes
        n_sem += cm.n_sem

    def each(step):
        def run(ar, ou, send, recv):
            for cm, (a0, o0, s0) in zip(comms, spans):
                getattr(cm, step)(ar[a0:a0 + len(cm.args)], ou[o0:o0 + len(cm.out_shapes)], _SemView(send, s0), _SemView(recv, s0))
        return run

    return Comm(args, shapes, aliases, n_sem, each("start"), each("finish"), [(cm, o0) for cm, (_, o0, _) in zip(comms, spans)])


def _place():
    x, y, c = lax.axis_index("x"), lax.axis_index("y"), lax.axis_index("c")
    chips = [(1 - x, y), (x, 1 - y), (1 - x, 1 - y)]
    return x, y, c, chips


def _remote(src, dst, send, recv, to):
    return pltpu.make_async_remote_copy(src_ref=src, dst_ref=dst, send_sem=send, recv_sem=recv, device_id=to, device_id_type=MESH)


def _any_specs(n):
    return [pl.BlockSpec(memory_space=pl.ANY)] * n


def _call(body, *, name, grid, in_specs, out_specs, out_shape, args, sem, scratch_shapes=(), aliases=None, comm=None):
    in_specs, out_specs, out_shape, scratch_shapes = list(in_specs), list(out_specs), list(out_shape), list(scratch_shapes)
    aliases = dict(aliases or {})
    if comm is None:
        return pl.pallas_call(body, name=name, grid=grid, in_specs=in_specs, out_specs=out_specs, out_shape=out_shape,
                              scratch_shapes=scratch_shapes, input_output_aliases=aliases, compiler_params=_params(sem))(*args)
    n_in, n_out, nc_in, nc_out, n_scr = len(in_specs), len(out_specs), len(comm.args), len(comm.out_shapes), len(scratch_shapes)

    def wrapped(*refs):
        ins, refs = refs[:n_in], refs[n_in:]
        cin, refs = refs[:nc_in], refs[nc_in:]
        outs, refs = refs[:n_out], refs[n_out:]
        cout, refs = refs[:nc_out], refs[nc_out:]
        scratch, (send, recv) = refs[:n_scr], refs[n_scr:]
        ids = [pl.program_id(k) for k in range(len(grid))]
        first = functools.reduce(jnp.logical_and, [i == 0 for i in ids])
        last = functools.reduce(jnp.logical_and, [i == g - 1 for i, g in zip(ids, grid)])

        @pl.when(first)
        def _():
            comm.start(cin, cout, send, recv)

        body(*ins, *outs, *scratch)

        @pl.when(last)
        def _():
            comm.finish(cin, cout, send, recv)

    res = pl.pallas_call(
        wrapped, name=name, grid=grid, in_specs=in_specs + _any_specs(nc_in), out_specs=out_specs + _any_specs(nc_out),
        out_shape=out_shape + comm.out_shapes,
        scratch_shapes=scratch_shapes + [pltpu.SemaphoreType.DMA((comm.n_sem,)), pltpu.SemaphoreType.DMA((comm.n_sem,))],
        input_output_aliases={**aliases, **{n_in + a: n_out + o for a, o in comm.aliases.items()}},
        compiler_params=_params(("arbitrary",) * len(grid)),
    )(*args, *comm.args)
    comm.set_results(res[n_out:])
    return list(res[:n_out])


def _run_comm(comm, name):
    nc_in, nc_out = len(comm.args), len(comm.out_shapes)

    def body(*refs):
        cin, cout, (send, recv) = refs[:nc_in], refs[nc_in:nc_in + nc_out], refs[nc_in + nc_out:]
        comm.start(cin, cout, send, recv)
        comm.finish(cin, cout, send, recv)

    res = pl.pallas_call(
        body, name=name, in_specs=_any_specs(nc_in), out_specs=_any_specs(nc_out), out_shape=comm.out_shapes,
        scratch_shapes=[pltpu.SemaphoreType.DMA((comm.n_sem,)), pltpu.SemaphoreType.DMA((comm.n_sem,))],
        input_output_aliases=comm.aliases,
    )(*comm.args)
    comm.set_results(res)


def _gather_comm(slots, chunks):
    n = len(slots)

    def rows(t, who):
        rh = slots[t].shape[1] // 2
        lo, hi, nch = chunks[t]
        rc = rh // nch
        return pl.ds(who * rh + lo * rc, (hi - lo) * rc)

    def start(args, outs, send, recv):
        x, y, c, chips = _place()
        s = 2 * x + y
        for t in range(n):
            mine = outs[t].at[s, rows(t, c)]
            for j, chip in enumerate(chips):
                _remote(mine, mine, send.at[6 * t + j], recv.at[6 * t + j], (*chip, c)).start()

    def finish(args, outs, send, recv):
        x, y, c, chips = _place()
        s = 2 * x + y
        sib = (x, y, 1 - c)
        for t in range(n):
            for j, chip in enumerate(chips):
                blk = outs[t].at[2 * chip[0] + chip[1], rows(t, c)]
                _remote(blk, blk, send.at[6 * t + j], recv.at[6 * t + j], (*chip, c)).wait_recv()
                _remote(blk, blk, send.at[6 * t + 3 + j], recv.at[6 * t + 3 + j], sib).start()
        for t in range(n):
            for j, chip in enumerate(chips):
                blk = outs[t].at[2 * chip[0] + chip[1], rows(t, 1 - c)]
                _remote(blk, blk, send.at[6 * t + 3 + j], recv.at[6 * t + 3 + j], sib).wait_recv()
        for t in range(n):
            mine = outs[t].at[s, rows(t, c)]
            for j, chip in enumerate(chips):
                _remote(mine, mine, send.at[6 * t + j], recv.at[6 * t + j], (*chip, c)).wait_send()
                blk = outs[t].at[2 * chip[0] + chip[1], rows(t, c)]
                _remote(blk, blk, send.at[6 * t + 3 + j], recv.at[6 * t + 3 + j], sib).wait_send()

    shapes = [jax.ShapeDtypeStruct(w.shape, w.dtype) for w in slots]
    return Comm(slots, shapes, {t: t for t in range(n)}, 6 * n, start, finish)


def _scatter_comm(parts, landed, chunks):
    n = len(parts)
    prev = [t for t in range(n) if landed[t] is not None]

    def rows(t):
        lo, hi, nch = chunks[t]
        rc = parts[t].shape[1] // nch
        return pl.ds(lo * rc, (hi - lo) * rc)

    def start(args, outs, send, recv):
        x, y, c, chips = _place()
        s = 2 * x + y
        for t in range(n):
            for j, chip in enumerate(chips):
                _remote(args[t].at[2 * chip[0] + chip[1], rows(t)], outs[t].at[s, rows(t)],
                        send.at[3 * t + j], recv.at[3 * t + j], (*chip, c)).start()

    def finish(args, outs, send, recv):
        x, y, c, chips = _place()
        for t in range(n):
            for j, chip in enumerate(chips):
                blk = outs[t].at[2 * chip[0] + chip[1], rows(t)]
                _remote(blk, blk, send.at[3 * t + j], recv.at[3 * t + j], (*chip, c)).wait_recv()
        for t in range(n):
            for j, chip in enumerate(chips):
                src = args[t].at[2 * chip[0] + chip[1], rows(t)]
                _remote(src, src, send.at[3 * t + j], recv.at[3 * t + j], (*chip, c)).wait_send()

    shapes = [jax.ShapeDtypeStruct(p.shape, p.dtype) for p in parts]
    return Comm(list(parts) + [landed[t] for t in prev], shapes, {n + i: t for i, t in enumerate(prev)}, 3 * n, start, finish)


def _swap_comm(gs):
    n = len(gs)

    def copy(args, outs, send, recv, t):
        _, _, c, _ = _place()
        rh = gs[t].shape[1] // 2
        x, y = lax.axis_index("x"), lax.axis_index("y")
        return _remote(args[t].at[:, pl.ds((1 - c) * rh, rh), :], outs[t], send.at[t], recv.at[t], (x, y, 1 - c))

    def start(args, outs, send, recv):
        for t in range(n):
            copy(args, outs, send, recv, t).start()

    def finish(args, outs, send, recv):
        for t in range(n):
            copy(args, outs, send, recv, t).wait()

    shapes = [jax.ShapeDtypeStruct((g.shape[0], g.shape[1] // 2, g.shape[2]), g.dtype) for g in gs]
    return Comm(gs, shapes, {}, n, start, finish)


def _join_comm(gs):
    n = len(gs)

    def half(outs, t, who):
        rh = gs[t].shape[0] // 2
        return outs[t].at[pl.ds(who * rh, rh), :]

    def start(args, outs, send, recv):
        x, y, c, _ = _place()
        for t in range(n):
            _remote(half(outs, t, c), half(outs, t, c), send.at[t], recv.at[t], (x, y, 1 - c)).start()

    def finish(args, outs, send, recv):
        x, y, c, _ = _place()
        for t in range(n):
            _remote(half(outs, t, 1 - c), half(outs, t, 1 - c), send.at[t], recv.at[t], (x, y, 1 - c)).wait_recv()
        for t in range(n):
            _remote(half(outs, t, c), half(outs, t, c), send.at[t], recv.at[t], (x, y, 1 - c)).wait_send()

    shapes = [jax.ShapeDtypeStruct(g.shape, g.dtype) for g in gs]
    return Comm(gs, shapes, {t: t for t in range(n)}, n, start, finish)


_DN = {"nn": (((1,), (0,)), ((), ())), "nt": (((1,), (1,)), ((), ())), "tn": (((0,), (0,)), ((), ()))}


def _mm(a, b, *, mode, out_dtype, name, out_groups=1, tm=1024, tn=1024, tk=2048, comm=None):
    ga, ra, ca = a.shape
    gb, rb, cb = b.shape
    if mode == "nn":
        M, K, N = ra, ga * ca, gb * cb
        assert rb == K and ga == 1 or (rb == K)
    elif mode == "nt":
        M, K, N = ra, ga * ca, rb
        assert gb * cb == K
    else:
        K, M, N = ra, ga * ca, gb * cb
        assert rb == K
    go = out_groups
    if mode == "nn":
        tk = _tile(ca, tk); assert rb % tk == 0 and (ga == 1 or True)
        tn = _tile(min(cb, N // go), tn); tm = _tile(M, tm, 8)
    elif mode == "nt":
        tk = _tile(ca, tk); tk = _tile(cb, tk) if cb % tk else tk; assert ca % tk == 0 and cb % tk == 0
        tn = _tile(N // go, tn); tm = _tile(M, tm, 8)
    else:
        tk = _tile(K, tk, 8); tm = _tile(ca, tm); tn = _tile(min(cb, N // go), tn)
    assert (N // go) % tn == 0 and M % tm == 0 and K % tk == 0, (name, M, N, K, tm, tn, tk)
    nk = K // tk
    kpa = max(ca // tk, 1)
    kpb = max(cb // tk, 1)
    npb = max(cb // tn, 1)
    npo = (N // go) // tn
    mpa = max(ca // tm, 1)

    if mode == "nn":
        a_spec = pl.BlockSpec((1, tm, tk), lambda j, i, k: (k // kpa, i, k % kpa))
        b_spec = pl.BlockSpec((1, tk, tn), lambda j, i, k: (j // npb, k, j % npb))
    elif mode == "nt":
        a_spec = pl.BlockSpec((1, tm, tk), lambda j, i, k: (k // kpa, i, k % kpa))
        b_spec = pl.BlockSpec((1, tn, tk), lambda j, i, k: (k // kpb, j, k % kpb))
    else:
        a_spec = pl.BlockSpec((1, tk, tm), lambda j, i, k: (i // mpa, k, i % mpa))
        b_spec = pl.BlockSpec((1, tk, tn), lambda j, i, k: (j // npb, k, j % npb))
    o_spec = pl.BlockSpec((1, tm, tn), lambda j, i, k: (j // npo, i, j % npo))
    dn = _DN[mode]

    def body(a_ref, b_ref, o_ref, *acc):
        p = lax.dot_general(a_ref[0], b_ref[0], dn, preferred_element_type=F32)
        if nk == 1:
            o_ref[0] = p.astype(out_dtype)
        else:
            k = pl.program_id(2)

            @pl.when(k == 0)
            def _():
                acc[0][...] = p

            @pl.when(k > 0)
            def _():
                acc[0][...] += p

            @pl.when(k == nk - 1)
            def _():
                o_ref[0] = acc[0][...].astype(out_dtype)

    return _call(
        body, name=name, grid=(N // tn, M // tm, nk), in_specs=[a_spec, b_spec], out_specs=[o_spec],
        out_shape=[jax.ShapeDtypeStruct((go, M, N // go), out_dtype)],
        scratch_shapes=[pltpu.VMEM((tm, tn), F32)] if nk > 1 else [],
        sem=("parallel", "parallel", "arbitrary"), args=(a, b), comm=comm)[0]


def _rows(tr, d):
    return pl.BlockSpec((tr, d), lambda i: (i, 0))


def _vec(d):
    return pl.BlockSpec((1, d), lambda i: (0, 0))


def _modulate(x, sc, sh, name):
    S, D = x.shape
    tr = min(256, S)

    def body(x_ref, sc_ref, sh_ref, h_ref):
        h_ref[...] = (x_ref[...] * (1.0 + sc_ref[...]) + sh_ref[...]).astype(BF16)

    return pl.pallas_call(
        body, name=name, grid=(S // tr,), in_specs=[_rows(tr, D), _vec(D), _vec(D)], out_specs=_rows(tr, D),
        out_shape=jax.ShapeDtypeStruct((S, D), BF16), compiler_params=_params(("parallel",)),
    )(x, sc, sh)


def _ln_fwd(x, y, gate, gamma, beta, sc, sh, name, comm=None):
    S, D = x.shape
    tr = min(256, S)
    emit_h = sc is not None

    def body(*refs):
        if emit_h:
            x_ref, y_ref, g_ref, ga_ref, be_ref, sc_ref, sh_ref, xo_ref, xh_ref, rs_ref, h_ref = refs
        else:
            x_ref, y_ref, g_ref, ga_ref, be_ref, xo_ref, xh_ref, rs_ref = refs
        z = DEEPNORM_ALPHA * x_ref[...] + (1.0 + g_ref[...]) * y_ref[...]
        mu = jnp.mean(z, axis=-1, keepdims=True)
        zc = z - mu
        var = jnp.mean(zc * zc, axis=-1, keepdims=True)
        rstd = lax.rsqrt(var + LN_EPS)
        xh = zc * rstd
        xo = xh * ga_ref[...] + be_ref[...]
        xo_ref[...] = xo
        xh_ref[...] = xh
        rs_ref[...] = rstd
        if emit_h:
            h_ref[...] = (xo * (1.0 + sc_ref[...]) + sh_ref[...]).astype(BF16)

    ins = [x, y, gate, gamma, beta] + ([sc, sh] if emit_h else [])
    in_specs = [_rows(tr, D), _rows(tr, D)] + [_vec(D)] * (len(ins) - 2)
    out_shape = [jax.ShapeDtypeStruct((S, D), F32), jax.ShapeDtypeStruct((S, D), F32), jax.ShapeDtypeStruct((S, 1), F32)]
    out_specs = [_rows(tr, D), _rows(tr, D), _rows(tr, 1)]
    if emit_h:
        out_shape.append(jax.ShapeDtypeStruct((S, D), BF16))
        out_specs.append(_rows(tr, D))
    return _call(body, name=name, grid=(S // tr,), in_specs=in_specs, out_specs=out_specs, out_shape=out_shape,
                 sem=("parallel",), args=ins, comm=comm)


def _loss_head(xf, tgt, name):
    S, D = xf.shape
    tr = min(256, S)

    def body(x_ref, t_ref, dx_ref, l_ref):
        e = x_ref[...] - t_ref[...]
        dx_ref[...] = e * (1.0 / D)

        @pl.when(pl.program_id(0) == 0)
        def _():
            l_ref[...] = jnp.zeros_like(l_ref)

        l_ref[...] += jnp.sum(e * e, axis=0, keepdims=True)

    return pl.pallas_call(
        body, name=name, grid=(S // tr,), in_specs=[_rows(tr, D), _rows(tr, D)],
        out_specs=[_rows(tr, D), _vec(D)],
        out_shape=[jax.ShapeDtypeStruct((S, D), F32), jax.ShapeDtypeStruct((1, D), F32)],
        compiler_params=_params(("arbitrary",)),
    )(xf, tgt)


def _ln_bwd(dxo, xh, rstd, gamma, y, gate, name):
    S, D = dxo.shape
    tr = min(256, S)

    def body(dx_ref, xh_ref, rs_ref, ga_ref, y_ref, g_ref, dres_ref, dy_ref, dga_ref, dbe_ref, dg_ref):
        dxo_ = dx_ref[...]
        xh_ = xh_ref[...]
        dxh = dxo_ * ga_ref[...]
        m1 = jnp.mean(dxh, axis=-1, keepdims=True)
        m2 = jnp.mean(dxh * xh_, axis=-1, keepdims=True)
        dz = rs_ref[...] * (dxh - m1 - xh_ * m2)
        dres_ref[...] = DEEPNORM_ALPHA * dz
        dy_ref[...] = ((1.0 + g_ref[...]) * dz).astype(BF16)

        @pl.when(pl.program_id(0) == 0)
        def _():
            dga_ref[...] = jnp.zeros_like(dga_ref)
            dbe_ref[...] = jnp.zeros_like(dbe_ref)
            dg_ref[...] = jnp.zeros_like(dg_ref)

        dga_ref[...] += jnp.sum(dxo_ * xh_, axis=0, keepdims=True)
        dbe_ref[...] += jnp.sum(dxo_, axis=0, keepdims=True)
        dg_ref[...] += jnp.sum(dz * y_ref[...], axis=0, keepdims=True)

    return pl.pallas_call(
        body, name=name, grid=(S // tr,),
        in_specs=[_rows(tr, D), _rows(tr, D), _rows(tr, 1), _vec(D), _rows(tr, D), _vec(D)],
        out_specs=[_rows(tr, D), _rows(tr, D), _vec(D), _vec(D), _vec(D)],
        out_shape=[jax.ShapeDtypeStruct((S, D), F32), jax.ShapeDtypeStruct((S, D), BF16)] + [jax.ShapeDtypeStruct((1, D), F32)] * 3,
        compiler_params=_params(("arbitrary",)),
    )(dxo, xh, rstd, gamma, y, gate)


def _mod_bwd(dh, x, sc, dres, name):
    S, D = x.shape
    tr = min(256, S)

    def body(dh_ref, x_ref, sc_ref, dr_ref, dx_ref, dsc_ref, dsh_ref):
        dh_ = dh_ref[...]
        dx_ref[...] = dr_ref[...] + dh_ * (1.0 + sc_ref[...])

        @pl.when(pl.program_id(0) == 0)
        def _():
            dsc_ref[...] = jnp.zeros_like(dsc_ref)
            dsh_ref[...] = jnp.zeros_like(dsh_ref)

        dsc_ref[...] += jnp.sum(dh_ * x_ref[...], axis=0, keepdims=True)
        dsh_ref[...] += jnp.sum(dh_, axis=0, keepdims=True)

    return pl.pallas_call(
        body, name=name, grid=(S // tr,),
        in_specs=[_rows(tr, D), _rows(tr, D), _vec(D), _rows(tr, D)],
        out_specs=[_rows(tr, D), _vec(D), _vec(D)],
        out_shape=[jax.ShapeDtypeStruct((S, D), F32), jax.ShapeDtypeStruct((1, D), F32), jax.ShapeDtypeStruct((1, D), F32)],
        compiler_params=_params(("arbitrary",)),
    )(dh, x, sc, dres)


def _log_sigmoid(z):
    return jnp.minimum(z, 0.0) - jnp.log(1.0 + jnp.exp(-jnp.abs(z)))


def _fox_gate_fwd(proj, b_f, n_heads, name):
    S, PW = proj.shape
    blk = min(256, S)
    last = PW // LANES - 1

    def body(fl_ref, b_ref, cum_ref):
        r = lax.broadcasted_iota(jnp.int32, (blk, blk), 0)
        c = lax.broadcasted_iota(jnp.int32, (blk, blk), 1)
        tril = (c <= r).astype(F32)
        carry = jnp.zeros((1, LANES), F32)
        for i in range(S // blk):
            lf = _log_sigmoid(fl_ref[i * blk:(i + 1) * blk, :] + b_ref[...])
            cum_ref[i * blk:(i + 1) * blk, :] = jnp.dot(tril, lf, preferred_element_type=F32, precision=HIGHEST) + carry
            carry = carry + jnp.sum(lf, axis=0, keepdims=True)

    return pl.pallas_call(
        body, name=name, grid=(1,),
        in_specs=[pl.BlockSpec((S, LANES), lambda i: (0, last)), pl.BlockSpec((1, LANES), lambda i: (0, 0))],
        out_specs=pl.BlockSpec((S, LANES), lambda i: (0, 0)),
        out_shape=jax.ShapeDtypeStruct((S, LANES), F32), compiler_params=_params(("arbitrary",)),
    )(proj, b_f)


def _fox_gate_bwd(dcum, proj, b_f, n_heads, name):
    S, PW = proj.shape
    blk = min(256, S)
    last = PW // LANES - 1
    nb = S // blk

    def body(dc_ref, fl_ref, b_ref, dfl_ref, db_ref):
        r = lax.broadcasted_iota(jnp.int32, (blk, blk), 0)
        c = lax.broadcasted_iota(jnp.int32, (blk, blk), 1)
        triu = (c >= r).astype(F32)
        lane = lax.broadcasted_iota(jnp.int32, (blk, LANES), 1)
        carry = jnp.zeros((1, LANES), F32)
        dbs = jnp.zeros((1, LANES), F32)
        for i in reversed(range(nb)):
            dc = dc_ref[i * blk:(i + 1) * blk, :]
            dlf = jnp.dot(triu, dc, preferred_element_type=F32, precision=HIGHEST) + carry
            carry = carry + jnp.sum(dc, axis=0, keepdims=True)
            z = fl_ref[i * blk:(i + 1) * blk, :] + b_ref[...]
            e = jnp.exp(-jnp.abs(z))
            sig_neg = jnp.where(z >= 0, e / (1.0 + e), 1.0 / (1.0 + e))
            dfl = jnp.where(lane < n_heads, dlf * sig_neg, 0.0)
            dfl_ref[i * blk:(i + 1) * blk, :] = dfl.astype(BF16)
            dbs = dbs + jnp.sum(dfl, axis=0, keepdims=True)
        db_ref[...] = dbs

    return pl.pallas_call(
        body, name=name, grid=(1,),
        in_specs=[pl.BlockSpec((S, LANES), lambda i: (0, 0)), pl.BlockSpec((S, LANES), lambda i: (0, last)),
                  pl.BlockSpec((1, LANES), lambda i: (0, 0))],
        out_specs=[pl.BlockSpec((S, LANES), lambda i: (0, 0)), pl.BlockSpec((1, LANES), lambda i: (0, 0))],
        out_shape=[jax.ShapeDtypeStruct((S, LANES), BF16), jax.ShapeDtypeStruct((1, LANES), F32)],
        compiler_params=_params(("arbitrary",)),
    )(dcum, proj, b_f)


def _fox_scores(q_ref, kb_ref, cq_ref, ck_ref, qi, tq, scale):
    kk = (qi + 1) * tq
    rows = slice(qi * tq, (qi + 1) * tq)
    qb = q_ref[rows, :].astype(BF16)
    s = lax.dot_general(qb, kb_ref[0:kk, :], _DN["nt"], preferred_element_type=F32) * scale
    s = s + (cq_ref[0, rows, :] - ck_ref[0, :, 0:kk])
    r = lax.broadcasted_iota(jnp.int32, (tq, kk), 0) + qi * tq
    c = lax.broadcasted_iota(jnp.int32, (tq, kk), 1)
    mask = c <= r
    return jnp.where(mask, s, NEG), mask, qb


def _fox_fwd(proj, cq, ck, n_heads, name, comm=None):
    S = proj.shape[0]
    H = n_heads
    tq = min(FOX_TQ, S)
    nq = S // tq
    scale = FDH ** -0.5

    def body(q_ref, k_ref, v_ref, cq_ref, ck_ref, o_ref, lse_ref, kb_ref, vb_ref):
        kb_ref[...] = k_ref[...].astype(BF16)
        vb_ref[...] = v_ref[...].astype(BF16)
        for qi in range(nq):
            kk = (qi + 1) * tq
            rows = slice(qi * tq, (qi + 1) * tq)
            s, _, _ = _fox_scores(q_ref, kb_ref, cq_ref, ck_ref, qi, tq, scale)
            m = jnp.max(s, axis=-1, keepdims=True)
            p = jnp.exp(s - m)
            l = jnp.sum(p, axis=-1, keepdims=True)
            p = p * (1.0 / l)
            o_ref[rows, :] = jnp.dot(p.astype(BF16), vb_ref[0:kk, :], preferred_element_type=F32).astype(BF16)
            lse_ref[0, rows, :] = m + jnp.log(l)

    col = lambda off: pl.BlockSpec((S, FDH), lambda h: (0, h + off))
    stat_c = pl.BlockSpec((1, S, 1), lambda h: (h, 0, 0))
    stat_r = pl.BlockSpec((1, 1, S), lambda h: (h, 0, 0))
    return _call(
        body, name=name, grid=(H,),
        in_specs=[col(0), col(H), col(2 * H), stat_c, stat_r],
        out_specs=[col(0), stat_c],
        out_shape=[jax.ShapeDtypeStruct((S, H * FDH), BF16), jax.ShapeDtypeStruct((H, S, 1), F32)],
        scratch_shapes=[pltpu.VMEM((S, FDH), BF16), pltpu.VMEM((S, FDH), BF16)],
        sem=("parallel",), args=(proj, proj, proj, cq, ck), comm=comm)


def _fox_bwd(proj, cq, ck, lse, do, n_heads, name, comm=None):
    S = proj.shape[0]
    H = n_heads
    tq = min(FOX_TQ, S)
    nq = S // tq
    scale = FDH ** -0.5

    def body(q_ref, k_ref, v_ref, cq_ref, ck_ref, lse_ref, do_ref, dq_ref, dk_ref, dv_ref, dcq_ref, dck_ref,
             kb_ref, vb_ref, dka_ref, dva_ref):
        kb_ref[...] = k_ref[...].astype(BF16)
        vb_ref[...] = v_ref[...].astype(BF16)
        dka_ref[...] = jnp.zeros_like(dka_ref)
        dva_ref[...] = jnp.zeros_like(dva_ref)
        dck_ref[...] = jnp.zeros_like(dck_ref)
        for qi in range(nq):
            kk = (qi + 1) * tq
            rows = slice(qi * tq, (qi + 1) * tq)
            s, mask, qb = _fox_scores(q_ref, kb_ref, cq_ref, ck_ref, qi, tq, scale)
            p = jnp.where(mask, jnp.exp(s - lse_ref[0, rows, :]), 0.0)
            dob = do_ref[rows, :]
            dp = lax.dot_general(dob, vb_ref[0:kk, :], _DN["nt"], preferred_element_type=F32)
            delta = jnp.sum(p * dp, axis=-1, keepdims=True)
            ds = p * (dp - delta)
            dcq_ref[0, rows, :] = jnp.sum(ds, axis=-1, keepdims=True)
            dck_ref[0, :, 0:kk] -= jnp.sum(ds, axis=0, keepdims=True)
            dsb = (ds * scale).astype(BF16)
            dq_ref[rows, :] = jnp.dot(dsb, kb_ref[0:kk, :], preferred_element_type=F32).astype(BF16)
            dka_ref[0:kk, :] += lax.dot_general(dsb, qb, _DN["tn"], preferred_element_type=F32)
            dva_ref[0:kk, :] += lax.dot_general(p.astype(BF16), dob, _DN["tn"], preferred_element_type=F32)
        dk_ref[...] = dka_ref[...].astype(BF16)
        dv_ref[...] = dva_ref[...].astype(BF16)

    col = lambda off: pl.BlockSpec((S, FDH), lambda h: (0, h + off))
    stat_c = pl.BlockSpec((1, S, 1), lambda h: (h, 0, 0))
    stat_r = pl.BlockSpec((1, 1, S), lambda h: (h, 0, 0))
    wide = jax.ShapeDtypeStruct((S, H * FDH), BF16)
    return _call(
        body, name=name, grid=(H,),
        in_specs=[col(0), col(H), col(2 * H), stat_c, stat_r, stat_c, col(0)],
        out_specs=[col(0), col(0), col(0), stat_c, stat_r],
        out_shape=[wide, wide, wide, jax.ShapeDtypeStruct((H, S, 1), F32), jax.ShapeDtypeStruct((H, 1, S), F32)],
        scratch_shapes=[pltpu.VMEM((S, FDH), BF16), pltpu.VMEM((S, FDH), BF16), pltpu.VMEM((S, FDH), F32), pltpu.VMEM((S, FDH), F32)],
        sem=("parallel",), args=(proj, proj, proj, cq, ck, lse, do), comm=comm)


def _rope_tables(pos, sign):
    inv = ROPE_THETA ** (-jnp.arange(0, ROPE_DIM, 2, dtype=F32) / ROPE_DIM)
    ang = pos.astype(F32)[:, None] * inv
    cos, sin = jnp.cos(ang), sign * jnp.sin(ang)
    l64 = jnp.arange(LANES) % SDH
    idx = l64 % (ROPE_DIM // 2)
    c = jnp.where(l64 < ROPE_DIM, cos[:, idx], 1.0)
    sa = jnp.where(l64 < ROPE_DIM // 2, -sin[:, idx], 0.0)
    sb = jnp.where((l64 >= ROPE_DIM // 2) & (l64 < ROPE_DIM), sin[:, idx], 0.0)
    rot = jnp.stack([c, sa, sb])
    ident = jnp.stack([jnp.ones_like(c), jnp.zeros_like(c), jnp.zeros_like(c)])
    return jnp.stack([rot, ident]).astype(F32)


def _rope(xin, tabs, n_rot, out_dtype, name, comm=None):
    S, W = xin.shape

    def body(x_ref, t_ref, o_ref):
        xv = x_ref[...]
        o = xv * t_ref[0, 0] + pltpu.roll(xv, LANES - ROPE_DIM // 2, 1) * t_ref[0, 1] + pltpu.roll(xv, ROPE_DIM // 2, 1) * t_ref[0, 2]
        o_ref[...] = o.astype(out_dtype)

    return _call(
        body, name=name, grid=(W // LANES,),
        in_specs=[pl.BlockSpec((S, LANES), lambda j: (0, j)),
                  pl.BlockSpec((1, 3, S, LANES), lambda j: (jnp.where(j < n_rot, 0, 1), 0, 0, 0))],
        out_specs=[pl.BlockSpec((S, LANES), lambda j: (0, j))],
        out_shape=[jax.ShapeDtypeStruct((S, W), out_dtype)], sem=("parallel",), args=(xin, tabs), comm=comm)[0]


SWA_PER_STEP = 2


def _swa_bias():
    r = jnp.arange(WIN)[:, None]
    c = jnp.arange(2 * WIN)[None, :]
    first = c <= r
    later = (c > r) & (c <= r + WIN)
    return jnp.where(jnp.stack([first, later]), 0.0, NEG).astype(F32)


def _swa_probs(q_ref, k_ref, sk_ref, b_ref, n, j, scale):
    st = pl.multiple_of(jnp.maximum(n - 1, 0) * WIN, WIN)
    qb = q_ref[0, j]
    kb = k_ref[0, pl.ds(st, 2 * WIN), :]
    gm = qb.shape[0]
    s = lax.dot_general(qb, kb, _DN["nt"], preferred_element_type=F32) * scale
    s = (s.reshape(gm // WIN, WIN, 2 * WIN) + b_ref[jnp.minimum(n, 1)][None]).reshape(gm, 2 * WIN)
    sink = sk_ref[0]
    m = jnp.maximum(jnp.max(s, axis=-1, keepdims=True), sink)
    e = jnp.exp(s - m)
    es = jnp.exp(sink - m)
    inv = 1.0 / (jnp.sum(e, axis=-1, keepdims=True) + es)
    return e * inv, es * inv, st, qb, kb


def _swa_specs(S, gm):
    blk = pl.BlockSpec((1, SWA_PER_STEP, gm, SDH), lambda g, n: (g, n, 0, 0))
    kv = pl.BlockSpec((1, S, SDH), lambda g, n: (g, 0, 0))
    col = pl.BlockSpec((1, gm, 1), lambda g, n: (g, 0, 0))
    bias = pl.BlockSpec((2, WIN, 2 * WIN), lambda g, n: (0, 0, 0))
    return blk, kv, col, bias


def _swa_fwd(q, k, v, sinks, name, comm=None):
    KH, nb, gm, _ = q.shape
    S = k.shape[1]
    scale = SDH ** -0.5

    def body(q_ref, k_ref, v_ref, sk_ref, b_ref, o_ref):
        for j in range(SWA_PER_STEP):
            p, _, st, _, _ = _swa_probs(q_ref, k_ref, sk_ref, b_ref, pl.program_id(1) * SWA_PER_STEP + j, j, scale)
            vb = v_ref[0, pl.ds(st, 2 * WIN), :]
            o_ref[0, j] = jnp.dot(p.astype(BF16), vb, preferred_element_type=F32).astype(BF16)

    blk, kv, col, bias = _swa_specs(S, gm)
    return _call(
        body, name=name, grid=(KH, nb // SWA_PER_STEP), in_specs=[blk, kv, kv, col, bias], out_specs=[blk],
        out_shape=[jax.ShapeDtypeStruct(q.shape, BF16)], sem=("parallel", "parallel"), args=(q, k, v, sinks, _swa_bias()), comm=comm)[0]


def _swa_bwd(q, k, v, sinks, do, name, comm=None):
    KH, nb, gm, _ = q.shape
    S = k.shape[1]
    scale = SDH ** -0.5

    def body(q_ref, k_ref, v_ref, sk_ref, b_ref, do_ref, dq_ref, dk_ref, dv_ref, dsk_ref):
        @pl.when(pl.program_id(1) == 0)
        def _():
            dk_ref[...] = jnp.zeros_like(dk_ref)
            dv_ref[...] = jnp.zeros_like(dv_ref)
            dsk_ref[...] = jnp.zeros_like(dsk_ref)

        blocks = []
        for j in range(SWA_PER_STEP):
            p, ps, st, qb, kb = _swa_probs(q_ref, k_ref, sk_ref, b_ref, pl.program_id(1) * SWA_PER_STEP + j, j, scale)
            vb = v_ref[0, pl.ds(st, 2 * WIN), :]
            dob = do_ref[0, j]
            dp = lax.dot_general(dob, vb, _DN["nt"], preferred_element_type=F32)
            delta = jnp.sum(p * dp, axis=-1, keepdims=True)
            dsb = (p * (dp - delta) * scale).astype(BF16)
            dq_ref[0, j] = jnp.dot(dsb, kb, preferred_element_type=F32)
            blocks.append((st, lax.dot_general(dsb, qb, _DN["tn"], preferred_element_type=F32),
                           lax.dot_general(p.astype(BF16), dob, _DN["tn"], preferred_element_type=F32), ps * delta))
        for st, dk, dv, dsk in blocks:
            dk_ref[0, pl.ds(st, 2 * WIN), :] += dk
            dv_ref[0, pl.ds(st, 2 * WIN), :] += dv
            dsk_ref[0] -= dsk

    blk, kv, col, bias = _swa_specs(S, gm)
    return _call(
        body, name=name, grid=(KH, nb // SWA_PER_STEP), in_specs=[blk, kv, kv, col, bias, blk], out_specs=[blk, kv, kv, col],
        out_shape=[jax.ShapeDtypeStruct(q.shape, F32), jax.ShapeDtypeStruct(k.shape, F32),
                   jax.ShapeDtypeStruct(k.shape, F32), jax.ShapeDtypeStruct(sinks.shape, F32)],
        sem=("parallel", "arbitrary"), args=(q, k, v, sinks, _swa_bias(), do), comm=comm)


def _shift_down(u, k):
    row = lax.broadcasted_iota(jnp.int32, u.shape, 0)
    return jnp.where(row >= k, pltpu.roll(u, k, 0), 0.0)


def _shift_up(u, k):
    n = u.shape[0]
    row = lax.broadcasted_iota(jnp.int32, u.shape, 0)
    return jnp.where(row < n - k, pltpu.roll(u, n - k, 0), 0.0)


def _conv3(u, w_ref, b_ref):
    return w_ref[0:1, :] * _shift_down(u, 2) + w_ref[1:2, :] * _shift_down(u, 1) + w_ref[2:3, :] * u + b_ref[...]


def _conv_gate(u, cw, cb, name, comm=None):
    S, F2 = u.shape
    Fh = F2 // 2
    tc = _tile(Fh, 256)
    nf = Fh // tc

    def body(ug_ref, uv_ref, wg_ref, wv_ref, bg_ref, bv_ref, a_ref):
        g = _conv3(ug_ref[...], wg_ref, bg_ref)
        val = _conv3(uv_ref[...], wv_ref, bv_ref)
        a_ref[...] = (g * (1.0 / (1.0 + jnp.exp(-g))) * val).astype(BF16)

    blk = lambda r, off: pl.BlockSpec((r, tc), lambda j: (0, j + off))
    return _call(
        body, name=name, grid=(nf,),
        in_specs=[blk(S, 0), blk(S, nf), blk(3, 0), blk(3, nf), blk(1, 0), blk(1, nf)], out_specs=[blk(S, 0)],
        out_shape=[jax.ShapeDtypeStruct((S, Fh), BF16)], sem=("parallel",), args=(u, u, cw, cw, cb, cb), comm=comm)[0]


def _conv_gate_bwd(u, da, cw, cb, name, comm=None):
    S, F2 = u.shape
    Fh = F2 // 2
    tc = _tile(Fh, 256)
    nf = Fh // tc

    def half(h, dx, uu, w_ref, du_ref, dw_ref, db_ref):
        up1, up2 = _shift_up(dx, 1), _shift_up(dx, 2)
        du = w_ref[2:3, :] * dx + w_ref[1:2, :] * up1 + w_ref[0:1, :] * up2
        du_ref[h] = du.astype(BF16)
        dw_ref[h, 0:1, :] = jnp.sum(up2 * uu, axis=0, keepdims=True)
        dw_ref[h, 1:2, :] = jnp.sum(up1 * uu, axis=0, keepdims=True)
        dw_ref[h, 2:3, :] = jnp.sum(dx * uu, axis=0, keepdims=True)
        db_ref[h] = jnp.sum(dx, axis=0, keepdims=True)

    def body(ug_ref, uv_ref, da_ref, wg_ref, wv_ref, bg_ref, bv_ref, du_ref, dw_ref, db_ref):
        ug = ug_ref[...]
        uv = uv_ref[...]
        g = _conv3(ug, wg_ref, bg_ref)
        val = _conv3(uv, wv_ref, bv_ref)
        sig = 1.0 / (1.0 + jnp.exp(-g))
        da_ = da_ref[...]
        dg = da_ * val * (sig * (1.0 + g * (1.0 - sig)))
        dval = da_ * (g * sig)
        half(0, dg, ug, wg_ref, du_ref, dw_ref, db_ref)
        half(1, dval, uv, wv_ref, du_ref, dw_ref, db_ref)

    blk = lambda r, off: pl.BlockSpec((r, tc), lambda j: (0, j + off))
    both = lambda r: pl.BlockSpec((2, r, tc), lambda j: (0, 0, j))
    return _call(
        body, name=name, grid=(nf,),
        in_specs=[blk(S, 0), blk(S, nf), blk(S, 0), blk(3, 0), blk(3, nf), blk(1, 0), blk(1, nf)],
        out_specs=[both(S), both(3), both(1)],
        out_shape=[jax.ShapeDtypeStruct((2, S, Fh), BF16), jax.ShapeDtypeStruct((2, 3, Fh), F32), jax.ShapeDtypeStruct((2, 1, Fh), F32)],
        sem=("parallel",), args=(u, u, da, cw, cw, cb, cb), comm=comm)


def _to_groups(t, kh):
    S, width = t.shape
    g = width // SDH // kh
    return t.reshape(S // WIN, WIN, kh, g, SDH).transpose(2, 0, 3, 1, 4).reshape(kh, S // WIN, g * WIN, SDH)


def _from_groups(t):
    kh, nb, gm, _ = t.shape
    g = gm // WIN
    return t.reshape(kh, nb, g, WIN, SDH).transpose(1, 3, 0, 2, 4).reshape(nb * WIN, kh * g * SDH)


class LocalWeights:
    def __init__(self, weights):
        self.weights, self.grads = weights, {}

    def w(self, name):
        return self.weights[name]

    def carry(self, stage):
        return None

    def carried(self, stage, comm):
        pass

    def grad(self, name, g):
        self.grads[name] = g


def _local_step(dm, x, tgt, pos, mod, sp, pp):
    S, D, FH, QH, KH, Fh = dm
    m = [[mod[i:i + 1, j * D:(j + 1) * D] for j in range(6)] for i in range(DEPTH)]

    def run(fn, *args, name, **kw):
        comm = pp.carry(name)
        out = fn(*args, name=name, comm=comm, **kw)
        if comm is not None:
            pp.carried(name, comm)
        return out

    sv = []
    xs = x
    h = _modulate(xs, m[0][1], m[0][0], "mod_in")
    for i in range(DEPTH):
        sh1, sc1, g1, sh2, sc2, g2 = m[i]
        L = {}
        L["x_in"], L["h1"] = xs, h
        if i == 0:
            proj = run(_mm, h[None], pp.w("fox_w_in"), mode="nn", out_dtype=F32, name="fox_proj", tn=896)[0]
            cum = _fox_gate_fwd(proj, sp["fox_b_f"], FH, "fox_gate")
            cq = cum[:, :FH].T[:, :, None]
            ck = cum[:, :FH].T[:, None, :]
            o, lse = run(_fox_fwd, proj, cq, ck, FH, name="fox_attn")
            L.update(proj=proj, cq=cq, ck=ck, lse=lse, o=o)
            y = run(_mm, o[None], pp.w("fox_w_o"), mode="nn", out_dtype=F32, name="fox_out")[0]
        else:
            proj = run(_mm, h[None], pp.w("swa_w_in"), mode="nn", out_dtype=F32, name="swa_proj", tn=640)[0]
            tabs = _rope_tables(pos, 1.0)
            n_rot = (QH + KH) * SDH // LANES
            pr = run(_rope, proj, tabs, n_rot, BF16, name="swa_rope")
            qh = _to_groups(pr[:, :QH * SDH], KH)
            kh = pr[:, QH * SDH:(QH + KH) * SDH].reshape(S, KH, SDH).transpose(1, 0, 2)
            vh = pr[:, (QH + KH) * SDH:].reshape(S, KH, SDH).transpose(1, 0, 2)
            oh = run(_swa_fwd, qh, kh, vh, sp["sinks"], name="swa_attn")
            o = _from_groups(oh)
            L.update(qh=qh, kh=kh, vh=vh, o=o)
            y = run(_mm, o[None], pp.w("swa_w_o"), mode="nn", out_dtype=F32, name="swa_out")[0]
        L["y1"] = y
        x1, L["xh1"], L["rs1"], h2 = run(_ln_fwd, xs, y, g1, sp["ln_mix_g"][i], sp["ln_mix_b"][i], sc2, sh2, name=f"ln_mix{i}")
        L["x1"], L["h2"] = x1, h2
        u = run(_mm, h2[None], pp.w(f"ffn_w_up{i}"), mode="nn", out_dtype=F32, name=f"ffn_up{i}", tm=512, tn=1408)[0]
        a = run(_conv_gate, u, sp["conv_w"][i], sp["conv_b"][i], name=f"ffn_gate{i}")
        y2 = run(_mm, a[None], pp.w(f"ffn_w_down{i}"), mode="nn", out_dtype=F32, name=f"ffn_down{i}", tk=2816)[0]
        L.update(u=u, a=a, y2=y2)
        if i + 1 < DEPTH:
            xs, L["xh2"], L["rs2"], h = run(_ln_fwd, x1, y2, g2, sp["ln_ffn_g"][i], sp["ln_ffn_b"][i], m[i + 1][1], m[i + 1][0], name=f"ln_ffn{i}")
        else:
            xs, L["xh2"], L["rs2"] = run(_ln_fwd, x1, y2, g2, sp["ln_ffn_g"][i], sp["ln_ffn_b"][i], None, None, name=f"ln_ffn{i}")
        sv.append(L)

    dx, loss_cols = _loss_head(xs, tgt, "loss_head")

    gs = {k: [None] * DEPTH for k in ("conv_w", "conv_b", "ln_mix_g", "ln_mix_b", "ln_ffn_g", "ln_ffn_b")}
    dmod = [None] * DEPTH
    for i in reversed(range(DEPTH)):
        sh1, sc1, g1, sh2, sc2, g2 = m[i]
        L = sv[i]
        dres, dy, gs["ln_ffn_g"][i], gs["ln_ffn_b"][i], dg2 = _ln_bwd(dx, L["xh2"], L["rs2"], sp["ln_ffn_g"][i], L["y2"], g2, f"ln_ffn_bwd{i}")
        da = run(_mm, dy[None], pp.w(f"ffn_w_down{i}"), mode="nt", out_dtype=F32, name=f"ffn_down_dx{i}", tm=512, tn=1408)[0]
        pp.grad(f"ffn_w_down{i}", run(_mm, L["a"][None], dy[None], mode="tn", out_dtype=BF16, name=f"ffn_down_dw{i}", tm=1408))
        du, dcw, dcb = run(_conv_gate_bwd, L["u"], da, sp["conv_w"][i], sp["conv_b"][i], name=f"ffn_gate_bwd{i}")
        gs["conv_w"][i] = dcw.transpose(1, 0, 2).reshape(3, 2 * Fh)
        gs["conv_b"][i] = dcb.transpose(1, 0, 2).reshape(1, 2 * Fh)
        dh2 = run(_mm, du, pp.w(f"ffn_w_up{i}"), mode="nt", out_dtype=F32, name=f"ffn_up_dx{i}", tk=2816)[0]
        pp.grad(f"ffn_w_up{i}", run(_mm, L["h2"][None], du, mode="tn", out_dtype=BF16, name=f"ffn_up_dw{i}", out_groups=N_CHIPS, tn=1408))
        dx, dsc2, dsh2 = _mod_bwd(dh2, L["x1"], sc2, dres, f"mod_ffn_bwd{i}")
        dres, dy, gs["ln_mix_g"][i], gs["ln_mix_b"][i], dg1 = _ln_bwd(dx, L["xh1"], L["rs1"], sp["ln_mix_g"][i], L["y1"], g1, f"ln_mix_bwd{i}")
        if i == 0:
            do = run(_mm, dy[None], pp.w("fox_w_o"), mode="nt", out_dtype=BF16, name="fox_out_dx")[0]
            pp.grad("fox_w_o", run(_mm, L["o"][None], dy[None], mode="tn", out_dtype=BF16, name="fox_out_dw"))
            dq, dk, dv, dcq, dck = run(_fox_bwd, L["proj"], L["cq"], L["ck"], L["lse"], do, FH, name="fox_attn_bwd")
            dcum = dcq[:, :, 0].T + dck[:, 0, :].T
            dcum = jnp.pad(dcum, ((0, 0), (0, LANES - FH)))
            dfl, db_f = _fox_gate_bwd(dcum, L["proj"], sp["fox_b_f"], FH, "fox_gate_bwd")
            gs["fox_b_f"] = db_f
            dproj = jnp.concatenate([dq, dk, dv, dfl], axis=1)
            pp.grad("fox_w_in", run(_mm, L["h1"][None], dproj[None], mode="tn", out_dtype=BF16, name="fox_proj_dw", tn=896))
            dh1 = run(_mm, dproj[None], pp.w("fox_w_in"), mode="nt", out_dtype=F32, name="fox_proj_dx", tk=6272, tm=512)[0]
        else:
            do = run(_mm, dy[None], pp.w("swa_w_o"), mode="nt", out_dtype=BF16, name="swa_out_dx")[0]
            pp.grad("swa_w_o", run(_mm, L["o"][None], dy[None], mode="tn", out_dtype=BF16, name="swa_out_dw"))
            dqh, dkh, dvh, dsk = run(_swa_bwd, L["qh"], L["kh"], L["vh"], sp["sinks"], _to_groups(do, KH), name="swa_attn_bwd")
            gs["sinks"] = jnp.sum(dsk.reshape(QH, WIN), axis=1)
            dpr = jnp.concatenate([_from_groups(dqh), dkh.transpose(1, 0, 2).reshape(S, KH * SDH),
                                   dvh.transpose(1, 0, 2).reshape(S, KH * SDH)], axis=1)
            n_rot = (QH + KH) * SDH // LANES
            dproj = _rope(dpr, _rope_tables(pos, -1.0), n_rot, BF16, "swa_rope_bwd")
            dh1 = run(_mm, dproj[None], pp.w("swa_w_in"), mode="nt", out_dtype=F32, name="swa_proj_dx", tk=640)[0]
            pp.grad("swa_w_in", run(_mm, L["h1"][None], dproj[None], mode="tn", out_dtype=BF16, name="swa_proj_dw", out_groups=N_CHIPS, tn=640))
        dx, dsc1, dsh1 = _mod_bwd(dh1, L["x_in"], sc1, dres, f"mod_mix_bwd{i}")
        dmod[i] = jnp.concatenate([dsh1, dsc1, dg1, dsh2, dsc2, dg2], axis=1)
    return loss_cols, dx, gs, jnp.concatenate(dmod, axis=0)


def _allgather_small(v, name):
    m_per, n = v.shape

    def body(x_ref, out_ref, send_sems, recv_sems, local_sem):
        x, y, c, chips = _place()
        me, sibling = (x, y, c), (x, y, 1 - c)

        def rows(px, py, pc):
            return out_ref.at[pl.ds((4 * px + 2 * py + pc) * m_per, m_per), :]

        def copy(k, block, to, src=None):
            return _remote(rows(*block) if src is None else src, rows(*block), send_sems.at[k], recv_sems.at[k], to)

        mine = pltpu.make_async_copy(x_ref, rows(*me), local_sem)
        mine.start()
        first = [copy(0, me, sibling, src=x_ref)]
        first += [copy(1 + j, me, (*chip, c), src=x_ref) for j, chip in enumerate(chips)]
        for cp in first:
            cp.start()
        passed = [copy(4 + j, (*chip, c), sibling) for j, chip in enumerate(chips)]
        for j, chip in enumerate(chips):
            copy(1 + j, (*chip, c), me).wait_recv()
            passed[j].start()
        copy(0, sibling, me).wait_recv()
        for j, chip in enumerate(chips):
            copy(4 + j, (*chip, 1 - c), me).wait_recv()
        for cp in first + passed:
            cp.wait_send()
        mine.wait()

    return pl.pallas_call(
        body, name=name, out_shape=jax.ShapeDtypeStruct((N_DEV * m_per, n), v.dtype),
        in_specs=[pl.BlockSpec(memory_space=pltpu.VMEM)], out_specs=pl.BlockSpec(memory_space=pltpu.VMEM),
        scratch_shapes=[pltpu.SemaphoreType.DMA((7,)), pltpu.SemaphoreType.DMA((7,)), pltpu.SemaphoreType.DMA],
        compiler_params=pltpu.CompilerParams(vmem_limit_bytes=VMEM_LIMIT),
    )(v)


def _row_tile(r, pref=256):
    return _tile(r, pref, 16)


def _cast_bf16(w, layer, chip, name):
    _, R, C = w.shape
    tr = _row_tile(R)

    def body(s_ref, w_ref, o_ref):
        o_ref[...] = w_ref[...].astype(BF16)

    return pl.pallas_call(
        body, name=name,
        grid_spec=pltpu.PrefetchScalarGridSpec(
            num_scalar_prefetch=1, grid=(R // tr,),
            in_specs=[pl.BlockSpec((None, tr, C), lambda i, s: (layer, i, 0))],
            out_specs=pl.BlockSpec((None, tr, C), lambda i, s: (s[0], i, 0))),
        out_shape=jax.ShapeDtypeStruct((N_CHIPS, R, C), BF16), compiler_params=_params(("parallel",)),
    )(jnp.reshape(chip, (1,)).astype(jnp.int32), w)


def _add_sibling(g, got, c, name):
    G, R, C = g.shape
    rh = R // 2
    tr = _row_tile(rh)
    nb = rh // tr

    def body(c_ref, g_ref, o_ref, p_ref):
        p_ref[...] = (g_ref[...].astype(F32) + o_ref[...].astype(F32)).astype(BF16)

    return pl.pallas_call(
        body, name=name,
        grid_spec=pltpu.PrefetchScalarGridSpec(
            num_scalar_prefetch=1, grid=(G, nb),
            in_specs=[pl.BlockSpec((1, tr, C), lambda s, i, c_ref: (s, c_ref[0] * nb + i, 0)),
                      pl.BlockSpec((1, tr, C), lambda s, i, c_ref: (s, i, 0))],
            out_specs=pl.BlockSpec((1, tr, C), lambda s, i, c_ref: (s, i, 0))),
        out_shape=jax.ShapeDtypeStruct((G, rh, C), BF16), compiler_params=_params(("parallel", "parallel")),
    )(jnp.reshape(c, (1,)).astype(jnp.int32), g, got)


def _sum_chips(part, landed, chip, c, name):
    G, rh, C = part.shape
    tr = _row_tile(rh)
    nb = rh // tr

    def body(p_ref, own_ref, *rest):
        acc = own_ref[...].astype(F32)
        for ref in rest[:G - 1]:
            acc = acc + ref[...].astype(F32)
        rest[G - 1][...] = acc

    slot = lambda k: pl.BlockSpec((None, tr, C), lambda i, p: ((p[0] + k) % G, i, 0))
    return pl.pallas_call(
        body, name=name,
        grid_spec=pltpu.PrefetchScalarGridSpec(
            num_scalar_prefetch=1, grid=(nb,), in_specs=[slot(k) for k in range(G)],
            out_specs=pl.BlockSpec((tr, C), lambda i, p: (p[1] * nb + i, 0))),
        out_shape=jax.ShapeDtypeStruct((2 * rh, C), F32), compiler_params=_params(("parallel",)),
    )(jnp.stack([chip, c]).astype(jnp.int32), part, *([landed] * (G - 1)))


def _adam_math(w, g, m, v):
    m = ADAM_B1 * m + (1.0 - ADAM_B1) * g
    v = ADAM_B2 * v + (1.0 - ADAM_B2) * (g * g)
    m_hat = m / (1.0 - ADAM_B1 ** ADAM_STEP)
    v_hat = v / (1.0 - ADAM_B2 ** ADAM_STEP)
    delta = -ADAM_LR * (m_hat / (jnp.sqrt(v_hat) + ADAM_EPS) + ADAM_WD * w)
    return delta, m, v


def _adamw(w, g, m, v, layer, prev, name, by_cols=False):
    L, R, C = w.shape
    tr = R if by_cols else _tile(R, 128, 8)
    tc = _tile(C, 256) if by_cols else C
    n_prev = len(prev)

    def body(w_ref, g_ref, m_ref, v_ref, *rest):
        go_ref, d_ref, mo_ref, vo_ref = rest[n_prev:]
        gv = g_ref[...]
        go_ref[...] = gv
        d_ref[...], mo_ref[...], vo_ref[...] = _adam_math(w_ref[...], gv, m_ref[...], v_ref[...])

    lay = pl.BlockSpec((None, tr, tc), lambda i: (layer, i // (C // tc), i % (C // tc)))
    flat = pl.BlockSpec((tr, tc), lambda i: (i // (C // tc), i % (C // tc)))
    return _call(
        body, name=name, grid=((R // tr) * (C // tc),), in_specs=[lay, flat, lay, lay] + _any_specs(n_prev), out_specs=[lay] * 4,
        out_shape=[jax.ShapeDtypeStruct((L, R, C), F32)] * 4, aliases={4 + k: k for k in range(n_prev)},
        sem=("parallel",), args=(w, g, m, v, *prev))


def _cond_rows(c_row, cw, name):
    D = c_row.shape[1]
    nr, fc = cw.shape

    def body(c_ref, e_ref, o_ref):
        o_ref[...] = jnp.zeros_like(o_ref)
        cv = c_ref[...]
        o_ref[0:1, 0:D] = cv * (1.0 / (1.0 + jnp.exp(-cv)))
        o_ref[8:8 + nr, 0:fc] = e_ref[...]

    return pl.pallas_call(body, name=name, out_shape=jax.ShapeDtypeStruct((16, max(D, fc)), F32))(c_row, cw)


def _ada_fwd(cact, ada_w, ada_b, layer, chip, name):
    _, D, NC = ada_w.shape
    tn = _tile(NC, 1024)
    nj = NC // tn

    def body(idx_ref, c_ref, w_ref, b_ref, o_ref):
        acc = jnp.dot(c_ref[...].astype(BF16), w_ref[0].astype(BF16), preferred_element_type=F32)
        o_ref[...] = acc + b_ref[pl.ds(idx_ref[0], 1), :]

    return pl.pallas_call(
        body, name=name,
        grid_spec=pltpu.PrefetchScalarGridSpec(
            num_scalar_prefetch=1, grid=(nj,),
            in_specs=[pl.BlockSpec((8, D), lambda j, idx: (0, 0)),
                      pl.BlockSpec((1, D, tn), lambda j, idx: (idx[0], 0, j)),
                      pl.BlockSpec((DEPTH, tn), lambda j, idx: (0, idx[1] * nj + j))],
            out_specs=pl.BlockSpec((8, tn), lambda j, idx: (0, j))),
        out_shape=jax.ShapeDtypeStruct((8, NC), F32), compiler_params=_params(("parallel",)),
    )(jnp.stack([layer, chip]).astype(jnp.int32), cact, ada_w, ada_b)


def _ada_grad_adamw(cact_t, dmod, w, m, v, name, comm=None):
    L, D, NC = w.shape
    tr = _tile(D, 128, 8)

    def body(c_ref, d_ref, w_ref, m_ref, v_ref, g_ref, dl_ref, mo_ref, vo_ref):
        g = jnp.dot(c_ref[...], d_ref[...], preferred_element_type=F32, precision=HIGHEST)
        g_ref[...] = g
        dl_ref[...], mo_ref[...], vo_ref[...] = _adam_math(w_ref[...], g, m_ref[...], v_ref[...])

    lay = pl.BlockSpec((None, tr, NC), lambda l, i: (l, i, 0))
    return _call(
        body, name=name, grid=(L, D // tr),
        in_specs=[pl.BlockSpec((tr, N_DEV), lambda l, i: (i, 0)), pl.BlockSpec((None, N_DEV, NC), lambda l, i: (l, 0, 0)), lay, lay, lay],
        out_specs=[lay] * 4, out_shape=[jax.ShapeDtypeStruct((L, D, NC), F32)] * 4,
        sem=("parallel", "parallel"), args=(cact_t, dmod, w, m, v), comm=comm)


def _sum_devices(gathered, name):
    n, R, C = gathered.shape

    def body(g_ref, o_ref):
        acc = g_ref[0]
        for j in range(1, n):
            acc = acc + g_ref[j]
        o_ref[...] = acc

    return pl.pallas_call(body, name=name, out_shape=jax.ShapeDtypeStruct((R, C), F32),
                          compiler_params=pltpu.CompilerParams(vmem_limit_bytes=VMEM_LIMIT))(gathered)


def _adamw_small(w, g, m, v, name):
    def body(w_ref, g_ref, m_ref, v_ref, d_ref, mo_ref, vo_ref):
        d_ref[...], mo_ref[...], vo_ref[...] = _adam_math(w_ref[...], g_ref[...], m_ref[...], v_ref[...])

    return pl.pallas_call(body, name=name, out_shape=[jax.ShapeDtypeStruct(w.shape, F32)] * 3)(w, g, m, v)


def _pad_rows(flat, unit=8 * LANES):
    n = flat.shape[0]
    total = -(-n // unit) * unit
    return jnp.pad(flat, (0, total - n)).reshape(total // LANES, LANES)


def _pad_lanes(v2d):
    return jnp.pad(v2d.reshape(1, -1), ((0, 0), (0, LANES - v2d.size)))


PLAN = {
    "fox_proj": [("gather", "ffn_w_up0", 0, 2, 8)],
    "fox_attn": [("gather", "ffn_w_up0", 2, 6, 8)],
    "fox_out": [("gather", "ffn_w_up0", 6, 7, 8)],
    "ln_mix0": [("gather", "ffn_w_up0", 7, 8, 8)],
    "ffn_up0": [("gather", "ffn_w_down0", 0, 1, 1)],
    "ffn_gate0": [("gather", "swa_w_in", 0, 1, 1)],
    "ffn_down0": [("gather", "swa_w_o", 0, 1, 1), ("gather", "ffn_w_up1", 0, 1, 8)],
    "ln_ffn0": [("gather", "ffn_w_up1", 1, 2, 8)],
    "swa_proj": [("gather", "ffn_w_up1", 2, 3, 8)],
    "swa_rope": [("gather", "ffn_w_up1", 3, 4, 8)],
    "swa_attn": [("gather", "ffn_w_up1", 4, 8, 8)],
    "ffn_up1": [("gather", "ffn_w_down1", 0, 1, 1)],
    "ffn_gate_bwd1": [("swap", "ffn_w_down1")],
    "ffn_up_dx1": [("scatter", "ffn_w_down1", 0, 1, 1)],
    "swa_out_dx": [("swap", "ffn_w_up1")],
    "swa_attn_bwd": [("scatter", "ffn_w_up1", 0, 6, 8), ("swap", "swa_w_o")],
    "swa_proj_dx": [("scatter", "swa_w_o", 0, 1, 1)],
    "ffn_down_dx0": [("scatter", "ffn_w_up1", 6, 8, 8), ("swap", "swa_w_in")],
    "ffn_down_dw0": [("scatter", "swa_w_in", 0, 1, 1)],
    "ffn_gate_bwd0": [("swap", "ffn_w_down0")],
    "ffn_up_dx0": [("scatter", "ffn_w_down0", 0, 1, 1)],
    "fox_out_dx": [("swap", "ffn_w_up0")],
    "fox_attn_bwd": [("scatter", "ffn_w_up0", 0, 5, 8), ("swap", "fox_w_o")],
    "fox_proj_dw": [("scatter", "fox_w_o", 0, 1, 1), ("scatter", "ffn_w_up0", 5, 6, 8)],
    "fox_proj_dx": [("scatter", "ffn_w_up0", 6, 8, 8), ("swap", "fox_w_in")],
}


class Exchanges:
    def __init__(self, dm, slots, chip, c):
        self.dm, self.slots, self.chip, self.c = dm, dict(slots), chip, c
        self.raw, self.part, self.landed, self.grads, self.views, self.pending = {}, {}, {}, {}, {}, {}

    def gather_now(self, keys, name):
        comm = _gather_comm([self.slots[k] for k in keys], [(0, 1, 1)] * len(keys))
        _run_comm(comm, name)
        self.slots.update(zip(keys, comm.results))

    def w(self, key):
        if key not in self.views:
            S, D, FH, QH, KH, Fh = self.dm
            full = self.slots[key]
            if key == "fox_w_in":
                cols = full.shape[2]
                full = jnp.pad(full.transpose(1, 0, 2).reshape(D, N_CHIPS * cols), ((0, 0), (0, 3 * D + LANES - N_CHIPS * cols)))[None]
            elif key in ("fox_w_o", "swa_w_o"):
                full = full.reshape(1, D, D)
            elif key.startswith("ffn_w_down"):
                full = full.reshape(1, Fh, D)
            self.views[key] = full
        return self.views[key]

    def carry(self, stage):
        todo = []
        for kind, key, *chunk in PLAN.get(stage, ()):
            if kind == "gather":
                todo.append((kind, [key], _gather_comm([self.slots[key]], [tuple(chunk)])))
            elif kind == "swap":
                todo.append((kind, [key], _swap_comm([self.raw[key]])))
            elif kind == "scatter":
                todo.append((kind, [(key, *chunk)], _scatter_comm([self.part[key]], [self.landed.get(key)], [tuple(chunk)])))
        self.pending[stage] = todo
        return _merge([cm for _, _, cm in todo])

    def carried(self, stage, comm):
        for kind, keys, cm in self.pending.pop(stage):
            if kind == "gather":
                self.slots[keys[0]] = cm.results[0]
            elif kind == "swap":
                self.part[keys[0]] = _add_sibling(self.raw[keys[0]], cm.results[0], self.c, f"add_sibling_{keys[0]}")
            else:
                self.landed[keys[0][0]] = cm.results[0]

    def grad(self, key, g):
        S, D, FH, QH, KH, Fh = self.dm
        if key == "fox_w_in":
            cols = self.slots[key].shape[2]
            g = g[0][:, :N_CHIPS * cols].reshape(D, N_CHIPS, cols).transpose(1, 0, 2)
        elif key in ("fox_w_o", "swa_w_o"):
            g = g.reshape(N_CHIPS, D // N_CHIPS, D)
        elif key.startswith("ffn_w_down"):
            g = g.reshape(N_CHIPS, Fh // N_CHIPS, D)
        self.raw[key] = g

    def finish(self):
        last = "fox_w_in"
        keys = list(self.landed)
        halves = [_sum_chips(self.part[k], self.landed[k], self.chip, self.c, f"sum_chips_{k}") for k in keys]
        send, join = _scatter_comm([self.part[last]], [None], [(0, 1, 1)]), _join_comm(halves)
        _run_comm(_merge([send, join]), "grads_tail")
        grads = dict(zip(keys, join.results))
        join = _join_comm([_sum_chips(self.part[last], send.results[0], self.chip, self.c, f"sum_chips_{last}")])
        _run_comm(join, "grads_join_last")
        grads[last] = join.results[0]
        return grads


def _step(dm, a):
    S, D, FH, QH, KH, Fh = dm
    ix, iy, ic = lax.axis_index("x"), lax.axis_index("y"), lax.axis_index("c")
    chip = 2 * ix + iy
    dev = 2 * chip + ic
    F2c = a["ffn_w_up"].shape[2]
    NC = a["ada_w"].shape[2]
    PW = 3 * D + LANES
    fox_cols = a["fox_w_in"].shape[2]

    e0 = _cond_rows(a["c"], a["ffn_conv_w"].reshape(DEPTH * 3, F2c), "silu_c")
    g0 = _allgather_small(e0, "gather_cond").reshape(N_DEV, 16, e0.shape[1])
    cact = g0[:, 0, :D]
    conv_w = g0[0::2, 8:8 + DEPTH * 3, :F2c].transpose(1, 0, 2).reshape(DEPTH, 3, N_CHIPS * F2c)
    rows = _ada_fwd(cact, a["ada_w"], a["ada_b"], ic, chip, "ada_proj")
    g1 = _allgather_small(rows, "gather_mod").reshape(N_CHIPS, DEPTH, 8, NC)
    mod = lax.dynamic_index_in_dim(g1, dev, axis=2, keepdims=False).transpose(1, 0, 2).reshape(DEPTH, N_CHIPS * NC)

    names = ["fox_w_in", "fox_w_o", "swa_w_in", "swa_w_o", "ffn_w_up", "ffn_w_up", "ffn_w_down", "ffn_w_down"]
    layers = [0, 0, 0, 0, 0, 1, 0, 1]
    keys = ["fox_w_in", "fox_w_o", "swa_w_in", "swa_w_o", "ffn_w_up0", "ffn_w_up1", "ffn_w_down0", "ffn_w_down1"]
    slots = {k: _cast_bf16(a[nm], l, chip, f"cast_{k}") for k, nm, l in zip(keys, names, layers)}
    pp = Exchanges(dm, slots, chip, ic)
    pp.gather_now(["fox_w_in", "fox_w_o"], "gather_fox")
    sp = {"fox_b_f": _pad_lanes(a["fox_b_f"]), "sinks": jnp.repeat(a["swa_sinks"].reshape(KH, QH // KH), WIN, axis=1)[:, :, None],
          "conv_w": [conv_w[i] for i in range(DEPTH)], "conv_b": [a["ffn_conv_b"][i:i + 1] for i in range(DEPTH)]}
    for nm in ("ln_mix_g", "ln_mix_b", "ln_ffn_g", "ln_ffn_b"):
        sp[nm] = [a[nm][i:i + 1] for i in range(DEPTH)]

    loss_cols, grad_x, gs, dmod = _local_step(dm, a["x"][0], a["loss_target"][0], a["positions"][0], mod, sp, pp)
    loss = lax.psum(0.5 / D * jnp.sum(loss_cols), ("x", "y", "c"))
    out = {"loss": loss, "grad_x": grad_x[None]}

    def run(fn, *args, name, **kw):
        comm = pp.carry(name)
        res = fn(*args, name=name, comm=comm, **kw)
        if comm is not None:
            pp.carried(name, comm)
        return res

    pieces = [dmod.reshape(-1), gs["fox_b_f"].reshape(-1), _pad_lanes(gs["sinks"]).reshape(-1),
              jnp.stack(gs["conv_w"]).reshape(-1), jnp.stack(gs["conv_b"]).reshape(-1)]
    pieces += [jnp.stack(gs[nm]).reshape(-1) for nm in ("ln_mix_g", "ln_mix_b", "ln_ffn_g", "ln_ffn_b")]
    sizes = [p.shape[0] for p in pieces]
    packed = _pad_rows(jnp.concatenate(pieces))
    allp = _allgather_small(packed, "gather_small").reshape(N_DEV, packed.shape[0], LANES)
    tot = _sum_devices(allp, "sum_small").reshape(-1)
    offs = [sum(sizes[:k]) for k in range(len(sizes))]
    take = lambda k: tot[offs[k]:offs[k] + sizes[k]]
    g_small = {"ada_b": take(0).reshape(DEPTH, -1), "fox_b_f": take(1)[:FH].reshape(1, FH), "swa_sinks": take(2)[:QH].reshape(1, QH),
               "ffn_conv_w": lax.dynamic_slice_in_dim(take(3).reshape(DEPTH, 3, N_CHIPS * F2c), chip * F2c, F2c, axis=2),
               "ffn_conv_b": take(4).reshape(DEPTH, -1)}
    for k, nm in enumerate(("ln_mix_g", "ln_mix_b", "ln_ffn_g", "ln_ffn_b")):
        g_small[nm] = take(5 + k).reshape(DEPTH, D)
    small = list(g_small)
    pack = lambda pre: _pad_rows(jnp.concatenate([(a[pre + nm] if pre else a[nm]).reshape(-1) for nm in small]))
    gp = _pad_rows(jnp.concatenate([g_small[nm].reshape(-1) for nm in small]))
    ds_, ms_, vs_ = _adamw_small(pack(""), gp, pack("m_"), pack("v_"), "adamw_small")
    off = 0
    for nm in small:
        n_el = a[nm].size
        out["grad_" + nm] = g_small[nm]
        for pre, arr in (("delta_", ds_), ("new_m_", ms_), ("new_v_", vs_)):
            out[pre + nm] = arr.reshape(-1)[off:off + n_el].reshape(a[nm].shape)
        off += n_el

    dmod_all = allp.reshape(N_DEV, -1)[:, :DEPTH * N_CHIPS * NC].reshape(N_DEV, DEPTH, N_CHIPS * NC)
    dmod_mine = lax.dynamic_slice_in_dim(dmod_all, chip * NC, NC, axis=2).transpose(1, 0, 2)
    ada = _ada_grad_adamw(cact.T, dmod_mine, a["ada_w"], a["m_ada_w"], a["v_ada_w"], "ada_grad")
    for pre, arr in zip(("grad_", "delta_", "new_m_", "new_v_"), ada):
        out[pre + "ada_w"] = arr

    grads = pp.finish()
    upd = {}
    for k, nm, l in zip(keys[1:], names[1:], layers[1:]):
        upd[nm] = _adamw(a[nm], grads[k], a["m_" + nm], a["v_" + nm], l, upd.get(nm, ()), f"adamw_{k}")
    tview = lambda t: jnp.swapaxes(t, 1, 2)
    res = _adamw(tview(a["fox_w_in"]), grads["fox_w_in"].T, tview(a["m_fox_w_in"]), tview(a["v_fox_w_in"]), 0, (), "adamw_fox_w_in", by_cols=True)
    upd["fox_w_in"] = [tview(r) for r in res]
    for nm, res in upd.items():
        for pre, arr in zip(("grad_", "delta_", "new_m_", "new_v_"), res):
            out[pre + nm] = arr
    return out


_WEIGHTS = ["fox_w_in", "fox_b_f", "fox_w_o", "swa_w_in", "swa_sinks", "swa_w_o", "ada_w", "ada_b", "ffn_w_up", "ffn_conv_w",
            "ffn_conv_b", "ffn_w_down", "ln_mix_g", "ln_mix_b", "ln_ffn_g", "ln_ffn_b"]
_INPUTS = (["x", "c", "positions"] + _WEIGHTS + ["loss_target"] + ["m_" + w for w in _WEIGHTS] + ["v_" + w for w in _WEIGHTS])


def kernel(x, c, positions, fox_w_in, fox_b_f, fox_w_o, swa_w_in, swa_sinks, swa_w_o, ada_w, ada_b, ffn_w_up, ffn_conv_w, ffn_conv_b, ffn_w_down, ln_mix_g, ln_mix_b, ln_ffn_g, ln_ffn_b, loss_target, m_fox_w_in, m_fox_b_f, m_fox_w_o, m_swa_w_in, m_swa_sinks, m_swa_w_o, m_ada_w, m_ada_b, m_ffn_w_up, m_ffn_conv_w, m_ffn_conv_b, m_ffn_w_down, m_ln_mix_g, m_ln_mix_b, m_ln_ffn_g, m_ln_ffn_b, v_fox_w_in, v_fox_b_f, v_fox_w_o, v_swa_w_in, v_swa_sinks, v_swa_w_o, v_ada_w, v_ada_b, v_ffn_w_up, v_ffn_conv_w, v_ffn_conv_b, v_ffn_w_down, v_ln_mix_g, v_ln_mix_b, v_ln_ffn_g, v_ln_ffn_b):
    args = (x, c, positions, fox_w_in, fox_b_f, fox_w_o, swa_w_in, swa_sinks, swa_w_o, ada_w, ada_b, ffn_w_up, ffn_conv_w, ffn_conv_b, ffn_w_down, ln_mix_g, ln_mix_b, ln_ffn_g, ln_ffn_b, loss_target, m_fox_w_in, m_fox_b_f, m_fox_w_o, m_swa_w_in, m_swa_sinks, m_swa_w_o, m_ada_w, m_ada_b, m_ffn_w_up, m_ffn_conv_w, m_ffn_conv_b, m_ffn_w_down, m_ln_mix_g, m_ln_mix_b, m_ln_ffn_g, m_ln_ffn_b, v_fox_w_in, v_fox_b_f, v_fox_w_o, v_swa_w_in, v_swa_sinks, v_swa_w_o, v_ada_w, v_ada_b, v_ffn_w_up, v_ffn_conv_w, v_ffn_conv_b, v_ffn_w_down, v_ln_mix_g, v_ln_mix_b, v_ln_ffn_g, v_ln_ffn_b)
    out = _step(PROD, dict(zip(_INPUTS, args)))
    order = ["loss", "grad_x"] + [p + w for p in ("grad_", "delta_", "new_m_", "new_v_") for w in _WEIGHTS]
    return tuple(out[k] for k in order)
```

```python
import functools
from typing import NamedTuple

import jax
import jax.numpy as jnp
from jax import lax
from jax.experimental import pallas as pl
from jax.experimental.pallas import tpu as pltpu

F32 = jnp.float32
BF16 = jnp.bfloat16
MESH = pl.DeviceIdType.MESH
HIGHEST = lax.Precision.HIGHEST

N_CHIPS = 4
N_DEV = 8
LANES = 128
VMEM_LIMIT = 56 * 1024 * 1024

DEPTH = 2
DEEPNORM_ALPHA = (2.0 * DEPTH) ** 0.25
LN_EPS = 1e-5
ROPE_THETA = 500000.0
ADAM_LR, ADAM_B1, ADAM_B2, ADAM_EPS, ADAM_WD, ADAM_STEP = 0.001, 0.9, 0.999, 1e-08, 0.01, 10
NEG = -1e30


class Dims(NamedTuple):
    S: int
    D: int
    FH: int
    QH: int
    KH: int
    F: int


PROD = Dims(S=2048, D=2048, FH=16, QH=32, KH=4, F=5632)
FDH = 128
SDH = 64
WIN = 128
ROPE_DIM = 16
FOX_TQ = 256


def _params(sem=None, vmem=VMEM_LIMIT):
    return pltpu.CompilerParams(dimension_semantics=sem, vmem_limit_bytes=vmem)


def _tile(n, pref, unit=LANES):
    if n <= pref:
        return n
    t = (pref // unit) * unit
    while t > 0:
        if n % t == 0:
            return t
        t -= unit
    return n


class Comm:
    def __init__(self, args, out_shapes, aliases, n_sem, start, finish, members=()):
        self.args, self.out_shapes, self.aliases, self.n_sem = list(args), list(out_shapes), dict(aliases), n_sem
        self.start, self.finish = start, finish
        self.members = members
        self.results = None

    def set_results(self, res):
        self.results = list(res)
        for cm, o0 in self.members:
            cm.set_results(self.results[o0:o0 + len(cm.out_shapes)])


class _SemView:
    def __init__(self, sems, first):
        self.sems, self.first = sems, first

    @property
    def at(self):
        return self

    def __getitem__(self, k):
        return self.sems.at[self.first + k]


def _merge(comms):
    comms = [cm for cm in comms if cm is not None]
    if len(comms) < 2:
        return comms[0] if comms else None
    args, shapes, aliases, spans, n_sem = [], [], {}, [], 0
    for cm in comms:
        spans.append((len(args), len(shapes), n_sem))
        aliases.update({len(args) + a: len(shapes) + o for a, o in cm.aliases.items()})
        args += cm.args
        shapes += cm.out_shapes
        n_sem += cm.n_sem

    def each(step):
        def run(ar, ou, send, recv):
            for cm, (a0, o0, s0) in zip(comms, spans):
                getattr(cm, step)(ar[a0:a0 + len(cm.args)], ou[o0:o0 + len(cm.out_shapes)], _SemView(send, s0), _SemView(recv, s0))
        return run

    return Comm(args, shapes, aliases, n_sem, each("start"), each("finish"), [(cm, o0) for cm, (_, o0, _) in zip(comms, spans)])


def _place():
    x, y, c = lax.axis_index("x"), lax.axis_index("y"), lax.axis_index("c")
    chips = [(1 - x, y), (x, 1 - y), (1 - x, 1 - y)]
    return x, y, c, chips


def _remote(src, dst, send, recv, to):
    return pltpu.make_async_remote_copy(src_ref=src, dst_ref=dst, send_sem=send, recv_sem=recv, device_id=to, device_id_type=MESH)


def _any_specs(n):
    return [pl.BlockSpec(memory_space=pl.ANY)] * n


def _call(body, *, name, grid, in_specs, out_specs, out_shape, args, sem, scratch_shapes=(), aliases=None, comm=None):
    in_specs, out_specs, out_shape, scratch_shapes = list(in_specs), list(out_specs), list(out_shape), list(scratch_shapes)
    aliases = dict(aliases or {})
    if comm is None:
        return pl.pallas_call(body, name=name, grid=grid, in_specs=in_specs, out_specs=out_specs, out_shape=out_shape,
                              scratch_shapes=scratch_shapes, input_output_aliases=aliases, compiler_params=_params(sem))(*args)
    n_in, n_out, nc_in, nc_out, n_scr = len(in_specs), len(out_specs), len(comm.args), len(comm.out_shapes), len(scratch_shapes)

    def wrapped(*refs):
        ins, refs = refs[:n_in], refs[n_in:]
        cin, refs = refs[:nc_in], refs[nc_in:]
        outs, refs = refs[:n_out], refs[n_out:]
        cout, refs = refs[:nc_out], refs[nc_out:]
        scratch, (send, recv) = refs[:n_scr], refs[n_scr:]
        ids = [pl.program_id(k) for k in range(len(grid))]
        first = functools.reduce(jnp.logical_and, [i == 0 for i in ids])
        last = functools.reduce(jnp.logical_and, [i == g - 1 for i, g in zip(ids, grid)])

        @pl.when(first)
        def _():
            comm.start(cin, cout, send, recv)

        body(*ins, *outs, *scratch)

        @pl.when(last)
        def _():
            comm.finish(cin, cout, send, recv)

    res = pl.pallas_call(
        wrapped, name=name, grid=grid, in_specs=in_specs + _any_specs(nc_in), out_specs=out_specs + _any_specs(nc_out),
        out_shape=out_shape + comm.out_shapes,
        scratch_shapes=scratch_shapes + [pltpu.SemaphoreType.DMA((comm.n_sem,)), pltpu.SemaphoreType.DMA((comm.n_sem,))],
        input_output_aliases={**aliases, **{n_in + a: n_out + o for a, o in comm.aliases.items()}},
        compiler_params=_params(("arbitrary",) * len(grid)),
    )(*args, *comm.args)
    comm.set_results(res[n_out:])
    return list(res[:n_out])


def _run_comm(comm, name):
    nc_in, nc_out = len(comm.args), len(comm.out_shapes)

    def body(*refs):
        cin, cout, (send, recv) = refs[:nc_in], refs[nc_in:nc_in + nc_out], refs[nc_in + nc_out:]
        comm.start(cin, cout, send, recv)
        comm.finish(cin, cout, send, recv)

    res = pl.pallas_call(
        body, name=name, in_specs=_any_specs(nc_in), out_specs=_any_specs(nc_out), out_shape=comm.out_shapes,
        scratch_shapes=[pltpu.SemaphoreType.DMA((comm.n_sem,)), pltpu.SemaphoreType.DMA((comm.n_sem,))],
        input_output_aliases=comm.aliases,
    )(*comm.args)
    comm.set_results(res)


def _gather_comm(slots, chunks):
    n = len(slots)

    def rows(t, who):
        rh = slots[t].shape[1] // 2
        lo, hi, nch = chunks[t]
        rc = rh // nch
        return pl.ds(who * rh + lo * rc, (hi - lo) * rc)

    def start(args, outs, send, recv):
        x, y, c, chips = _place()
        s = 2 * x + y
        for t in range(n):
            mine = outs[t].at[s, rows(t, c)]
            for j, chip in enumerate(chips):
                _remote(mine, mine, send.at[6 * t + j], recv.at[6 * t + j], (*chip, c)).start()

    def finish(args, outs, send, recv):
        x, y, c, chips = _place()
        s = 2 * x + y
        sib = (x, y, 1 - c)
        for t in range(n):
            for j, chip in enumerate(chips):
                blk = outs[t].at[2 * chip[0] + chip[1], rows(t, c)]
                _remote(blk, blk, send.at[6 * t + j], recv.at[6 * t + j], (*chip, c)).wait_recv()
                _remote(blk, blk, send.at[6 * t + 3 + j], recv.at[6 * t + 3 + j], sib).start()
        for t in range(n):
            for j, chip in enumerate(chips):
                blk = outs[t].at[2 * chip[0] + chip[1], rows(t, 1 - c)]
                _remote(blk, blk, send.at[6 * t + 3 + j], recv.at[6 * t + 3 + j], sib).wait_recv()
        for t in range(n):
            mine = outs[t].at[s, rows(t, c)]
            for j, chip in enumerate(chips):
                _remote(mine, mine, send.at[6 * t + j], recv.at[6 * t + j], (*chip, c)).wait_send()
                blk = outs[t].at[2 * chip[0] + chip[1], rows(t, c)]
                _remote(blk, blk, send.at[6 * t + 3 + j], recv.at[6 * t + 3 + j], sib).wait_send()

    shapes = [jax.ShapeDtypeStruct(w.shape, w.dtype) for w in slots]
    return Comm(slots, shapes, {t: t for t in range(n)}, 6 * n, start, finish)


def _scatter_comm(parts, landed, chunks):
    n = len(parts)
    prev = [t for t in range(n) if landed[t] is not None]

    def rows(t):
        lo, hi, nch = chunks[t]
        rc = parts[t].shape[1] // nch
        return pl.ds(lo * rc, (hi - lo) * rc)

    def start(args, outs, send, recv):
        x, y, c, chips = _place()
        s = 2 * x + y
        for t in range(n):
            for j, chip in enumerate(chips):
                _remote(args[t].at[2 * chip[0] + chip[1], rows(t)], outs[t].at[s, rows(t)],
                        send.at[3 * t + j], recv.at[3 * t + j], (*chip, c)).start()

    def finish(args, outs, send, recv):
        x, y, c, chips = _place()
        for t in range(n):
            for j, chip in enumerate(chips):
                blk = outs[t].at[2 * chip[0] + chip[1], rows(t)]
                _remote(blk, blk, send.at[3 * t + j], recv.at[3 * t + j], (*chip, c)).wait_recv()
        for t in range(n):
            for j, chip in enumerate(chips):
                src = args[t].at[2 * chip[0] + chip[1], rows(t)]
                _remote(src, src, send.at[3 * t + j], recv.at[3 * t + j], (*chip, c)).wait_send()

    shapes = [jax.ShapeDtypeStruct(p.shape, p.dtype) for p in parts]
    return Comm(list(parts) + [landed[t] for t in prev], shapes, {n + i: t for i, t in enumerate(prev)}, 3 * n, start, finish)


def _swap_comm(gs):
    n = len(gs)

    def copy(args, outs, send, recv, t):
        _, _, c, _ = _place()
        rh = gs[t].shape[1] // 2
        x, y = lax.axis_index("x"), lax.axis_index("y")
        return _remote(args[t].at[:, pl.ds((1 - c) * rh, rh), :], outs[t], send.at[t], recv.at[t], (x, y, 1 - c))

    def start(args, outs, send, recv):
        for t in range(n):
            copy(args, outs, send, recv, t).start()

    def finish(args, outs, send, recv):
        for t in range(n):
            copy(args, outs, send, recv, t).wait()

    shapes = [jax.ShapeDtypeStruct((g.shape[0], g.shape[1] // 2, g.shape[2]), g.dtype) for g in gs]
    return Comm(gs, shapes, {}, n, start, finish)


def _join_comm(gs):
    n = len(gs)

    def half(outs, t, who):
        rh = gs[t].shape[0] // 2
        return outs[t].at[pl.ds(who * rh, rh), :]

    def start(args, outs, send, recv):
        x, y, c, _ = _place()
        for t in range(n):
            _remote(half(outs, t, c), half(outs, t, c), send.at[t], recv.at[t], (x, y, 1 - c)).start()

    def finish(args, outs, send, recv):
        x, y, c, _ = _place()
        for t in range(n):
            _remote(half(outs, t, 1 - c), half(outs, t, 1 - c), send.at[t], recv.at[t], (x, y, 1 - c)).wait_recv()
        for t in range(n):
            _remote(half(outs, t, c), half(outs, t, c), send.at[t], recv.at[t], (x, y, 1 - c)).wait_send()

    shapes = [jax.ShapeDtypeStruct(g.shape, g.dtype) for g in gs]
    return Comm(gs, shapes, {t: t for t in range(n)}, n, start, finish)


_DN = {"nn": (((1,), (0,)), ((), ())), "nt": (((1,), (1,)), ((), ())), "tn": (((0,), (0,)), ((), ()))}


def _mm(a, b, *, mode, out_dtype, name, out_groups=1, tm=1024, tn=1024, tk=2048, comm=None):
    ga, ra, ca = a.shape
    gb, rb, cb = b.shape
    if mode == "nn":
        M, K, N = ra, ga * ca, gb * cb
        assert rb == K and ga == 1 or (rb == K)
    elif mode == "nt":
        M, K, N = ra, ga * ca, rb
        assert gb * cb == K
    else:
        K, M, N = ra, ga * ca, gb * cb
        assert rb == K
    go = out_groups
    if mode == "nn":
        tk = _tile(ca, tk); assert rb % tk == 0 and (ga == 1 or True)
        tn = _tile(min(cb, N // go), tn); tm = _tile(M, tm, 8)
    elif mode == "nt":
        tk = _tile(ca, tk); tk = _tile(cb, tk) if cb % tk else tk; assert ca % tk == 0 and cb % tk == 0
        tn = _tile(N // go, tn); tm = _tile(M, tm, 8)
    else:
        tk = _tile(K, tk, 8); tm = _tile(ca, tm); tn = _tile(min(cb, N // go), tn)
    assert (N // go) % tn == 0 and M % tm == 0 and K % tk == 0, (name, M, N, K, tm, tn, tk)
    nk = K // tk
    kpa = max(ca // tk, 1)
    kpb = max(cb // tk, 1)
    npb = max(cb // tn, 1)
    npo = (N // go) // tn
    mpa = max(ca // tm, 1)

    if mode == "nn":
        a_spec = pl.BlockSpec((1, tm, tk), lambda j, i, k: (k // kpa, i, k % kpa))
        b_spec = pl.BlockSpec((1, tk, tn), lambda j, i, k: (j // npb, k, j % npb))
    elif mode == "nt":
        a_spec = pl.BlockSpec((1, tm, tk), lambda j, i, k: (k // kpa, i, k % kpa))
        b_spec = pl.BlockSpec((1, tn, tk), lambda j, i, k: (k // kpb, j, k % kpb))
    else:
        a_spec = pl.BlockSpec((1, tk, tm), lambda j, i, k: (i // mpa, k, i % mpa))
        b_spec = pl.BlockSpec((1, tk, tn), lambda j, i, k: (j // npb, k, j % npb))
    o_spec = pl.BlockSpec((1, tm, tn), lambda j, i, k: (j // npo, i, j % npo))
    dn = _DN[mode]

    def body(a_ref, b_ref, o_ref, *acc):
        p = lax.dot_general(a_ref[0], b_ref[0], dn, preferred_element_type=F32)
        if nk == 1:
            o_ref[0] = p.astype(out_dtype)
        else:
            k = pl.program_id(2)

            @pl.when(k == 0)
            def _():
                acc[0][...] = p

            @pl.when(k > 0)
            def _():
                acc[0][...] += p

            @pl.when(k == nk - 1)
            def _():
                o_ref[0] = acc[0][...].astype(out_dtype)

    return _call(
        body, name=name, grid=(N // tn, M // tm, nk), in_specs=[a_spec, b_spec], out_specs=[o_spec],
        out_shape=[jax.ShapeDtypeStruct((go, M, N // go), out_dtype)],
        scratch_shapes=[pltpu.VMEM((tm, tn), F32)] if nk > 1 else [],
        sem=("parallel", "parallel", "arbitrary"), args=(a, b), comm=comm)[0]


def _rows(tr, d):
    return pl.BlockSpec((tr, d), lambda i: (i, 0))


def _vec(d):
    return pl.BlockSpec((1, d), lambda i: (0, 0))


def _modulate(x, sc, sh, name):
    S, D = x.shape
    tr = min(256, S)

    def body(x_ref, sc_ref, sh_ref, h_ref):
        h_ref[...] = (x_ref[...] * (1.0 + sc_ref[...]) + sh_ref[...]).astype(BF16)

    return pl.pallas_call(
        body, name=name, grid=(S // tr,), in_specs=[_rows(tr, D), _vec(D), _vec(D)], out_specs=_rows(tr, D),
        out_shape=jax.ShapeDtypeStruct((S, D), BF16), compiler_params=_params(("parallel",)),
    )(x, sc, sh)


def _ln_fwd(x, y, gate, gamma, beta, sc, sh, name, comm=None):
    S, D = x.shape
    tr = min(256, S)
    emit_h = sc is not None

    def body(*refs):
        if emit_h:
            x_ref, y_ref, g_ref, ga_ref, be_ref, sc_ref, sh_ref, xo_ref, xh_ref, rs_ref, h_ref = refs
        else:
            x_ref, y_ref, g_ref, ga_ref, be_ref, xo_ref, xh_ref, rs_ref = refs
        z = DEEPNORM_ALPHA * x_ref[...] + (1.0 + g_ref[...]) * y_ref[...]
        mu = jnp.mean(z, axis=-1, keepdims=True)
        zc = z - mu
        var = jnp.mean(zc * zc, axis=-1, keepdims=True)
        rstd = lax.rsqrt(var + LN_EPS)
        xh = zc * rstd
        xo = xh * ga_ref[...] + be_ref[...]
        xo_ref[...] = xo
        xh_ref[...] = xh
        rs_ref[...] = rstd
        if emit_h:
            h_ref[...] = (xo * (1.0 + sc_ref[...]) + sh_ref[...]).astype(BF16)

    ins = [x, y, gate, gamma, beta] + ([sc, sh] if emit_h else [])
    in_specs = [_rows(tr, D), _rows(tr, D)] + [_vec(D)] * (len(ins) - 2)
    out_shape = [jax.ShapeDtypeStruct((S, D), F32), jax.ShapeDtypeStruct((S, D), F32), jax.ShapeDtypeStruct((S, 1), F32)]
    out_specs = [_rows(tr, D), _rows(tr, D), _rows(tr, 1)]
    if emit_h:
        out_shape.append(jax.ShapeDtypeStruct((S, D), BF16))
        out_specs.append(_rows(tr, D))
    return _call(body, name=name, grid=(S // tr,), in_specs=in_specs, out_specs=out_specs, out_shape=out_shape,
                 sem=("parallel",), args=ins, comm=comm)


def _loss_head(xf, tgt, name):
    S, D = xf.shape
    tr = min(256, S)

    def body(x_ref, t_ref, dx_ref, l_ref):
        e = x_ref[...] - t_ref[...]
        dx_ref[...] = e * (1.0 / D)

        @pl.when(pl.program_id(0) == 0)
        def _():
            l_ref[...] = jnp.zeros_like(l_ref)

        l_ref[...] += jnp.sum(e * e, axis=0, keepdims=True)

    return pl.pallas_call(
        body, name=name, grid=(S // tr,), in_specs=[_rows(tr, D), _rows(tr, D)],
        out_specs=[_rows(tr, D), _vec(D)],
        out_shape=[jax.ShapeDtypeStruct((S, D), F32), jax.ShapeDtypeStruct((1, D), F32)],
        compiler_params=_params(("arbitrary",)),
    )(xf, tgt)


def _ln_bwd(dxo, xh, rstd, gamma, y, gate, name, pre=None):
    S, D = dxo.shape
    tr = min(256, S)
    n_pre = 0 if pre is None else 3

    def body(dx_ref, xh_ref, rs_ref, ga_ref, y_ref, g_ref, *rest):
        dres_ref, dy_ref, dga_ref, dbe_ref, dg_ref = rest[n_pre:n_pre + 5]
        first = pl.program_id(0) == 0
        dxo_ = dx_ref[...]
        xh_ = xh_ref[...]
        if pre is not None:
            dh_ref, sc_ref, be_ref = rest[:3]
            dsc_ref, dsh_ref = rest[n_pre + 5:]
            dh_ = dh_ref[...]
            dxo_ = dxo_ + dh_ * (1.0 + sc_ref[...])

            @pl.when(first)
            def _():
                dsc_ref[...] = jnp.zeros_like(dsc_ref)
                dsh_ref[...] = jnp.zeros_like(dsh_ref)

            dsc_ref[...] += jnp.sum(dh_ * (xh_ * ga_ref[...] + be_ref[...]), axis=0, keepdims=True)
            dsh_ref[...] += jnp.sum(dh_, axis=0, keepdims=True)
        dxh = dxo_ * ga_ref[...]
        m1 = jnp.mean(dxh, axis=-1, keepdims=True)
        m2 = jnp.mean(dxh * xh_, axis=-1, keepdims=True)
        dz = rs_ref[...] * (dxh - m1 - xh_ * m2)
        dres_ref[...] = DEEPNORM_ALPHA * dz
        dy_ref[...] = ((1.0 + g_ref[...]) * dz).astype(BF16)

        @pl.when(first)
        def _():
            dga_ref[...] = jnp.zeros_like(dga_ref)
            dbe_ref[...] = jnp.zeros_like(dbe_ref)
            dg_ref[...] = jnp.zeros_like(dg_ref)

        dga_ref[...] += jnp.sum(dxo_ * xh_, axis=0, keepdims=True)
        dbe_ref[...] += jnp.sum(dxo_, axis=0, keepdims=True)
        dg_ref[...] += jnp.sum(dz * y_ref[...], axis=0, keepdims=True)

    extra_in = [] if pre is None else [_rows(tr, D), _vec(D), _vec(D)]
    return pl.pallas_call(
        body, name=name, grid=(S // tr,),
        in_specs=[_rows(tr, D), _rows(tr, D), _rows(tr, 1), _vec(D), _rows(tr, D), _vec(D)] + extra_in,
        out_specs=[_rows(tr, D), _rows(tr, D)] + [_vec(D)] * (3 + (0 if pre is None else 2)),
        out_shape=[jax.ShapeDtypeStruct((S, D), F32), jax.ShapeDtypeStruct((S, D), BF16)]
        + [jax.ShapeDtypeStruct((1, D), F32)] * (3 + (0 if pre is None else 2)),
        compiler_params=_params(("arbitrary",)),
    )(dxo, xh, rstd, gamma, y, gate, *(pre or ()))


def _mod_bwd(dh, x, sc, dres, name):
    S, D = x.shape
    tr = min(256, S)

    def body(dh_ref, x_ref, sc_ref, dr_ref, dx_ref, dsc_ref, dsh_ref):
        dh_ = dh_ref[...]
        dx_ref[...] = dr_ref[...] + dh_ * (1.0 + sc_ref[...])

        @pl.when(pl.program_id(0) == 0)
        def _():
            dsc_ref[...] = jnp.zeros_like(dsc_ref)
            dsh_ref[...] = jnp.zeros_like(dsh_ref)

        dsc_ref[...] += jnp.sum(dh_ * x_ref[...], axis=0, keepdims=True)
        dsh_ref[...] += jnp.sum(dh_, axis=0, keepdims=True)

    return pl.pallas_call(
        body, name=name, grid=(S // tr,),
        in_specs=[_rows(tr, D), _rows(tr, D), _vec(D), _rows(tr, D)],
        out_specs=[_rows(tr, D), _vec(D), _vec(D)],
        out_shape=[jax.ShapeDtypeStruct((S, D), F32), jax.ShapeDtypeStruct((1, D), F32), jax.ShapeDtypeStruct((1, D), F32)],
        compiler_params=_params(("arbitrary",)),
    )(dh, x, sc, dres)


def _log_sigmoid(z):
    return jnp.minimum(z, 0.0) - jnp.log(1.0 + jnp.exp(-jnp.abs(z)))


def _fox_gate_fwd(proj, b_f, n_heads, name):
    S, PW = proj.shape
    blk = min(256, S)
    last = PW // LANES - 1

    def body(fl_ref, b_ref, cum_ref):
        r = lax.broadcasted_iota(jnp.int32, (blk, blk), 0)
        c = lax.broadcasted_iota(jnp.int32, (blk, blk), 1)
        tril = (c <= r).astype(F32)
        carry = jnp.zeros((1, LANES), F32)
        for i in range(S // blk):
            lf = _log_sigmoid(fl_ref[i * blk:(i + 1) * blk, :] + b_ref[...])
            cum_ref[i * blk:(i + 1) * blk, :] = jnp.dot(tril, lf, preferred_element_type=F32, precision=HIGHEST) + carry
            carry = carry + jnp.sum(lf, axis=0, keepdims=True)

    return pl.pallas_call(
        body, name=name, grid=(1,),
        in_specs=[pl.BlockSpec((S, LANES), lambda i: (0, last)), pl.BlockSpec((1, LANES), lambda i: (0, 0))],
        out_specs=pl.BlockSpec((S, LANES), lambda i: (0, 0)),
        out_shape=jax.ShapeDtypeStruct((S, LANES), F32), compiler_params=_params(("arbitrary",)),
    )(proj, b_f)


def _fox_gate_bwd(dcum, proj, b_f, n_heads, name):
    S, PW = proj.shape
    blk = min(256, S)
    last = PW // LANES - 1
    nb = S // blk

    def body(dc_ref, fl_ref, b_ref, dfl_ref, db_ref):
        r = lax.broadcasted_iota(jnp.int32, (blk, blk), 0)
        c = lax.broadcasted_iota(jnp.int32, (blk, blk), 1)
        triu = (c >= r).astype(F32)
        lane = lax.broadcasted_iota(jnp.int32, (blk, LANES), 1)
        carry = jnp.zeros((1, LANES), F32)
        dbs = jnp.zeros((1, LANES), F32)
        for i in reversed(range(nb)):
            dc = dc_ref[i * blk:(i + 1) * blk, :]
            dlf = jnp.dot(triu, dc, preferred_element_type=F32, precision=HIGHEST) + carry
            carry = carry + jnp.sum(dc, axis=0, keepdims=True)
            z = fl_ref[i * blk:(i + 1) * blk, :] + b_ref[...]
            e = jnp.exp(-jnp.abs(z))
            sig_neg = jnp.where(z >= 0, e / (1.0 + e), 1.0 / (1.0 + e))
            dfl = jnp.where(lane < n_heads, dlf * sig_neg, 0.0)
            dfl_ref[i * blk:(i + 1) * blk, :] = dfl.astype(BF16)
            dbs = dbs + jnp.sum(dfl, axis=0, keepdims=True)
        db_ref[...] = dbs

    return pl.pallas_call(
        body, name=name, grid=(1,),
        in_specs=[pl.BlockSpec((S, LANES), lambda i: (0, 0)), pl.BlockSpec((S, LANES), lambda i: (0, last)),
                  pl.BlockSpec((1, LANES), lambda i: (0, 0))],
        out_specs=[pl.BlockSpec((S, LANES), lambda i: (0, 0)), pl.BlockSpec((1, LANES), lambda i: (0, 0))],
        out_shape=[jax.ShapeDtypeStruct((S, LANES), BF16), jax.ShapeDtypeStruct((1, LANES), F32)],
        compiler_params=_params(("arbitrary",)),
    )(dcum, proj, b_f)


def _fox_scores(q_ref, kb_ref, cq_ref, ck_ref, qi, tq, scale):
    kk = (qi + 1) * tq
    rows = slice(qi * tq, (qi + 1) * tq)
    qb = q_ref[rows, :].astype(BF16)
    s = lax.dot_general(qb, kb_ref[0:kk, :], _DN["nt"], preferred_element_type=F32) * scale
    s = s + (cq_ref[0, rows, :] - ck_ref[0, :, 0:kk])
    r = lax.broadcasted_iota(jnp.int32, (tq, kk), 0) + qi * tq
    c = lax.broadcasted_iota(jnp.int32, (tq, kk), 1)
    mask = c <= r
    return jnp.where(mask, s, NEG), mask, qb


def _fox_fwd(proj, cq, ck, n_heads, name, comm=None):
    S = proj.shape[0]
    H = n_heads
    tq = min(FOX_TQ, S)
    nq = S // tq
    scale = FDH ** -0.5

    def body(q_ref, k_ref, v_ref, cq_ref, ck_ref, o_ref, lse_ref, kb_ref, vb_ref):
        kb_ref[...] = k_ref[...].astype(BF16)
        vb_ref[...] = v_ref[...].astype(BF16)
        for qi in range(nq):
            kk = (qi + 1) * tq
            rows = slice(qi * tq, (qi + 1) * tq)
            s, _, _ = _fox_scores(q_ref, kb_ref, cq_ref, ck_ref, qi, tq, scale)
            m = jnp.max(s, axis=-1, keepdims=True)
            p = jnp.exp(s - m)
            l = jnp.sum(p, axis=-1, keepdims=True)
            p = p * (1.0 / l)
            o_ref[rows, :] = jnp.dot(p.astype(BF16), vb_ref[0:kk, :], preferred_element_type=F32).astype(BF16)
            lse_ref[0, rows, :] = m + jnp.log(l)

    col = lambda off: pl.BlockSpec((S, FDH), lambda h: (0, h + off))
    stat_c = pl.BlockSpec((1, S, 1), lambda h: (h, 0, 0))
    stat_r = pl.BlockSpec((1, 1, S), lambda h: (h, 0, 0))
    return _call(
        body, name=name, grid=(H,),
        in_specs=[col(0), col(H), col(2 * H), stat_c, stat_r],
        out_specs=[col(0), stat_c],
        out_shape=[jax.ShapeDtypeStruct((S, H * FDH), BF16), jax.ShapeDtypeStruct((H, S, 1), F32)],
        scratch_shapes=[pltpu.VMEM((S, FDH), BF16), pltpu.VMEM((S, FDH), BF16)],
        sem=("parallel",), args=(proj, proj, proj, cq, ck), comm=comm)


def _fox_bwd(proj, cq, ck, lse, do, n_heads, name, comm=None):
    S = proj.shape[0]
    H = n_heads
    tq = min(FOX_TQ, S)
    nq = S // tq
    scale = FDH ** -0.5

    def body(q_ref, k_ref, v_ref, cq_ref, ck_ref, lse_ref, do_ref, dq_ref, dk_ref, dv_ref, dcq_ref, dck_ref,
             kb_ref, vb_ref, dka_ref, dva_ref):
        kb_ref[...] = k_ref[...].astype(BF16)
        vb_ref[...] = v_ref[...].astype(BF16)
        dka_ref[...] = jnp.zeros_like(dka_ref)
        dva_ref[...] = jnp.zeros_like(dva_ref)
        dck_ref[...] = jnp.zeros_like(dck_ref)
        for qi in range(nq):
            kk = (qi + 1) * tq
            rows = slice(qi * tq, (qi + 1) * tq)
            s, mask, qb = _fox_scores(q_ref, kb_ref, cq_ref, ck_ref, qi, tq, scale)
            p = jnp.where(mask, jnp.exp(s - lse_ref[0, rows, :]), 0.0)
            dob = do_ref[rows, :]
            dp = lax.dot_general(dob, vb_ref[0:kk, :], _DN["nt"], preferred_element_type=F32)
            delta = jnp.sum(p * dp, axis=-1, keepdims=True)
            ds = p * (dp - delta)
            dcq_ref[0, rows, :] = jnp.sum(ds, axis=-1, keepdims=True)
            dck_ref[0, :, 0:kk] -= jnp.sum(ds, axis=0, keepdims=True)
            dsb = (ds * scale).astype(BF16)
            dq_ref[rows, :] = jnp.dot(dsb, kb_ref[0:kk, :], preferred_element_type=F32).astype(BF16)
            dka_ref[0:kk, :] += lax.dot_general(dsb, qb, _DN["tn"], preferred_element_type=F32)
            dva_ref[0:kk, :] += lax.dot_general(p.astype(BF16), dob, _DN["tn"], preferred_element_type=F32)
        dk_ref[...] = dka_ref[...].astype(BF16)
        dv_ref[...] = dva_ref[...].astype(BF16)

    col = lambda off: pl.BlockSpec((S, FDH), lambda h: (0, h + off))
    stat_c = pl.BlockSpec((1, S, 1), lambda h: (h, 0, 0))
    stat_r = pl.BlockSpec((1, 1, S), lambda h: (h, 0, 0))
    wide = jax.ShapeDtypeStruct((S, H * FDH), BF16)
    return _call(
        body, name=name, grid=(H,),
        in_specs=[col(0), col(H), col(2 * H), stat_c, stat_r, stat_c, col(0)],
        out_specs=[col(0), col(0), col(0), stat_c, stat_r],
        out_shape=[wide, wide, wide, jax.ShapeDtypeStruct((H, S, 1), F32), jax.ShapeDtypeStruct((H, 1, S), F32)],
        scratch_shapes=[pltpu.VMEM((S, FDH), BF16), pltpu.VMEM((S, FDH), BF16), pltpu.VMEM((S, FDH), F32), pltpu.VMEM((S, FDH), F32)],
        sem=("parallel",), args=(proj, proj, proj, cq, ck, lse, do), comm=comm)


def _rope_tables(pos, sign):
    inv = ROPE_THETA ** (-jnp.arange(0, ROPE_DIM, 2, dtype=F32) / ROPE_DIM)
    ang = pos.astype(F32)[:, None] * inv
    cos, sin = jnp.cos(ang), sign * jnp.sin(ang)
    l64 = jnp.arange(LANES) % SDH
    idx = l64 % (ROPE_DIM // 2)
    c = jnp.where(l64 < ROPE_DIM, cos[:, idx], 1.0)
    sa = jnp.where(l64 < ROPE_DIM // 2, -sin[:, idx], 0.0)
    sb = jnp.where((l64 >= ROPE_DIM // 2) & (l64 < ROPE_DIM), sin[:, idx], 0.0)
    rot = jnp.stack([c, sa, sb])
    ident = jnp.stack([jnp.ones_like(c), jnp.zeros_like(c), jnp.zeros_like(c)])
    return jnp.stack([rot, ident]).astype(F32)


def _rope(xin, tabs, n_rot, out_dtype, name, comm=None):
    S, W = xin.shape

    def body(x_ref, t_ref, o_ref):
        xv = x_ref[...]
        o = xv * t_ref[0, 0] + pltpu.roll(xv, LANES - ROPE_DIM // 2, 1) * t_ref[0, 1] + pltpu.roll(xv, ROPE_DIM // 2, 1) * t_ref[0, 2]
        o_ref[...] = o.astype(out_dtype)

    return _call(
        body, name=name, grid=(W // LANES,),
        in_specs=[pl.BlockSpec((S, LANES), lambda j: (0, j)),
                  pl.BlockSpec((1, 3, S, LANES), lambda j: (jnp.where(j < n_rot, 0, 1), 0, 0, 0))],
        out_specs=[pl.BlockSpec((S, LANES), lambda j: (0, j))],
        out_shape=[jax.ShapeDtypeStruct((S, W), out_dtype)], sem=("parallel",), args=(xin, tabs), comm=comm)[0]


SWA_PER_STEP = 2


def _swa_bias():
    r = jnp.arange(WIN)[:, None]
    c = jnp.arange(2 * WIN)[None, :]
    first = c <= r
    later = (c > r) & (c <= r + WIN)
    return jnp.where(jnp.stack([first, later]), 0.0, NEG).astype(F32)


def _swa_probs(q_ref, k_ref, sk_ref, b_ref, n, j, scale):
    st = pl.multiple_of(jnp.maximum(n - 1, 0) * WIN, WIN)
    qb = q_ref[0, j]
    kb = k_ref[0, pl.ds(st, 2 * WIN), :]
    gm = qb.shape[0]
    s = lax.dot_general(qb, kb, _DN["nt"], preferred_element_type=F32) * scale
    s = (s.reshape(gm // WIN, WIN, 2 * WIN) + b_ref[jnp.minimum(n, 1)][None]).reshape(gm, 2 * WIN)
    sink = sk_ref[0]
    m = jnp.maximum(jnp.max(s, axis=-1, keepdims=True), sink)
    e = jnp.exp(s - m)
    es = jnp.exp(sink - m)
    inv = 1.0 / (jnp.sum(e, axis=-1, keepdims=True) + es)
    return e * inv, es * inv, st, qb, kb


def _swa_specs(S, gm):
    blk = pl.BlockSpec((1, SWA_PER_STEP, gm, SDH), lambda g, n: (g, n, 0, 0))
    kv = pl.BlockSpec((1, S, SDH), lambda g, n: (g, 0, 0))
    col = pl.BlockSpec((1, gm, 1), lambda g, n: (g, 0, 0))
    bias = pl.BlockSpec((2, WIN, 2 * WIN), lambda g, n: (0, 0, 0))
    return blk, kv, col, bias


def _swa_fwd(q, k, v, sinks, name, comm=None):
    KH, nb, gm, _ = q.shape
    S = k.shape[1]
    scale = SDH ** -0.5

    def body(q_ref, k_ref, v_ref, sk_ref, b_ref, o_ref):
        for j in range(SWA_PER_STEP):
            p, _, st, _, _ = _swa_probs(q_ref, k_ref, sk_ref, b_ref, pl.program_id(1) * SWA_PER_STEP + j, j, scale)
            vb = v_ref[0, pl.ds(st, 2 * WIN), :]
            o_ref[0, j] = jnp.dot(p.astype(BF16), vb, preferred_element_type=F32).astype(BF16)

    blk, kv, col, bias = _swa_specs(S, gm)
    return _call(
        body, name=name, grid=(KH, nb // SWA_PER_STEP), in_specs=[blk, kv, kv, col, bias], out_specs=[blk],
        out_shape=[jax.ShapeDtypeStruct(q.shape, BF16)], sem=("parallel", "parallel"), args=(q, k, v, sinks, _swa_bias()), comm=comm)[0]


def _swa_bwd(q, k, v, sinks, do, name, comm=None):
    KH, nb, gm, _ = q.shape
    S = k.shape[1]
    scale = SDH ** -0.5

    def body(q_ref, k_ref, v_ref, sk_ref, b_ref, do_ref, dq_ref, dk_ref, dv_ref, dsk_ref):
        @pl.when(pl.program_id(1) == 0)
        def _():
            dk_ref[...] = jnp.zeros_like(dk_ref)
            dv_ref[...] = jnp.zeros_like(dv_ref)
            dsk_ref[...] = jnp.zeros_like(dsk_ref)

        blocks = []
        for j in range(SWA_PER_STEP):
            p, ps, st, qb, kb = _swa_probs(q_ref, k_ref, sk_ref, b_ref, pl.program_id(1) * SWA_PER_STEP + j, j, scale)
            vb = v_ref[0, pl.ds(st, 2 * WIN), :]
            dob = do_ref[0, j]
            dp = lax.dot_general(dob, vb, _DN["nt"], preferred_element_type=F32)
            delta = jnp.sum(p * dp, axis=-1, keepdims=True)
            dsb = (p * (dp - delta) * scale).astype(BF16)
            dq_ref[0, j] = jnp.dot(dsb, kb, preferred_element_type=F32)
            blocks.append((st, lax.dot_general(dsb, qb, _DN["tn"], preferred_element_type=F32),
                           lax.dot_general(p.astype(BF16), dob, _DN["tn"], preferred_element_type=F32), ps * delta))
        for st, dk, dv, dsk in blocks:
            dk_ref[0, pl.ds(st, 2 * WIN), :] += dk
            dv_ref[0, pl.ds(st, 2 * WIN), :] += dv
            dsk_ref[0] -= dsk

    blk, kv, col, bias = _swa_specs(S, gm)
    return _call(
        body, name=name, grid=(KH, nb // SWA_PER_STEP), in_specs=[blk, kv, kv, col, bias, blk], out_specs=[blk, kv, kv, col],
        out_shape=[jax.ShapeDtypeStruct(q.shape, F32), jax.ShapeDtypeStruct(k.shape, F32),
                   jax.ShapeDtypeStruct(k.shape, F32), jax.ShapeDtypeStruct(sinks.shape, F32)],
        sem=("parallel", "arbitrary"), args=(q, k, v, sinks, _swa_bias(), do), comm=comm)


def _shift_down(u, k):
    row = lax.broadcasted_iota(jnp.int32, u.shape, 0)
    return jnp.where(row >= k, pltpu.roll(u, k, 0), 0.0)


def _shift_up(u, k):
    n = u.shape[0]
    row = lax.broadcasted_iota(jnp.int32, u.shape, 0)
    return jnp.where(row < n - k, pltpu.roll(u, n - k, 0), 0.0)


def _conv3(u, w_ref, b_ref):
    return w_ref[0:1, :] * _shift_down(u, 2) + w_ref[1:2, :] * _shift_down(u, 1) + w_ref[2:3, :] * u + b_ref[...]


def _conv_gate(u, cw, cb, name, comm=None):
    S, F2 = u.shape
    Fh = F2 // 2
    tc = _tile(Fh, 256)
    nf = Fh // tc

    def body(ug_ref, uv_ref, wg_ref, wv_ref, bg_ref, bv_ref, a_ref):
        g = _conv3(ug_ref[...], wg_ref, bg_ref)
        val = _conv3(uv_ref[...], wv_ref, bv_ref)
        a_ref[...] = (g * (1.0 / (1.0 + jnp.exp(-g))) * val).astype(BF16)

    blk = lambda r, off: pl.BlockSpec((r, tc), lambda j: (0, j + off))
    return _call(
        body, name=name, grid=(nf,),
        in_specs=[blk(S, 0), blk(S, nf), blk(3, 0), blk(3, nf), blk(1, 0), blk(1, nf)], out_specs=[blk(S, 0)],
        out_shape=[jax.ShapeDtypeStruct((S, Fh), BF16)], sem=("parallel",), args=(u, u, cw, cw, cb, cb), comm=comm)[0]


def _conv_gate_bwd(u, da, cw, cb, name, comm=None):
    S, F2 = u.shape
    Fh = F2 // 2
    tc = _tile(Fh, 256)
    nf = Fh // tc

    def half(h, dx, uu, w_ref, du_ref, dw_ref, db_ref):
        up1, up2 = _shift_up(dx, 1), _shift_up(dx, 2)
        du = w_ref[2:3, :] * dx + w_ref[1:2, :] * up1 + w_ref[0:1, :] * up2
        du_ref[h] = du.astype(BF16)
        dw_ref[h, 0:1, :] = jnp.sum(up2 * uu, axis=0, keepdims=True)
        dw_ref[h, 1:2, :] = jnp.sum(up1 * uu, axis=0, keepdims=True)
        dw_ref[h, 2:3, :] = jnp.sum(dx * uu, axis=0, keepdims=True)
        db_ref[h] = jnp.sum(dx, axis=0, keepdims=True)

    def body(ug_ref, uv_ref, da_ref, wg_ref, wv_ref, bg_ref, bv_ref, du_ref, dw_ref, db_ref):
        ug = ug_ref[...]
        uv = uv_ref[...]
        g = _conv3(ug, wg_ref, bg_ref)
        val = _conv3(uv, wv_ref, bv_ref)
        sig = 1.0 / (1.0 + jnp.exp(-g))
        da_ = da_ref[...]
        dg = da_ * val * (sig * (1.0 + g * (1.0 - sig)))
        dval = da_ * (g * sig)
        half(0, dg, ug, wg_ref, du_ref, dw_ref, db_ref)
        half(1, dval, uv, wv_ref, du_ref, dw_ref, db_ref)

    blk = lambda r, off: pl.BlockSpec((r, tc), lambda j: (0, j + off))
    both = lambda r: pl.BlockSpec((2, r, tc), lambda j: (0, 0, j))
    return _call(
        body, name=name, grid=(nf,),
        in_specs=[blk(S, 0), blk(S, nf), blk(S, 0), blk(3, 0), blk(3, nf), blk(1, 0), blk(1, nf)],
        out_specs=[both(S), both(3), both(1)],
        out_shape=[jax.ShapeDtypeStruct((2, S, Fh), BF16), jax.ShapeDtypeStruct((2, 3, Fh), F32), jax.ShapeDtypeStruct((2, 1, Fh), F32)],
        sem=("parallel",), args=(u, u, da, cw, cw, cb, cb), comm=comm)


def _to_groups(t, kh):
    S, width = t.shape
    g = width // SDH // kh
    return t.reshape(S // WIN, WIN, kh, g, SDH).transpose(2, 0, 3, 1, 4).reshape(kh, S // WIN, g * WIN, SDH)


def _from_groups(t):
    kh, nb, gm, _ = t.shape
    g = gm // WIN
    return t.reshape(kh, nb, g, WIN, SDH).transpose(1, 3, 0, 2, 4).reshape(nb * WIN, kh * g * SDH)


class LocalWeights:
    def __init__(self, weights):
        self.weights, self.grads = weights, {}

    def w(self, name):
        return self.weights[name]

    def carry(self, stage):
        return None

    def carried(self, stage, comm):
        pass

    def grad(self, name, g):
        self.grads[name] = g


def _local_step(dm, x, tgt, pos, mod, sp, pp):
    S, D, FH, QH, KH, Fh = dm
    m = [[mod[i:i + 1, j * D:(j + 1) * D] for j in range(6)] for i in range(DEPTH)]

    def run(fn, *args, name, **kw):
        comm = pp.carry(name)
        out = fn(*args, name=name, comm=comm, **kw)
        if comm is not None:
            pp.carried(name, comm)
        return out

    sv = []
    xs = x
    h = _modulate(xs, m[0][1], m[0][0], "mod_in")
    for i in range(DEPTH):
        sh1, sc1, g1, sh2, sc2, g2 = m[i]
        L = {}
        L["x_in"], L["h1"] = xs, h
        if i == 0:
            proj = run(_mm, h[None], pp.w("fox_w_in"), mode="nn", out_dtype=F32, name="fox_proj", tn=896)[0]
            cum = _fox_gate_fwd(proj, sp["fox_b_f"], FH, "fox_gate")
            cq = cum[:, :FH].T[:, :, None]
            ck = cum[:, :FH].T[:, None, :]
            o, lse = run(_fox_fwd, proj, cq, ck, FH, name="fox_attn")
            L.update(proj=proj, cq=cq, ck=ck, lse=lse, o=o)
            y = run(_mm, o[None], pp.w("fox_w_o"), mode="nn", out_dtype=F32, name="fox_out")[0]
        else:
            proj = run(_mm, h[None], pp.w("swa_w_in"), mode="nn", out_dtype=F32, name="swa_proj", tn=640)[0]
            tabs = _rope_tables(pos, 1.0)
            n_rot = (QH + KH) * SDH // LANES
            pr = run(_rope, proj, tabs, n_rot, BF16, name="swa_rope")
            qh = _to_groups(pr[:, :QH * SDH], KH)
            kh = pr[:, QH * SDH:(QH + KH) * SDH].reshape(S, KH, SDH).transpose(1, 0, 2)
            vh = pr[:, (QH + KH) * SDH:].reshape(S, KH, SDH).transpose(1, 0, 2)
            oh = run(_swa_fwd, qh, kh, vh, sp["sinks"], name="swa_attn")
            o = _from_groups(oh)
            L.update(qh=qh, kh=kh, vh=vh, o=o)
            y = run(_mm, o[None], pp.w("swa_w_o"), mode="nn", out_dtype=F32, name="swa_out")[0]
        L["y1"] = y
        x1, L["xh1"], L["rs1"], h2 = run(_ln_fwd, xs, y, g1, sp["ln_mix_g"][i], sp["ln_mix_b"][i], sc2, sh2, name=f"ln_mix{i}")
        L["x1"], L["h2"] = x1, h2
        u = run(_mm, h2[None], pp.w(f"ffn_w_up{i}"), mode="nn", out_dtype=F32, name=f"ffn_up{i}", tm=512, tn=1408)[0]
        a = run(_conv_gate, u, sp["conv_w"][i], sp["conv_b"][i], name=f"ffn_gate{i}")
        y2 = run(_mm, a[None], pp.w(f"ffn_w_down{i}"), mode="nn", out_dtype=F32, name=f"ffn_down{i}", tk=5632, tm=512)[0]
        L.update(u=u, a=a, y2=y2)
        if i + 1 < DEPTH:
            xs, L["xh2"], L["rs2"], h = run(_ln_fwd, x1, y2, g2, sp["ln_ffn_g"][i], sp["ln_ffn_b"][i], m[i + 1][1], m[i + 1][0], name=f"ln_ffn{i}")
        else:
            xs, L["xh2"], L["rs2"] = run(_ln_fwd, x1, y2, g2, sp["ln_ffn_g"][i], sp["ln_ffn_b"][i], None, None, name=f"ln_ffn{i}")
        sv.append(L)

    dx, loss_cols = _loss_head(xs, tgt, "loss_head")

    gs = {k: [None] * DEPTH for k in ("conv_w", "conv_b", "ln_mix_g", "ln_mix_b", "ln_ffn_g", "ln_ffn_b")}
    dmp = [dict() for _ in range(DEPTH)]
    dres, pend = dx, None
    for i in reversed(range(DEPTH)):
        sh1, sc1, g1, sh2, sc2, g2 = m[i]
        L = sv[i]
        res = _ln_bwd(dres, L["xh2"], L["rs2"], sp["ln_ffn_g"][i], L["y2"], g2, f"ln_ffn_bwd{i}",
                      None if pend is None else (*pend, sp["ln_ffn_b"][i]))
        dres, dy, gs["ln_ffn_g"][i], gs["ln_ffn_b"][i], dmp[i]["g2"] = res[:5]
        if pend is not None:
            dmp[i + 1]["sc1"], dmp[i + 1]["sh1"] = res[5:]
        da = run(_mm, dy[None], pp.w(f"ffn_w_down{i}"), mode="nt", out_dtype=F32, name=f"ffn_down_dx{i}", tm=512, tn=1408)[0]
        pp.grad(f"ffn_w_down{i}", run(_mm, L["a"][None], dy[None], mode="tn", out_dtype=BF16, name=f"ffn_down_dw{i}", tm=1408))
        du, dcw, dcb = run(_conv_gate_bwd, L["u"], da, sp["conv_w"][i], sp["conv_b"][i], name=f"ffn_gate_bwd{i}")
        gs["conv_w"][i] = dcw.transpose(1, 0, 2).reshape(3, 2 * Fh)
        gs["conv_b"][i] = dcb.transpose(1, 0, 2).reshape(1, 2 * Fh)
        dh2 = run(_mm, du, pp.w(f"ffn_w_up{i}"), mode="nt", out_dtype=F32, name=f"ffn_up_dx{i}", tk=2816)[0]
        pp.grad(f"ffn_w_up{i}", run(_mm, L["h2"][None], du, mode="tn", out_dtype=BF16, name=f"ffn_up_dw{i}", out_groups=N_CHIPS, tn=1408))
        dres, dy, gs["ln_mix_g"][i], gs["ln_mix_b"][i], dmp[i]["g1"], dmp[i]["sc2"], dmp[i]["sh2"] = _ln_bwd(
            dres, L["xh1"], L["rs1"], sp["ln_mix_g"][i], L["y1"], g1, f"ln_mix_bwd{i}", (dh2, sc2, sp["ln_mix_b"][i]))
        if i == 0:
            do = run(_mm, dy[None], pp.w("fox_w_o"), mode="nt", out_dtype=BF16, name="fox_out_dx")[0]
            pp.grad("fox_w_o", run(_mm, L["o"][None], dy[None], mode="tn", out_dtype=BF16, name="fox_out_dw"))
            dq, dk, dv, dcq, dck = run(_fox_bwd, L["proj"], L["cq"], L["ck"], L["lse"], do, FH, name="fox_attn_bwd")
            dcum = dcq[:, :, 0].T + dck[:, 0, :].T
            dcum = jnp.pad(dcum, ((0, 0), (0, LANES - FH)))
            dfl, db_f = _fox_gate_bwd(dcum, L["proj"], sp["fox_b_f"], FH, "fox_gate_bwd")
            gs["fox_b_f"] = db_f
            dproj = jnp.concatenate([dq, dk, dv, dfl], axis=1)
            pp.grad("fox_w_in", run(_mm, L["h1"][None], dproj[None], mode="tn", out_dtype=BF16, name="fox_proj_dw", tn=896))
            dh1 = run(_mm, dproj[None], pp.w("fox_w_in"), mode="nt", out_dtype=F32, name="fox_proj_dx", tk=6272, tm=512)[0]
        else:
            do = run(_mm, dy[None], pp.w("swa_w_o"), mode="nt", out_dtype=BF16, name="swa_out_dx")[0]
            pp.grad("swa_w_o", run(_mm, L["o"][None], dy[None], mode="tn", out_dtype=BF16, name="swa_out_dw"))
            dqh, dkh, dvh, dsk = run(_swa_bwd, L["qh"], L["kh"], L["vh"], sp["sinks"], _to_groups(do, KH), name="swa_attn_bwd")
            gs["sinks"] = jnp.sum(dsk.reshape(QH, WIN), axis=1)
            dpr = jnp.concatenate([_from_groups(dqh), dkh.transpose(1, 0, 2).reshape(S, KH * SDH),
                                   dvh.transpose(1, 0, 2).reshape(S, KH * SDH)], axis=1)
            n_rot = (QH + KH) * SDH // LANES
            dproj = _rope(dpr, _rope_tables(pos, -1.0), n_rot, BF16, "swa_rope_bwd")
            dh1 = run(_mm, dproj[None], pp.w("swa_w_in"), mode="nt", out_dtype=F32, name="swa_proj_dx", tk=640)[0]
            pp.grad("swa_w_in", run(_mm, L["h1"][None], dproj[None], mode="tn", out_dtype=BF16, name="swa_proj_dw", out_groups=N_CHIPS, tn=640))
        pend = (dh1, sc1)
    grad_x, dmp[0]["sc1"], dmp[0]["sh1"] = _mod_bwd(pend[0], sv[0]["x_in"], pend[1], dres, "mod_mix_bwd0")
    dmod = [jnp.concatenate([p["sh1"], p["sc1"], p["g1"], p["sh2"], p["sc2"], p["g2"]], axis=1) for p in dmp]
    return loss_cols, grad_x, gs, jnp.concatenate(dmod, axis=0)


def _allgather_small(v, name):
    m_per, n = v.shape

    def body(x_ref, out_ref, send_sems, recv_sems, local_sem):
        x, y, c, chips = _place()
        me, sibling = (x, y, c), (x, y, 1 - c)

        def rows(px, py, pc):
            return out_ref.at[pl.ds((4 * px + 2 * py + pc) * m_per, m_per), :]

        def copy(k, block, to, src=None):
            return _remote(rows(*block) if src is None else src, rows(*block), send_sems.at[k], recv_sems.at[k], to)

        mine = pltpu.make_async_copy(x_ref, rows(*me), local_sem)
        mine.start()
        first = [copy(0, me, sibling, src=x_ref)]
        first += [copy(1 + j, me, (*chip, c), src=x_ref) for j, chip in enumerate(chips)]
        for cp in first:
            cp.start()
        passed = [copy(4 + j, (*chip, c), sibling) for j, chip in enumerate(chips)]
        for j, chip in enumerate(chips):
            copy(1 + j, (*chip, c), me).wait_recv()
            passed[j].start()
        copy(0, sibling, me).wait_recv()
        for j, chip in enumerate(chips):
            copy(4 + j, (*chip, 1 - c), me).wait_recv()
        for cp in first + passed:
            cp.wait_send()
        mine.wait()

    return pl.pallas_call(
        body, name=name, out_shape=jax.ShapeDtypeStruct((N_DEV * m_per, n), v.dtype),
        in_specs=[pl.BlockSpec(memory_space=pltpu.VMEM)], out_specs=pl.BlockSpec(memory_space=pltpu.VMEM),
        scratch_shapes=[pltpu.SemaphoreType.DMA((7,)), pltpu.SemaphoreType.DMA((7,)), pltpu.SemaphoreType.DMA],
        compiler_params=pltpu.CompilerParams(vmem_limit_bytes=VMEM_LIMIT),
    )(v)


def _row_tile(r, pref=256):
    return _tile(r, pref, 16)


def _cast_bf16(w, layer, chip, name):
    _, R, C = w.shape
    tr = _row_tile(R)

    def body(s_ref, w_ref, o_ref):
        o_ref[...] = w_ref[...].astype(BF16)

    return pl.pallas_call(
        body, name=name,
        grid_spec=pltpu.PrefetchScalarGridSpec(
            num_scalar_prefetch=1, grid=(R // tr,),
            in_specs=[pl.BlockSpec((None, tr, C), lambda i, s: (layer, i, 0))],
            out_specs=pl.BlockSpec((None, tr, C), lambda i, s: (s[0], i, 0))),
        out_shape=jax.ShapeDtypeStruct((N_CHIPS, R, C), BF16), compiler_params=_params(("parallel",)),
    )(jnp.reshape(chip, (1,)).astype(jnp.int32), w)


def _add_sibling(g, got, c, name):
    G, R, C = g.shape
    rh = R // 2
    tr = _row_tile(rh)
    nb = rh // tr

    def body(c_ref, g_ref, o_ref, p_ref):
        p_ref[...] = (g_ref[...].astype(F32) + o_ref[...].astype(F32)).astype(BF16)

    return pl.pallas_call(
        body, name=name,
        grid_spec=pltpu.PrefetchScalarGridSpec(
            num_scalar_prefetch=1, grid=(G, nb),
            in_specs=[pl.BlockSpec((1, tr, C), lambda s, i, c_ref: (s, c_ref[0] * nb + i, 0)),
                      pl.BlockSpec((1, tr, C), lambda s, i, c_ref: (s, i, 0))],
            out_specs=pl.BlockSpec((1, tr, C), lambda s, i, c_ref: (s, i, 0))),
        out_shape=jax.ShapeDtypeStruct((G, rh, C), BF16), compiler_params=_params(("parallel", "parallel")),
    )(jnp.reshape(c, (1,)).astype(jnp.int32), g, got)


def _sum_chips(part, landed, chip, c, name):
    G, rh, C = part.shape
    tr = _row_tile(rh)
    nb = rh // tr

    def body(p_ref, own_ref, *rest):
        acc = own_ref[...].astype(F32)
        for ref in rest[:G - 1]:
            acc = acc + ref[...].astype(F32)
        rest[G - 1][...] = acc

    slot = lambda k: pl.BlockSpec((None, tr, C), lambda i, p: ((p[0] + k) % G, i, 0))
    return pl.pallas_call(
        body, name=name,
        grid_spec=pltpu.PrefetchScalarGridSpec(
            num_scalar_prefetch=1, grid=(nb,), in_specs=[slot(k) for k in range(G)],
            out_specs=pl.BlockSpec((tr, C), lambda i, p: (p[1] * nb + i, 0))),
        out_shape=jax.ShapeDtypeStruct((2 * rh, C), F32), compiler_params=_params(("parallel",)),
    )(jnp.stack([chip, c]).astype(jnp.int32), part, *([landed] * (G - 1)))


def _adam_math(w, g, m, v):
    m = ADAM_B1 * m + (1.0 - ADAM_B1) * g
    v = ADAM_B2 * v + (1.0 - ADAM_B2) * (g * g)
    m_hat = m / (1.0 - ADAM_B1 ** ADAM_STEP)
    v_hat = v / (1.0 - ADAM_B2 ** ADAM_STEP)
    delta = -ADAM_LR * (m_hat / (jnp.sqrt(v_hat) + ADAM_EPS) + ADAM_WD * w)
    return delta, m, v


def _adamw(w, g, m, v, layer, prev, name, by_cols=False):
    L, R, C = w.shape
    tr = R if by_cols else _tile(R, 128, 8)
    tc = _tile(C, 256) if by_cols else C
    n_prev = len(prev)

    def body(w_ref, g_ref, m_ref, v_ref, *rest):
        go_ref, d_ref, mo_ref, vo_ref = rest[n_prev:]
        gv = g_ref[...]
        go_ref[...] = gv
        d_ref[...], mo_ref[...], vo_ref[...] = _adam_math(w_ref[...], gv, m_ref[...], v_ref[...])

    lay = pl.BlockSpec((None, tr, tc), lambda i: (layer, i // (C // tc), i % (C // tc)))
    flat = pl.BlockSpec((tr, tc), lambda i: (i // (C // tc), i % (C // tc)))
    return _call(
        body, name=name, grid=((R // tr) * (C // tc),), in_specs=[lay, flat, lay, lay] + _any_specs(n_prev), out_specs=[lay] * 4,
        out_shape=[jax.ShapeDtypeStruct((L, R, C), F32)] * 4, aliases={4 + k: k for k in range(n_prev)},
        sem=("parallel",), args=(w, g, m, v, *prev))


def _cond_rows(c_row, cw, name):
    D = c_row.shape[1]
    nr, fc = cw.shape

    def body(c_ref, e_ref, o_ref):
        o_ref[...] = jnp.zeros_like(o_ref)
        cv = c_ref[...]
        o_ref[0:1, 0:D] = cv * (1.0 / (1.0 + jnp.exp(-cv)))
        o_ref[8:8 + nr, 0:fc] = e_ref[...]

    return pl.pallas_call(body, name=name, out_shape=jax.ShapeDtypeStruct((16, max(D, fc)), F32))(c_row, cw)


def _ada_fwd(cact, ada_w, ada_b, layer, chip, name):
    _, D, NC = ada_w.shape
    tn = _tile(NC, 1024)
    nj = NC // tn

    def body(idx_ref, c_ref, w_ref, b_ref, o_ref):
        acc = jnp.dot(c_ref[...].astype(BF16), w_ref[0].astype(BF16), preferred_element_type=F32)
        o_ref[...] = acc + b_ref[pl.ds(idx_ref[0], 1), :]

    return pl.pallas_call(
        body, name=name,
        grid_spec=pltpu.PrefetchScalarGridSpec(
            num_scalar_prefetch=1, grid=(nj,),
            in_specs=[pl.BlockSpec((8, D), lambda j, idx: (0, 0)),
                      pl.BlockSpec((1, D, tn), lambda j, idx: (idx[0], 0, j)),
                      pl.BlockSpec((DEPTH, tn), lambda j, idx: (0, idx[1] * nj + j))],
            out_specs=pl.BlockSpec((8, tn), lambda j, idx: (0, j))),
        out_shape=jax.ShapeDtypeStruct((8, NC), F32), compiler_params=_params(("parallel",)),
    )(jnp.stack([layer, chip]).astype(jnp.int32), cact, ada_w, ada_b)


def _ada_grad_adamw(cact_t, dmod, w, m, v, name, comm=None):
    L, D, NC = w.shape
    tr = _tile(D, 128, 8)

    def body(c_ref, d_ref, w_ref, m_ref, v_ref, g_ref, dl_ref, mo_ref, vo_ref):
        g = jnp.dot(c_ref[...], d_ref[...], preferred_element_type=F32, precision=HIGHEST)
        g_ref[...] = g
        dl_ref[...], mo_ref[...], vo_ref[...] = _adam_math(w_ref[...], g, m_ref[...], v_ref[...])

    lay = pl.BlockSpec((None, tr, NC), lambda l, i: (l, i, 0))
    return _call(
        body, name=name, grid=(L, D // tr),
        in_specs=[pl.BlockSpec((tr, N_DEV), lambda l, i: (i, 0)), pl.BlockSpec((None, N_DEV, NC), lambda l, i: (l, 0, 0)), lay, lay, lay],
        out_specs=[lay] * 4, out_shape=[jax.ShapeDtypeStruct((L, D, NC), F32)] * 4,
        sem=("parallel", "parallel"), args=(cact_t, dmod, w, m, v), comm=comm)


def _sum_devices(gathered, name):
    n, R, C = gathered.shape

    def body(g_ref, o_ref):
        acc = g_ref[0]
        for j in range(1, n):
            acc = acc + g_ref[j]
        o_ref[...] = acc

    return pl.pallas_call(body, name=name, out_shape=jax.ShapeDtypeStruct((R, C), F32),
                          compiler_params=pltpu.CompilerParams(vmem_limit_bytes=VMEM_LIMIT))(gathered)


def _adamw_small(w, g, m, v, name):
    def body(w_ref, g_ref, m_ref, v_ref, d_ref, mo_ref, vo_ref):
        d_ref[...], mo_ref[...], vo_ref[...] = _adam_math(w_ref[...], g_ref[...], m_ref[...], v_ref[...])

    return pl.pallas_call(body, name=name, out_shape=[jax.ShapeDtypeStruct(w.shape, F32)] * 3)(w, g, m, v)


def _pad_rows(flat, unit=8 * LANES):
    n = flat.shape[0]
    total = -(-n // unit) * unit
    return jnp.pad(flat, (0, total - n)).reshape(total // LANES, LANES)


def _pad_lanes(v2d):
    return jnp.pad(v2d.reshape(1, -1), ((0, 0), (0, LANES - v2d.size)))


PLAN = {
    "fox_proj": [("gather", "ffn_w_up0", 0, 2, 8)],
    "fox_attn": [("gather", "ffn_w_up0", 2, 6, 8)],
    "fox_out": [("gather", "ffn_w_up0", 6, 7, 8)],
    "ln_mix0": [("gather", "ffn_w_up0", 7, 8, 8)],
    "ffn_up0": [("gather", "ffn_w_down0", 0, 1, 1)],
    "ffn_gate0": [("gather", "swa_w_in", 0, 1, 1)],
    "ffn_down0": [("gather", "swa_w_o", 0, 1, 1), ("gather", "ffn_w_up1", 0, 1, 8)],
    "ln_ffn0": [("gather", "ffn_w_up1", 1, 2, 8)],
    "swa_proj": [("gather", "ffn_w_up1", 2, 3, 8)],
    "swa_rope": [("gather", "ffn_w_up1", 3, 4, 8)],
    "swa_attn": [("gather", "ffn_w_up1", 4, 8, 8)],
    "ffn_up1": [("gather", "ffn_w_down1", 0, 1, 1)],
    "ffn_gate_bwd1": [("swap", "ffn_w_down1")],
    "ffn_up_dx1": [("scatter", "ffn_w_down1", 0, 1, 1)],
    "swa_out_dx": [("swap", "ffn_w_up1")],
    "swa_attn_bwd": [("scatter", "ffn_w_up1", 0, 6, 8), ("swap", "swa_w_o")],
    "swa_proj_dx": [("scatter", "swa_w_o", 0, 1, 1)],
    "ffn_down_dx0": [("scatter", "ffn_w_up1", 6, 8, 8), ("swap", "swa_w_in")],
    "ffn_down_dw0": [("scatter", "swa_w_in", 0, 1, 1)],
    "ffn_gate_bwd0": [("swap", "ffn_w_down0")],
    "ffn_up_dx0": [("scatter", "ffn_w_down0", 0, 1, 1)],
    "fox_out_dx": [("swap", "ffn_w_up0")],
    "fox_attn_bwd": [("scatter", "ffn_w_up0", 0, 5, 8), ("swap", "fox_w_o")],
    "fox_proj_dw": [("scatter", "fox_w_o", 0, 1, 1), ("scatter", "ffn_w_up0", 5, 6, 8)],
    "fox_proj_dx": [("scatter", "ffn_w_up0", 6, 8, 8), ("swap", "fox_w_in")],
}


class Exchanges:
    def __init__(self, dm, slots, chip, c):
        self.dm, self.slots, self.chip, self.c = dm, dict(slots), chip, c
        self.raw, self.part, self.landed, self.grads, self.views, self.pending = {}, {}, {}, {}, {}, {}

    def gather_now(self, keys, name):
        comm = _gather_comm([self.slots[k] for k in keys], [(0, 1, 1)] * len(keys))
        _run_comm(comm, name)
        self.slots.update(zip(keys, comm.results))

    def w(self, key):
        if key not in self.views:
            S, D, FH, QH, KH, Fh = self.dm
            full = self.slots[key]
            if key == "fox_w_in":
                cols = full.shape[2]
                full = jnp.pad(full.transpose(1, 0, 2).reshape(D, N_CHIPS * cols), ((0, 0), (0, 3 * D + LANES - N_CHIPS * cols)))[None]
            elif key in ("fox_w_o", "swa_w_o"):
                full = full.reshape(1, D, D)
            elif key.startswith("ffn_w_down"):
                full = full.reshape(1, Fh, D)
            self.views[key] = full
        return self.views[key]

    def carry(self, stage):
        todo = []
        for kind, key, *chunk in PLAN.get(stage, ()):
            if kind == "gather":
                todo.append((kind, [key], _gather_comm([self.slots[key]], [tuple(chunk)])))
            elif kind == "swap":
                todo.append((kind, [key], _swap_comm([self.raw[key]])))
            elif kind == "scatter":
                todo.append((kind, [(key, *chunk)], _scatter_comm([self.part[key]], [self.landed.get(key)], [tuple(chunk)])))
        self.pending[stage] = todo
        return _merge([cm for _, _, cm in todo])

    def carried(self, stage, comm):
        for kind, keys, cm in self.pending.pop(stage):
            if kind == "gather":
                self.slots[keys[0]] = cm.results[0]
            elif kind == "swap":
                self.part[keys[0]] = _add_sibling(self.raw[keys[0]], cm.results[0], self.c, f"add_sibling_{keys[0]}")
            else:
                self.landed[keys[0][0]] = cm.results[0]

    def grad(self, key, g):
        S, D, FH, QH, KH, Fh = self.dm
        if key == "fox_w_in":
            cols = self.slots[key].shape[2]
            g = g[0][:, :N_CHIPS * cols].reshape(D, N_CHIPS, cols).transpose(1, 0, 2)
        elif key in ("fox_w_o", "swa_w_o"):
            g = g.reshape(N_CHIPS, D // N_CHIPS, D)
        elif key.startswith("ffn_w_down"):
            g = g.reshape(N_CHIPS, Fh // N_CHIPS, D)
        self.raw[key] = g

    def finish(self):
        last = "fox_w_in"
        keys = list(self.landed)
        halves = [_sum_chips(self.part[k], self.landed[k], self.chip, self.c, f"sum_chips_{k}") for k in keys]
        send, join = _scatter_comm([self.part[last]], [None], [(0, 1, 1)]), _join_comm(halves)
        _run_comm(_merge([send, join]), "grads_tail")
        grads = dict(zip(keys, join.results))
        join = _join_comm([_sum_chips(self.part[last], send.results[0], self.chip, self.c, f"sum_chips_{last}")])
        _run_comm(join, "grads_join_last")
        grads[last] = join.results[0]
        return grads


def _step(dm, a):
    S, D, FH, QH, KH, Fh = dm
    ix, iy, ic = lax.axis_index("x"), lax.axis_index("y"), lax.axis_index("c")
    chip = 2 * ix + iy
    dev = 2 * chip + ic
    F2c = a["ffn_w_up"].shape[2]
    NC = a["ada_w"].shape[2]
    PW = 3 * D + LANES
    fox_cols = a["fox_w_in"].shape[2]

    e0 = _cond_rows(a["c"], a["ffn_conv_w"].reshape(DEPTH * 3, F2c), "silu_c")
    g0 = _allgather_small(e0, "gather_cond").reshape(N_DEV, 16, e0.shape[1])
    cact = g0[:, 0, :D]
    conv_w = g0[0::2, 8:8 + DEPTH * 3, :F2c].transpose(1, 0, 2).reshape(DEPTH, 3, N_CHIPS * F2c)
    rows = _ada_fwd(cact, a["ada_w"], a["ada_b"], ic, chip, "ada_proj")
    g1 = _allgather_small(rows, "gather_mod").reshape(N_CHIPS, DEPTH, 8, NC)
    mod = lax.dynamic_index_in_dim(g1, dev, axis=2, keepdims=False).transpose(1, 0, 2).reshape(DEPTH, N_CHIPS * NC)

    names = ["fox_w_in", "fox_w_o", "swa_w_in", "swa_w_o", "ffn_w_up", "ffn_w_up", "ffn_w_down", "ffn_w_down"]
    layers = [0, 0, 0, 0, 0, 1, 0, 1]
    keys = ["fox_w_in", "fox_w_o", "swa_w_in", "swa_w_o", "ffn_w_up0", "ffn_w_up1", "ffn_w_down0", "ffn_w_down1"]
    slots = {k: _cast_bf16(a[nm], l, chip, f"cast_{k}") for k, nm, l in zip(keys, names, layers)}
    pp = Exchanges(dm, slots, chip, ic)
    pp.gather_now(["fox_w_in", "fox_w_o"], "gather_fox")
    sp = {"fox_b_f": _pad_lanes(a["fox_b_f"]), "sinks": jnp.repeat(a["swa_sinks"].reshape(KH, QH // KH), WIN, axis=1)[:, :, None],
          "conv_w": [conv_w[i] for i in range(DEPTH)], "conv_b": [a["ffn_conv_b"][i:i + 1] for i in range(DEPTH)]}
    for nm in ("ln_mix_g", "ln_mix_b", "ln_ffn_g", "ln_ffn_b"):
        sp[nm] = [a[nm][i:i + 1] for i in range(DEPTH)]

    loss_cols, grad_x, gs, dmod = _local_step(dm, a["x"][0], a["loss_target"][0], a["positions"][0], mod, sp, pp)
    loss = lax.psum(0.5 / D * jnp.sum(loss_cols), ("x", "y", "c"))
    out = {"loss": loss, "grad_x": grad_x[None]}

    def run(fn, *args, name, **kw):
        comm = pp.carry(name)
        res = fn(*args, name=name, comm=comm, **kw)
        if comm is not None:
            pp.carried(name, comm)
        return res

    pieces = [dmod.reshape(-1), gs["fox_b_f"].reshape(-1), _pad_lanes(gs["sinks"]).reshape(-1),
              jnp.stack(gs["conv_w"]).reshape(-1), jnp.stack(gs["conv_b"]).reshape(-1)]
    pieces += [jnp.stack(gs[nm]).reshape(-1) for nm in ("ln_mix_g", "ln_mix_b", "ln_ffn_g", "ln_ffn_b")]
    sizes = [p.shape[0] for p in pieces]
    packed = _pad_rows(jnp.concatenate(pieces))
    allp = _allgather_small(packed, "gather_small").reshape(N_DEV, packed.shape[0], LANES)
    tot = _sum_devices(allp, "sum_small").reshape(-1)
    offs = [sum(sizes[:k]) for k in range(len(sizes))]
    take = lambda k: tot[offs[k]:offs[k] + sizes[k]]
    g_small = {"ada_b": take(0).reshape(DEPTH, -1), "fox_b_f": take(1)[:FH].reshape(1, FH), "swa_sinks": take(2)[:QH].reshape(1, QH),
               "ffn_conv_w": lax.dynamic_slice_in_dim(take(3).reshape(DEPTH, 3, N_CHIPS * F2c), chip * F2c, F2c, axis=2),
               "ffn_conv_b": take(4).reshape(DEPTH, -1)}
    for k, nm in enumerate(("ln_mix_g", "ln_mix_b", "ln_ffn_g", "ln_ffn_b")):
        g_small[nm] = take(5 + k).reshape(DEPTH, D)
    small = list(g_small)
    pack = lambda pre: _pad_rows(jnp.concatenate([(a[pre + nm] if pre else a[nm]).reshape(-1) for nm in small]))
    gp = _pad_rows(jnp.concatenate([g_small[nm].reshape(-1) for nm in small]))
    ds_, ms_, vs_ = _adamw_small(pack(""), gp, pack("m_"), pack("v_"), "adamw_small")
    off = 0
    for nm in small:
        n_el = a[nm].size
        out["grad_" + nm] = g_small[nm]
        for pre, arr in (("delta_", ds_), ("new_m_", ms_), ("new_v_", vs_)):
            out[pre + nm] = arr.reshape(-1)[off:off + n_el].reshape(a[nm].shape)
        off += n_el

    dmod_all = allp.reshape(N_DEV, -1)[:, :DEPTH * N_CHIPS * NC].reshape(N_DEV, DEPTH, N_CHIPS * NC)
    dmod_mine = lax.dynamic_slice_in_dim(dmod_all, chip * NC, NC, axis=2).transpose(1, 0, 2)
    ada = _ada_grad_adamw(cact.T, dmod_mine, a["ada_w"], a["m_ada_w"], a["v_ada_w"], "ada_grad")
    for pre, arr in zip(("grad_", "delta_", "new_m_", "new_v_"), ada):
        out[pre + "ada_w"] = arr

    grads = pp.finish()
    upd = {}
    for k, nm, l in zip(keys[1:], names[1:], layers[1:]):
        upd[nm] = _adamw(a[nm], grads[k], a["m_" + nm], a["v_" + nm], l, upd.get(nm, ()), f"adamw_{k}")
    tview = lambda t: jnp.swapaxes(t, 1, 2)
    res = _adamw(tview(a["fox_w_in"]), grads["fox_w_in"].T, tview(a["m_fox_w_in"]), tview(a["v_fox_w_in"]), 0, (), "adamw_fox_w_in", by_cols=True)
    upd["fox_w_in"] = [tview(r) for r in res]
    for nm, res in upd.items():
        for pre, arr in zip(("grad_", "delta_", "new_m_", "new_v_"), res):
            out[pre + nm] = arr
    return out


_WEIGHTS = ["fox_w_in", "fox_b_f", "fox_w_o", "swa_w_in", "swa_sinks", "swa_w_o", "ada_w", "ada_b", "ffn_w_up", "ffn_conv_w",
            "ffn_conv_b", "ffn_w_down", "ln_mix_g", "ln_mix_b", "ln_ffn_g", "ln_ffn_b"]
_INPUTS = (["x", "c", "positions"] + _WEIGHTS + ["loss_target"] + ["m_" + w for w in _WEIGHTS] + ["v_" + w for w in _WEIGHTS])


def kernel(x, c, positions, fox_w_in, fox_b_f, fox_w_o, swa_w_in, swa_sinks, swa_w_o, ada_w, ada_b, ffn_w_up, ffn_conv_w, ffn_conv_b, ffn_w_down, ln_mix_g, ln_mix_b, ln_ffn_g, ln_ffn_b, loss_target, m_fox_w_in, m_fox_b_f, m_fox_w_o, m_swa_w_in, m_swa_sinks, m_swa_w_o, m_ada_w, m_ada_b, m_ffn_w_up, m_ffn_conv_w, m_ffn_conv_b, m_ffn_w_down, m_ln_mix_g, m_ln_mix_b, m_ln_ffn_g, m_ln_ffn_b, v_fox_w_in, v_fox_b_f, v_fox_w_o, v_swa_w_in, v_swa_sinks, v_swa_w_o, v_ada_w, v_ada_b, v_ffn_w_up, v_ffn_conv_w, v_ffn_conv_b, v_ffn_w_down, v_ln_mix_g, v_ln_mix_b, v_ln_ffn_g, v_ln_ffn_b):
    args = (x, c, positions, fox_w_in, fox_b_f, fox_w_o, swa_w_in, swa_sinks, swa_w_o, ada_w, ada_b, ffn_w_up, ffn_conv_w, ffn_conv_b, ffn_w_down, ln_mix_g, ln_mix_b, ln_ffn_g, ln_ffn_b, loss_target, m_fox_w_in, m_fox_b_f, m_fox_w_o, m_swa_w_in, m_swa_sinks, m_swa_w_o, m_ada_w, m_ada_b, m_ffn_w_up, m_ffn_conv_w, m_ffn_conv_b, m_ffn_w_down, m_ln_mix_g, m_ln_mix_b, m_ln_ffn_g, m_ln_ffn_b, v_fox_w_in, v_fox_b_f, v_fox_w_o, v_swa_w_in, v_swa_sinks, v_swa_w_o, v_ada_w, v_ada_b, v_ffn_w_up, v_ffn_conv_w, v_ffn_conv_b, v_ffn_w_down, v_ln_mix_g, v_ln_mix_b, v_ln_ffn_g, v_ln_ffn_b)
    out = _step(PROD, dict(zip(_INPUTS, args)))
    order = ["loss", "grad_x"] + [p + w for p in ("grad_", "delta_", "new_m_", "new_v_") for w in _WEIGHTS]
    return tuple(out[k] for k in order)
```

```python
import functools
from typing import NamedTuple

import jax
import jax.numpy as jnp
from jax import lax
from jax.experimental import pallas as pl
from jax.experimental.pallas import tpu as pltpu

F32 = jnp.float32
BF16 = jnp.bfloat16
MESH = pl.DeviceIdType.MESH
HIGHEST = lax.Precision.HIGHEST

N_CHIPS = 4
N_DEV = 8
LANES = 128
VMEM_LIMIT = 56 * 1024 * 1024

DEPTH = 2
DEEPNORM_ALPHA = (2.0 * DEPTH) ** 0.25
LN_EPS = 1e-5
ROPE_THETA = 500000.0
ADAM_LR, ADAM_B1, ADAM_B2, ADAM_EPS, ADAM_WD, ADAM_STEP = 0.001, 0.9, 0.999, 1e-08, 0.01, 10
NEG = -1e30


class Dims(NamedTuple):
    S: int
    D: int
    FH: int
    QH: int
    KH: int
    F: int


PROD = Dims(S=2048, D=2048, FH=16, QH=32, KH=4, F=5632)
FDH = 128
SDH = 64
WIN = 128
ROPE_DIM = 16
FOX_TQ = 256


def _params(sem=None, vmem=VMEM_LIMIT):
    return pltpu.CompilerParams(dimension_semantics=sem, vmem_limit_bytes=vmem)


def _tile(n, pref, unit=LANES):
    if n <= pref:
        return n
    t = (pref // unit) * unit
    while t > 0:
        if n % t == 0:
            return t
        t -= unit
    return n


class Comm:
    def __init__(self, args, out_shapes, aliases, n_sem, start, finish, members=()):
        self.args, self.out_shapes, self.aliases, self.n_sem = list(args), list(out_shapes), dict(aliases), n_sem
        self.start, self.finish = start, finish
        self.members = members
        self.results = None

    def set_results(self, res):
        self.results = list(res)
        for cm, o0 in self.members:
            cm.set_results(self.results[o0:o0 + len(cm.out_shapes)])


class _SemView:
    def __init__(self, sems, first):
        self.sems, self.first = sems, first

    @property
    def at(self):
        return self

    def __getitem__(self, k):
        return self.sems.at[self.first + k]


def _merge(comms):
    comms = [cm for cm in comms if cm is not None]
    if len(comms) < 2:
        return comms[0] if comms else None
    args, shapes, aliases, spans, n_sem = [], [], {}, [], 0
    for cm in comms:
        spans.append((len(args), len(shapes), n_sem))
        aliases.update({len(args) + a: len(shapes) + o for a, o in cm.aliases.items()})
        args += cm.args
        shapes += cm.out_shapes
        n_sem += cm.n_sem

    def each(step):
        def run(ar, ou, send, recv):
            for cm, (a0, o0, s0) in zip(comms, spans):
                getattr(cm, step)(ar[a0:a0 + len(cm.args)], ou[o0:o0 + len(cm.out_shapes)], _SemView(send, s0), _SemView(recv, s0))
        return run

    return Comm(args, shapes, aliases, n_sem, each("start"), each("finish"), [(cm, o0) for cm, (_, o0, _) in zip(comms, spans)])


def _place():
    x, y, c = lax.axis_index("x"), lax.axis_index("y"), lax.axis_index("c")
    chips = [(1 - x, y), (x, 1 - y), (1 - x, 1 - y)]
    return x, y, c, chips


def _remote(src, dst, send, recv, to):
    return pltpu.make_async_remote_copy(src_ref=src, dst_ref=dst, send_sem=send, recv_sem=recv, device_id=to, device_id_type=MESH)


def _any_specs(n):
    return [pl.BlockSpec(memory_space=pl.ANY)] * n


def _call(body, *, name, grid, in_specs, out_specs, out_shape, args, sem, scratch_shapes=(), aliases=None, comm=None):
    in_specs, out_specs, out_shape, scratch_shapes = list(in_specs), list(out_specs), list(out_shape), list(scratch_shapes)
    aliases = dict(aliases or {})
    if comm is None:
        return pl.pallas_call(body, name=name, grid=grid, in_specs=in_specs, out_specs=out_specs, out_shape=out_shape,
                              scratch_shapes=scratch_shapes, input_output_aliases=aliases, compiler_params=_params(sem))(*args)
    n_in, n_out, nc_in, nc_out, n_scr = len(in_specs), len(out_specs), len(comm.args), len(comm.out_shapes), len(scratch_shapes)

    def wrapped(*refs):
        ins, refs = refs[:n_in], refs[n_in:]
        cin, refs = refs[:nc_in], refs[nc_in:]
        outs, refs = refs[:n_out], refs[n_out:]
        cout, refs = refs[:nc_out], refs[nc_out:]
        scratch, (send, recv) = refs[:n_scr], refs[n_scr:]
        ids = [pl.program_id(k) for k in range(len(grid))]
        first = functools.reduce(jnp.logical_and, [i == 0 for i in ids])
        last = functools.reduce(jnp.logical_and, [i == g - 1 for i, g in zip(ids, grid)])

        @pl.when(first)
        def _():
            comm.start(cin, cout, send, recv)

        body(*ins, *outs, *scratch)

        @pl.when(last)
        def _():
            comm.finish(cin, cout, send, recv)

    res = pl.pallas_call(
        wrapped, name=name, grid=grid, in_specs=in_specs + _any_specs(nc_in), out_specs=out_specs + _any_specs(nc_out),
        out_shape=out_shape + comm.out_shapes,
        scratch_shapes=scratch_shapes + [pltpu.SemaphoreType.DMA((comm.n_sem,)), pltpu.SemaphoreType.DMA((comm.n_sem,))],
        input_output_aliases={**aliases, **{n_in + a: n_out + o for a, o in comm.aliases.items()}},
        compiler_params=_params(("arbitrary",) * len(grid)),
    )(*args, *comm.args)
    comm.set_results(res[n_out:])
    return list(res[:n_out])


def _run_comm(comm, name):
    nc_in, nc_out = len(comm.args), len(comm.out_shapes)

    def body(*refs):
        cin, cout, (send, recv) = refs[:nc_in], refs[nc_in:nc_in + nc_out], refs[nc_in + nc_out:]
        comm.start(cin, cout, send, recv)
        comm.finish(cin, cout, send, recv)

    res = pl.pallas_call(
        body, name=name, in_specs=_any_specs(nc_in), out_specs=_any_specs(nc_out), out_shape=comm.out_shapes,
        scratch_shapes=[pltpu.SemaphoreType.DMA((comm.n_sem,)), pltpu.SemaphoreType.DMA((comm.n_sem,))],
        input_output_aliases=comm.aliases,
    )(*comm.args)
    comm.set_results(res)


_HBM = pl.BlockSpec(memory_space=pltpu.HBM)
_SEM = pl.BlockSpec(memory_space=pltpu.SEMAPHORE)
_EFFECT = pltpu.SideEffectType.DATAFLOW_SIDE_EFFECTING


def _split_start(comm, name):
    n = len(comm.args)
    back = {o: a for a, o in comm.aliases.items()}
    assert len(back) == len(comm.out_shapes)

    def body(*refs):
        send, recv, thru, token = refs[n], refs[n + 1], refs[n + 2:2 * n + 2], refs[2 * n + 2]
        comm.start(thru, [thru[back[o]] for o in range(len(back))], send, recv)
        token[...] = jnp.zeros_like(token)

    res = pl.pallas_call(
        body, name=name,
        out_shape=(pltpu.SemaphoreType.DMA((comm.n_sem,)), pltpu.SemaphoreType.DMA((comm.n_sem,)),
                   *[pltpu.HBM(a.shape, a.dtype) for a in comm.args], jax.ShapeDtypeStruct((8, LANES), F32)),
        in_specs=[_HBM] * n, out_specs=(_SEM, _SEM, *[_HBM] * n, pl.BlockSpec(memory_space=pltpu.VMEM)),
        input_output_aliases={i: 2 + i for i in range(n)},
        compiler_params=pltpu.CompilerParams(has_side_effects=_EFFECT),
    )(*[pltpu.with_memory_space_constraint(a, pltpu.HBM) for a in comm.args])
    return (res[0], res[1], list(res[2:2 + n])), res[2 + n]


def _split_wait(comm, state, after, name):
    send, recv, thru = state
    n, n_after = len(thru), len(after)
    back = {o: a for a, o in comm.aliases.items()}

    def body(*refs):
        ins, send_ref, recv_ref = refs[:n], refs[n], refs[n + 1]
        comm.finish(ins, [ins[back[o]] for o in range(len(back))], send_ref, recv_ref)

    res = pl.pallas_call(
        body, name=name, out_shape=tuple(pltpu.HBM(a.shape, a.dtype) for a in thru),
        in_specs=[_HBM] * n + [_SEM, _SEM] + _any_specs(n_after), out_specs=[_HBM] * n,
        input_output_aliases={i: i for i in range(n)}, compiler_params=pltpu.CompilerParams(has_side_effects=_EFFECT),
    )(*thru, send, recv, *after)
    comm.set_results([res[back[o]] for o in range(len(back))])


def _forward_comm(slots, chunks):
    n = len(slots)

    def rows(t, who):
        rh = slots[t].shape[1] // 2
        lo, hi, nch = chunks[t]
        rc = rh // nch
        return pl.ds(who * rh + lo * rc, (hi - lo) * rc)

    def copy(outs, send, recv, t, j, chip, who):
        x, y, c, _ = _place()
        blk = outs[t].at[2 * chip[0] + chip[1], rows(t, who)]
        return _remote(blk, blk, send.at[3 * t + j], recv.at[3 * t + j], (x, y, 1 - c))

    def start(args, outs, send, recv):
        _, _, c, chips = _place()
        for t in range(n):
            for j, chip in enumerate(chips):
                copy(outs, send, recv, t, j, chip, c).start()

    def finish(args, outs, send, recv):
        _, _, c, chips = _place()
        for t in range(n):
            for j, chip in enumerate(chips):
                copy(outs, send, recv, t, j, chip, 1 - c).wait_recv()
        for t in range(n):
            for j, chip in enumerate(chips):
                copy(outs, send, recv, t, j, chip, c).wait_send()

    shapes = [jax.ShapeDtypeStruct(w.shape, w.dtype) for w in slots]
    return Comm(slots, shapes, {t: t for t in range(n)}, 3 * n, start, finish)


def _gather_comm(slots, chunks, forward=True):
    n = len(slots)

    def rows(t, who):
        rh = slots[t].shape[1] // 2
        lo, hi, nch = chunks[t]
        rc = rh // nch
        return pl.ds(who * rh + lo * rc, (hi - lo) * rc)

    def start(args, outs, send, recv):
        x, y, c, chips = _place()
        s = 2 * x + y
        for t in range(n):
            mine = outs[t].at[s, rows(t, c)]
            for j, chip in enumerate(chips):
                _remote(mine, mine, send.at[6 * t + j], recv.at[6 * t + j], (*chip, c)).start()

    def finish(args, outs, send, recv):
        x, y, c, chips = _place()
        s = 2 * x + y
        sib = (x, y, 1 - c)
        for t in range(n):
            for j, chip in enumerate(chips):
                blk = outs[t].at[2 * chip[0] + chip[1], rows(t, c)]
                _remote(blk, blk, send.at[6 * t + j], recv.at[6 * t + j], (*chip, c)).wait_recv()
                if forward:
                    _remote(blk, blk, send.at[6 * t + 3 + j], recv.at[6 * t + 3 + j], sib).start()
        for t in range(n if forward else 0):
            for j, chip in enumerate(chips):
                blk = outs[t].at[2 * chip[0] + chip[1], rows(t, 1 - c)]
                _remote(blk, blk, send.at[6 * t + 3 + j], recv.at[6 * t + 3 + j], sib).wait_recv()
        for t in range(n):
            mine = outs[t].at[s, rows(t, c)]
            for j, chip in enumerate(chips):
                _remote(mine, mine, send.at[6 * t + j], recv.at[6 * t + j], (*chip, c)).wait_send()
                if forward:
                    blk = outs[t].at[2 * chip[0] + chip[1], rows(t, c)]
                    _remote(blk, blk, send.at[6 * t + 3 + j], recv.at[6 * t + 3 + j], sib).wait_send()

    shapes = [jax.ShapeDtypeStruct(w.shape, w.dtype) for w in slots]
    return Comm(slots, shapes, {t: t for t in range(n)}, 6 * n, start, finish)


def _scatter_comm(parts, landed, chunks):
    n = len(parts)
    prev = [t for t in range(n) if landed[t] is not None]

    def rows(t):
        lo, hi, nch = chunks[t]
        rc = parts[t].shape[1] // nch
        return pl.ds(lo * rc, (hi - lo) * rc)

    def start(args, outs, send, recv):
        x, y, c, chips = _place()
        s = 2 * x + y
        for t in range(n):
            for j, chip in enumerate(chips):
                _remote(args[t].at[2 * chip[0] + chip[1], rows(t)], outs[t].at[s, rows(t)],
                        send.at[3 * t + j], recv.at[3 * t + j], (*chip, c)).start()

    def finish(args, outs, send, recv):
        x, y, c, chips = _place()
        for t in range(n):
            for j, chip in enumerate(chips):
                blk = outs[t].at[2 * chip[0] + chip[1], rows(t)]
                _remote(blk, blk, send.at[3 * t + j], recv.at[3 * t + j], (*chip, c)).wait_recv()
        for t in range(n):
            for j, chip in enumerate(chips):
                src = args[t].at[2 * chip[0] + chip[1], rows(t)]
                _remote(src, src, send.at[3 * t + j], recv.at[3 * t + j], (*chip, c)).wait_send()

    shapes = [jax.ShapeDtypeStruct(p.shape, p.dtype) for p in parts]
    return Comm(list(parts) + [landed[t] for t in prev], shapes, {n + i: t for i, t in enumerate(prev)}, 3 * n, start, finish)


def _swap_comm(gs):
    n = len(gs)

    def copy(args, outs, send, recv, t):
        _, _, c, _ = _place()
        rh = gs[t].shape[1] // 2
        x, y = lax.axis_index("x"), lax.axis_index("y")
        return _remote(args[t].at[:, pl.ds((1 - c) * rh, rh), :], outs[t], send.at[t], recv.at[t], (x, y, 1 - c))

    def start(args, outs, send, recv):
        for t in range(n):
            copy(args, outs, send, recv, t).start()

    def finish(args, outs, send, recv):
        for t in range(n):
            copy(args, outs, send, recv, t).wait()

    shapes = [jax.ShapeDtypeStruct((g.shape[0], g.shape[1] // 2, g.shape[2]), g.dtype) for g in gs]
    return Comm(gs, shapes, {}, n, start, finish)


def _join_comm(gs):
    n = len(gs)

    def half(outs, t, who):
        rh = gs[t].shape[0] // 2
        return outs[t].at[pl.ds(who * rh, rh), :]

    def start(args, outs, send, recv):
        x, y, c, _ = _place()
        for t in range(n):
            _remote(half(outs, t, c), half(outs, t, c), send.at[t], recv.at[t], (x, y, 1 - c)).start()

    def finish(args, outs, send, recv):
        x, y, c, _ = _place()
        for t in range(n):
            _remote(half(outs, t, 1 - c), half(outs, t, 1 - c), send.at[t], recv.at[t], (x, y, 1 - c)).wait_recv()
        for t in range(n):
            _remote(half(outs, t, c), half(outs, t, c), send.at[t], recv.at[t], (x, y, 1 - c)).wait_send()

    shapes = [jax.ShapeDtypeStruct(g.shape, g.dtype) for g in gs]
    return Comm(gs, shapes, {t: t for t in range(n)}, n, start, finish)


_DN = {"nn": (((1,), (0,)), ((), ())), "nt": (((1,), (1,)), ((), ())), "tn": (((0,), (0,)), ((), ()))}


def _mm(a, b, *, mode, out_dtype, name, out_groups=1, tm=1024, tn=1024, tk=2048, comm=None):
    ga, ra, ca = a.shape
    gb, rb, cb = b.shape
    if mode == "nn":
        M, K, N = ra, ga * ca, gb * cb
        assert rb == K and ga == 1 or (rb == K)
    elif mode == "nt":
        M, K, N = ra, ga * ca, rb
        assert gb * cb == K
    else:
        K, M, N = ra, ga * ca, gb * cb
        assert rb == K
    go = out_groups
    if mode == "nn":
        tk = _tile(ca, tk); assert rb % tk == 0 and (ga == 1 or True)
        tn = _tile(min(cb, N // go), tn); tm = _tile(M, tm, 8)
    elif mode == "nt":
        tk = _tile(ca, tk); tk = _tile(cb, tk) if cb % tk else tk; assert ca % tk == 0 and cb % tk == 0
        tn = _tile(N // go, tn); tm = _tile(M, tm, 8)
    else:
        tk = _tile(K, tk, 8); tm = _tile(ca, tm); tn = _tile(min(cb, N // go), tn)
    assert (N // go) % tn == 0 and M % tm == 0 and K % tk == 0, (name, M, N, K, tm, tn, tk)
    nk = K // tk
    kpa = max(ca // tk, 1)
    kpb = max(cb // tk, 1)
    npb = max(cb // tn, 1)
    npo = (N // go) // tn
    mpa = max(ca // tm, 1)

    if mode == "nn":
        a_spec = pl.BlockSpec((1, tm, tk), lambda j, i, k: (k // kpa, i, k % kpa))
        b_spec = pl.BlockSpec((1, tk, tn), lambda j, i, k: (j // npb, k, j % npb))
    elif mode == "nt":
        a_spec = pl.BlockSpec((1, tm, tk), lambda j, i, k: (k // kpa, i, k % kpa))
        b_spec = pl.BlockSpec((1, tn, tk), lambda j, i, k: (k // kpb, j, k % kpb))
    else:
        a_spec = pl.BlockSpec((1, tk, tm), lambda j, i, k: (i // mpa, k, i % mpa))
        b_spec = pl.BlockSpec((1, tk, tn), lambda j, i, k: (j // npb, k, j % npb))
    o_spec = pl.BlockSpec((1, tm, tn), lambda j, i, k: (j // npo, i, j % npo))
    dn = _DN[mode]

    def body(a_ref, b_ref, o_ref, *acc):
        p = lax.dot_general(a_ref[0], b_ref[0], dn, preferred_element_type=F32)
        if nk == 1:
            o_ref[0] = p.astype(out_dtype)
        else:
            k = pl.program_id(2)

            @pl.when(k == 0)
            def _():
                acc[0][...] = p

            @pl.when(k > 0)
            def _():
                acc[0][...] += p

            @pl.when(k == nk - 1)
            def _():
                o_ref[0] = acc[0][...].astype(out_dtype)

    return _call(
        body, name=name, grid=(N // tn, M // tm, nk), in_specs=[a_spec, b_spec], out_specs=[o_spec],
        out_shape=[jax.ShapeDtypeStruct((go, M, N // go), out_dtype)],
        scratch_shapes=[pltpu.VMEM((tm, tn), F32)] if nk > 1 else [],
        sem=("parallel", "parallel", "arbitrary"), args=(a, b), comm=comm)[0]


def _rows(tr, d):
    return pl.BlockSpec((tr, d), lambda i: (i, 0))


def _vec(d):
    return pl.BlockSpec((1, d), lambda i: (0, 0))


def _modulate(x, sc, sh, name):
    S, D = x.shape
    tr = min(256, S)

    def body(x_ref, sc_ref, sh_ref, h_ref):
        h_ref[...] = (x_ref[...] * (1.0 + sc_ref[...]) + sh_ref[...]).astype(BF16)

    return pl.pallas_call(
        body, name=name, grid=(S // tr,), in_specs=[_rows(tr, D), _vec(D), _vec(D)], out_specs=_rows(tr, D),
        out_shape=jax.ShapeDtypeStruct((S, D), BF16), compiler_params=_params(("parallel",)),
    )(x, sc, sh)


def _ln_fwd(x, y, gate, gamma, beta, sc, sh, name, comm=None):
    S, D = x.shape
    tr = min(256, S)
    emit_h = sc is not None

    def body(*refs):
        if emit_h:
            x_ref, y_ref, g_ref, ga_ref, be_ref, sc_ref, sh_ref, xo_ref, xh_ref, rs_ref, h_ref = refs
        else:
            x_ref, y_ref, g_ref, ga_ref, be_ref, xo_ref, xh_ref, rs_ref = refs
        z = DEEPNORM_ALPHA * x_ref[...] + (1.0 + g_ref[...]) * y_ref[...]
        mu = jnp.mean(z, axis=-1, keepdims=True)
        zc = z - mu
        var = jnp.mean(zc * zc, axis=-1, keepdims=True)
        rstd = lax.rsqrt(var + LN_EPS)
        xh = zc * rstd
        xo = xh * ga_ref[...] + be_ref[...]
        xo_ref[...] = xo
        xh_ref[...] = xh
        rs_ref[...] = rstd
        if emit_h:
            h_ref[...] = (xo * (1.0 + sc_ref[...]) + sh_ref[...]).astype(BF16)

    ins = [x, y, gate, gamma, beta] + ([sc, sh] if emit_h else [])
    in_specs = [_rows(tr, D), _rows(tr, D)] + [_vec(D)] * (len(ins) - 2)
    out_shape = [jax.ShapeDtypeStruct((S, D), F32), jax.ShapeDtypeStruct((S, D), F32), jax.ShapeDtypeStruct((S, 1), F32)]
    out_specs = [_rows(tr, D), _rows(tr, D), _rows(tr, 1)]
    if emit_h:
        out_shape.append(jax.ShapeDtypeStruct((S, D), BF16))
        out_specs.append(_rows(tr, D))
    return _call(body, name=name, grid=(S // tr,), in_specs=in_specs, out_specs=out_specs, out_shape=out_shape,
                 sem=("parallel",), args=ins, comm=comm)


def _loss_head(xf, tgt, name):
    S, D = xf.shape
    tr = min(256, S)

    def body(x_ref, t_ref, dx_ref, l_ref):
        e = x_ref[...] - t_ref[...]
        dx_ref[...] = e * (1.0 / D)

        @pl.when(pl.program_id(0) == 0)
        def _():
            l_ref[...] = jnp.zeros_like(l_ref)

        l_ref[...] += jnp.sum(e * e, axis=0, keepdims=True)

    return pl.pallas_call(
        body, name=name, grid=(S // tr,), in_specs=[_rows(tr, D), _rows(tr, D)],
        out_specs=[_rows(tr, D), _vec(D)],
        out_shape=[jax.ShapeDtypeStruct((S, D), F32), jax.ShapeDtypeStruct((1, D), F32)],
        compiler_params=_params(("arbitrary",)),
    )(xf, tgt)


def _ln_bwd(dxo, xh, rstd, gamma, y, gate, name, pre=None):
    S, D = dxo.shape
    tr = min(256, S)
    n_pre = 0 if pre is None else 3

    def body(dx_ref, xh_ref, rs_ref, ga_ref, y_ref, g_ref, *rest):
        dres_ref, dy_ref, dga_ref, dbe_ref, dg_ref = rest[n_pre:n_pre + 5]
        first = pl.program_id(0) == 0
        dxo_ = dx_ref[...]
        xh_ = xh_ref[...]
        if pre is not None:
            dh_ref, sc_ref, be_ref = rest[:3]
            dsc_ref, dsh_ref = rest[n_pre + 5:]
            dh_ = dh_ref[...]
            dxo_ = dxo_ + dh_ * (1.0 + sc_ref[...])

            @pl.when(first)
            def _():
                dsc_ref[...] = jnp.zeros_like(dsc_ref)
                dsh_ref[...] = jnp.zeros_like(dsh_ref)

            dsc_ref[...] += jnp.sum(dh_ * (xh_ * ga_ref[...] + be_ref[...]), axis=0, keepdims=True)
            dsh_ref[...] += jnp.sum(dh_, axis=0, keepdims=True)
        dxh = dxo_ * ga_ref[...]
        m1 = jnp.mean(dxh, axis=-1, keepdims=True)
        m2 = jnp.mean(dxh * xh_, axis=-1, keepdims=True)
        dz = rs_ref[...] * (dxh - m1 - xh_ * m2)
        dres_ref[...] = DEEPNORM_ALPHA * dz
        dy_ref[...] = ((1.0 + g_ref[...]) * dz).astype(BF16)

        @pl.when(first)
        def _():
            dga_ref[...] = jnp.zeros_like(dga_ref)
            dbe_ref[...] = jnp.zeros_like(dbe_ref)
            dg_ref[...] = jnp.zeros_like(dg_ref)

        dga_ref[...] += jnp.sum(dxo_ * xh_, axis=0, keepdims=True)
        dbe_ref[...] += jnp.sum(dxo_, axis=0, keepdims=True)
        dg_ref[...] += jnp.sum(dz * y_ref[...], axis=0, keepdims=True)

    extra_in = [] if pre is None else [_rows(tr, D), _vec(D), _vec(D)]
    return pl.pallas_call(
        body, name=name, grid=(S // tr,),
        in_specs=[_rows(tr, D), _rows(tr, D), _rows(tr, 1), _vec(D), _rows(tr, D), _vec(D)] + extra_in,
        out_specs=[_rows(tr, D), _rows(tr, D)] + [_vec(D)] * (3 + (0 if pre is None else 2)),
        out_shape=[jax.ShapeDtypeStruct((S, D), F32), jax.ShapeDtypeStruct((S, D), BF16)]
        + [jax.ShapeDtypeStruct((1, D), F32)] * (3 + (0 if pre is None else 2)),
        compiler_params=_params(("arbitrary",)),
    )(dxo, xh, rstd, gamma, y, gate, *(pre or ()))


def _mod_bwd(dh, x, sc, dres, name):
    S, D = x.shape
    tr = min(256, S)

    def body(dh_ref, x_ref, sc_ref, dr_ref, dx_ref, dsc_ref, dsh_ref):
        dh_ = dh_ref[...]
        dx_ref[...] = dr_ref[...] + dh_ * (1.0 + sc_ref[...])

        @pl.when(pl.program_id(0) == 0)
        def _():
            dsc_ref[...] = jnp.zeros_like(dsc_ref)
            dsh_ref[...] = jnp.zeros_like(dsh_ref)

        dsc_ref[...] += jnp.sum(dh_ * x_ref[...], axis=0, keepdims=True)
        dsh_ref[...] += jnp.sum(dh_, axis=0, keepdims=True)

    return pl.pallas_call(
        body, name=name, grid=(S // tr,),
        in_specs=[_rows(tr, D), _rows(tr, D), _vec(D), _rows(tr, D)],
        out_specs=[_rows(tr, D), _vec(D), _vec(D)],
        out_shape=[jax.ShapeDtypeStruct((S, D), F32), jax.ShapeDtypeStruct((1, D), F32), jax.ShapeDtypeStruct((1, D), F32)],
        compiler_params=_params(("arbitrary",)),
    )(dh, x, sc, dres)


def _log_sigmoid(z):
    return jnp.minimum(z, 0.0) - jnp.log(1.0 + jnp.exp(-jnp.abs(z)))


def _fox_gate_fwd(proj, b_f, n_heads, name):
    S, PW = proj.shape
    blk = min(256, S)
    last = PW // LANES - 1

    def body(fl_ref, b_ref, cum_ref):
        r = lax.broadcasted_iota(jnp.int32, (blk, blk), 0)
        c = lax.broadcasted_iota(jnp.int32, (blk, blk), 1)
        tril = (c <= r).astype(F32)
        carry = jnp.zeros((1, LANES), F32)
        for i in range(S // blk):
            lf = _log_sigmoid(fl_ref[i * blk:(i + 1) * blk, :] + b_ref[...])
            cum_ref[i * blk:(i + 1) * blk, :] = jnp.dot(tril, lf, preferred_element_type=F32, precision=HIGHEST) + carry
            carry = carry + jnp.sum(lf, axis=0, keepdims=True)

    return pl.pallas_call(
        body, name=name, grid=(1,),
        in_specs=[pl.BlockSpec((S, LANES), lambda i: (0, last)), pl.BlockSpec((1, LANES), lambda i: (0, 0))],
        out_specs=pl.BlockSpec((S, LANES), lambda i: (0, 0)),
        out_shape=jax.ShapeDtypeStruct((S, LANES), F32), compiler_params=_params(("arbitrary",)),
    )(proj, b_f)


def _fox_gate_bwd(dcum, proj, b_f, n_heads, name):
    S, PW = proj.shape
    blk = min(256, S)
    last = PW // LANES - 1
    nb = S // blk

    def body(dc_ref, fl_ref, b_ref, dfl_ref, db_ref):
        r = lax.broadcasted_iota(jnp.int32, (blk, blk), 0)
        c = lax.broadcasted_iota(jnp.int32, (blk, blk), 1)
        triu = (c >= r).astype(F32)
        lane = lax.broadcasted_iota(jnp.int32, (blk, LANES), 1)
        carry = jnp.zeros((1, LANES), F32)
        dbs = jnp.zeros((1, LANES), F32)
        for i in reversed(range(nb)):
            dc = dc_ref[i * blk:(i + 1) * blk, :]
            dlf = jnp.dot(triu, dc, preferred_element_type=F32, precision=HIGHEST) + carry
            carry = carry + jnp.sum(dc, axis=0, keepdims=True)
            z = fl_ref[i * blk:(i + 1) * blk, :] + b_ref[...]
            e = jnp.exp(-jnp.abs(z))
            sig_neg = jnp.where(z >= 0, e / (1.0 + e), 1.0 / (1.0 + e))
            dfl = jnp.where(lane < n_heads, dlf * sig_neg, 0.0)
            dfl_ref[i * blk:(i + 1) * blk, :] = dfl.astype(BF16)
            dbs = dbs + jnp.sum(dfl, axis=0, keepdims=True)
        db_ref[...] = dbs

    return pl.pallas_call(
        body, name=name, grid=(1,),
        in_specs=[pl.BlockSpec((S, LANES), lambda i: (0, 0)), pl.BlockSpec((S, LANES), lambda i: (0, last)),
                  pl.BlockSpec((1, LANES), lambda i: (0, 0))],
        out_specs=[pl.BlockSpec((S, LANES), lambda i: (0, 0)), pl.BlockSpec((1, LANES), lambda i: (0, 0))],
        out_shape=[jax.ShapeDtypeStruct((S, LANES), BF16), jax.ShapeDtypeStruct((1, LANES), F32)],
        compiler_params=_params(("arbitrary",)),
    )(dcum, proj, b_f)


def _fox_scores(q_ref, kb_ref, cq_ref, ck_ref, qi, tq, scale):
    kk = (qi + 1) * tq
    rows = slice(qi * tq, (qi + 1) * tq)
    qb = q_ref[rows, :].astype(BF16)
    s = lax.dot_general(qb, kb_ref[0:kk, :], _DN["nt"], preferred_element_type=F32) * scale
    s = s + (cq_ref[0, rows, :] - ck_ref[0, :, 0:kk])
    r = lax.broadcasted_iota(jnp.int32, (tq, kk), 0) + qi * tq
    c = lax.broadcasted_iota(jnp.int32, (tq, kk), 1)
    mask = c <= r
    return jnp.where(mask, s, NEG), mask, qb


def _fox_fwd(proj, cq, ck, n_heads, name, comm=None):
    S = proj.shape[0]
    H = n_heads
    tq = min(FOX_TQ, S)
    nq = S // tq
    scale = FDH ** -0.5

    def body(q_ref, k_ref, v_ref, cq_ref, ck_ref, o_ref, lse_ref, kb_ref, vb_ref):
        kb_ref[...] = k_ref[...].astype(BF16)
        vb_ref[...] = v_ref[...].astype(BF16)
        for qi in range(nq):
            kk = (qi + 1) * tq
            rows = slice(qi * tq, (qi + 1) * tq)
            s, _, _ = _fox_scores(q_ref, kb_ref, cq_ref, ck_ref, qi, tq, scale)
            m = jnp.max(s, axis=-1, keepdims=True)
            p = jnp.exp(s - m)
            l = jnp.sum(p, axis=-1, keepdims=True)
            p = p * (1.0 / l)
            o_ref[rows, :] = jnp.dot(p.astype(BF16), vb_ref[0:kk, :], preferred_element_type=F32).astype(BF16)
            lse_ref[0, rows, :] = m + jnp.log(l)

    col = lambda off: pl.BlockSpec((S, FDH), lambda h: (0, h + off))
    stat_c = pl.BlockSpec((1, S, 1), lambda h: (h, 0, 0))
    stat_r = pl.BlockSpec((1, 1, S), lambda h: (h, 0, 0))
    return _call(
        body, name=name, grid=(H,),
        in_specs=[col(0), col(H), col(2 * H), stat_c, stat_r],
        out_specs=[col(0), stat_c],
        out_shape=[jax.ShapeDtypeStruct((S, H * FDH), BF16), jax.ShapeDtypeStruct((H, S, 1), F32)],
        scratch_shapes=[pltpu.VMEM((S, FDH), BF16), pltpu.VMEM((S, FDH), BF16)],
        sem=("parallel",), args=(proj, proj, proj, cq, ck), comm=comm)


def _fox_bwd(proj, cq, ck, lse, do, n_heads, name, comm=None):
    S = proj.shape[0]
    H = n_heads
    tq = min(FOX_TQ, S)
    nq = S // tq
    scale = FDH ** -0.5

    def body(q_ref, k_ref, v_ref, cq_ref, ck_ref, lse_ref, do_ref, dq_ref, dk_ref, dv_ref, dcq_ref, dck_ref,
             kb_ref, vb_ref, dka_ref, dva_ref):
        kb_ref[...] = k_ref[...].astype(BF16)
        vb_ref[...] = v_ref[...].astype(BF16)
        dka_ref[...] = jnp.zeros_like(dka_ref)
        dva_ref[...] = jnp.zeros_like(dva_ref)
        dck_ref[...] = jnp.zeros_like(dck_ref)
        for qi in range(nq):
            kk = (qi + 1) * tq
            rows = slice(qi * tq, (qi + 1) * tq)
            s, mask, qb = _fox_scores(q_ref, kb_ref, cq_ref, ck_ref, qi, tq, scale)
            p = jnp.where(mask, jnp.exp(s - lse_ref[0, rows, :]), 0.0)
            dob = do_ref[rows, :]
            dp = lax.dot_general(dob, vb_ref[0:kk, :], _DN["nt"], preferred_element_type=F32)
            delta = jnp.sum(p * dp, axis=-1, keepdims=True)
            ds = p * (dp - delta)
            dcq_ref[0, rows, :] = jnp.sum(ds, axis=-1, keepdims=True)
            dck_ref[0, :, 0:kk] -= jnp.sum(ds, axis=0, keepdims=True)
            dsb = (ds * scale).astype(BF16)
            dq_ref[rows, :] = jnp.dot(dsb, kb_ref[0:kk, :], preferred_element_type=F32).astype(BF16)
            dka_ref[0:kk, :] += lax.dot_general(dsb, qb, _DN["tn"], preferred_element_type=F32)
            dva_ref[0:kk, :] += lax.dot_general(p.astype(BF16), dob, _DN["tn"], preferred_element_type=F32)
        dk_ref[...] = dka_ref[...].astype(BF16)
        dv_ref[...] = dva_ref[...].astype(BF16)

    col = lambda off: pl.BlockSpec((S, FDH), lambda h: (0, h + off))
    stat_c = pl.BlockSpec((1, S, 1), lambda h: (h, 0, 0))
    stat_r = pl.BlockSpec((1, 1, S), lambda h: (h, 0, 0))
    wide = jax.ShapeDtypeStruct((S, H * FDH), BF16)
    return _call(
        body, name=name, grid=(H,),
        in_specs=[col(0), col(H), col(2 * H), stat_c, stat_r, stat_c, col(0)],
        out_specs=[col(0), col(0), col(0), stat_c, stat_r],
        out_shape=[wide, wide, wide, jax.ShapeDtypeStruct((H, S, 1), F32), jax.ShapeDtypeStruct((H, 1, S), F32)],
        scratch_shapes=[pltpu.VMEM((S, FDH), BF16), pltpu.VMEM((S, FDH), BF16), pltpu.VMEM((S, FDH), F32), pltpu.VMEM((S, FDH), F32)],
        sem=("parallel",), args=(proj, proj, proj, cq, ck, lse, do), comm=comm)


def _rope_tables(pos, sign):
    inv = ROPE_THETA ** (-jnp.arange(0, ROPE_DIM, 2, dtype=F32) / ROPE_DIM)
    ang = pos.astype(F32)[:, None] * inv
    cos, sin = jnp.cos(ang), sign * jnp.sin(ang)
    l64 = jnp.arange(LANES) % SDH
    idx = l64 % (ROPE_DIM // 2)
    c = jnp.where(l64 < ROPE_DIM, cos[:, idx], 1.0)
    sa = jnp.where(l64 < ROPE_DIM // 2, -sin[:, idx], 0.0)
    sb = jnp.where((l64 >= ROPE_DIM // 2) & (l64 < ROPE_DIM), sin[:, idx], 0.0)
    rot = jnp.stack([c, sa, sb])
    ident = jnp.stack([jnp.ones_like(c), jnp.zeros_like(c), jnp.zeros_like(c)])
    return jnp.stack([rot, ident]).astype(F32)


def _rope(xin, tabs, n_rot, out_dtype, name, comm=None):
    S, W = xin.shape

    def body(x_ref, t_ref, o_ref):
        xv = x_ref[...]
        o = xv * t_ref[0, 0] + pltpu.roll(xv, LANES - ROPE_DIM // 2, 1) * t_ref[0, 1] + pltpu.roll(xv, ROPE_DIM // 2, 1) * t_ref[0, 2]
        o_ref[...] = o.astype(out_dtype)

    return _call(
        body, name=name, grid=(W // LANES,),
        in_specs=[pl.BlockSpec((S, LANES), lambda j: (0, j)),
                  pl.BlockSpec((1, 3, S, LANES), lambda j: (jnp.where(j < n_rot, 0, 1), 0, 0, 0))],
        out_specs=[pl.BlockSpec((S, LANES), lambda j: (0, j))],
        out_shape=[jax.ShapeDtypeStruct((S, W), out_dtype)], sem=("parallel",), args=(xin, tabs), comm=comm)[0]


SWA_PER_STEP = 2


def _swa_bias():
    r = jnp.arange(WIN)[:, None]
    c = jnp.arange(2 * WIN)[None, :]
    first = c <= r
    later = (c > r) & (c <= r + WIN)
    return jnp.where(jnp.stack([first, later]), 0.0, NEG).astype(F32)


def _swa_probs(q_ref, k_ref, sk_ref, b_ref, n, j, scale):
    st = pl.multiple_of(jnp.maximum(n - 1, 0) * WIN, WIN)
    qb = q_ref[0, j]
    kb = k_ref[0, pl.ds(st, 2 * WIN), :]
    gm = qb.shape[0]
    s = lax.dot_general(qb, kb, _DN["nt"], preferred_element_type=F32) * scale
    s = (s.reshape(gm // WIN, WIN, 2 * WIN) + b_ref[jnp.minimum(n, 1)][None]).reshape(gm, 2 * WIN)
    sink = sk_ref[0]
    m = jnp.maximum(jnp.max(s, axis=-1, keepdims=True), sink)
    e = jnp.exp(s - m)
    es = jnp.exp(sink - m)
    inv = 1.0 / (jnp.sum(e, axis=-1, keepdims=True) + es)
    return e * inv, es * inv, st, qb, kb


def _swa_specs(S, gm):
    blk = pl.BlockSpec((1, SWA_PER_STEP, gm, SDH), lambda g, n: (g, n, 0, 0))
    kv = pl.BlockSpec((1, S, SDH), lambda g, n: (g, 0, 0))
    col = pl.BlockSpec((1, gm, 1), lambda g, n: (g, 0, 0))
    bias = pl.BlockSpec((2, WIN, 2 * WIN), lambda g, n: (0, 0, 0))
    return blk, kv, col, bias


def _swa_fwd(q, k, v, sinks, name, comm=None):
    KH, nb, gm, _ = q.shape
    S = k.shape[1]
    scale = SDH ** -0.5

    def body(q_ref, k_ref, v_ref, sk_ref, b_ref, o_ref):
        for j in range(SWA_PER_STEP):
            p, _, st, _, _ = _swa_probs(q_ref, k_ref, sk_ref, b_ref, pl.program_id(1) * SWA_PER_STEP + j, j, scale)
            vb = v_ref[0, pl.ds(st, 2 * WIN), :]
            o_ref[0, j] = jnp.dot(p.astype(BF16), vb, preferred_element_type=F32).astype(BF16)

    blk, kv, col, bias = _swa_specs(S, gm)
    return _call(
        body, name=name, grid=(KH, nb // SWA_PER_STEP), in_specs=[blk, kv, kv, col, bias], out_specs=[blk],
        out_shape=[jax.ShapeDtypeStruct(q.shape, BF16)], sem=("parallel", "parallel"), args=(q, k, v, sinks, _swa_bias()), comm=comm)[0]


def _swa_bwd(q, k, v, sinks, do, name, comm=None):
    KH, nb, gm, _ = q.shape
    S = k.shape[1]
    scale = SDH ** -0.5

    def body(q_ref, k_ref, v_ref, sk_ref, b_ref, do_ref, dq_ref, dk_ref, dv_ref, dsk_ref):
        @pl.when(pl.program_id(1) == 0)
        def _():
            dk_ref[...] = jnp.zeros_like(dk_ref)
            dv_ref[...] = jnp.zeros_like(dv_ref)
            dsk_ref[...] = jnp.zeros_like(dsk_ref)

        blocks = []
        for j in range(SWA_PER_STEP):
            p, ps, st, qb, kb = _swa_probs(q_ref, k_ref, sk_ref, b_ref, pl.program_id(1) * SWA_PER_STEP + j, j, scale)
            vb = v_ref[0, pl.ds(st, 2 * WIN), :]
            dob = do_ref[0, j]
            dp = lax.dot_general(dob, vb, _DN["nt"], preferred_element_type=F32)
            delta = jnp.sum(p * dp, axis=-1, keepdims=True)
            dsb = (p * (dp - delta) * scale).astype(BF16)
            dq_ref[0, j] = jnp.dot(dsb, kb, preferred_element_type=F32)
            blocks.append((st, lax.dot_general(dsb, qb, _DN["tn"], preferred_element_type=F32),
                           lax.dot_general(p.astype(BF16), dob, _DN["tn"], preferred_element_type=F32), ps * delta))
        for st, dk, dv, dsk in blocks:
            dk_ref[0, pl.ds(st, 2 * WIN), :] += dk
            dv_ref[0, pl.ds(st, 2 * WIN), :] += dv
            dsk_ref[0] -= dsk

    blk, kv, col, bias = _swa_specs(S, gm)
    return _call(
        body, name=name, grid=(KH, nb // SWA_PER_STEP), in_specs=[blk, kv, kv, col, bias, blk], out_specs=[blk, kv, kv, col],
        out_shape=[jax.ShapeDtypeStruct(q.shape, F32), jax.ShapeDtypeStruct(k.shape, F32),
                   jax.ShapeDtypeStruct(k.shape, F32), jax.ShapeDtypeStruct(sinks.shape, F32)],
        sem=("parallel", "arbitrary"), args=(q, k, v, sinks, _swa_bias(), do), comm=comm)


def _shift_down(u, k):
    row = lax.broadcasted_iota(jnp.int32, u.shape, 0)
    return jnp.where(row >= k, pltpu.roll(u, k, 0), 0.0)


def _shift_up(u, k):
    n = u.shape[0]
    row = lax.broadcasted_iota(jnp.int32, u.shape, 0)
    return jnp.where(row < n - k, pltpu.roll(u, n - k, 0), 0.0)


def _conv3(u, w_ref, b_ref):
    return w_ref[0:1, :] * _shift_down(u, 2) + w_ref[1:2, :] * _shift_down(u, 1) + w_ref[2:3, :] * u + b_ref[...]


def _conv_gate(u, cw, cb, name, comm=None):
    S, F2 = u.shape
    Fh = F2 // 2
    tc = _tile(Fh, 256)
    nf = Fh // tc

    def body(ug_ref, uv_ref, wg_ref, wv_ref, bg_ref, bv_ref, a_ref):
        g = _conv3(ug_ref[...], wg_ref, bg_ref)
        val = _conv3(uv_ref[...], wv_ref, bv_ref)
        a_ref[...] = (g * (1.0 / (1.0 + jnp.exp(-g))) * val).astype(BF16)

    blk = lambda r, off: pl.BlockSpec((r, tc), lambda j: (0, j + off))
    return _call(
        body, name=name, grid=(nf,),
        in_specs=[blk(S, 0), blk(S, nf), blk(3, 0), blk(3, nf), blk(1, 0), blk(1, nf)], out_specs=[blk(S, 0)],
        out_shape=[jax.ShapeDtypeStruct((S, Fh), BF16)], sem=("parallel",), args=(u, u, cw, cw, cb, cb), comm=comm)[0]


def _conv_gate_bwd(u, da, cw, cb, name, comm=None):
    S, F2 = u.shape
    Fh = F2 // 2
    tc = _tile(Fh, 256)
    nf = Fh // tc

    def half(h, dx, uu, w_ref, du_ref, dw_ref, db_ref):
        up1, up2 = _shift_up(dx, 1), _shift_up(dx, 2)
        du = w_ref[2:3, :] * dx + w_ref[1:2, :] * up1 + w_ref[0:1, :] * up2
        du_ref[h] = du.astype(BF16)
        dw_ref[h, 0:1, :] = jnp.sum(up2 * uu, axis=0, keepdims=True)
        dw_ref[h, 1:2, :] = jnp.sum(up1 * uu, axis=0, keepdims=True)
        dw_ref[h, 2:3, :] = jnp.sum(dx * uu, axis=0, keepdims=True)
        db_ref[h] = jnp.sum(dx, axis=0, keepdims=True)

    def body(ug_ref, uv_ref, da_ref, wg_ref, wv_ref, bg_ref, bv_ref, du_ref, dw_ref, db_ref):
        ug = ug_ref[...]
        uv = uv_ref[...]
        g = _conv3(ug, wg_ref, bg_ref)
        val = _conv3(uv, wv_ref, bv_ref)
        sig = 1.0 / (1.0 + jnp.exp(-g))
        da_ = da_ref[...]
        dg = da_ * val * (sig * (1.0 + g * (1.0 - sig)))
        dval = da_ * (g * sig)
        half(0, dg, ug, wg_ref, du_ref, dw_ref, db_ref)
        half(1, dval, uv, wv_ref, du_ref, dw_ref, db_ref)

    blk = lambda r, off: pl.BlockSpec((r, tc), lambda j: (0, j + off))
    both = lambda r: pl.BlockSpec((2, r, tc), lambda j: (0, 0, j))
    return _call(
        body, name=name, grid=(nf,),
        in_specs=[blk(S, 0), blk(S, nf), blk(S, 0), blk(3, 0), blk(3, nf), blk(1, 0), blk(1, nf)],
        out_specs=[both(S), both(3), both(1)],
        out_shape=[jax.ShapeDtypeStruct((2, S, Fh), BF16), jax.ShapeDtypeStruct((2, 3, Fh), F32), jax.ShapeDtypeStruct((2, 1, Fh), F32)],
        sem=("parallel",), args=(u, u, da, cw, cw, cb, cb), comm=comm)


def _to_groups(t, kh):
    S, width = t.shape
    g = width // SDH // kh
    return t.reshape(S // WIN, WIN, kh, g, SDH).transpose(2, 0, 3, 1, 4).reshape(kh, S // WIN, g * WIN, SDH)


def _from_groups(t):
    kh, nb, gm, _ = t.shape
    g = gm // WIN
    return t.reshape(kh, nb, g, WIN, SDH).transpose(1, 3, 0, 2, 4).reshape(nb * WIN, kh * g * SDH)


class LocalWeights:
    def __init__(self, weights):
        self.weights, self.grads = weights, {}

    def w(self, name):
        return self.weights[name]

    def carry(self, stage):
        return None

    def carried(self, stage, comm):
        pass

    def grad(self, name, g):
        self.grads[name] = g


def _local_step(dm, x, tgt, pos, mod, sp, pp):
    S, D, FH, QH, KH, Fh = dm
    m = [[mod[i:i + 1, j * D:(j + 1) * D] for j in range(6)] for i in range(DEPTH)]

    def run(fn, *args, name, **kw):
        comm = pp.carry(name)
        out = fn(*args, name=name, comm=comm, **kw)
        if comm is not None:
            pp.carried(name, comm)
        return out

    sv = []
    xs = x
    h = _modulate(xs, m[0][1], m[0][0], "mod_in")
    for i in range(DEPTH):
        sh1, sc1, g1, sh2, sc2, g2 = m[i]
        L = {}
        L["x_in"], L["h1"] = xs, h
        if i == 0:
            proj = run(_mm, h[None], pp.w("fox_w_in"), mode="nn", out_dtype=F32, name="fox_proj", tn=896)[0]
            cum = _fox_gate_fwd(proj, sp["fox_b_f"], FH, "fox_gate")
            cq = cum[:, :FH].T[:, :, None]
            ck = cum[:, :FH].T[:, None, :]
            o, lse = run(_fox_fwd, proj, cq, ck, FH, name="fox_attn")
            L.update(proj=proj, cq=cq, ck=ck, lse=lse, o=o)
            y = run(_mm, o[None], pp.w("fox_w_o"), mode="nn", out_dtype=F32, name="fox_out")[0]
        else:
            proj = run(_mm, h[None], pp.w("swa_w_in"), mode="nn", out_dtype=F32, name="swa_proj", tn=640)[0]
            tabs = _rope_tables(pos, 1.0)
            n_rot = (QH + KH) * SDH // LANES
            pr = run(_rope, proj, tabs, n_rot, BF16, name="swa_rope")
            qh = _to_groups(pr[:, :QH * SDH], KH)
            kh = pr[:, QH * SDH:(QH + KH) * SDH].reshape(S, KH, SDH).transpose(1, 0, 2)
            vh = pr[:, (QH + KH) * SDH:].reshape(S, KH, SDH).transpose(1, 0, 2)
            oh = run(_swa_fwd, qh, kh, vh, sp["sinks"], name="swa_attn")
            o = _from_groups(oh)
            L.update(qh=qh, kh=kh, vh=vh, o=o)
            y = run(_mm, o[None], pp.w("swa_w_o"), mode="nn", out_dtype=F32, name="swa_out")[0]
        L["y1"] = y
        x1, L["xh1"], L["rs1"], h2 = run(_ln_fwd, xs, y, g1, sp["ln_mix_g"][i], sp["ln_mix_b"][i], sc2, sh2, name=f"ln_mix{i}")
        L["x1"], L["h2"] = x1, h2
        u = run(_mm, h2[None], pp.w(f"ffn_w_up{i}"), mode="nn", out_dtype=F32, name=f"ffn_up{i}", tm=512, tn=1408)[0]
        a = run(_conv_gate, u, sp["conv_w"][i], sp["conv_b"][i], name=f"ffn_gate{i}")
        y2 = run(_mm, a[None], pp.w(f"ffn_w_down{i}"), mode="nn", out_dtype=F32, name=f"ffn_down{i}", tk=5632, tm=512)[0]
        L.update(u=u, a=a, y2=y2)
        if i + 1 < DEPTH:
            xs, L["xh2"], L["rs2"], h = run(_ln_fwd, x1, y2, g2, sp["ln_ffn_g"][i], sp["ln_ffn_b"][i], m[i + 1][1], m[i + 1][0], name=f"ln_ffn{i}")
        else:
            xs, L["xh2"], L["rs2"] = run(_ln_fwd, x1, y2, g2, sp["ln_ffn_g"][i], sp["ln_ffn_b"][i], None, None, name=f"ln_ffn{i}")
        sv.append(L)

    dx, loss_cols = _loss_head(xs, tgt, "loss_head")

    gs = {k: [None] * DEPTH for k in ("conv_w", "conv_b", "ln_mix_g", "ln_mix_b", "ln_ffn_g", "ln_ffn_b")}
    dmp = [dict() for _ in range(DEPTH)]
    dres, pend = dx, None
    for i in reversed(range(DEPTH)):
        sh1, sc1, g1, sh2, sc2, g2 = m[i]
        L = sv[i]
        res = _ln_bwd(dres, L["xh2"], L["rs2"], sp["ln_ffn_g"][i], L["y2"], g2, f"ln_ffn_bwd{i}",
                      None if pend is None else (*pend, sp["ln_ffn_b"][i]))
        dres, dy, gs["ln_ffn_g"][i], gs["ln_ffn_b"][i], dmp[i]["g2"] = res[:5]
        if pend is not None:
            dmp[i + 1]["sc1"], dmp[i + 1]["sh1"] = res[5:]
        da = run(_mm, dy[None], pp.w(f"ffn_w_down{i}"), mode="nt", out_dtype=F32, name=f"ffn_down_dx{i}", tm=512, tn=1408)[0]
        pp.grad(f"ffn_w_down{i}", run(_mm, L["a"][None], dy[None], mode="tn", out_dtype=BF16, name=f"ffn_down_dw{i}", tm=1408))
        du, dcw, dcb = run(_conv_gate_bwd, L["u"], da, sp["conv_w"][i], sp["conv_b"][i], name=f"ffn_gate_bwd{i}")
        gs["conv_w"][i] = dcw.transpose(1, 0, 2).reshape(3, 2 * Fh)
        gs["conv_b"][i] = dcb.transpose(1, 0, 2).reshape(1, 2 * Fh)
        dh2 = run(_mm, du, pp.w(f"ffn_w_up{i}"), mode="nt", out_dtype=F32, name=f"ffn_up_dx{i}", tk=2816)[0]
        pp.grad(f"ffn_w_up{i}", run(_mm, L["h2"][None], du, mode="tn", out_dtype=BF16, name=f"ffn_up_dw{i}", out_groups=N_CHIPS, tn=1408))
        dres, dy, gs["ln_mix_g"][i], gs["ln_mix_b"][i], dmp[i]["g1"], dmp[i]["sc2"], dmp[i]["sh2"] = _ln_bwd(
            dres, L["xh1"], L["rs1"], sp["ln_mix_g"][i], L["y1"], g1, f"ln_mix_bwd{i}", (dh2, sc2, sp["ln_mix_b"][i]))
        if i == 0:
            do = run(_mm, dy[None], pp.w("fox_w_o"), mode="nt", out_dtype=BF16, name="fox_out_dx")[0]
            pp.grad("fox_w_o", run(_mm, L["o"][None], dy[None], mode="tn", out_dtype=BF16, name="fox_out_dw"))
            dq, dk, dv, dcq, dck = run(_fox_bwd, L["proj"], L["cq"], L["ck"], L["lse"], do, FH, name="fox_attn_bwd")
            dcum = dcq[:, :, 0].T + dck[:, 0, :].T
            dcum = jnp.pad(dcum, ((0, 0), (0, LANES - FH)))
            dfl, db_f = _fox_gate_bwd(dcum, L["proj"], sp["fox_b_f"], FH, "fox_gate_bwd")
            gs["fox_b_f"] = db_f
            dproj = jnp.concatenate([dq, dk, dv, dfl], axis=1)
            pp.grad("fox_w_in", run(_mm, L["h1"][None], dproj[None], mode="tn", out_dtype=BF16, name="fox_proj_dw", tn=896))
            dh1 = run(_mm, dproj[None], pp.w("fox_w_in"), mode="nt", out_dtype=F32, name="fox_proj_dx", tk=6272, tm=512)[0]
        else:
            do = run(_mm, dy[None], pp.w("swa_w_o"), mode="nt", out_dtype=BF16, name="swa_out_dx")[0]
            pp.grad("swa_w_o", run(_mm, L["o"][None], dy[None], mode="tn", out_dtype=BF16, name="swa_out_dw"))
            dqh, dkh, dvh, dsk = run(_swa_bwd, L["qh"], L["kh"], L["vh"], sp["sinks"], _to_groups(do, KH), name="swa_attn_bwd")
            gs["sinks"] = jnp.sum(dsk.reshape(QH, WIN), axis=1)
            dpr = jnp.concatenate([_from_groups(dqh), dkh.transpose(1, 0, 2).reshape(S, KH * SDH),
                                   dvh.transpose(1, 0, 2).reshape(S, KH * SDH)], axis=1)
            n_rot = (QH + KH) * SDH // LANES
            dproj = _rope(dpr, _rope_tables(pos, -1.0), n_rot, BF16, "swa_rope_bwd")
            dh1 = run(_mm, dproj[None], pp.w("swa_w_in"), mode="nt", out_dtype=F32, name="swa_proj_dx", tk=640)[0]
            pp.grad("swa_w_in", run(_mm, L["h1"][None], dproj[None], mode="tn", out_dtype=BF16, name="swa_proj_dw", out_groups=N_CHIPS, tn=640))
        pend = (dh1, sc1)
    grad_x, dmp[0]["sc1"], dmp[0]["sh1"] = _mod_bwd(pend[0], sv[0]["x_in"], pend[1], dres, "mod_mix_bwd0")
    dmod = [jnp.concatenate([p["sh1"], p["sc1"], p["g1"], p["sh2"], p["sc2"], p["g2"]], axis=1) for p in dmp]
    return loss_cols, grad_x, gs, jnp.concatenate(dmod, axis=0)


def _allgather_small(v, name):
    m_per, n = v.shape

    def body(x_ref, out_ref, send_sems, recv_sems, local_sem):
        x, y, c, chips = _place()
        me, sibling = (x, y, c), (x, y, 1 - c)

        def rows(px, py, pc):
            return out_ref.at[pl.ds((4 * px + 2 * py + pc) * m_per, m_per), :]

        def copy(k, block, to, src=None):
            return _remote(rows(*block) if src is None else src, rows(*block), send_sems.at[k], recv_sems.at[k], to)

        mine = pltpu.make_async_copy(x_ref, rows(*me), local_sem)
        mine.start()
        first = [copy(0, me, sibling, src=x_ref)]
        first += [copy(1 + j, me, (*chip, c), src=x_ref) for j, chip in enumerate(chips)]
        for cp in first:
            cp.start()
        passed = [copy(4 + j, (*chip, c), sibling) for j, chip in enumerate(chips)]
        for j, chip in enumerate(chips):
            copy(1 + j, (*chip, c), me).wait_recv()
            passed[j].start()
        copy(0, sibling, me).wait_recv()
        for j, chip in enumerate(chips):
            copy(4 + j, (*chip, 1 - c), me).wait_recv()
        for cp in first + passed:
            cp.wait_send()
        mine.wait()

    return pl.pallas_call(
        body, name=name, out_shape=jax.ShapeDtypeStruct((N_DEV * m_per, n), v.dtype),
        in_specs=[pl.BlockSpec(memory_space=pltpu.VMEM)], out_specs=pl.BlockSpec(memory_space=pltpu.VMEM),
        scratch_shapes=[pltpu.SemaphoreType.DMA((7,)), pltpu.SemaphoreType.DMA((7,)), pltpu.SemaphoreType.DMA],
        compiler_params=pltpu.CompilerParams(vmem_limit_bytes=VMEM_LIMIT),
    )(v)


def _row_tile(r, pref=256):
    return _tile(r, pref, 16)


def _cast_bf16(w, layer, chip, name):
    _, R, C = w.shape
    tr = _row_tile(R)

    def body(s_ref, w_ref, o_ref):
        o_ref[...] = w_ref[...].astype(BF16)

    return pl.pallas_call(
        body, name=name,
        grid_spec=pltpu.PrefetchScalarGridSpec(
            num_scalar_prefetch=1, grid=(R // tr,),
            in_specs=[pl.BlockSpec((None, tr, C), lambda i, s: (layer, i, 0))],
            out_specs=pl.BlockSpec((None, tr, C), lambda i, s: (s[0], i, 0))),
        out_shape=jax.ShapeDtypeStruct((N_CHIPS, R, C), BF16), compiler_params=_params(("parallel",)),
    )(jnp.reshape(chip, (1,)).astype(jnp.int32), w)


def _add_sibling(g, got, c, name):
    G, R, C = g.shape
    rh = R // 2
    tr = _row_tile(rh)
    nb = rh // tr

    def body(c_ref, g_ref, o_ref, p_ref):
        p_ref[...] = (g_ref[...].astype(F32) + o_ref[...].astype(F32)).astype(BF16)

    return pl.pallas_call(
        body, name=name,
        grid_spec=pltpu.PrefetchScalarGridSpec(
            num_scalar_prefetch=1, grid=(G, nb),
            in_specs=[pl.BlockSpec((1, tr, C), lambda s, i, c_ref: (s, c_ref[0] * nb + i, 0)),
                      pl.BlockSpec((1, tr, C), lambda s, i, c_ref: (s, i, 0))],
            out_specs=pl.BlockSpec((1, tr, C), lambda s, i, c_ref: (s, i, 0))),
        out_shape=jax.ShapeDtypeStruct((G, rh, C), BF16), compiler_params=_params(("parallel", "parallel")),
    )(jnp.reshape(c, (1,)).astype(jnp.int32), g, got)


def _sum_chips(part, landed, chip, c, name):
    G, rh, C = part.shape
    tr = _row_tile(rh)
    nb = rh // tr

    def body(p_ref, own_ref, *rest):
        acc = own_ref[...].astype(F32)
        for ref in rest[:G - 1]:
            acc = acc + ref[...].astype(F32)
        rest[G - 1][...] = acc

    slot = lambda k: pl.BlockSpec((None, tr, C), lambda i, p: ((p[0] + k) % G, i, 0))
    return pl.pallas_call(
        body, name=name,
        grid_spec=pltpu.PrefetchScalarGridSpec(
            num_scalar_prefetch=1, grid=(nb,), in_specs=[slot(k) for k in range(G)],
            out_specs=pl.BlockSpec((tr, C), lambda i, p: (p[1] * nb + i, 0))),
        out_shape=jax.ShapeDtypeStruct((2 * rh, C), F32), compiler_params=_params(("parallel",)),
    )(jnp.stack([chip, c]).astype(jnp.int32), part, *([landed] * (G - 1)))


def _adam_math(w, g, m, v):
    m = ADAM_B1 * m + (1.0 - ADAM_B1) * g
    v = ADAM_B2 * v + (1.0 - ADAM_B2) * (g * g)
    m_hat = m / (1.0 - ADAM_B1 ** ADAM_STEP)
    v_hat = v / (1.0 - ADAM_B2 ** ADAM_STEP)
    delta = -ADAM_LR * (m_hat / (jnp.sqrt(v_hat) + ADAM_EPS) + ADAM_WD * w)
    return delta, m, v


def _adamw(w, g, m, v, layer, prev, name, by_cols=False):
    L, R, C = w.shape
    tr = R if by_cols else _tile(R, 128, 8)
    tc = _tile(C, 256) if by_cols else C
    n_prev = len(prev)

    def body(w_ref, g_ref, m_ref, v_ref, *rest):
        go_ref, d_ref, mo_ref, vo_ref = rest[n_prev:]
        gv = g_ref[...]
        go_ref[...] = gv
        d_ref[...], mo_ref[...], vo_ref[...] = _adam_math(w_ref[...], gv, m_ref[...], v_ref[...])

    lay = pl.BlockSpec((None, tr, tc), lambda i: (layer, i // (C // tc), i % (C // tc)))
    flat = pl.BlockSpec((tr, tc), lambda i: (i // (C // tc), i % (C // tc)))
    return _call(
        body, name=name, grid=((R // tr) * (C // tc),), in_specs=[lay, flat, lay, lay] + _any_specs(n_prev), out_specs=[lay] * 4,
        out_shape=[jax.ShapeDtypeStruct((L, R, C), F32)] * 4, aliases={4 + k: k for k in range(n_prev)},
        sem=("parallel",), args=(w, g, m, v, *prev))


def _cond_rows(c_row, cw, name):
    D = c_row.shape[1]
    nr, fc = cw.shape

    def body(c_ref, e_ref, o_ref):
        o_ref[...] = jnp.zeros_like(o_ref)
        cv = c_ref[...]
        o_ref[0:1, 0:D] = cv * (1.0 / (1.0 + jnp.exp(-cv)))
        o_ref[8:8 + nr, 0:fc] = e_ref[...]

    return pl.pallas_call(body, name=name, out_shape=jax.ShapeDtypeStruct((16, max(D, fc)), F32))(c_row, cw)


def _ada_fwd(cact, ada_w, ada_b, layer, chip, name):
    _, D, NC = ada_w.shape
    tn = _tile(NC, 1024)
    nj = NC // tn

    def body(idx_ref, c_ref, w_ref, b_ref, o_ref):
        acc = jnp.dot(c_ref[...].astype(BF16), w_ref[0].astype(BF16), preferred_element_type=F32)
        o_ref[...] = acc + b_ref[pl.ds(idx_ref[0], 1), :]

    return pl.pallas_call(
        body, name=name,
        grid_spec=pltpu.PrefetchScalarGridSpec(
            num_scalar_prefetch=1, grid=(nj,),
            in_specs=[pl.BlockSpec((8, D), lambda j, idx: (0, 0)),
                      pl.BlockSpec((1, D, tn), lambda j, idx: (idx[0], 0, j)),
                      pl.BlockSpec((DEPTH, tn), lambda j, idx: (0, idx[1] * nj + j))],
            out_specs=pl.BlockSpec((8, tn), lambda j, idx: (0, j))),
        out_shape=jax.ShapeDtypeStruct((8, NC), F32), compiler_params=_params(("parallel",)),
    )(jnp.stack([layer, chip]).astype(jnp.int32), cact, ada_w, ada_b)


def _ada_grad_adamw(cact_t, dmod, w, m, v, name, comm=None):
    L, D, NC = w.shape
    tr = _tile(D, 128, 8)

    def body(c_ref, d_ref, w_ref, m_ref, v_ref, g_ref, dl_ref, mo_ref, vo_ref):
        g = jnp.dot(c_ref[...], d_ref[...], preferred_element_type=F32, precision=HIGHEST)
        g_ref[...] = g
        dl_ref[...], mo_ref[...], vo_ref[...] = _adam_math(w_ref[...], g, m_ref[...], v_ref[...])

    lay = pl.BlockSpec((None, tr, NC), lambda l, i: (l, i, 0))
    return _call(
        body, name=name, grid=(L, D // tr),
        in_specs=[pl.BlockSpec((tr, N_DEV), lambda l, i: (i, 0)), pl.BlockSpec((None, N_DEV, NC), lambda l, i: (l, 0, 0)), lay, lay, lay],
        out_specs=[lay] * 4, out_shape=[jax.ShapeDtypeStruct((L, D, NC), F32)] * 4,
        sem=("parallel", "parallel"), args=(cact_t, dmod, w, m, v), comm=comm)


def _sum_devices(gathered, name):
    n, R, C = gathered.shape

    def body(g_ref, o_ref):
        acc = g_ref[0]
        for j in range(1, n):
            acc = acc + g_ref[j]
        o_ref[...] = acc

    return pl.pallas_call(body, name=name, out_shape=jax.ShapeDtypeStruct((R, C), F32),
                          compiler_params=pltpu.CompilerParams(vmem_limit_bytes=VMEM_LIMIT))(gathered)


def _adamw_small(w, g, m, v, name):
    def body(w_ref, g_ref, m_ref, v_ref, d_ref, mo_ref, vo_ref):
        d_ref[...], mo_ref[...], vo_ref[...] = _adam_math(w_ref[...], g_ref[...], m_ref[...], v_ref[...])

    return pl.pallas_call(body, name=name, out_shape=[jax.ShapeDtypeStruct(w.shape, F32)] * 3)(w, g, m, v)


def _pad_rows(flat, unit=8 * LANES):
    n = flat.shape[0]
    total = -(-n // unit) * unit
    return jnp.pad(flat, (0, total - n)).reshape(total // LANES, LANES)


def _pad_lanes(v2d):
    return jnp.pad(v2d.reshape(1, -1), ((0, 0), (0, LANES - v2d.size)))


PLAN = {
    "fox_proj": [("gather", "ffn_w_up0", 0, 2, 8)],
    "fox_attn": [("gather", "ffn_w_up0", 2, 6, 8)],
    "fox_out": [("gather", "ffn_w_up0", 6, 7, 8)],
    "ln_mix0": [("gather", "ffn_w_up0", 7, 8, 8)],
    "ffn_up0": [("gather", "ffn_w_down0", 0, 1, 1)],
    "ffn_gate0": [("gather", "swa_w_in", 0, 1, 1)],
    "ffn_down0": [("gather", "swa_w_o", 0, 1, 1), ("gather", "ffn_w_up1", 0, 1, 8)],
    "ln_ffn0": [("gather", "ffn_w_up1", 1, 2, 8)],
    "swa_proj": [("gather", "ffn_w_up1", 2, 3, 8)],
    "swa_rope": [("gather", "ffn_w_up1", 3, 4, 8)],
    "swa_attn": [("gather", "ffn_w_up1", 4, 8, 8)],
    "ffn_up1": [("gather", "ffn_w_down1", 0, 1, 1)],
    "ffn_gate_bwd1": [("swap", "ffn_w_down1")],
    "ffn_up_dx1": [("scatter", "ffn_w_down1", 0, 1, 1)],
    "swa_out_dx": [("swap", "ffn_w_up1")],
    "swa_attn_bwd": [("scatter", "ffn_w_up1", 0, 6, 8), ("swap", "swa_w_o")],
    "swa_proj_dx": [("scatter", "swa_w_o", 0, 1, 1)],
    "ffn_down_dx0": [("scatter", "ffn_w_up1", 6, 8, 8), ("swap", "swa_w_in")],
    "ffn_down_dw0": [("scatter", "swa_w_in", 0, 1, 1)],
    "ffn_gate_bwd0": [("swap", "ffn_w_down0")],
    "ffn_up_dx0": [("scatter", "ffn_w_down0", 0, 1, 1)],
    "fox_out_dx": [("swap", "ffn_w_up0")],
    "fox_attn_bwd": [("scatter", "ffn_w_up0", 0, 5, 8), ("swap", "fox_w_o")],
    "fox_proj_dw": [("scatter", "fox_w_o", 0, 1, 1), ("scatter", "ffn_w_up0", 5, 6, 8)],
    "fox_proj_dx": [("scatter", "ffn_w_up0", 6, 8, 8), ("swap", "fox_w_in")],
}


class Exchanges:
    def __init__(self, dm, slots, chip, c):
        self.dm, self.slots, self.chip, self.c = dm, dict(slots), chip, c
        self.raw, self.part, self.landed, self.grads, self.views, self.pending = {}, {}, {}, {}, {}, {}

    def gather_start(self, keys, name):
        self.first = (keys, _gather_comm([self.slots[k] for k in keys], [(0, 1, 1)] * len(keys), forward=False))
        self.first_state, token = _split_start(self.first[1], name + "_start")
        return token

    def gather_finish(self, after, name):
        keys, comm = self.first
        _split_wait(comm, self.first_state, after, name + "_wait")
        pass_on = _forward_comm(comm.results, [(0, 1, 1)] * len(keys))
        _run_comm(pass_on, name + "_pass")
        self.slots.update(zip(keys, pass_on.results))

    def w(self, key):
        if key not in self.views:
            S, D, FH, QH, KH, Fh = self.dm
            full = self.slots[key]
            if key == "fox_w_in":
                cols = full.shape[2]
                full = jnp.pad(full.transpose(1, 0, 2).reshape(D, N_CHIPS * cols), ((0, 0), (0, 3 * D + LANES - N_CHIPS * cols)))[None]
            elif key in ("fox_w_o", "swa_w_o"):
                full = full.reshape(1, D, D)
            elif key.startswith("ffn_w_down"):
                full = full.reshape(1, Fh, D)
            self.views[key] = full
        return self.views[key]

    def carry(self, stage):
        todo = []
        for kind, key, *chunk in PLAN.get(stage, ()):
            if kind == "gather":
                todo.append((kind, [key], _gather_comm([self.slots[key]], [tuple(chunk)])))
            elif kind == "swap":
                todo.append((kind, [key], _swap_comm([self.raw[key]])))
            elif kind == "scatter":
                todo.append((kind, [(key, *chunk)], _scatter_comm([self.part[key]], [self.landed.get(key)], [tuple(chunk)])))
        self.pending[stage] = todo
        return _merge([cm for _, _, cm in todo])

    def carried(self, stage, comm):
        for kind, keys, cm in self.pending.pop(stage):
            if kind == "gather":
                self.slots[keys[0]] = cm.results[0]
            elif kind == "swap":
                self.part[keys[0]] = _add_sibling(self.raw[keys[0]], cm.results[0], self.c, f"add_sibling_{keys[0]}")
            else:
                self.landed[keys[0][0]] = cm.results[0]

    def grad(self, key, g):
        S, D, FH, QH, KH, Fh = self.dm
        if key == "fox_w_in":
            cols = self.slots[key].shape[2]
            g = g[0][:, :N_CHIPS * cols].reshape(D, N_CHIPS, cols).transpose(1, 0, 2)
        elif key in ("fox_w_o", "swa_w_o"):
            g = g.reshape(N_CHIPS, D // N_CHIPS, D)
        elif key.startswith("ffn_w_down"):
            g = g.reshape(N_CHIPS, Fh // N_CHIPS, D)
        self.raw[key] = g

    def last_start(self, last):
        part = self.part[last]
        self.last = (last, _scatter_comm([part], [lax.empty(part.shape, part.dtype)], [(0, 1, 1)]))
        self.last_state, token = _split_start(self.last[1], "grads_last_start")
        return token

    def join_landed(self):
        keys = list(self.landed)
        join = _join_comm([_sum_chips(self.part[k], self.landed[k], self.chip, self.c, f"sum_chips_{k}") for k in keys])
        _run_comm(join, "grads_join")
        return dict(zip(keys, join.results))

    def last_finish(self, after):
        last, comm = self.last
        _split_wait(comm, self.last_state, after, "grads_last_wait")
        join = _join_comm([_sum_chips(self.part[last], comm.results[0], self.chip, self.c, f"sum_chips_{last}")])
        _run_comm(join, "grads_join_last")
        return join.results[0]


def _step(dm, a):
    S, D, FH, QH, KH, Fh = dm
    ix, iy, ic = lax.axis_index("x"), lax.axis_index("y"), lax.axis_index("c")
    chip = 2 * ix + iy
    dev = 2 * chip + ic
    F2c = a["ffn_w_up"].shape[2]
    NC = a["ada_w"].shape[2]

    names = ["fox_w_in", "fox_w_o", "swa_w_in", "swa_w_o", "ffn_w_up", "ffn_w_up", "ffn_w_down", "ffn_w_down"]
    layers = [0, 0, 0, 0, 0, 1, 0, 1]
    keys = ["fox_w_in", "fox_w_o", "swa_w_in", "swa_w_o", "ffn_w_up0", "ffn_w_up1", "ffn_w_down0", "ffn_w_down1"]
    slots = {k: _cast_bf16(a[nm], l, chip, f"cast_{k}") for k, nm, l in zip(keys, names, layers)}
    pp = Exchanges(dm, slots, chip, ic)
    token = pp.gather_start(["fox_w_in", "fox_w_o"], "gather_fox")

    e0 = _cond_rows(a["c"] + token[0:1, 0:1], a["ffn_conv_w"].reshape(DEPTH * 3, F2c), "silu_c")
    g0 = _allgather_small(e0, "gather_cond").reshape(N_DEV, 16, e0.shape[1])
    cact = g0[:, 0, :D]
    conv_w = g0[0::2, 8:8 + DEPTH * 3, :F2c].transpose(1, 0, 2).reshape(DEPTH, 3, N_CHIPS * F2c)
    rows = _ada_fwd(cact, a["ada_w"], a["ada_b"], ic, chip, "ada_proj")
    g1 = _allgather_small(rows, "gather_mod").reshape(N_CHIPS, DEPTH, 8, NC)
    mod = lax.dynamic_index_in_dim(g1, dev, axis=2, keepdims=False).transpose(1, 0, 2).reshape(DEPTH, N_CHIPS * NC)

    pp.gather_finish([mod], "gather_fox")
    sp = {"fox_b_f": _pad_lanes(a["fox_b_f"]), "sinks": jnp.repeat(a["swa_sinks"].reshape(KH, QH // KH), WIN, axis=1)[:, :, None],
          "conv_w": [conv_w[i] for i in range(DEPTH)], "conv_b": [a["ffn_conv_b"][i:i + 1] for i in range(DEPTH)]}
    for nm in ("ln_mix_g", "ln_mix_b", "ln_ffn_g", "ln_ffn_b"):
        sp[nm] = [a[nm][i:i + 1] for i in range(DEPTH)]

    loss_cols, grad_x, gs, dmod = _local_step(dm, a["x"][0], a["loss_target"][0], a["positions"][0], mod, sp, pp)
    loss = lax.psum(0.5 / D * jnp.sum(loss_cols), ("x", "y", "c"))
    out = {"loss": loss, "grad_x": grad_x[None]}

    token = pp.last_start("fox_w_in")
    dmod = dmod + token[0:1, 0:1]

    pieces = [dmod.reshape(-1), gs["fox_b_f"].reshape(-1), _pad_lanes(gs["sinks"]).reshape(-1),
              jnp.stack(gs["conv_w"]).reshape(-1), jnp.stack(gs["conv_b"]).reshape(-1)]
    pieces += [jnp.stack(gs[nm]).reshape(-1) for nm in ("ln_mix_g", "ln_mix_b", "ln_ffn_g", "ln_ffn_b")]
    sizes = [p.shape[0] for p in pieces]
    packed = _pad_rows(jnp.concatenate(pieces))
    allp = _allgather_small(packed, "gather_small").reshape(N_DEV, packed.shape[0], LANES)
    tot = _sum_devices(allp, "sum_small").reshape(-1)
    offs = [sum(sizes[:k]) for k in range(len(sizes))]
    take = lambda k: tot[offs[k]:offs[k] + sizes[k]]
    g_small = {"ada_b": take(0).reshape(DEPTH, -1), "fox_b_f": take(1)[:FH].reshape(1, FH), "swa_sinks": take(2)[:QH].reshape(1, QH),
               "ffn_conv_w": lax.dynamic_slice_in_dim(take(3).reshape(DEPTH, 3, N_CHIPS * F2c), chip * F2c, F2c, axis=2),
               "ffn_conv_b": take(4).reshape(DEPTH, -1)}
    for k, nm in enumerate(("ln_mix_g", "ln_mix_b", "ln_ffn_g", "ln_ffn_b")):
        g_small[nm] = take(5 + k).reshape(DEPTH, D)
    small = list(g_small)
    pack = lambda pre: _pad_rows(jnp.concatenate([(a[pre + nm] if pre else a[nm]).reshape(-1) for nm in small]))
    gp = _pad_rows(jnp.concatenate([g_small[nm].reshape(-1) for nm in small]))
    ds_, ms_, vs_ = _adamw_small(pack(""), gp, pack("m_"), pack("v_"), "adamw_small")
    off = 0
    for nm in small:
        n_el = a[nm].size
        out["grad_" + nm] = g_small[nm]
        for pre, arr in (("delta_", ds_), ("new_m_", ms_), ("new_v_", vs_)):
            out[pre + nm] = arr.reshape(-1)[off:off + n_el].reshape(a[nm].shape)
        off += n_el

    dmod_all = allp.reshape(N_DEV, -1)[:, :DEPTH * N_CHIPS * NC].reshape(N_DEV, DEPTH, N_CHIPS * NC)
    dmod_mine = lax.dynamic_slice_in_dim(dmod_all, chip * NC, NC, axis=2).transpose(1, 0, 2)
    ada = _ada_grad_adamw(cact.T, dmod_mine, a["ada_w"], a["m_ada_w"], a["v_ada_w"], "ada_grad")
    for pre, arr in zip(("grad_", "delta_", "new_m_", "new_v_"), ada):
        out[pre + "ada_w"] = arr

    grads = pp.join_landed()
    upd = {}
    for k, nm, l in zip(keys[1:], names[1:], layers[1:]):
        upd[nm] = _adamw(a[nm], grads[k], a["m_" + nm], a["v_" + nm], l, upd.get(nm, ()), f"adamw_{k}")
    g_last = pp.last_finish([ada[1], ds_] + [res[1] for res in upd.values()])
    tview = lambda t: jnp.swapaxes(t, 1, 2)
    res = _adamw(tview(a["fox_w_in"]), g_last.T, tview(a["m_fox_w_in"]), tview(a["v_fox_w_in"]), 0, (), "adamw_fox_w_in", by_cols=True)
    upd["fox_w_in"] = [tview(r) for r in res]
    for nm, res in upd.items():
        for pre, arr in zip(("grad_", "delta_", "new_m_", "new_v_"), res):
            out[pre + nm] = arr
    return out


_WEIGHTS = ["fox_w_in", "fox_b_f", "fox_w_o", "swa_w_in", "swa_sinks", "swa_w_o", "ada_w", "ada_b", "ffn_w_up", "ffn_conv_w",
            "ffn_conv_b", "ffn_w_down", "ln_mix_g", "ln_mix_b", "ln_ffn_g", "ln_ffn_b"]
_INPUTS = (["x", "c", "positions"] + _WEIGHTS + ["loss_target"] + ["m_" + w for w in _WEIGHTS] + ["v_" + w for w in _WEIGHTS])


def kernel(x, c, positions, fox_w_in, fox_b_f, fox_w_o, swa_w_in, swa_sinks, swa_w_o, ada_w, ada_b, ffn_w_up, ffn_conv_w, ffn_conv_b, ffn_w_down, ln_mix_g, ln_mix_b, ln_ffn_g, ln_ffn_b, loss_target, m_fox_w_in, m_fox_b_f, m_fox_w_o, m_swa_w_in, m_swa_sinks, m_swa_w_o, m_ada_w, m_ada_b, m_ffn_w_up, m_ffn_conv_w, m_ffn_conv_b, m_ffn_w_down, m_ln_mix_g, m_ln_mix_b, m_ln_ffn_g, m_ln_ffn_b, v_fox_w_in, v_fox_b_f, v_fox_w_o, v_swa_w_in, v_swa_sinks, v_swa_w_o, v_ada_w, v_ada_b, v_ffn_w_up, v_ffn_conv_w, v_ffn_conv_b, v_ffn_w_down, v_ln_mix_g, v_ln_mix_b, v_ln_ffn_g, v_ln_ffn_b):
    args = (x, c, positions, fox_w_in, fox_b_f, fox_w_o, swa_w_in, swa_sinks, swa_w_o, ada_w, ada_b, ffn_w_up, ffn_conv_w, ffn_conv_b, ffn_w_down, ln_mix_g, ln_mix_b, ln_ffn_g, ln_ffn_b, loss_target, m_fox_w_in, m_fox_b_f, m_fox_w_o, m_swa_w_in, m_swa_sinks, m_swa_w_o, m_ada_w, m_ada_b, m_ffn_w_up, m_ffn_conv_w, m_ffn_conv_b, m_ffn_w_down, m_ln_mix_g, m_ln_mix_b, m_ln_ffn_g, m_ln_ffn_b, v_fox_w_in, v_fox_b_f, v_fox_w_o, v_swa_w_in, v_swa_sinks, v_swa_w_o, v_ada_w, v_ada_b, v_ffn_w_up, v_ffn_conv_w, v_ffn_conv_b, v_ffn_w_down, v_ln_mix_g, v_ln_mix_b, v_ln_ffn_g, v_ln_ffn_b)
    out = _step(PROD, dict(zip(_INPUTS, args)))
    order = ["loss", "grad_x"] + [p + w for p in ("grad_", "delta_", "new_m_", "new_v_") for w in _WEIGHTS]
    return tuple(out[k] for k in order)
```

```python
import functools
from typing import NamedTuple

import jax
import jax.numpy as jnp
from jax import lax
from jax.experimental import pallas as pl
from jax.experimental.pallas import tpu as pltpu

F32 = jnp.float32
BF16 = jnp.bfloat16
MESH = pl.DeviceIdType.MESH
HIGHEST = lax.Precision.HIGHEST

N_CHIPS = 4
N_DEV = 8
LANES = 128
VMEM_LIMIT = 56 * 1024 * 1024

DEPTH = 2
DEEPNORM_ALPHA = (2.0 * DEPTH) ** 0.25
LN_EPS = 1e-5
ROPE_THETA = 500000.0
ADAM_LR, ADAM_B1, ADAM_B2, ADAM_EPS, ADAM_WD, ADAM_STEP = 0.001, 0.9, 0.999, 1e-08, 0.01, 10
NEG = -1e30


class Dims(NamedTuple):
    S: int
    D: int
    FH: int
    QH: int
    KH: int
    F: int


PROD = Dims(S=2048, D=2048, FH=16, QH=32, KH=4, F=5632)
FDH = 128
SDH = 64
WIN = 128
ROPE_DIM = 16
FOX_TQ = 256


def _params(sem=None, vmem=VMEM_LIMIT):
    return pltpu.CompilerParams(dimension_semantics=sem, vmem_limit_bytes=vmem)


def _tile(n, pref, unit=LANES):
    if n <= pref:
        return n
    t = (pref // unit) * unit
    while t > 0:
        if n % t == 0:
            return t
        t -= unit
    return n


class Comm:
    def __init__(self, args, out_shapes, aliases, n_sem, start, finish, members=()):
        self.args, self.out_shapes, self.aliases, self.n_sem = list(args), list(out_shapes), dict(aliases), n_sem
        self.start, self.finish = start, finish
        self.members = members
        self.results = None

    def set_results(self, res):
        self.results = list(res)
        for cm, o0 in self.members:
            cm.set_results(self.results[o0:o0 + len(cm.out_shapes)])


class _SemView:
    def __init__(self, sems, first):
        self.sems, self.first = sems, first

    @property
    def at(self):
        return self

    def __getitem__(self, k):
        return self.sems.at[self.first + k]


def _merge(comms):
    comms = [cm for cm in comms if cm is not None]
    if len(comms) < 2:
        return comms[0] if comms else None
    args, shapes, aliases, spans, n_sem = [], [], {}, [], 0
    for cm in comms:
        spans.append((len(args), len(shapes), n_sem))
        aliases.update({len(args) + a: len(shapes) + o for a, o in cm.aliases.items()})
        args += cm.args
        shapes += cm.out_shapes
        n_sem += cm.n_sem

    def each(step):
        def run(ar, ou, send, recv):
            for cm, (a0, o0, s0) in zip(comms, spans):
                getattr(cm, step)(ar[a0:a0 + len(cm.args)], ou[o0:o0 + len(cm.out_shapes)], _SemView(send, s0), _SemView(recv, s0))
        return run

    return Comm(args, shapes, aliases, n_sem, each("start"), each("finish"), [(cm, o0) for cm, (_, o0, _) in zip(comms, spans)])


def _place():
    x, y, c = lax.axis_index("x"), lax.axis_index("y"), lax.axis_index("c")
    chips = [(1 - x, y), (x, 1 - y), (1 - x, 1 - y)]
    return x, y, c, chips


def _remote(src, dst, send, recv, to):
    return pltpu.make_async_remote_copy(src_ref=src, dst_ref=dst, send_sem=send, recv_sem=recv, device_id=to, device_id_type=MESH)


def _any_specs(n):
    return [pl.BlockSpec(memory_space=pl.ANY)] * n


def _call(body, *, name, grid, in_specs, out_specs, out_shape, args, sem, scratch_shapes=(), aliases=None, comm=None):
    in_specs, out_specs, out_shape, scratch_shapes = list(in_specs), list(out_specs), list(out_shape), list(scratch_shapes)
    aliases = dict(aliases or {})
    if comm is None:
        return pl.pallas_call(body, name=name, grid=grid, in_specs=in_specs, out_specs=out_specs, out_shape=out_shape,
                              scratch_shapes=scratch_shapes, input_output_aliases=aliases, compiler_params=_params(sem))(*args)
    n_in, n_out, nc_in, nc_out, n_scr = len(in_specs), len(out_specs), len(comm.args), len(comm.out_shapes), len(scratch_shapes)

    def wrapped(*refs):
        ins, refs = refs[:n_in], refs[n_in:]
        cin, refs = refs[:nc_in], refs[nc_in:]
        outs, refs = refs[:n_out], refs[n_out:]
        cout, refs = refs[:nc_out], refs[nc_out:]
        scratch, (send, recv) = refs[:n_scr], refs[n_scr:]
        ids = [pl.program_id(k) for k in range(len(grid))]
        first = functools.reduce(jnp.logical_and, [i == 0 for i in ids])
        last = functools.reduce(jnp.logical_and, [i == g - 1 for i, g in zip(ids, grid)])

        @pl.when(first)
        def _():
            comm.start(cin, cout, send, recv)

        body(*ins, *outs, *scratch)

        @pl.when(last)
        def _():
            comm.finish(cin, cout, send, recv)

    res = pl.pallas_call(
        wrapped, name=name, grid=grid, in_specs=in_specs + _any_specs(nc_in), out_specs=out_specs + _any_specs(nc_out),
        out_shape=out_shape + comm.out_shapes,
        scratch_shapes=scratch_shapes + [pltpu.SemaphoreType.DMA((comm.n_sem,)), pltpu.SemaphoreType.DMA((comm.n_sem,))],
        input_output_aliases={**aliases, **{n_in + a: n_out + o for a, o in comm.aliases.items()}},
        compiler_params=_params(("arbitrary",) * len(grid)),
    )(*args, *comm.args)
    comm.set_results(res[n_out:])
    return list(res[:n_out])


def _run_comm(comm, name):
    nc_in, nc_out = len(comm.args), len(comm.out_shapes)

    def body(*refs):
        cin, cout, (send, recv) = refs[:nc_in], refs[nc_in:nc_in + nc_out], refs[nc_in + nc_out:]
        comm.start(cin, cout, send, recv)
        comm.finish(cin, cout, send, recv)

    res = pl.pallas_call(
        body, name=name, in_specs=_any_specs(nc_in), out_specs=_any_specs(nc_out), out_shape=comm.out_shapes,
        scratch_shapes=[pltpu.SemaphoreType.DMA((comm.n_sem,)), pltpu.SemaphoreType.DMA((comm.n_sem,))],
        input_output_aliases=comm.aliases,
    )(*comm.args)
    comm.set_results(res)


_HBM = pl.BlockSpec(memory_space=pltpu.HBM)
_SEM = pl.BlockSpec(memory_space=pltpu.SEMAPHORE)
_EFFECT = pltpu.SideEffectType.DATAFLOW_SIDE_EFFECTING


def _split_start(comm, name, after=()):
    n = len(comm.args)
    back = {o: a for a, o in comm.aliases.items()}
    assert len(back) == len(comm.out_shapes)

    n_after = len(after)

    def body(*refs):
        refs = refs[n + n_after:]
        send, recv, thru, token = refs[0], refs[1], refs[2:n + 2], refs[n + 2]
        comm.start(thru, [thru[back[o]] for o in range(len(back))], send, recv)
        token[...] = jnp.zeros_like(token)

    res = pl.pallas_call(
        body, name=name,
        out_shape=(pltpu.SemaphoreType.DMA((comm.n_sem,)), pltpu.SemaphoreType.DMA((comm.n_sem,)),
                   *[pltpu.HBM(a.shape, a.dtype) for a in comm.args], jax.ShapeDtypeStruct((8, LANES), F32)),
        in_specs=[_HBM] * n + _any_specs(n_after), out_specs=(_SEM, _SEM, *[_HBM] * n, pl.BlockSpec(memory_space=pltpu.VMEM)),
        input_output_aliases={i: 2 + i for i in range(n)},
        compiler_params=pltpu.CompilerParams(has_side_effects=_EFFECT),
    )(*[pltpu.with_memory_space_constraint(a, pltpu.HBM) for a in comm.args], *after)
    return (res[0], res[1], list(res[2:2 + n])), res[2 + n]


def _split_wait(comm, state, after, name):
    send, recv, thru = state
    n, n_after = len(thru), len(after)
    back = {o: a for a, o in comm.aliases.items()}

    def body(*refs):
        ins, send_ref, recv_ref = refs[:n], refs[n], refs[n + 1]
        comm.finish(ins, [ins[back[o]] for o in range(len(back))], send_ref, recv_ref)

    res = pl.pallas_call(
        body, name=name, out_shape=tuple(pltpu.HBM(a.shape, a.dtype) for a in thru),
        in_specs=[_HBM] * n + [_SEM, _SEM] + _any_specs(n_after), out_specs=[_HBM] * n,
        input_output_aliases={i: i for i in range(n)}, compiler_params=pltpu.CompilerParams(has_side_effects=_EFFECT),
    )(*thru, send, recv, *after)
    comm.set_results([res[back[o]] for o in range(len(back))])


def _forward_comm(slots, chunks):
    n = len(slots)

    def rows(t, who):
        rh = slots[t].shape[1] // 2
        lo, hi, nch = chunks[t]
        rc = rh // nch
        return pl.ds(who * rh + lo * rc, (hi - lo) * rc)

    def copy(outs, send, recv, t, j, chip, who):
        x, y, c, _ = _place()
        blk = outs[t].at[2 * chip[0] + chip[1], rows(t, who)]
        return _remote(blk, blk, send.at[3 * t + j], recv.at[3 * t + j], (x, y, 1 - c))

    def start(args, outs, send, recv):
        _, _, c, chips = _place()
        for t in range(n):
            for j, chip in enumerate(chips):
                copy(outs, send, recv, t, j, chip, c).start()

    def finish(args, outs, send, recv):
        _, _, c, chips = _place()
        for t in range(n):
            for j, chip in enumerate(chips):
                copy(outs, send, recv, t, j, chip, 1 - c).wait_recv()
        for t in range(n):
            for j, chip in enumerate(chips):
                copy(outs, send, recv, t, j, chip, c).wait_send()

    shapes = [jax.ShapeDtypeStruct(w.shape, w.dtype) for w in slots]
    return Comm(slots, shapes, {t: t for t in range(n)}, 3 * n, start, finish)


def _gather_comm(slots, chunks, forward=True):
    n = len(slots)

    def rows(t, who):
        rh = slots[t].shape[1] // 2
        lo, hi, nch = chunks[t]
        rc = rh // nch
        return pl.ds(who * rh + lo * rc, (hi - lo) * rc)

    def start(args, outs, send, recv):
        x, y, c, chips = _place()
        s = 2 * x + y
        for t in range(n):
            mine = outs[t].at[s, rows(t, c)]
            for j, chip in enumerate(chips):
                _remote(mine, mine, send.at[6 * t + j], recv.at[6 * t + j], (*chip, c)).start()

    def finish(args, outs, send, recv):
        x, y, c, chips = _place()
        s = 2 * x + y
        sib = (x, y, 1 - c)
        for t in range(n):
            for j, chip in enumerate(chips):
                blk = outs[t].at[2 * chip[0] + chip[1], rows(t, c)]
                _remote(blk, blk, send.at[6 * t + j], recv.at[6 * t + j], (*chip, c)).wait_recv()
                if forward:
                    _remote(blk, blk, send.at[6 * t + 3 + j], recv.at[6 * t + 3 + j], sib).start()
        for t in range(n if forward else 0):
            for j, chip in enumerate(chips):
                blk = outs[t].at[2 * chip[0] + chip[1], rows(t, 1 - c)]
                _remote(blk, blk, send.at[6 * t + 3 + j], recv.at[6 * t + 3 + j], sib).wait_recv()
        for t in range(n):
            mine = outs[t].at[s, rows(t, c)]
            for j, chip in enumerate(chips):
                _remote(mine, mine, send.at[6 * t + j], recv.at[6 * t + j], (*chip, c)).wait_send()
                if forward:
                    blk = outs[t].at[2 * chip[0] + chip[1], rows(t, c)]
                    _remote(blk, blk, send.at[6 * t + 3 + j], recv.at[6 * t + 3 + j], sib).wait_send()

    shapes = [jax.ShapeDtypeStruct(w.shape, w.dtype) for w in slots]
    return Comm(slots, shapes, {t: t for t in range(n)}, 6 * n, start, finish)


def _scatter_comm(parts, landed, chunks):
    n = len(parts)
    prev = [t for t in range(n) if landed[t] is not None]

    def rows(t):
        lo, hi, nch = chunks[t]
        rc = parts[t].shape[1] // nch
        return pl.ds(lo * rc, (hi - lo) * rc)

    def start(args, outs, send, recv):
        x, y, c, chips = _place()
        s = 2 * x + y
        for t in range(n):
            for j, chip in enumerate(chips):
                _remote(args[t].at[2 * chip[0] + chip[1], rows(t)], outs[t].at[s, rows(t)],
                        send.at[3 * t + j], recv.at[3 * t + j], (*chip, c)).start()

    def finish(args, outs, send, recv):
        x, y, c, chips = _place()
        for t in range(n):
            for j, chip in enumerate(chips):
                blk = outs[t].at[2 * chip[0] + chip[1], rows(t)]
                _remote(blk, blk, send.at[3 * t + j], recv.at[3 * t + j], (*chip, c)).wait_recv()
        for t in range(n):
            for j, chip in enumerate(chips):
                src = args[t].at[2 * chip[0] + chip[1], rows(t)]
                _remote(src, src, send.at[3 * t + j], recv.at[3 * t + j], (*chip, c)).wait_send()

    shapes = [jax.ShapeDtypeStruct(p.shape, p.dtype) for p in parts]
    return Comm(list(parts) + [landed[t] for t in prev], shapes, {n + i: t for i, t in enumerate(prev)}, 3 * n, start, finish)


def _swap_comm(gs):
    n = len(gs)

    def copy(args, outs, send, recv, t):
        _, _, c, _ = _place()
        rh = gs[t].shape[1] // 2
        x, y = lax.axis_index("x"), lax.axis_index("y")
        return _remote(args[t].at[:, pl.ds((1 - c) * rh, rh), :], outs[t], send.at[t], recv.at[t], (x, y, 1 - c))

    def start(args, outs, send, recv):
        for t in range(n):
            copy(args, outs, send, recv, t).start()

    def finish(args, outs, send, recv):
        for t in range(n):
            copy(args, outs, send, recv, t).wait()

    shapes = [jax.ShapeDtypeStruct((g.shape[0], g.shape[1] // 2, g.shape[2]), g.dtype) for g in gs]
    return Comm(gs, shapes, {}, n, start, finish)


def _join_comm(gs):
    n = len(gs)

    def half(outs, t, who):
        rh = gs[t].shape[0] // 2
        return outs[t].at[pl.ds(who * rh, rh), :]

    def start(args, outs, send, recv):
        x, y, c, _ = _place()
        for t in range(n):
            _remote(half(outs, t, c), half(outs, t, c), send.at[t], recv.at[t], (x, y, 1 - c)).start()

    def finish(args, outs, send, recv):
        x, y, c, _ = _place()
        for t in range(n):
            _remote(half(outs, t, 1 - c), half(outs, t, 1 - c), send.at[t], recv.at[t], (x, y, 1 - c)).wait_recv()
        for t in range(n):
            _remote(half(outs, t, c), half(outs, t, c), send.at[t], recv.at[t], (x, y, 1 - c)).wait_send()

    shapes = [jax.ShapeDtypeStruct(g.shape, g.dtype) for g in gs]
    return Comm(gs, shapes, {t: t for t in range(n)}, n, start, finish)


_DN = {"nn": (((1,), (0,)), ((), ())), "nt": (((1,), (1,)), ((), ())), "tn": (((0,), (0,)), ((), ()))}


def _mm(a, b, *, mode, out_dtype, name, out_groups=1, tm=1024, tn=1024, tk=2048, comm=None):
    ga, ra, ca = a.shape
    gb, rb, cb = b.shape
    if mode == "nn":
        M, K, N = ra, ga * ca, gb * cb
        assert rb == K and ga == 1 or (rb == K)
    elif mode == "nt":
        M, K, N = ra, ga * ca, rb
        assert gb * cb == K
    else:
        K, M, N = ra, ga * ca, gb * cb
        assert rb == K
    go = out_groups
    if mode == "nn":
        tk = _tile(ca, tk); assert rb % tk == 0 and (ga == 1 or True)
        tn = _tile(min(cb, N // go), tn); tm = _tile(M, tm, 8)
    elif mode == "nt":
        tk = _tile(ca, tk); tk = _tile(cb, tk) if cb % tk else tk; assert ca % tk == 0 and cb % tk == 0
        tn = _tile(N // go, tn); tm = _tile(M, tm, 8)
    else:
        tk = _tile(K, tk, 8); tm = _tile(ca, tm); tn = _tile(min(cb, N // go), tn)
    assert (N // go) % tn == 0 and M % tm == 0 and K % tk == 0, (name, M, N, K, tm, tn, tk)
    nk = K // tk
    kpa = max(ca // tk, 1)
    kpb = max(cb // tk, 1)
    npb = max(cb // tn, 1)
    npo = (N // go) // tn
    mpa = max(ca // tm, 1)

    if mode == "nn":
        a_spec = pl.BlockSpec((1, tm, tk), lambda j, i, k: (k // kpa, i, k % kpa))
        b_spec = pl.BlockSpec((1, tk, tn), lambda j, i, k: (j // npb, k, j % npb))
    elif mode == "nt":
        a_spec = pl.BlockSpec((1, tm, tk), lambda j, i, k: (k // kpa, i, k % kpa))
        b_spec = pl.BlockSpec((1, tn, tk), lambda j, i, k: (k // kpb, j, k % kpb))
    else:
        a_spec = pl.BlockSpec((1, tk, tm), lambda j, i, k: (i // mpa, k, i % mpa))
        b_spec = pl.BlockSpec((1, tk, tn), lambda j, i, k: (j // npb, k, j % npb))
    o_spec = pl.BlockSpec((1, tm, tn), lambda j, i, k: (j // npo, i, j % npo))
    dn = _DN[mode]

    def body(a_ref, b_ref, o_ref, *acc):
        p = lax.dot_general(a_ref[0], b_ref[0], dn, preferred_element_type=F32)
        if nk == 1:
            o_ref[0] = p.astype(out_dtype)
        else:
            k = pl.program_id(2)

            @pl.when(k == 0)
            def _():
                acc[0][...] = p

            @pl.when(k > 0)
            def _():
                acc[0][...] += p

            @pl.when(k == nk - 1)
            def _():
                o_ref[0] = acc[0][...].astype(out_dtype)

    return _call(
        body, name=name, grid=(N // tn, M // tm, nk), in_specs=[a_spec, b_spec], out_specs=[o_spec],
        out_shape=[jax.ShapeDtypeStruct((go, M, N // go), out_dtype)],
        scratch_shapes=[pltpu.VMEM((tm, tn), F32)] if nk > 1 else [],
        sem=("parallel", "parallel", "arbitrary"), args=(a, b), comm=comm)[0]


def _rows(tr, d):
    return pl.BlockSpec((tr, d), lambda i: (i, 0))


def _vec(d):
    return pl.BlockSpec((1, d), lambda i: (0, 0))


def _modulate(x, sc, sh, name):
    S, D = x.shape
    tr = min(256, S)

    def body(x_ref, sc_ref, sh_ref, h_ref):
        h_ref[...] = (x_ref[...] * (1.0 + sc_ref[...]) + sh_ref[...]).astype(BF16)

    return pl.pallas_call(
        body, name=name, grid=(S // tr,), in_specs=[_rows(tr, D), _vec(D), _vec(D)], out_specs=_rows(tr, D),
        out_shape=jax.ShapeDtypeStruct((S, D), BF16), compiler_params=_params(("parallel",)),
    )(x, sc, sh)


def _ln_fwd(x, y, gate, gamma, beta, sc, sh, name, comm=None):
    S, D = x.shape
    tr = min(256, S)
    emit_h = sc is not None

    def body(*refs):
        if emit_h:
            x_ref, y_ref, g_ref, ga_ref, be_ref, sc_ref, sh_ref, xo_ref, xh_ref, rs_ref, h_ref = refs
        else:
            x_ref, y_ref, g_ref, ga_ref, be_ref, xo_ref, xh_ref, rs_ref = refs
        z = DEEPNORM_ALPHA * x_ref[...] + (1.0 + g_ref[...]) * y_ref[...]
        mu = jnp.mean(z, axis=-1, keepdims=True)
        zc = z - mu
        var = jnp.mean(zc * zc, axis=-1, keepdims=True)
        rstd = lax.rsqrt(var + LN_EPS)
        xh = zc * rstd
        xo = xh * ga_ref[...] + be_ref[...]
        xo_ref[...] = xo
        xh_ref[...] = xh
        rs_ref[...] = rstd
        if emit_h:
            h_ref[...] = (xo * (1.0 + sc_ref[...]) + sh_ref[...]).astype(BF16)

    ins = [x, y, gate, gamma, beta] + ([sc, sh] if emit_h else [])
    in_specs = [_rows(tr, D), _rows(tr, D)] + [_vec(D)] * (len(ins) - 2)
    out_shape = [jax.ShapeDtypeStruct((S, D), F32), jax.ShapeDtypeStruct((S, D), F32), jax.ShapeDtypeStruct((S, 1), F32)]
    out_specs = [_rows(tr, D), _rows(tr, D), _rows(tr, 1)]
    if emit_h:
        out_shape.append(jax.ShapeDtypeStruct((S, D), BF16))
        out_specs.append(_rows(tr, D))
    return _call(body, name=name, grid=(S // tr,), in_specs=in_specs, out_specs=out_specs, out_shape=out_shape,
                 sem=("parallel",), args=ins, comm=comm)


def _loss_head(xf, tgt, name):
    S, D = xf.shape
    tr = min(256, S)

    def body(x_ref, t_ref, dx_ref, l_ref):
        e = x_ref[...] - t_ref[...]
        dx_ref[...] = e * (1.0 / D)

        @pl.when(pl.program_id(0) == 0)
        def _():
            l_ref[...] = jnp.zeros_like(l_ref)

        l_ref[...] += jnp.sum(e * e, axis=0, keepdims=True)

    return pl.pallas_call(
        body, name=name, grid=(S // tr,), in_specs=[_rows(tr, D), _rows(tr, D)],
        out_specs=[_rows(tr, D), _vec(D)],
        out_shape=[jax.ShapeDtypeStruct((S, D), F32), jax.ShapeDtypeStruct((1, D), F32)],
        compiler_params=_params(("arbitrary",)),
    )(xf, tgt)


def _ln_bwd(dxo, xh, rstd, gamma, y, gate, name, pre=None):
    S, D = dxo.shape
    tr = min(256, S)
    n_pre = 0 if pre is None else 3

    def body(dx_ref, xh_ref, rs_ref, ga_ref, y_ref, g_ref, *rest):
        dres_ref, dy_ref, dga_ref, dbe_ref, dg_ref = rest[n_pre:n_pre + 5]
        first = pl.program_id(0) == 0
        dxo_ = dx_ref[...]
        xh_ = xh_ref[...]
        if pre is not None:
            dh_ref, sc_ref, be_ref = rest[:3]
            dsc_ref, dsh_ref = rest[n_pre + 5:]
            dh_ = dh_ref[...]
            dxo_ = dxo_ + dh_ * (1.0 + sc_ref[...])

            @pl.when(first)
            def _():
                dsc_ref[...] = jnp.zeros_like(dsc_ref)
                dsh_ref[...] = jnp.zeros_like(dsh_ref)

            dsc_ref[...] += jnp.sum(dh_ * (xh_ * ga_ref[...] + be_ref[...]), axis=0, keepdims=True)
            dsh_ref[...] += jnp.sum(dh_, axis=0, keepdims=True)
        dxh = dxo_ * ga_ref[...]
        m1 = jnp.mean(dxh, axis=-1, keepdims=True)
        m2 = jnp.mean(dxh * xh_, axis=-1, keepdims=True)
        dz = rs_ref[...] * (dxh - m1 - xh_ * m2)
        dres_ref[...] = DEEPNORM_ALPHA * dz
        dy_ref[...] = ((1.0 + g_ref[...]) * dz).astype(BF16)

        @pl.when(first)
        def _():
            dga_ref[...] = jnp.zeros_like(dga_ref)
            dbe_ref[...] = jnp.zeros_like(dbe_ref)
            dg_ref[...] = jnp.zeros_like(dg_ref)

        dga_ref[...] += jnp.sum(dxo_ * xh_, axis=0, keepdims=True)
        dbe_ref[...] += jnp.sum(dxo_, axis=0, keepdims=True)
        dg_ref[...] += jnp.sum(dz * y_ref[...], axis=0, keepdims=True)

    extra_in = [] if pre is None else [_rows(tr, D), _vec(D), _vec(D)]
    return pl.pallas_call(
        body, name=name, grid=(S // tr,),
        in_specs=[_rows(tr, D), _rows(tr, D), _rows(tr, 1), _vec(D), _rows(tr, D), _vec(D)] + extra_in,
        out_specs=[_rows(tr, D), _rows(tr, D)] + [_vec(D)] * (3 + (0 if pre is None else 2)),
        out_shape=[jax.ShapeDtypeStruct((S, D), F32), jax.ShapeDtypeStruct((S, D), BF16)]
        + [jax.ShapeDtypeStruct((1, D), F32)] * (3 + (0 if pre is None else 2)),
        compiler_params=_params(("arbitrary",)),
    )(dxo, xh, rstd, gamma, y, gate, *(pre or ()))


def _mod_bwd(dh, x, sc, dres, name):
    S, D = x.shape
    tr = min(256, S)

    def body(dh_ref, x_ref, sc_ref, dr_ref, dx_ref, dsc_ref, dsh_ref):
        dh_ = dh_ref[...]
        dx_ref[...] = dr_ref[...] + dh_ * (1.0 + sc_ref[...])

        @pl.when(pl.program_id(0) == 0)
        def _():
            dsc_ref[...] = jnp.zeros_like(dsc_ref)
            dsh_ref[...] = jnp.zeros_like(dsh_ref)

        dsc_ref[...] += jnp.sum(dh_ * x_ref[...], axis=0, keepdims=True)
        dsh_ref[...] += jnp.sum(dh_, axis=0, keepdims=True)

    return pl.pallas_call(
        body, name=name, grid=(S // tr,),
        in_specs=[_rows(tr, D), _rows(tr, D), _vec(D), _rows(tr, D)],
        out_specs=[_rows(tr, D), _vec(D), _vec(D)],
        out_shape=[jax.ShapeDtypeStruct((S, D), F32), jax.ShapeDtypeStruct((1, D), F32), jax.ShapeDtypeStruct((1, D), F32)],
        compiler_params=_params(("arbitrary",)),
    )(dh, x, sc, dres)


def _log_sigmoid(z):
    return jnp.minimum(z, 0.0) - jnp.log(1.0 + jnp.exp(-jnp.abs(z)))


def _fox_gate_fwd(proj, b_f, n_heads, name):
    S, PW = proj.shape
    blk = min(256, S)
    last = PW // LANES - 1

    def body(fl_ref, b_ref, cum_ref):
        r = lax.broadcasted_iota(jnp.int32, (blk, blk), 0)
        c = lax.broadcasted_iota(jnp.int32, (blk, blk), 1)
        tril = (c <= r).astype(F32)
        carry = jnp.zeros((1, LANES), F32)
        for i in range(S // blk):
            lf = _log_sigmoid(fl_ref[i * blk:(i + 1) * blk, :] + b_ref[...])
            cum_ref[i * blk:(i + 1) * blk, :] = jnp.dot(tril, lf, preferred_element_type=F32, precision=HIGHEST) + carry
            carry = carry + jnp.sum(lf, axis=0, keepdims=True)

    return pl.pallas_call(
        body, name=name, grid=(1,),
        in_specs=[pl.BlockSpec((S, LANES), lambda i: (0, last)), pl.BlockSpec((1, LANES), lambda i: (0, 0))],
        out_specs=pl.BlockSpec((S, LANES), lambda i: (0, 0)),
        out_shape=jax.ShapeDtypeStruct((S, LANES), F32), compiler_params=_params(("arbitrary",)),
    )(proj, b_f)


def _fox_gate_bwd(dcum, proj, b_f, n_heads, name):
    S, PW = proj.shape
    blk = min(256, S)
    last = PW // LANES - 1
    nb = S // blk

    def body(dc_ref, fl_ref, b_ref, dfl_ref, db_ref):
        r = lax.broadcasted_iota(jnp.int32, (blk, blk), 0)
        c = lax.broadcasted_iota(jnp.int32, (blk, blk), 1)
        triu = (c >= r).astype(F32)
        lane = lax.broadcasted_iota(jnp.int32, (blk, LANES), 1)
        carry = jnp.zeros((1, LANES), F32)
        dbs = jnp.zeros((1, LANES), F32)
        for i in reversed(range(nb)):
            dc = dc_ref[i * blk:(i + 1) * blk, :]
            dlf = jnp.dot(triu, dc, preferred_element_type=F32, precision=HIGHEST) + carry
            carry = carry + jnp.sum(dc, axis=0, keepdims=True)
            z = fl_ref[i * blk:(i + 1) * blk, :] + b_ref[...]
            e = jnp.exp(-jnp.abs(z))
            sig_neg = jnp.where(z >= 0, e / (1.0 + e), 1.0 / (1.0 + e))
            dfl = jnp.where(lane < n_heads, dlf * sig_neg, 0.0)
            dfl_ref[i * blk:(i + 1) * blk, :] = dfl.astype(BF16)
            dbs = dbs + jnp.sum(dfl, axis=0, keepdims=True)
        db_ref[...] = dbs

    return pl.pallas_call(
        body, name=name, grid=(1,),
        in_specs=[pl.BlockSpec((S, LANES), lambda i: (0, 0)), pl.BlockSpec((S, LANES), lambda i: (0, last)),
                  pl.BlockSpec((1, LANES), lambda i: (0, 0))],
        out_specs=[pl.BlockSpec((S, LANES), lambda i: (0, 0)), pl.BlockSpec((1, LANES), lambda i: (0, 0))],
        out_shape=[jax.ShapeDtypeStruct((S, LANES), BF16), jax.ShapeDtypeStruct((1, LANES), F32)],
        compiler_params=_params(("arbitrary",)),
    )(dcum, proj, b_f)


def _fox_scores(q_ref, kb_ref, cq_ref, ck_ref, qi, tq, scale):
    kk = (qi + 1) * tq
    rows = slice(qi * tq, (qi + 1) * tq)
    qb = q_ref[rows, :].astype(BF16)
    s = lax.dot_general(qb, kb_ref[0:kk, :], _DN["nt"], preferred_element_type=F32) * scale
    s = s + (cq_ref[0, rows, :] - ck_ref[0, :, 0:kk])
    r = lax.broadcasted_iota(jnp.int32, (tq, kk), 0) + qi * tq
    c = lax.broadcasted_iota(jnp.int32, (tq, kk), 1)
    mask = c <= r
    return jnp.where(mask, s, NEG), mask, qb


def _fox_fwd(proj, cq, ck, n_heads, name, comm=None):
    S = proj.shape[0]
    H = n_heads
    tq = min(FOX_TQ, S)
    nq = S // tq
    scale = FDH ** -0.5

    def body(q_ref, k_ref, v_ref, cq_ref, ck_ref, o_ref, lse_ref, kb_ref, vb_ref):
        kb_ref[...] = k_ref[...].astype(BF16)
        vb_ref[...] = v_ref[...].astype(BF16)
        for qi in range(nq):
            kk = (qi + 1) * tq
            rows = slice(qi * tq, (qi + 1) * tq)
            s, _, _ = _fox_scores(q_ref, kb_ref, cq_ref, ck_ref, qi, tq, scale)
            m = jnp.max(s, axis=-1, keepdims=True)
            p = jnp.exp(s - m)
            l = jnp.sum(p, axis=-1, keepdims=True)
            p = p * (1.0 / l)
            o_ref[rows, :] = jnp.dot(p.astype(BF16), vb_ref[0:kk, :], preferred_element_type=F32).astype(BF16)
            lse_ref[0, rows, :] = m + jnp.log(l)

    col = lambda off: pl.BlockSpec((S, FDH), lambda h: (0, h + off))
    stat_c = pl.BlockSpec((1, S, 1), lambda h: (h, 0, 0))
    stat_r = pl.BlockSpec((1, 1, S), lambda h: (h, 0, 0))
    return _call(
        body, name=name, grid=(H,),
        in_specs=[col(0), col(H), col(2 * H), stat_c, stat_r],
        out_specs=[col(0), stat_c],
        out_shape=[jax.ShapeDtypeStruct((S, H * FDH), BF16), jax.ShapeDtypeStruct((H, S, 1), F32)],
        scratch_shapes=[pltpu.VMEM((S, FDH), BF16), pltpu.VMEM((S, FDH), BF16)],
        sem=("parallel",), args=(proj, proj, proj, cq, ck), comm=comm)


def _fox_bwd(proj, cq, ck, lse, do, n_heads, name, comm=None):
    S = proj.shape[0]
    H = n_heads
    tq = min(FOX_TQ, S)
    nq = S // tq
    scale = FDH ** -0.5

    def body(q_ref, k_ref, v_ref, cq_ref, ck_ref, lse_ref, do_ref, dq_ref, dk_ref, dv_ref, dcq_ref, dck_ref,
             kb_ref, vb_ref, dka_ref, dva_ref):
        kb_ref[...] = k_ref[...].astype(BF16)
        vb_ref[...] = v_ref[...].astype(BF16)
        dka_ref[...] = jnp.zeros_like(dka_ref)
        dva_ref[...] = jnp.zeros_like(dva_ref)
        dck_ref[...] = jnp.zeros_like(dck_ref)
        for qi in range(nq):
            kk = (qi + 1) * tq
            rows = slice(qi * tq, (qi + 1) * tq)
            s, mask, qb = _fox_scores(q_ref, kb_ref, cq_ref, ck_ref, qi, tq, scale)
            p = jnp.where(mask, jnp.exp(s - lse_ref[0, rows, :]), 0.0)
            dob = do_ref[rows, :]
            dp = lax.dot_general(dob, vb_ref[0:kk, :], _DN["nt"], preferred_element_type=F32)
            delta = jnp.sum(p * dp, axis=-1, keepdims=True)
            ds = p * (dp - delta)
            dcq_ref[0, rows, :] = jnp.sum(ds, axis=-1, keepdims=True)
            dck_ref[0, :, 0:kk] -= jnp.sum(ds, axis=0, keepdims=True)
            dsb = (ds * scale).astype(BF16)
            dq_ref[rows, :] = jnp.dot(dsb, kb_ref[0:kk, :], preferred_element_type=F32).astype(BF16)
            dka_ref[0:kk, :] += lax.dot_general(dsb, qb, _DN["tn"], preferred_element_type=F32)
            dva_ref[0:kk, :] += lax.dot_general(p.astype(BF16), dob, _DN["tn"], preferred_element_type=F32)
        dk_ref[...] = dka_ref[...].astype(BF16)
        dv_ref[...] = dva_ref[...].astype(BF16)

    col = lambda off: pl.BlockSpec((S, FDH), lambda h: (0, h + off))
    stat_c = pl.BlockSpec((1, S, 1), lambda h: (h, 0, 0))
    stat_r = pl.BlockSpec((1, 1, S), lambda h: (h, 0, 0))
    wide = jax.ShapeDtypeStruct((S, H * FDH), BF16)
    return _call(
        body, name=name, grid=(H,),
        in_specs=[col(0), col(H), col(2 * H), stat_c, stat_r, stat_c, col(0)],
        out_specs=[col(0), col(0), col(0), stat_c, stat_r],
        out_shape=[wide, wide, wide, jax.ShapeDtypeStruct((H, S, 1), F32), jax.ShapeDtypeStruct((H, 1, S), F32)],
        scratch_shapes=[pltpu.VMEM((S, FDH), BF16), pltpu.VMEM((S, FDH), BF16), pltpu.VMEM((S, FDH), F32), pltpu.VMEM((S, FDH), F32)],
        sem=("parallel",), args=(proj, proj, proj, cq, ck, lse, do), comm=comm)


def _rope_tables(pos, sign):
    inv = ROPE_THETA ** (-jnp.arange(0, ROPE_DIM, 2, dtype=F32) / ROPE_DIM)
    ang = pos.astype(F32)[:, None] * inv
    cos, sin = jnp.cos(ang), sign * jnp.sin(ang)
    l64 = jnp.arange(LANES) % SDH
    idx = l64 % (ROPE_DIM // 2)
    c = jnp.where(l64 < ROPE_DIM, cos[:, idx], 1.0)
    sa = jnp.where(l64 < ROPE_DIM // 2, -sin[:, idx], 0.0)
    sb = jnp.where((l64 >= ROPE_DIM // 2) & (l64 < ROPE_DIM), sin[:, idx], 0.0)
    rot = jnp.stack([c, sa, sb])
    ident = jnp.stack([jnp.ones_like(c), jnp.zeros_like(c), jnp.zeros_like(c)])
    return jnp.stack([rot, ident]).astype(F32)


def _rope(xin, tabs, n_rot, out_dtype, name, comm=None):
    S, W = xin.shape

    def body(x_ref, t_ref, o_ref):
        xv = x_ref[...]
        o = xv * t_ref[0, 0] + pltpu.roll(xv, LANES - ROPE_DIM // 2, 1) * t_ref[0, 1] + pltpu.roll(xv, ROPE_DIM // 2, 1) * t_ref[0, 2]
        o_ref[...] = o.astype(out_dtype)

    return _call(
        body, name=name, grid=(W // LANES,),
        in_specs=[pl.BlockSpec((S, LANES), lambda j: (0, j)),
                  pl.BlockSpec((1, 3, S, LANES), lambda j: (jnp.where(j < n_rot, 0, 1), 0, 0, 0))],
        out_specs=[pl.BlockSpec((S, LANES), lambda j: (0, j))],
        out_shape=[jax.ShapeDtypeStruct((S, W), out_dtype)], sem=("parallel",), args=(xin, tabs), comm=comm)[0]


SWA_PER_STEP = 2


def _swa_bias():
    r = jnp.arange(WIN)[:, None]
    c = jnp.arange(2 * WIN)[None, :]
    first = c <= r
    later = (c > r) & (c <= r + WIN)
    return jnp.where(jnp.stack([first, later]), 0.0, NEG).astype(F32)


def _swa_probs(q_ref, k_ref, sk_ref, b_ref, n, j, scale):
    st = pl.multiple_of(jnp.maximum(n - 1, 0) * WIN, WIN)
    qb = q_ref[0, j]
    kb = k_ref[0, pl.ds(st, 2 * WIN), :]
    gm = qb.shape[0]
    s = lax.dot_general(qb, kb, _DN["nt"], preferred_element_type=F32) * scale
    s = (s.reshape(gm // WIN, WIN, 2 * WIN) + b_ref[jnp.minimum(n, 1)][None]).reshape(gm, 2 * WIN)
    sink = sk_ref[0]
    m = jnp.maximum(jnp.max(s, axis=-1, keepdims=True), sink)
    e = jnp.exp(s - m)
    es = jnp.exp(sink - m)
    inv = 1.0 / (jnp.sum(e, axis=-1, keepdims=True) + es)
    return e * inv, es * inv, st, qb, kb


def _swa_specs(S, gm):
    blk = pl.BlockSpec((1, SWA_PER_STEP, gm, SDH), lambda g, n: (g, n, 0, 0))
    kv = pl.BlockSpec((1, S, SDH), lambda g, n: (g, 0, 0))
    col = pl.BlockSpec((1, gm, 1), lambda g, n: (g, 0, 0))
    bias = pl.BlockSpec((2, WIN, 2 * WIN), lambda g, n: (0, 0, 0))
    return blk, kv, col, bias


def _swa_fwd(q, k, v, sinks, name, comm=None):
    KH, nb, gm, _ = q.shape
    S = k.shape[1]
    scale = SDH ** -0.5

    def body(q_ref, k_ref, v_ref, sk_ref, b_ref, o_ref):
        for j in range(SWA_PER_STEP):
            p, _, st, _, _ = _swa_probs(q_ref, k_ref, sk_ref, b_ref, pl.program_id(1) * SWA_PER_STEP + j, j, scale)
            vb = v_ref[0, pl.ds(st, 2 * WIN), :]
            o_ref[0, j] = jnp.dot(p.astype(BF16), vb, preferred_element_type=F32).astype(BF16)

    blk, kv, col, bias = _swa_specs(S, gm)
    return _call(
        body, name=name, grid=(KH, nb // SWA_PER_STEP), in_specs=[blk, kv, kv, col, bias], out_specs=[blk],
        out_shape=[jax.ShapeDtypeStruct(q.shape, BF16)], sem=("parallel", "parallel"), args=(q, k, v, sinks, _swa_bias()), comm=comm)[0]


def _swa_bwd(q, k, v, sinks, do, name, comm=None):
    KH, nb, gm, _ = q.shape
    S = k.shape[1]
    scale = SDH ** -0.5

    def body(q_ref, k_ref, v_ref, sk_ref, b_ref, do_ref, dq_ref, dk_ref, dv_ref, dsk_ref):
        @pl.when(pl.program_id(1) == 0)
        def _():
            dk_ref[...] = jnp.zeros_like(dk_ref)
            dv_ref[...] = jnp.zeros_like(dv_ref)
            dsk_ref[...] = jnp.zeros_like(dsk_ref)

        blocks = []
        for j in range(SWA_PER_STEP):
            p, ps, st, qb, kb = _swa_probs(q_ref, k_ref, sk_ref, b_ref, pl.program_id(1) * SWA_PER_STEP + j, j, scale)
            vb = v_ref[0, pl.ds(st, 2 * WIN), :]
            dob = do_ref[0, j]
            dp = lax.dot_general(dob, vb, _DN["nt"], preferred_element_type=F32)
            delta = jnp.sum(p * dp, axis=-1, keepdims=True)
            dsb = (p * (dp - delta) * scale).astype(BF16)
            dq_ref[0, j] = jnp.dot(dsb, kb, preferred_element_type=F32)
            blocks.append((st, lax.dot_general(dsb, qb, _DN["tn"], preferred_element_type=F32),
                           lax.dot_general(p.astype(BF16), dob, _DN["tn"], preferred_element_type=F32), ps * delta))
        for st, dk, dv, dsk in blocks:
            dk_ref[0, pl.ds(st, 2 * WIN), :] += dk
            dv_ref[0, pl.ds(st, 2 * WIN), :] += dv
            dsk_ref[0] -= dsk

    blk, kv, col, bias = _swa_specs(S, gm)
    return _call(
        body, name=name, grid=(KH, nb // SWA_PER_STEP), in_specs=[blk, kv, kv, col, bias, blk], out_specs=[blk, kv, kv, col],
        out_shape=[jax.ShapeDtypeStruct(q.shape, F32), jax.ShapeDtypeStruct(k.shape, F32),
                   jax.ShapeDtypeStruct(k.shape, F32), jax.ShapeDtypeStruct(sinks.shape, F32)],
        sem=("parallel", "arbitrary"), args=(q, k, v, sinks, _swa_bias(), do), comm=comm)


def _shift_down(u, k):
    row = lax.broadcasted_iota(jnp.int32, u.shape, 0)
    return jnp.where(row >= k, pltpu.roll(u, k, 0), 0.0)


def _shift_up(u, k):
    n = u.shape[0]
    row = lax.broadcasted_iota(jnp.int32, u.shape, 0)
    return jnp.where(row < n - k, pltpu.roll(u, n - k, 0), 0.0)


def _conv3(u, w_ref, b_ref):
    return w_ref[0:1, :] * _shift_down(u, 2) + w_ref[1:2, :] * _shift_down(u, 1) + w_ref[2:3, :] * u + b_ref[...]


def _conv_gate(u, cw, cb, name, comm=None):
    S, F2 = u.shape
    Fh = F2 // 2
    tc = _tile(Fh, 256)
    nf = Fh // tc

    def body(ug_ref, uv_ref, wg_ref, wv_ref, bg_ref, bv_ref, a_ref):
        g = _conv3(ug_ref[...], wg_ref, bg_ref)
        val = _conv3(uv_ref[...], wv_ref, bv_ref)
        a_ref[...] = (g * (1.0 / (1.0 + jnp.exp(-g))) * val).astype(BF16)

    blk = lambda r, off: pl.BlockSpec((r, tc), lambda j: (0, j + off))
    return _call(
        body, name=name, grid=(nf,),
        in_specs=[blk(S, 0), blk(S, nf), blk(3, 0), blk(3, nf), blk(1, 0), blk(1, nf)], out_specs=[blk(S, 0)],
        out_shape=[jax.ShapeDtypeStruct((S, Fh), BF16)], sem=("parallel",), args=(u, u, cw, cw, cb, cb), comm=comm)[0]


def _conv_gate_bwd(u, da, cw, cb, name, comm=None):
    S, F2 = u.shape
    Fh = F2 // 2
    tc = _tile(Fh, 256)
    nf = Fh // tc

    def half(h, dx, uu, w_ref, du_ref, dw_ref, db_ref):
        up1, up2 = _shift_up(dx, 1), _shift_up(dx, 2)
        du = w_ref[2:3, :] * dx + w_ref[1:2, :] * up1 + w_ref[0:1, :] * up2
        du_ref[h] = du.astype(BF16)
        dw_ref[h, 0:1, :] = jnp.sum(up2 * uu, axis=0, keepdims=True)
        dw_ref[h, 1:2, :] = jnp.sum(up1 * uu, axis=0, keepdims=True)
        dw_ref[h, 2:3, :] = jnp.sum(dx * uu, axis=0, keepdims=True)
        db_ref[h] = jnp.sum(dx, axis=0, keepdims=True)

    def body(ug_ref, uv_ref, da_ref, wg_ref, wv_ref, bg_ref, bv_ref, du_ref, dw_ref, db_ref):
        ug = ug_ref[...]
        uv = uv_ref[...]
        g = _conv3(ug, wg_ref, bg_ref)
        val = _conv3(uv, wv_ref, bv_ref)
        sig = 1.0 / (1.0 + jnp.exp(-g))
        da_ = da_ref[...]
        dg = da_ * val * (sig * (1.0 + g * (1.0 - sig)))
        dval = da_ * (g * sig)
        half(0, dg, ug, wg_ref, du_ref, dw_ref, db_ref)
        half(1, dval, uv, wv_ref, du_ref, dw_ref, db_ref)

    blk = lambda r, off: pl.BlockSpec((r, tc), lambda j: (0, j + off))
    both = lambda r: pl.BlockSpec((2, r, tc), lambda j: (0, 0, j))
    return _call(
        body, name=name, grid=(nf,),
        in_specs=[blk(S, 0), blk(S, nf), blk(S, 0), blk(3, 0), blk(3, nf), blk(1, 0), blk(1, nf)],
        out_specs=[both(S), both(3), both(1)],
        out_shape=[jax.ShapeDtypeStruct((2, S, Fh), BF16), jax.ShapeDtypeStruct((2, 3, Fh), F32), jax.ShapeDtypeStruct((2, 1, Fh), F32)],
        sem=("parallel",), args=(u, u, da, cw, cw, cb, cb), comm=comm)


def _to_groups(t, kh):
    S, width = t.shape
    g = width // SDH // kh
    return t.reshape(S // WIN, WIN, kh, g, SDH).transpose(2, 0, 3, 1, 4).reshape(kh, S // WIN, g * WIN, SDH)


def _from_groups(t):
    kh, nb, gm, _ = t.shape
    g = gm // WIN
    return t.reshape(kh, nb, g, WIN, SDH).transpose(1, 3, 0, 2, 4).reshape(nb * WIN, kh * g * SDH)


class LocalWeights:
    def __init__(self, weights):
        self.weights, self.grads = weights, {}

    def w(self, name):
        return self.weights[name]

    def carry(self, stage):
        return None

    def carried(self, stage, comm):
        pass

    def grad(self, name, g):
        self.grads[name] = g


def _local_step(dm, x, tgt, pos, mod, sp, pp):
    S, D, FH, QH, KH, Fh = dm
    m = [[mod[i:i + 1, j * D:(j + 1) * D] for j in range(6)] for i in range(DEPTH)]

    def run(fn, *args, name, **kw):
        comm = pp.carry(name)
        out = fn(*args, name=name, comm=comm, **kw)
        if comm is not None:
            pp.carried(name, comm)
        return out

    sv = []
    xs = x
    h = _modulate(xs, m[0][1], m[0][0], "mod_in")
    for i in range(DEPTH):
        sh1, sc1, g1, sh2, sc2, g2 = m[i]
        L = {}
        L["x_in"], L["h1"] = xs, h
        if i == 0:
            proj = run(_mm, h[None], pp.w("fox_w_in"), mode="nn", out_dtype=F32, name="fox_proj", tn=896)[0]
            cum = _fox_gate_fwd(proj, sp["fox_b_f"], FH, "fox_gate")
            cq = cum[:, :FH].T[:, :, None]
            ck = cum[:, :FH].T[:, None, :]
            o, lse = run(_fox_fwd, proj, cq, ck, FH, name="fox_attn")
            L.update(proj=proj, cq=cq, ck=ck, lse=lse, o=o)
            y = run(_mm, o[None], pp.w("fox_w_o"), mode="nn", out_dtype=F32, name="fox_out")[0]
        else:
            proj = run(_mm, h[None], pp.w("swa_w_in"), mode="nn", out_dtype=F32, name="swa_proj", tn=640)[0]
            tabs = _rope_tables(pos, 1.0)
            n_rot = (QH + KH) * SDH // LANES
            pr = run(_rope, proj, tabs, n_rot, BF16, name="swa_rope")
            qh = _to_groups(pr[:, :QH * SDH], KH)
            kh = pr[:, QH * SDH:(QH + KH) * SDH].reshape(S, KH, SDH).transpose(1, 0, 2)
            vh = pr[:, (QH + KH) * SDH:].reshape(S, KH, SDH).transpose(1, 0, 2)
            oh = run(_swa_fwd, qh, kh, vh, sp["sinks"], name="swa_attn")
            o = _from_groups(oh)
            L.update(qh=qh, kh=kh, vh=vh, o=o)
            y = run(_mm, o[None], pp.w("swa_w_o"), mode="nn", out_dtype=F32, name="swa_out")[0]
        L["y1"] = y
        x1, L["xh1"], L["rs1"], h2 = run(_ln_fwd, xs, y, g1, sp["ln_mix_g"][i], sp["ln_mix_b"][i], sc2, sh2, name=f"ln_mix{i}")
        L["x1"], L["h2"] = x1, h2
        u = run(_mm, h2[None], pp.w(f"ffn_w_up{i}"), mode="nn", out_dtype=F32, name=f"ffn_up{i}", tm=512, tn=1408)[0]
        a = run(_conv_gate, u, sp["conv_w"][i], sp["conv_b"][i], name=f"ffn_gate{i}")
        y2 = run(_mm, a[None], pp.w(f"ffn_w_down{i}"), mode="nn", out_dtype=F32, name=f"ffn_down{i}", tk=5632, tm=512)[0]
        L.update(u=u, a=a, y2=y2)
        if i + 1 < DEPTH:
            xs, L["xh2"], L["rs2"], h = run(_ln_fwd, x1, y2, g2, sp["ln_ffn_g"][i], sp["ln_ffn_b"][i], m[i + 1][1], m[i + 1][0], name=f"ln_ffn{i}")
        else:
            xs, L["xh2"], L["rs2"] = run(_ln_fwd, x1, y2, g2, sp["ln_ffn_g"][i], sp["ln_ffn_b"][i], None, None, name=f"ln_ffn{i}")
        sv.append(L)

    dx, loss_cols = _loss_head(xs, tgt, "loss_head")

    gs = {k: [None] * DEPTH for k in ("conv_w", "conv_b", "ln_mix_g", "ln_mix_b", "ln_ffn_g", "ln_ffn_b")}
    dmp = [dict() for _ in range(DEPTH)]
    dres, pend = dx, None
    for i in reversed(range(DEPTH)):
        sh1, sc1, g1, sh2, sc2, g2 = m[i]
        L = sv[i]
        res = _ln_bwd(dres, L["xh2"], L["rs2"], sp["ln_ffn_g"][i], L["y2"], g2, f"ln_ffn_bwd{i}",
                      None if pend is None else (*pend, sp["ln_ffn_b"][i]))
        dres, dy, gs["ln_ffn_g"][i], gs["ln_ffn_b"][i], dmp[i]["g2"] = res[:5]
        if pend is not None:
            dmp[i + 1]["sc1"], dmp[i + 1]["sh1"] = res[5:]
        da = run(_mm, dy[None], pp.w(f"ffn_w_down{i}"), mode="nt", out_dtype=F32, name=f"ffn_down_dx{i}", tm=512, tn=1408)[0]
        pp.grad(f"ffn_w_down{i}", run(_mm, L["a"][None], dy[None], mode="tn", out_dtype=BF16, name=f"ffn_down_dw{i}", tm=1408))
        du, dcw, dcb = run(_conv_gate_bwd, L["u"], da, sp["conv_w"][i], sp["conv_b"][i], name=f"ffn_gate_bwd{i}")
        gs["conv_w"][i] = dcw.transpose(1, 0, 2).reshape(3, 2 * Fh)
        gs["conv_b"][i] = dcb.transpose(1, 0, 2).reshape(1, 2 * Fh)
        dh2 = run(_mm, du, pp.w(f"ffn_w_up{i}"), mode="nt", out_dtype=F32, name=f"ffn_up_dx{i}", tk=2816)[0]
        pp.grad(f"ffn_w_up{i}", run(_mm, L["h2"][None], du, mode="tn", out_dtype=BF16, name=f"ffn_up_dw{i}", out_groups=N_CHIPS, tn=1408))
        dres, dy, gs["ln_mix_g"][i], gs["ln_mix_b"][i], dmp[i]["g1"], dmp[i]["sc2"], dmp[i]["sh2"] = _ln_bwd(
            dres, L["xh1"], L["rs1"], sp["ln_mix_g"][i], L["y1"], g1, f"ln_mix_bwd{i}", (dh2, sc2, sp["ln_mix_b"][i]))
        if i == 0:
            do = run(_mm, dy[None], pp.w("fox_w_o"), mode="nt", out_dtype=BF16, name="fox_out_dx")[0]
            pp.grad("fox_w_o", run(_mm, L["o"][None], dy[None], mode="tn", out_dtype=BF16, name="fox_out_dw"))
            dq, dk, dv, dcq, dck = run(_fox_bwd, L["proj"], L["cq"], L["ck"], L["lse"], do, FH, name="fox_attn_bwd")
            dcum = dcq[:, :, 0].T + dck[:, 0, :].T
            dcum = jnp.pad(dcum, ((0, 0), (0, LANES - FH)))
            dfl, db_f = _fox_gate_bwd(dcum, L["proj"], sp["fox_b_f"], FH, "fox_gate_bwd")
            gs["fox_b_f"] = db_f
            dproj = jnp.concatenate([dq, dk, dv, dfl], axis=1)
            pp.grad("fox_w_in", run(_mm, L["h1"][None], dproj[None], mode="tn", out_dtype=BF16, name="fox_proj_dw", tn=896))
            dh1 = run(_mm, dproj[None], pp.w("fox_w_in"), mode="nt", out_dtype=F32, name="fox_proj_dx", tk=6272, tm=512)[0]
        else:
            do = run(_mm, dy[None], pp.w("swa_w_o"), mode="nt", out_dtype=BF16, name="swa_out_dx")[0]
            pp.grad("swa_w_o", run(_mm, L["o"][None], dy[None], mode="tn", out_dtype=BF16, name="swa_out_dw"))
            dqh, dkh, dvh, dsk = run(_swa_bwd, L["qh"], L["kh"], L["vh"], sp["sinks"], _to_groups(do, KH), name="swa_attn_bwd")
            gs["sinks"] = jnp.sum(dsk.reshape(QH, WIN), axis=1)
            dpr = jnp.concatenate([_from_groups(dqh), dkh.transpose(1, 0, 2).reshape(S, KH * SDH),
                                   dvh.transpose(1, 0, 2).reshape(S, KH * SDH)], axis=1)
            n_rot = (QH + KH) * SDH // LANES
            dproj = _rope(dpr, _rope_tables(pos, -1.0), n_rot, BF16, "swa_rope_bwd")
            dh1 = run(_mm, dproj[None], pp.w("swa_w_in"), mode="nt", out_dtype=F32, name="swa_proj_dx", tk=640)[0]
            pp.grad("swa_w_in", run(_mm, L["h1"][None], dproj[None], mode="tn", out_dtype=BF16, name="swa_proj_dw", out_groups=N_CHIPS, tn=640))
        pend = (dh1, sc1)
    grad_x, dmp[0]["sc1"], dmp[0]["sh1"] = _mod_bwd(pend[0], sv[0]["x_in"], pend[1], dres, "mod_mix_bwd0")
    dmod = [jnp.concatenate([p["sh1"], p["sc1"], p["g1"], p["sh2"], p["sc2"], p["g2"]], axis=1) for p in dmp]
    return loss_cols, grad_x, gs, jnp.concatenate(dmod, axis=0)


def _allgather_small(v, name):
    m_per, n = v.shape

    def body(x_ref, out_ref, send_sems, recv_sems, local_sem):
        x, y, c, chips = _place()
        me, sibling = (x, y, c), (x, y, 1 - c)

        def rows(px, py, pc):
            return out_ref.at[pl.ds((4 * px + 2 * py + pc) * m_per, m_per), :]

        def copy(k, block, to, src=None):
            return _remote(rows(*block) if src is None else src, rows(*block), send_sems.at[k], recv_sems.at[k], to)

        mine = pltpu.make_async_copy(x_ref, rows(*me), local_sem)
        mine.start()
        first = [copy(0, me, sibling, src=x_ref)]
        first += [copy(1 + j, me, (*chip, c), src=x_ref) for j, chip in enumerate(chips)]
        for cp in first:
            cp.start()
        passed = [copy(4 + j, (*chip, c), sibling) for j, chip in enumerate(chips)]
        for j, chip in enumerate(chips):
            copy(1 + j, (*chip, c), me).wait_recv()
            passed[j].start()
        copy(0, sibling, me).wait_recv()
        for j, chip in enumerate(chips):
            copy(4 + j, (*chip, 1 - c), me).wait_recv()
        for cp in first + passed:
            cp.wait_send()
        mine.wait()

    return pl.pallas_call(
        body, name=name, out_shape=jax.ShapeDtypeStruct((N_DEV * m_per, n), v.dtype),
        in_specs=[pl.BlockSpec(memory_space=pltpu.VMEM)], out_specs=pl.BlockSpec(memory_space=pltpu.VMEM),
        scratch_shapes=[pltpu.SemaphoreType.DMA((7,)), pltpu.SemaphoreType.DMA((7,)), pltpu.SemaphoreType.DMA],
        compiler_params=pltpu.CompilerParams(vmem_limit_bytes=VMEM_LIMIT),
    )(v)


def _row_tile(r, pref=256):
    return _tile(r, pref, 16)


def _cast_bf16(w, layer, chip, name, after=()):
    _, R, C = w.shape
    tr = _row_tile(R)

    def body(s_ref, w_ref, *rest):
        rest[-1][...] = w_ref[...].astype(BF16)

    return pl.pallas_call(
        body, name=name,
        grid_spec=pltpu.PrefetchScalarGridSpec(
            num_scalar_prefetch=1, grid=(R // tr,),
            in_specs=[pl.BlockSpec((None, tr, C), lambda i, s: (layer, i, 0))] + _any_specs(len(after)),
            out_specs=pl.BlockSpec((None, tr, C), lambda i, s: (s[0], i, 0))),
        out_shape=jax.ShapeDtypeStruct((N_CHIPS, R, C), BF16), compiler_params=_params(("parallel",)),
    )(jnp.reshape(chip, (1,)).astype(jnp.int32), w, *after)


def _add_sibling(g, got, c, name):
    G, R, C = g.shape
    rh = R // 2
    tr = _row_tile(rh)
    nb = rh // tr

    def body(c_ref, g_ref, o_ref, p_ref):
        p_ref[...] = (g_ref[...].astype(F32) + o_ref[...].astype(F32)).astype(BF16)

    return pl.pallas_call(
        body, name=name,
        grid_spec=pltpu.PrefetchScalarGridSpec(
            num_scalar_prefetch=1, grid=(G, nb),
            in_specs=[pl.BlockSpec((1, tr, C), lambda s, i, c_ref: (s, c_ref[0] * nb + i, 0)),
                      pl.BlockSpec((1, tr, C), lambda s, i, c_ref: (s, i, 0))],
            out_specs=pl.BlockSpec((1, tr, C), lambda s, i, c_ref: (s, i, 0))),
        out_shape=jax.ShapeDtypeStruct((G, rh, C), BF16), compiler_params=_params(("parallel", "parallel")),
    )(jnp.reshape(c, (1,)).astype(jnp.int32), g, got)


def _sum_chips(part, landed, chip, c, name):
    G, rh, C = part.shape
    tr = _row_tile(rh)
    nb = rh // tr

    def body(p_ref, own_ref, *rest):
        acc = own_ref[...].astype(F32)
        for ref in rest[:G - 1]:
            acc = acc + ref[...].astype(F32)
        rest[G - 1][...] = acc

    slot = lambda k: pl.BlockSpec((None, tr, C), lambda i, p: ((p[0] + k) % G, i, 0))
    return pl.pallas_call(
        body, name=name,
        grid_spec=pltpu.PrefetchScalarGridSpec(
            num_scalar_prefetch=1, grid=(nb,), in_specs=[slot(k) for k in range(G)],
            out_specs=pl.BlockSpec((tr, C), lambda i, p: (p[1] * nb + i, 0))),
        out_shape=jax.ShapeDtypeStruct((2 * rh, C), F32), compiler_params=_params(("parallel",)),
    )(jnp.stack([chip, c]).astype(jnp.int32), part, *([landed] * (G - 1)))


def _adam_math(w, g, m, v):
    m = ADAM_B1 * m + (1.0 - ADAM_B1) * g
    v = ADAM_B2 * v + (1.0 - ADAM_B2) * (g * g)
    m_hat = m / (1.0 - ADAM_B1 ** ADAM_STEP)
    v_hat = v / (1.0 - ADAM_B2 ** ADAM_STEP)
    delta = -ADAM_LR * (m_hat / (jnp.sqrt(v_hat) + ADAM_EPS) + ADAM_WD * w)
    return delta, m, v


def _adamw(w, g, m, v, layer, prev, name, by_cols=False, after=()):
    L, R, C = w.shape
    tr = R if by_cols else _tile(R, 128, 8)
    tc = _tile(C, 256) if by_cols else C
    n_alias = len(prev)
    prev = tuple(prev) + tuple(after)
    n_prev = len(prev)

    def body(w_ref, g_ref, m_ref, v_ref, *rest):
        go_ref, d_ref, mo_ref, vo_ref = rest[n_prev:]
        gv = g_ref[...]
        go_ref[...] = gv
        d_ref[...], mo_ref[...], vo_ref[...] = _adam_math(w_ref[...], gv, m_ref[...], v_ref[...])

    lay = pl.BlockSpec((None, tr, tc), lambda i: (layer, i // (C // tc), i % (C // tc)))
    flat = pl.BlockSpec((tr, tc), lambda i: (i // (C // tc), i % (C // tc)))
    return _call(
        body, name=name, grid=((R // tr) * (C // tc),), in_specs=[lay, flat, lay, lay] + _any_specs(n_prev), out_specs=[lay] * 4,
        out_shape=[jax.ShapeDtypeStruct((L, R, C), F32)] * 4, aliases={4 + k: k for k in range(n_alias)},
        sem=("parallel",), args=(w, g, m, v, *prev))


def _cond_rows(c_row, cw, name):
    D = c_row.shape[1]
    nr, fc = cw.shape

    def body(c_ref, e_ref, o_ref):
        o_ref[...] = jnp.zeros_like(o_ref)
        cv = c_ref[...]
        o_ref[0:1, 0:D] = cv * (1.0 / (1.0 + jnp.exp(-cv)))
        o_ref[8:8 + nr, 0:fc] = e_ref[...]

    return pl.pallas_call(body, name=name, out_shape=jax.ShapeDtypeStruct((16, max(D, fc)), F32))(c_row, cw)


def _ada_fwd(cact, ada_w, ada_b, layer, chip, name):
    _, D, NC = ada_w.shape
    tn = _tile(NC, 1024)
    nj = NC // tn

    def body(idx_ref, c_ref, w_ref, b_ref, o_ref):
        acc = jnp.dot(c_ref[...].astype(BF16), w_ref[0].astype(BF16), preferred_element_type=F32)
        o_ref[...] = acc + b_ref[pl.ds(idx_ref[0], 1), :]

    return pl.pallas_call(
        body, name=name,
        grid_spec=pltpu.PrefetchScalarGridSpec(
            num_scalar_prefetch=1, grid=(nj,),
            in_specs=[pl.BlockSpec((8, D), lambda j, idx: (0, 0)),
                      pl.BlockSpec((1, D, tn), lambda j, idx: (idx[0], 0, j)),
                      pl.BlockSpec((DEPTH, tn), lambda j, idx: (0, idx[1] * nj + j))],
            out_specs=pl.BlockSpec((8, tn), lambda j, idx: (0, j))),
        out_shape=jax.ShapeDtypeStruct((8, NC), F32), compiler_params=_params(("parallel",)),
    )(jnp.stack([layer, chip]).astype(jnp.int32), cact, ada_w, ada_b)


def _ada_grad_adamw(cact_t, dmod, w, m, v, name, comm=None):
    L, D, NC = w.shape
    tr = _tile(D, 128, 8)

    def body(c_ref, d_ref, w_ref, m_ref, v_ref, g_ref, dl_ref, mo_ref, vo_ref):
        g = jnp.dot(c_ref[...], d_ref[...], preferred_element_type=F32, precision=HIGHEST)
        g_ref[...] = g
        dl_ref[...], mo_ref[...], vo_ref[...] = _adam_math(w_ref[...], g, m_ref[...], v_ref[...])

    lay = pl.BlockSpec((None, tr, NC), lambda l, i: (l, i, 0))
    return _call(
        body, name=name, grid=(L, D // tr),
        in_specs=[pl.BlockSpec((tr, N_DEV), lambda l, i: (i, 0)), pl.BlockSpec((None, N_DEV, NC), lambda l, i: (l, 0, 0)), lay, lay, lay],
        out_specs=[lay] * 4, out_shape=[jax.ShapeDtypeStruct((L, D, NC), F32)] * 4,
        sem=("parallel", "parallel"), args=(cact_t, dmod, w, m, v), comm=comm)


def _sum_devices(gathered, name):
    n, R, C = gathered.shape

    def body(g_ref, o_ref):
        acc = g_ref[0]
        for j in range(1, n):
            acc = acc + g_ref[j]
        o_ref[...] = acc

    return pl.pallas_call(body, name=name, out_shape=jax.ShapeDtypeStruct((R, C), F32),
                          compiler_params=pltpu.CompilerParams(vmem_limit_bytes=VMEM_LIMIT))(gathered)


def _adamw_small(w, g, m, v, name):
    def body(w_ref, g_ref, m_ref, v_ref, d_ref, mo_ref, vo_ref):
        d_ref[...], mo_ref[...], vo_ref[...] = _adam_math(w_ref[...], g_ref[...], m_ref[...], v_ref[...])

    return pl.pallas_call(body, name=name, out_shape=[jax.ShapeDtypeStruct(w.shape, F32)] * 3)(w, g, m, v)


def _pad_rows(flat, unit=8 * LANES):
    n = flat.shape[0]
    total = -(-n // unit) * unit
    return jnp.pad(flat, (0, total - n)).reshape(total // LANES, LANES)


def _pad_lanes(v2d):
    return jnp.pad(v2d.reshape(1, -1), ((0, 0), (0, LANES - v2d.size)))


PLAN = {
    "fox_proj": [("gather", "ffn_w_up0", 0, 2, 8)],
    "fox_attn": [("gather", "ffn_w_up0", 2, 6, 8)],
    "fox_out": [("gather", "ffn_w_up0", 6, 7, 8)],
    "ln_mix0": [("gather", "ffn_w_up0", 7, 8, 8)],
    "ffn_up0": [("gather", "ffn_w_down0", 0, 1, 1)],
    "ffn_gate0": [("gather", "swa_w_in", 0, 1, 1)],
    "ffn_down0": [("gather", "swa_w_o", 0, 1, 1), ("gather", "ffn_w_up1", 0, 1, 8)],
    "ln_ffn0": [("gather", "ffn_w_up1", 1, 2, 8)],
    "swa_proj": [("gather", "ffn_w_up1", 2, 3, 8)],
    "swa_rope": [("gather", "ffn_w_up1", 3, 4, 8)],
    "swa_attn": [("gather", "ffn_w_up1", 4, 8, 8)],
    "ffn_up1": [("gather", "ffn_w_down1", 0, 1, 1)],
    "ffn_gate_bwd1": [("swap", "ffn_w_down1")],
    "ffn_up_dx1": [("scatter", "ffn_w_down1", 0, 1, 1)],
    "swa_out_dx": [("swap", "ffn_w_up1")],
    "swa_attn_bwd": [("scatter", "ffn_w_up1", 0, 6, 8), ("swap", "swa_w_o")],
    "swa_proj_dx": [("scatter", "swa_w_o", 0, 1, 1)],
    "ffn_down_dx0": [("scatter", "ffn_w_up1", 6, 8, 8), ("swap", "swa_w_in")],
    "ffn_down_dw0": [("scatter", "swa_w_in", 0, 1, 1)],
    "ffn_gate_bwd0": [("swap", "ffn_w_down0")],
    "ffn_up_dx0": [("scatter", "ffn_w_down0", 0, 1, 1)],
    "fox_out_dx": [("swap", "ffn_w_up0")],
    "fox_attn_bwd": [("scatter", "ffn_w_up0", 0, 5, 8), ("swap", "fox_w_o")],
    "fox_proj_dw": [("scatter", "fox_w_o", 0, 1, 1), ("scatter", "ffn_w_up0", 5, 6, 8)],
    "fox_proj_dx": [("scatter", "ffn_w_up0", 6, 8, 8), ("swap", "fox_w_in")],
}


class Exchanges:
    def __init__(self, dm, slots, chip, c):
        self.dm, self.slots, self.chip, self.c = dm, dict(slots), chip, c
        self.raw, self.part, self.landed, self.grads, self.views, self.pending = {}, {}, {}, {}, {}, {}

    def gather_start(self, keys, name, after):
        self.first = (keys, _gather_comm([self.slots[k] for k in keys], [(0, 1, 1)] * len(keys), forward=False))
        self.first_state, token = _split_start(self.first[1], name + "_start", after)
        return token

    def gather_finish(self, after, name):
        keys, comm = self.first
        _split_wait(comm, self.first_state, after, name + "_wait")
        pass_on = _forward_comm(comm.results, [(0, 1, 1)] * len(keys))
        _run_comm(pass_on, name + "_pass")
        self.slots.update(zip(keys, pass_on.results))

    def w(self, key):
        if key not in self.views:
            S, D, FH, QH, KH, Fh = self.dm
            full = self.slots[key]
            if key == "fox_w_in":
                cols = full.shape[2]
                full = jnp.pad(full.transpose(1, 0, 2).reshape(D, N_CHIPS * cols), ((0, 0), (0, 3 * D + LANES - N_CHIPS * cols)))[None]
            elif key in ("fox_w_o", "swa_w_o"):
                full = full.reshape(1, D, D)
            elif key.startswith("ffn_w_down"):
                full = full.reshape(1, Fh, D)
            self.views[key] = full
        return self.views[key]

    def carry(self, stage):
        todo = []
        for kind, key, *chunk in PLAN.get(stage, ()):
            if kind == "gather":
                todo.append((kind, [key], _gather_comm([self.slots[key]], [tuple(chunk)])))
            elif kind == "swap":
                todo.append((kind, [key], _swap_comm([self.raw[key]])))
            elif kind == "scatter":
                todo.append((kind, [(key, *chunk)], _scatter_comm([self.part[key]], [self.landed.get(key)], [tuple(chunk)])))
        self.pending[stage] = todo
        return _merge([cm for _, _, cm in todo])

    def carried(self, stage, comm):
        for kind, keys, cm in self.pending.pop(stage):
            if kind == "gather":
                self.slots[keys[0]] = cm.results[0]
            elif kind == "swap":
                self.part[keys[0]] = _add_sibling(self.raw[keys[0]], cm.results[0], self.c, f"add_sibling_{keys[0]}")
            else:
                self.landed[keys[0][0]] = cm.results[0]

    def grad(self, key, g):
        S, D, FH, QH, KH, Fh = self.dm
        if key == "fox_w_in":
            cols = self.slots[key].shape[2]
            g = g[0][:, :N_CHIPS * cols].reshape(D, N_CHIPS, cols).transpose(1, 0, 2)
        elif key in ("fox_w_o", "swa_w_o"):
            g = g.reshape(N_CHIPS, D // N_CHIPS, D)
        elif key.startswith("ffn_w_down"):
            g = g.reshape(N_CHIPS, Fh // N_CHIPS, D)
        self.raw[key] = g

    def last_start(self, last, after):
        part = self.part[last]
        self.last = (last, _scatter_comm([part], [lax.empty(part.shape, part.dtype)], [(0, 1, 1)]))
        self.last_state, token = _split_start(self.last[1], "grads_last_start", after)
        return token

    def join_landed(self):
        keys = list(self.landed)
        join = _join_comm([_sum_chips(self.part[k], self.landed[k], self.chip, self.c, f"sum_chips_{k}") for k in keys])
        _run_comm(join, "grads_join")
        return dict(zip(keys, join.results))

    def last_finish(self, after):
        last, comm = self.last
        _split_wait(comm, self.last_state, after, "grads_last_wait")
        join = _join_comm([_sum_chips(self.part[last], comm.results[0], self.chip, self.c, f"sum_chips_{last}")])
        _run_comm(join, "grads_join_last")
        return join.results[0]


def _step(dm, a):
    S, D, FH, QH, KH, Fh = dm
    ix, iy, ic = lax.axis_index("x"), lax.axis_index("y"), lax.axis_index("c")
    chip = 2 * ix + iy
    dev = 2 * chip + ic
    F2c = a["ffn_w_up"].shape[2]
    NC = a["ada_w"].shape[2]

    names = ["fox_w_in", "fox_w_o", "swa_w_in", "swa_w_o", "ffn_w_up", "ffn_w_up", "ffn_w_down", "ffn_w_down"]
    layers = [0, 0, 0, 0, 0, 1, 0, 1]
    keys = ["fox_w_in", "fox_w_o", "swa_w_in", "swa_w_o", "ffn_w_up0", "ffn_w_up1", "ffn_w_down0", "ffn_w_down1"]
    cast = lambda t, after: _cast_bf16(a[names[t]], layers[t], chip, f"cast_{keys[t]}", after)
    pp = Exchanges(dm, {keys[t]: cast(t, ()) for t in (0, 1)}, chip, ic)

    e0 = _cond_rows(a["c"], a["ffn_conv_w"].reshape(DEPTH * 3, F2c), "silu_c")
    g0 = _allgather_small(e0, "gather_cond").reshape(N_DEV, 16, e0.shape[1])
    cact = g0[:, 0, :D]
    conv_w = g0[0::2, 8:8 + DEPTH * 3, :F2c].transpose(1, 0, 2).reshape(DEPTH, 3, N_CHIPS * F2c)
    rows = _ada_fwd(cact, a["ada_w"], a["ada_b"], ic, chip, "ada_proj")
    g1 = _allgather_small(rows, "gather_mod").reshape(N_CHIPS, DEPTH, 8, NC)
    mod = lax.dynamic_index_in_dim(g1, dev, axis=2, keepdims=False).transpose(1, 0, 2).reshape(DEPTH, N_CHIPS * NC)

    token = pp.gather_start(keys[:2], "gather_fox", [mod])
    pp.slots.update({keys[t]: cast(t, (token,)) for t in range(2, len(keys))})
    pp.gather_finish([pp.slots[k] for k in keys[2:]], "gather_fox")
    sp = {"fox_b_f": _pad_lanes(a["fox_b_f"]), "sinks": jnp.repeat(a["swa_sinks"].reshape(KH, QH // KH), WIN, axis=1)[:, :, None],
          "conv_w": [conv_w[i] for i in range(DEPTH)], "conv_b": [a["ffn_conv_b"][i:i + 1] for i in range(DEPTH)]}
    for nm in ("ln_mix_g", "ln_mix_b", "ln_ffn_g", "ln_ffn_b"):
        sp[nm] = [a[nm][i:i + 1] for i in range(DEPTH)]

    loss_cols, grad_x, gs, dmod = _local_step(dm, a["x"][0], a["loss_target"][0], a["positions"][0], mod, sp, pp)
    loss = lax.psum(0.5 / D * jnp.sum(loss_cols), ("x", "y", "c"))
    out = {"loss": loss, "grad_x": grad_x[None]}

    pieces = [dmod.reshape(-1), gs["fox_b_f"].reshape(-1), _pad_lanes(gs["sinks"]).reshape(-1),
              jnp.stack(gs["conv_w"]).reshape(-1), jnp.stack(gs["conv_b"]).reshape(-1)]
    pieces += [jnp.stack(gs[nm]).reshape(-1) for nm in ("ln_mix_g", "ln_mix_b", "ln_ffn_g", "ln_ffn_b")]
    sizes = [p.shape[0] for p in pieces]
    packed = _pad_rows(jnp.concatenate(pieces))
    allp = _allgather_small(packed, "gather_small").reshape(N_DEV, packed.shape[0], LANES)
    tot = _sum_devices(allp, "sum_small").reshape(-1)
    offs = [sum(sizes[:k]) for k in range(len(sizes))]
    take = lambda k: tot[offs[k]:offs[k] + sizes[k]]
    g_small = {"ada_b": take(0).reshape(DEPTH, -1), "fox_b_f": take(1)[:FH].reshape(1, FH), "swa_sinks": take(2)[:QH].reshape(1, QH),
               "ffn_conv_w": lax.dynamic_slice_in_dim(take(3).reshape(DEPTH, 3, N_CHIPS * F2c), chip * F2c, F2c, axis=2),
               "ffn_conv_b": take(4).reshape(DEPTH, -1)}
    for k, nm in enumerate(("ln_mix_g", "ln_mix_b", "ln_ffn_g", "ln_ffn_b")):
        g_small[nm] = take(5 + k).reshape(DEPTH, D)
    small = list(g_small)
    pack = lambda pre: _pad_rows(jnp.concatenate([(a[pre + nm] if pre else a[nm]).reshape(-1) for nm in small]))
    gp = _pad_rows(jnp.concatenate([g_small[nm].reshape(-1) for nm in small]))
    ds_, ms_, vs_ = _adamw_small(pack(""), gp, pack("m_"), pack("v_"), "adamw_small")
    off = 0
    for nm in small:
        n_el = a[nm].size
        out["grad_" + nm] = g_small[nm]
        for pre, arr in (("delta_", ds_), ("new_m_", ms_), ("new_v_", vs_)):
            out[pre + nm] = arr.reshape(-1)[off:off + n_el].reshape(a[nm].shape)
        off += n_el

    dmod_all = allp.reshape(N_DEV, -1)[:, :DEPTH * N_CHIPS * NC].reshape(N_DEV, DEPTH, N_CHIPS * NC)
    dmod_mine = lax.dynamic_slice_in_dim(dmod_all, chip * NC, NC, axis=2).transpose(1, 0, 2)

    grads = pp.join_landed()
    token = pp.last_start("fox_w_in", [ds_, dmod_mine] + list(grads.values()))
    ada = _ada_grad_adamw(cact.T, dmod_mine + token[0, 0], a["ada_w"], a["m_ada_w"], a["v_ada_w"], "ada_grad")
    for pre, arr in zip(("grad_", "delta_", "new_m_", "new_v_"), ada):
        out[pre + "ada_w"] = arr
    upd = {}
    for k, nm, l in zip(keys[1:], names[1:], layers[1:]):
        upd[nm] = _adamw(a[nm], grads[k], a["m_" + nm], a["v_" + nm], l, upd.get(nm, ()), f"adamw_{k}", after=(token,))
    g_last = pp.last_finish([ada[1]] + [res[1] for res in upd.values()])
    tview = lambda t: jnp.swapaxes(t, 1, 2)
    res = _adamw(tview(a["fox_w_in"]), g_last.T, tview(a["m_fox_w_in"]), tview(a["v_fox_w_in"]), 0, (), "adamw_fox_w_in", by_cols=True)
    upd["fox_w_in"] = [tview(r) for r in res]
    for nm, res in upd.items():
        for pre, arr in zip(("grad_", "delta_", "new_m_", "new_v_"), res):
            out[pre + nm] = arr
    return out


_WEIGHTS = ["fox_w_in", "fox_b_f", "fox_w_o", "swa_w_in", "swa_sinks", "swa_w_o", "ada_w", "ada_b", "ffn_w_up", "ffn_conv_w",
            "ffn_conv_b", "ffn_w_down", "ln_mix_g", "ln_mix_b", "ln_ffn_g", "ln_ffn_b"]
_INPUTS = (["x", "c", "positions"] + _WEIGHTS + ["loss_target"] + ["m_" + w for w in _WEIGHTS] + ["v_" + w for w in _WEIGHTS])


def kernel(x, c, positions, fox_w_in, fox_b_f, fox_w_o, swa_w_in, swa_sinks, swa_w_o, ada_w, ada_b, ffn_w_up, ffn_conv_w, ffn_conv_b, ffn_w_down, ln_mix_g, ln_mix_b, ln_ffn_g, ln_ffn_b, loss_target, m_fox_w_in, m_fox_b_f, m_fox_w_o, m_swa_w_in, m_swa_sinks, m_swa_w_o, m_ada_w, m_ada_b, m_ffn_w_up, m_ffn_conv_w, m_ffn_conv_b, m_ffn_w_down, m_ln_mix_g, m_ln_mix_b, m_ln_ffn_g, m_ln_ffn_b, v_fox_w_in, v_fox_b_f, v_fox_w_o, v_swa_w_in, v_swa_sinks, v_swa_w_o, v_ada_w, v_ada_b, v_ffn_w_up, v_ffn_conv_w, v_ffn_conv_b, v_ffn_w_down, v_ln_mix_g, v_ln_mix_b, v_ln_ffn_g, v_ln_ffn_b):
    args = (x, c, positions, fox_w_in, fox_b_f, fox_w_o, swa_w_in, swa_sinks, swa_w_o, ada_w, ada_b, ffn_w_up, ffn_conv_w, ffn_conv_b, ffn_w_down, ln_mix_g, ln_mix_b, ln_ffn_g, ln_ffn_b, loss_target, m_fox_w_in, m_fox_b_f, m_fox_w_o, m_swa_w_in, m_swa_sinks, m_swa_w_o, m_ada_w, m_ada_b, m_ffn_w_up, m_ffn_conv_w, m_ffn_conv_b, m_ffn_w_down, m_ln_mix_g, m_ln_mix_b, m_ln_ffn_g, m_ln_ffn_b, v_fox_w_in, v_fox_b_f, v_fox_w_o, v_swa_w_in, v_swa_sinks, v_swa_w_o, v_ada_w, v_ada_b, v_ffn_w_up, v_ffn_conv_w, v_ffn_conv_b, v_ffn_w_down, v_ln_mix_g, v_ln_mix_b, v_ln_ffn_g, v_ln_ffn_b)
    out = _step(PROD, dict(zip(_INPUTS, args)))
    order = ["loss", "grad_x"] + [p + w for p in ("grad_", "delta_", "new_m_", "new_v_") for w in _WEIGHTS]
    return tuple(out[k] for k in order)
```

```python
import functools
from typing import NamedTuple

import jax
import jax.numpy as jnp
from jax import lax
from jax.experimental import pallas as pl
from jax.experimental.pallas import tpu as pltpu

F32 = jnp.float32
BF16 = jnp.bfloat16
MESH = pl.DeviceIdType.MESH
HIGHEST = lax.Precision.HIGHEST

N_CHIPS = 4
N_DEV = 8
LANES = 128
VMEM_LIMIT = 56 * 1024 * 1024

DEPTH = 2
DEEPNORM_ALPHA = (2.0 * DEPTH) ** 0.25
LN_EPS = 1e-5
ROPE_THETA = 500000.0
ADAM_LR, ADAM_B1, ADAM_B2, ADAM_EPS, ADAM_WD, ADAM_STEP = 0.001, 0.9, 0.999, 1e-08, 0.01, 10
NEG = -1e30


class Dims(NamedTuple):
    S: int
    D: int
    FH: int
    QH: int
    KH: int
    F: int


PROD = Dims(S=2048, D=2048, FH=16, QH=32, KH=4, F=5632)
FDH = 128
SDH = 64
WIN = 128
ROPE_DIM = 16
FOX_TQ = 256


def _params(sem=None, vmem=VMEM_LIMIT):
    return pltpu.CompilerParams(dimension_semantics=sem, vmem_limit_bytes=vmem)


def _tile(n, pref, unit=LANES):
    if n <= pref:
        return n
    t = (pref // unit) * unit
    while t > 0:
        if n % t == 0:
            return t
        t -= unit
    return n


class Comm:
    def __init__(self, args, out_shapes, aliases, n_sem, start, finish, members=()):
        self.args, self.out_shapes, self.aliases, self.n_sem = list(args), list(out_shapes), dict(aliases), n_sem
        self.start, self.finish = start, finish
        self.members = members
        self.results = None

    def set_results(self, res):
        self.results = list(res)
        for cm, o0 in self.members:
            cm.set_results(self.results[o0:o0 + len(cm.out_shapes)])


class _SemView:
    def __init__(self, sems, first):
        self.sems, self.first = sems, first

    @property
    def at(self):
        return self

    def __getitem__(self, k):
        return self.sems.at[self.first + k]


def _merge(comms):
    comms = [cm for cm in comms if cm is not None]
    if len(comms) < 2:
        return comms[0] if comms else None
    args, shapes, aliases, spans, n_sem = [], [], {}, [], 0
    for cm in comms:
        spans.append((len(args), len(shapes), n_sem))
        aliases.update({len(args) + a: len(shapes) + o for a, o in cm.aliases.items()})
        args += cm.args
        shapes += cm.out_shapes
        n_sem += cm.n_sem

    def each(step):
        def run(ar, ou, send, recv):
            for cm, (a0, o0, s0) in zip(comms, spans):
                getattr(cm, step)(ar[a0:a0 + len(cm.args)], ou[o0:o0 + len(cm.out_shapes)], _SemView(send, s0), _SemView(recv, s0))
        return run

    return Comm(args, shapes, aliases, n_sem, each("start"), each("finish"), [(cm, o0) for cm, (_, o0, _) in zip(comms, spans)])


def _place():
    x, y, c = lax.axis_index("x"), lax.axis_index("y"), lax.axis_index("c")
    chips = [(1 - x, y), (x, 1 - y), (1 - x, 1 - y)]
    return x, y, c, chips


def _remote(src, dst, send, recv, to):
    return pltpu.make_async_remote_copy(src_ref=src, dst_ref=dst, send_sem=send, recv_sem=recv, device_id=to, device_id_type=MESH)


def _any_specs(n):
    return [pl.BlockSpec(memory_space=pl.ANY)] * n


def _call(body, *, name, grid, in_specs, out_specs, out_shape, args, sem, scratch_shapes=(), aliases=None, comm=None):
    in_specs, out_specs, out_shape, scratch_shapes = list(in_specs), list(out_specs), list(out_shape), list(scratch_shapes)
    aliases = dict(aliases or {})
    if comm is None:
        return pl.pallas_call(body, name=name, grid=grid, in_specs=in_specs, out_specs=out_specs, out_shape=out_shape,
                              scratch_shapes=scratch_shapes, input_output_aliases=aliases, compiler_params=_params(sem))(*args)
    n_in, n_out, nc_in, nc_out, n_scr = len(in_specs), len(out_specs), len(comm.args), len(comm.out_shapes), len(scratch_shapes)

    def wrapped(*refs):
        ins, refs = refs[:n_in], refs[n_in:]
        cin, refs = refs[:nc_in], refs[nc_in:]
        outs, refs = refs[:n_out], refs[n_out:]
        cout, refs = refs[:nc_out], refs[nc_out:]
        scratch, (send, recv) = refs[:n_scr], refs[n_scr:]
        ids = [pl.program_id(k) for k in range(len(grid))]
        first = functools.reduce(jnp.logical_and, [i == 0 for i in ids])
        last = functools.reduce(jnp.logical_and, [i == g - 1 for i, g in zip(ids, grid)])

        @pl.when(first)
        def _():
            comm.start(cin, cout, send, recv)

        body(*ins, *outs, *scratch)

        @pl.when(last)
        def _():
            comm.finish(cin, cout, send, recv)

    res = pl.pallas_call(
        wrapped, name=name, grid=grid, in_specs=in_specs + _any_specs(nc_in), out_specs=out_specs + _any_specs(nc_out),
        out_shape=out_shape + comm.out_shapes,
        scratch_shapes=scratch_shapes + [pltpu.SemaphoreType.DMA((comm.n_sem,)), pltpu.SemaphoreType.DMA((comm.n_sem,))],
        input_output_aliases={**aliases, **{n_in + a: n_out + o for a, o in comm.aliases.items()}},
        compiler_params=_params(("arbitrary",) * len(grid)),
    )(*args, *comm.args)
    comm.set_results(res[n_out:])
    return list(res[:n_out])


def _run_comm(comm, name):
    nc_in, nc_out = len(comm.args), len(comm.out_shapes)

    def body(*refs):
        cin, cout, (send, recv) = refs[:nc_in], refs[nc_in:nc_in + nc_out], refs[nc_in + nc_out:]
        comm.start(cin, cout, send, recv)
        comm.finish(cin, cout, send, recv)

    res = pl.pallas_call(
        body, name=name, in_specs=_any_specs(nc_in), out_specs=_any_specs(nc_out), out_shape=comm.out_shapes,
        scratch_shapes=[pltpu.SemaphoreType.DMA((comm.n_sem,)), pltpu.SemaphoreType.DMA((comm.n_sem,))],
        input_output_aliases=comm.aliases,
    )(*comm.args)
    comm.set_results(res)


_HBM = pl.BlockSpec(memory_space=pltpu.HBM)
_SEM = pl.BlockSpec(memory_space=pltpu.SEMAPHORE)
_EFFECT = pltpu.SideEffectType.DATAFLOW_SIDE_EFFECTING


def _split_start(comm, name, after=()):
    n = len(comm.args)
    back = {o: a for a, o in comm.aliases.items()}
    assert len(back) == len(comm.out_shapes)

    n_after = len(after)

    def body(*refs):
        refs = refs[n + n_after:]
        send, recv, thru, token = refs[0], refs[1], refs[2:n + 2], refs[n + 2]
        comm.start(thru, [thru[back[o]] for o in range(len(back))], send, recv)
        token[...] = jnp.zeros_like(token)

    res = pl.pallas_call(
        body, name=name,
        out_shape=(pltpu.SemaphoreType.DMA((comm.n_sem,)), pltpu.SemaphoreType.DMA((comm.n_sem,)),
                   *[pltpu.HBM(a.shape, a.dtype) for a in comm.args], jax.ShapeDtypeStruct((8, LANES), F32)),
        in_specs=[_HBM] * n + _any_specs(n_after), out_specs=(_SEM, _SEM, *[_HBM] * n, pl.BlockSpec(memory_space=pltpu.VMEM)),
        input_output_aliases={i: 2 + i for i in range(n)},
        compiler_params=pltpu.CompilerParams(has_side_effects=_EFFECT),
    )(*[pltpu.with_memory_space_constraint(a, pltpu.HBM) for a in comm.args], *after)
    return (res[0], res[1], list(res[2:2 + n])), res[2 + n]


def _split_wait(comm, state, after, name):
    send, recv, thru = state
    n, n_after = len(thru), len(after)
    back = {o: a for a, o in comm.aliases.items()}

    def body(*refs):
        ins, send_ref, recv_ref = refs[:n], refs[n], refs[n + 1]
        comm.finish(ins, [ins[back[o]] for o in range(len(back))], send_ref, recv_ref)

    res = pl.pallas_call(
        body, name=name, out_shape=tuple(pltpu.HBM(a.shape, a.dtype) for a in thru),
        in_specs=[_HBM] * n + [_SEM, _SEM] + _any_specs(n_after), out_specs=[_HBM] * n,
        input_output_aliases={i: i for i in range(n)}, compiler_params=pltpu.CompilerParams(has_side_effects=_EFFECT),
    )(*thru, send, recv, *after)
    comm.set_results([res[back[o]] for o in range(len(back))])


def _forward_comm(slots, chunks):
    n = len(slots)

    def rows(t, who):
        rh = slots[t].shape[1] // 2
        lo, hi, nch = chunks[t]
        rc = rh // nch
        return pl.ds(who * rh + lo * rc, (hi - lo) * rc)

    def copy(outs, send, recv, t, j, chip, who):
        x, y, c, _ = _place()
        blk = outs[t].at[2 * chip[0] + chip[1], rows(t, who)]
        return _remote(blk, blk, send.at[3 * t + j], recv.at[3 * t + j], (x, y, 1 - c))

    def start(args, outs, send, recv):
        _, _, c, chips = _place()
        for t in range(n):
            for j, chip in enumerate(chips):
                copy(outs, send, recv, t, j, chip, c).start()

    def finish(args, outs, send, recv):
        _, _, c, chips = _place()
        for t in range(n):
            for j, chip in enumerate(chips):
                copy(outs, send, recv, t, j, chip, 1 - c).wait_recv()
        for t in range(n):
            for j, chip in enumerate(chips):
                copy(outs, send, recv, t, j, chip, c).wait_send()

    shapes = [jax.ShapeDtypeStruct(w.shape, w.dtype) for w in slots]
    return Comm(slots, shapes, {t: t for t in range(n)}, 3 * n, start, finish)


def _gather_comm(slots, chunks, forward=True):
    n = len(slots)

    def rows(t, who):
        rh = slots[t].shape[1] // 2
        lo, hi, nch = chunks[t]
        rc = rh // nch
        return pl.ds(who * rh + lo * rc, (hi - lo) * rc)

    def start(args, outs, send, recv):
        x, y, c, chips = _place()
        s = 2 * x + y
        for t in range(n):
            mine = outs[t].at[s, rows(t, c)]
            for j, chip in enumerate(chips):
                _remote(mine, mine, send.at[6 * t + j], recv.at[6 * t + j], (*chip, c)).start()

    def finish(args, outs, send, recv):
        x, y, c, chips = _place()
        s = 2 * x + y
        sib = (x, y, 1 - c)
        for t in range(n):
            for j, chip in enumerate(chips):
                blk = outs[t].at[2 * chip[0] + chip[1], rows(t, c)]
                _remote(blk, blk, send.at[6 * t + j], recv.at[6 * t + j], (*chip, c)).wait_recv()
                if forward:
                    _remote(blk, blk, send.at[6 * t + 3 + j], recv.at[6 * t + 3 + j], sib).start()
        for t in range(n if forward else 0):
            for j, chip in enumerate(chips):
                blk = outs[t].at[2 * chip[0] + chip[1], rows(t, 1 - c)]
                _remote(blk, blk, send.at[6 * t + 3 + j], recv.at[6 * t + 3 + j], sib).wait_recv()
        for t in range(n):
            mine = outs[t].at[s, rows(t, c)]
            for j, chip in enumerate(chips):
                _remote(mine, mine, send.at[6 * t + j], recv.at[6 * t + j], (*chip, c)).wait_send()
                if forward:
                    blk = outs[t].at[2 * chip[0] + chip[1], rows(t, c)]
                    _remote(blk, blk, send.at[6 * t + 3 + j], recv.at[6 * t + 3 + j], sib).wait_send()

    shapes = [jax.ShapeDtypeStruct(w.shape, w.dtype) for w in slots]
    return Comm(slots, shapes, {t: t for t in range(n)}, 6 * n, start, finish)


def _scatter_comm(parts, landed, chunks):
    n = len(parts)
    prev = [t for t in range(n) if landed[t] is not None]

    def rows(t):
        lo, hi, nch = chunks[t]
        rc = parts[t].shape[1] // nch
        return pl.ds(lo * rc, (hi - lo) * rc)

    def start(args, outs, send, recv):
        x, y, c, chips = _place()
        s = 2 * x + y
        for t in range(n):
            for j, chip in enumerate(chips):
                _remote(args[t].at[2 * chip[0] + chip[1], rows(t)], outs[t].at[s, rows(t)],
                        send.at[3 * t + j], recv.at[3 * t + j], (*chip, c)).start()

    def finish(args, outs, send, recv):
        x, y, c, chips = _place()
        for t in range(n):
            for j, chip in enumerate(chips):
                blk = outs[t].at[2 * chip[0] + chip[1], rows(t)]
                _remote(blk, blk, send.at[3 * t + j], recv.at[3 * t + j], (*chip, c)).wait_recv()
        for t in range(n):
            for j, chip in enumerate(chips):
                src = args[t].at[2 * chip[0] + chip[1], rows(t)]
                _remote(src, src, send.at[3 * t + j], recv.at[3 * t + j], (*chip, c)).wait_send()

    shapes = [jax.ShapeDtypeStruct(p.shape, p.dtype) for p in parts]
    return Comm(list(parts) + [landed[t] for t in prev], shapes, {n + i: t for i, t in enumerate(prev)}, 3 * n, start, finish)


def _swap_comm(gs):
    n = len(gs)

    def copy(args, outs, send, recv, t):
        _, _, c, _ = _place()
        rh = gs[t].shape[1] // 2
        x, y = lax.axis_index("x"), lax.axis_index("y")
        return _remote(args[t].at[:, pl.ds((1 - c) * rh, rh), :], outs[t], send.at[t], recv.at[t], (x, y, 1 - c))

    def start(args, outs, send, recv):
        for t in range(n):
            copy(args, outs, send, recv, t).start()

    def finish(args, outs, send, recv):
        for t in range(n):
            copy(args, outs, send, recv, t).wait()

    shapes = [jax.ShapeDtypeStruct((g.shape[0], g.shape[1] // 2, g.shape[2]), g.dtype) for g in gs]
    return Comm(gs, shapes, {}, n, start, finish)


def _join_comm(gs):
    n = len(gs)

    def half(outs, t, who):
        rh = gs[t].shape[0] // 2
        return outs[t].at[pl.ds(who * rh, rh), :]

    def start(args, outs, send, recv):
        x, y, c, _ = _place()
        for t in range(n):
            _remote(half(outs, t, c), half(outs, t, c), send.at[t], recv.at[t], (x, y, 1 - c)).start()

    def finish(args, outs, send, recv):
        x, y, c, _ = _place()
        for t in range(n):
            _remote(half(outs, t, 1 - c), half(outs, t, 1 - c), send.at[t], recv.at[t], (x, y, 1 - c)).wait_recv()
        for t in range(n):
            _remote(half(outs, t, c), half(outs, t, c), send.at[t], recv.at[t], (x, y, 1 - c)).wait_send()

    shapes = [jax.ShapeDtypeStruct(g.shape, g.dtype) for g in gs]
    return Comm(gs, shapes, {t: t for t in range(n)}, n, start, finish)


_DN = {"nn": (((1,), (0,)), ((), ())), "nt": (((1,), (1,)), ((), ())), "tn": (((0,), (0,)), ((), ()))}


def _mm(a, b, *, mode, out_dtype, name, out_groups=1, tm=1024, tn=1024, tk=2048, comm=None):
    ga, ra, ca = a.shape
    gb, rb, cb = b.shape
    if mode == "nn":
        M, K, N = ra, ga * ca, gb * cb
        assert rb == K and ga == 1 or (rb == K)
    elif mode == "nt":
        M, K, N = ra, ga * ca, rb
        assert gb * cb == K
    else:
        K, M, N = ra, ga * ca, gb * cb
        assert rb == K
    go = out_groups
    if mode == "nn":
        tk = _tile(ca, tk); assert rb % tk == 0 and (ga == 1 or True)
        tn = _tile(min(cb, N // go), tn); tm = _tile(M, tm, 8)
    elif mode == "nt":
        tk = _tile(ca, tk); tk = _tile(cb, tk) if cb % tk else tk; assert ca % tk == 0 and cb % tk == 0
        tn = _tile(N // go, tn); tm = _tile(M, tm, 8)
    else:
        tk = _tile(K, tk, 8); tm = _tile(ca, tm); tn = _tile(min(cb, N // go), tn)
    assert (N // go) % tn == 0 and M % tm == 0 and K % tk == 0, (name, M, N, K, tm, tn, tk)
    nk = K // tk
    kpa = max(ca // tk, 1)
    kpb = max(cb // tk, 1)
    npb = max(cb // tn, 1)
    npo = (N // go) // tn
    mpa = max(ca // tm, 1)

    if mode == "nn":
        a_spec = pl.BlockSpec((1, tm, tk), lambda j, i, k: (k // kpa, i, k % kpa))
        b_spec = pl.BlockSpec((1, tk, tn), lambda j, i, k: (j // npb, k, j % npb))
    elif mode == "nt":
        a_spec = pl.BlockSpec((1, tm, tk), lambda j, i, k: (k // kpa, i, k % kpa))
        b_spec = pl.BlockSpec((1, tn, tk), lambda j, i, k: (k // kpb, j, k % kpb))
    else:
        a_spec = pl.BlockSpec((1, tk, tm), lambda j, i, k: (i // mpa, k, i % mpa))
        b_spec = pl.BlockSpec((1, tk, tn), lambda j, i, k: (j // npb, k, j % npb))
    o_spec = pl.BlockSpec((1, tm, tn), lambda j, i, k: (j // npo, i, j % npo))
    dn = _DN[mode]

    def body(a_ref, b_ref, o_ref, *acc):
        p = lax.dot_general(a_ref[0], b_ref[0], dn, preferred_element_type=F32)
        if nk == 1:
            o_ref[0] = p.astype(out_dtype)
        else:
            k = pl.program_id(2)

            @pl.when(k == 0)
            def _():
                acc[0][...] = p

            @pl.when(k > 0)
            def _():
                acc[0][...] += p

            @pl.when(k == nk - 1)
            def _():
                o_ref[0] = acc[0][...].astype(out_dtype)

    return _call(
        body, name=name, grid=(N // tn, M // tm, nk), in_specs=[a_spec, b_spec], out_specs=[o_spec],
        out_shape=[jax.ShapeDtypeStruct((go, M, N // go), out_dtype)],
        scratch_shapes=[pltpu.VMEM((tm, tn), F32)] if nk > 1 else [],
        sem=("parallel", "parallel", "arbitrary"), args=(a, b), comm=comm)[0]


def _rows(tr, d):
    return pl.BlockSpec((tr, d), lambda i: (i, 0))


def _vec(d):
    return pl.BlockSpec((1, d), lambda i: (0, 0))


def _modulate(x, sc, sh, name):
    S, D = x.shape
    tr = min(256, S)

    def body(x_ref, sc_ref, sh_ref, h_ref):
        h_ref[...] = (x_ref[...] * (1.0 + sc_ref[...]) + sh_ref[...]).astype(BF16)

    return pl.pallas_call(
        body, name=name, grid=(S // tr,), in_specs=[_rows(tr, D), _vec(D), _vec(D)], out_specs=_rows(tr, D),
        out_shape=jax.ShapeDtypeStruct((S, D), BF16), compiler_params=_params(("parallel",)),
    )(x, sc, sh)


def _ln_fwd(x, y, gate, gamma, beta, sc, sh, name, comm=None):
    S, D = x.shape
    tr = min(256, S)
    emit_h = sc is not None

    def body(*refs):
        if emit_h:
            x_ref, y_ref, g_ref, ga_ref, be_ref, sc_ref, sh_ref, xo_ref, xh_ref, rs_ref, h_ref = refs
        else:
            x_ref, y_ref, g_ref, ga_ref, be_ref, xo_ref, xh_ref, rs_ref = refs
        z = DEEPNORM_ALPHA * x_ref[...] + (1.0 + g_ref[...]) * y_ref[...]
        mu = jnp.mean(z, axis=-1, keepdims=True)
        zc = z - mu
        var = jnp.mean(zc * zc, axis=-1, keepdims=True)
        rstd = lax.rsqrt(var + LN_EPS)
        xh = zc * rstd
        xo = xh * ga_ref[...] + be_ref[...]
        xo_ref[...] = xo
        xh_ref[...] = xh
        rs_ref[...] = rstd
        if emit_h:
            h_ref[...] = (xo * (1.0 + sc_ref[...]) + sh_ref[...]).astype(BF16)

    ins = [x, y, gate, gamma, beta] + ([sc, sh] if emit_h else [])
    in_specs = [_rows(tr, D), _rows(tr, D)] + [_vec(D)] * (len(ins) - 2)
    out_shape = [jax.ShapeDtypeStruct((S, D), F32), jax.ShapeDtypeStruct((S, D), F32), jax.ShapeDtypeStruct((S, 1), F32)]
    out_specs = [_rows(tr, D), _rows(tr, D), _rows(tr, 1)]
    if emit_h:
        out_shape.append(jax.ShapeDtypeStruct((S, D), BF16))
        out_specs.append(_rows(tr, D))
    return _call(body, name=name, grid=(S // tr,), in_specs=in_specs, out_specs=out_specs, out_shape=out_shape,
                 sem=("parallel",), args=ins, comm=comm)


def _loss_head(xf, tgt, name):
    S, D = xf.shape
    tr = min(256, S)

    def body(x_ref, t_ref, dx_ref, l_ref):
        e = x_ref[...] - t_ref[...]
        dx_ref[...] = e * (1.0 / D)

        @pl.when(pl.program_id(0) == 0)
        def _():
            l_ref[...] = jnp.zeros_like(l_ref)

        l_ref[...] += jnp.sum(e * e, axis=0, keepdims=True)

    return pl.pallas_call(
        body, name=name, grid=(S // tr,), in_specs=[_rows(tr, D), _rows(tr, D)],
        out_specs=[_rows(tr, D), _vec(D)],
        out_shape=[jax.ShapeDtypeStruct((S, D), F32), jax.ShapeDtypeStruct((1, D), F32)],
        compiler_params=_params(("arbitrary",)),
    )(xf, tgt)


def _ln_bwd(dxo, xh, rstd, gamma, y, gate, name, pre=None):
    S, D = dxo.shape
    tr = min(256, S)
    n_pre = 0 if pre is None else 3

    def body(dx_ref, xh_ref, rs_ref, ga_ref, y_ref, g_ref, *rest):
        dres_ref, dy_ref, dga_ref, dbe_ref, dg_ref = rest[n_pre:n_pre + 5]
        first = pl.program_id(0) == 0
        dxo_ = dx_ref[...]
        xh_ = xh_ref[...]
        if pre is not None:
            dh_ref, sc_ref, be_ref = rest[:3]
            dsc_ref, dsh_ref = rest[n_pre + 5:]
            dh_ = dh_ref[...]
            dxo_ = dxo_ + dh_ * (1.0 + sc_ref[...])

            @pl.when(first)
            def _():
                dsc_ref[...] = jnp.zeros_like(dsc_ref)
                dsh_ref[...] = jnp.zeros_like(dsh_ref)

            dsc_ref[...] += jnp.sum(dh_ * (xh_ * ga_ref[...] + be_ref[...]), axis=0, keepdims=True)
            dsh_ref[...] += jnp.sum(dh_, axis=0, keepdims=True)
        dxh = dxo_ * ga_ref[...]
        m1 = jnp.mean(dxh, axis=-1, keepdims=True)
        m2 = jnp.mean(dxh * xh_, axis=-1, keepdims=True)
        dz = rs_ref[...] * (dxh - m1 - xh_ * m2)
        dres_ref[...] = DEEPNORM_ALPHA * dz
        dy_ref[...] = ((1.0 + g_ref[...]) * dz).astype(BF16)

        @pl.when(first)
        def _():
            dga_ref[...] = jnp.zeros_like(dga_ref)
            dbe_ref[...] = jnp.zeros_like(dbe_ref)
            dg_ref[...] = jnp.zeros_like(dg_ref)

        dga_ref[...] += jnp.sum(dxo_ * xh_, axis=0, keepdims=True)
        dbe_ref[...] += jnp.sum(dxo_, axis=0, keepdims=True)
        dg_ref[...] += jnp.sum(dz * y_ref[...], axis=0, keepdims=True)

    extra_in = [] if pre is None else [_rows(tr, D), _vec(D), _vec(D)]
    return pl.pallas_call(
        body, name=name, grid=(S // tr,),
        in_specs=[_rows(tr, D), _rows(tr, D), _rows(tr, 1), _vec(D), _rows(tr, D), _vec(D)] + extra_in,
        out_specs=[_rows(tr, D), _rows(tr, D)] + [_vec(D)] * (3 + (0 if pre is None else 2)),
        out_shape=[jax.ShapeDtypeStruct((S, D), F32), jax.ShapeDtypeStruct((S, D), BF16)]
        + [jax.ShapeDtypeStruct((1, D), F32)] * (3 + (0 if pre is None else 2)),
        compiler_params=_params(("arbitrary",)),
    )(dxo, xh, rstd, gamma, y, gate, *(pre or ()))


def _mod_bwd(dh, x, sc, dres, name):
    S, D = x.shape
    tr = min(256, S)

    def body(dh_ref, x_ref, sc_ref, dr_ref, dx_ref, dsc_ref, dsh_ref):
        dh_ = dh_ref[...]
        dx_ref[...] = dr_ref[...] + dh_ * (1.0 + sc_ref[...])

        @pl.when(pl.program_id(0) == 0)
        def _():
            dsc_ref[...] = jnp.zeros_like(dsc_ref)
            dsh_ref[...] = jnp.zeros_like(dsh_ref)

        dsc_ref[...] += jnp.sum(dh_ * x_ref[...], axis=0, keepdims=True)
        dsh_ref[...] += jnp.sum(dh_, axis=0, keepdims=True)

    return pl.pallas_call(
        body, name=name, grid=(S // tr,),
        in_specs=[_rows(tr, D), _rows(tr, D), _vec(D), _rows(tr, D)],
        out_specs=[_rows(tr, D), _vec(D), _vec(D)],
        out_shape=[jax.ShapeDtypeStruct((S, D), F32), jax.ShapeDtypeStruct((1, D), F32), jax.ShapeDtypeStruct((1, D), F32)],
        compiler_params=_params(("arbitrary",)),
    )(dh, x, sc, dres)


def _log_sigmoid(z):
    return jnp.minimum(z, 0.0) - jnp.log(1.0 + jnp.exp(-jnp.abs(z)))


def _fox_gate_fwd(proj, b_f, n_heads, name):
    S, PW = proj.shape
    blk = min(256, S)
    last = PW // LANES - 1

    def body(fl_ref, b_ref, cum_ref):
        r = lax.broadcasted_iota(jnp.int32, (blk, blk), 0)
        c = lax.broadcasted_iota(jnp.int32, (blk, blk), 1)
        tril = (c <= r).astype(F32)
        carry = jnp.zeros((1, LANES), F32)
        for i in range(S // blk):
            lf = _log_sigmoid(fl_ref[i * blk:(i + 1) * blk, :] + b_ref[...])
            cum_ref[i * blk:(i + 1) * blk, :] = jnp.dot(tril, lf, preferred_element_type=F32, precision=HIGHEST) + carry
            carry = carry + jnp.sum(lf, axis=0, keepdims=True)

    return pl.pallas_call(
        body, name=name, grid=(1,),
        in_specs=[pl.BlockSpec((S, LANES), lambda i: (0, last)), pl.BlockSpec((1, LANES), lambda i: (0, 0))],
        out_specs=pl.BlockSpec((S, LANES), lambda i: (0, 0)),
        out_shape=jax.ShapeDtypeStruct((S, LANES), F32), compiler_params=_params(("arbitrary",)),
    )(proj, b_f)


def _fox_gate_bwd(dcum, proj, b_f, n_heads, name):
    S, PW = proj.shape
    blk = min(256, S)
    last = PW // LANES - 1
    nb = S // blk

    def body(dc_ref, fl_ref, b_ref, dfl_ref, db_ref):
        r = lax.broadcasted_iota(jnp.int32, (blk, blk), 0)
        c = lax.broadcasted_iota(jnp.int32, (blk, blk), 1)
        triu = (c >= r).astype(F32)
        lane = lax.broadcasted_iota(jnp.int32, (blk, LANES), 1)
        carry = jnp.zeros((1, LANES), F32)
        dbs = jnp.zeros((1, LANES), F32)
        for i in reversed(range(nb)):
            dc = dc_ref[i * blk:(i + 1) * blk, :]
            dlf = jnp.dot(triu, dc, preferred_element_type=F32, precision=HIGHEST) + carry
            carry = carry + jnp.sum(dc, axis=0, keepdims=True)
            z = fl_ref[i * blk:(i + 1) * blk, :] + b_ref[...]
            e = jnp.exp(-jnp.abs(z))
            sig_neg = jnp.where(z >= 0, e / (1.0 + e), 1.0 / (1.0 + e))
            dfl = jnp.where(lane < n_heads, dlf * sig_neg, 0.0)
            dfl_ref[i * blk:(i + 1) * blk, :] = dfl.astype(BF16)
            dbs = dbs + jnp.sum(dfl, axis=0, keepdims=True)
        db_ref[...] = dbs

    return pl.pallas_call(
        body, name=name, grid=(1,),
        in_specs=[pl.BlockSpec((S, LANES), lambda i: (0, 0)), pl.BlockSpec((S, LANES), lambda i: (0, last)),
                  pl.BlockSpec((1, LANES), lambda i: (0, 0))],
        out_specs=[pl.BlockSpec((S, LANES), lambda i: (0, 0)), pl.BlockSpec((1, LANES), lambda i: (0, 0))],
        out_shape=[jax.ShapeDtypeStruct((S, LANES), BF16), jax.ShapeDtypeStruct((1, LANES), F32)],
        compiler_params=_params(("arbitrary",)),
    )(dcum, proj, b_f)


def _fox_scores(q_ref, kb_ref, cq_ref, ck_ref, qi, tq, scale):
    kk = (qi + 1) * tq
    rows = slice(qi * tq, (qi + 1) * tq)
    qb = q_ref[rows, :].astype(BF16)
    s = lax.dot_general(qb, kb_ref[0:kk, :], _DN["nt"], preferred_element_type=F32) * scale
    s = s + (cq_ref[0, rows, :] - ck_ref[0, :, 0:kk])
    r = lax.broadcasted_iota(jnp.int32, (tq, kk), 0) + qi * tq
    c = lax.broadcasted_iota(jnp.int32, (tq, kk), 1)
    mask = c <= r
    return jnp.where(mask, s, NEG), mask, qb


def _fox_fwd(proj, cq, ck, n_heads, name, comm=None):
    S = proj.shape[0]
    H = n_heads
    tq = min(FOX_TQ, S)
    nq = S // tq
    scale = FDH ** -0.5

    def body(q_ref, k_ref, v_ref, cq_ref, ck_ref, o_ref, lse_ref, kb_ref, vb_ref):
        kb_ref[...] = k_ref[...].astype(BF16)
        vb_ref[...] = v_ref[...].astype(BF16)
        for qi in range(nq):
            kk = (qi + 1) * tq
            rows = slice(qi * tq, (qi + 1) * tq)
            s, _, _ = _fox_scores(q_ref, kb_ref, cq_ref, ck_ref, qi, tq, scale)
            m = jnp.max(s, axis=-1, keepdims=True)
            p = jnp.exp(s - m)
            l = jnp.sum(p, axis=-1, keepdims=True)
            p = p * (1.0 / l)
            o_ref[rows, :] = jnp.dot(p.astype(BF16), vb_ref[0:kk, :], preferred_element_type=F32).astype(BF16)
            lse_ref[0, rows, :] = m + jnp.log(l)

    col = lambda off: pl.BlockSpec((S, FDH), lambda h: (0, h + off))
    stat_c = pl.BlockSpec((1, S, 1), lambda h: (h, 0, 0))
    stat_r = pl.BlockSpec((1, 1, S), lambda h: (h, 0, 0))
    return _call(
        body, name=name, grid=(H,),
        in_specs=[col(0), col(H), col(2 * H), stat_c, stat_r],
        out_specs=[col(0), stat_c],
        out_shape=[jax.ShapeDtypeStruct((S, H * FDH), BF16), jax.ShapeDtypeStruct((H, S, 1), F32)],
        scratch_shapes=[pltpu.VMEM((S, FDH), BF16), pltpu.VMEM((S, FDH), BF16)],
        sem=("parallel",), args=(proj, proj, proj, cq, ck), comm=comm)


def _fox_bwd(proj, cq, ck, lse, do, n_heads, name, comm=None):
    S = proj.shape[0]
    H = n_heads
    tq = min(FOX_TQ, S)
    nq = S // tq
    scale = FDH ** -0.5

    def body(q_ref, k_ref, v_ref, cq_ref, ck_ref, lse_ref, do_ref, dq_ref, dk_ref, dv_ref, dcq_ref, dck_ref,
             kb_ref, vb_ref, dka_ref, dva_ref):
        kb_ref[...] = k_ref[...].astype(BF16)
        vb_ref[...] = v_ref[...].astype(BF16)
        dka_ref[...] = jnp.zeros_like(dka_ref)
        dva_ref[...] = jnp.zeros_like(dva_ref)
        dck_ref[...] = jnp.zeros_like(dck_ref)
        for qi in range(nq):
            kk = (qi + 1) * tq
            rows = slice(qi * tq, (qi + 1) * tq)
            s, mask, qb = _fox_scores(q_ref, kb_ref, cq_ref, ck_ref, qi, tq, scale)
            p = jnp.where(mask, jnp.exp(s - lse_ref[0, rows, :]), 0.0)
            dob = do_ref[rows, :]
            dp = lax.dot_general(dob, vb_ref[0:kk, :], _DN["nt"], preferred_element_type=F32)
            delta = jnp.sum(p * dp, axis=-1, keepdims=True)
            ds = p * (dp - delta)
            dcq_ref[0, rows, :] = jnp.sum(ds, axis=-1, keepdims=True)
            dck_ref[0, :, 0:kk] -= jnp.sum(ds, axis=0, keepdims=True)
            dsb = (ds * scale).astype(BF16)
            dq_ref[rows, :] = jnp.dot(dsb, kb_ref[0:kk, :], preferred_element_type=F32).astype(BF16)
            dka_ref[0:kk, :] += lax.dot_general(dsb, qb, _DN["tn"], preferred_element_type=F32)
            dva_ref[0:kk, :] += lax.dot_general(p.astype(BF16), dob, _DN["tn"], preferred_element_type=F32)
        dk_ref[...] = dka_ref[...].astype(BF16)
        dv_ref[...] = dva_ref[...].astype(BF16)

    col = lambda off: pl.BlockSpec((S, FDH), lambda h: (0, h + off))
    stat_c = pl.BlockSpec((1, S, 1), lambda h: (h, 0, 0))
    stat_r = pl.BlockSpec((1, 1, S), lambda h: (h, 0, 0))
    wide = jax.ShapeDtypeStruct((S, H * FDH), BF16)
    return _call(
        body, name=name, grid=(H,),
        in_specs=[col(0), col(H), col(2 * H), stat_c, stat_r, stat_c, col(0)],
        out_specs=[col(0), col(0), col(0), stat_c, stat_r],
        out_shape=[wide, wide, wide, jax.ShapeDtypeStruct((H, S, 1), F32), jax.ShapeDtypeStruct((H, 1, S), F32)],
        scratch_shapes=[pltpu.VMEM((S, FDH), BF16), pltpu.VMEM((S, FDH), BF16), pltpu.VMEM((S, FDH), F32), pltpu.VMEM((S, FDH), F32)],
        sem=("parallel",), args=(proj, proj, proj, cq, ck, lse, do), comm=comm)


def _rope_tables(pos, sign):
    inv = ROPE_THETA ** (-jnp.arange(0, ROPE_DIM, 2, dtype=F32) / ROPE_DIM)
    ang = pos.astype(F32)[:, None] * inv
    cos, sin = jnp.cos(ang), sign * jnp.sin(ang)
    l64 = jnp.arange(LANES) % SDH
    idx = l64 % (ROPE_DIM // 2)
    c = jnp.where(l64 < ROPE_DIM, cos[:, idx], 1.0)
    sa = jnp.where(l64 < ROPE_DIM // 2, -sin[:, idx], 0.0)
    sb = jnp.where((l64 >= ROPE_DIM // 2) & (l64 < ROPE_DIM), sin[:, idx], 0.0)
    rot = jnp.stack([c, sa, sb])
    ident = jnp.stack([jnp.ones_like(c), jnp.zeros_like(c), jnp.zeros_like(c)])
    return jnp.stack([rot, ident]).astype(F32)


def _rope(xin, tabs, n_rot, out_dtype, name, comm=None):
    S, W = xin.shape

    def body(x_ref, t_ref, o_ref):
        xv = x_ref[...]
        o = xv * t_ref[0, 0] + pltpu.roll(xv, LANES - ROPE_DIM // 2, 1) * t_ref[0, 1] + pltpu.roll(xv, ROPE_DIM // 2, 1) * t_ref[0, 2]
        o_ref[...] = o.astype(out_dtype)

    return _call(
        body, name=name, grid=(W // LANES,),
        in_specs=[pl.BlockSpec((S, LANES), lambda j: (0, j)),
                  pl.BlockSpec((1, 3, S, LANES), lambda j: (jnp.where(j < n_rot, 0, 1), 0, 0, 0))],
        out_specs=[pl.BlockSpec((S, LANES), lambda j: (0, j))],
        out_shape=[jax.ShapeDtypeStruct((S, W), out_dtype)], sem=("parallel",), args=(xin, tabs), comm=comm)[0]


SWA_PER_STEP = 2


def _swa_bias():
    r = jnp.arange(WIN)[:, None]
    c = jnp.arange(2 * WIN)[None, :]
    first = c <= r
    later = (c > r) & (c <= r + WIN)
    return jnp.where(jnp.stack([first, later]), 0.0, NEG).astype(F32)


def _swa_probs(q_ref, k_ref, sk_ref, b_ref, n, j, scale):
    st = pl.multiple_of(jnp.maximum(n - 1, 0) * WIN, WIN)
    qb = q_ref[0, j]
    kb = k_ref[0, pl.ds(st, 2 * WIN), :]
    gm = qb.shape[0]
    s = lax.dot_general(qb, kb, _DN["nt"], preferred_element_type=F32) * scale
    s = (s.reshape(gm // WIN, WIN, 2 * WIN) + b_ref[jnp.minimum(n, 1)][None]).reshape(gm, 2 * WIN)
    sink = sk_ref[0]
    m = jnp.maximum(jnp.max(s, axis=-1, keepdims=True), sink)
    e = jnp.exp(s - m)
    es = jnp.exp(sink - m)
    inv = 1.0 / (jnp.sum(e, axis=-1, keepdims=True) + es)
    return e * inv, es * inv, st, qb, kb


def _swa_specs(S, gm):
    blk = pl.BlockSpec((1, SWA_PER_STEP, gm, SDH), lambda g, n: (g, n, 0, 0))
    kv = pl.BlockSpec((1, S, SDH), lambda g, n: (g, 0, 0))
    col = pl.BlockSpec((1, gm, 1), lambda g, n: (g, 0, 0))
    bias = pl.BlockSpec((2, WIN, 2 * WIN), lambda g, n: (0, 0, 0))
    return blk, kv, col, bias


def _swa_fwd(q, k, v, sinks, name, comm=None):
    KH, nb, gm, _ = q.shape
    S = k.shape[1]
    scale = SDH ** -0.5

    def body(q_ref, k_ref, v_ref, sk_ref, b_ref, o_ref):
        for j in range(SWA_PER_STEP):
            p, _, st, _, _ = _swa_probs(q_ref, k_ref, sk_ref, b_ref, pl.program_id(1) * SWA_PER_STEP + j, j, scale)
            vb = v_ref[0, pl.ds(st, 2 * WIN), :]
            o_ref[0, j] = jnp.dot(p.astype(BF16), vb, preferred_element_type=F32).astype(BF16)

    blk, kv, col, bias = _swa_specs(S, gm)
    return _call(
        body, name=name, grid=(KH, nb // SWA_PER_STEP), in_specs=[blk, kv, kv, col, bias], out_specs=[blk],
        out_shape=[jax.ShapeDtypeStruct(q.shape, BF16)], sem=("parallel", "parallel"), args=(q, k, v, sinks, _swa_bias()), comm=comm)[0]


def _swa_bwd(q, k, v, sinks, do, name, comm=None):
    KH, nb, gm, _ = q.shape
    S = k.shape[1]
    scale = SDH ** -0.5

    def body(q_ref, k_ref, v_ref, sk_ref, b_ref, do_ref, dq_ref, dk_ref, dv_ref, dsk_ref):
        @pl.when(pl.program_id(1) == 0)
        def _():
            dk_ref[...] = jnp.zeros_like(dk_ref)
            dv_ref[...] = jnp.zeros_like(dv_ref)
            dsk_ref[...] = jnp.zeros_like(dsk_ref)

        blocks = []
        for j in range(SWA_PER_STEP):
            p, ps, st, qb, kb = _swa_probs(q_ref, k_ref, sk_ref, b_ref, pl.program_id(1) * SWA_PER_STEP + j, j, scale)
            vb = v_ref[0, pl.ds(st, 2 * WIN), :]
            dob = do_ref[0, j]
            dp = lax.dot_general(dob, vb, _DN["nt"], preferred_element_type=F32)
            delta = jnp.sum(p * dp, axis=-1, keepdims=True)
            dsb = (p * (dp - delta) * scale).astype(BF16)
            dq_ref[0, j] = jnp.dot(dsb, kb, preferred_element_type=F32)
            blocks.append((st, lax.dot_general(dsb, qb, _DN["tn"], preferred_element_type=F32),
                           lax.dot_general(p.astype(BF16), dob, _DN["tn"], preferred_element_type=F32), ps * delta))
        for st, dk, dv, dsk in blocks:
            dk_ref[0, pl.ds(st, 2 * WIN), :] += dk
            dv_ref[0, pl.ds(st, 2 * WIN), :] += dv
            dsk_ref[0] -= dsk

    blk, kv, col, bias = _swa_specs(S, gm)
    return _call(
        body, name=name, grid=(KH, nb // SWA_PER_STEP), in_specs=[blk, kv, kv, col, bias, blk], out_specs=[blk, kv, kv, col],
        out_shape=[jax.ShapeDtypeStruct(q.shape, F32), jax.ShapeDtypeStruct(k.shape, F32),
                   jax.ShapeDtypeStruct(k.shape, F32), jax.ShapeDtypeStruct(sinks.shape, F32)],
        sem=("parallel", "arbitrary"), args=(q, k, v, sinks, _swa_bias(), do), comm=comm)


def _shift_down(u, k):
    row = lax.broadcasted_iota(jnp.int32, u.shape, 0)
    return jnp.where(row >= k, pltpu.roll(u, k, 0), 0.0)


def _shift_up(u, k):
    n = u.shape[0]
    row = lax.broadcasted_iota(jnp.int32, u.shape, 0)
    return jnp.where(row < n - k, pltpu.roll(u, n - k, 0), 0.0)


def _conv3(u, w_ref, b_ref):
    return w_ref[0:1, :] * _shift_down(u, 2) + w_ref[1:2, :] * _shift_down(u, 1) + w_ref[2:3, :] * u + b_ref[...]


def _conv_gate(u, cw, cb, name, comm=None):
    S, F2 = u.shape
    Fh = F2 // 2
    tc = _tile(Fh, 256)
    nf = Fh // tc

    def body(ug_ref, uv_ref, wg_ref, wv_ref, bg_ref, bv_ref, a_ref):
        g = _conv3(ug_ref[...], wg_ref, bg_ref)
        val = _conv3(uv_ref[...], wv_ref, bv_ref)
        a_ref[...] = (g * (1.0 / (1.0 + jnp.exp(-g))) * val).astype(BF16)

    blk = lambda r, off: pl.BlockSpec((r, tc), lambda j: (0, j + off))
    return _call(
        body, name=name, grid=(nf,),
        in_specs=[blk(S, 0), blk(S, nf), blk(3, 0), blk(3, nf), blk(1, 0), blk(1, nf)], out_specs=[blk(S, 0)],
        out_shape=[jax.ShapeDtypeStruct((S, Fh), BF16)], sem=("parallel",), args=(u, u, cw, cw, cb, cb), comm=comm)[0]


def _conv_gate_bwd(u, da, cw, cb, name, comm=None):
    S, F2 = u.shape
    Fh = F2 // 2
    tc = _tile(Fh, 256)
    nf = Fh // tc

    def half(h, dx, uu, w_ref, du_ref, dw_ref, db_ref):
        up1, up2 = _shift_up(dx, 1), _shift_up(dx, 2)
        du = w_ref[2:3, :] * dx + w_ref[1:2, :] * up1 + w_ref[0:1, :] * up2
        du_ref[h] = du.astype(BF16)
        dw_ref[h, 0:1, :] = jnp.sum(up2 * uu, axis=0, keepdims=True)
        dw_ref[h, 1:2, :] = jnp.sum(up1 * uu, axis=0, keepdims=True)
        dw_ref[h, 2:3, :] = jnp.sum(dx * uu, axis=0, keepdims=True)
        db_ref[h] = jnp.sum(dx, axis=0, keepdims=True)

    def body(ug_ref, uv_ref, da_ref, wg_ref, wv_ref, bg_ref, bv_ref, du_ref, dw_ref, db_ref):
        ug = ug_ref[...]
        uv = uv_ref[...]
        g = _conv3(ug, wg_ref, bg_ref)
        val = _conv3(uv, wv_ref, bv_ref)
        sig = 1.0 / (1.0 + jnp.exp(-g))
        da_ = da_ref[...]
        dg = da_ * val * (sig * (1.0 + g * (1.0 - sig)))
        dval = da_ * (g * sig)
        half(0, dg, ug, wg_ref, du_ref, dw_ref, db_ref)
        half(1, dval, uv, wv_ref, du_ref, dw_ref, db_ref)

    blk = lambda r, off: pl.BlockSpec((r, tc), lambda j: (0, j + off))
    both = lambda r: pl.BlockSpec((2, r, tc), lambda j: (0, 0, j))
    return _call(
        body, name=name, grid=(nf,),
        in_specs=[blk(S, 0), blk(S, nf), blk(S, 0), blk(3, 0), blk(3, nf), blk(1, 0), blk(1, nf)],
        out_specs=[both(S), both(3), both(1)],
        out_shape=[jax.ShapeDtypeStruct((2, S, Fh), BF16), jax.ShapeDtypeStruct((2, 3, Fh), F32), jax.ShapeDtypeStruct((2, 1, Fh), F32)],
        sem=("parallel",), args=(u, u, da, cw, cw, cb, cb), comm=comm)


def _to_groups(t, kh):
    S, width = t.shape
    g = width // SDH // kh
    return t.reshape(S // WIN, WIN, kh, g, SDH).transpose(2, 0, 3, 1, 4).reshape(kh, S // WIN, g * WIN, SDH)


def _from_groups(t):
    kh, nb, gm, _ = t.shape
    g = gm // WIN
    return t.reshape(kh, nb, g, WIN, SDH).transpose(1, 3, 0, 2, 4).reshape(nb * WIN, kh * g * SDH)


class LocalWeights:
    def __init__(self, weights):
        self.weights, self.grads = weights, {}

    def w(self, name):
        return self.weights[name]

    def carry(self, stage, last=()):
        return None

    def carried(self, stage, comm):
        pass

    def grad(self, name, g):
        self.grads[name] = g


def _local_step(dm, x, tgt, pos, mod, sp, pp):
    S, D, FH, QH, KH, Fh = dm
    m = [[mod[i:i + 1, j * D:(j + 1) * D] for j in range(6)] for i in range(DEPTH)]

    last = []

    def run(fn, *args, name, **kw):
        comm = pp.carry(name, last)
        args = [pp.w(arg[1]) if isinstance(arg, tuple) and arg[:1] == ("w",) else arg for arg in args]
        out = fn(*args, name=name, comm=comm, **kw)
        if comm is not None:
            pp.carried(name, comm)
        last[:] = list(out) if isinstance(out, (list, tuple)) else [out]
        return out

    sv = []
    xs = x
    h = _modulate(xs, m[0][1], m[0][0], "mod_in")
    last[:] = [h]
    for i in range(DEPTH):
        sh1, sc1, g1, sh2, sc2, g2 = m[i]
        L = {}
        L["x_in"], L["h1"] = xs, h
        if i == 0:
            proj = run(_mm, h[None], ("w", "fox_w_in"), mode="nn", out_dtype=F32, name="fox_proj", tn=896)[0]
            cum = _fox_gate_fwd(proj, sp["fox_b_f"], FH, "fox_gate")
            cq = cum[:, :FH].T[:, :, None]
            ck = cum[:, :FH].T[:, None, :]
            o, lse = run(_fox_fwd, proj, cq, ck, FH, name="fox_attn")
            L.update(proj=proj, cq=cq, ck=ck, lse=lse, o=o)
            y = run(_mm, o[None], ("w", "fox_w_o"), mode="nn", out_dtype=F32, name="fox_out")[0]
        else:
            proj = run(_mm, h[None], ("w", "swa_w_in"), mode="nn", out_dtype=F32, name="swa_proj", tn=640)[0]
            tabs = _rope_tables(pos, 1.0)
            n_rot = (QH + KH) * SDH // LANES
            pr = run(_rope, proj, tabs, n_rot, BF16, name="swa_rope")
            qh = _to_groups(pr[:, :QH * SDH], KH)
            kh = pr[:, QH * SDH:(QH + KH) * SDH].reshape(S, KH, SDH).transpose(1, 0, 2)
            vh = pr[:, (QH + KH) * SDH:].reshape(S, KH, SDH).transpose(1, 0, 2)
            oh = run(_swa_fwd, qh, kh, vh, sp["sinks"], name="swa_attn")
            o = _from_groups(oh)
            L.update(qh=qh, kh=kh, vh=vh, o=o)
            y = run(_mm, o[None], ("w", "swa_w_o"), mode="nn", out_dtype=F32, name="swa_out")[0]
        L["y1"] = y
        x1, L["xh1"], L["rs1"], h2 = run(_ln_fwd, xs, y, g1, sp["ln_mix_g"][i], sp["ln_mix_b"][i], sc2, sh2, name=f"ln_mix{i}")
        L["x1"], L["h2"] = x1, h2
        u = run(_mm, h2[None], ("w", f"ffn_w_up{i}"), mode="nn", out_dtype=F32, name=f"ffn_up{i}", tm=512, tn=1408)[0]
        a = run(_conv_gate, u, sp["conv_w"][i], sp["conv_b"][i], name=f"ffn_gate{i}")
        y2 = run(_mm, a[None], ("w", f"ffn_w_down{i}"), mode="nn", out_dtype=F32, name=f"ffn_down{i}", tk=5632, tm=512)[0]
        L.update(u=u, a=a, y2=y2)
        if i + 1 < DEPTH:
            xs, L["xh2"], L["rs2"], h = run(_ln_fwd, x1, y2, g2, sp["ln_ffn_g"][i], sp["ln_ffn_b"][i], m[i + 1][1], m[i + 1][0], name=f"ln_ffn{i}")
        else:
            xs, L["xh2"], L["rs2"] = run(_ln_fwd, x1, y2, g2, sp["ln_ffn_g"][i], sp["ln_ffn_b"][i], None, None, name=f"ln_ffn{i}")
        sv.append(L)

    dx, loss_cols = _loss_head(xs, tgt, "loss_head")

    gs = {k: [None] * DEPTH for k in ("conv_w", "conv_b", "ln_mix_g", "ln_mix_b", "ln_ffn_g", "ln_ffn_b")}
    dmp = [dict() for _ in range(DEPTH)]
    dres, pend = dx, None
    for i in reversed(range(DEPTH)):
        sh1, sc1, g1, sh2, sc2, g2 = m[i]
        L = sv[i]
        res = _ln_bwd(dres, L["xh2"], L["rs2"], sp["ln_ffn_g"][i], L["y2"], g2, f"ln_ffn_bwd{i}",
                      None if pend is None else (*pend, sp["ln_ffn_b"][i]))
        dres, dy, gs["ln_ffn_g"][i], gs["ln_ffn_b"][i], dmp[i]["g2"] = res[:5]
        if pend is not None:
            dmp[i + 1]["sc1"], dmp[i + 1]["sh1"] = res[5:]
        da = run(_mm, dy[None], pp.w(f"ffn_w_down{i}"), mode="nt", out_dtype=F32, name=f"ffn_down_dx{i}", tm=512, tn=1408)[0]
        pp.grad(f"ffn_w_down{i}", run(_mm, L["a"][None], dy[None], mode="tn", out_dtype=BF16, name=f"ffn_down_dw{i}", tm=1408))
        du, dcw, dcb = run(_conv_gate_bwd, L["u"], da, sp["conv_w"][i], sp["conv_b"][i], name=f"ffn_gate_bwd{i}")
        gs["conv_w"][i] = dcw.transpose(1, 0, 2).reshape(3, 2 * Fh)
        gs["conv_b"][i] = dcb.transpose(1, 0, 2).reshape(1, 2 * Fh)
        dh2 = run(_mm, du, pp.w(f"ffn_w_up{i}"), mode="nt", out_dtype=F32, name=f"ffn_up_dx{i}", tk=2816)[0]
        pp.grad(f"ffn_w_up{i}", run(_mm, L["h2"][None], du, mode="tn", out_dtype=BF16, name=f"ffn_up_dw{i}", out_groups=N_CHIPS, tn=1408))
        dres, dy, gs["ln_mix_g"][i], gs["ln_mix_b"][i], dmp[i]["g1"], dmp[i]["sc2"], dmp[i]["sh2"] = _ln_bwd(
            dres, L["xh1"], L["rs1"], sp["ln_mix_g"][i], L["y1"], g1, f"ln_mix_bwd{i}", (dh2, sc2, sp["ln_mix_b"][i]))
        if i == 0:
            do = run(_mm, dy[None], pp.w("fox_w_o"), mode="nt", out_dtype=BF16, name="fox_out_dx")[0]
            pp.grad("fox_w_o", run(_mm, L["o"][None], dy[None], mode="tn", out_dtype=BF16, name="fox_out_dw"))
            dq, dk, dv, dcq, dck = run(_fox_bwd, L["proj"], L["cq"], L["ck"], L["lse"], do, FH, name="fox_attn_bwd")
            dcum = dcq[:, :, 0].T + dck[:, 0, :].T
            dcum = jnp.pad(dcum, ((0, 0), (0, LANES - FH)))
            dfl, db_f = _fox_gate_bwd(dcum, L["proj"], sp["fox_b_f"], FH, "fox_gate_bwd")
            gs["fox_b_f"] = db_f
            dproj = jnp.concatenate([dq, dk, dv, dfl], axis=1)
            pp.grad("fox_w_in", run(_mm, L["h1"][None], dproj[None], mode="tn", out_dtype=BF16, name="fox_proj_dw", tn=896))
            dh1 = run(_mm, dproj[None], pp.w("fox_w_in"), mode="nt", out_dtype=F32, name="fox_proj_dx", tk=6272, tm=512)[0]
        else:
            do = run(_mm, dy[None], pp.w("swa_w_o"), mode="nt", out_dtype=BF16, name="swa_out_dx")[0]
            pp.grad("swa_w_o", run(_mm, L["o"][None], dy[None], mode="tn", out_dtype=BF16, name="swa_out_dw"))
            dqh, dkh, dvh, dsk = run(_swa_bwd, L["qh"], L["kh"], L["vh"], sp["sinks"], _to_groups(do, KH), name="swa_attn_bwd")
            gs["sinks"] = jnp.sum(dsk.reshape(QH, WIN), axis=1)
            dpr = jnp.concatenate([_from_groups(dqh), dkh.transpose(1, 0, 2).reshape(S, KH * SDH),
                                   dvh.transpose(1, 0, 2).reshape(S, KH * SDH)], axis=1)
            n_rot = (QH + KH) * SDH // LANES
            dproj = _rope(dpr, _rope_tables(pos, -1.0), n_rot, BF16, "swa_rope_bwd")
            dh1 = run(_mm, dproj[None], pp.w("swa_w_in"), mode="nt", out_dtype=F32, name="swa_proj_dx", tk=640)[0]
            pp.grad("swa_w_in", run(_mm, L["h1"][None], dproj[None], mode="tn", out_dtype=BF16, name="swa_proj_dw", out_groups=N_CHIPS, tn=640))
        pend = (dh1, sc1)
    grad_x, dmp[0]["sc1"], dmp[0]["sh1"] = _mod_bwd(pend[0], sv[0]["x_in"], pend[1], dres, "mod_mix_bwd0")
    dmod = [jnp.concatenate([p["sh1"], p["sc1"], p["g1"], p["sh2"], p["sc2"], p["g2"]], axis=1) for p in dmp]
    return loss_cols, grad_x, gs, jnp.concatenate(dmod, axis=0)


def _allgather_small(v, name):
    m_per, n = v.shape

    def body(x_ref, out_ref, send_sems, recv_sems, local_sem):
        x, y, c, chips = _place()
        me, sibling = (x, y, c), (x, y, 1 - c)

        def rows(px, py, pc):
            return out_ref.at[pl.ds((4 * px + 2 * py + pc) * m_per, m_per), :]

        def copy(k, block, to, src=None):
            return _remote(rows(*block) if src is None else src, rows(*block), send_sems.at[k], recv_sems.at[k], to)

        mine = pltpu.make_async_copy(x_ref, rows(*me), local_sem)
        mine.start()
        first = [copy(0, me, sibling, src=x_ref)]
        first += [copy(1 + j, me, (*chip, c), src=x_ref) for j, chip in enumerate(chips)]
        for cp in first:
            cp.start()
        passed = [copy(4 + j, (*chip, c), sibling) for j, chip in enumerate(chips)]
        for j, chip in enumerate(chips):
            copy(1 + j, (*chip, c), me).wait_recv()
            passed[j].start()
        copy(0, sibling, me).wait_recv()
        for j, chip in enumerate(chips):
            copy(4 + j, (*chip, 1 - c), me).wait_recv()
        for cp in first + passed:
            cp.wait_send()
        mine.wait()

    return pl.pallas_call(
        body, name=name, out_shape=jax.ShapeDtypeStruct((N_DEV * m_per, n), v.dtype),
        in_specs=[pl.BlockSpec(memory_space=pltpu.VMEM)], out_specs=pl.BlockSpec(memory_space=pltpu.VMEM),
        scratch_shapes=[pltpu.SemaphoreType.DMA((7,)), pltpu.SemaphoreType.DMA((7,)), pltpu.SemaphoreType.DMA],
        compiler_params=pltpu.CompilerParams(vmem_limit_bytes=VMEM_LIMIT),
    )(v)


def _row_tile(r, pref=256):
    return _tile(r, pref, 16)


def _cast_bf16(w, layer, chip, name, after=()):
    _, R, C = w.shape
    tr = _row_tile(R)

    def body(s_ref, w_ref, *rest):
        rest[-1][...] = w_ref[...].astype(BF16)

    return pl.pallas_call(
        body, name=name,
        grid_spec=pltpu.PrefetchScalarGridSpec(
            num_scalar_prefetch=1, grid=(R // tr,),
            in_specs=[pl.BlockSpec((None, tr, C), lambda i, s: (layer, i, 0))] + _any_specs(len(after)),
            out_specs=pl.BlockSpec((None, tr, C), lambda i, s: (s[0], i, 0))),
        out_shape=jax.ShapeDtypeStruct((N_CHIPS, R, C), BF16), compiler_params=_params(("parallel",)),
    )(jnp.reshape(chip, (1,)).astype(jnp.int32), w, *after)


def _add_sibling(g, got, c, name):
    G, R, C = g.shape
    rh = R // 2
    tr = _row_tile(rh)
    nb = rh // tr

    def body(c_ref, g_ref, o_ref, p_ref):
        p_ref[...] = (g_ref[...].astype(F32) + o_ref[...].astype(F32)).astype(BF16)

    return pl.pallas_call(
        body, name=name,
        grid_spec=pltpu.PrefetchScalarGridSpec(
            num_scalar_prefetch=1, grid=(G, nb),
            in_specs=[pl.BlockSpec((1, tr, C), lambda s, i, c_ref: (s, c_ref[0] * nb + i, 0)),
                      pl.BlockSpec((1, tr, C), lambda s, i, c_ref: (s, i, 0))],
            out_specs=pl.BlockSpec((1, tr, C), lambda s, i, c_ref: (s, i, 0))),
        out_shape=jax.ShapeDtypeStruct((G, rh, C), BF16), compiler_params=_params(("parallel", "parallel")),
    )(jnp.reshape(c, (1,)).astype(jnp.int32), g, got)


def _sum_chips(part, landed, chip, c, name):
    G, rh, C = part.shape
    tr = _row_tile(rh)
    nb = rh // tr

    def body(p_ref, own_ref, *rest):
        acc = own_ref[...].astype(F32)
        for ref in rest[:G - 1]:
            acc = acc + ref[...].astype(F32)
        rest[G - 1][...] = acc

    slot = lambda k: pl.BlockSpec((None, tr, C), lambda i, p: ((p[0] + k) % G, i, 0))
    return pl.pallas_call(
        body, name=name,
        grid_spec=pltpu.PrefetchScalarGridSpec(
            num_scalar_prefetch=1, grid=(nb,), in_specs=[slot(k) for k in range(G)],
            out_specs=pl.BlockSpec((tr, C), lambda i, p: (p[1] * nb + i, 0))),
        out_shape=jax.ShapeDtypeStruct((2 * rh, C), F32), compiler_params=_params(("parallel",)),
    )(jnp.stack([chip, c]).astype(jnp.int32), part, *([landed] * (G - 1)))


def _adam_math(w, g, m, v):
    m = ADAM_B1 * m + (1.0 - ADAM_B1) * g
    v = ADAM_B2 * v + (1.0 - ADAM_B2) * (g * g)
    m_hat = m / (1.0 - ADAM_B1 ** ADAM_STEP)
    v_hat = v / (1.0 - ADAM_B2 ** ADAM_STEP)
    delta = -ADAM_LR * (m_hat / (jnp.sqrt(v_hat) + ADAM_EPS) + ADAM_WD * w)
    return delta, m, v


def _adamw(w, g, m, v, layer, prev, name, by_cols=False, after=()):
    L, R, C = w.shape
    tr = R if by_cols else _tile(R, 128, 8)
    tc = _tile(C, 256) if by_cols else C
    n_alias = len(prev)
    prev = tuple(prev) + tuple(after)
    n_prev = len(prev)

    def body(w_ref, g_ref, m_ref, v_ref, *rest):
        go_ref, d_ref, mo_ref, vo_ref = rest[n_prev:]
        gv = g_ref[...]
        go_ref[...] = gv
        d_ref[...], mo_ref[...], vo_ref[...] = _adam_math(w_ref[...], gv, m_ref[...], v_ref[...])

    lay = pl.BlockSpec((None, tr, tc), lambda i: (layer, i // (C // tc), i % (C // tc)))
    flat = pl.BlockSpec((tr, tc), lambda i: (i // (C // tc), i % (C // tc)))
    return _call(
        body, name=name, grid=((R // tr) * (C // tc),), in_specs=[lay, flat, lay, lay] + _any_specs(n_prev), out_specs=[lay] * 4,
        out_shape=[jax.ShapeDtypeStruct((L, R, C), F32)] * 4, aliases={4 + k: k for k in range(n_alias)},
        sem=("parallel",), args=(w, g, m, v, *prev))


def _cond_rows(c_row, cw, name):
    D = c_row.shape[1]
    nr, fc = cw.shape

    def body(c_ref, e_ref, o_ref):
        o_ref[...] = jnp.zeros_like(o_ref)
        cv = c_ref[...]
        o_ref[0:1, 0:D] = cv * (1.0 / (1.0 + jnp.exp(-cv)))
        o_ref[8:8 + nr, 0:fc] = e_ref[...]

    return pl.pallas_call(body, name=name, out_shape=jax.ShapeDtypeStruct((16, max(D, fc)), F32))(c_row, cw)


def _ada_fwd(cact, ada_w, ada_b, layer, chip, name):
    _, D, NC = ada_w.shape
    tn = _tile(NC, 1024)
    nj = NC // tn

    def body(idx_ref, c_ref, w_ref, b_ref, o_ref):
        acc = jnp.dot(c_ref[...].astype(BF16), w_ref[0].astype(BF16), preferred_element_type=F32)
        o_ref[...] = acc + b_ref[pl.ds(idx_ref[0], 1), :]

    return pl.pallas_call(
        body, name=name,
        grid_spec=pltpu.PrefetchScalarGridSpec(
            num_scalar_prefetch=1, grid=(nj,),
            in_specs=[pl.BlockSpec((8, D), lambda j, idx: (0, 0)),
                      pl.BlockSpec((1, D, tn), lambda j, idx: (idx[0], 0, j)),
                      pl.BlockSpec((DEPTH, tn), lambda j, idx: (0, idx[1] * nj + j))],
            out_specs=pl.BlockSpec((8, tn), lambda j, idx: (0, j))),
        out_shape=jax.ShapeDtypeStruct((8, NC), F32), compiler_params=_params(("parallel",)),
    )(jnp.stack([layer, chip]).astype(jnp.int32), cact, ada_w, ada_b)


def _ada_grad_adamw(cact_t, dmod, w, m, v, name, comm=None):
    L, D, NC = w.shape
    tr = _tile(D, 128, 8)

    def body(c_ref, d_ref, w_ref, m_ref, v_ref, g_ref, dl_ref, mo_ref, vo_ref):
        g = jnp.dot(c_ref[...], d_ref[...], preferred_element_type=F32, precision=HIGHEST)
        g_ref[...] = g
        dl_ref[...], mo_ref[...], vo_ref[...] = _adam_math(w_ref[...], g, m_ref[...], v_ref[...])

    lay = pl.BlockSpec((None, tr, NC), lambda l, i: (l, i, 0))
    return _call(
        body, name=name, grid=(L, D // tr),
        in_specs=[pl.BlockSpec((tr, N_DEV), lambda l, i: (i, 0)), pl.BlockSpec((None, N_DEV, NC), lambda l, i: (l, 0, 0)), lay, lay, lay],
        out_specs=[lay] * 4, out_shape=[jax.ShapeDtypeStruct((L, D, NC), F32)] * 4,
        sem=("parallel", "parallel"), args=(cact_t, dmod, w, m, v), comm=comm)


def _sum_devices(gathered, name):
    n, R, C = gathered.shape

    def body(g_ref, o_ref):
        acc = g_ref[0]
        for j in range(1, n):
            acc = acc + g_ref[j]
        o_ref[...] = acc

    return pl.pallas_call(body, name=name, out_shape=jax.ShapeDtypeStruct((R, C), F32),
                          compiler_params=pltpu.CompilerParams(vmem_limit_bytes=VMEM_LIMIT))(gathered)


def _adamw_small(w, g, m, v, name):
    def body(w_ref, g_ref, m_ref, v_ref, d_ref, mo_ref, vo_ref):
        d_ref[...], mo_ref[...], vo_ref[...] = _adam_math(w_ref[...], g_ref[...], m_ref[...], v_ref[...])

    return pl.pallas_call(body, name=name, out_shape=[jax.ShapeDtypeStruct(w.shape, F32)] * 3)(w, g, m, v)


def _pad_rows(flat, unit=8 * LANES):
    n = flat.shape[0]
    total = -(-n // unit) * unit
    return jnp.pad(flat, (0, total - n)).reshape(total // LANES, LANES)


def _pad_lanes(v2d):
    return jnp.pad(v2d.reshape(1, -1), ((0, 0), (0, LANES - v2d.size)))


FORWARD = {
    "fox_proj": ((), ("ffn_w_up0",)),
    "ffn_up0": (("ffn_w_up0",), ("ffn_w_down0", "swa_w_in")),
    "ffn_down0": (("ffn_w_down0", "swa_w_in"), ("swa_w_o", "ffn_w_up1")),
    "swa_out": (("swa_w_o", "ffn_w_up1"), ("ffn_w_down1",)),
    "ffn_down1": (("ffn_w_down1",), ()),
}
PLAN = {
    "ffn_gate_bwd1": [("swap", "ffn_w_down1")],
    "ffn_up_dx1": [("scatter", "ffn_w_down1", 0, 1, 1)],
    "swa_out_dx": [("swap", "ffn_w_up1")],
    "swa_attn_bwd": [("scatter", "ffn_w_up1", 0, 6, 8), ("swap", "swa_w_o")],
    "swa_proj_dx": [("scatter", "swa_w_o", 0, 1, 1)],
    "ffn_down_dx0": [("scatter", "ffn_w_up1", 6, 8, 8), ("swap", "swa_w_in")],
    "ffn_down_dw0": [("scatter", "swa_w_in", 0, 1, 1)],
    "ffn_gate_bwd0": [("swap", "ffn_w_down0")],
    "ffn_up_dx0": [("scatter", "ffn_w_down0", 0, 1, 1)],
    "fox_out_dx": [("swap", "ffn_w_up0")],
    "fox_attn_bwd": [("scatter", "ffn_w_up0", 0, 5, 8), ("swap", "fox_w_o")],
    "fox_proj_dw": [("scatter", "fox_w_o", 0, 1, 1), ("scatter", "ffn_w_up0", 5, 6, 8)],
    "fox_proj_dx": [("scatter", "ffn_w_up0", 6, 8, 8), ("swap", "fox_w_in")],
}


class Exchanges:
    def __init__(self, dm, slots, chip, c):
        self.dm, self.slots, self.chip, self.c = dm, dict(slots), chip, c
        self.raw, self.part, self.landed, self.grads, self.views, self.pending = {}, {}, {}, {}, {}, {}

    def gather_start(self, keys, name, after):
        self.first = (keys, _gather_comm([self.slots[k] for k in keys], [(0, 1, 1)] * len(keys), forward=False))
        self.first_state, token = _split_start(self.first[1], name + "_start", after)
        return token

    def gather_finish(self, after, name):
        keys, comm = self.first
        _split_wait(comm, self.first_state, after, name + "_wait")
        pass_on = _forward_comm(comm.results, [(0, 1, 1)] * len(keys))
        _run_comm(pass_on, name + "_pass")
        self.slots.update(zip(keys, pass_on.results))
        self.fence = list(pass_on.results)

    def before(self, stage, last):
        need, nxt = FORWARD[stage]
        if need:
            self.gather_finish(list(last), "gather_" + "_".join(need))
        if nxt:
            return self.gather_start(list(nxt), "gather_" + "_".join(nxt), list(last) + self.fence)
        return None

    def w(self, key):
        if key not in self.views:
            S, D, FH, QH, KH, Fh = self.dm
            full = self.slots[key]
            if key == "fox_w_in":
                cols = full.shape[2]
                full = jnp.pad(full.transpose(1, 0, 2).reshape(D, N_CHIPS * cols), ((0, 0), (0, 3 * D + LANES - N_CHIPS * cols)))[None]
            elif key in ("fox_w_o", "swa_w_o"):
                full = full.reshape(1, D, D)
            elif key.startswith("ffn_w_down"):
                full = full.reshape(1, Fh, D)
            self.views[key] = full
        return self.views[key]

    def carry(self, stage, last=()):
        todo = []
        token = self.before(stage, last) if stage in FORWARD else None
        if token is not None:
            todo.append(("order", [], Comm([token], [], {}, 1, lambda *refs: None, lambda *refs: None)))
        for kind, key, *chunk in PLAN.get(stage, ()):
            if kind == "gather":
                todo.append((kind, [key], _gather_comm([self.slots[key]], [tuple(chunk)])))
            elif kind == "swap":
                todo.append((kind, [key], _swap_comm([self.raw[key]])))
            elif kind == "scatter":
                todo.append((kind, [(key, *chunk)], _scatter_comm([self.part[key]], [self.landed.get(key)], [tuple(chunk)])))
        self.pending[stage] = todo
        return _merge([cm for _, _, cm in todo])

    def carried(self, stage, comm):
        for kind, keys, cm in self.pending.pop(stage):
            if kind == "gather":
                self.slots[keys[0]] = cm.results[0]
            elif kind == "swap":
                self.part[keys[0]] = _add_sibling(self.raw[keys[0]], cm.results[0], self.c, f"add_sibling_{keys[0]}")
            elif kind == "scatter":
                self.landed[keys[0][0]] = cm.results[0]

    def grad(self, key, g):
        S, D, FH, QH, KH, Fh = self.dm
        if key == "fox_w_in":
            cols = self.slots[key].shape[2]
            g = g[0][:, :N_CHIPS * cols].reshape(D, N_CHIPS, cols).transpose(1, 0, 2)
        elif key in ("fox_w_o", "swa_w_o"):
            g = g.reshape(N_CHIPS, D // N_CHIPS, D)
        elif key.startswith("ffn_w_down"):
            g = g.reshape(N_CHIPS, Fh // N_CHIPS, D)
        self.raw[key] = g

    def last_start(self, last, after):
        part = self.part[last]
        self.last = (last, _scatter_comm([part], [lax.empty(part.shape, part.dtype)], [(0, 1, 1)]))
        self.last_state, token = _split_start(self.last[1], "grads_last_start", after)
        return token

    def join_landed(self):
        keys = list(self.landed)
        join = _join_comm([_sum_chips(self.part[k], self.landed[k], self.chip, self.c, f"sum_chips_{k}") for k in keys])
        _run_comm(join, "grads_join")
        return dict(zip(keys, join.results))

    def last_finish(self, after):
        last, comm = self.last
        _split_wait(comm, self.last_state, after, "grads_last_wait")
        join = _join_comm([_sum_chips(self.part[last], comm.results[0], self.chip, self.c, f"sum_chips_{last}")])
        _run_comm(join, "grads_join_last")
        return join.results[0]


def _step(dm, a):
    S, D, FH, QH, KH, Fh = dm
    ix, iy, ic = lax.axis_index("x"), lax.axis_index("y"), lax.axis_index("c")
    chip = 2 * ix + iy
    dev = 2 * chip + ic
    F2c = a["ffn_w_up"].shape[2]
    NC = a["ada_w"].shape[2]

    names = ["fox_w_in", "fox_w_o", "swa_w_in", "swa_w_o", "ffn_w_up", "ffn_w_up", "ffn_w_down", "ffn_w_down"]
    layers = [0, 0, 0, 0, 0, 1, 0, 1]
    keys = ["fox_w_in", "fox_w_o", "swa_w_in", "swa_w_o", "ffn_w_up0", "ffn_w_up1", "ffn_w_down0", "ffn_w_down1"]
    cast = lambda t, after: _cast_bf16(a[names[t]], layers[t], chip, f"cast_{keys[t]}", after)
    pp = Exchanges(dm, {keys[t]: cast(t, ()) for t in (0, 1)}, chip, ic)

    e0 = _cond_rows(a["c"], a["ffn_conv_w"].reshape(DEPTH * 3, F2c), "silu_c")
    g0 = _allgather_small(e0, "gather_cond").reshape(N_DEV, 16, e0.shape[1])
    cact = g0[:, 0, :D]
    conv_w = g0[0::2, 8:8 + DEPTH * 3, :F2c].transpose(1, 0, 2).reshape(DEPTH, 3, N_CHIPS * F2c)
    rows = _ada_fwd(cact, a["ada_w"], a["ada_b"], ic, chip, "ada_proj")
    g1 = _allgather_small(rows, "gather_mod").reshape(N_CHIPS, DEPTH, 8, NC)
    mod = lax.dynamic_index_in_dim(g1, dev, axis=2, keepdims=False).transpose(1, 0, 2).reshape(DEPTH, N_CHIPS * NC)

    token = pp.gather_start(keys[:2], "gather_fox", [mod])
    pp.slots.update({keys[t]: cast(t, (token,)) for t in range(2, len(keys))})
    pp.gather_finish([pp.slots[k] for k in keys[2:]], "gather_fox")
    sp = {"fox_b_f": _pad_lanes(a["fox_b_f"]), "sinks": jnp.repeat(a["swa_sinks"].reshape(KH, QH // KH), WIN, axis=1)[:, :, None],
          "conv_w": [conv_w[i] for i in range(DEPTH)], "conv_b": [a["ffn_conv_b"][i:i + 1] for i in range(DEPTH)]}
    for nm in ("ln_mix_g", "ln_mix_b", "ln_ffn_g", "ln_ffn_b"):
        sp[nm] = [a[nm][i:i + 1] for i in range(DEPTH)]

    loss_cols, grad_x, gs, dmod = _local_step(dm, a["x"][0], a["loss_target"][0], a["positions"][0], mod, sp, pp)
    loss = lax.psum(0.5 / D * jnp.sum(loss_cols), ("x", "y", "c"))
    out = {"loss": loss, "grad_x": grad_x[None]}

    pieces = [dmod.reshape(-1), gs["fox_b_f"].reshape(-1), _pad_lanes(gs["sinks"]).reshape(-1),
              jnp.stack(gs["conv_w"]).reshape(-1), jnp.stack(gs["conv_b"]).reshape(-1)]
    pieces += [jnp.stack(gs[nm]).reshape(-1) for nm in ("ln_mix_g", "ln_mix_b", "ln_ffn_g", "ln_ffn_b")]
    sizes = [p.shape[0] for p in pieces]
    packed = _pad_rows(jnp.concatenate(pieces))
    allp = _allgather_small(packed, "gather_small").reshape(N_DEV, packed.shape[0], LANES)
    tot = _sum_devices(allp, "sum_small").reshape(-1)
    offs = [sum(sizes[:k]) for k in range(len(sizes))]
    take = lambda k: tot[offs[k]:offs[k] + sizes[k]]
    g_small = {"ada_b": take(0).reshape(DEPTH, -1), "fox_b_f": take(1)[:FH].reshape(1, FH), "swa_sinks": take(2)[:QH].reshape(1, QH),
               "ffn_conv_w": lax.dynamic_slice_in_dim(take(3).reshape(DEPTH, 3, N_CHIPS * F2c), chip * F2c, F2c, axis=2),
               "ffn_conv_b": take(4).reshape(DEPTH, -1)}
    for k, nm in enumerate(("ln_mix_g", "ln_mix_b", "ln_ffn_g", "ln_ffn_b")):
        g_small[nm] = take(5 + k).reshape(DEPTH, D)
    small = list(g_small)
    pack = lambda pre: _pad_rows(jnp.concatenate([(a[pre + nm] if pre else a[nm]).reshape(-1) for nm in small]))
    gp = _pad_rows(jnp.concatenate([g_small[nm].reshape(-1) for nm in small]))
    ds_, ms_, vs_ = _adamw_small(pack(""), gp, pack("m_"), pack("v_"), "adamw_small")
    off = 0
    for nm in small:
        n_el = a[nm].size
        out["grad_" + nm] = g_small[nm]
        for pre, arr in (("delta_", ds_), ("new_m_", ms_), ("new_v_", vs_)):
            out[pre + nm] = arr.reshape(-1)[off:off + n_el].reshape(a[nm].shape)
        off += n_el

    dmod_all = allp.reshape(N_DEV, -1)[:, :DEPTH * N_CHIPS * NC].reshape(N_DEV, DEPTH, N_CHIPS * NC)
    dmod_mine = lax.dynamic_slice_in_dim(dmod_all, chip * NC, NC, axis=2).transpose(1, 0, 2)

    grads = pp.join_landed()
    token = pp.last_start("fox_w_in", [ds_, dmod_mine] + list(grads.values()))
    ada = _ada_grad_adamw(cact.T, dmod_mine + token[0, 0], a["ada_w"], a["m_ada_w"], a["v_ada_w"], "ada_grad")
    for pre, arr in zip(("grad_", "delta_", "new_m_", "new_v_"), ada):
        out[pre + "ada_w"] = arr
    upd = {}
    for k, nm, l in zip(keys[1:], names[1:], layers[1:]):
        upd[nm] = _adamw(a[nm], grads[k], a["m_" + nm], a["v_" + nm], l, upd.get(nm, ()), f"adamw_{k}", after=(token,))
    g_last = pp.last_finish([ada[1]] + [res[1] for res in upd.values()])
    tview = lambda t: jnp.swapaxes(t, 1, 2)
    res = _adamw(tview(a["fox_w_in"]), g_last.T, tview(a["m_fox_w_in"]), tview(a["v_fox_w_in"]), 0, (), "adamw_fox_w_in", by_cols=True)
    upd["fox_w_in"] = [tview(r) for r in res]
    for nm, res in upd.items():
        for pre, arr in zip(("grad_", "delta_", "new_m_", "new_v_"), res):
            out[pre + nm] = arr
    return out


_WEIGHTS = ["fox_w_in", "fox_b_f", "fox_w_o", "swa_w_in", "swa_sinks", "swa_w_o", "ada_w", "ada_b", "ffn_w_up", "ffn_conv_w",
            "ffn_conv_b", "ffn_w_down", "ln_mix_g", "ln_mix_b", "ln_ffn_g", "ln_ffn_b"]
_INPUTS = (["x", "c", "positions"] + _WEIGHTS + ["loss_target"] + ["m_" + w for w in _WEIGHTS] + ["v_" + w for w in _WEIGHTS])


def kernel(x, c, positions, fox_w_in, fox_b_f, fox_w_o, swa_w_in, swa_sinks, swa_w_o, ada_w, ada_b, ffn_w_up, ffn_conv_w, ffn_conv_b, ffn_w_down, ln_mix_g, ln_mix_b, ln_ffn_g, ln_ffn_b, loss_target, m_fox_w_in, m_fox_b_f, m_fox_w_o, m_swa_w_in, m_swa_sinks, m_swa_w_o, m_ada_w, m_ada_b, m_ffn_w_up, m_ffn_conv_w, m_ffn_conv_b, m_ffn_w_down, m_ln_mix_g, m_ln_mix_b, m_ln_ffn_g, m_ln_ffn_b, v_fox_w_in, v_fox_b_f, v_fox_w_o, v_swa_w_in, v_swa_sinks, v_swa_w_o, v_ada_w, v_ada_b, v_ffn_w_up, v_ffn_conv_w, v_ffn_conv_b, v_ffn_w_down, v_ln_mix_g, v_ln_mix_b, v_ln_ffn_g, v_ln_ffn_b):
    args = (x, c, positions, fox_w_in, fox_b_f, fox_w_o, swa_w_in, swa_sinks, swa_w_o, ada_w, ada_b, ffn_w_up, ffn_conv_w, ffn_conv_b, ffn_w_down, ln_mix_g, ln_mix_b, ln_ffn_g, ln_ffn_b, loss_target, m_fox_w_in, m_fox_b_f, m_fox_w_o, m_swa_w_in, m_swa_sinks, m_swa_w_o, m_ada_w, m_ada_b, m_ffn_w_up, m_ffn_conv_w, m_ffn_conv_b, m_ffn_w_down, m_ln_mix_g, m_ln_mix_b, m_ln_ffn_g, m_ln_ffn_b, v_fox_w_in, v_fox_b_f, v_fox_w_o, v_swa_w_in, v_swa_sinks, v_swa_w_o, v_ada_w, v_ada_b, v_ffn_w_up, v_ffn_conv_w, v_ffn_conv_b, v_ffn_w_down, v_ln_mix_g, v_ln_mix_b, v_ln_ffn_g, v_ln_ffn_b)
    out = _step(PROD, dict(zip(_INPUTS, args)))
    order = ["loss", "grad_x"] + [p + w for p in ("grad_", "delta_", "new_m_", "new_v_") for w in _WEIGHTS]
    return tuple(out[k] for k in order)
```

```python
import functools
from typing import NamedTuple

import jax
import jax.numpy as jnp
from jax import lax
from jax.experimental import pallas as pl
from jax.experimental.pallas import tpu as pltpu

F32 = jnp.float32
BF16 = jnp.bfloat16
MESH = pl.DeviceIdType.MESH
HIGHEST = lax.Precision.HIGHEST

N_CHIPS = 4
N_DEV = 8
LANES = 128
VMEM_LIMIT = 56 * 1024 * 1024

DEPTH = 2
DEEPNORM_ALPHA = (2.0 * DEPTH) ** 0.25
LN_EPS = 1e-5
ROPE_THETA = 500000.0
ADAM_LR, ADAM_B1, ADAM_B2, ADAM_EPS, ADAM_WD, ADAM_STEP = 0.001, 0.9, 0.999, 1e-08, 0.01, 10
NEG = -1e30


class Dims(NamedTuple):
    S: int
    D: int
    FH: int
    QH: int
    KH: int
    F: int


PROD = Dims(S=2048, D=2048, FH=16, QH=32, KH=4, F=5632)
FDH = 128
SDH = 64
WIN = 128
ROPE_DIM = 16
FOX_TQ = 256


def _params(sem=None, vmem=VMEM_LIMIT):
    return pltpu.CompilerParams(dimension_semantics=sem, vmem_limit_bytes=vmem)


def _tile(n, pref, unit=LANES):
    if n <= pref:
        return n
    t = (pref // unit) * unit
    while t > 0:
        if n % t == 0:
            return t
        t -= unit
    return n


class Comm:
    def __init__(self, args, out_shapes, aliases, n_sem, start, finish, members=()):
        self.args, self.out_shapes, self.aliases, self.n_sem = list(args), list(out_shapes), dict(aliases), n_sem
        self.start, self.finish = start, finish
        self.members = members
        self.results = None

    def set_results(self, res):
        self.results = list(res)
        for cm, o0 in self.members:
            cm.set_results(self.results[o0:o0 + len(cm.out_shapes)])


class _SemView:
    def __init__(self, sems, first):
        self.sems, self.first = sems, first

    @property
    def at(self):
        return self

    def __getitem__(self, k):
        return self.sems.at[self.first + k]


def _merge(comms):
    comms = [cm for cm in comms if cm is not None]
    if len(comms) < 2:
        return comms[0] if comms else None
    args, shapes, aliases, spans, n_sem = [], [], {}, [], 0
    for cm in comms:
        spans.append((len(args), len(shapes), n_sem))
        aliases.update({len(args) + a: len(shapes) + o for a, o in cm.aliases.items()})
        args += cm.args
        shapes += cm.out_shapes
        n_sem += cm.n_sem

    def each(step):
        def run(ar, ou, send, recv):
            for cm, (a0, o0, s0) in zip(comms, spans):
                getattr(cm, step)(ar[a0:a0 + len(cm.args)], ou[o0:o0 + len(cm.out_shapes)], _SemView(send, s0), _SemView(recv, s0))
        return run

    return Comm(args, shapes, aliases, n_sem, each("start"), each("finish"), [(cm, o0) for cm, (_, o0, _) in zip(comms, spans)])


def _place():
    x, y, c = lax.axis_index("x"), lax.axis_index("y"), lax.axis_index("c")
    chips = [(1 - x, y), (x, 1 - y), (1 - x, 1 - y)]
    return x, y, c, chips


def _remote(src, dst, send, recv, to):
    return pltpu.make_async_remote_copy(src_ref=src, dst_ref=dst, send_sem=send, recv_sem=recv, device_id=to, device_id_type=MESH)


def _any_specs(n):
    return [pl.BlockSpec(memory_space=pl.ANY)] * n


def _call(body, *, name, grid, in_specs, out_specs, out_shape, args, sem, scratch_shapes=(), aliases=None, comm=None):
    in_specs, out_specs, out_shape, scratch_shapes = list(in_specs), list(out_specs), list(out_shape), list(scratch_shapes)
    aliases = dict(aliases or {})
    if comm is None:
        return pl.pallas_call(body, name=name, grid=grid, in_specs=in_specs, out_specs=out_specs, out_shape=out_shape,
                              scratch_shapes=scratch_shapes, input_output_aliases=aliases, compiler_params=_params(sem))(*args)
    n_in, n_out, nc_in, nc_out, n_scr = len(in_specs), len(out_specs), len(comm.args), len(comm.out_shapes), len(scratch_shapes)

    def wrapped(*refs):
        ins, refs = refs[:n_in], refs[n_in:]
        cin, refs = refs[:nc_in], refs[nc_in:]
        outs, refs = refs[:n_out], refs[n_out:]
        cout, refs = refs[:nc_out], refs[nc_out:]
        scratch, (send, recv) = refs[:n_scr], refs[n_scr:]
        ids = [pl.program_id(k) for k in range(len(grid))]
        first = functools.reduce(jnp.logical_and, [i == 0 for i in ids])
        last = functools.reduce(jnp.logical_and, [i == g - 1 for i, g in zip(ids, grid)])

        @pl.when(first)
        def _():
            comm.start(cin, cout, send, recv)

        body(*ins, *outs, *scratch)

        @pl.when(last)
        def _():
            comm.finish(cin, cout, send, recv)

    res = pl.pallas_call(
        wrapped, name=name, grid=grid, in_specs=in_specs + _any_specs(nc_in), out_specs=out_specs + _any_specs(nc_out),
        out_shape=out_shape + comm.out_shapes,
        scratch_shapes=scratch_shapes + [pltpu.SemaphoreType.DMA((comm.n_sem,)), pltpu.SemaphoreType.DMA((comm.n_sem,))],
        input_output_aliases={**aliases, **{n_in + a: n_out + o for a, o in comm.aliases.items()}},
        compiler_params=_params(("arbitrary",) * len(grid)),
    )(*args, *comm.args)
    comm.set_results(res[n_out:])
    return list(res[:n_out])


def _run_comm(comm, name):
    nc_in, nc_out = len(comm.args), len(comm.out_shapes)

    def body(*refs):
        cin, cout, (send, recv) = refs[:nc_in], refs[nc_in:nc_in + nc_out], refs[nc_in + nc_out:]
        comm.start(cin, cout, send, recv)
        comm.finish(cin, cout, send, recv)

    res = pl.pallas_call(
        body, name=name, in_specs=_any_specs(nc_in), out_specs=_any_specs(nc_out), out_shape=comm.out_shapes,
        scratch_shapes=[pltpu.SemaphoreType.DMA((comm.n_sem,)), pltpu.SemaphoreType.DMA((comm.n_sem,))],
        input_output_aliases=comm.aliases,
    )(*comm.args)
    comm.set_results(res)


_HBM = pl.BlockSpec(memory_space=pltpu.HBM)
_SEM = pl.BlockSpec(memory_space=pltpu.SEMAPHORE)
_EFFECT = pltpu.SideEffectType.DATAFLOW_SIDE_EFFECTING


def _split_start(comm, name, after=()):
    n = len(comm.args)
    back = {o: a for a, o in comm.aliases.items()}
    assert len(back) == len(comm.out_shapes)

    n_after = len(after)

    def body(*refs):
        refs = refs[n + n_after:]
        send, recv, thru, token = refs[0], refs[1], refs[2:n + 2], refs[n + 2]
        comm.start(thru, [thru[back[o]] for o in range(len(back))], send, recv)
        token[...] = jnp.zeros_like(token)

    res = pl.pallas_call(
        body, name=name,
        out_shape=(pltpu.SemaphoreType.DMA((comm.n_sem,)), pltpu.SemaphoreType.DMA((comm.n_sem,)),
                   *[pltpu.HBM(a.shape, a.dtype) for a in comm.args], jax.ShapeDtypeStruct((8, LANES), F32)),
        in_specs=[_HBM] * n + _any_specs(n_after), out_specs=(_SEM, _SEM, *[_HBM] * n, pl.BlockSpec(memory_space=pltpu.VMEM)),
        input_output_aliases={i: 2 + i for i in range(n)},
        compiler_params=pltpu.CompilerParams(has_side_effects=_EFFECT),
    )(*[pltpu.with_memory_space_constraint(a, pltpu.HBM) for a in comm.args], *after)
    return (res[0], res[1], list(res[2:2 + n])), res[2 + n]


def _split_wait(comm, state, after, name):
    send, recv, thru = state
    n, n_after = len(thru), len(after)
    back = {o: a for a, o in comm.aliases.items()}

    def body(*refs):
        ins, send_ref, recv_ref = refs[:n], refs[n], refs[n + 1]
        comm.finish(ins, [ins[back[o]] for o in range(len(back))], send_ref, recv_ref)

    res = pl.pallas_call(
        body, name=name, out_shape=tuple(pltpu.HBM(a.shape, a.dtype) for a in thru),
        in_specs=[_HBM] * n + [_SEM, _SEM] + _any_specs(n_after), out_specs=[_HBM] * n,
        input_output_aliases={i: i for i in range(n)}, compiler_params=pltpu.CompilerParams(has_side_effects=_EFFECT),
    )(*thru, send, recv, *after)
    comm.set_results([res[back[o]] for o in range(len(back))])


def _forward_comm(slots):
    n = len(slots)

    def rows(t, who):
        rh = slots[t].shape[1] // 2
        return pl.ds(who * rh, rh)

    def copy(outs, send, recv, t, j, chip, who):
        x, y, c, _ = _place()
        blk = outs[t].at[2 * chip[0] + chip[1], rows(t, who)]
        return _remote(blk, blk, send.at[3 * t + j], recv.at[3 * t + j], (x, y, 1 - c))

    def start(args, outs, send, recv):
        _, _, c, chips = _place()
        for t in range(n):
            for j, chip in enumerate(chips):
                copy(outs, send, recv, t, j, chip, c).start()

    def finish(args, outs, send, recv):
        _, _, c, chips = _place()
        for t in range(n):
            for j, chip in enumerate(chips):
                copy(outs, send, recv, t, j, chip, 1 - c).wait_recv()
        for t in range(n):
            for j, chip in enumerate(chips):
                copy(outs, send, recv, t, j, chip, c).wait_send()

    shapes = [jax.ShapeDtypeStruct(w.shape, w.dtype) for w in slots]
    return Comm(slots, shapes, {t: t for t in range(n)}, 3 * n, start, finish)


def _gather_comm(slots):
    n = len(slots)

    def rows(t, who):
        rh = slots[t].shape[1] // 2
        return pl.ds(who * rh, rh)

    def start(args, outs, send, recv):
        x, y, c, chips = _place()
        for t in range(n):
            mine = outs[t].at[2 * x + y, rows(t, c)]
            for j, chip in enumerate(chips):
                _remote(mine, mine, send.at[3 * t + j], recv.at[3 * t + j], (*chip, c)).start()

    def finish(args, outs, send, recv):
        x, y, c, chips = _place()
        for t in range(n):
            for j, chip in enumerate(chips):
                blk = outs[t].at[2 * chip[0] + chip[1], rows(t, c)]
                _remote(blk, blk, send.at[3 * t + j], recv.at[3 * t + j], (*chip, c)).wait_recv()
        for t in range(n):
            mine = outs[t].at[2 * x + y, rows(t, c)]
            for j, chip in enumerate(chips):
                _remote(mine, mine, send.at[3 * t + j], recv.at[3 * t + j], (*chip, c)).wait_send()

    shapes = [jax.ShapeDtypeStruct(w.shape, w.dtype) for w in slots]
    return Comm(slots, shapes, {t: t for t in range(n)}, 3 * n, start, finish)


def _scatter_comm(parts, landed, chunks):
    n = len(parts)
    prev = [t for t in range(n) if landed[t] is not None]

    def rows(t):
        lo, hi, nch = chunks[t]
        rc = parts[t].shape[1] // nch
        return pl.ds(lo * rc, (hi - lo) * rc)

    def start(args, outs, send, recv):
        x, y, c, chips = _place()
        s = 2 * x + y
        for t in range(n):
            for j, chip in enumerate(chips):
                _remote(args[t].at[2 * chip[0] + chip[1], rows(t)], outs[t].at[s, rows(t)],
                        send.at[3 * t + j], recv.at[3 * t + j], (*chip, c)).start()

    def finish(args, outs, send, recv):
        x, y, c, chips = _place()
        for t in range(n):
            for j, chip in enumerate(chips):
                blk = outs[t].at[2 * chip[0] + chip[1], rows(t)]
                _remote(blk, blk, send.at[3 * t + j], recv.at[3 * t + j], (*chip, c)).wait_recv()
        for t in range(n):
            for j, chip in enumerate(chips):
                src = args[t].at[2 * chip[0] + chip[1], rows(t)]
                _remote(src, src, send.at[3 * t + j], recv.at[3 * t + j], (*chip, c)).wait_send()

    shapes = [jax.ShapeDtypeStruct(p.shape, p.dtype) for p in parts]
    return Comm(list(parts) + [landed[t] for t in prev], shapes, {n + i: t for i, t in enumerate(prev)}, 3 * n, start, finish)


def _swap_comm(gs):
    n = len(gs)

    def copy(args, outs, send, recv, t):
        _, _, c, _ = _place()
        rh = gs[t].shape[1] // 2
        x, y = lax.axis_index("x"), lax.axis_index("y")
        return _remote(args[t].at[:, pl.ds((1 - c) * rh, rh), :], outs[t], send.at[t], recv.at[t], (x, y, 1 - c))

    def start(args, outs, send, recv):
        for t in range(n):
            copy(args, outs, send, recv, t).start()

    def finish(args, outs, send, recv):
        for t in range(n):
            copy(args, outs, send, recv, t).wait()

    shapes = [jax.ShapeDtypeStruct((g.shape[0], g.shape[1] // 2, g.shape[2]), g.dtype) for g in gs]
    return Comm(gs, shapes, {}, n, start, finish)


def _join_comm(gs):
    n = len(gs)

    def half(outs, t, who):
        rh = gs[t].shape[0] // 2
        return outs[t].at[pl.ds(who * rh, rh), :]

    def start(args, outs, send, recv):
        x, y, c, _ = _place()
        for t in range(n):
            _remote(half(outs, t, c), half(outs, t, c), send.at[t], recv.at[t], (x, y, 1 - c)).start()

    def finish(args, outs, send, recv):
        x, y, c, _ = _place()
        for t in range(n):
            _remote(half(outs, t, 1 - c), half(outs, t, 1 - c), send.at[t], recv.at[t], (x, y, 1 - c)).wait_recv()
        for t in range(n):
            _remote(half(outs, t, c), half(outs, t, c), send.at[t], recv.at[t], (x, y, 1 - c)).wait_send()

    shapes = [jax.ShapeDtypeStruct(g.shape, g.dtype) for g in gs]
    return Comm(gs, shapes, {t: t for t in range(n)}, n, start, finish)


_DN = {"nn": (((1,), (0,)), ((), ())), "nt": (((1,), (1,)), ((), ())), "tn": (((0,), (0,)), ((), ()))}


def _mm(a, b, *, mode, out_dtype, name, out_groups=1, tm=1024, tn=1024, tk=2048, comm=None):
    ga, ra, ca = a.shape
    gb, rb, cb = b.shape
    if mode == "nn":
        M, K, N = ra, ga * ca, gb * cb
        assert rb == K and ga == 1 or (rb == K)
    elif mode == "nt":
        M, K, N = ra, ga * ca, rb
        assert gb * cb == K
    else:
        K, M, N = ra, ga * ca, gb * cb
        assert rb == K
    go = out_groups
    if mode == "nn":
        tk = _tile(ca, tk); assert rb % tk == 0 and (ga == 1 or True)
        tn = _tile(min(cb, N // go), tn); tm = _tile(M, tm, 8)
    elif mode == "nt":
        tk = _tile(ca, tk); tk = _tile(cb, tk) if cb % tk else tk; assert ca % tk == 0 and cb % tk == 0
        tn = _tile(N // go, tn); tm = _tile(M, tm, 8)
    else:
        tk = _tile(K, tk, 8); tm = _tile(ca, tm); tn = _tile(min(cb, N // go), tn)
    assert (N // go) % tn == 0 and M % tm == 0 and K % tk == 0, (name, M, N, K, tm, tn, tk)
    nk = K // tk
    kpa = max(ca // tk, 1)
    kpb = max(cb // tk, 1)
    npb = max(cb // tn, 1)
    npo = (N // go) // tn
    mpa = max(ca // tm, 1)

    if mode == "nn":
        a_spec = pl.BlockSpec((1, tm, tk), lambda j, i, k: (k // kpa, i, k % kpa))
        b_spec = pl.BlockSpec((1, tk, tn), lambda j, i, k: (j // npb, k, j % npb))
    elif mode == "nt":
        a_spec = pl.BlockSpec((1, tm, tk), lambda j, i, k: (k // kpa, i, k % kpa))
        b_spec = pl.BlockSpec((1, tn, tk), lambda j, i, k: (k // kpb, j, k % kpb))
    else:
        a_spec = pl.BlockSpec((1, tk, tm), lambda j, i, k: (i // mpa, k, i % mpa))
        b_spec = pl.BlockSpec((1, tk, tn), lambda j, i, k: (j // npb, k, j % npb))
    o_spec = pl.BlockSpec((1, tm, tn), lambda j, i, k: (j // npo, i, j % npo))
    dn = _DN[mode]

    def body(a_ref, b_ref, o_ref, *acc):
        p = lax.dot_general(a_ref[0], b_ref[0], dn, preferred_element_type=F32)
        if nk == 1:
            o_ref[0] = p.astype(out_dtype)
        else:
            k = pl.program_id(2)

            @pl.when(k == 0)
            def _():
                acc[0][...] = p

            @pl.when(k > 0)
            def _():
                acc[0][...] += p

            @pl.when(k == nk - 1)
            def _():
                o_ref[0] = acc[0][...].astype(out_dtype)

    return _call(
        body, name=name, grid=(N // tn, M // tm, nk), in_specs=[a_spec, b_spec], out_specs=[o_spec],
        out_shape=[jax.ShapeDtypeStruct((go, M, N // go), out_dtype)],
        scratch_shapes=[pltpu.VMEM((tm, tn), F32)] if nk > 1 else [],
        sem=("parallel", "parallel", "arbitrary"), args=(a, b), comm=comm)[0]


def _rows(tr, d):
    return pl.BlockSpec((tr, d), lambda i: (i, 0))


def _vec(d):
    return pl.BlockSpec((1, d), lambda i: (0, 0))


def _modulate(x, sc, sh, name):
    S, D = x.shape
    tr = min(256, S)

    def body(x_ref, sc_ref, sh_ref, h_ref):
        h_ref[...] = (x_ref[...] * (1.0 + sc_ref[...]) + sh_ref[...]).astype(BF16)

    return pl.pallas_call(
        body, name=name, grid=(S // tr,), in_specs=[_rows(tr, D), _vec(D), _vec(D)], out_specs=_rows(tr, D),
        out_shape=jax.ShapeDtypeStruct((S, D), BF16), compiler_params=_params(("parallel",)),
    )(x, sc, sh)


def _ln_fwd(x, y, gate, gamma, beta, sc, sh, name, comm=None):
    S, D = x.shape
    tr = min(256, S)
    emit_h = sc is not None

    def body(*refs):
        if emit_h:
            x_ref, y_ref, g_ref, ga_ref, be_ref, sc_ref, sh_ref, xo_ref, xh_ref, rs_ref, h_ref = refs
        else:
            x_ref, y_ref, g_ref, ga_ref, be_ref, xo_ref, xh_ref, rs_ref = refs
        z = DEEPNORM_ALPHA * x_ref[...] + (1.0 + g_ref[...]) * y_ref[...]
        mu = jnp.mean(z, axis=-1, keepdims=True)
        zc = z - mu
        var = jnp.mean(zc * zc, axis=-1, keepdims=True)
        rstd = lax.rsqrt(var + LN_EPS)
        xh = zc * rstd
        xo = xh * ga_ref[...] + be_ref[...]
        xo_ref[...] = xo
        xh_ref[...] = xh
        rs_ref[...] = rstd
        if emit_h:
            h_ref[...] = (xo * (1.0 + sc_ref[...]) + sh_ref[...]).astype(BF16)

    ins = [x, y, gate, gamma, beta] + ([sc, sh] if emit_h else [])
    in_specs = [_rows(tr, D), _rows(tr, D)] + [_vec(D)] * (len(ins) - 2)
    out_shape = [jax.ShapeDtypeStruct((S, D), F32), jax.ShapeDtypeStruct((S, D), F32), jax.ShapeDtypeStruct((S, 1), F32)]
    out_specs = [_rows(tr, D), _rows(tr, D), _rows(tr, 1)]
    if emit_h:
        out_shape.append(jax.ShapeDtypeStruct((S, D), BF16))
        out_specs.append(_rows(tr, D))
    return _call(body, name=name, grid=(S // tr,), in_specs=in_specs, out_specs=out_specs, out_shape=out_shape,
                 sem=("parallel",), args=ins, comm=comm)


def _loss_head(xf, tgt, name):
    S, D = xf.shape
    tr = min(256, S)

    def body(x_ref, t_ref, dx_ref, l_ref):
        e = x_ref[...] - t_ref[...]
        dx_ref[...] = e * (1.0 / D)

        @pl.when(pl.program_id(0) == 0)
        def _():
            l_ref[...] = jnp.zeros_like(l_ref)

        l_ref[...] += jnp.sum(e * e, axis=0, keepdims=True)

    return pl.pallas_call(
        body, name=name, grid=(S // tr,), in_specs=[_rows(tr, D), _rows(tr, D)],
        out_specs=[_rows(tr, D), _vec(D)],
        out_shape=[jax.ShapeDtypeStruct((S, D), F32), jax.ShapeDtypeStruct((1, D), F32)],
        compiler_params=_params(("arbitrary",)),
    )(xf, tgt)


def _ln_bwd(dxo, xh, rstd, gamma, y, gate, name, pre=None):
    S, D = dxo.shape
    tr = min(256, S)
    n_pre = 0 if pre is None else 3

    def body(dx_ref, xh_ref, rs_ref, ga_ref, y_ref, g_ref, *rest):
        dres_ref, dy_ref, dga_ref, dbe_ref, dg_ref = rest[n_pre:n_pre + 5]
        first = pl.program_id(0) == 0
        dxo_ = dx_ref[...]
        xh_ = xh_ref[...]
        if pre is not None:
            dh_ref, sc_ref, be_ref = rest[:3]
            dsc_ref, dsh_ref = rest[n_pre + 5:]
            dh_ = dh_ref[...]
            dxo_ = dxo_ + dh_ * (1.0 + sc_ref[...])

            @pl.when(first)
            def _():
                dsc_ref[...] = jnp.zeros_like(dsc_ref)
                dsh_ref[...] = jnp.zeros_like(dsh_ref)

            dsc_ref[...] += jnp.sum(dh_ * (xh_ * ga_ref[...] + be_ref[...]), axis=0, keepdims=True)
            dsh_ref[...] += jnp.sum(dh_, axis=0, keepdims=True)
        dxh = dxo_ * ga_ref[...]
        m1 = jnp.mean(dxh, axis=-1, keepdims=True)
        m2 = jnp.mean(dxh * xh_, axis=-1, keepdims=True)
        dz = rs_ref[...] * (dxh - m1 - xh_ * m2)
        dres_ref[...] = DEEPNORM_ALPHA * dz
        dy_ref[...] = ((1.0 + g_ref[...]) * dz).astype(BF16)

        @pl.when(first)
        def _():
            dga_ref[...] = jnp.zeros_like(dga_ref)
            dbe_ref[...] = jnp.zeros_like(dbe_ref)
            dg_ref[...] = jnp.zeros_like(dg_ref)

        dga_ref[...] += jnp.sum(dxo_ * xh_, axis=0, keepdims=True)
        dbe_ref[...] += jnp.sum(dxo_, axis=0, keepdims=True)
        dg_ref[...] += jnp.sum(dz * y_ref[...], axis=0, keepdims=True)

    extra_in = [] if pre is None else [_rows(tr, D), _vec(D), _vec(D)]
    return pl.pallas_call(
        body, name=name, grid=(S // tr,),
        in_specs=[_rows(tr, D), _rows(tr, D), _rows(tr, 1), _vec(D), _rows(tr, D), _vec(D)] + extra_in,
        out_specs=[_rows(tr, D), _rows(tr, D)] + [_vec(D)] * (3 + (0 if pre is None else 2)),
        out_shape=[jax.ShapeDtypeStruct((S, D), F32), jax.ShapeDtypeStruct((S, D), BF16)]
        + [jax.ShapeDtypeStruct((1, D), F32)] * (3 + (0 if pre is None else 2)),
        compiler_params=_params(("arbitrary",)),
    )(dxo, xh, rstd, gamma, y, gate, *(pre or ()))


def _mod_bwd(dh, x, sc, dres, name):
    S, D = x.shape
    tr = min(256, S)

    def body(dh_ref, x_ref, sc_ref, dr_ref, dx_ref, dsc_ref, dsh_ref):
        dh_ = dh_ref[...]
        dx_ref[...] = dr_ref[...] + dh_ * (1.0 + sc_ref[...])

        @pl.when(pl.program_id(0) == 0)
        def _():
            dsc_ref[...] = jnp.zeros_like(dsc_ref)
            dsh_ref[...] = jnp.zeros_like(dsh_ref)

        dsc_ref[...] += jnp.sum(dh_ * x_ref[...], axis=0, keepdims=True)
        dsh_ref[...] += jnp.sum(dh_, axis=0, keepdims=True)

    return pl.pallas_call(
        body, name=name, grid=(S // tr,),
        in_specs=[_rows(tr, D), _rows(tr, D), _vec(D), _rows(tr, D)],
        out_specs=[_rows(tr, D), _vec(D), _vec(D)],
        out_shape=[jax.ShapeDtypeStruct((S, D), F32), jax.ShapeDtypeStruct((1, D), F32), jax.ShapeDtypeStruct((1, D), F32)],
        compiler_params=_params(("arbitrary",)),
    )(dh, x, sc, dres)


def _log_sigmoid(z):
    return jnp.minimum(z, 0.0) - jnp.log(1.0 + jnp.exp(-jnp.abs(z)))


def _fox_gate_fwd(proj, b_f, n_heads, name):
    S, PW = proj.shape
    blk = min(256, S)
    last = PW // LANES - 1

    def body(fl_ref, b_ref, cum_ref):
        r = lax.broadcasted_iota(jnp.int32, (blk, blk), 0)
        c = lax.broadcasted_iota(jnp.int32, (blk, blk), 1)
        tril = (c <= r).astype(F32)
        carry = jnp.zeros((1, LANES), F32)
        for i in range(S // blk):
            lf = _log_sigmoid(fl_ref[i * blk:(i + 1) * blk, :] + b_ref[...])
            cum_ref[i * blk:(i + 1) * blk, :] = jnp.dot(tril, lf, preferred_element_type=F32, precision=HIGHEST) + carry
            carry = carry + jnp.sum(lf, axis=0, keepdims=True)

    return pl.pallas_call(
        body, name=name, grid=(1,),
        in_specs=[pl.BlockSpec((S, LANES), lambda i: (0, last)), pl.BlockSpec((1, LANES), lambda i: (0, 0))],
        out_specs=pl.BlockSpec((S, LANES), lambda i: (0, 0)),
        out_shape=jax.ShapeDtypeStruct((S, LANES), F32), compiler_params=_params(("arbitrary",)),
    )(proj, b_f)


def _fox_gate_bwd(dcum, proj, b_f, n_heads, name):
    S, PW = proj.shape
    blk = min(256, S)
    last = PW // LANES - 1
    nb = S // blk

    def body(dc_ref, fl_ref, b_ref, dfl_ref, db_ref):
        r = lax.broadcasted_iota(jnp.int32, (blk, blk), 0)
        c = lax.broadcasted_iota(jnp.int32, (blk, blk), 1)
        triu = (c >= r).astype(F32)
        lane = lax.broadcasted_iota(jnp.int32, (blk, LANES), 1)
        carry = jnp.zeros((1, LANES), F32)
        dbs = jnp.zeros((1, LANES), F32)
        for i in reversed(range(nb)):
            dc = dc_ref[i * blk:(i + 1) * blk, :]
            dlf = jnp.dot(triu, dc, preferred_element_type=F32, precision=HIGHEST) + carry
            carry = carry + jnp.sum(dc, axis=0, keepdims=True)
            z = fl_ref[i * blk:(i + 1) * blk, :] + b_ref[...]
            e = jnp.exp(-jnp.abs(z))
            sig_neg = jnp.where(z >= 0, e / (1.0 + e), 1.0 / (1.0 + e))
            dfl = jnp.where(lane < n_heads, dlf * sig_neg, 0.0)
            dfl_ref[i * blk:(i + 1) * blk, :] = dfl.astype(BF16)
            dbs = dbs + jnp.sum(dfl, axis=0, keepdims=True)
        db_ref[...] = dbs

    return pl.pallas_call(
        body, name=name, grid=(1,),
        in_specs=[pl.BlockSpec((S, LANES), lambda i: (0, 0)), pl.BlockSpec((S, LANES), lambda i: (0, last)),
                  pl.BlockSpec((1, LANES), lambda i: (0, 0))],
        out_specs=[pl.BlockSpec((S, LANES), lambda i: (0, 0)), pl.BlockSpec((1, LANES), lambda i: (0, 0))],
        out_shape=[jax.ShapeDtypeStruct((S, LANES), BF16), jax.ShapeDtypeStruct((1, LANES), F32)],
        compiler_params=_params(("arbitrary",)),
    )(dcum, proj, b_f)


def _fox_scores(q_ref, kb_ref, cq_ref, ck_ref, qi, tq, scale):
    kk = (qi + 1) * tq
    rows = slice(qi * tq, (qi + 1) * tq)
    qb = q_ref[rows, :].astype(BF16)
    s = lax.dot_general(qb, kb_ref[0:kk, :], _DN["nt"], preferred_element_type=F32) * scale
    s = s + (cq_ref[0, rows, :] - ck_ref[0, :, 0:kk])
    r = lax.broadcasted_iota(jnp.int32, (tq, kk), 0) + qi * tq
    c = lax.broadcasted_iota(jnp.int32, (tq, kk), 1)
    mask = c <= r
    return jnp.where(mask, s, NEG), mask, qb


def _fox_fwd(proj, cq, ck, n_heads, name, comm=None):
    S = proj.shape[0]
    H = n_heads
    tq = min(FOX_TQ, S)
    nq = S // tq
    scale = FDH ** -0.5

    def body(q_ref, k_ref, v_ref, cq_ref, ck_ref, o_ref, lse_ref, kb_ref, vb_ref):
        kb_ref[...] = k_ref[...].astype(BF16)
        vb_ref[...] = v_ref[...].astype(BF16)
        for qi in range(nq):
            kk = (qi + 1) * tq
            rows = slice(qi * tq, (qi + 1) * tq)
            s, _, _ = _fox_scores(q_ref, kb_ref, cq_ref, ck_ref, qi, tq, scale)
            m = jnp.max(s, axis=-1, keepdims=True)
            p = jnp.exp(s - m)
            l = jnp.sum(p, axis=-1, keepdims=True)
            p = p * (1.0 / l)
            o_ref[rows, :] = jnp.dot(p.astype(BF16), vb_ref[0:kk, :], preferred_element_type=F32).astype(BF16)
            lse_ref[0, rows, :] = m + jnp.log(l)

    col = lambda off: pl.BlockSpec((S, FDH), lambda h: (0, h + off))
    stat_c = pl.BlockSpec((1, S, 1), lambda h: (h, 0, 0))
    stat_r = pl.BlockSpec((1, 1, S), lambda h: (h, 0, 0))
    return _call(
        body, name=name, grid=(H,),
        in_specs=[col(0), col(H), col(2 * H), stat_c, stat_r],
        out_specs=[col(0), stat_c],
        out_shape=[jax.ShapeDtypeStruct((S, H * FDH), BF16), jax.ShapeDtypeStruct((H, S, 1), F32)],
        scratch_shapes=[pltpu.VMEM((S, FDH), BF16), pltpu.VMEM((S, FDH), BF16)],
        sem=("parallel",), args=(proj, proj, proj, cq, ck), comm=comm)


def _fox_bwd(proj, cq, ck, lse, do, n_heads, name, comm=None):
    S = proj.shape[0]
    H = n_heads
    tq = min(FOX_TQ, S)
    nq = S // tq
    scale = FDH ** -0.5

    def body(q_ref, k_ref, v_ref, cq_ref, ck_ref, lse_ref, do_ref, dq_ref, dk_ref, dv_ref, dcq_ref, dck_ref,
             kb_ref, vb_ref, dka_ref, dva_ref):
        kb_ref[...] = k_ref[...].astype(BF16)
        vb_ref[...] = v_ref[...].astype(BF16)
        dka_ref[...] = jnp.zeros_like(dka_ref)
        dva_ref[...] = jnp.zeros_like(dva_ref)
        dck_ref[...] = jnp.zeros_like(dck_ref)
        for qi in range(nq):
            kk = (qi + 1) * tq
            rows = slice(qi * tq, (qi + 1) * tq)
            s, mask, qb = _fox_scores(q_ref, kb_ref, cq_ref, ck_ref, qi, tq, scale)
            p = jnp.where(mask, jnp.exp(s - lse_ref[0, rows, :]), 0.0)
            dob = do_ref[rows, :]
            dp = lax.dot_general(dob, vb_ref[0:kk, :], _DN["nt"], preferred_element_type=F32)
            delta = jnp.sum(p * dp, axis=-1, keepdims=True)
            ds = p * (dp - delta)
            dcq_ref[0, rows, :] = jnp.sum(ds, axis=-1, keepdims=True)
            dck_ref[0, :, 0:kk] -= jnp.sum(ds, axis=0, keepdims=True)
            dsb = (ds * scale).astype(BF16)
            dq_ref[rows, :] = jnp.dot(dsb, kb_ref[0:kk, :], preferred_element_type=F32).astype(BF16)
            dka_ref[0:kk, :] += lax.dot_general(dsb, qb, _DN["tn"], preferred_element_type=F32)
            dva_ref[0:kk, :] += lax.dot_general(p.astype(BF16), dob, _DN["tn"], preferred_element_type=F32)
        dk_ref[...] = dka_ref[...].astype(BF16)
        dv_ref[...] = dva_ref[...].astype(BF16)

    col = lambda off: pl.BlockSpec((S, FDH), lambda h: (0, h + off))
    stat_c = pl.BlockSpec((1, S, 1), lambda h: (h, 0, 0))
    stat_r = pl.BlockSpec((1, 1, S), lambda h: (h, 0, 0))
    wide = jax.ShapeDtypeStruct((S, H * FDH), BF16)
    return _call(
        body, name=name, grid=(H,),
        in_specs=[col(0), col(H), col(2 * H), stat_c, stat_r, stat_c, col(0)],
        out_specs=[col(0), col(0), col(0), stat_c, stat_r],
        out_shape=[wide, wide, wide, jax.ShapeDtypeStruct((H, S, 1), F32), jax.ShapeDtypeStruct((H, 1, S), F32)],
        scratch_shapes=[pltpu.VMEM((S, FDH), BF16), pltpu.VMEM((S, FDH), BF16), pltpu.VMEM((S, FDH), F32), pltpu.VMEM((S, FDH), F32)],
        sem=("parallel",), args=(proj, proj, proj, cq, ck, lse, do), comm=comm)


def _rope_tables(pos, sign):
    inv = ROPE_THETA ** (-jnp.arange(0, ROPE_DIM, 2, dtype=F32) / ROPE_DIM)
    ang = pos.astype(F32)[:, None] * inv
    cos, sin = jnp.cos(ang), sign * jnp.sin(ang)
    l64 = jnp.arange(LANES) % SDH
    idx = l64 % (ROPE_DIM // 2)
    c = jnp.where(l64 < ROPE_DIM, cos[:, idx], 1.0)
    sa = jnp.where(l64 < ROPE_DIM // 2, -sin[:, idx], 0.0)
    sb = jnp.where((l64 >= ROPE_DIM // 2) & (l64 < ROPE_DIM), sin[:, idx], 0.0)
    rot = jnp.stack([c, sa, sb])
    ident = jnp.stack([jnp.ones_like(c), jnp.zeros_like(c), jnp.zeros_like(c)])
    return jnp.stack([rot, ident]).astype(F32)


def _rope(xin, tabs, n_rot, out_dtype, name, comm=None):
    S, W = xin.shape

    def body(x_ref, t_ref, o_ref):
        xv = x_ref[...]
        o = xv * t_ref[0, 0] + pltpu.roll(xv, LANES - ROPE_DIM // 2, 1) * t_ref[0, 1] + pltpu.roll(xv, ROPE_DIM // 2, 1) * t_ref[0, 2]
        o_ref[...] = o.astype(out_dtype)

    return _call(
        body, name=name, grid=(W // LANES,),
        in_specs=[pl.BlockSpec((S, LANES), lambda j: (0, j)),
                  pl.BlockSpec((1, 3, S, LANES), lambda j: (jnp.where(j < n_rot, 0, 1), 0, 0, 0))],
        out_specs=[pl.BlockSpec((S, LANES), lambda j: (0, j))],
        out_shape=[jax.ShapeDtypeStruct((S, W), out_dtype)], sem=("parallel",), args=(xin, tabs), comm=comm)[0]


SWA_PER_STEP = 2


def _swa_bias():
    r = jnp.arange(WIN)[:, None]
    c = jnp.arange(2 * WIN)[None, :]
    first = c <= r
    later = (c > r) & (c <= r + WIN)
    return jnp.where(jnp.stack([first, later]), 0.0, NEG).astype(F32)


def _swa_probs(q_ref, k_ref, sk_ref, b_ref, n, j, scale):
    st = pl.multiple_of(jnp.maximum(n - 1, 0) * WIN, WIN)
    qb = q_ref[0, j]
    kb = k_ref[0, pl.ds(st, 2 * WIN), :]
    gm = qb.shape[0]
    s = lax.dot_general(qb, kb, _DN["nt"], preferred_element_type=F32) * scale
    s = (s.reshape(gm // WIN, WIN, 2 * WIN) + b_ref[jnp.minimum(n, 1)][None]).reshape(gm, 2 * WIN)
    sink = sk_ref[0]
    m = jnp.maximum(jnp.max(s, axis=-1, keepdims=True), sink)
    e = jnp.exp(s - m)
    es = jnp.exp(sink - m)
    inv = 1.0 / (jnp.sum(e, axis=-1, keepdims=True) + es)
    return e * inv, es * inv, st, qb, kb


def _swa_specs(S, gm):
    blk = pl.BlockSpec((1, SWA_PER_STEP, gm, SDH), lambda g, n: (g, n, 0, 0))
    kv = pl.BlockSpec((1, S, SDH), lambda g, n: (g, 0, 0))
    col = pl.BlockSpec((1, gm, 1), lambda g, n: (g, 0, 0))
    bias = pl.BlockSpec((2, WIN, 2 * WIN), lambda g, n: (0, 0, 0))
    return blk, kv, col, bias


def _swa_fwd(q, k, v, sinks, name, comm=None):
    KH, nb, gm, _ = q.shape
    S = k.shape[1]
    scale = SDH ** -0.5

    def body(q_ref, k_ref, v_ref, sk_ref, b_ref, o_ref):
        for j in range(SWA_PER_STEP):
            p, _, st, _, _ = _swa_probs(q_ref, k_ref, sk_ref, b_ref, pl.program_id(1) * SWA_PER_STEP + j, j, scale)
            vb = v_ref[0, pl.ds(st, 2 * WIN), :]
            o_ref[0, j] = jnp.dot(p.astype(BF16), vb, preferred_element_type=F32).astype(BF16)

    blk, kv, col, bias = _swa_specs(S, gm)
    return _call(
        body, name=name, grid=(KH, nb // SWA_PER_STEP), in_specs=[blk, kv, kv, col, bias], out_specs=[blk],
        out_shape=[jax.ShapeDtypeStruct(q.shape, BF16)], sem=("parallel", "parallel"), args=(q, k, v, sinks, _swa_bias()), comm=comm)[0]


def _swa_bwd(q, k, v, sinks, do, name, comm=None):
    KH, nb, gm, _ = q.shape
    S = k.shape[1]
    scale = SDH ** -0.5

    def body(q_ref, k_ref, v_ref, sk_ref, b_ref, do_ref, dq_ref, dk_ref, dv_ref, dsk_ref):
        @pl.when(pl.program_id(1) == 0)
        def _():
            dk_ref[...] = jnp.zeros_like(dk_ref)
            dv_ref[...] = jnp.zeros_like(dv_ref)
            dsk_ref[...] = jnp.zeros_like(dsk_ref)

        blocks = []
        for j in range(SWA_PER_STEP):
            p, ps, st, qb, kb = _swa_probs(q_ref, k_ref, sk_ref, b_ref, pl.program_id(1) * SWA_PER_STEP + j, j, scale)
            vb = v_ref[0, pl.ds(st, 2 * WIN), :]
            dob = do_ref[0, j]
            dp = lax.dot_general(dob, vb, _DN["nt"], preferred_element_type=F32)
            delta = jnp.sum(p * dp, axis=-1, keepdims=True)
            dsb = (p * (dp - delta) * scale).astype(BF16)
            dq_ref[0, j] = jnp.dot(dsb, kb, preferred_element_type=F32)
            blocks.append((st, lax.dot_general(dsb, qb, _DN["tn"], preferred_element_type=F32),
                           lax.dot_general(p.astype(BF16), dob, _DN["tn"], preferred_element_type=F32), ps * delta))
        for st, dk, dv, dsk in blocks:
            dk_ref[0, pl.ds(st, 2 * WIN), :] += dk
            dv_ref[0, pl.ds(st, 2 * WIN), :] += dv
            dsk_ref[0] -= dsk

    blk, kv, col, bias = _swa_specs(S, gm)
    return _call(
        body, name=name, grid=(KH, nb // SWA_PER_STEP), in_specs=[blk, kv, kv, col, bias, blk], out_specs=[blk, kv, kv, col],
        out_shape=[jax.ShapeDtypeStruct(q.shape, F32), jax.ShapeDtypeStruct(k.shape, F32),
                   jax.ShapeDtypeStruct(k.shape, F32), jax.ShapeDtypeStruct(sinks.shape, F32)],
        sem=("parallel", "arbitrary"), args=(q, k, v, sinks, _swa_bias(), do), comm=comm)


def _shift_down(u, k):
    row = lax.broadcasted_iota(jnp.int32, u.shape, 0)
    return jnp.where(row >= k, pltpu.roll(u, k, 0), 0.0)


def _shift_up(u, k):
    n = u.shape[0]
    row = lax.broadcasted_iota(jnp.int32, u.shape, 0)
    return jnp.where(row < n - k, pltpu.roll(u, n - k, 0), 0.0)


def _conv3(u, w_ref, b_ref):
    return w_ref[0:1, :] * _shift_down(u, 2) + w_ref[1:2, :] * _shift_down(u, 1) + w_ref[2:3, :] * u + b_ref[...]


def _conv_gate(u, cw, cb, name, comm=None):
    S, F2 = u.shape
    Fh = F2 // 2
    tc = _tile(Fh, 256)
    nf = Fh // tc

    def body(ug_ref, uv_ref, wg_ref, wv_ref, bg_ref, bv_ref, a_ref):
        g = _conv3(ug_ref[...], wg_ref, bg_ref)
        val = _conv3(uv_ref[...], wv_ref, bv_ref)
        a_ref[...] = (g * (1.0 / (1.0 + jnp.exp(-g))) * val).astype(BF16)

    blk = lambda r, off: pl.BlockSpec((r, tc), lambda j: (0, j + off))
    return _call(
        body, name=name, grid=(nf,),
        in_specs=[blk(S, 0), blk(S, nf), blk(3, 0), blk(3, nf), blk(1, 0), blk(1, nf)], out_specs=[blk(S, 0)],
        out_shape=[jax.ShapeDtypeStruct((S, Fh), BF16)], sem=("parallel",), args=(u, u, cw, cw, cb, cb), comm=comm)[0]


def _conv_gate_bwd(u, da, cw, cb, name, comm=None):
    S, F2 = u.shape
    Fh = F2 // 2
    tc = _tile(Fh, 256)
    nf = Fh // tc

    def half(h, dx, uu, w_ref, du_ref, dw_ref, db_ref):
        up1, up2 = _shift_up(dx, 1), _shift_up(dx, 2)
        du = w_ref[2:3, :] * dx + w_ref[1:2, :] * up1 + w_ref[0:1, :] * up2
        du_ref[h] = du.astype(BF16)
        dw_ref[h, 0:1, :] = jnp.sum(up2 * uu, axis=0, keepdims=True)
        dw_ref[h, 1:2, :] = jnp.sum(up1 * uu, axis=0, keepdims=True)
        dw_ref[h, 2:3, :] = jnp.sum(dx * uu, axis=0, keepdims=True)
        db_ref[h] = jnp.sum(dx, axis=0, keepdims=True)

    def body(ug_ref, uv_ref, da_ref, wg_ref, wv_ref, bg_ref, bv_ref, du_ref, dw_ref, db_ref):
        ug = ug_ref[...]
        uv = uv_ref[...]
        g = _conv3(ug, wg_ref, bg_ref)
        val = _conv3(uv, wv_ref, bv_ref)
        sig = 1.0 / (1.0 + jnp.exp(-g))
        da_ = da_ref[...]
        dg = da_ * val * (sig * (1.0 + g * (1.0 - sig)))
        dval = da_ * (g * sig)
        half(0, dg, ug, wg_ref, du_ref, dw_ref, db_ref)
        half(1, dval, uv, wv_ref, du_ref, dw_ref, db_ref)

    blk = lambda r, off: pl.BlockSpec((r, tc), lambda j: (0, j + off))
    both = lambda r: pl.BlockSpec((2, r, tc), lambda j: (0, 0, j))
    return _call(
        body, name=name, grid=(nf,),
        in_specs=[blk(S, 0), blk(S, nf), blk(S, 0), blk(3, 0), blk(3, nf), blk(1, 0), blk(1, nf)],
        out_specs=[both(S), both(3), both(1)],
        out_shape=[jax.ShapeDtypeStruct((2, S, Fh), BF16), jax.ShapeDtypeStruct((2, 3, Fh), F32), jax.ShapeDtypeStruct((2, 1, Fh), F32)],
        sem=("parallel",), args=(u, u, da, cw, cw, cb, cb), comm=comm)


def _to_groups(t, kh):
    S, width = t.shape
    g = width // SDH // kh
    return t.reshape(S // WIN, WIN, kh, g, SDH).transpose(2, 0, 3, 1, 4).reshape(kh, S // WIN, g * WIN, SDH)


def _from_groups(t):
    kh, nb, gm, _ = t.shape
    g = gm // WIN
    return t.reshape(kh, nb, g, WIN, SDH).transpose(1, 3, 0, 2, 4).reshape(nb * WIN, kh * g * SDH)


class LocalWeights:
    def __init__(self, weights):
        self.weights, self.grads = weights, {}

    def w(self, name):
        return self.weights[name]

    def carry(self, stage, last=()):
        return None

    def carried(self, stage, comm):
        pass

    def grad(self, name, g):
        self.grads[name] = g


def _local_step(dm, x, tgt, pos, mod, sp, pp):
    S, D, FH, QH, KH, Fh = dm
    m = [[mod[i:i + 1, j * D:(j + 1) * D] for j in range(6)] for i in range(DEPTH)]

    last = []

    def run(fn, *args, name, **kw):
        comm = pp.carry(name, last)
        args = [pp.w(arg[1]) if isinstance(arg, tuple) and arg[:1] == ("w",) else arg for arg in args]
        out = fn(*args, name=name, comm=comm, **kw)
        if comm is not None:
            pp.carried(name, comm)
        last[:] = list(out) if isinstance(out, (list, tuple)) else [out]
        return out

    sv = []
    xs = x
    h = _modulate(xs, m[0][1], m[0][0], "mod_in")
    last[:] = [h]
    for i in range(DEPTH):
        sh1, sc1, g1, sh2, sc2, g2 = m[i]
        L = {}
        L["x_in"], L["h1"] = xs, h
        if i == 0:
            proj = run(_mm, h[None], ("w", "fox_w_in"), mode="nn", out_dtype=F32, name="fox_proj", tn=896)[0]
            cum = _fox_gate_fwd(proj, sp["fox_b_f"], FH, "fox_gate")
            cq = cum[:, :FH].T[:, :, None]
            ck = cum[:, :FH].T[:, None, :]
            o, lse = run(_fox_fwd, proj, cq, ck, FH, name="fox_attn")
            L.update(proj=proj, cq=cq, ck=ck, lse=lse, o=o)
            y = run(_mm, o[None], ("w", "fox_w_o"), mode="nn", out_dtype=F32, name="fox_out")[0]
        else:
            proj = run(_mm, h[None], ("w", "swa_w_in"), mode="nn", out_dtype=F32, name="swa_proj", tn=640)[0]
            tabs = _rope_tables(pos, 1.0)
            n_rot = (QH + KH) * SDH // LANES
            pr = run(_rope, proj, tabs, n_rot, BF16, name="swa_rope")
            qh = _to_groups(pr[:, :QH * SDH], KH)
            kh = pr[:, QH * SDH:(QH + KH) * SDH].reshape(S, KH, SDH).transpose(1, 0, 2)
            vh = pr[:, (QH + KH) * SDH:].reshape(S, KH, SDH).transpose(1, 0, 2)
            oh = run(_swa_fwd, qh, kh, vh, sp["sinks"], name="swa_attn")
            o = _from_groups(oh)
            L.update(qh=qh, kh=kh, vh=vh, o=o)
            y = run(_mm, o[None], ("w", "swa_w_o"), mode="nn", out_dtype=F32, name="swa_out")[0]
        L["y1"] = y
        x1, L["xh1"], L["rs1"], h2 = run(_ln_fwd, xs, y, g1, sp["ln_mix_g"][i], sp["ln_mix_b"][i], sc2, sh2, name=f"ln_mix{i}")
        L["x1"], L["h2"] = x1, h2
        u = run(_mm, h2[None], ("w", f"ffn_w_up{i}"), mode="nn", out_dtype=F32, name=f"ffn_up{i}", tm=512, tn=1408)[0]
        a = run(_conv_gate, u, sp["conv_w"][i], sp["conv_b"][i], name=f"ffn_gate{i}")
        y2 = run(_mm, a[None], ("w", f"ffn_w_down{i}"), mode="nn", out_dtype=F32, name=f"ffn_down{i}", tk=5632, tm=512)[0]
        L.update(u=u, a=a, y2=y2)
        if i + 1 < DEPTH:
            xs, L["xh2"], L["rs2"], h = run(_ln_fwd, x1, y2, g2, sp["ln_ffn_g"][i], sp["ln_ffn_b"][i], m[i + 1][1], m[i + 1][0], name=f"ln_ffn{i}")
        else:
            xs, L["xh2"], L["rs2"] = run(_ln_fwd, x1, y2, g2, sp["ln_ffn_g"][i], sp["ln_ffn_b"][i], None, None, name=f"ln_ffn{i}")
        sv.append(L)

    dx, loss_cols = _loss_head(xs, tgt, "loss_head")

    gs = {k: [None] * DEPTH for k in ("conv_w", "conv_b", "ln_mix_g", "ln_mix_b", "ln_ffn_g", "ln_ffn_b")}
    dmp = [dict() for _ in range(DEPTH)]
    dres, pend = dx, None
    for i in reversed(range(DEPTH)):
        sh1, sc1, g1, sh2, sc2, g2 = m[i]
        L = sv[i]
        res = _ln_bwd(dres, L["xh2"], L["rs2"], sp["ln_ffn_g"][i], L["y2"], g2, f"ln_ffn_bwd{i}",
                      None if pend is None else (*pend, sp["ln_ffn_b"][i]))
        dres, dy, gs["ln_ffn_g"][i], gs["ln_ffn_b"][i], dmp[i]["g2"] = res[:5]
        if pend is not None:
            dmp[i + 1]["sc1"], dmp[i + 1]["sh1"] = res[5:]
        da = run(_mm, dy[None], pp.w(f"ffn_w_down{i}"), mode="nt", out_dtype=F32, name=f"ffn_down_dx{i}", tm=512, tn=1408)[0]
        pp.grad(f"ffn_w_down{i}", run(_mm, L["a"][None], dy[None], mode="tn", out_dtype=BF16, name=f"ffn_down_dw{i}", tm=1408))
        du, dcw, dcb = run(_conv_gate_bwd, L["u"], da, sp["conv_w"][i], sp["conv_b"][i], name=f"ffn_gate_bwd{i}")
        gs["conv_w"][i] = dcw.transpose(1, 0, 2).reshape(3, 2 * Fh)
        gs["conv_b"][i] = dcb.transpose(1, 0, 2).reshape(1, 2 * Fh)
        dh2 = run(_mm, du, pp.w(f"ffn_w_up{i}"), mode="nt", out_dtype=F32, name=f"ffn_up_dx{i}", tk=2816)[0]
        pp.grad(f"ffn_w_up{i}", run(_mm, L["h2"][None], du, mode="tn", out_dtype=BF16, name=f"ffn_up_dw{i}", out_groups=N_CHIPS, tn=1408))
        dres, dy, gs["ln_mix_g"][i], gs["ln_mix_b"][i], dmp[i]["g1"], dmp[i]["sc2"], dmp[i]["sh2"] = _ln_bwd(
            dres, L["xh1"], L["rs1"], sp["ln_mix_g"][i], L["y1"], g1, f"ln_mix_bwd{i}", (dh2, sc2, sp["ln_mix_b"][i]))
        if i == 0:
            do = run(_mm, dy[None], pp.w("fox_w_o"), mode="nt", out_dtype=BF16, name="fox_out_dx")[0]
            pp.grad("fox_w_o", run(_mm, L["o"][None], dy[None], mode="tn", out_dtype=BF16, name="fox_out_dw"))
            dq, dk, dv, dcq, dck = run(_fox_bwd, L["proj"], L["cq"], L["ck"], L["lse"], do, FH, name="fox_attn_bwd")
            dcum = dcq[:, :, 0].T + dck[:, 0, :].T
            dcum = jnp.pad(dcum, ((0, 0), (0, LANES - FH)))
            dfl, db_f = _fox_gate_bwd(dcum, L["proj"], sp["fox_b_f"], FH, "fox_gate_bwd")
            gs["fox_b_f"] = db_f
            dproj = jnp.concatenate([dq, dk, dv, dfl], axis=1)
            pp.grad("fox_w_in", run(_mm, L["h1"][None], dproj[None], mode="tn", out_dtype=BF16, name="fox_proj_dw", tn=896))
            dh1 = run(_mm, dproj[None], pp.w("fox_w_in"), mode="nt", out_dtype=F32, name="fox_proj_dx", tk=6272, tm=512)[0]
        else:
            do = run(_mm, dy[None], pp.w("swa_w_o"), mode="nt", out_dtype=BF16, name="swa_out_dx")[0]
            pp.grad("swa_w_o", run(_mm, L["o"][None], dy[None], mode="tn", out_dtype=BF16, name="swa_out_dw"))
            dqh, dkh, dvh, dsk = run(_swa_bwd, L["qh"], L["kh"], L["vh"], sp["sinks"], _to_groups(do, KH), name="swa_attn_bwd")
            gs["sinks"] = jnp.sum(dsk.reshape(QH, WIN), axis=1)
            dpr = jnp.concatenate([_from_groups(dqh), dkh.transpose(1, 0, 2).reshape(S, KH * SDH),
                                   dvh.transpose(1, 0, 2).reshape(S, KH * SDH)], axis=1)
            n_rot = (QH + KH) * SDH // LANES
            dproj = _rope(dpr, _rope_tables(pos, -1.0), n_rot, BF16, "swa_rope_bwd")
            dh1 = run(_mm, dproj[None], pp.w("swa_w_in"), mode="nt", out_dtype=F32, name="swa_proj_dx", tk=640)[0]
            pp.grad("swa_w_in", run(_mm, L["h1"][None], dproj[None], mode="tn", out_dtype=BF16, name="swa_proj_dw", out_groups=N_CHIPS, tn=640))
        pend = (dh1, sc1)
    grad_x, dmp[0]["sc1"], dmp[0]["sh1"] = _mod_bwd(pend[0], sv[0]["x_in"], pend[1], dres, "mod_mix_bwd0")
    dmod = [jnp.concatenate([p["sh1"], p["sc1"], p["g1"], p["sh2"], p["sc2"], p["g2"]], axis=1) for p in dmp]
    return loss_cols, grad_x, gs, jnp.concatenate(dmod, axis=0)


def _allgather_small(v, name):
    m_per, n = v.shape

    def body(x_ref, out_ref, send_sems, recv_sems, local_sem):
        x, y, c, chips = _place()
        me, sibling = (x, y, c), (x, y, 1 - c)

        def rows(px, py, pc):
            return out_ref.at[pl.ds((4 * px + 2 * py + pc) * m_per, m_per), :]

        def copy(k, block, to, src=None):
            return _remote(rows(*block) if src is None else src, rows(*block), send_sems.at[k], recv_sems.at[k], to)

        mine = pltpu.make_async_copy(x_ref, rows(*me), local_sem)
        mine.start()
        first = [copy(0, me, sibling, src=x_ref)]
        first += [copy(1 + j, me, (*chip, c), src=x_ref) for j, chip in enumerate(chips)]
        for cp in first:
            cp.start()
        passed = [copy(4 + j, (*chip, c), sibling) for j, chip in enumerate(chips)]
        for j, chip in enumerate(chips):
            copy(1 + j, (*chip, c), me).wait_recv()
            passed[j].start()
        copy(0, sibling, me).wait_recv()
        for j, chip in enumerate(chips):
            copy(4 + j, (*chip, 1 - c), me).wait_recv()
        for cp in first + passed:
            cp.wait_send()
        mine.wait()

    return pl.pallas_call(
        body, name=name, out_shape=jax.ShapeDtypeStruct((N_DEV * m_per, n), v.dtype),
        in_specs=[pl.BlockSpec(memory_space=pltpu.VMEM)], out_specs=pl.BlockSpec(memory_space=pltpu.VMEM),
        scratch_shapes=[pltpu.SemaphoreType.DMA((7,)), pltpu.SemaphoreType.DMA((7,)), pltpu.SemaphoreType.DMA],
        compiler_params=pltpu.CompilerParams(vmem_limit_bytes=VMEM_LIMIT),
    )(v)


def _row_tile(r, pref=256):
    return _tile(r, pref, 16)


def _cast_bf16(w, layer, chip, name, after=()):
    _, R, C = w.shape
    tr = _row_tile(R)

    def body(s_ref, w_ref, *rest):
        rest[-1][...] = w_ref[...].astype(BF16)

    return pl.pallas_call(
        body, name=name,
        grid_spec=pltpu.PrefetchScalarGridSpec(
            num_scalar_prefetch=1, grid=(R // tr,),
            in_specs=[pl.BlockSpec((None, tr, C), lambda i, s: (layer, i, 0))] + _any_specs(len(after)),
            out_specs=pl.BlockSpec((None, tr, C), lambda i, s: (s[0], i, 0))),
        out_shape=jax.ShapeDtypeStruct((N_CHIPS, R, C), BF16), compiler_params=_params(("parallel",)),
    )(jnp.reshape(chip, (1,)).astype(jnp.int32), w, *after)


def _add_sibling(g, got, c, name):
    G, R, C = g.shape
    rh = R // 2
    tr = _row_tile(rh)
    nb = rh // tr

    def body(c_ref, g_ref, o_ref, p_ref):
        p_ref[...] = (g_ref[...].astype(F32) + o_ref[...].astype(F32)).astype(BF16)

    return pl.pallas_call(
        body, name=name,
        grid_spec=pltpu.PrefetchScalarGridSpec(
            num_scalar_prefetch=1, grid=(G, nb),
            in_specs=[pl.BlockSpec((1, tr, C), lambda s, i, c_ref: (s, c_ref[0] * nb + i, 0)),
                      pl.BlockSpec((1, tr, C), lambda s, i, c_ref: (s, i, 0))],
            out_specs=pl.BlockSpec((1, tr, C), lambda s, i, c_ref: (s, i, 0))),
        out_shape=jax.ShapeDtypeStruct((G, rh, C), BF16), compiler_params=_params(("parallel", "parallel")),
    )(jnp.reshape(c, (1,)).astype(jnp.int32), g, got)


def _sum_chips(part, landed, chip, c, name):
    G, rh, C = part.shape
    tr = _row_tile(rh)
    nb = rh // tr

    def body(p_ref, own_ref, *rest):
        acc = own_ref[...].astype(F32)
        for ref in rest[:G - 1]:
            acc = acc + ref[...].astype(F32)
        rest[G - 1][...] = acc

    slot = lambda k: pl.BlockSpec((None, tr, C), lambda i, p: ((p[0] + k) % G, i, 0))
    return pl.pallas_call(
        body, name=name,
        grid_spec=pltpu.PrefetchScalarGridSpec(
            num_scalar_prefetch=1, grid=(nb,), in_specs=[slot(k) for k in range(G)],
            out_specs=pl.BlockSpec((tr, C), lambda i, p: (p[1] * nb + i, 0))),
        out_shape=jax.ShapeDtypeStruct((2 * rh, C), F32), compiler_params=_params(("parallel",)),
    )(jnp.stack([chip, c]).astype(jnp.int32), part, *([landed] * (G - 1)))


def _adam_math(w, g, m, v):
    m = ADAM_B1 * m + (1.0 - ADAM_B1) * g
    v = ADAM_B2 * v + (1.0 - ADAM_B2) * (g * g)
    m_hat = m / (1.0 - ADAM_B1 ** ADAM_STEP)
    v_hat = v / (1.0 - ADAM_B2 ** ADAM_STEP)
    delta = -ADAM_LR * (m_hat / (jnp.sqrt(v_hat) + ADAM_EPS) + ADAM_WD * w)
    return delta, m, v


def _adamw(w, g, m, v, layer, prev, name, by_cols=False, after=()):
    L, R, C = w.shape
    tr = R if by_cols else _tile(R, 128, 8)
    tc = _tile(C, 256) if by_cols else C
    n_alias = len(prev)
    prev = tuple(prev) + tuple(after)
    n_prev = len(prev)

    def body(w_ref, g_ref, m_ref, v_ref, *rest):
        go_ref, d_ref, mo_ref, vo_ref = rest[n_prev:]
        gv = g_ref[...]
        go_ref[...] = gv
        d_ref[...], mo_ref[...], vo_ref[...] = _adam_math(w_ref[...], gv, m_ref[...], v_ref[...])

    lay = pl.BlockSpec((None, tr, tc), lambda i: (layer, i // (C // tc), i % (C // tc)))
    flat = pl.BlockSpec((tr, tc), lambda i: (i // (C // tc), i % (C // tc)))
    return _call(
        body, name=name, grid=((R // tr) * (C // tc),), in_specs=[lay, flat, lay, lay] + _any_specs(n_prev), out_specs=[lay] * 4,
        out_shape=[jax.ShapeDtypeStruct((L, R, C), F32)] * 4, aliases={4 + k: k for k in range(n_alias)},
        sem=("parallel",), args=(w, g, m, v, *prev))


def _cond_rows(c_row, cw, name):
    D = c_row.shape[1]
    nr, fc = cw.shape

    def body(c_ref, e_ref, o_ref):
        o_ref[...] = jnp.zeros_like(o_ref)
        cv = c_ref[...]
        o_ref[0:1, 0:D] = cv * (1.0 / (1.0 + jnp.exp(-cv)))
        o_ref[8:8 + nr, 0:fc] = e_ref[...]

    return pl.pallas_call(body, name=name, out_shape=jax.ShapeDtypeStruct((16, max(D, fc)), F32))(c_row, cw)


def _ada_fwd(cact, ada_w, ada_b, layer, chip, name):
    _, D, NC = ada_w.shape
    tn = _tile(NC, 1024)
    nj = NC // tn

    def body(idx_ref, c_ref, w_ref, b_ref, o_ref):
        acc = jnp.dot(c_ref[...].astype(BF16), w_ref[0].astype(BF16), preferred_element_type=F32)
        o_ref[...] = acc + b_ref[pl.ds(idx_ref[0], 1), :]

    return pl.pallas_call(
        body, name=name,
        grid_spec=pltpu.PrefetchScalarGridSpec(
            num_scalar_prefetch=1, grid=(nj,),
            in_specs=[pl.BlockSpec((8, D), lambda j, idx: (0, 0)),
                      pl.BlockSpec((1, D, tn), lambda j, idx: (idx[0], 0, j)),
                      pl.BlockSpec((DEPTH, tn), lambda j, idx: (0, idx[1] * nj + j))],
            out_specs=pl.BlockSpec((8, tn), lambda j, idx: (0, j))),
        out_shape=jax.ShapeDtypeStruct((8, NC), F32), compiler_params=_params(("parallel",)),
    )(jnp.stack([layer, chip]).astype(jnp.int32), cact, ada_w, ada_b)


def _ada_grad_adamw(cact_t, dmod, w, m, v, name):
    L, D, NC = w.shape
    tr = _tile(D, 128, 8)

    def body(c_ref, d_ref, w_ref, m_ref, v_ref, g_ref, dl_ref, mo_ref, vo_ref):
        g = jnp.dot(c_ref[...], d_ref[...], preferred_element_type=F32, precision=HIGHEST)
        g_ref[...] = g
        dl_ref[...], mo_ref[...], vo_ref[...] = _adam_math(w_ref[...], g, m_ref[...], v_ref[...])

    lay = pl.BlockSpec((None, tr, NC), lambda l, i: (l, i, 0))
    return _call(
        body, name=name, grid=(L, D // tr),
        in_specs=[pl.BlockSpec((tr, N_DEV), lambda l, i: (i, 0)), pl.BlockSpec((None, N_DEV, NC), lambda l, i: (l, 0, 0)), lay, lay, lay],
        out_specs=[lay] * 4, out_shape=[jax.ShapeDtypeStruct((L, D, NC), F32)] * 4,
        sem=("parallel", "parallel"), args=(cact_t, dmod, w, m, v))


def _sum_devices(gathered, name):
    n, R, C = gathered.shape

    def body(g_ref, o_ref):
        acc = g_ref[0]
        for j in range(1, n):
            acc = acc + g_ref[j]
        o_ref[...] = acc

    return pl.pallas_call(body, name=name, out_shape=jax.ShapeDtypeStruct((R, C), F32),
                          compiler_params=pltpu.CompilerParams(vmem_limit_bytes=VMEM_LIMIT))(gathered)


def _adamw_small(w, g, m, v, name):
    def body(w_ref, g_ref, m_ref, v_ref, d_ref, mo_ref, vo_ref):
        d_ref[...], mo_ref[...], vo_ref[...] = _adam_math(w_ref[...], g_ref[...], m_ref[...], v_ref[...])

    return pl.pallas_call(body, name=name, out_shape=[jax.ShapeDtypeStruct(w.shape, F32)] * 3)(w, g, m, v)


def _pad_rows(flat, unit=8 * LANES):
    n = flat.shape[0]
    total = -(-n // unit) * unit
    return jnp.pad(flat, (0, total - n)).reshape(total // LANES, LANES)


def _pad_lanes(v2d):
    return jnp.pad(v2d.reshape(1, -1), ((0, 0), (0, LANES - v2d.size)))


FORWARD = {
    "fox_proj": ((), ("ffn_w_up0",)),
    "ffn_up0": (("ffn_w_up0",), ("ffn_w_down0", "swa_w_in")),
    "ffn_down0": (("ffn_w_down0", "swa_w_in"), ("swa_w_o", "ffn_w_up1")),
    "swa_out": (("swa_w_o", "ffn_w_up1"), ("ffn_w_down1",)),
    "ffn_down1": (("ffn_w_down1",), ()),
}
PLAN = {
    "ffn_gate_bwd1": [("swap", "ffn_w_down1")],
    "ffn_up_dx1": [("scatter", "ffn_w_down1", 0, 1, 1)],
    "swa_out_dx": [("swap", "ffn_w_up1")],
    "swa_attn_bwd": [("scatter", "ffn_w_up1", 0, 6, 8), ("swap", "swa_w_o")],
    "swa_proj_dx": [("scatter", "swa_w_o", 0, 1, 1)],
    "ffn_down_dx0": [("scatter", "ffn_w_up1", 6, 8, 8), ("swap", "swa_w_in")],
    "ffn_down_dw0": [("scatter", "swa_w_in", 0, 1, 1)],
    "ffn_gate_bwd0": [("swap", "ffn_w_down0")],
    "ffn_up_dx0": [("scatter", "ffn_w_down0", 0, 1, 1)],
    "fox_out_dx": [("swap", "ffn_w_up0")],
    "fox_attn_bwd": [("scatter", "ffn_w_up0", 0, 5, 8), ("swap", "fox_w_o")],
    "fox_proj_dw": [("scatter", "fox_w_o", 0, 1, 1), ("scatter", "ffn_w_up0", 5, 6, 8)],
    "fox_proj_dx": [("scatter", "ffn_w_up0", 6, 8, 8), ("swap", "fox_w_in")],
}


class Exchanges:
    def __init__(self, dm, slots, chip, c):
        self.dm, self.slots, self.chip, self.c = dm, dict(slots), chip, c
        self.raw, self.part, self.landed, self.grads, self.views, self.pending = {}, {}, {}, {}, {}, {}

    def gather_start(self, keys, name, after):
        self.first = (keys, _gather_comm([self.slots[k] for k in keys]))
        self.first_state, token = _split_start(self.first[1], name + "_start", after)
        return token

    def gather_finish(self, after, name):
        keys, comm = self.first
        _split_wait(comm, self.first_state, after, name + "_wait")
        pass_on = _forward_comm(comm.results)
        _run_comm(pass_on, name + "_pass")
        self.slots.update(zip(keys, pass_on.results))
        self.fence = list(pass_on.results)

    def before(self, stage, last):
        need, nxt = FORWARD[stage]
        if need:
            self.gather_finish(list(last), "gather_" + "_".join(need))
        if nxt:
            return self.gather_start(list(nxt), "gather_" + "_".join(nxt), list(last) + self.fence)
        return None

    def w(self, key):
        if key not in self.views:
            S, D, FH, QH, KH, Fh = self.dm
            full = self.slots[key]
            if key == "fox_w_in":
                cols = full.shape[2]
                full = jnp.pad(full.transpose(1, 0, 2).reshape(D, N_CHIPS * cols), ((0, 0), (0, 3 * D + LANES - N_CHIPS * cols)))[None]
            elif key in ("fox_w_o", "swa_w_o"):
                full = full.reshape(1, D, D)
            elif key.startswith("ffn_w_down"):
                full = full.reshape(1, Fh, D)
            self.views[key] = full
        return self.views[key]

    def carry(self, stage, last=()):
        todo = []
        token = self.before(stage, last) if stage in FORWARD else None
        if token is not None:
            todo.append(("order", [], Comm([token], [], {}, 1, lambda *refs: None, lambda *refs: None)))
        for kind, key, *chunk in PLAN.get(stage, ()):
            if kind == "swap":
                todo.append((kind, [key], _swap_comm([self.raw[key]])))
            elif kind == "scatter":
                todo.append((kind, [(key, *chunk)], _scatter_comm([self.part[key]], [self.landed.get(key)], [tuple(chunk)])))
        self.pending[stage] = todo
        return _merge([cm for _, _, cm in todo])

    def carried(self, stage, comm):
        for kind, keys, cm in self.pending.pop(stage):
            if kind == "swap":
                self.part[keys[0]] = _add_sibling(self.raw[keys[0]], cm.results[0], self.c, f"add_sibling_{keys[0]}")
            elif kind == "scatter":
                self.landed[keys[0][0]] = cm.results[0]

    def grad(self, key, g):
        S, D, FH, QH, KH, Fh = self.dm
        if key == "fox_w_in":
            cols = self.slots[key].shape[2]
            g = g[0][:, :N_CHIPS * cols].reshape(D, N_CHIPS, cols).transpose(1, 0, 2)
        elif key in ("fox_w_o", "swa_w_o"):
            g = g.reshape(N_CHIPS, D // N_CHIPS, D)
        elif key.startswith("ffn_w_down"):
            g = g.reshape(N_CHIPS, Fh // N_CHIPS, D)
        self.raw[key] = g

    def last_start(self, last, after):
        part = self.part[last]
        self.last = (last, _scatter_comm([part], [lax.empty(part.shape, part.dtype)], [(0, 1, 1)]))
        self.last_state, token = _split_start(self.last[1], "grads_last_start", after)
        return token

    def join_landed(self):
        keys = list(self.landed)
        join = _join_comm([_sum_chips(self.part[k], self.landed[k], self.chip, self.c, f"sum_chips_{k}") for k in keys])
        _run_comm(join, "grads_join")
        return dict(zip(keys, join.results))

    def last_finish(self, after):
        last, comm = self.last
        _split_wait(comm, self.last_state, after, "grads_last_wait")
        join = _join_comm([_sum_chips(self.part[last], comm.results[0], self.chip, self.c, f"sum_chips_{last}")])
        _run_comm(join, "grads_join_last")
        return join.results[0]


def _step(dm, a):
    S, D, FH, QH, KH, Fh = dm
    ix, iy, ic = lax.axis_index("x"), lax.axis_index("y"), lax.axis_index("c")
    chip = 2 * ix + iy
    dev = 2 * chip + ic
    F2c = a["ffn_w_up"].shape[2]
    NC = a["ada_w"].shape[2]

    names = ["fox_w_in", "fox_w_o", "swa_w_in", "swa_w_o", "ffn_w_up", "ffn_w_up", "ffn_w_down", "ffn_w_down"]
    layers = [0, 0, 0, 0, 0, 1, 0, 1]
    keys = ["fox_w_in", "fox_w_o", "swa_w_in", "swa_w_o", "ffn_w_up0", "ffn_w_up1", "ffn_w_down0", "ffn_w_down1"]
    cast = lambda t, after: _cast_bf16(a[names[t]], layers[t], chip, f"cast_{keys[t]}", after)
    pp = Exchanges(dm, {keys[t]: cast(t, ()) for t in (0, 1)}, chip, ic)

    e0 = _cond_rows(a["c"], a["ffn_conv_w"].reshape(DEPTH * 3, F2c), "silu_c")
    g0 = _allgather_small(e0, "gather_cond").reshape(N_DEV, 16, e0.shape[1])
    cact = g0[:, 0, :D]
    conv_w = g0[0::2, 8:8 + DEPTH * 3, :F2c].transpose(1, 0, 2).reshape(DEPTH, 3, N_CHIPS * F2c)
    rows = _ada_fwd(cact, a["ada_w"], a["ada_b"], ic, chip, "ada_proj")
    g1 = _allgather_small(rows, "gather_mod").reshape(N_CHIPS, DEPTH, 8, NC)
    mod = lax.dynamic_index_in_dim(g1, dev, axis=2, keepdims=False).transpose(1, 0, 2).reshape(DEPTH, N_CHIPS * NC)

    token = pp.gather_start(keys[:2], "gather_fox", [mod])
    pp.slots.update({keys[t]: cast(t, (token,)) for t in range(2, len(keys))})
    pp.gather_finish([pp.slots[k] for k in keys[2:]], "gather_fox")
    sp = {"fox_b_f": _pad_lanes(a["fox_b_f"]), "sinks": jnp.repeat(a["swa_sinks"].reshape(KH, QH // KH), WIN, axis=1)[:, :, None],
          "conv_w": [conv_w[i] for i in range(DEPTH)], "conv_b": [a["ffn_conv_b"][i:i + 1] for i in range(DEPTH)]}
    for nm in ("ln_mix_g", "ln_mix_b", "ln_ffn_g", "ln_ffn_b"):
        sp[nm] = [a[nm][i:i + 1] for i in range(DEPTH)]

    loss_cols, grad_x, gs, dmod = _local_step(dm, a["x"][0], a["loss_target"][0], a["positions"][0], mod, sp, pp)
    loss = lax.psum(0.5 / D * jnp.sum(loss_cols), ("x", "y", "c"))
    out = {"loss": loss, "grad_x": grad_x[None]}

    pieces = [dmod.reshape(-1), gs["fox_b_f"].reshape(-1), _pad_lanes(gs["sinks"]).reshape(-1),
              jnp.stack(gs["conv_w"]).reshape(-1), jnp.stack(gs["conv_b"]).reshape(-1)]
    pieces += [jnp.stack(gs[nm]).reshape(-1) for nm in ("ln_mix_g", "ln_mix_b", "ln_ffn_g", "ln_ffn_b")]
    sizes = [p.shape[0] for p in pieces]
    packed = _pad_rows(jnp.concatenate(pieces))
    allp = _allgather_small(packed, "gather_small").reshape(N_DEV, packed.shape[0], LANES)
    tot = _sum_devices(allp, "sum_small").reshape(-1)
    offs = [sum(sizes[:k]) for k in range(len(sizes))]
    take = lambda k: tot[offs[k]:offs[k] + sizes[k]]
    g_small = {"ada_b": take(0).reshape(DEPTH, -1), "fox_b_f": take(1)[:FH].reshape(1, FH), "swa_sinks": take(2)[:QH].reshape(1, QH),
               "ffn_conv_w": lax.dynamic_slice_in_dim(take(3).reshape(DEPTH, 3, N_CHIPS * F2c), chip * F2c, F2c, axis=2),
               "ffn_conv_b": take(4).reshape(DEPTH, -1)}
    for k, nm in enumerate(("ln_mix_g", "ln_mix_b", "ln_ffn_g", "ln_ffn_b")):
        g_small[nm] = take(5 + k).reshape(DEPTH, D)
    small = list(g_small)
    pack = lambda pre: _pad_rows(jnp.concatenate([(a[pre + nm] if pre else a[nm]).reshape(-1) for nm in small]))
    gp = _pad_rows(jnp.concatenate([g_small[nm].reshape(-1) for nm in small]))
    ds_, ms_, vs_ = _adamw_small(pack(""), gp, pack("m_"), pack("v_"), "adamw_small")
    off = 0
    for nm in small:
        n_el = a[nm].size
        out["grad_" + nm] = g_small[nm]
        for pre, arr in (("delta_", ds_), ("new_m_", ms_), ("new_v_", vs_)):
            out[pre + nm] = arr.reshape(-1)[off:off + n_el].reshape(a[nm].shape)
        off += n_el

    dmod_all = allp.reshape(N_DEV, -1)[:, :DEPTH * N_CHIPS * NC].reshape(N_DEV, DEPTH, N_CHIPS * NC)
    dmod_mine = lax.dynamic_slice_in_dim(dmod_all, chip * NC, NC, axis=2).transpose(1, 0, 2)

    grads = pp.join_landed()
    token = pp.last_start("fox_w_in", [ds_, dmod_mine] + list(grads.values()))
    ada = _ada_grad_adamw(cact.T, dmod_mine + token[0, 0], a["ada_w"], a["m_ada_w"], a["v_ada_w"], "ada_grad")
    for pre, arr in zip(("grad_", "delta_", "new_m_", "new_v_"), ada):
        out[pre + "ada_w"] = arr
    upd = {}
    for k, nm, l in zip(keys[1:], names[1:], layers[1:]):
        upd[nm] = _adamw(a[nm], grads[k], a["m_" + nm], a["v_" + nm], l, upd.get(nm, ()), f"adamw_{k}", after=(token,))
    g_last = pp.last_finish([ada[1]] + [res[1] for res in upd.values()])
    tview = lambda t: jnp.swapaxes(t, 1, 2)
    res = _adamw(tview(a["fox_w_in"]), g_last.T, tview(a["m_fox_w_in"]), tview(a["v_fox_w_in"]), 0, (), "adamw_fox_w_in", by_cols=True)
    upd["fox_w_in"] = [tview(r) for r in res]
    for nm, res in upd.items():
        for pre, arr in zip(("grad_", "delta_", "new_m_", "new_v_"), res):
            out[pre + nm] = arr
    return out


_WEIGHTS = ["fox_w_in", "fox_b_f", "fox_w_o", "swa_w_in", "swa_sinks", "swa_w_o", "ada_w", "ada_b", "ffn_w_up", "ffn_conv_w",
            "ffn_conv_b", "ffn_w_down", "ln_mix_g", "ln_mix_b", "ln_ffn_g", "ln_ffn_b"]
_INPUTS = (["x", "c", "positions"] + _WEIGHTS + ["loss_target"] + ["m_" + w for w in _WEIGHTS] + ["v_" + w for w in _WEIGHTS])


def kernel(x, c, positions, fox_w_in, fox_b_f, fox_w_o, swa_w_in, swa_sinks, swa_w_o, ada_w, ada_b, ffn_w_up, ffn_conv_w, ffn_conv_b, ffn_w_down, ln_mix_g, ln_mix_b, ln_ffn_g, ln_ffn_b, loss_target, m_fox_w_in, m_fox_b_f, m_fox_w_o, m_swa_w_in, m_swa_sinks, m_swa_w_o, m_ada_w, m_ada_b, m_ffn_w_up, m_ffn_conv_w, m_ffn_conv_b, m_ffn_w_down, m_ln_mix_g, m_ln_mix_b, m_ln_ffn_g, m_ln_ffn_b, v_fox_w_in, v_fox_b_f, v_fox_w_o, v_swa_w_in, v_swa_sinks, v_swa_w_o, v_ada_w, v_ada_b, v_ffn_w_up, v_ffn_conv_w, v_ffn_conv_b, v_ffn_w_down, v_ln_mix_g, v_ln_mix_b, v_ln_ffn_g, v_ln_ffn_b):
    args = (x, c, positions, fox_w_in, fox_b_f, fox_w_o, swa_w_in, swa_sinks, swa_w_o, ada_w, ada_b, ffn_w_up, ffn_conv_w, ffn_conv_b, ffn_w_down, ln_mix_g, ln_mix_b, ln_ffn_g, ln_ffn_b, loss_target, m_fox_w_in, m_fox_b_f, m_fox_w_o, m_swa_w_in, m_swa_sinks, m_swa_w_o, m_ada_w, m_ada_b, m_ffn_w_up, m_ffn_conv_w, m_ffn_conv_b, m_ffn_w_down, m_ln_mix_g, m_ln_mix_b, m_ln_ffn_g, m_ln_ffn_b, v_fox_w_in, v_fox_b_f, v_fox_w_o, v_swa_w_in, v_swa_sinks, v_swa_w_o, v_ada_w, v_ada_b, v_ffn_w_up, v_ffn_conv_w, v_ffn_conv_b, v_ffn_w_down, v_ln_mix_g, v_ln_mix_b, v_ln_ffn_g, v_ln_ffn_b)
    out = _step(PROD, dict(zip(_INPUTS, args)))
    order = ["loss", "grad_x"] + [p + w for p in ("grad_", "delta_", "new_m_", "new_v_") for w in _WEIGHTS]
    return tuple(out[k] for k in order)
```

```python
import functools
from typing import NamedTuple

import jax
import jax.numpy as jnp
from jax import lax
from jax.experimental import pallas as pl
from jax.experimental.pallas import tpu as pltpu

F32 = jnp.float32
BF16 = jnp.bfloat16
MESH = pl.DeviceIdType.MESH
HIGHEST = lax.Precision.HIGHEST

N_CHIPS = 4
N_DEV = 8
LANES = 128
VMEM_LIMIT = 56 * 1024 * 1024

DEPTH = 2
DEEPNORM_ALPHA = (2.0 * DEPTH) ** 0.25
LN_EPS = 1e-5
ROPE_THETA = 500000.0
ADAM_LR, ADAM_B1, ADAM_B2, ADAM_EPS, ADAM_WD, ADAM_STEP = 0.001, 0.9, 0.999, 1e-08, 0.01, 10
NEG = -1e30


class Dims(NamedTuple):
    S: int
    D: int
    FH: int
    QH: int
    KH: int
    F: int


PROD = Dims(S=2048, D=2048, FH=16, QH=32, KH=4, F=5632)
FDH = 128
SDH = 64
WIN = 128
ROPE_DIM = 16
FOX_TQ = 256


def _params(sem=None, vmem=VMEM_LIMIT):
    return pltpu.CompilerParams(dimension_semantics=sem, vmem_limit_bytes=vmem)


def _tile(n, pref, unit=LANES):
    if n <= pref:
        return n
    t = (pref // unit) * unit
    while t > 0:
        if n % t == 0:
            return t
        t -= unit
    return n


class Comm:
    def __init__(self, args, out_shapes, aliases, n_sem, start, finish, members=()):
        self.args, self.out_shapes, self.aliases, self.n_sem = list(args), list(out_shapes), dict(aliases), n_sem
        self.start, self.finish = start, finish
        self.members = members
        self.results = None

    def set_results(self, res):
        self.results = list(res)
        for cm, o0 in self.members:
            cm.set_results(self.results[o0:o0 + len(cm.out_shapes)])


class _SemView:
    def __init__(self, sems, first):
        self.sems, self.first = sems, first

    @property
    def at(self):
        return self

    def __getitem__(self, k):
        return self.sems.at[self.first + k]


def _merge(comms):
    comms = [cm for cm in comms if cm is not None]
    if len(comms) < 2:
        return comms[0] if comms else None
    args, shapes, aliases, spans, n_sem = [], [], {}, [], 0
    for cm in comms:
        spans.append((len(args), len(shapes), n_sem))
        aliases.update({len(args) + a: len(shapes) + o for a, o in cm.aliases.items()})
        args += cm.args
        shapes += cm.out_shapes
        n_sem += cm.n_sem

    def each(step):
        def run(ar, ou, send, recv):
            for cm, (a0, o0, s0) in zip(comms, spans):
                getattr(cm, step)(ar[a0:a0 + len(cm.args)], ou[o0:o0 + len(cm.out_shapes)], _SemView(send, s0), _SemView(recv, s0))
        return run

    return Comm(args, shapes, aliases, n_sem, each("start"), each("finish"), [(cm, o0) for cm, (_, o0, _) in zip(comms, spans)])


def _place():
    x, y, c = lax.axis_index("x"), lax.axis_index("y"), lax.axis_index("c")
    chips = [(1 - x, y), (x, 1 - y), (1 - x, 1 - y)]
    return x, y, c, chips


def _remote(src, dst, send, recv, to):
    return pltpu.make_async_remote_copy(src_ref=src, dst_ref=dst, send_sem=send, recv_sem=recv, device_id=to, device_id_type=MESH)


def _any_specs(n):
    return [pl.BlockSpec(memory_space=pl.ANY)] * n


def _call(body, *, name, grid, in_specs, out_specs, out_shape, args, sem, scratch_shapes=(), aliases=None, comm=None):
    in_specs, out_specs, out_shape, scratch_shapes = list(in_specs), list(out_specs), list(out_shape), list(scratch_shapes)
    aliases = dict(aliases or {})
    if comm is None:
        return pl.pallas_call(body, name=name, grid=grid, in_specs=in_specs, out_specs=out_specs, out_shape=out_shape,
                              scratch_shapes=scratch_shapes, input_output_aliases=aliases, compiler_params=_params(sem))(*args)
    n_in, n_out, nc_in, nc_out, n_scr = len(in_specs), len(out_specs), len(comm.args), len(comm.out_shapes), len(scratch_shapes)

    def wrapped(*refs):
        ins, refs = refs[:n_in], refs[n_in:]
        cin, refs = refs[:nc_in], refs[nc_in:]
        outs, refs = refs[:n_out], refs[n_out:]
        cout, refs = refs[:nc_out], refs[nc_out:]
        scratch, (send, recv) = refs[:n_scr], refs[n_scr:]
        ids = [pl.program_id(k) for k in range(len(grid))]
        first = functools.reduce(jnp.logical_and, [i == 0 for i in ids])
        last = functools.reduce(jnp.logical_and, [i == g - 1 for i, g in zip(ids, grid)])

        @pl.when(first)
        def _():
            comm.start(cin, cout, send, recv)

        body(*ins, *outs, *scratch)

        @pl.when(last)
        def _():
            comm.finish(cin, cout, send, recv)

    res = pl.pallas_call(
        wrapped, name=name, grid=grid, in_specs=in_specs + _any_specs(nc_in), out_specs=out_specs + _any_specs(nc_out),
        out_shape=out_shape + comm.out_shapes,
        scratch_shapes=scratch_shapes + [pltpu.SemaphoreType.DMA((comm.n_sem,)), pltpu.SemaphoreType.DMA((comm.n_sem,))],
        input_output_aliases={**aliases, **{n_in + a: n_out + o for a, o in comm.aliases.items()}},
        compiler_params=_params(("arbitrary",) * len(grid)),
    )(*args, *comm.args)
    comm.set_results(res[n_out:])
    return list(res[:n_out])


def _run_comm(comm, name):
    nc_in, nc_out = len(comm.args), len(comm.out_shapes)

    def body(*refs):
        cin, cout, (send, recv) = refs[:nc_in], refs[nc_in:nc_in + nc_out], refs[nc_in + nc_out:]
        comm.start(cin, cout, send, recv)
        comm.finish(cin, cout, send, recv)

    res = pl.pallas_call(
        body, name=name, in_specs=_any_specs(nc_in), out_specs=_any_specs(nc_out), out_shape=comm.out_shapes,
        scratch_shapes=[pltpu.SemaphoreType.DMA((comm.n_sem,)), pltpu.SemaphoreType.DMA((comm.n_sem,))],
        input_output_aliases=comm.aliases,
    )(*comm.args)
    comm.set_results(res)


_HBM = pl.BlockSpec(memory_space=pltpu.HBM)
_SEM = pl.BlockSpec(memory_space=pltpu.SEMAPHORE)
_EFFECT = pltpu.SideEffectType.DATAFLOW_SIDE_EFFECTING


def _split_start(comm, name, after=()):
    n = len(comm.args)
    back = {o: a for a, o in comm.aliases.items()}
    assert len(back) == len(comm.out_shapes)

    n_after = len(after)

    def body(*refs):
        refs = refs[n + n_after:]
        send, recv, thru, token = refs[0], refs[1], refs[2:n + 2], refs[n + 2]
        comm.start(thru, [thru[back[o]] for o in range(len(back))], send, recv)
        token[...] = jnp.zeros_like(token)

    res = pl.pallas_call(
        body, name=name,
        out_shape=(pltpu.SemaphoreType.DMA((comm.n_sem,)), pltpu.SemaphoreType.DMA((comm.n_sem,)),
                   *[pltpu.HBM(a.shape, a.dtype) for a in comm.args], jax.ShapeDtypeStruct((8, LANES), F32)),
        in_specs=[_HBM] * n + _any_specs(n_after), out_specs=(_SEM, _SEM, *[_HBM] * n, pl.BlockSpec(memory_space=pltpu.VMEM)),
        input_output_aliases={i: 2 + i for i in range(n)},
        compiler_params=pltpu.CompilerParams(has_side_effects=_EFFECT),
    )(*[pltpu.with_memory_space_constraint(a, pltpu.HBM) for a in comm.args], *after)
    return (res[0], res[1], list(res[2:2 + n])), res[2 + n]


def _split_wait(comm, state, after, name):
    send, recv, thru = state
    n, n_after = len(thru), len(after)
    back = {o: a for a, o in comm.aliases.items()}

    def body(*refs):
        ins, send_ref, recv_ref = refs[:n], refs[n], refs[n + 1]
        comm.finish(ins, [ins[back[o]] for o in range(len(back))], send_ref, recv_ref)

    res = pl.pallas_call(
        body, name=name, out_shape=tuple(pltpu.HBM(a.shape, a.dtype) for a in thru),
        in_specs=[_HBM] * n + [_SEM, _SEM] + _any_specs(n_after), out_specs=[_HBM] * n,
        input_output_aliases={i: i for i in range(n)}, compiler_params=pltpu.CompilerParams(has_side_effects=_EFFECT),
    )(*thru, send, recv, *after)
    comm.set_results([res[back[o]] for o in range(len(back))])


def _forward_comm(slots):
    n = len(slots)

    def rows(t, who):
        rh = slots[t].shape[1] // 2
        return pl.ds(who * rh, rh)

    def copy(outs, send, recv, t, j, chip, who):
        x, y, c, _ = _place()
        blk = outs[t].at[2 * chip[0] + chip[1], rows(t, who)]
        return _remote(blk, blk, send.at[3 * t + j], recv.at[3 * t + j], (x, y, 1 - c))

    def start(args, outs, send, recv):
        _, _, c, chips = _place()
        for t in range(n):
            for j, chip in enumerate(chips):
                copy(outs, send, recv, t, j, chip, c).start()

    def finish(args, outs, send, recv):
        _, _, c, chips = _place()
        for t in range(n):
            for j, chip in enumerate(chips):
                copy(outs, send, recv, t, j, chip, 1 - c).wait_recv()
        for t in range(n):
            for j, chip in enumerate(chips):
                copy(outs, send, recv, t, j, chip, c).wait_send()

    shapes = [jax.ShapeDtypeStruct(w.shape, w.dtype) for w in slots]
    return Comm(slots, shapes, {t: t for t in range(n)}, 3 * n, start, finish)


def _gather_comm(slots):
    n = len(slots)

    def rows(t, who):
        rh = slots[t].shape[1] // 2
        return pl.ds(who * rh, rh)

    def start(args, outs, send, recv):
        x, y, c, chips = _place()
        for t in range(n):
            mine = outs[t].at[2 * x + y, rows(t, c)]
            for j, chip in enumerate(chips):
                _remote(mine, mine, send.at[3 * t + j], recv.at[3 * t + j], (*chip, c)).start()

    def finish(args, outs, send, recv):
        x, y, c, chips = _place()
        for t in range(n):
            for j, chip in enumerate(chips):
                blk = outs[t].at[2 * chip[0] + chip[1], rows(t, c)]
                _remote(blk, blk, send.at[3 * t + j], recv.at[3 * t + j], (*chip, c)).wait_recv()
        for t in range(n):
            mine = outs[t].at[2 * x + y, rows(t, c)]
            for j, chip in enumerate(chips):
                _remote(mine, mine, send.at[3 * t + j], recv.at[3 * t + j], (*chip, c)).wait_send()

    shapes = [jax.ShapeDtypeStruct(w.shape, w.dtype) for w in slots]
    return Comm(slots, shapes, {t: t for t in range(n)}, 3 * n, start, finish)


def _scatter_comm(parts, landed, chunks):
    n = len(parts)
    prev = [t for t in range(n) if landed[t] is not None]

    def rows(t):
        lo, hi, nch = chunks[t]
        rc = parts[t].shape[1] // nch
        return pl.ds(lo * rc, (hi - lo) * rc)

    def start(args, outs, send, recv):
        x, y, c, chips = _place()
        s = 2 * x + y
        for t in range(n):
            for j, chip in enumerate(chips):
                _remote(args[t].at[2 * chip[0] + chip[1], rows(t)], outs[t].at[s, rows(t)],
                        send.at[3 * t + j], recv.at[3 * t + j], (*chip, c)).start()

    def finish(args, outs, send, recv):
        x, y, c, chips = _place()
        for t in range(n):
            for j, chip in enumerate(chips):
                blk = outs[t].at[2 * chip[0] + chip[1], rows(t)]
                _remote(blk, blk, send.at[3 * t + j], recv.at[3 * t + j], (*chip, c)).wait_recv()
        for t in range(n):
            for j, chip in enumerate(chips):
                src = args[t].at[2 * chip[0] + chip[1], rows(t)]
                _remote(src, src, send.at[3 * t + j], recv.at[3 * t + j], (*chip, c)).wait_send()

    shapes = [jax.ShapeDtypeStruct(p.shape, p.dtype) for p in parts]
    return Comm(list(parts) + [landed[t] for t in prev], shapes, {n + i: t for i, t in enumerate(prev)}, 3 * n, start, finish)


def _swap_comm(gs):
    n = len(gs)

    def copy(args, outs, send, recv, t):
        _, _, c, _ = _place()
        rh = gs[t].shape[1] // 2
        x, y = lax.axis_index("x"), lax.axis_index("y")
        return _remote(args[t].at[:, pl.ds((1 - c) * rh, rh), :], outs[t], send.at[t], recv.at[t], (x, y, 1 - c))

    def start(args, outs, send, recv):
        for t in range(n):
            copy(args, outs, send, recv, t).start()

    def finish(args, outs, send, recv):
        for t in range(n):
            copy(args, outs, send, recv, t).wait()

    shapes = [jax.ShapeDtypeStruct((g.shape[0], g.shape[1] // 2, g.shape[2]), g.dtype) for g in gs]
    return Comm(gs, shapes, {}, n, start, finish)


def _join_comm(gs):
    n = len(gs)

    def half(outs, t, who):
        rh = gs[t].shape[0] // 2
        return outs[t].at[pl.ds(who * rh, rh), :]

    def start(args, outs, send, recv):
        x, y, c, _ = _place()
        for t in range(n):
            _remote(half(outs, t, c), half(outs, t, c), send.at[t], recv.at[t], (x, y, 1 - c)).start()

    def finish(args, outs, send, recv):
        x, y, c, _ = _place()
        for t in range(n):
            _remote(half(outs, t, 1 - c), half(outs, t, 1 - c), send.at[t], recv.at[t], (x, y, 1 - c)).wait_recv()
        for t in range(n):
            _remote(half(outs, t, c), half(outs, t, c), send.at[t], recv.at[t], (x, y, 1 - c)).wait_send()

    shapes = [jax.ShapeDtypeStruct(g.shape, g.dtype) for g in gs]
    return Comm(gs, shapes, {t: t for t in range(n)}, n, start, finish)


_DN = {"nn": (((1,), (0,)), ((), ())), "nt": (((1,), (1,)), ((), ())), "tn": (((0,), (0,)), ((), ()))}


def _mm(a, b, *, mode, out_dtype, name, out_groups=1, tm=1024, tn=1024, tk=2048, comm=None):
    ga, ra, ca = a.shape
    gb, rb, cb = b.shape
    if mode == "nn":
        M, K, N = ra, ga * ca, gb * cb
        assert rb == K and ga == 1 or (rb == K)
    elif mode == "nt":
        M, K, N = ra, ga * ca, rb
        assert gb * cb == K
    else:
        K, M, N = ra, ga * ca, gb * cb
        assert rb == K
    go = out_groups
    if mode == "nn":
        tk = _tile(ca, tk); assert rb % tk == 0 and (ga == 1 or True)
        tn = _tile(min(cb, N // go), tn); tm = _tile(M, tm, 8)
    elif mode == "nt":
        tk = _tile(ca, tk); tk = _tile(cb, tk) if cb % tk else tk; assert ca % tk == 0 and cb % tk == 0
        tn = _tile(N // go, tn); tm = _tile(M, tm, 8)
    else:
        tk = _tile(K, tk, 8); tm = _tile(ca, tm); tn = _tile(min(cb, N // go), tn)
    assert (N // go) % tn == 0 and M % tm == 0 and K % tk == 0, (name, M, N, K, tm, tn, tk)
    nk = K // tk
    kpa = max(ca // tk, 1)
    kpb = max(cb // tk, 1)
    npb = max(cb // tn, 1)
    npo = (N // go) // tn
    mpa = max(ca // tm, 1)

    if mode == "nn":
        a_spec = pl.BlockSpec((1, tm, tk), lambda j, i, k: (k // kpa, i, k % kpa))
        b_spec = pl.BlockSpec((1, tk, tn), lambda j, i, k: (j // npb, k, j % npb))
    elif mode == "nt":
        a_spec = pl.BlockSpec((1, tm, tk), lambda j, i, k: (k // kpa, i, k % kpa))
        b_spec = pl.BlockSpec((1, tn, tk), lambda j, i, k: (k // kpb, j, k % kpb))
    else:
        a_spec = pl.BlockSpec((1, tk, tm), lambda j, i, k: (i // mpa, k, i % mpa))
        b_spec = pl.BlockSpec((1, tk, tn), lambda j, i, k: (j // npb, k, j % npb))
    o_spec = pl.BlockSpec((1, tm, tn), lambda j, i, k: (j // npo, i, j % npo))
    dn = _DN[mode]

    def body(a_ref, b_ref, o_ref, *acc):
        p = lax.dot_general(a_ref[0], b_ref[0], dn, preferred_element_type=F32)
        if nk == 1:
            o_ref[0] = p.astype(out_dtype)
        else:
            k = pl.program_id(2)

            @pl.when(k == 0)
            def _():
                acc[0][...] = p

            @pl.when(k > 0)
            def _():
                acc[0][...] += p

            @pl.when(k == nk - 1)
            def _():
                o_ref[0] = acc[0][...].astype(out_dtype)

    return _call(
        body, name=name, grid=(N // tn, M // tm, nk), in_specs=[a_spec, b_spec], out_specs=[o_spec],
        out_shape=[jax.ShapeDtypeStruct((go, M, N // go), out_dtype)],
        scratch_shapes=[pltpu.VMEM((tm, tn), F32)] if nk > 1 else [],
        sem=("parallel", "parallel", "arbitrary"), args=(a, b), comm=comm)[0]


def _rows(tr, d):
    return pl.BlockSpec((tr, d), lambda i: (i, 0))


def _vec(d):
    return pl.BlockSpec((1, d), lambda i: (0, 0))


def _modulate(x, sc, sh, name):
    S, D = x.shape
    tr = min(256, S)

    def body(x_ref, sc_ref, sh_ref, h_ref):
        h_ref[...] = (x_ref[...] * (1.0 + sc_ref[...]) + sh_ref[...]).astype(BF16)

    return pl.pallas_call(
        body, name=name, grid=(S // tr,), in_specs=[_rows(tr, D), _vec(D), _vec(D)], out_specs=_rows(tr, D),
        out_shape=jax.ShapeDtypeStruct((S, D), BF16), compiler_params=_params(("parallel",)),
    )(x, sc, sh)


def _ln_fwd(x, y, gate, gamma, beta, sc, sh, name, comm=None):
    S, D = x.shape
    tr = min(256, S)
    emit_h = sc is not None

    def body(*refs):
        if emit_h:
            x_ref, y_ref, g_ref, ga_ref, be_ref, sc_ref, sh_ref, xo_ref, xh_ref, rs_ref, h_ref = refs
        else:
            x_ref, y_ref, g_ref, ga_ref, be_ref, xo_ref, xh_ref, rs_ref = refs
        z = DEEPNORM_ALPHA * x_ref[...] + (1.0 + g_ref[...]) * y_ref[...]
        mu = jnp.mean(z, axis=-1, keepdims=True)
        zc = z - mu
        var = jnp.mean(zc * zc, axis=-1, keepdims=True)
        rstd = lax.rsqrt(var + LN_EPS)
        xh = zc * rstd
        xo = xh * ga_ref[...] + be_ref[...]
        xo_ref[...] = xo
        xh_ref[...] = xh
        rs_ref[...] = rstd
        if emit_h:
            h_ref[...] = (xo * (1.0 + sc_ref[...]) + sh_ref[...]).astype(BF16)

    ins = [x, y, gate, gamma, beta] + ([sc, sh] if emit_h else [])
    in_specs = [_rows(tr, D), _rows(tr, D)] + [_vec(D)] * (len(ins) - 2)
    out_shape = [jax.ShapeDtypeStruct((S, D), F32), jax.ShapeDtypeStruct((S, D), F32), jax.ShapeDtypeStruct((S, 1), F32)]
    out_specs = [_rows(tr, D), _rows(tr, D), _rows(tr, 1)]
    if emit_h:
        out_shape.append(jax.ShapeDtypeStruct((S, D), BF16))
        out_specs.append(_rows(tr, D))
    return _call(body, name=name, grid=(S // tr,), in_specs=in_specs, out_specs=out_specs, out_shape=out_shape,
                 sem=("parallel",), args=ins, comm=comm)


def _loss_head(xf, tgt, name):
    S, D = xf.shape
    tr = min(256, S)

    def body(x_ref, t_ref, dx_ref, l_ref):
        e = x_ref[...] - t_ref[...]
        dx_ref[...] = e * (1.0 / D)

        @pl.when(pl.program_id(0) == 0)
        def _():
            l_ref[...] = jnp.zeros_like(l_ref)

        l_ref[...] += jnp.sum(e * e, axis=0, keepdims=True)

    return pl.pallas_call(
        body, name=name, grid=(S // tr,), in_specs=[_rows(tr, D), _rows(tr, D)],
        out_specs=[_rows(tr, D), _vec(D)],
        out_shape=[jax.ShapeDtypeStruct((S, D), F32), jax.ShapeDtypeStruct((1, D), F32)],
        compiler_params=_params(("arbitrary",)),
    )(xf, tgt)


def _ln_bwd(dxo, xh, rstd, gamma, y, gate, name, pre=None):
    S, D = dxo.shape
    tr = min(256, S)
    n_pre = 0 if pre is None else 3

    def body(dx_ref, xh_ref, rs_ref, ga_ref, y_ref, g_ref, *rest):
        dres_ref, dy_ref, dga_ref, dbe_ref, dg_ref = rest[n_pre:n_pre + 5]
        first = pl.program_id(0) == 0
        dxo_ = dx_ref[...]
        xh_ = xh_ref[...]
        if pre is not None:
            dh_ref, sc_ref, be_ref = rest[:3]
            dsc_ref, dsh_ref = rest[n_pre + 5:]
            dh_ = dh_ref[...]
            dxo_ = dxo_ + dh_ * (1.0 + sc_ref[...])

            @pl.when(first)
            def _():
                dsc_ref[...] = jnp.zeros_like(dsc_ref)
                dsh_ref[...] = jnp.zeros_like(dsh_ref)

            dsc_ref[...] += jnp.sum(dh_ * (xh_ * ga_ref[...] + be_ref[...]), axis=0, keepdims=True)
            dsh_ref[...] += jnp.sum(dh_, axis=0, keepdims=True)
        dxh = dxo_ * ga_ref[...]
        m1 = jnp.mean(dxh, axis=-1, keepdims=True)
        m2 = jnp.mean(dxh * xh_, axis=-1, keepdims=True)
        dz = rs_ref[...] * (dxh - m1 - xh_ * m2)
        dres_ref[...] = DEEPNORM_ALPHA * dz
        dy_ref[...] = ((1.0 + g_ref[...]) * dz).astype(BF16)

        @pl.when(first)
        def _():
            dga_ref[...] = jnp.zeros_like(dga_ref)
            dbe_ref[...] = jnp.zeros_like(dbe_ref)
            dg_ref[...] = jnp.zeros_like(dg_ref)

        dga_ref[...] += jnp.sum(dxo_ * xh_, axis=0, keepdims=True)
        dbe_ref[...] += jnp.sum(dxo_, axis=0, keepdims=True)
        dg_ref[...] += jnp.sum(dz * y_ref[...], axis=0, keepdims=True)

    extra_in = [] if pre is None else [_rows(tr, D), _vec(D), _vec(D)]
    return pl.pallas_call(
        body, name=name, grid=(S // tr,),
        in_specs=[_rows(tr, D), _rows(tr, D), _rows(tr, 1), _vec(D), _rows(tr, D), _vec(D)] + extra_in,
        out_specs=[_rows(tr, D), _rows(tr, D)] + [_vec(D)] * (3 + (0 if pre is None else 2)),
        out_shape=[jax.ShapeDtypeStruct((S, D), F32), jax.ShapeDtypeStruct((S, D), BF16)]
        + [jax.ShapeDtypeStruct((1, D), F32)] * (3 + (0 if pre is None else 2)),
        compiler_params=_params(("arbitrary",)),
    )(dxo, xh, rstd, gamma, y, gate, *(pre or ()))


def _mod_bwd(dh, x, sc, dres, name):
    S, D = x.shape
    tr = min(256, S)

    def body(dh_ref, x_ref, sc_ref, dr_ref, dx_ref, dsc_ref, dsh_ref):
        dh_ = dh_ref[...]
        dx_ref[...] = dr_ref[...] + dh_ * (1.0 + sc_ref[...])

        @pl.when(pl.program_id(0) == 0)
        def _():
            dsc_ref[...] = jnp.zeros_like(dsc_ref)
            dsh_ref[...] = jnp.zeros_like(dsh_ref)

        dsc_ref[...] += jnp.sum(dh_ * x_ref[...], axis=0, keepdims=True)
        dsh_ref[...] += jnp.sum(dh_, axis=0, keepdims=True)

    return pl.pallas_call(
        body, name=name, grid=(S // tr,),
        in_specs=[_rows(tr, D), _rows(tr, D), _vec(D), _rows(tr, D)],
        out_specs=[_rows(tr, D), _vec(D), _vec(D)],
        out_shape=[jax.ShapeDtypeStruct((S, D), F32), jax.ShapeDtypeStruct((1, D), F32), jax.ShapeDtypeStruct((1, D), F32)],
        compiler_params=_params(("arbitrary",)),
    )(dh, x, sc, dres)


def _log_sigmoid(z):
    return jnp.minimum(z, 0.0) - jnp.log(1.0 + jnp.exp(-jnp.abs(z)))


def _fox_gate_fwd(proj, b_f, n_heads, name):
    S, PW = proj.shape
    blk = min(256, S)
    last = PW // LANES - 1

    def body(fl_ref, b_ref, cum_ref):
        r = lax.broadcasted_iota(jnp.int32, (blk, blk), 0)
        c = lax.broadcasted_iota(jnp.int32, (blk, blk), 1)
        tril = (c <= r).astype(F32)
        carry = jnp.zeros((1, LANES), F32)
        for i in range(S // blk):
            lf = _log_sigmoid(fl_ref[i * blk:(i + 1) * blk, :] + b_ref[...])
            cum_ref[i * blk:(i + 1) * blk, :] = jnp.dot(tril, lf, preferred_element_type=F32, precision=HIGHEST) + carry
            carry = carry + jnp.sum(lf, axis=0, keepdims=True)

    return pl.pallas_call(
        body, name=name, grid=(1,),
        in_specs=[pl.BlockSpec((S, LANES), lambda i: (0, last)), pl.BlockSpec((1, LANES), lambda i: (0, 0))],
        out_specs=pl.BlockSpec((S, LANES), lambda i: (0, 0)),
        out_shape=jax.ShapeDtypeStruct((S, LANES), F32), compiler_params=_params(("arbitrary",)),
    )(proj, b_f)


def _fox_gate_bwd(dcum, proj, b_f, n_heads, name):
    S, PW = proj.shape
    blk = min(256, S)
    last = PW // LANES - 1
    nb = S // blk

    def body(dc_ref, fl_ref, b_ref, dfl_ref, db_ref):
        r = lax.broadcasted_iota(jnp.int32, (blk, blk), 0)
        c = lax.broadcasted_iota(jnp.int32, (blk, blk), 1)
        triu = (c >= r).astype(F32)
        lane = lax.broadcasted_iota(jnp.int32, (blk, LANES), 1)
        carry = jnp.zeros((1, LANES), F32)
        dbs = jnp.zeros((1, LANES), F32)
        for i in reversed(range(nb)):
            dc = dc_ref[i * blk:(i + 1) * blk, :]
            dlf = jnp.dot(triu, dc, preferred_element_type=F32, precision=HIGHEST) + carry
            carry = carry + jnp.sum(dc, axis=0, keepdims=True)
            z = fl_ref[i * blk:(i + 1) * blk, :] + b_ref[...]
            e = jnp.exp(-jnp.abs(z))
            sig_neg = jnp.where(z >= 0, e / (1.0 + e), 1.0 / (1.0 + e))
            dfl = jnp.where(lane < n_heads, dlf * sig_neg, 0.0)
            dfl_ref[i * blk:(i + 1) * blk, :] = dfl.astype(BF16)
            dbs = dbs + jnp.sum(dfl, axis=0, keepdims=True)
        db_ref[...] = dbs

    return pl.pallas_call(
        body, name=name, grid=(1,),
        in_specs=[pl.BlockSpec((S, LANES), lambda i: (0, 0)), pl.BlockSpec((S, LANES), lambda i: (0, last)),
                  pl.BlockSpec((1, LANES), lambda i: (0, 0))],
        out_specs=[pl.BlockSpec((S, LANES), lambda i: (0, 0)), pl.BlockSpec((1, LANES), lambda i: (0, 0))],
        out_shape=[jax.ShapeDtypeStruct((S, LANES), BF16), jax.ShapeDtypeStruct((1, LANES), F32)],
        compiler_params=_params(("arbitrary",)),
    )(dcum, proj, b_f)


def _fox_scores(q_ref, kb_ref, cq_ref, ck_ref, qi, tq, scale):
    kk = (qi + 1) * tq
    rows = slice(qi * tq, (qi + 1) * tq)
    qb = q_ref[rows, :].astype(BF16)
    s = lax.dot_general(qb, kb_ref[0:kk, :], _DN["nt"], preferred_element_type=F32) * scale
    s = s + (cq_ref[0, rows, :] - ck_ref[0, :, 0:kk])
    r = lax.broadcasted_iota(jnp.int32, (tq, kk), 0) + qi * tq
    c = lax.broadcasted_iota(jnp.int32, (tq, kk), 1)
    mask = c <= r
    return jnp.where(mask, s, NEG), mask, qb


def _fox_fwd(proj, cq, ck, n_heads, name, comm=None):
    S = proj.shape[0]
    H = n_heads
    tq = min(FOX_TQ, S)
    nq = S // tq
    scale = FDH ** -0.5

    def body(q_ref, k_ref, v_ref, cq_ref, ck_ref, o_ref, lse_ref, kb_ref, vb_ref):
        kb_ref[...] = k_ref[...].astype(BF16)
        vb_ref[...] = v_ref[...].astype(BF16)
        for qi in range(nq):
            kk = (qi + 1) * tq
            rows = slice(qi * tq, (qi + 1) * tq)
            s, _, _ = _fox_scores(q_ref, kb_ref, cq_ref, ck_ref, qi, tq, scale)
            m = jnp.max(s, axis=-1, keepdims=True)
            p = jnp.exp(s - m)
            l = jnp.sum(p, axis=-1, keepdims=True)
            p = p * (1.0 / l)
            o_ref[rows, :] = jnp.dot(p.astype(BF16), vb_ref[0:kk, :], preferred_element_type=F32).astype(BF16)
            lse_ref[0, rows, :] = m + jnp.log(l)

    col = lambda off: pl.BlockSpec((S, FDH), lambda h: (0, h + off))
    stat_c = pl.BlockSpec((1, S, 1), lambda h: (h, 0, 0))
    stat_r = pl.BlockSpec((1, 1, S), lambda h: (h, 0, 0))
    return _call(
        body, name=name, grid=(H,),
        in_specs=[col(0), col(H), col(2 * H), stat_c, stat_r],
        out_specs=[col(0), stat_c],
        out_shape=[jax.ShapeDtypeStruct((S, H * FDH), BF16), jax.ShapeDtypeStruct((H, S, 1), F32)],
        scratch_shapes=[pltpu.VMEM((S, FDH), BF16), pltpu.VMEM((S, FDH), BF16)],
        sem=("parallel",), args=(proj, proj, proj, cq, ck), comm=comm)


def _fox_bwd(proj, cq, ck, lse, do, n_heads, name, comm=None):
    S = proj.shape[0]
    H = n_heads
    tq = min(FOX_TQ, S)
    nq = S // tq
    scale = FDH ** -0.5

    def body(q_ref, k_ref, v_ref, cq_ref, ck_ref, lse_ref, do_ref, dq_ref, dk_ref, dv_ref, dcq_ref, dck_ref,
             kb_ref, vb_ref, dka_ref, dva_ref):
        kb_ref[...] = k_ref[...].astype(BF16)
        vb_ref[...] = v_ref[...].astype(BF16)
        dka_ref[...] = jnp.zeros_like(dka_ref)
        dva_ref[...] = jnp.zeros_like(dva_ref)
        dck_ref[...] = jnp.zeros_like(dck_ref)
        for qi in range(nq):
            kk = (qi + 1) * tq
            rows = slice(qi * tq, (qi + 1) * tq)
            s, mask, qb = _fox_scores(q_ref, kb_ref, cq_ref, ck_ref, qi, tq, scale)
            p = jnp.where(mask, jnp.exp(s - lse_ref[0, rows, :]), 0.0)
            dob = do_ref[rows, :]
            dp = lax.dot_general(dob, vb_ref[0:kk, :], _DN["nt"], preferred_element_type=F32)
            delta = jnp.sum(p * dp, axis=-1, keepdims=True)
            ds = p * (dp - delta)
            dcq_ref[0, rows, :] = jnp.sum(ds, axis=-1, keepdims=True)
            dck_ref[0, :, 0:kk] -= jnp.sum(ds, axis=0, keepdims=True)
            dsb = (ds * scale).astype(BF16)
            dq_ref[rows, :] = jnp.dot(dsb, kb_ref[0:kk, :], preferred_element_type=F32).astype(BF16)
            dka_ref[0:kk, :] += lax.dot_general(dsb, qb, _DN["tn"], preferred_element_type=F32)
            dva_ref[0:kk, :] += lax.dot_general(p.astype(BF16), dob, _DN["tn"], preferred_element_type=F32)
        dk_ref[...] = dka_ref[...].astype(BF16)
        dv_ref[...] = dva_ref[...].astype(BF16)

    col = lambda off: pl.BlockSpec((S, FDH), lambda h: (0, h + off))
    stat_c = pl.BlockSpec((1, S, 1), lambda h: (h, 0, 0))
    stat_r = pl.BlockSpec((1, 1, S), lambda h: (h, 0, 0))
    wide = jax.ShapeDtypeStruct((S, H * FDH), BF16)
    return _call(
        body, name=name, grid=(H,),
        in_specs=[col(0), col(H), col(2 * H), stat_c, stat_r, stat_c, col(0)],
        out_specs=[col(0), col(0), col(0), stat_c, stat_r],
        out_shape=[wide, wide, wide, jax.ShapeDtypeStruct((H, S, 1), F32), jax.ShapeDtypeStruct((H, 1, S), F32)],
        scratch_shapes=[pltpu.VMEM((S, FDH), BF16), pltpu.VMEM((S, FDH), BF16), pltpu.VMEM((S, FDH), F32), pltpu.VMEM((S, FDH), F32)],
        sem=("parallel",), args=(proj, proj, proj, cq, ck, lse, do), comm=comm)


def _rope_tables(pos, sign):
    inv = ROPE_THETA ** (-jnp.arange(0, ROPE_DIM, 2, dtype=F32) / ROPE_DIM)
    ang = pos.astype(F32)[:, None] * inv
    cos, sin = jnp.cos(ang), sign * jnp.sin(ang)
    l64 = jnp.arange(LANES) % SDH
    idx = l64 % (ROPE_DIM // 2)
    c = jnp.where(l64 < ROPE_DIM, cos[:, idx], 1.0)
    sa = jnp.where(l64 < ROPE_DIM // 2, -sin[:, idx], 0.0)
    sb = jnp.where((l64 >= ROPE_DIM // 2) & (l64 < ROPE_DIM), sin[:, idx], 0.0)
    rot = jnp.stack([c, sa, sb])
    ident = jnp.stack([jnp.ones_like(c), jnp.zeros_like(c), jnp.zeros_like(c)])
    return jnp.stack([rot, ident]).astype(F32)


def _rope(xin, tabs, n_rot, out_dtype, name, comm=None):
    S, W = xin.shape

    def body(x_ref, t_ref, o_ref):
        xv = x_ref[...]
        o = xv * t_ref[0, 0] + pltpu.roll(xv, LANES - ROPE_DIM // 2, 1) * t_ref[0, 1] + pltpu.roll(xv, ROPE_DIM // 2, 1) * t_ref[0, 2]
        o_ref[...] = o.astype(out_dtype)

    return _call(
        body, name=name, grid=(W // LANES,),
        in_specs=[pl.BlockSpec((S, LANES), lambda j: (0, j)),
                  pl.BlockSpec((1, 3, S, LANES), lambda j: (jnp.where(j < n_rot, 0, 1), 0, 0, 0))],
        out_specs=[pl.BlockSpec((S, LANES), lambda j: (0, j))],
        out_shape=[jax.ShapeDtypeStruct((S, W), out_dtype)], sem=("parallel",), args=(xin, tabs), comm=comm)[0]


SWA_PER_STEP = 2


def _swa_bias():
    r = jnp.arange(WIN)[:, None]
    c = jnp.arange(2 * WIN)[None, :]
    first = c <= r
    later = (c > r) & (c <= r + WIN)
    return jnp.where(jnp.stack([first, later]), 0.0, NEG).astype(F32)


def _swa_probs(q_ref, k_ref, sk_ref, b_ref, n, j, scale):
    st = pl.multiple_of(jnp.maximum(n - 1, 0) * WIN, WIN)
    qb = q_ref[0, j]
    kb = k_ref[0, pl.ds(st, 2 * WIN), :]
    gm = qb.shape[0]
    s = lax.dot_general(qb, kb, _DN["nt"], preferred_element_type=F32) * scale
    s = (s.reshape(gm // WIN, WIN, 2 * WIN) + b_ref[jnp.minimum(n, 1)][None]).reshape(gm, 2 * WIN)
    sink = sk_ref[0]
    m = jnp.maximum(jnp.max(s, axis=-1, keepdims=True), sink)
    e = jnp.exp(s - m)
    es = jnp.exp(sink - m)
    inv = 1.0 / (jnp.sum(e, axis=-1, keepdims=True) + es)
    return e * inv, es * inv, st, qb, kb


def _swa_specs(S, gm):
    blk = pl.BlockSpec((1, SWA_PER_STEP, gm, SDH), lambda g, n: (g, n, 0, 0))
    kv = pl.BlockSpec((1, S, SDH), lambda g, n: (g, 0, 0))
    col = pl.BlockSpec((1, gm, 1), lambda g, n: (g, 0, 0))
    bias = pl.BlockSpec((2, WIN, 2 * WIN), lambda g, n: (0, 0, 0))
    return blk, kv, col, bias


def _swa_fwd(q, k, v, sinks, name, comm=None):
    KH, nb, gm, _ = q.shape
    S = k.shape[1]
    scale = SDH ** -0.5

    def body(q_ref, k_ref, v_ref, sk_ref, b_ref, o_ref):
        for j in range(SWA_PER_STEP):
            p, _, st, _, _ = _swa_probs(q_ref, k_ref, sk_ref, b_ref, pl.program_id(1) * SWA_PER_STEP + j, j, scale)
            vb = v_ref[0, pl.ds(st, 2 * WIN), :]
            o_ref[0, j] = jnp.dot(p.astype(BF16), vb, preferred_element_type=F32).astype(BF16)

    blk, kv, col, bias = _swa_specs(S, gm)
    return _call(
        body, name=name, grid=(KH, nb // SWA_PER_STEP), in_specs=[blk, kv, kv, col, bias], out_specs=[blk],
        out_shape=[jax.ShapeDtypeStruct(q.shape, BF16)], sem=("parallel", "parallel"), args=(q, k, v, sinks, _swa_bias()), comm=comm)[0]


def _swa_bwd(q, k, v, sinks, do, name, comm=None):
    KH, nb, gm, _ = q.shape
    S = k.shape[1]
    scale = SDH ** -0.5

    def body(q_ref, k_ref, v_ref, sk_ref, b_ref, do_ref, dq_ref, dk_ref, dv_ref, dsk_ref):
        @pl.when(pl.program_id(1) == 0)
        def _():
            dk_ref[...] = jnp.zeros_like(dk_ref)
            dv_ref[...] = jnp.zeros_like(dv_ref)
            dsk_ref[...] = jnp.zeros_like(dsk_ref)

        blocks = []
        for j in range(SWA_PER_STEP):
            p, ps, st, qb, kb = _swa_probs(q_ref, k_ref, sk_ref, b_ref, pl.program_id(1) * SWA_PER_STEP + j, j, scale)
            vb = v_ref[0, pl.ds(st, 2 * WIN), :]
            dob = do_ref[0, j]
            dp = lax.dot_general(dob, vb, _DN["nt"], preferred_element_type=F32)
            delta = jnp.sum(p * dp, axis=-1, keepdims=True)
            dsb = (p * (dp - delta) * scale).astype(BF16)
            dq_ref[0, j] = jnp.dot(dsb, kb, preferred_element_type=F32)
            blocks.append((st, lax.dot_general(dsb, qb, _DN["tn"], preferred_element_type=F32),
                           lax.dot_general(p.astype(BF16), dob, _DN["tn"], preferred_element_type=F32), ps * delta))
        for st, dk, dv, dsk in blocks:
            dk_ref[0, pl.ds(st, 2 * WIN), :] += dk
            dv_ref[0, pl.ds(st, 2 * WIN), :] += dv
            dsk_ref[0] -= dsk

    blk, kv, col, bias = _swa_specs(S, gm)
    return _call(
        body, name=name, grid=(KH, nb // SWA_PER_STEP), in_specs=[blk, kv, kv, col, bias, blk], out_specs=[blk, kv, kv, col],
        out_shape=[jax.ShapeDtypeStruct(q.shape, F32), jax.ShapeDtypeStruct(k.shape, F32),
                   jax.ShapeDtypeStruct(k.shape, F32), jax.ShapeDtypeStruct(sinks.shape, F32)],
        sem=("parallel", "arbitrary"), args=(q, k, v, sinks, _swa_bias(), do), comm=comm)


def _shift_down(u, k):
    row = lax.broadcasted_iota(jnp.int32, u.shape, 0)
    return jnp.where(row >= k, pltpu.roll(u, k, 0), 0.0)


def _shift_up(u, k):
    n = u.shape[0]
    row = lax.broadcasted_iota(jnp.int32, u.shape, 0)
    return jnp.where(row < n - k, pltpu.roll(u, n - k, 0), 0.0)


def _conv3(u, w_ref, b_ref):
    return w_ref[0:1, :] * _shift_down(u, 2) + w_ref[1:2, :] * _shift_down(u, 1) + w_ref[2:3, :] * u + b_ref[...]


def _conv_gate(u, cw, cb, name, comm=None):
    S, F2 = u.shape
    Fh = F2 // 2
    tc = _tile(Fh, 256)
    nf = Fh // tc

    def body(ug_ref, uv_ref, wg_ref, wv_ref, bg_ref, bv_ref, a_ref):
        g = _conv3(ug_ref[...], wg_ref, bg_ref)
        val = _conv3(uv_ref[...], wv_ref, bv_ref)
        a_ref[...] = (g * (1.0 / (1.0 + jnp.exp(-g))) * val).astype(BF16)

    blk = lambda r, off: pl.BlockSpec((r, tc), lambda j: (0, j + off))
    return _call(
        body, name=name, grid=(nf,),
        in_specs=[blk(S, 0), blk(S, nf), blk(3, 0), blk(3, nf), blk(1, 0), blk(1, nf)], out_specs=[blk(S, 0)],
        out_shape=[jax.ShapeDtypeStruct((S, Fh), BF16)], sem=("parallel",), args=(u, u, cw, cw, cb, cb), comm=comm)[0]


def _conv_gate_bwd(u, da, cw, cb, name, comm=None):
    S, F2 = u.shape
    Fh = F2 // 2
    tc = _tile(Fh, 256)
    nf = Fh // tc

    def half(h, dx, uu, w_ref, du_ref, dw_ref, db_ref):
        up1, up2 = _shift_up(dx, 1), _shift_up(dx, 2)
        du = w_ref[2:3, :] * dx + w_ref[1:2, :] * up1 + w_ref[0:1, :] * up2
        du_ref[h] = du.astype(BF16)
        dw_ref[h, 0:1, :] = jnp.sum(up2 * uu, axis=0, keepdims=True)
        dw_ref[h, 1:2, :] = jnp.sum(up1 * uu, axis=0, keepdims=True)
        dw_ref[h, 2:3, :] = jnp.sum(dx * uu, axis=0, keepdims=True)
        db_ref[h] = jnp.sum(dx, axis=0, keepdims=True)

    def body(ug_ref, uv_ref, da_ref, wg_ref, wv_ref, bg_ref, bv_ref, du_ref, dw_ref, db_ref):
        ug = ug_ref[...]
        uv = uv_ref[...]
        g = _conv3(ug, wg_ref, bg_ref)
        val = _conv3(uv, wv_ref, bv_ref)
        sig = 1.0 / (1.0 + jnp.exp(-g))
        da_ = da_ref[...]
        dg = da_ * val * (sig * (1.0 + g * (1.0 - sig)))
        dval = da_ * (g * sig)
        half(0, dg, ug, wg_ref, du_ref, dw_ref, db_ref)
        half(1, dval, uv, wv_ref, du_ref, dw_ref, db_ref)

    blk = lambda r, off: pl.BlockSpec((r, tc), lambda j: (0, j + off))
    both = lambda r: pl.BlockSpec((2, r, tc), lambda j: (0, 0, j))
    return _call(
        body, name=name, grid=(nf,),
        in_specs=[blk(S, 0), blk(S, nf), blk(S, 0), blk(3, 0), blk(3, nf), blk(1, 0), blk(1, nf)],
        out_specs=[both(S), both(3), both(1)],
        out_shape=[jax.ShapeDtypeStruct((2, S, Fh), BF16), jax.ShapeDtypeStruct((2, 3, Fh), F32), jax.ShapeDtypeStruct((2, 1, Fh), F32)],
        sem=("parallel",), args=(u, u, da, cw, cw, cb, cb), comm=comm)


def _to_groups(t, kh):
    S, width = t.shape
    g = width // SDH // kh
    return t.reshape(S // WIN, WIN, kh, g, SDH).transpose(2, 0, 3, 1, 4).reshape(kh, S // WIN, g * WIN, SDH)


def _from_groups(t):
    kh, nb, gm, _ = t.shape
    g = gm // WIN
    return t.reshape(kh, nb, g, WIN, SDH).transpose(1, 3, 0, 2, 4).reshape(nb * WIN, kh * g * SDH)


class LocalWeights:
    def __init__(self, weights):
        self.weights, self.grads = weights, {}

    def w(self, name):
        return self.weights[name]

    def carry(self, stage, last=()):
        return None

    def carried(self, stage, comm):
        pass

    def grad(self, name, g):
        self.grads[name] = g


def _local_step(dm, x, tgt, pos, mod, sp, pp, h0=None):
    S, D, FH, QH, KH, Fh = dm
    m = [[mod[i:i + 1, j * D:(j + 1) * D] for j in range(6)] for i in range(DEPTH)]

    last = []

    def run(fn, *args, name, **kw):
        comm = pp.carry(name, last)
        args = [pp.w(arg[1]) if isinstance(arg, tuple) and arg[:1] == ("w",) else arg for arg in args]
        out = fn(*args, name=name, comm=comm, **kw)
        if comm is not None:
            pp.carried(name, comm)
        last[:] = list(out) if isinstance(out, (list, tuple)) else [out]
        return out

    sv = []
    xs = x
    h = _modulate(xs, m[0][1], m[0][0], "mod_in") if h0 is None else h0
    last[:] = [h]
    for i in range(DEPTH):
        sh1, sc1, g1, sh2, sc2, g2 = m[i]
        L = {}
        L["x_in"], L["h1"] = xs, h
        if i == 0:
            proj = run(_mm, h[None], ("w", "fox_w_in"), mode="nn", out_dtype=F32, name="fox_proj", tn=896)[0]
            cum = _fox_gate_fwd(proj, sp["fox_b_f"], FH, "fox_gate")
            cq = cum[:, :FH].T[:, :, None]
            ck = cum[:, :FH].T[:, None, :]
            o, lse = run(_fox_fwd, proj, cq, ck, FH, name="fox_attn")
            L.update(proj=proj, cq=cq, ck=ck, lse=lse, o=o)
            y = run(_mm, o[None], ("w", "fox_w_o"), mode="nn", out_dtype=F32, name="fox_out")[0]
        else:
            proj = run(_mm, h[None], ("w", "swa_w_in"), mode="nn", out_dtype=F32, name="swa_proj", tn=640)[0]
            tabs = _rope_tables(pos, 1.0)
            n_rot = (QH + KH) * SDH // LANES
            pr = run(_rope, proj, tabs, n_rot, BF16, name="swa_rope")
            qh = _to_groups(pr[:, :QH * SDH], KH)
            kh = pr[:, QH * SDH:(QH + KH) * SDH].reshape(S, KH, SDH).transpose(1, 0, 2)
            vh = pr[:, (QH + KH) * SDH:].reshape(S, KH, SDH).transpose(1, 0, 2)
            oh = run(_swa_fwd, qh, kh, vh, sp["sinks"], name="swa_attn")
            o = _from_groups(oh)
            L.update(qh=qh, kh=kh, vh=vh, o=o)
            y = run(_mm, o[None], ("w", "swa_w_o"), mode="nn", out_dtype=F32, name="swa_out")[0]
        L["y1"] = y
        x1, L["xh1"], L["rs1"], h2 = run(_ln_fwd, xs, y, g1, sp["ln_mix_g"][i], sp["ln_mix_b"][i], sc2, sh2, name=f"ln_mix{i}")
        L["x1"], L["h2"] = x1, h2
        u = run(_mm, h2[None], ("w", f"ffn_w_up{i}"), mode="nn", out_dtype=F32, name=f"ffn_up{i}", tm=512, tn=1408)[0]
        a = run(_conv_gate, u, sp["conv_w"][i], sp["conv_b"][i], name=f"ffn_gate{i}")
        y2 = run(_mm, a[None], ("w", f"ffn_w_down{i}"), mode="nn", out_dtype=F32, name=f"ffn_down{i}", tk=5632, tm=512)[0]
        L.update(u=u, a=a, y2=y2)
        if i + 1 < DEPTH:
            xs, L["xh2"], L["rs2"], h = run(_ln_fwd, x1, y2, g2, sp["ln_ffn_g"][i], sp["ln_ffn_b"][i], m[i + 1][1], m[i + 1][0], name=f"ln_ffn{i}")
        else:
            xs, L["xh2"], L["rs2"] = run(_ln_fwd, x1, y2, g2, sp["ln_ffn_g"][i], sp["ln_ffn_b"][i], None, None, name=f"ln_ffn{i}")
        sv.append(L)

    dx, loss_cols = _loss_head(xs, tgt, "loss_head")

    gs = {k: [None] * DEPTH for k in ("conv_w", "conv_b", "ln_mix_g", "ln_mix_b", "ln_ffn_g", "ln_ffn_b")}
    dmp = [dict() for _ in range(DEPTH)]
    dres, pend = dx, None
    for i in reversed(range(DEPTH)):
        sh1, sc1, g1, sh2, sc2, g2 = m[i]
        L = sv[i]
        res = _ln_bwd(dres, L["xh2"], L["rs2"], sp["ln_ffn_g"][i], L["y2"], g2, f"ln_ffn_bwd{i}",
                      None if pend is None else (*pend, sp["ln_ffn_b"][i]))
        dres, dy, gs["ln_ffn_g"][i], gs["ln_ffn_b"][i], dmp[i]["g2"] = res[:5]
        if pend is not None:
            dmp[i + 1]["sc1"], dmp[i + 1]["sh1"] = res[5:]
        da = run(_mm, dy[None], pp.w(f"ffn_w_down{i}"), mode="nt", out_dtype=F32, name=f"ffn_down_dx{i}", tm=512, tn=1408)[0]
        pp.grad(f"ffn_w_down{i}", run(_mm, L["a"][None], dy[None], mode="tn", out_dtype=BF16, name=f"ffn_down_dw{i}", tm=1408))
        du, dcw, dcb = run(_conv_gate_bwd, L["u"], da, sp["conv_w"][i], sp["conv_b"][i], name=f"ffn_gate_bwd{i}")
        gs["conv_w"][i] = dcw.transpose(1, 0, 2).reshape(3, 2 * Fh)
        gs["conv_b"][i] = dcb.transpose(1, 0, 2).reshape(1, 2 * Fh)
        dh2 = run(_mm, du, pp.w(f"ffn_w_up{i}"), mode="nt", out_dtype=F32, name=f"ffn_up_dx{i}", tk=2816)[0]
        pp.grad(f"ffn_w_up{i}", run(_mm, L["h2"][None], du, mode="tn", out_dtype=BF16, name=f"ffn_up_dw{i}", out_groups=N_CHIPS, tn=1408))
        dres, dy, gs["ln_mix_g"][i], gs["ln_mix_b"][i], dmp[i]["g1"], dmp[i]["sc2"], dmp[i]["sh2"] = _ln_bwd(
            dres, L["xh1"], L["rs1"], sp["ln_mix_g"][i], L["y1"], g1, f"ln_mix_bwd{i}", (dh2, sc2, sp["ln_mix_b"][i]))
        if i == 0:
            do = run(_mm, dy[None], pp.w("fox_w_o"), mode="nt", out_dtype=BF16, name="fox_out_dx")[0]
            pp.grad("fox_w_o", run(_mm, L["o"][None], dy[None], mode="tn", out_dtype=BF16, name="fox_out_dw"))
            dq, dk, dv, dcq, dck = run(_fox_bwd, L["proj"], L["cq"], L["ck"], L["lse"], do, FH, name="fox_attn_bwd")
            dcum = dcq[:, :, 0].T + dck[:, 0, :].T
            dcum = jnp.pad(dcum, ((0, 0), (0, LANES - FH)))
            dfl, db_f = _fox_gate_bwd(dcum, L["proj"], sp["fox_b_f"], FH, "fox_gate_bwd")
            gs["fox_b_f"] = db_f
            dproj = jnp.concatenate([dq, dk, dv, dfl], axis=1)
            pp.grad("fox_w_in", run(_mm, L["h1"][None], dproj[None], mode="tn", out_dtype=BF16, name="fox_proj_dw", tn=896))
            dh1 = run(_mm, dproj[None], pp.w("fox_w_in"), mode="nt", out_dtype=F32, name="fox_proj_dx", tk=6272, tm=512)[0]
        else:
            do = run(_mm, dy[None], pp.w("swa_w_o"), mode="nt", out_dtype=BF16, name="swa_out_dx")[0]
            pp.grad("swa_w_o", run(_mm, L["o"][None], dy[None], mode="tn", out_dtype=BF16, name="swa_out_dw"))
            dqh, dkh, dvh, dsk = run(_swa_bwd, L["qh"], L["kh"], L["vh"], sp["sinks"], _to_groups(do, KH), name="swa_attn_bwd")
            gs["sinks"] = jnp.sum(dsk.reshape(QH, WIN), axis=1)
            dpr = jnp.concatenate([_from_groups(dqh), dkh.transpose(1, 0, 2).reshape(S, KH * SDH),
                                   dvh.transpose(1, 0, 2).reshape(S, KH * SDH)], axis=1)
            n_rot = (QH + KH) * SDH // LANES
            dproj = _rope(dpr, _rope_tables(pos, -1.0), n_rot, BF16, "swa_rope_bwd")
            dh1 = run(_mm, dproj[None], pp.w("swa_w_in"), mode="nt", out_dtype=F32, name="swa_proj_dx", tk=640)[0]
            pp.grad("swa_w_in", run(_mm, L["h1"][None], dproj[None], mode="tn", out_dtype=BF16, name="swa_proj_dw", out_groups=N_CHIPS, tn=640))
        pend = (dh1, sc1)
    grad_x, dmp[0]["sc1"], dmp[0]["sh1"] = _mod_bwd(pend[0], sv[0]["x_in"], pend[1], dres, "mod_mix_bwd0")
    dmod = [jnp.concatenate([p["sh1"], p["sc1"], p["g1"], p["sh2"], p["sc2"], p["g2"]], axis=1) for p in dmp]
    return loss_cols, grad_x, gs, jnp.concatenate(dmod, axis=0)


def _allgather_small(v, name):
    m_per, n = v.shape

    def body(x_ref, out_ref, send_sems, recv_sems, local_sem):
        x, y, c, chips = _place()
        me, sibling = (x, y, c), (x, y, 1 - c)

        def rows(px, py, pc):
            return out_ref.at[pl.ds((4 * px + 2 * py + pc) * m_per, m_per), :]

        def copy(k, block, to, src=None):
            return _remote(rows(*block) if src is None else src, rows(*block), send_sems.at[k], recv_sems.at[k], to)

        mine = pltpu.make_async_copy(x_ref, rows(*me), local_sem)
        mine.start()
        first = [copy(0, me, sibling, src=x_ref)]
        first += [copy(1 + j, me, (*chip, c), src=x_ref) for j, chip in enumerate(chips)]
        for cp in first:
            cp.start()
        passed = [copy(4 + j, (*chip, c), sibling) for j, chip in enumerate(chips)]
        for j, chip in enumerate(chips):
            copy(1 + j, (*chip, c), me).wait_recv()
            passed[j].start()
        copy(0, sibling, me).wait_recv()
        for j, chip in enumerate(chips):
            copy(4 + j, (*chip, 1 - c), me).wait_recv()
        for cp in first + passed:
            cp.wait_send()
        mine.wait()

    return pl.pallas_call(
        body, name=name, out_shape=jax.ShapeDtypeStruct((N_DEV * m_per, n), v.dtype),
        in_specs=[pl.BlockSpec(memory_space=pltpu.VMEM)], out_specs=pl.BlockSpec(memory_space=pltpu.VMEM),
        scratch_shapes=[pltpu.SemaphoreType.DMA((7,)), pltpu.SemaphoreType.DMA((7,)), pltpu.SemaphoreType.DMA],
        compiler_params=pltpu.CompilerParams(vmem_limit_bytes=VMEM_LIMIT),
    )(v)


def _row_tile(r, pref=256):
    return _tile(r, pref, 16)


def _cast_bf16(w, layer, chip, name, after=()):
    _, R, C = w.shape
    tr = _row_tile(R)

    def body(s_ref, w_ref, *rest):
        rest[-1][...] = w_ref[...].astype(BF16)

    return pl.pallas_call(
        body, name=name,
        grid_spec=pltpu.PrefetchScalarGridSpec(
            num_scalar_prefetch=1, grid=(R // tr,),
            in_specs=[pl.BlockSpec((None, tr, C), lambda i, s: (layer, i, 0))] + _any_specs(len(after)),
            out_specs=pl.BlockSpec((None, tr, C), lambda i, s: (s[0], i, 0))),
        out_shape=jax.ShapeDtypeStruct((N_CHIPS, R, C), BF16), compiler_params=_params(("parallel",)),
    )(jnp.reshape(chip, (1,)).astype(jnp.int32), w, *after)


def _add_sibling(g, got, c, name):
    G, R, C = g.shape
    rh = R // 2
    tr = _row_tile(rh)
    nb = rh // tr

    def body(c_ref, g_ref, o_ref, p_ref):
        p_ref[...] = (g_ref[...].astype(F32) + o_ref[...].astype(F32)).astype(BF16)

    return pl.pallas_call(
        body, name=name,
        grid_spec=pltpu.PrefetchScalarGridSpec(
            num_scalar_prefetch=1, grid=(G, nb),
            in_specs=[pl.BlockSpec((1, tr, C), lambda s, i, c_ref: (s, c_ref[0] * nb + i, 0)),
                      pl.BlockSpec((1, tr, C), lambda s, i, c_ref: (s, i, 0))],
            out_specs=pl.BlockSpec((1, tr, C), lambda s, i, c_ref: (s, i, 0))),
        out_shape=jax.ShapeDtypeStruct((G, rh, C), BF16), compiler_params=_params(("parallel", "parallel")),
    )(jnp.reshape(c, (1,)).astype(jnp.int32), g, got)


def _sum_chips(part, landed, chip, c, name):
    G, rh, C = part.shape
    tr = _row_tile(rh)
    nb = rh // tr

    def body(p_ref, own_ref, *rest):
        acc = own_ref[...].astype(F32)
        for ref in rest[:G - 1]:
            acc = acc + ref[...].astype(F32)
        rest[G - 1][...] = acc

    slot = lambda k: pl.BlockSpec((None, tr, C), lambda i, p: ((p[0] + k) % G, i, 0))
    return pl.pallas_call(
        body, name=name,
        grid_spec=pltpu.PrefetchScalarGridSpec(
            num_scalar_prefetch=1, grid=(nb,), in_specs=[slot(k) for k in range(G)],
            out_specs=pl.BlockSpec((tr, C), lambda i, p: (p[1] * nb + i, 0))),
        out_shape=jax.ShapeDtypeStruct((2 * rh, C), F32), compiler_params=_params(("parallel",)),
    )(jnp.stack([chip, c]).astype(jnp.int32), part, *([landed] * (G - 1)))


def _adam_math(w, g, m, v):
    m = ADAM_B1 * m + (1.0 - ADAM_B1) * g
    v = ADAM_B2 * v + (1.0 - ADAM_B2) * (g * g)
    m_hat = m / (1.0 - ADAM_B1 ** ADAM_STEP)
    v_hat = v / (1.0 - ADAM_B2 ** ADAM_STEP)
    delta = -ADAM_LR * (m_hat / (jnp.sqrt(v_hat) + ADAM_EPS) + ADAM_WD * w)
    return delta, m, v


def _adamw(w, g, m, v, layer, prev, name, by_cols=False, after=()):
    L, R, C = w.shape
    tr = R if by_cols else _tile(R, 128, 8)
    tc = _tile(C, 256) if by_cols else C
    n_alias = len(prev)
    prev = tuple(prev) + tuple(after)
    n_prev = len(prev)

    def body(w_ref, g_ref, m_ref, v_ref, *rest):
        go_ref, d_ref, mo_ref, vo_ref = rest[n_prev:]
        gv = g_ref[...]
        go_ref[...] = gv
        d_ref[...], mo_ref[...], vo_ref[...] = _adam_math(w_ref[...], gv, m_ref[...], v_ref[...])

    lay = pl.BlockSpec((None, tr, tc), lambda i: (layer, i // (C // tc), i % (C // tc)))
    flat = pl.BlockSpec((tr, tc), lambda i: (i // (C // tc), i % (C // tc)))
    return _call(
        body, name=name, grid=((R // tr) * (C // tc),), in_specs=[lay, flat, lay, lay] + _any_specs(n_prev), out_specs=[lay] * 4,
        out_shape=[jax.ShapeDtypeStruct((L, R, C), F32)] * 4, aliases={4 + k: k for k in range(n_alias)},
        sem=("parallel",), args=(w, g, m, v, *prev))


def _cond_rows(c_row, cw, name, after=()):
    D = c_row.shape[1]
    nr, fc = cw.shape

    def body(c_ref, e_ref, *rest):
        o_ref = rest[-1]
        o_ref[...] = jnp.zeros_like(o_ref)
        cv = c_ref[...]
        o_ref[0:1, 0:D] = cv * (1.0 / (1.0 + jnp.exp(-cv)))
        o_ref[8:8 + nr, 0:fc] = e_ref[...]

    vmem = pl.BlockSpec(memory_space=pltpu.VMEM)
    return pl.pallas_call(body, name=name, in_specs=[vmem, vmem] + _any_specs(len(after)), out_specs=vmem,
                          out_shape=jax.ShapeDtypeStruct((16, max(D, fc)), F32))(c_row, cw, *after)


def _ada_fwd(cact, ada_w, ada_b, layer, chip, name):
    _, D, NC = ada_w.shape
    tn = _tile(NC, 1024)
    nj = NC // tn

    def body(idx_ref, c_ref, w_ref, b_ref, o_ref):
        acc = jnp.dot(c_ref[...].astype(BF16), w_ref[0].astype(BF16), preferred_element_type=F32)
        o_ref[...] = acc + b_ref[pl.ds(idx_ref[0], 1), :]

    return pl.pallas_call(
        body, name=name,
        grid_spec=pltpu.PrefetchScalarGridSpec(
            num_scalar_prefetch=1, grid=(nj,),
            in_specs=[pl.BlockSpec((8, D), lambda j, idx: (0, 0)),
                      pl.BlockSpec((1, D, tn), lambda j, idx: (idx[0], 0, j)),
                      pl.BlockSpec((DEPTH, tn), lambda j, idx: (0, idx[1] * nj + j))],
            out_specs=pl.BlockSpec((8, tn), lambda j, idx: (0, j))),
        out_shape=jax.ShapeDtypeStruct((8, NC), F32), compiler_params=_params(("parallel",)),
    )(jnp.stack([layer, chip]).astype(jnp.int32), cact, ada_w, ada_b)


def _ada_grad_adamw(cact_t, dmod, w, m, v, name):
    L, D, NC = w.shape
    tr = _tile(D, 128, 8)

    def body(c_ref, d_ref, w_ref, m_ref, v_ref, g_ref, dl_ref, mo_ref, vo_ref):
        g = jnp.dot(c_ref[...], d_ref[...], preferred_element_type=F32, precision=HIGHEST)
        g_ref[...] = g
        dl_ref[...], mo_ref[...], vo_ref[...] = _adam_math(w_ref[...], g, m_ref[...], v_ref[...])

    lay = pl.BlockSpec((None, tr, NC), lambda l, i: (l, i, 0))
    return _call(
        body, name=name, grid=(L, D // tr),
        in_specs=[pl.BlockSpec((tr, N_DEV), lambda l, i: (i, 0)), pl.BlockSpec((None, N_DEV, NC), lambda l, i: (l, 0, 0)), lay, lay, lay],
        out_specs=[lay] * 4, out_shape=[jax.ShapeDtypeStruct((L, D, NC), F32)] * 4,
        sem=("parallel", "parallel"), args=(cact_t, dmod, w, m, v))


def _sum_devices(gathered, name):
    n, R, C = gathered.shape

    def body(g_ref, o_ref):
        acc = g_ref[0]
        for j in range(1, n):
            acc = acc + g_ref[j]
        o_ref[...] = acc

    return pl.pallas_call(body, name=name, out_shape=jax.ShapeDtypeStruct((R, C), F32),
                          compiler_params=pltpu.CompilerParams(vmem_limit_bytes=VMEM_LIMIT))(gathered)


def _adamw_small(w, g, m, v, name):
    def body(w_ref, g_ref, m_ref, v_ref, d_ref, mo_ref, vo_ref):
        d_ref[...], mo_ref[...], vo_ref[...] = _adam_math(w_ref[...], g_ref[...], m_ref[...], v_ref[...])

    return pl.pallas_call(body, name=name, out_shape=[jax.ShapeDtypeStruct(w.shape, F32)] * 3)(w, g, m, v)


def _pad_rows(flat, unit=8 * LANES):
    n = flat.shape[0]
    total = -(-n // unit) * unit
    return jnp.pad(flat, (0, total - n)).reshape(total // LANES, LANES)


def _pad_lanes(v2d):
    return jnp.pad(v2d.reshape(1, -1), ((0, 0), (0, LANES - v2d.size)))


FORWARD = {
    "fox_proj": ((), ("ffn_w_up0",)),
    "ffn_up0": (("ffn_w_up0",), ("ffn_w_down0", "swa_w_in")),
    "ffn_down0": (("ffn_w_down0", "swa_w_in"), ("swa_w_o", "ffn_w_up1")),
    "swa_out": (("swa_w_o", "ffn_w_up1"), ("ffn_w_down1",)),
    "ffn_down1": (("ffn_w_down1",), ()),
}
PLAN = {
    "ffn_gate_bwd1": [("swap", "ffn_w_down1")],
    "ffn_up_dx1": [("scatter", "ffn_w_down1", 0, 1, 1)],
    "swa_out_dx": [("swap", "ffn_w_up1")],
    "swa_attn_bwd": [("scatter", "ffn_w_up1", 0, 6, 8), ("swap", "swa_w_o")],
    "swa_proj_dx": [("scatter", "swa_w_o", 0, 1, 1)],
    "ffn_down_dx0": [("scatter", "ffn_w_up1", 6, 8, 8), ("swap", "swa_w_in")],
    "ffn_down_dw0": [("scatter", "swa_w_in", 0, 1, 1)],
    "ffn_gate_bwd0": [("swap", "ffn_w_down0")],
    "ffn_up_dx0": [("scatter", "ffn_w_down0", 0, 1, 1)],
    "fox_out_dx": [("swap", "ffn_w_up0")],
    "fox_attn_bwd": [("scatter", "ffn_w_up0", 0, 5, 8), ("swap", "fox_w_o")],
    "fox_proj_dw": [("scatter", "fox_w_o", 0, 1, 1), ("scatter", "ffn_w_up0", 5, 6, 8)],
    "fox_proj_dx": [("scatter", "ffn_w_up0", 6, 8, 8), ("swap", "fox_w_in")],
}


class Exchanges:
    def __init__(self, dm, slots, chip, c):
        self.dm, self.slots, self.chip, self.c = dm, dict(slots), chip, c
        self.raw, self.part, self.landed, self.grads, self.views, self.pending = {}, {}, {}, {}, {}, {}

    def gather_start(self, keys, name, after):
        self.first = (keys, _gather_comm([self.slots[k] for k in keys]))
        self.first_state, token = _split_start(self.first[1], name + "_start", after)
        return token

    def gather_finish(self, after, name):
        keys, comm = self.first
        _split_wait(comm, self.first_state, after, name + "_wait")
        pass_on = _forward_comm(comm.results)
        _run_comm(pass_on, name + "_pass")
        self.slots.update(zip(keys, pass_on.results))
        self.fence = list(pass_on.results)

    def before(self, stage, last):
        need, nxt = FORWARD[stage]
        if need:
            self.gather_finish(list(last), "gather_" + "_".join(need))
        if nxt:
            return self.gather_start(list(nxt), "gather_" + "_".join(nxt), list(last) + self.fence)
        return None

    def w(self, key):
        if key not in self.views:
            S, D, FH, QH, KH, Fh = self.dm
            full = self.slots[key]
            if key == "fox_w_in":
                cols = full.shape[2]
                full = jnp.pad(full.transpose(1, 0, 2).reshape(D, N_CHIPS * cols), ((0, 0), (0, 3 * D + LANES - N_CHIPS * cols)))[None]
            elif key in ("fox_w_o", "swa_w_o"):
                full = full.reshape(1, D, D)
            elif key.startswith("ffn_w_down"):
                full = full.reshape(1, Fh, D)
            self.views[key] = full
        return self.views[key]

    def carry(self, stage, last=()):
        todo = []
        token = self.before(stage, last) if stage in FORWARD else None
        if token is not None:
            todo.append(("order", [], Comm([token], [], {}, 1, lambda *refs: None, lambda *refs: None)))
        for kind, key, *chunk in PLAN.get(stage, ()):
            if kind == "swap":
                todo.append((kind, [key], _swap_comm([self.raw[key]])))
            elif kind == "scatter":
                todo.append((kind, [(key, *chunk)], _scatter_comm([self.part[key]], [self.landed.get(key)], [tuple(chunk)])))
        self.pending[stage] = todo
        return _merge([cm for _, _, cm in todo])

    def carried(self, stage, comm):
        for kind, keys, cm in self.pending.pop(stage):
            if kind == "swap":
                self.part[keys[0]] = _add_sibling(self.raw[keys[0]], cm.results[0], self.c, f"add_sibling_{keys[0]}")
            elif kind == "scatter":
                self.landed[keys[0][0]] = cm.results[0]

    def grad(self, key, g):
        S, D, FH, QH, KH, Fh = self.dm
        if key == "fox_w_in":
            cols = self.slots[key].shape[2]
            g = g[0][:, :N_CHIPS * cols].reshape(D, N_CHIPS, cols).transpose(1, 0, 2)
        elif key in ("fox_w_o", "swa_w_o"):
            g = g.reshape(N_CHIPS, D // N_CHIPS, D)
        elif key.startswith("ffn_w_down"):
            g = g.reshape(N_CHIPS, Fh // N_CHIPS, D)
        self.raw[key] = g

    def last_start(self, last, after):
        part = self.part[last]
        self.last = (last, _scatter_comm([part], [lax.empty(part.shape, part.dtype)], [(0, 1, 1)]))
        self.last_state, token = _split_start(self.last[1], "grads_last_start", after)
        return token

    def join_landed(self):
        keys = list(self.landed)
        join = _join_comm([_sum_chips(self.part[k], self.landed[k], self.chip, self.c, f"sum_chips_{k}") for k in keys])
        _run_comm(join, "grads_join")
        return dict(zip(keys, join.results))

    def last_finish(self, after):
        last, comm = self.last
        _split_wait(comm, self.last_state, after, "grads_last_wait")
        join = _join_comm([_sum_chips(self.part[last], comm.results[0], self.chip, self.c, f"sum_chips_{last}")])
        _run_comm(join, "grads_join_last")
        return join.results[0]


def _step(dm, a):
    S, D, FH, QH, KH, Fh = dm
    ix, iy, ic = lax.axis_index("x"), lax.axis_index("y"), lax.axis_index("c")
    chip = 2 * ix + iy
    dev = 2 * chip + ic
    F2c = a["ffn_w_up"].shape[2]
    NC = a["ada_w"].shape[2]

    names = ["fox_w_in", "fox_w_o", "swa_w_in", "swa_w_o", "ffn_w_up", "ffn_w_up", "ffn_w_down", "ffn_w_down"]
    layers = [0, 0, 0, 0, 0, 1, 0, 1]
    keys = ["fox_w_in", "fox_w_o", "swa_w_in", "swa_w_o", "ffn_w_up0", "ffn_w_up1", "ffn_w_down0", "ffn_w_down1"]
    cast = lambda t, after: _cast_bf16(a[names[t]], layers[t], chip, f"cast_{keys[t]}", after)
    pp = Exchanges(dm, {keys[t]: cast(t, ()) for t in (0, 1)}, chip, ic)

    e0 = _cond_rows(a["c"], a["ffn_conv_w"].reshape(DEPTH * 3, F2c), "silu_c", [pp.slots[k] for k in keys[:2]])
    g0 = _allgather_small(e0, "gather_cond").reshape(N_DEV, 16, e0.shape[1])
    cact = g0[:, 0, :D]
    conv_w = g0[0::2, 8:8 + DEPTH * 3, :F2c].transpose(1, 0, 2).reshape(DEPTH, 3, N_CHIPS * F2c)
    rows = _ada_fwd(cact, a["ada_w"], a["ada_b"], ic, chip, "ada_proj")
    g1 = _allgather_small(rows, "gather_mod").reshape(N_CHIPS, DEPTH, 8, NC)
    mod = lax.dynamic_index_in_dim(g1, dev, axis=2, keepdims=False).transpose(1, 0, 2).reshape(DEPTH, N_CHIPS * NC)

    token = pp.gather_start(keys[:2], "gather_fox", [mod])
    pp.slots.update({keys[t]: cast(t, (token,)) for t in range(2, len(keys))})
    h0 = _modulate(a["x"][0], mod[0:1, D:2 * D], mod[0:1, 0:D], "mod_in")
    pp.gather_finish([pp.slots[k] for k in keys[2:]] + [h0], "gather_fox")
    sp = {"fox_b_f": _pad_lanes(a["fox_b_f"]), "sinks": jnp.repeat(a["swa_sinks"].reshape(KH, QH // KH), WIN, axis=1)[:, :, None],
          "conv_w": [conv_w[i] for i in range(DEPTH)], "conv_b": [a["ffn_conv_b"][i:i + 1] for i in range(DEPTH)]}
    for nm in ("ln_mix_g", "ln_mix_b", "ln_ffn_g", "ln_ffn_b"):
        sp[nm] = [a[nm][i:i + 1] for i in range(DEPTH)]

    loss_cols, grad_x, gs, dmod = _local_step(dm, a["x"][0], a["loss_target"][0], a["positions"][0], mod, sp, pp, h0)
    loss = lax.psum(0.5 / D * jnp.sum(loss_cols), ("x", "y", "c"))
    out = {"loss": loss, "grad_x": grad_x[None]}

    pieces = [dmod.reshape(-1), gs["fox_b_f"].reshape(-1), _pad_lanes(gs["sinks"]).reshape(-1),
              jnp.stack(gs["conv_w"]).reshape(-1), jnp.stack(gs["conv_b"]).reshape(-1)]
    pieces += [jnp.stack(gs[nm]).reshape(-1) for nm in ("ln_mix_g", "ln_mix_b", "ln_ffn_g", "ln_ffn_b")]
    sizes = [p.shape[0] for p in pieces]
    packed = _pad_rows(jnp.concatenate(pieces))
    allp = _allgather_small(packed, "gather_small").reshape(N_DEV, packed.shape[0], LANES)
    tot = _sum_devices(allp, "sum_small").reshape(-1)
    offs = [sum(sizes[:k]) for k in range(len(sizes))]
    take = lambda k: tot[offs[k]:offs[k] + sizes[k]]
    g_small = {"ada_b": take(0).reshape(DEPTH, -1), "fox_b_f": take(1)[:FH].reshape(1, FH), "swa_sinks": take(2)[:QH].reshape(1, QH),
               "ffn_conv_w": lax.dynamic_slice_in_dim(take(3).reshape(DEPTH, 3, N_CHIPS * F2c), chip * F2c, F2c, axis=2),
               "ffn_conv_b": take(4).reshape(DEPTH, -1)}
    for k, nm in enumerate(("ln_mix_g", "ln_mix_b", "ln_ffn_g", "ln_ffn_b")):
        g_small[nm] = take(5 + k).reshape(DEPTH, D)
    small = list(g_small)
    pack = lambda pre: _pad_rows(jnp.concatenate([(a[pre + nm] if pre else a[nm]).reshape(-1) for nm in small]))
    gp = _pad_rows(jnp.concatenate([g_small[nm].reshape(-1) for nm in small]))
    ds_, ms_, vs_ = _adamw_small(pack(""), gp, pack("m_"), pack("v_"), "adamw_small")
    off = 0
    for nm in small:
        n_el = a[nm].size
        out["grad_" + nm] = g_small[nm]
        for pre, arr in (("delta_", ds_), ("new_m_", ms_), ("new_v_", vs_)):
            out[pre + nm] = arr.reshape(-1)[off:off + n_el].reshape(a[nm].shape)
        off += n_el

    dmod_all = allp.reshape(N_DEV, -1)[:, :DEPTH * N_CHIPS * NC].reshape(N_DEV, DEPTH, N_CHIPS * NC)
    dmod_mine = lax.dynamic_slice_in_dim(dmod_all, chip * NC, NC, axis=2).transpose(1, 0, 2)

    grads = pp.join_landed()
    token = pp.last_start("fox_w_in", [ds_, dmod_mine] + list(grads.values()))
    ada = _ada_grad_adamw(cact.T, dmod_mine + token[0, 0], a["ada_w"], a["m_ada_w"], a["v_ada_w"], "ada_grad")
    for pre, arr in zip(("grad_", "delta_", "new_m_", "new_v_"), ada):
        out[pre + "ada_w"] = arr
    upd = {}
    for k, nm, l in zip(keys[1:], names[1:], layers[1:]):
        upd[nm] = _adamw(a[nm], grads[k], a["m_" + nm], a["v_" + nm], l, upd.get(nm, ()), f"adamw_{k}", after=(token,))
    g_last = pp.last_finish([ada[1]] + [res[1] for res in upd.values()])
    tview = lambda t: jnp.swapaxes(t, 1, 2)
    res = _adamw(tview(a["fox_w_in"]), g_last.T, tview(a["m_fox_w_in"]), tview(a["v_fox_w_in"]), 0, (), "adamw_fox_w_in", by_cols=True)
    upd["fox_w_in"] = [tview(r) for r in res]
    for nm, res in upd.items():
        for pre, arr in zip(("grad_", "delta_", "new_m_", "new_v_"), res):
            out[pre + nm] = arr
    return out


_WEIGHTS = ["fox_w_in", "fox_b_f", "fox_w_o", "swa_w_in", "swa_sinks", "swa_w_o", "ada_w", "ada_b", "ffn_w_up", "ffn_conv_w",
            "ffn_conv_b", "ffn_w_down", "ln_mix_g", "ln_mix_b", "ln_ffn_g", "ln_ffn_b"]
_INPUTS = (["x", "c", "positions"] + _WEIGHTS + ["loss_target"] + ["m_" + w for w in _WEIGHTS] + ["v_" + w for w in _WEIGHTS])


def kernel(x, c, positions, fox_w_in, fox_b_f, fox_w_o, swa_w_in, swa_sinks, swa_w_o, ada_w, ada_b, ffn_w_up, ffn_conv_w, ffn_conv_b, ffn_w_down, ln_mix_g, ln_mix_b, ln_ffn_g, ln_ffn_b, loss_target, m_fox_w_in, m_fox_b_f, m_fox_w_o, m_swa_w_in, m_swa_sinks, m_swa_w_o, m_ada_w, m_ada_b, m_ffn_w_up, m_ffn_conv_w, m_ffn_conv_b, m_ffn_w_down, m_ln_mix_g, m_ln_mix_b, m_ln_ffn_g, m_ln_ffn_b, v_fox_w_in, v_fox_b_f, v_fox_w_o, v_swa_w_in, v_swa_sinks, v_swa_w_o, v_ada_w, v_ada_b, v_ffn_w_up, v_ffn_conv_w, v_ffn_conv_b, v_ffn_w_down, v_ln_mix_g, v_ln_mix_b, v_ln_ffn_g, v_ln_ffn_b):
    args = (x, c, positions, fox_w_in, fox_b_f, fox_w_o, swa_w_in, swa_sinks, swa_w_o, ada_w, ada_b, ffn_w_up, ffn_conv_w, ffn_conv_b, ffn_w_down, ln_mix_g, ln_mix_b, ln_ffn_g, ln_ffn_b, loss_target, m_fox_w_in, m_fox_b_f, m_fox_w_o, m_swa_w_in, m_swa_sinks, m_swa_w_o, m_ada_w, m_ada_b, m_ffn_w_up, m_ffn_conv_w, m_ffn_conv_b, m_ffn_w_down, m_ln_mix_g, m_ln_mix_b, m_ln_ffn_g, m_ln_ffn_b, v_fox_w_in, v_fox_b_f, v_fox_w_o, v_swa_w_in, v_swa_sinks, v_swa_w_o, v_ada_w, v_ada_b, v_ffn_w_up, v_ffn_conv_w, v_ffn_conv_b, v_ffn_w_down, v_ln_mix_g, v_ln_mix_b, v_ln_ffn_g, v_ln_ffn_b)
    out = _step(PROD, dict(zip(_INPUTS, args)))
    order = ["loss", "grad_x"] + [p + w for p in ("grad_", "delta_", "new_m_", "new_v_") for w in _WEIGHTS]
    return tuple(out[k] for k in order)
```

```python
import functools
from typing import NamedTuple

import jax
import jax.numpy as jnp
from jax import lax
from jax.experimental import pallas as pl
from jax.experimental.pallas import tpu as pltpu

F32 = jnp.float32
BF16 = jnp.bfloat16
MESH = pl.DeviceIdType.MESH
HIGHEST = lax.Precision.HIGHEST

N_CHIPS = 4
N_DEV = 8
LANES = 128
VMEM_LIMIT = 56 * 1024 * 1024

DEPTH = 2
DEEPNORM_ALPHA = (2.0 * DEPTH) ** 0.25
LN_EPS = 1e-5
ROPE_THETA = 500000.0
ADAM_LR, ADAM_B1, ADAM_B2, ADAM_EPS, ADAM_WD, ADAM_STEP = 0.001, 0.9, 0.999, 1e-08, 0.01, 10
NEG = -1e30


class Dims(NamedTuple):
    S: int
    D: int
    FH: int
    QH: int
    KH: int
    F: int


PROD = Dims(S=2048, D=2048, FH=16, QH=32, KH=4, F=5632)
FDH = 128
SDH = 64
WIN = 128
ROPE_DIM = 16
FOX_TQ = 256


def _params(sem=None, vmem=VMEM_LIMIT):
    return pltpu.CompilerParams(dimension_semantics=sem, vmem_limit_bytes=vmem)


def _tile(n, pref, unit=LANES):
    if n <= pref:
        return n
    t = (pref // unit) * unit
    while t > 0:
        if n % t == 0:
            return t
        t -= unit
    return n


class Comm:
    def __init__(self, args, out_shapes, aliases, n_sem, start, finish, members=()):
        self.args, self.out_shapes, self.aliases, self.n_sem = list(args), list(out_shapes), dict(aliases), n_sem
        self.start, self.finish = start, finish
        self.members = members
        self.results = None

    def set_results(self, res):
        self.results = list(res)
        for cm, o0 in self.members:
            cm.set_results(self.results[o0:o0 + len(cm.out_shapes)])


class _SemView:
    def __init__(self, sems, first):
        self.sems, self.first = sems, first

    @property
    def at(self):
        return self

    def __getitem__(self, k):
        return self.sems.at[self.first + k]


def _merge(comms):
    comms = [cm for cm in comms if cm is not None]
    if len(comms) < 2:
        return comms[0] if comms else None
    args, shapes, aliases, spans, n_sem = [], [], {}, [], 0
    for cm in comms:
        spans.append((len(args), len(shapes), n_sem))
        aliases.update({len(args) + a: len(shapes) + o for a, o in cm.aliases.items()})
        args += cm.args
        shapes += cm.out_shapes
        n_sem += cm.n_sem

    def each(step):
        def run(ar, ou, send, recv):
            for cm, (a0, o0, s0) in zip(comms, spans):
                getattr(cm, step)(ar[a0:a0 + len(cm.args)], ou[o0:o0 + len(cm.out_shapes)], _SemView(send, s0), _SemView(recv, s0))
        return run

    return Comm(args, shapes, aliases, n_sem, each("start"), each("finish"), [(cm, o0) for cm, (_, o0, _) in zip(comms, spans)])


def _place():
    x, y, c = lax.axis_index("x"), lax.axis_index("y"), lax.axis_index("c")
    chips = [(1 - x, y), (x, 1 - y), (1 - x, 1 - y)]
    return x, y, c, chips


def _remote(src, dst, send, recv, to):
    return pltpu.make_async_remote_copy(src_ref=src, dst_ref=dst, send_sem=send, recv_sem=recv, device_id=to, device_id_type=MESH)


def _any_specs(n):
    return [pl.BlockSpec(memory_space=pl.ANY)] * n


def _call(body, *, name, grid, in_specs, out_specs, out_shape, args, sem, scratch_shapes=(), aliases=None, comm=None):
    in_specs, out_specs, out_shape, scratch_shapes = list(in_specs), list(out_specs), list(out_shape), list(scratch_shapes)
    aliases = dict(aliases or {})
    if comm is None:
        return pl.pallas_call(body, name=name, grid=grid, in_specs=in_specs, out_specs=out_specs, out_shape=out_shape,
                              scratch_shapes=scratch_shapes, input_output_aliases=aliases, compiler_params=_params(sem))(*args)
    n_in, n_out, nc_in, nc_out, n_scr = len(in_specs), len(out_specs), len(comm.args), len(comm.out_shapes), len(scratch_shapes)

    def wrapped(*refs):
        ins, refs = refs[:n_in], refs[n_in:]
        cin, refs = refs[:nc_in], refs[nc_in:]
        outs, refs = refs[:n_out], refs[n_out:]
        cout, refs = refs[:nc_out], refs[nc_out:]
        scratch, (send, recv) = refs[:n_scr], refs[n_scr:]
        ids = [pl.program_id(k) for k in range(len(grid))]
        first = functools.reduce(jnp.logical_and, [i == 0 for i in ids])
        last = functools.reduce(jnp.logical_and, [i == g - 1 for i, g in zip(ids, grid)])

        @pl.when(first)
        def _():
            comm.start(cin, cout, send, recv)

        body(*ins, *outs, *scratch)

        @pl.when(last)
        def _():
            comm.finish(cin, cout, send, recv)

    res = pl.pallas_call(
        wrapped, name=name, grid=grid, in_specs=in_specs + _any_specs(nc_in), out_specs=out_specs + _any_specs(nc_out),
        out_shape=out_shape + comm.out_shapes,
        scratch_shapes=scratch_shapes + [pltpu.SemaphoreType.DMA((comm.n_sem,)), pltpu.SemaphoreType.DMA((comm.n_sem,))],
        input_output_aliases={**aliases, **{n_in + a: n_out + o for a, o in comm.aliases.items()}},
        compiler_params=_params(("arbitrary",) * len(grid)),
    )(*args, *comm.args)
    comm.set_results(res[n_out:])
    return list(res[:n_out])


def _run_comm(comm, name):
    nc_in, nc_out = len(comm.args), len(comm.out_shapes)

    def body(*refs):
        cin, cout, (send, recv) = refs[:nc_in], refs[nc_in:nc_in + nc_out], refs[nc_in + nc_out:]
        comm.start(cin, cout, send, recv)
        comm.finish(cin, cout, send, recv)

    res = pl.pallas_call(
        body, name=name, in_specs=_any_specs(nc_in), out_specs=_any_specs(nc_out), out_shape=comm.out_shapes,
        scratch_shapes=[pltpu.SemaphoreType.DMA((comm.n_sem,)), pltpu.SemaphoreType.DMA((comm.n_sem,))],
        input_output_aliases=comm.aliases,
    )(*comm.args)
    comm.set_results(res)


_HBM = pl.BlockSpec(memory_space=pltpu.HBM)
_SEM = pl.BlockSpec(memory_space=pltpu.SEMAPHORE)
_EFFECT = pltpu.SideEffectType.DATAFLOW_SIDE_EFFECTING


def _split_start(comm, name, after=()):
    n = len(comm.args)
    back = {o: a for a, o in comm.aliases.items()}
    assert len(back) == len(comm.out_shapes)

    n_after = len(after)

    def body(*refs):
        refs = refs[n + n_after:]
        send, recv, thru, token = refs[0], refs[1], refs[2:n + 2], refs[n + 2]
        comm.start(thru, [thru[back[o]] for o in range(len(back))], send, recv)
        token[...] = jnp.zeros_like(token)

    res = pl.pallas_call(
        body, name=name,
        out_shape=(pltpu.SemaphoreType.DMA((comm.n_sem,)), pltpu.SemaphoreType.DMA((comm.n_sem,)),
                   *[pltpu.HBM(a.shape, a.dtype) for a in comm.args], jax.ShapeDtypeStruct((8, LANES), F32)),
        in_specs=[_HBM] * n + _any_specs(n_after), out_specs=(_SEM, _SEM, *[_HBM] * n, pl.BlockSpec(memory_space=pltpu.VMEM)),
        input_output_aliases={i: 2 + i for i in range(n)},
        compiler_params=pltpu.CompilerParams(has_side_effects=_EFFECT),
    )(*[pltpu.with_memory_space_constraint(a, pltpu.HBM) for a in comm.args], *after)
    return (res[0], res[1], list(res[2:2 + n])), res[2 + n]


def _split_wait(comm, state, after, name):
    send, recv, thru = state
    n, n_after = len(thru), len(after)
    back = {o: a for a, o in comm.aliases.items()}

    def body(*refs):
        ins, send_ref, recv_ref = refs[:n], refs[n], refs[n + 1]
        comm.finish(ins, [ins[back[o]] for o in range(len(back))], send_ref, recv_ref)

    res = pl.pallas_call(
        body, name=name, out_shape=tuple(pltpu.HBM(a.shape, a.dtype) for a in thru),
        in_specs=[_HBM] * n + [_SEM, _SEM] + _any_specs(n_after), out_specs=[_HBM] * n,
        input_output_aliases={i: i for i in range(n)}, compiler_params=pltpu.CompilerParams(has_side_effects=_EFFECT),
    )(*thru, send, recv, *after)
    comm.set_results([res[back[o]] for o in range(len(back))])


def _forward_comm(slots):
    n = len(slots)

    def rows(t, who):
        rh = slots[t].shape[1] // 2
        return pl.ds(who * rh, rh)

    def copy(outs, send, recv, t, j, chip, who):
        x, y, c, _ = _place()
        blk = outs[t].at[2 * chip[0] + chip[1], rows(t, who)]
        return _remote(blk, blk, send.at[3 * t + j], recv.at[3 * t + j], (x, y, 1 - c))

    def start(args, outs, send, recv):
        _, _, c, chips = _place()
        for t in range(n):
            for j, chip in enumerate(chips):
                copy(outs, send, recv, t, j, chip, c).start()

    def finish(args, outs, send, recv):
        _, _, c, chips = _place()
        for t in range(n):
            for j, chip in enumerate(chips):
                copy(outs, send, recv, t, j, chip, 1 - c).wait_recv()
        for t in range(n):
            for j, chip in enumerate(chips):
                copy(outs, send, recv, t, j, chip, c).wait_send()

    shapes = [jax.ShapeDtypeStruct(w.shape, w.dtype) for w in slots]
    return Comm(slots, shapes, {t: t for t in range(n)}, 3 * n, start, finish)


def _gather_comm(slots):
    n = len(slots)

    def rows(t, who):
        rh = slots[t].shape[1] // 2
        return pl.ds(who * rh, rh)

    def start(args, outs, send, recv):
        x, y, c, chips = _place()
        for t in range(n):
            mine = outs[t].at[2 * x + y, rows(t, c)]
            for j, chip in enumerate(chips):
                _remote(mine, mine, send.at[3 * t + j], recv.at[3 * t + j], (*chip, c)).start()

    def finish(args, outs, send, recv):
        x, y, c, chips = _place()
        for t in range(n):
            for j, chip in enumerate(chips):
                blk = outs[t].at[2 * chip[0] + chip[1], rows(t, c)]
                _remote(blk, blk, send.at[3 * t + j], recv.at[3 * t + j], (*chip, c)).wait_recv()
        for t in range(n):
            mine = outs[t].at[2 * x + y, rows(t, c)]
            for j, chip in enumerate(chips):
                _remote(mine, mine, send.at[3 * t + j], recv.at[3 * t + j], (*chip, c)).wait_send()

    shapes = [jax.ShapeDtypeStruct(w.shape, w.dtype) for w in slots]
    return Comm(slots, shapes, {t: t for t in range(n)}, 3 * n, start, finish)


def _scatter_comm(parts, landed, chunks):
    n = len(parts)
    prev = [t for t in range(n) if landed[t] is not None]

    def rows(t):
        lo, hi, nch = chunks[t]
        rc = parts[t].shape[1] // nch
        return pl.ds(lo * rc, (hi - lo) * rc)

    def start(args, outs, send, recv):
        x, y, c, chips = _place()
        s = 2 * x + y
        for t in range(n):
            for j, chip in enumerate(chips):
                _remote(args[t].at[2 * chip[0] + chip[1], rows(t)], outs[t].at[s, rows(t)],
                        send.at[3 * t + j], recv.at[3 * t + j], (*chip, c)).start()

    def finish(args, outs, send, recv):
        x, y, c, chips = _place()
        for t in range(n):
            for j, chip in enumerate(chips):
                blk = outs[t].at[2 * chip[0] + chip[1], rows(t)]
                _remote(blk, blk, send.at[3 * t + j], recv.at[3 * t + j], (*chip, c)).wait_recv()
        for t in range(n):
            for j, chip in enumerate(chips):
                src = args[t].at[2 * chip[0] + chip[1], rows(t)]
                _remote(src, src, send.at[3 * t + j], recv.at[3 * t + j], (*chip, c)).wait_send()

    shapes = [jax.ShapeDtypeStruct(p.shape, p.dtype) for p in parts]
    return Comm(list(parts) + [landed[t] for t in prev], shapes, {n + i: t for i, t in enumerate(prev)}, 3 * n, start, finish)


def _swap_comm(gs):
    n = len(gs)

    def copy(args, outs, send, recv, t):
        _, _, c, _ = _place()
        rh = gs[t].shape[1] // 2
        x, y = lax.axis_index("x"), lax.axis_index("y")
        return _remote(args[t].at[:, pl.ds((1 - c) * rh, rh), :], outs[t], send.at[t], recv.at[t], (x, y, 1 - c))

    def start(args, outs, send, recv):
        for t in range(n):
            copy(args, outs, send, recv, t).start()

    def finish(args, outs, send, recv):
        for t in range(n):
            copy(args, outs, send, recv, t).wait()

    shapes = [jax.ShapeDtypeStruct((g.shape[0], g.shape[1] // 2, g.shape[2]), g.dtype) for g in gs]
    return Comm(gs, shapes, {}, n, start, finish)


def _join_comm(gs):
    n = len(gs)

    def half(outs, t, who):
        rh = gs[t].shape[0] // 2
        return outs[t].at[pl.ds(who * rh, rh), :]

    def start(args, outs, send, recv):
        x, y, c, _ = _place()
        for t in range(n):
            _remote(half(outs, t, c), half(outs, t, c), send.at[t], recv.at[t], (x, y, 1 - c)).start()

    def finish(args, outs, send, recv):
        x, y, c, _ = _place()
        for t in range(n):
            _remote(half(outs, t, 1 - c), half(outs, t, 1 - c), send.at[t], recv.at[t], (x, y, 1 - c)).wait_recv()
        for t in range(n):
            _remote(half(outs, t, c), half(outs, t, c), send.at[t], recv.at[t], (x, y, 1 - c)).wait_send()

    shapes = [jax.ShapeDtypeStruct(g.shape, g.dtype) for g in gs]
    return Comm(gs, shapes, {t: t for t in range(n)}, n, start, finish)


_DN = {"nn": (((1,), (0,)), ((), ())), "nt": (((1,), (1,)), ((), ())), "tn": (((0,), (0,)), ((), ()))}


def _mm(a, b, *, mode, out_dtype, name, out_groups=1, tm=1024, tn=1024, tk=2048, comm=None):
    ga, ra, ca = a.shape
    gb, rb, cb = b.shape
    if mode == "nn":
        M, K, N = ra, ga * ca, gb * cb
        assert rb == K and ga == 1 or (rb == K)
    elif mode == "nt":
        M, K, N = ra, ga * ca, rb
        assert gb * cb == K
    else:
        K, M, N = ra, ga * ca, gb * cb
        assert rb == K
    go = out_groups
    if mode == "nn":
        tk = _tile(ca, tk); assert rb % tk == 0 and (ga == 1 or True)
        tn = _tile(min(cb, N // go), tn); tm = _tile(M, tm, 8)
    elif mode == "nt":
        tk = _tile(ca, tk); tk = _tile(cb, tk) if cb % tk else tk; assert ca % tk == 0 and cb % tk == 0
        tn = _tile(N // go, tn); tm = _tile(M, tm, 8)
    else:
        tk = _tile(K, tk, 8); tm = _tile(ca, tm); tn = _tile(min(cb, N // go), tn)
    assert (N // go) % tn == 0 and M % tm == 0 and K % tk == 0, (name, M, N, K, tm, tn, tk)
    nk = K // tk
    kpa = max(ca // tk, 1)
    kpb = max(cb // tk, 1)
    npb = max(cb // tn, 1)
    npo = (N // go) // tn
    mpa = max(ca // tm, 1)

    if mode == "nn":
        a_spec = pl.BlockSpec((1, tm, tk), lambda j, i, k: (k // kpa, i, k % kpa))
        b_spec = pl.BlockSpec((1, tk, tn), lambda j, i, k: (j // npb, k, j % npb))
    elif mode == "nt":
        a_spec = pl.BlockSpec((1, tm, tk), lambda j, i, k: (k // kpa, i, k % kpa))
        b_spec = pl.BlockSpec((1, tn, tk), lambda j, i, k: (k // kpb, j, k % kpb))
    else:
        a_spec = pl.BlockSpec((1, tk, tm), lambda j, i, k: (i // mpa, k, i % mpa))
        b_spec = pl.BlockSpec((1, tk, tn), lambda j, i, k: (j // npb, k, j % npb))
    o_spec = pl.BlockSpec((1, tm, tn), lambda j, i, k: (j // npo, i, j % npo))
    dn = _DN[mode]

    def body(a_ref, b_ref, o_ref, *acc):
        p = lax.dot_general(a_ref[0], b_ref[0], dn, preferred_element_type=F32)
        if nk == 1:
            o_ref[0] = p.astype(out_dtype)
        else:
            k = pl.program_id(2)

            @pl.when(k == 0)
            def _():
                acc[0][...] = p

            @pl.when(k > 0)
            def _():
                acc[0][...] += p

            @pl.when(k == nk - 1)
            def _():
                o_ref[0] = acc[0][...].astype(out_dtype)

    return _call(
        body, name=name, grid=(N // tn, M // tm, nk), in_specs=[a_spec, b_spec], out_specs=[o_spec],
        out_shape=[jax.ShapeDtypeStruct((go, M, N // go), out_dtype)],
        scratch_shapes=[pltpu.VMEM((tm, tn), F32)] if nk > 1 else [],
        sem=("parallel", "parallel", "arbitrary"), args=(a, b), comm=comm)[0]


def _rows(tr, d):
    return pl.BlockSpec((tr, d), lambda i: (i, 0))


def _vec(d):
    return pl.BlockSpec((1, d), lambda i: (0, 0))


def _modulate(x, sc, sh, name):
    S, D = x.shape
    tr = min(256, S)

    def body(x_ref, sc_ref, sh_ref, h_ref):
        h_ref[...] = (x_ref[...] * (1.0 + sc_ref[...]) + sh_ref[...]).astype(BF16)

    return pl.pallas_call(
        body, name=name, grid=(S // tr,), in_specs=[_rows(tr, D), _vec(D), _vec(D)], out_specs=_rows(tr, D),
        out_shape=jax.ShapeDtypeStruct((S, D), BF16), compiler_params=_params(("parallel",)),
    )(x, sc, sh)


def _ln_fwd(x, y, gate, gamma, beta, sc, sh, name, comm=None):
    S, D = x.shape
    tr = min(256, S)
    emit_h = sc is not None

    def body(*refs):
        if emit_h:
            x_ref, y_ref, g_ref, ga_ref, be_ref, sc_ref, sh_ref, xo_ref, xh_ref, rs_ref, h_ref = refs
        else:
            x_ref, y_ref, g_ref, ga_ref, be_ref, xo_ref, xh_ref, rs_ref = refs
        z = DEEPNORM_ALPHA * x_ref[...] + (1.0 + g_ref[...]) * y_ref[...]
        mu = jnp.mean(z, axis=-1, keepdims=True)
        zc = z - mu
        var = jnp.mean(zc * zc, axis=-1, keepdims=True)
        rstd = lax.rsqrt(var + LN_EPS)
        xh = zc * rstd
        xo = xh * ga_ref[...] + be_ref[...]
        xo_ref[...] = xo
        xh_ref[...] = xh
        rs_ref[...] = rstd
        if emit_h:
            h_ref[...] = (xo * (1.0 + sc_ref[...]) + sh_ref[...]).astype(BF16)

    ins = [x, y, gate, gamma, beta] + ([sc, sh] if emit_h else [])
    in_specs = [_rows(tr, D), _rows(tr, D)] + [_vec(D)] * (len(ins) - 2)
    out_shape = [jax.ShapeDtypeStruct((S, D), F32), jax.ShapeDtypeStruct((S, D), F32), jax.ShapeDtypeStruct((S, 1), F32)]
    out_specs = [_rows(tr, D), _rows(tr, D), _rows(tr, 1)]
    if emit_h:
        out_shape.append(jax.ShapeDtypeStruct((S, D), BF16))
        out_specs.append(_rows(tr, D))
    return _call(body, name=name, grid=(S // tr,), in_specs=in_specs, out_specs=out_specs, out_shape=out_shape,
                 sem=("parallel",), args=ins, comm=comm)


def _loss_head(xf, tgt, name):
    S, D = xf.shape
    tr = min(256, S)

    def body(x_ref, t_ref, dx_ref, l_ref):
        e = x_ref[...] - t_ref[...]
        dx_ref[...] = e * (1.0 / D)

        @pl.when(pl.program_id(0) == 0)
        def _():
            l_ref[...] = jnp.zeros_like(l_ref)

        l_ref[...] += jnp.sum(e * e, axis=0, keepdims=True)

    return pl.pallas_call(
        body, name=name, grid=(S // tr,), in_specs=[_rows(tr, D), _rows(tr, D)],
        out_specs=[_rows(tr, D), _vec(D)],
        out_shape=[jax.ShapeDtypeStruct((S, D), F32), jax.ShapeDtypeStruct((1, D), F32)],
        compiler_params=_params(("arbitrary",)),
    )(xf, tgt)


def _ln_bwd(dxo, xh, rstd, gamma, y, gate, name, pre=None):
    S, D = dxo.shape
    tr = min(256, S)
    n_pre = 0 if pre is None else 3

    def body(dx_ref, xh_ref, rs_ref, ga_ref, y_ref, g_ref, *rest):
        dres_ref, dy_ref, dga_ref, dbe_ref, dg_ref = rest[n_pre:n_pre + 5]
        first = pl.program_id(0) == 0
        dxo_ = dx_ref[...]
        xh_ = xh_ref[...]
        if pre is not None:
            dh_ref, sc_ref, be_ref = rest[:3]
            dsc_ref, dsh_ref = rest[n_pre + 5:]
            dh_ = dh_ref[...]
            dxo_ = dxo_ + dh_ * (1.0 + sc_ref[...])

            @pl.when(first)
            def _():
                dsc_ref[...] = jnp.zeros_like(dsc_ref)
                dsh_ref[...] = jnp.zeros_like(dsh_ref)

            dsc_ref[...] += jnp.sum(dh_ * (xh_ * ga_ref[...] + be_ref[...]), axis=0, keepdims=True)
            dsh_ref[...] += jnp.sum(dh_, axis=0, keepdims=True)
        dxh = dxo_ * ga_ref[...]
        m1 = jnp.mean(dxh, axis=-1, keepdims=True)
        m2 = jnp.mean(dxh * xh_, axis=-1, keepdims=True)
        dz = rs_ref[...] * (dxh - m1 - xh_ * m2)
        dres_ref[...] = DEEPNORM_ALPHA * dz
        dy_ref[...] = ((1.0 + g_ref[...]) * dz).astype(BF16)

        @pl.when(first)
        def _():
            dga_ref[...] = jnp.zeros_like(dga_ref)
            dbe_ref[...] = jnp.zeros_like(dbe_ref)
            dg_ref[...] = jnp.zeros_like(dg_ref)

        dga_ref[...] += jnp.sum(dxo_ * xh_, axis=0, keepdims=True)
        dbe_ref[...] += jnp.sum(dxo_, axis=0, keepdims=True)
        dg_ref[...] += jnp.sum(dz * y_ref[...], axis=0, keepdims=True)

    extra_in = [] if pre is None else [_rows(tr, D), _vec(D), _vec(D)]
    return pl.pallas_call(
        body, name=name, grid=(S // tr,),
        in_specs=[_rows(tr, D), _rows(tr, D), _rows(tr, 1), _vec(D), _rows(tr, D), _vec(D)] + extra_in,
        out_specs=[_rows(tr, D), _rows(tr, D)] + [_vec(D)] * (3 + (0 if pre is None else 2)),
        out_shape=[jax.ShapeDtypeStruct((S, D), F32), jax.ShapeDtypeStruct((S, D), BF16)]
        + [jax.ShapeDtypeStruct((1, D), F32)] * (3 + (0 if pre is None else 2)),
        compiler_params=_params(("arbitrary",)),
    )(dxo, xh, rstd, gamma, y, gate, *(pre or ()))


def _mod_bwd(dh, x, sc, dres, name):
    S, D = x.shape
    tr = min(256, S)

    def body(dh_ref, x_ref, sc_ref, dr_ref, dx_ref, dsc_ref, dsh_ref):
        dh_ = dh_ref[...]
        dx_ref[...] = dr_ref[...] + dh_ * (1.0 + sc_ref[...])

        @pl.when(pl.program_id(0) == 0)
        def _():
            dsc_ref[...] = jnp.zeros_like(dsc_ref)
            dsh_ref[...] = jnp.zeros_like(dsh_ref)

        dsc_ref[...] += jnp.sum(dh_ * x_ref[...], axis=0, keepdims=True)
        dsh_ref[...] += jnp.sum(dh_, axis=0, keepdims=True)

    return pl.pallas_call(
        body, name=name, grid=(S // tr,),
        in_specs=[_rows(tr, D), _rows(tr, D), _vec(D), _rows(tr, D)],
        out_specs=[_rows(tr, D), _vec(D), _vec(D)],
        out_shape=[jax.ShapeDtypeStruct((S, D), F32), jax.ShapeDtypeStruct((1, D), F32), jax.ShapeDtypeStruct((1, D), F32)],
        compiler_params=_params(("arbitrary",)),
    )(dh, x, sc, dres)


def _log_sigmoid(z):
    return jnp.minimum(z, 0.0) - jnp.log(1.0 + jnp.exp(-jnp.abs(z)))


def _fox_gate_fwd(proj, b_f, n_heads, name):
    S, PW = proj.shape
    blk = min(256, S)
    last = PW // LANES - 1

    def body(fl_ref, b_ref, cum_ref):
        r = lax.broadcasted_iota(jnp.int32, (blk, blk), 0)
        c = lax.broadcasted_iota(jnp.int32, (blk, blk), 1)
        tril = (c <= r).astype(F32)
        carry = jnp.zeros((1, LANES), F32)
        for i in range(S // blk):
            lf = _log_sigmoid(fl_ref[i * blk:(i + 1) * blk, :] + b_ref[...])
            cum_ref[i * blk:(i + 1) * blk, :] = jnp.dot(tril, lf, preferred_element_type=F32, precision=HIGHEST) + carry
            carry = carry + jnp.sum(lf, axis=0, keepdims=True)

    return pl.pallas_call(
        body, name=name, grid=(1,),
        in_specs=[pl.BlockSpec((S, LANES), lambda i: (0, last)), pl.BlockSpec((1, LANES), lambda i: (0, 0))],
        out_specs=pl.BlockSpec((S, LANES), lambda i: (0, 0)),
        out_shape=jax.ShapeDtypeStruct((S, LANES), F32), compiler_params=_params(("arbitrary",)),
    )(proj, b_f)


def _fox_gate_bwd(dcum, proj, b_f, n_heads, name):
    S, PW = proj.shape
    blk = min(256, S)
    last = PW // LANES - 1
    nb = S // blk

    def body(dc_ref, fl_ref, b_ref, dfl_ref, db_ref):
        r = lax.broadcasted_iota(jnp.int32, (blk, blk), 0)
        c = lax.broadcasted_iota(jnp.int32, (blk, blk), 1)
        triu = (c >= r).astype(F32)
        lane = lax.broadcasted_iota(jnp.int32, (blk, LANES), 1)
        carry = jnp.zeros((1, LANES), F32)
        dbs = jnp.zeros((1, LANES), F32)
        for i in reversed(range(nb)):
            dc = dc_ref[i * blk:(i + 1) * blk, :]
            dlf = jnp.dot(triu, dc, preferred_element_type=F32, precision=HIGHEST) + carry
            carry = carry + jnp.sum(dc, axis=0, keepdims=True)
            z = fl_ref[i * blk:(i + 1) * blk, :] + b_ref[...]
            e = jnp.exp(-jnp.abs(z))
            sig_neg = jnp.where(z >= 0, e / (1.0 + e), 1.0 / (1.0 + e))
            dfl = jnp.where(lane < n_heads, dlf * sig_neg, 0.0)
            dfl_ref[i * blk:(i + 1) * blk, :] = dfl.astype(BF16)
            dbs = dbs + jnp.sum(dfl, axis=0, keepdims=True)
        db_ref[...] = dbs

    return pl.pallas_call(
        body, name=name, grid=(1,),
        in_specs=[pl.BlockSpec((S, LANES), lambda i: (0, 0)), pl.BlockSpec((S, LANES), lambda i: (0, last)),
                  pl.BlockSpec((1, LANES), lambda i: (0, 0))],
        out_specs=[pl.BlockSpec((S, LANES), lambda i: (0, 0)), pl.BlockSpec((1, LANES), lambda i: (0, 0))],
        out_shape=[jax.ShapeDtypeStruct((S, LANES), BF16), jax.ShapeDtypeStruct((1, LANES), F32)],
        compiler_params=_params(("arbitrary",)),
    )(dcum, proj, b_f)


def _fox_scores(q_ref, kb_ref, cq_ref, ck_ref, qi, tq, scale):
    kk = (qi + 1) * tq
    rows = slice(qi * tq, (qi + 1) * tq)
    qb = q_ref[rows, :].astype(BF16)
    s = lax.dot_general(qb, kb_ref[0:kk, :], _DN["nt"], preferred_element_type=F32) * scale
    s = s + (cq_ref[0, rows, :] - ck_ref[0, :, 0:kk])
    r = lax.broadcasted_iota(jnp.int32, (tq, kk), 0) + qi * tq
    c = lax.broadcasted_iota(jnp.int32, (tq, kk), 1)
    mask = c <= r
    return jnp.where(mask, s, NEG), mask, qb


def _fox_fwd(proj, cq, ck, n_heads, name, comm=None):
    S = proj.shape[0]
    H = n_heads
    tq = min(FOX_TQ, S)
    nq = S // tq
    scale = FDH ** -0.5

    def body(q_ref, k_ref, v_ref, cq_ref, ck_ref, o_ref, lse_ref, kb_ref, vb_ref):
        kb_ref[...] = k_ref[...].astype(BF16)
        vb_ref[...] = v_ref[...].astype(BF16)
        for qi in range(nq):
            kk = (qi + 1) * tq
            rows = slice(qi * tq, (qi + 1) * tq)
            s, _, _ = _fox_scores(q_ref, kb_ref, cq_ref, ck_ref, qi, tq, scale)
            m = jnp.max(s, axis=-1, keepdims=True)
            p = jnp.exp(s - m)
            l = jnp.sum(p, axis=-1, keepdims=True)
            p = p * (1.0 / l)
            o_ref[rows, :] = jnp.dot(p.astype(BF16), vb_ref[0:kk, :], preferred_element_type=F32).astype(BF16)
            lse_ref[0, rows, :] = m + jnp.log(l)

    col = lambda off: pl.BlockSpec((S, FDH), lambda h: (0, h + off))
    stat_c = pl.BlockSpec((1, S, 1), lambda h: (h, 0, 0))
    stat_r = pl.BlockSpec((1, 1, S), lambda h: (h, 0, 0))
    return _call(
        body, name=name, grid=(H,),
        in_specs=[col(0), col(H), col(2 * H), stat_c, stat_r],
        out_specs=[col(0), stat_c],
        out_shape=[jax.ShapeDtypeStruct((S, H * FDH), BF16), jax.ShapeDtypeStruct((H, S, 1), F32)],
        scratch_shapes=[pltpu.VMEM((S, FDH), BF16), pltpu.VMEM((S, FDH), BF16)],
        sem=("parallel",), args=(proj, proj, proj, cq, ck), comm=comm)


def _fox_bwd(proj, cq, ck, lse, do, n_heads, name, comm=None):
    S = proj.shape[0]
    H = n_heads
    tq = min(FOX_TQ, S)
    nq = S // tq
    scale = FDH ** -0.5

    def body(q_ref, k_ref, v_ref, cq_ref, ck_ref, lse_ref, do_ref, dq_ref, dk_ref, dv_ref, dcq_ref, dck_ref,
             kb_ref, vb_ref, dka_ref, dva_ref):
        kb_ref[...] = k_ref[...].astype(BF16)
        vb_ref[...] = v_ref[...].astype(BF16)
        dka_ref[...] = jnp.zeros_like(dka_ref)
        dva_ref[...] = jnp.zeros_like(dva_ref)
        dck_ref[...] = jnp.zeros_like(dck_ref)
        for qi in range(nq):
            kk = (qi + 1) * tq
            rows = slice(qi * tq, (qi + 1) * tq)
            s, mask, qb = _fox_scores(q_ref, kb_ref, cq_ref, ck_ref, qi, tq, scale)
            p = jnp.where(mask, jnp.exp(s - lse_ref[0, rows, :]), 0.0)
            dob = do_ref[rows, :]
            dp = lax.dot_general(dob, vb_ref[0:kk, :], _DN["nt"], preferred_element_type=F32)
            delta = jnp.sum(p * dp, axis=-1, keepdims=True)
            ds = p * (dp - delta)
            dcq_ref[0, rows, :] = jnp.sum(ds, axis=-1, keepdims=True)
            dck_ref[0, :, 0:kk] -= jnp.sum(ds, axis=0, keepdims=True)
            dsb = (ds * scale).astype(BF16)
            dq_ref[rows, :] = jnp.dot(dsb, kb_ref[0:kk, :], preferred_element_type=F32).astype(BF16)
            dka_ref[0:kk, :] += lax.dot_general(dsb, qb, _DN["tn"], preferred_element_type=F32)
            dva_ref[0:kk, :] += lax.dot_general(p.astype(BF16), dob, _DN["tn"], preferred_element_type=F32)
        dk_ref[...] = dka_ref[...].astype(BF16)
        dv_ref[...] = dva_ref[...].astype(BF16)

    col = lambda off: pl.BlockSpec((S, FDH), lambda h: (0, h + off))
    stat_c = pl.BlockSpec((1, S, 1), lambda h: (h, 0, 0))
    stat_r = pl.BlockSpec((1, 1, S), lambda h: (h, 0, 0))
    wide = jax.ShapeDtypeStruct((S, H * FDH), BF16)
    return _call(
        body, name=name, grid=(H,),
        in_specs=[col(0), col(H), col(2 * H), stat_c, stat_r, stat_c, col(0)],
        out_specs=[col(0), col(0), col(0), stat_c, stat_r],
        out_shape=[wide, wide, wide, jax.ShapeDtypeStruct((H, S, 1), F32), jax.ShapeDtypeStruct((H, 1, S), F32)],
        scratch_shapes=[pltpu.VMEM((S, FDH), BF16), pltpu.VMEM((S, FDH), BF16), pltpu.VMEM((S, FDH), F32), pltpu.VMEM((S, FDH), F32)],
        sem=("parallel",), args=(proj, proj, proj, cq, ck, lse, do), comm=comm)


def _rope_tables(pos, sign):
    inv = ROPE_THETA ** (-jnp.arange(0, ROPE_DIM, 2, dtype=F32) / ROPE_DIM)
    ang = pos.astype(F32)[:, None] * inv
    cos, sin = jnp.cos(ang), sign * jnp.sin(ang)
    l64 = jnp.arange(LANES) % SDH
    idx = l64 % (ROPE_DIM // 2)
    c = jnp.where(l64 < ROPE_DIM, cos[:, idx], 1.0)
    sa = jnp.where(l64 < ROPE_DIM // 2, -sin[:, idx], 0.0)
    sb = jnp.where((l64 >= ROPE_DIM // 2) & (l64 < ROPE_DIM), sin[:, idx], 0.0)
    rot = jnp.stack([c, sa, sb])
    ident = jnp.stack([jnp.ones_like(c), jnp.zeros_like(c), jnp.zeros_like(c)])
    return jnp.stack([rot, ident]).astype(F32)


def _rope(xin, tabs, n_rot, out_dtype, name, comm=None):
    S, W = xin.shape

    def body(x_ref, t_ref, o_ref):
        xv = x_ref[...]
        o = xv * t_ref[0, 0] + pltpu.roll(xv, LANES - ROPE_DIM // 2, 1) * t_ref[0, 1] + pltpu.roll(xv, ROPE_DIM // 2, 1) * t_ref[0, 2]
        o_ref[...] = o.astype(out_dtype)

    return _call(
        body, name=name, grid=(W // LANES,),
        in_specs=[pl.BlockSpec((S, LANES), lambda j: (0, j)),
                  pl.BlockSpec((1, 3, S, LANES), lambda j: (jnp.where(j < n_rot, 0, 1), 0, 0, 0))],
        out_specs=[pl.BlockSpec((S, LANES), lambda j: (0, j))],
        out_shape=[jax.ShapeDtypeStruct((S, W), out_dtype)], sem=("parallel",), args=(xin, tabs), comm=comm)[0]


SWA_PER_STEP = 2


def _swa_bias():
    r = jnp.arange(WIN)[:, None]
    c = jnp.arange(2 * WIN)[None, :]
    first = c <= r
    later = (c > r) & (c <= r + WIN)
    return jnp.where(jnp.stack([first, later]), 0.0, NEG).astype(F32)


def _swa_probs(q_ref, k_ref, sk_ref, b_ref, n, j, scale):
    st = pl.multiple_of(jnp.maximum(n - 1, 0) * WIN, WIN)
    qb = q_ref[0, j]
    kb = k_ref[0, pl.ds(st, 2 * WIN), :]
    gm = qb.shape[0]
    s = lax.dot_general(qb, kb, _DN["nt"], preferred_element_type=F32) * scale
    s = (s.reshape(gm // WIN, WIN, 2 * WIN) + b_ref[jnp.minimum(n, 1)][None]).reshape(gm, 2 * WIN)
    sink = sk_ref[0]
    m = jnp.maximum(jnp.max(s, axis=-1, keepdims=True), sink)
    e = jnp.exp(s - m)
    es = jnp.exp(sink - m)
    inv = 1.0 / (jnp.sum(e, axis=-1, keepdims=True) + es)
    return e * inv, es * inv, st, qb, kb


def _swa_specs(S, gm):
    blk = pl.BlockSpec((1, SWA_PER_STEP, gm, SDH), lambda g, n: (g, n, 0, 0))
    kv = pl.BlockSpec((1, S, SDH), lambda g, n: (g, 0, 0))
    col = pl.BlockSpec((1, gm, 1), lambda g, n: (g, 0, 0))
    bias = pl.BlockSpec((2, WIN, 2 * WIN), lambda g, n: (0, 0, 0))
    return blk, kv, col, bias


def _swa_fwd(q, k, v, sinks, name, comm=None):
    KH, nb, gm, _ = q.shape
    S = k.shape[1]
    scale = SDH ** -0.5

    def body(q_ref, k_ref, v_ref, sk_ref, b_ref, o_ref):
        for j in range(SWA_PER_STEP):
            p, _, st, _, _ = _swa_probs(q_ref, k_ref, sk_ref, b_ref, pl.program_id(1) * SWA_PER_STEP + j, j, scale)
            vb = v_ref[0, pl.ds(st, 2 * WIN), :]
            o_ref[0, j] = jnp.dot(p.astype(BF16), vb, preferred_element_type=F32).astype(BF16)

    blk, kv, col, bias = _swa_specs(S, gm)
    return _call(
        body, name=name, grid=(KH, nb // SWA_PER_STEP), in_specs=[blk, kv, kv, col, bias], out_specs=[blk],
        out_shape=[jax.ShapeDtypeStruct(q.shape, BF16)], sem=("parallel", "parallel"), args=(q, k, v, sinks, _swa_bias()), comm=comm)[0]


def _swa_bwd(q, k, v, sinks, do, name, comm=None):
    KH, nb, gm, _ = q.shape
    S = k.shape[1]
    scale = SDH ** -0.5

    def body(q_ref, k_ref, v_ref, sk_ref, b_ref, do_ref, dq_ref, dk_ref, dv_ref, dsk_ref):
        @pl.when(pl.program_id(1) == 0)
        def _():
            dk_ref[...] = jnp.zeros_like(dk_ref)
            dv_ref[...] = jnp.zeros_like(dv_ref)
            dsk_ref[...] = jnp.zeros_like(dsk_ref)

        blocks = []
        for j in range(SWA_PER_STEP):
            p, ps, st, qb, kb = _swa_probs(q_ref, k_ref, sk_ref, b_ref, pl.program_id(1) * SWA_PER_STEP + j, j, scale)
            vb = v_ref[0, pl.ds(st, 2 * WIN), :]
            dob = do_ref[0, j]
            dp = lax.dot_general(dob, vb, _DN["nt"], preferred_element_type=F32)
            delta = jnp.sum(p * dp, axis=-1, keepdims=True)
            dsb = (p * (dp - delta) * scale).astype(BF16)
            dq_ref[0, j] = jnp.dot(dsb, kb, preferred_element_type=F32)
            blocks.append((st, lax.dot_general(dsb, qb, _DN["tn"], preferred_element_type=F32),
                           lax.dot_general(p.astype(BF16), dob, _DN["tn"], preferred_element_type=F32), ps * delta))
        for st, dk, dv, dsk in blocks:
            dk_ref[0, pl.ds(st, 2 * WIN), :] += dk
            dv_ref[0, pl.ds(st, 2 * WIN), :] += dv
            dsk_ref[0] -= dsk

    blk, kv, col, bias = _swa_specs(S, gm)
    return _call(
        body, name=name, grid=(KH, nb // SWA_PER_STEP), in_specs=[blk, kv, kv, col, bias, blk], out_specs=[blk, kv, kv, col],
        out_shape=[jax.ShapeDtypeStruct(q.shape, F32), jax.ShapeDtypeStruct(k.shape, F32),
                   jax.ShapeDtypeStruct(k.shape, F32), jax.ShapeDtypeStruct(sinks.shape, F32)],
        sem=("parallel", "arbitrary"), args=(q, k, v, sinks, _swa_bias(), do), comm=comm)


def _shift_down(u, k):
    row = lax.broadcasted_iota(jnp.int32, u.shape, 0)
    return jnp.where(row >= k, pltpu.roll(u, k, 0), 0.0)


def _shift_up(u, k):
    n = u.shape[0]
    row = lax.broadcasted_iota(jnp.int32, u.shape, 0)
    return jnp.where(row < n - k, pltpu.roll(u, n - k, 0), 0.0)


def _conv3(u, w_ref, b_ref):
    return w_ref[0:1, :] * _shift_down(u, 2) + w_ref[1:2, :] * _shift_down(u, 1) + w_ref[2:3, :] * u + b_ref[...]


def _conv_gate(u, cw, cb, name, comm=None):
    S, F2 = u.shape
    Fh = F2 // 2
    tc = _tile(Fh, 256)
    nf = Fh // tc

    def body(ug_ref, uv_ref, wg_ref, wv_ref, bg_ref, bv_ref, a_ref):
        g = _conv3(ug_ref[...], wg_ref, bg_ref)
        val = _conv3(uv_ref[...], wv_ref, bv_ref)
        a_ref[...] = (g * (1.0 / (1.0 + jnp.exp(-g))) * val).astype(BF16)

    blk = lambda r, off: pl.BlockSpec((r, tc), lambda j: (0, j + off))
    return _call(
        body, name=name, grid=(nf,),
        in_specs=[blk(S, 0), blk(S, nf), blk(3, 0), blk(3, nf), blk(1, 0), blk(1, nf)], out_specs=[blk(S, 0)],
        out_shape=[jax.ShapeDtypeStruct((S, Fh), BF16)], sem=("parallel",), args=(u, u, cw, cw, cb, cb), comm=comm)[0]


def _conv_gate_bwd(u, da, cw, cb, name, comm=None):
    S, F2 = u.shape
    Fh = F2 // 2
    tc = _tile(Fh, 256)
    nf = Fh // tc

    def half(h, dx, uu, w_ref, du_ref, dw_ref, db_ref):
        up1, up2 = _shift_up(dx, 1), _shift_up(dx, 2)
        du = w_ref[2:3, :] * dx + w_ref[1:2, :] * up1 + w_ref[0:1, :] * up2
        du_ref[h] = du.astype(BF16)
        dw_ref[h, 0:1, :] = jnp.sum(up2 * uu, axis=0, keepdims=True)
        dw_ref[h, 1:2, :] = jnp.sum(up1 * uu, axis=0, keepdims=True)
        dw_ref[h, 2:3, :] = jnp.sum(dx * uu, axis=0, keepdims=True)
        db_ref[h] = jnp.sum(dx, axis=0, keepdims=True)

    def body(ug_ref, uv_ref, da_ref, wg_ref, wv_ref, bg_ref, bv_ref, du_ref, dw_ref, db_ref):
        ug = ug_ref[...]
        uv = uv_ref[...]
        g = _conv3(ug, wg_ref, bg_ref)
        val = _conv3(uv, wv_ref, bv_ref)
        sig = 1.0 / (1.0 + jnp.exp(-g))
        da_ = da_ref[...]
        dg = da_ * val * (sig * (1.0 + g * (1.0 - sig)))
        dval = da_ * (g * sig)
        half(0, dg, ug, wg_ref, du_ref, dw_ref, db_ref)
        half(1, dval, uv, wv_ref, du_ref, dw_ref, db_ref)

    blk = lambda r, off: pl.BlockSpec((r, tc), lambda j: (0, j + off))
    both = lambda r: pl.BlockSpec((2, r, tc), lambda j: (0, 0, j))
    return _call(
        body, name=name, grid=(nf,),
        in_specs=[blk(S, 0), blk(S, nf), blk(S, 0), blk(3, 0), blk(3, nf), blk(1, 0), blk(1, nf)],
        out_specs=[both(S), both(3), both(1)],
        out_shape=[jax.ShapeDtypeStruct((2, S, Fh), BF16), jax.ShapeDtypeStruct((2, 3, Fh), F32), jax.ShapeDtypeStruct((2, 1, Fh), F32)],
        sem=("parallel",), args=(u, u, da, cw, cw, cb, cb), comm=comm)


def _to_groups(t, kh):
    S, width = t.shape
    g = width // SDH // kh
    return t.reshape(S // WIN, WIN, kh, g, SDH).transpose(2, 0, 3, 1, 4).reshape(kh, S // WIN, g * WIN, SDH)


def _from_groups(t):
    kh, nb, gm, _ = t.shape
    g = gm // WIN
    return t.reshape(kh, nb, g, WIN, SDH).transpose(1, 3, 0, 2, 4).reshape(nb * WIN, kh * g * SDH)


class LocalWeights:
    def __init__(self, weights):
        self.weights, self.grads = weights, {}

    def w(self, name):
        return self.weights[name]

    def carry(self, stage, last=()):
        return None

    def carried(self, stage, comm):
        pass

    def grad(self, name, g):
        self.grads[name] = g


def _local_step(dm, x, tgt, pos, mod, sp, pp, h0=None):
    S, D, FH, QH, KH, Fh = dm
    m = [[mod[i:i + 1, j * D:(j + 1) * D] for j in range(6)] for i in range(DEPTH)]

    last = []

    def run(fn, *args, name, **kw):
        comm = pp.carry(name, last)
        args = [pp.w(arg[1]) if isinstance(arg, tuple) and arg[:1] == ("w",) else arg for arg in args]
        out = fn(*args, name=name, comm=comm, **kw)
        if comm is not None:
            pp.carried(name, comm)
        last[:] = list(out) if isinstance(out, (list, tuple)) else [out]
        return out

    sv = []
    xs = x
    h = _modulate(xs, m[0][1], m[0][0], "mod_in") if h0 is None else h0
    last[:] = [h]
    for i in range(DEPTH):
        sh1, sc1, g1, sh2, sc2, g2 = m[i]
        L = {}
        L["x_in"], L["h1"] = xs, h
        if i == 0:
            proj = run(_mm, h[None], ("w", "fox_w_in"), mode="nn", out_dtype=F32, name="fox_proj", tn=896)[0]
            cum = _fox_gate_fwd(proj, sp["fox_b_f"], FH, "fox_gate")
            cq = cum[:, :FH].T[:, :, None]
            ck = cum[:, :FH].T[:, None, :]
            o, lse = run(_fox_fwd, proj, cq, ck, FH, name="fox_attn")
            L.update(proj=proj, cq=cq, ck=ck, lse=lse, o=o)
            y = run(_mm, o[None], ("w", "fox_w_o"), mode="nn", out_dtype=F32, name="fox_out")[0]
        else:
            proj = run(_mm, h[None], ("w", "swa_w_in"), mode="nn", out_dtype=F32, name="swa_proj", tn=640)[0]
            tabs = _rope_tables(pos, 1.0)
            n_rot = (QH + KH) * SDH // LANES
            pr = run(_rope, proj, tabs, n_rot, BF16, name="swa_rope")
            qh = _to_groups(pr[:, :QH * SDH], KH)
            kh = pr[:, QH * SDH:(QH + KH) * SDH].reshape(S, KH, SDH).transpose(1, 0, 2)
            vh = pr[:, (QH + KH) * SDH:].reshape(S, KH, SDH).transpose(1, 0, 2)
            oh = run(_swa_fwd, qh, kh, vh, sp["sinks"], name="swa_attn")
            o = _from_groups(oh)
            L.update(qh=qh, kh=kh, vh=vh, o=o)
            y = run(_mm, o[None], ("w", "swa_w_o"), mode="nn", out_dtype=F32, name="swa_out")[0]
        L["y1"] = y
        x1, L["xh1"], L["rs1"], h2 = run(_ln_fwd, xs, y, g1, sp["ln_mix_g"][i], sp["ln_mix_b"][i], sc2, sh2, name=f"ln_mix{i}")
        L["x1"], L["h2"] = x1, h2
        u = run(_mm, h2[None], ("w", f"ffn_w_up{i}"), mode="nn", out_dtype=F32, name=f"ffn_up{i}", tm=1024, tn=1408)[0]
        a = run(_conv_gate, u, sp["conv_w"][i], sp["conv_b"][i], name=f"ffn_gate{i}")
        y2 = run(_mm, a[None], ("w", f"ffn_w_down{i}"), mode="nn", out_dtype=F32, name=f"ffn_down{i}", tk=5632, tm=512)[0]
        L.update(u=u, a=a, y2=y2)
        if i + 1 < DEPTH:
            xs, L["xh2"], L["rs2"], h = run(_ln_fwd, x1, y2, g2, sp["ln_ffn_g"][i], sp["ln_ffn_b"][i], m[i + 1][1], m[i + 1][0], name=f"ln_ffn{i}")
        else:
            xs, L["xh2"], L["rs2"] = run(_ln_fwd, x1, y2, g2, sp["ln_ffn_g"][i], sp["ln_ffn_b"][i], None, None, name=f"ln_ffn{i}")
        sv.append(L)

    dx, loss_cols = _loss_head(xs, tgt, "loss_head")

    gs = {k: [None] * DEPTH for k in ("conv_w", "conv_b", "ln_mix_g", "ln_mix_b", "ln_ffn_g", "ln_ffn_b")}
    dmp = [dict() for _ in range(DEPTH)]
    dres, pend = dx, None
    for i in reversed(range(DEPTH)):
        sh1, sc1, g1, sh2, sc2, g2 = m[i]
        L = sv[i]
        res = _ln_bwd(dres, L["xh2"], L["rs2"], sp["ln_ffn_g"][i], L["y2"], g2, f"ln_ffn_bwd{i}",
                      None if pend is None else (*pend, sp["ln_ffn_b"][i]))
        dres, dy, gs["ln_ffn_g"][i], gs["ln_ffn_b"][i], dmp[i]["g2"] = res[:5]
        if pend is not None:
            dmp[i + 1]["sc1"], dmp[i + 1]["sh1"] = res[5:]
        da = run(_mm, dy[None], pp.w(f"ffn_w_down{i}"), mode="nt", out_dtype=F32, name=f"ffn_down_dx{i}", tm=1024, tn=1408)[0]
        pp.grad(f"ffn_w_down{i}", run(_mm, L["a"][None], dy[None], mode="tn", out_dtype=BF16, name=f"ffn_down_dw{i}", tm=1408))
        du, dcw, dcb = run(_conv_gate_bwd, L["u"], da, sp["conv_w"][i], sp["conv_b"][i], name=f"ffn_gate_bwd{i}")
        gs["conv_w"][i] = dcw.transpose(1, 0, 2).reshape(3, 2 * Fh)
        gs["conv_b"][i] = dcb.transpose(1, 0, 2).reshape(1, 2 * Fh)
        dh2 = run(_mm, du, pp.w(f"ffn_w_up{i}"), mode="nt", out_dtype=F32, name=f"ffn_up_dx{i}", tk=2816)[0]
        pp.grad(f"ffn_w_up{i}", run(_mm, L["h2"][None], du, mode="tn", out_dtype=BF16, name=f"ffn_up_dw{i}", out_groups=N_CHIPS, tn=1408))
        dres, dy, gs["ln_mix_g"][i], gs["ln_mix_b"][i], dmp[i]["g1"], dmp[i]["sc2"], dmp[i]["sh2"] = _ln_bwd(
            dres, L["xh1"], L["rs1"], sp["ln_mix_g"][i], L["y1"], g1, f"ln_mix_bwd{i}", (dh2, sc2, sp["ln_mix_b"][i]))
        if i == 0:
            do = run(_mm, dy[None], pp.w("fox_w_o"), mode="nt", out_dtype=BF16, name="fox_out_dx")[0]
            pp.grad("fox_w_o", run(_mm, L["o"][None], dy[None], mode="tn", out_dtype=BF16, name="fox_out_dw"))
            dq, dk, dv, dcq, dck = run(_fox_bwd, L["proj"], L["cq"], L["ck"], L["lse"], do, FH, name="fox_attn_bwd")
            dcum = dcq[:, :, 0].T + dck[:, 0, :].T
            dcum = jnp.pad(dcum, ((0, 0), (0, LANES - FH)))
            dfl, db_f = _fox_gate_bwd(dcum, L["proj"], sp["fox_b_f"], FH, "fox_gate_bwd")
            gs["fox_b_f"] = db_f
            dproj = jnp.concatenate([dq, dk, dv, dfl], axis=1)
            pp.grad("fox_w_in", run(_mm, L["h1"][None], dproj[None], mode="tn", out_dtype=BF16, name="fox_proj_dw", tn=896))
            dh1 = run(_mm, dproj[None], pp.w("fox_w_in"), mode="nt", out_dtype=F32, name="fox_proj_dx", tk=6272, tm=512)[0]
        else:
            do = run(_mm, dy[None], pp.w("swa_w_o"), mode="nt", out_dtype=BF16, name="swa_out_dx")[0]
            pp.grad("swa_w_o", run(_mm, L["o"][None], dy[None], mode="tn", out_dtype=BF16, name="swa_out_dw"))
            dqh, dkh, dvh, dsk = run(_swa_bwd, L["qh"], L["kh"], L["vh"], sp["sinks"], _to_groups(do, KH), name="swa_attn_bwd")
            gs["sinks"] = jnp.sum(dsk.reshape(QH, WIN), axis=1)
            dpr = jnp.concatenate([_from_groups(dqh), dkh.transpose(1, 0, 2).reshape(S, KH * SDH),
                                   dvh.transpose(1, 0, 2).reshape(S, KH * SDH)], axis=1)
            n_rot = (QH + KH) * SDH // LANES
            dproj = _rope(dpr, _rope_tables(pos, -1.0), n_rot, BF16, "swa_rope_bwd")
            dh1 = run(_mm, dproj[None], pp.w("swa_w_in"), mode="nt", out_dtype=F32, name="swa_proj_dx", tk=640)[0]
            pp.grad("swa_w_in", run(_mm, L["h1"][None], dproj[None], mode="tn", out_dtype=BF16, name="swa_proj_dw", out_groups=N_CHIPS, tn=640))
        pend = (dh1, sc1)
    grad_x, dmp[0]["sc1"], dmp[0]["sh1"] = _mod_bwd(pend[0], sv[0]["x_in"], pend[1], dres, "mod_mix_bwd0")
    dmod = [jnp.concatenate([p["sh1"], p["sc1"], p["g1"], p["sh2"], p["sc2"], p["g2"]], axis=1) for p in dmp]
    return loss_cols, grad_x, gs, jnp.concatenate(dmod, axis=0)


def _allgather_small(v, name):
    m_per, n = v.shape

    def body(x_ref, out_ref, send_sems, recv_sems, local_sem):
        x, y, c, chips = _place()
        me, sibling = (x, y, c), (x, y, 1 - c)

        def rows(px, py, pc):
            return out_ref.at[pl.ds((4 * px + 2 * py + pc) * m_per, m_per), :]

        def copy(k, block, to, src=None):
            return _remote(rows(*block) if src is None else src, rows(*block), send_sems.at[k], recv_sems.at[k], to)

        mine = pltpu.make_async_copy(x_ref, rows(*me), local_sem)
        mine.start()
        first = [copy(0, me, sibling, src=x_ref)]
        first += [copy(1 + j, me, (*chip, c), src=x_ref) for j, chip in enumerate(chips)]
        for cp in first:
            cp.start()
        passed = [copy(4 + j, (*chip, c), sibling) for j, chip in enumerate(chips)]
        for j, chip in enumerate(chips):
            copy(1 + j, (*chip, c), me).wait_recv()
            passed[j].start()
        copy(0, sibling, me).wait_recv()
        for j, chip in enumerate(chips):
            copy(4 + j, (*chip, 1 - c), me).wait_recv()
        for cp in first + passed:
            cp.wait_send()
        mine.wait()

    return pl.pallas_call(
        body, name=name, out_shape=jax.ShapeDtypeStruct((N_DEV * m_per, n), v.dtype),
        in_specs=[pl.BlockSpec(memory_space=pltpu.VMEM)], out_specs=pl.BlockSpec(memory_space=pltpu.VMEM),
        scratch_shapes=[pltpu.SemaphoreType.DMA((7,)), pltpu.SemaphoreType.DMA((7,)), pltpu.SemaphoreType.DMA],
        compiler_params=pltpu.CompilerParams(vmem_limit_bytes=VMEM_LIMIT),
    )(v)


def _row_tile(r, pref=256):
    return _tile(r, pref, 16)


def _cast_bf16(w, layer, chip, name, after=()):
    _, R, C = w.shape
    tr = _row_tile(R)

    def body(s_ref, w_ref, *rest):
        rest[-1][...] = w_ref[...].astype(BF16)

    return pl.pallas_call(
        body, name=name,
        grid_spec=pltpu.PrefetchScalarGridSpec(
            num_scalar_prefetch=1, grid=(R // tr,),
            in_specs=[pl.BlockSpec((None, tr, C), lambda i, s: (layer, i, 0))] + _any_specs(len(after)),
            out_specs=pl.BlockSpec((None, tr, C), lambda i, s: (s[0], i, 0))),
        out_shape=jax.ShapeDtypeStruct((N_CHIPS, R, C), BF16), compiler_params=_params(("parallel",)),
    )(jnp.reshape(chip, (1,)).astype(jnp.int32), w, *after)


def _add_sibling(g, got, c, name):
    G, R, C = g.shape
    rh = R // 2
    tr = _row_tile(rh)
    nb = rh // tr

    def body(c_ref, g_ref, o_ref, p_ref):
        p_ref[...] = (g_ref[...].astype(F32) + o_ref[...].astype(F32)).astype(BF16)

    return pl.pallas_call(
        body, name=name,
        grid_spec=pltpu.PrefetchScalarGridSpec(
            num_scalar_prefetch=1, grid=(G, nb),
            in_specs=[pl.BlockSpec((1, tr, C), lambda s, i, c_ref: (s, c_ref[0] * nb + i, 0)),
                      pl.BlockSpec((1, tr, C), lambda s, i, c_ref: (s, i, 0))],
            out_specs=pl.BlockSpec((1, tr, C), lambda s, i, c_ref: (s, i, 0))),
        out_shape=jax.ShapeDtypeStruct((G, rh, C), BF16), compiler_params=_params(("parallel", "parallel")),
    )(jnp.reshape(c, (1,)).astype(jnp.int32), g, got)


def _sum_chips(part, landed, chip, c, name):
    G, rh, C = part.shape
    tr = _row_tile(rh)
    nb = rh // tr

    def body(p_ref, own_ref, *rest):
        acc = own_ref[...].astype(F32)
        for ref in rest[:G - 1]:
            acc = acc + ref[...].astype(F32)
        rest[G - 1][...] = acc

    slot = lambda k: pl.BlockSpec((None, tr, C), lambda i, p: ((p[0] + k) % G, i, 0))
    return pl.pallas_call(
        body, name=name,
        grid_spec=pltpu.PrefetchScalarGridSpec(
            num_scalar_prefetch=1, grid=(nb,), in_specs=[slot(k) for k in range(G)],
            out_specs=pl.BlockSpec((tr, C), lambda i, p: (p[1] * nb + i, 0))),
        out_shape=jax.ShapeDtypeStruct((2 * rh, C), F32), compiler_params=_params(("parallel",)),
    )(jnp.stack([chip, c]).astype(jnp.int32), part, *([landed] * (G - 1)))


def _adam_math(w, g, m, v):
    m = ADAM_B1 * m + (1.0 - ADAM_B1) * g
    v = ADAM_B2 * v + (1.0 - ADAM_B2) * (g * g)
    m_hat = m / (1.0 - ADAM_B1 ** ADAM_STEP)
    v_hat = v / (1.0 - ADAM_B2 ** ADAM_STEP)
    delta = -ADAM_LR * (m_hat / (jnp.sqrt(v_hat) + ADAM_EPS) + ADAM_WD * w)
    return delta, m, v


def _adamw(w, g, m, v, layer, prev, name, by_cols=False, after=()):
    L, R, C = w.shape
    tr = R if by_cols else _tile(R, 128, 8)
    tc = _tile(C, 256) if by_cols else C
    n_alias = len(prev)
    prev = tuple(prev) + tuple(after)
    n_prev = len(prev)

    def body(w_ref, g_ref, m_ref, v_ref, *rest):
        go_ref, d_ref, mo_ref, vo_ref = rest[n_prev:]
        gv = g_ref[...]
        go_ref[...] = gv
        d_ref[...], mo_ref[...], vo_ref[...] = _adam_math(w_ref[...], gv, m_ref[...], v_ref[...])

    lay = pl.BlockSpec((None, tr, tc), lambda i: (layer, i // (C // tc), i % (C // tc)))
    flat = pl.BlockSpec((tr, tc), lambda i: (i // (C // tc), i % (C // tc)))
    return _call(
        body, name=name, grid=((R // tr) * (C // tc),), in_specs=[lay, flat, lay, lay] + _any_specs(n_prev), out_specs=[lay] * 4,
        out_shape=[jax.ShapeDtypeStruct((L, R, C), F32)] * 4, aliases={4 + k: k for k in range(n_alias)},
        sem=("parallel",), args=(w, g, m, v, *prev))


def _cond_rows(c_row, cw, name, after=()):
    D = c_row.shape[1]
    nr, fc = cw.shape

    def body(c_ref, e_ref, *rest):
        o_ref = rest[-1]
        o_ref[...] = jnp.zeros_like(o_ref)
        cv = c_ref[...]
        o_ref[0:1, 0:D] = cv * (1.0 / (1.0 + jnp.exp(-cv)))
        o_ref[8:8 + nr, 0:fc] = e_ref[...]

    vmem = pl.BlockSpec(memory_space=pltpu.VMEM)
    return pl.pallas_call(body, name=name, in_specs=[vmem, vmem] + _any_specs(len(after)), out_specs=vmem,
                          out_shape=jax.ShapeDtypeStruct((16, max(D, fc)), F32))(c_row, cw, *after)


def _ada_fwd(cact, ada_w, ada_b, layer, chip, name):
    _, D, NC = ada_w.shape
    tn = _tile(NC, 1024)
    nj = NC // tn

    def body(idx_ref, c_ref, w_ref, b_ref, o_ref):
        acc = jnp.dot(c_ref[...].astype(BF16), w_ref[0].astype(BF16), preferred_element_type=F32)
        o_ref[...] = acc + b_ref[pl.ds(idx_ref[0], 1), :]

    return pl.pallas_call(
        body, name=name,
        grid_spec=pltpu.PrefetchScalarGridSpec(
            num_scalar_prefetch=1, grid=(nj,),
            in_specs=[pl.BlockSpec((8, D), lambda j, idx: (0, 0)),
                      pl.BlockSpec((1, D, tn), lambda j, idx: (idx[0], 0, j)),
                      pl.BlockSpec((DEPTH, tn), lambda j, idx: (0, idx[1] * nj + j))],
            out_specs=pl.BlockSpec((8, tn), lambda j, idx: (0, j))),
        out_shape=jax.ShapeDtypeStruct((8, NC), F32), compiler_params=_params(("parallel",)),
    )(jnp.stack([layer, chip]).astype(jnp.int32), cact, ada_w, ada_b)


def _ada_grad_adamw(cact_t, dmod, w, m, v, name):
    L, D, NC = w.shape
    tr = _tile(D, 128, 8)

    def body(c_ref, d_ref, w_ref, m_ref, v_ref, g_ref, dl_ref, mo_ref, vo_ref):
        g = jnp.dot(c_ref[...], d_ref[...], preferred_element_type=F32, precision=HIGHEST)
        g_ref[...] = g
        dl_ref[...], mo_ref[...], vo_ref[...] = _adam_math(w_ref[...], g, m_ref[...], v_ref[...])

    lay = pl.BlockSpec((None, tr, NC), lambda l, i: (l, i, 0))
    return _call(
        body, name=name, grid=(L, D // tr),
        in_specs=[pl.BlockSpec((tr, N_DEV), lambda l, i: (i, 0)), pl.BlockSpec((None, N_DEV, NC), lambda l, i: (l, 0, 0)), lay, lay, lay],
        out_specs=[lay] * 4, out_shape=[jax.ShapeDtypeStruct((L, D, NC), F32)] * 4,
        sem=("parallel", "parallel"), args=(cact_t, dmod, w, m, v))


def _sum_devices(gathered, name):
    n, R, C = gathered.shape

    def body(g_ref, o_ref):
        acc = g_ref[0]
        for j in range(1, n):
            acc = acc + g_ref[j]
        o_ref[...] = acc

    return pl.pallas_call(body, name=name, out_shape=jax.ShapeDtypeStruct((R, C), F32),
                          compiler_params=pltpu.CompilerParams(vmem_limit_bytes=VMEM_LIMIT))(gathered)


def _adamw_small(w, g, m, v, name):
    def body(w_ref, g_ref, m_ref, v_ref, d_ref, mo_ref, vo_ref):
        d_ref[...], mo_ref[...], vo_ref[...] = _adam_math(w_ref[...], g_ref[...], m_ref[...], v_ref[...])

    return pl.pallas_call(body, name=name, out_shape=[jax.ShapeDtypeStruct(w.shape, F32)] * 3)(w, g, m, v)


def _pad_rows(flat, unit=8 * LANES):
    n = flat.shape[0]
    total = -(-n // unit) * unit
    return jnp.pad(flat, (0, total - n)).reshape(total // LANES, LANES)


def _pad_lanes(v2d):
    return jnp.pad(v2d.reshape(1, -1), ((0, 0), (0, LANES - v2d.size)))


FORWARD = {
    "fox_proj": ((), ("ffn_w_up0",)),
    "ffn_up0": (("ffn_w_up0",), ("ffn_w_down0", "swa_w_in")),
    "ffn_down0": (("ffn_w_down0", "swa_w_in"), ("swa_w_o", "ffn_w_up1")),
    "swa_out": (("swa_w_o", "ffn_w_up1"), ("ffn_w_down1",)),
    "ffn_down1": (("ffn_w_down1",), ()),
}
PLAN = {
    "ffn_gate_bwd1": [("swap", "ffn_w_down1")],
    "ffn_up_dx1": [("scatter", "ffn_w_down1", 0, 1, 1)],
    "swa_out_dx": [("swap", "ffn_w_up1")],
    "swa_attn_bwd": [("scatter", "ffn_w_up1", 0, 6, 8), ("swap", "swa_w_o")],
    "swa_proj_dx": [("scatter", "swa_w_o", 0, 1, 1)],
    "ffn_down_dx0": [("scatter", "ffn_w_up1", 6, 8, 8), ("swap", "swa_w_in")],
    "ffn_down_dw0": [("scatter", "swa_w_in", 0, 1, 1)],
    "ffn_gate_bwd0": [("swap", "ffn_w_down0")],
    "ffn_up_dx0": [("scatter", "ffn_w_down0", 0, 1, 1)],
    "fox_out_dx": [("swap", "ffn_w_up0")],
    "fox_attn_bwd": [("scatter", "ffn_w_up0", 0, 5, 8), ("swap", "fox_w_o")],
    "fox_proj_dw": [("scatter", "fox_w_o", 0, 1, 1), ("scatter", "ffn_w_up0", 5, 6, 8)],
    "fox_proj_dx": [("scatter", "ffn_w_up0", 6, 8, 8), ("swap", "fox_w_in")],
}


class Exchanges:
    def __init__(self, dm, slots, chip, c):
        self.dm, self.slots, self.chip, self.c = dm, dict(slots), chip, c
        self.raw, self.part, self.landed, self.grads, self.views, self.pending = {}, {}, {}, {}, {}, {}

    def gather_start(self, keys, name, after):
        self.first = (keys, _gather_comm([self.slots[k] for k in keys]))
        self.first_state, token = _split_start(self.first[1], name + "_start", after)
        return token

    def gather_finish(self, after, name):
        keys, comm = self.first
        _split_wait(comm, self.first_state, after, name + "_wait")
        pass_on = _forward_comm(comm.results)
        _run_comm(pass_on, name + "_pass")
        self.slots.update(zip(keys, pass_on.results))
        self.fence = list(pass_on.results)

    def before(self, stage, last):
        need, nxt = FORWARD[stage]
        if need:
            self.gather_finish(list(last), "gather_" + "_".join(need))
        if nxt:
            return self.gather_start(list(nxt), "gather_" + "_".join(nxt), list(last) + self.fence)
        return None

    def w(self, key):
        if key not in self.views:
            S, D, FH, QH, KH, Fh = self.dm
            full = self.slots[key]
            if key == "fox_w_in":
                cols = full.shape[2]
                full = jnp.pad(full.transpose(1, 0, 2).reshape(D, N_CHIPS * cols), ((0, 0), (0, 3 * D + LANES - N_CHIPS * cols)))[None]
            elif key in ("fox_w_o", "swa_w_o"):
                full = full.reshape(1, D, D)
            elif key.startswith("ffn_w_down"):
                full = full.reshape(1, Fh, D)
            self.views[key] = full
        return self.views[key]

    def carry(self, stage, last=()):
        todo = []
        token = self.before(stage, last) if stage in FORWARD else None
        if token is not None:
            todo.append(("order", [], Comm([token], [], {}, 1, lambda *refs: None, lambda *refs: None)))
        for kind, key, *chunk in PLAN.get(stage, ()):
            if kind == "swap":
                todo.append((kind, [key], _swap_comm([self.raw[key]])))
            elif kind == "scatter":
                todo.append((kind, [(key, *chunk)], _scatter_comm([self.part[key]], [self.landed.get(key)], [tuple(chunk)])))
        self.pending[stage] = todo
        return _merge([cm for _, _, cm in todo])

    def carried(self, stage, comm):
        for kind, keys, cm in self.pending.pop(stage):
            if kind == "swap":
                self.part[keys[0]] = _add_sibling(self.raw[keys[0]], cm.results[0], self.c, f"add_sibling_{keys[0]}")
            elif kind == "scatter":
                self.landed[keys[0][0]] = cm.results[0]

    def grad(self, key, g):
        S, D, FH, QH, KH, Fh = self.dm
        if key == "fox_w_in":
            cols = self.slots[key].shape[2]
            g = g[0][:, :N_CHIPS * cols].reshape(D, N_CHIPS, cols).transpose(1, 0, 2)
        elif key in ("fox_w_o", "swa_w_o"):
            g = g.reshape(N_CHIPS, D // N_CHIPS, D)
        elif key.startswith("ffn_w_down"):
            g = g.reshape(N_CHIPS, Fh // N_CHIPS, D)
        self.raw[key] = g

    def last_start(self, last, after):
        part = self.part[last]
        self.last = (last, _scatter_comm([part], [lax.empty(part.shape, part.dtype)], [(0, 1, 1)]))
        self.last_state, token = _split_start(self.last[1], "grads_last_start", after)
        return token

    def join_landed(self):
        keys = list(self.landed)
        join = _join_comm([_sum_chips(self.part[k], self.landed[k], self.chip, self.c, f"sum_chips_{k}") for k in keys])
        _run_comm(join, "grads_join")
        return dict(zip(keys, join.results))

    def last_finish(self, after):
        last, comm = self.last
        _split_wait(comm, self.last_state, after, "grads_last_wait")
        join = _join_comm([_sum_chips(self.part[last], comm.results[0], self.chip, self.c, f"sum_chips_{last}")])
        _run_comm(join, "grads_join_last")
        return join.results[0]


def _step(dm, a):
    S, D, FH, QH, KH, Fh = dm
    ix, iy, ic = lax.axis_index("x"), lax.axis_index("y"), lax.axis_index("c")
    chip = 2 * ix + iy
    dev = 2 * chip + ic
    F2c = a["ffn_w_up"].shape[2]
    NC = a["ada_w"].shape[2]

    names = ["fox_w_in", "fox_w_o", "swa_w_in", "swa_w_o", "ffn_w_up", "ffn_w_up", "ffn_w_down", "ffn_w_down"]
    layers = [0, 0, 0, 0, 0, 1, 0, 1]
    keys = ["fox_w_in", "fox_w_o", "swa_w_in", "swa_w_o", "ffn_w_up0", "ffn_w_up1", "ffn_w_down0", "ffn_w_down1"]
    cast = lambda t, after: _cast_bf16(a[names[t]], layers[t], chip, f"cast_{keys[t]}", after)
    pp = Exchanges(dm, {keys[t]: cast(t, ()) for t in (0, 1)}, chip, ic)

    e0 = _cond_rows(a["c"], a["ffn_conv_w"].reshape(DEPTH * 3, F2c), "silu_c", [pp.slots[k] for k in keys[:2]])
    g0 = _allgather_small(e0, "gather_cond").reshape(N_DEV, 16, e0.shape[1])
    cact = g0[:, 0, :D]
    conv_w = g0[0::2, 8:8 + DEPTH * 3, :F2c].transpose(1, 0, 2).reshape(DEPTH, 3, N_CHIPS * F2c)
    rows = _ada_fwd(cact, a["ada_w"], a["ada_b"], ic, chip, "ada_proj")
    g1 = _allgather_small(rows, "gather_mod").reshape(N_CHIPS, DEPTH, 8, NC)
    mod = lax.dynamic_index_in_dim(g1, dev, axis=2, keepdims=False).transpose(1, 0, 2).reshape(DEPTH, N_CHIPS * NC)

    token = pp.gather_start(keys[:2], "gather_fox", [mod])
    pp.slots.update({keys[t]: cast(t, (token,)) for t in range(2, len(keys))})
    h0 = _modulate(a["x"][0], mod[0:1, D:2 * D], mod[0:1, 0:D], "mod_in")
    pp.gather_finish([pp.slots[k] for k in keys[2:]] + [h0], "gather_fox")
    sp = {"fox_b_f": _pad_lanes(a["fox_b_f"]), "sinks": jnp.repeat(a["swa_sinks"].reshape(KH, QH // KH), WIN, axis=1)[:, :, None],
          "conv_w": [conv_w[i] for i in range(DEPTH)], "conv_b": [a["ffn_conv_b"][i:i + 1] for i in range(DEPTH)]}
    for nm in ("ln_mix_g", "ln_mix_b", "ln_ffn_g", "ln_ffn_b"):
        sp[nm] = [a[nm][i:i + 1] for i in range(DEPTH)]

    loss_cols, grad_x, gs, dmod = _local_step(dm, a["x"][0], a["loss_target"][0], a["positions"][0], mod, sp, pp, h0)
    loss = lax.psum(0.5 / D * jnp.sum(loss_cols), ("x", "y", "c"))
    out = {"loss": loss, "grad_x": grad_x[None]}

    pieces = [dmod.reshape(-1), gs["fox_b_f"].reshape(-1), _pad_lanes(gs["sinks"]).reshape(-1),
              jnp.stack(gs["conv_w"]).reshape(-1), jnp.stack(gs["conv_b"]).reshape(-1)]
    pieces += [jnp.stack(gs[nm]).reshape(-1) for nm in ("ln_mix_g", "ln_mix_b", "ln_ffn_g", "ln_ffn_b")]
    sizes = [p.shape[0] for p in pieces]
    packed = _pad_rows(jnp.concatenate(pieces))
    allp = _allgather_small(packed, "gather_small").reshape(N_DEV, packed.shape[0], LANES)
    tot = _sum_devices(allp, "sum_small").reshape(-1)
    offs = [sum(sizes[:k]) for k in range(len(sizes))]
    take = lambda k: tot[offs[k]:offs[k] + sizes[k]]
    g_small = {"ada_b": take(0).reshape(DEPTH, -1), "fox_b_f": take(1)[:FH].reshape(1, FH), "swa_sinks": take(2)[:QH].reshape(1, QH),
               "ffn_conv_w": lax.dynamic_slice_in_dim(take(3).reshape(DEPTH, 3, N_CHIPS * F2c), chip * F2c, F2c, axis=2),
               "ffn_conv_b": take(4).reshape(DEPTH, -1)}
    for k, nm in enumerate(("ln_mix_g", "ln_mix_b", "ln_ffn_g", "ln_ffn_b")):
        g_small[nm] = take(5 + k).reshape(DEPTH, D)
    small = list(g_small)
    pack = lambda pre: _pad_rows(jnp.concatenate([(a[pre + nm] if pre else a[nm]).reshape(-1) for nm in small]))
    gp = _pad_rows(jnp.concatenate([g_small[nm].reshape(-1) for nm in small]))
    ds_, ms_, vs_ = _adamw_small(pack(""), gp, pack("m_"), pack("v_"), "adamw_small")
    off = 0
    for nm in small:
        n_el = a[nm].size
        out["grad_" + nm] = g_small[nm]
        for pre, arr in (("delta_", ds_), ("new_m_", ms_), ("new_v_", vs_)):
            out[pre + nm] = arr.reshape(-1)[off:off + n_el].reshape(a[nm].shape)
        off += n_el

    dmod_all = allp.reshape(N_DEV, -1)[:, :DEPTH * N_CHIPS * NC].reshape(N_DEV, DEPTH, N_CHIPS * NC)
    dmod_mine = lax.dynamic_slice_in_dim(dmod_all, chip * NC, NC, axis=2).transpose(1, 0, 2)

    grads = pp.join_landed()
    token = pp.last_start("fox_w_in", [ds_, dmod_mine] + list(grads.values()))
    ada = _ada_grad_adamw(cact.T, dmod_mine + token[0, 0], a["ada_w"], a["m_ada_w"], a["v_ada_w"], "ada_grad")
    for pre, arr in zip(("grad_", "delta_", "new_m_", "new_v_"), ada):
        out[pre + "ada_w"] = arr
    upd = {}
    for k, nm, l in zip(keys[1:], names[1:], layers[1:]):
        upd[nm] = _adamw(a[nm], grads[k], a["m_" + nm], a["v_" + nm], l, upd.get(nm, ()), f"adamw_{k}", after=(token,))
    g_last = pp.last_finish([ada[1]] + [res[1] for res in upd.values()])
    tview = lambda t: jnp.swapaxes(t, 1, 2)
    res = _adamw(tview(a["fox_w_in"]), g_last.T, tview(a["m_fox_w_in"]), tview(a["v_fox_w_in"]), 0, (), "adamw_fox_w_in", by_cols=True)
    upd["fox_w_in"] = [tview(r) for r in res]
    for nm, res in upd.items():
        for pre, arr in zip(("grad_", "delta_", "new_m_", "new_v_"), res):
            out[pre + nm] = arr
    return out


_WEIGHTS = ["fox_w_in", "fox_b_f", "fox_w_o", "swa_w_in", "swa_sinks", "swa_w_o", "ada_w", "ada_b", "ffn_w_up", "ffn_conv_w",
            "ffn_conv_b", "ffn_w_down", "ln_mix_g", "ln_mix_b", "ln_ffn_g", "ln_ffn_b"]
_INPUTS = (["x", "c", "positions"] + _WEIGHTS + ["loss_target"] + ["m_" + w for w in _WEIGHTS] + ["v_" + w for w in _WEIGHTS])


def kernel(x, c, positions, fox_w_in, fox_b_f, fox_w_o, swa_w_in, swa_sinks, swa_w_o, ada_w, ada_b, ffn_w_up, ffn_conv_w, ffn_conv_b, ffn_w_down, ln_mix_g, ln_mix_b, ln_ffn_g, ln_ffn_b, loss_target, m_fox_w_in, m_fox_b_f, m_fox_w_o, m_swa_w_in, m_swa_sinks, m_swa_w_o, m_ada_w, m_ada_b, m_ffn_w_up, m_ffn_conv_w, m_ffn_conv_b, m_ffn_w_down, m_ln_mix_g, m_ln_mix_b, m_ln_ffn_g, m_ln_ffn_b, v_fox_w_in, v_fox_b_f, v_fox_w_o, v_swa_w_in, v_swa_sinks, v_swa_w_o, v_ada_w, v_ada_b, v_ffn_w_up, v_ffn_conv_w, v_ffn_conv_b, v_ffn_w_down, v_ln_mix_g, v_ln_mix_b, v_ln_ffn_g, v_ln_ffn_b):
    args = (x, c, positions, fox_w_in, fox_b_f, fox_w_o, swa_w_in, swa_sinks, swa_w_o, ada_w, ada_b, ffn_w_up, ffn_conv_w, ffn_conv_b, ffn_w_down, ln_mix_g, ln_mix_b, ln_ffn_g, ln_ffn_b, loss_target, m_fox_w_in, m_fox_b_f, m_fox_w_o, m_swa_w_in, m_swa_sinks, m_swa_w_o, m_ada_w, m_ada_b, m_ffn_w_up, m_ffn_conv_w, m_ffn_conv_b, m_ffn_w_down, m_ln_mix_g, m_ln_mix_b, m_ln_ffn_g, m_ln_ffn_b, v_fox_w_in, v_fox_b_f, v_fox_w_o, v_swa_w_in, v_swa_sinks, v_swa_w_o, v_ada_w, v_ada_b, v_ffn_w_up, v_ffn_conv_w, v_ffn_conv_b, v_ffn_w_down, v_ln_mix_g, v_ln_mix_b, v_ln_ffn_g, v_ln_ffn_b)
    out = _step(PROD, dict(zip(_INPUTS, args)))
    order = ["loss", "grad_x"] + [p + w for p in ("grad_", "delta_", "new_m_", "new_v_") for w in _WEIGHTS]
    return tuple(out[k] for k in order)
```

```python
import functools
from typing import NamedTuple

import jax
import jax.numpy as jnp
from jax import lax
from jax.experimental import pallas as pl
from jax.experimental.pallas import tpu as pltpu

F32 = jnp.float32
BF16 = jnp.bfloat16
MESH = pl.DeviceIdType.MESH
HIGHEST = lax.Precision.HIGHEST

N_CHIPS = 4
N_DEV = 8
LANES = 128
VMEM_LIMIT = 56 * 1024 * 1024

DEPTH = 2
DEEPNORM_ALPHA = (2.0 * DEPTH) ** 0.25
LN_EPS = 1e-5
ROPE_THETA = 500000.0
ADAM_LR, ADAM_B1, ADAM_B2, ADAM_EPS, ADAM_WD, ADAM_STEP = 0.001, 0.9, 0.999, 1e-08, 0.01, 10
NEG = -1e30


class Dims(NamedTuple):
    S: int
    D: int
    FH: int
    QH: int
    KH: int
    F: int


PROD = Dims(S=2048, D=2048, FH=16, QH=32, KH=4, F=5632)
FDH = 128
SDH = 64
WIN = 128
ROPE_DIM = 16
FOX_TQ = 256


def _params(sem=None, vmem=VMEM_LIMIT):
    return pltpu.CompilerParams(dimension_semantics=sem, vmem_limit_bytes=vmem)


def _tile(n, pref, unit=LANES):
    if n <= pref:
        return n
    t = (pref // unit) * unit
    while t > 0:
        if n % t == 0:
            return t
        t -= unit
    return n


class Comm:
    def __init__(self, args, out_shapes, aliases, n_sem, start, finish, members=()):
        self.args, self.out_shapes, self.aliases, self.n_sem = list(args), list(out_shapes), dict(aliases), n_sem
        self.start, self.finish = start, finish
        self.members = members
        self.results = None

    def set_results(self, res):
        self.results = list(res)
        for cm, o0 in self.members:
            cm.set_results(self.results[o0:o0 + len(cm.out_shapes)])


class _SemView:
    def __init__(self, sems, first):
        self.sems, self.first = sems, first

    @property
    def at(self):
        return self

    def __getitem__(self, k):
        return self.sems.at[self.first + k]


def _merge(comms):
    comms = [cm for cm in comms if cm is not None]
    if len(comms) < 2:
        return comms[0] if comms else None
    args, shapes, aliases, spans, n_sem = [], [], {}, [], 0
    for cm in comms:
        spans.append((len(args), len(shapes), n_sem))
        aliases.update({len(args) + a: len(shapes) + o for a, o in cm.aliases.items()})
        args += cm.args
        shapes += cm.out_shapes
        n_sem += cm.n_sem

    def each(step):
        def run(ar, ou, send, recv):
            for cm, (a0, o0, s0) in zip(comms, spans):
                getattr(cm, step)(ar[a0:a0 + len(cm.args)], ou[o0:o0 + len(cm.out_shapes)], _SemView(send, s0), _SemView(recv, s0))
        return run

    return Comm(args, shapes, aliases, n_sem, each("start"), each("finish"), [(cm, o0) for cm, (_, o0, _) in zip(comms, spans)])


def _place():
    x, y, c = lax.axis_index("x"), lax.axis_index("y"), lax.axis_index("c")
    chips = [(1 - x, y), (x, 1 - y), (1 - x, 1 - y)]
    return x, y, c, chips


def _remote(src, dst, send, recv, to):
    return pltpu.make_async_remote_copy(src_ref=src, dst_ref=dst, send_sem=send, recv_sem=recv, device_id=to, device_id_type=MESH)


def _any_specs(n):
    return [pl.BlockSpec(memory_space=pl.ANY)] * n


def _call(body, *, name, grid, in_specs, out_specs, out_shape, args, sem, scratch_shapes=(), aliases=None, comm=None):
    in_specs, out_specs, out_shape, scratch_shapes = list(in_specs), list(out_specs), list(out_shape), list(scratch_shapes)
    aliases = dict(aliases or {})
    if comm is None:
        return pl.pallas_call(body, name=name, grid=grid, in_specs=in_specs, out_specs=out_specs, out_shape=out_shape,
                              scratch_shapes=scratch_shapes, input_output_aliases=aliases, compiler_params=_params(sem))(*args)
    n_in, n_out, nc_in, nc_out, n_scr = len(in_specs), len(out_specs), len(comm.args), len(comm.out_shapes), len(scratch_shapes)

    def wrapped(*refs):
        ins, refs = refs[:n_in], refs[n_in:]
        cin, refs = refs[:nc_in], refs[nc_in:]
        outs, refs = refs[:n_out], refs[n_out:]
        cout, refs = refs[:nc_out], refs[nc_out:]
        scratch, (send, recv) = refs[:n_scr], refs[n_scr:]
        ids = [pl.program_id(k) for k in range(len(grid))]
        first = functools.reduce(jnp.logical_and, [i == 0 for i in ids])
        last = functools.reduce(jnp.logical_and, [i == g - 1 for i, g in zip(ids, grid)])

        @pl.when(first)
        def _():
            comm.start(cin, cout, send, recv)

        body(*ins, *outs, *scratch)

        @pl.when(last)
        def _():
            comm.finish(cin, cout, send, recv)

    res = pl.pallas_call(
        wrapped, name=name, grid=grid, in_specs=in_specs + _any_specs(nc_in), out_specs=out_specs + _any_specs(nc_out),
        out_shape=out_shape + comm.out_shapes,
        scratch_shapes=scratch_shapes + [pltpu.SemaphoreType.DMA((comm.n_sem,)), pltpu.SemaphoreType.DMA((comm.n_sem,))],
        input_output_aliases={**aliases, **{n_in + a: n_out + o for a, o in comm.aliases.items()}},
        compiler_params=_params(("arbitrary",) * len(grid)),
    )(*args, *comm.args)
    comm.set_results(res[n_out:])
    return list(res[:n_out])


def _run_comm(comm, name):
    nc_in, nc_out = len(comm.args), len(comm.out_shapes)

    def body(*refs):
        cin, cout, (send, recv) = refs[:nc_in], refs[nc_in:nc_in + nc_out], refs[nc_in + nc_out:]
        comm.start(cin, cout, send, recv)
        comm.finish(cin, cout, send, recv)

    res = pl.pallas_call(
        body, name=name, in_specs=_any_specs(nc_in), out_specs=_any_specs(nc_out), out_shape=comm.out_shapes,
        scratch_shapes=[pltpu.SemaphoreType.DMA((comm.n_sem,)), pltpu.SemaphoreType.DMA((comm.n_sem,))],
        input_output_aliases=comm.aliases,
    )(*comm.args)
    comm.set_results(res)


_HBM = pl.BlockSpec(memory_space=pltpu.HBM)
_SEM = pl.BlockSpec(memory_space=pltpu.SEMAPHORE)
_EFFECT = pltpu.SideEffectType.DATAFLOW_SIDE_EFFECTING


def _split_start(comm, name, after=()):
    n = len(comm.args)
    back = {o: a for a, o in comm.aliases.items()}
    assert len(back) == len(comm.out_shapes)

    n_after = len(after)

    def body(*refs):
        refs = refs[n + n_after:]
        send, recv, thru, token = refs[0], refs[1], refs[2:n + 2], refs[n + 2]
        comm.start(thru, [thru[back[o]] for o in range(len(back))], send, recv)
        token[...] = jnp.zeros_like(token)

    res = pl.pallas_call(
        body, name=name,
        out_shape=(pltpu.SemaphoreType.DMA((comm.n_sem,)), pltpu.SemaphoreType.DMA((comm.n_sem,)),
                   *[pltpu.HBM(a.shape, a.dtype) for a in comm.args], jax.ShapeDtypeStruct((8, LANES), F32)),
        in_specs=[_HBM] * n + _any_specs(n_after), out_specs=(_SEM, _SEM, *[_HBM] * n, pl.BlockSpec(memory_space=pltpu.VMEM)),
        input_output_aliases={i: 2 + i for i in range(n)},
        compiler_params=pltpu.CompilerParams(has_side_effects=_EFFECT),
    )(*[pltpu.with_memory_space_constraint(a, pltpu.HBM) for a in comm.args], *after)
    return (res[0], res[1], list(res[2:2 + n])), res[2 + n]


def _split_wait(comm, state, after, name):
    send, recv, thru = state
    n, n_after = len(thru), len(after)
    back = {o: a for a, o in comm.aliases.items()}

    def body(*refs):
        ins, send_ref, recv_ref = refs[:n], refs[n], refs[n + 1]
        comm.finish(ins, [ins[back[o]] for o in range(len(back))], send_ref, recv_ref)

    res = pl.pallas_call(
        body, name=name, out_shape=tuple(pltpu.HBM(a.shape, a.dtype) for a in thru),
        in_specs=[_HBM] * n + [_SEM, _SEM] + _any_specs(n_after), out_specs=[_HBM] * n,
        input_output_aliases={i: i for i in range(n)}, compiler_params=pltpu.CompilerParams(has_side_effects=_EFFECT),
    )(*thru, send, recv, *after)
    comm.set_results([res[back[o]] for o in range(len(back))])


def _forward_comm(slots):
    n = len(slots)

    def rows(t, who):
        rh = slots[t].shape[1] // 2
        return pl.ds(who * rh, rh)

    def copy(outs, send, recv, t, j, chip, who):
        x, y, c, _ = _place()
        blk = outs[t].at[2 * chip[0] + chip[1], rows(t, who)]
        return _remote(blk, blk, send.at[3 * t + j], recv.at[3 * t + j], (x, y, 1 - c))

    def start(args, outs, send, recv):
        _, _, c, chips = _place()
        for t in range(n):
            for j, chip in enumerate(chips):
                copy(outs, send, recv, t, j, chip, c).start()

    def finish(args, outs, send, recv):
        _, _, c, chips = _place()
        for t in range(n):
            for j, chip in enumerate(chips):
                copy(outs, send, recv, t, j, chip, 1 - c).wait_recv()
        for t in range(n):
            for j, chip in enumerate(chips):
                copy(outs, send, recv, t, j, chip, c).wait_send()

    shapes = [jax.ShapeDtypeStruct(w.shape, w.dtype) for w in slots]
    return Comm(slots, shapes, {t: t for t in range(n)}, 3 * n, start, finish)


def _gather_comm(slots):
    n = len(slots)

    def rows(t, who):
        rh = slots[t].shape[1] // 2
        return pl.ds(who * rh, rh)

    def start(args, outs, send, recv):
        x, y, c, chips = _place()
        for t in range(n):
            mine = outs[t].at[2 * x + y, rows(t, c)]
            for j, chip in enumerate(chips):
                _remote(mine, mine, send.at[3 * t + j], recv.at[3 * t + j], (*chip, c)).start()

    def finish(args, outs, send, recv):
        x, y, c, chips = _place()
        for t in range(n):
            for j, chip in enumerate(chips):
                blk = outs[t].at[2 * chip[0] + chip[1], rows(t, c)]
                _remote(blk, blk, send.at[3 * t + j], recv.at[3 * t + j], (*chip, c)).wait_recv()
        for t in range(n):
            mine = outs[t].at[2 * x + y, rows(t, c)]
            for j, chip in enumerate(chips):
                _remote(mine, mine, send.at[3 * t + j], recv.at[3 * t + j], (*chip, c)).wait_send()

    shapes = [jax.ShapeDtypeStruct(w.shape, w.dtype) for w in slots]
    return Comm(slots, shapes, {t: t for t in range(n)}, 3 * n, start, finish)


def _scatter_comm(parts, landed, chunks):
    n = len(parts)
    prev = [t for t in range(n) if landed[t] is not None]

    def rows(t):
        lo, hi, nch = chunks[t]
        rc = parts[t].shape[1] // nch
        return pl.ds(lo * rc, (hi - lo) * rc)

    def start(args, outs, send, recv):
        x, y, c, chips = _place()
        s = 2 * x + y
        for t in range(n):
            for j, chip in enumerate(chips):
                _remote(args[t].at[2 * chip[0] + chip[1], rows(t)], outs[t].at[s, rows(t)],
                        send.at[3 * t + j], recv.at[3 * t + j], (*chip, c)).start()

    def finish(args, outs, send, recv):
        x, y, c, chips = _place()
        for t in range(n):
            for j, chip in enumerate(chips):
                blk = outs[t].at[2 * chip[0] + chip[1], rows(t)]
                _remote(blk, blk, send.at[3 * t + j], recv.at[3 * t + j], (*chip, c)).wait_recv()
        for t in range(n):
            for j, chip in enumerate(chips):
                src = args[t].at[2 * chip[0] + chip[1], rows(t)]
                _remote(src, src, send.at[3 * t + j], recv.at[3 * t + j], (*chip, c)).wait_send()

    shapes = [jax.ShapeDtypeStruct(p.shape, p.dtype) for p in parts]
    return Comm(list(parts) + [landed[t] for t in prev], shapes, {n + i: t for i, t in enumerate(prev)}, 3 * n, start, finish)


def _swap_comm(gs):
    n = len(gs)

    def copy(args, outs, send, recv, t):
        _, _, c, _ = _place()
        rh = gs[t].shape[1] // 2
        x, y = lax.axis_index("x"), lax.axis_index("y")
        return _remote(args[t].at[:, pl.ds((1 - c) * rh, rh), :], outs[t], send.at[t], recv.at[t], (x, y, 1 - c))

    def start(args, outs, send, recv):
        for t in range(n):
            copy(args, outs, send, recv, t).start()

    def finish(args, outs, send, recv):
        for t in range(n):
            copy(args, outs, send, recv, t).wait()

    shapes = [jax.ShapeDtypeStruct((g.shape[0], g.shape[1] // 2, g.shape[2]), g.dtype) for g in gs]
    return Comm(gs, shapes, {}, n, start, finish)


def _join_comm(gs):
    n = len(gs)

    def half(outs, t, who):
        rh = gs[t].shape[0] // 2
        return outs[t].at[pl.ds(who * rh, rh), :]

    def start(args, outs, send, recv):
        x, y, c, _ = _place()
        for t in range(n):
            _remote(half(outs, t, c), half(outs, t, c), send.at[t], recv.at[t], (x, y, 1 - c)).start()

    def finish(args, outs, send, recv):
        x, y, c, _ = _place()
        for t in range(n):
            _remote(half(outs, t, 1 - c), half(outs, t, 1 - c), send.at[t], recv.at[t], (x, y, 1 - c)).wait_recv()
        for t in range(n):
            _remote(half(outs, t, c), half(outs, t, c), send.at[t], recv.at[t], (x, y, 1 - c)).wait_send()

    shapes = [jax.ShapeDtypeStruct(g.shape, g.dtype) for g in gs]
    return Comm(gs, shapes, {t: t for t in range(n)}, n, start, finish)


_DN = {"nn": (((1,), (0,)), ((), ())), "nt": (((1,), (1,)), ((), ())), "tn": (((0,), (0,)), ((), ()))}


def _mm(a, b, *, mode, out_dtype, name, out_groups=1, tm=1024, tn=1024, tk=2048, comm=None):
    ga, ra, ca = a.shape
    gb, rb, cb = b.shape
    if mode == "nn":
        M, K, N = ra, ga * ca, gb * cb
        assert rb == K and ga == 1 or (rb == K)
    elif mode == "nt":
        M, K, N = ra, ga * ca, rb
        assert gb * cb == K
    else:
        K, M, N = ra, ga * ca, gb * cb
        assert rb == K
    go = out_groups
    if mode == "nn":
        tk = _tile(ca, tk); assert rb % tk == 0 and (ga == 1 or True)
        tn = _tile(min(cb, N // go), tn); tm = _tile(M, tm, 8)
    elif mode == "nt":
        tk = _tile(ca, tk); tk = _tile(cb, tk) if cb % tk else tk; assert ca % tk == 0 and cb % tk == 0
        tn = _tile(N // go, tn); tm = _tile(M, tm, 8)
    else:
        tk = _tile(K, tk, 8); tm = _tile(ca, tm); tn = _tile(min(cb, N // go), tn)
    assert (N // go) % tn == 0 and M % tm == 0 and K % tk == 0, (name, M, N, K, tm, tn, tk)
    nk = K // tk
    kpa = max(ca // tk, 1)
    kpb = max(cb // tk, 1)
    npb = max(cb // tn, 1)
    npo = (N // go) // tn
    mpa = max(ca // tm, 1)

    if mode == "nn":
        a_spec = pl.BlockSpec((1, tm, tk), lambda j, i, k: (k // kpa, i, k % kpa))
        b_spec = pl.BlockSpec((1, tk, tn), lambda j, i, k: (j // npb, k, j % npb))
    elif mode == "nt":
        a_spec = pl.BlockSpec((1, tm, tk), lambda j, i, k: (k // kpa, i, k % kpa))
        b_spec = pl.BlockSpec((1, tn, tk), lambda j, i, k: (k // kpb, j, k % kpb))
    else:
        a_spec = pl.BlockSpec((1, tk, tm), lambda j, i, k: (i // mpa, k, i % mpa))
        b_spec = pl.BlockSpec((1, tk, tn), lambda j, i, k: (j // npb, k, j % npb))
    o_spec = pl.BlockSpec((1, tm, tn), lambda j, i, k: (j // npo, i, j % npo))
    dn = _DN[mode]

    def body(a_ref, b_ref, o_ref, *acc):
        p = lax.dot_general(a_ref[0], b_ref[0], dn, preferred_element_type=F32)
        if nk == 1:
            o_ref[0] = p.astype(out_dtype)
        else:
            k = pl.program_id(2)

            @pl.when(k == 0)
            def _():
                acc[0][...] = p

            @pl.when(k > 0)
            def _():
                acc[0][...] += p

            @pl.when(k == nk - 1)
            def _():
                o_ref[0] = acc[0][...].astype(out_dtype)

    return _call(
        body, name=name, grid=(N // tn, M // tm, nk), in_specs=[a_spec, b_spec], out_specs=[o_spec],
        out_shape=[jax.ShapeDtypeStruct((go, M, N // go), out_dtype)],
        scratch_shapes=[pltpu.VMEM((tm, tn), F32)] if nk > 1 else [],
        sem=("parallel", "parallel", "arbitrary"), args=(a, b), comm=comm)[0]


def _rows(tr, d):
    return pl.BlockSpec((tr, d), lambda i: (i, 0))


def _vec(d):
    return pl.BlockSpec((1, d), lambda i: (0, 0))


def _modulate(x, sc, sh, name):
    S, D = x.shape
    tr = min(256, S)

    def body(x_ref, sc_ref, sh_ref, h_ref):
        h_ref[...] = (x_ref[...] * (1.0 + sc_ref[...]) + sh_ref[...]).astype(BF16)

    return pl.pallas_call(
        body, name=name, grid=(S // tr,), in_specs=[_rows(tr, D), _vec(D), _vec(D)], out_specs=_rows(tr, D),
        out_shape=jax.ShapeDtypeStruct((S, D), BF16), compiler_params=_params(("parallel",)),
    )(x, sc, sh)


def _ln_fwd(x, y, gate, gamma, beta, sc, sh, name, comm=None):
    S, D = x.shape
    tr = min(256, S)
    emit_h = sc is not None

    def body(*refs):
        if emit_h:
            x_ref, y_ref, g_ref, ga_ref, be_ref, sc_ref, sh_ref, xo_ref, xh_ref, rs_ref, h_ref = refs
        else:
            x_ref, y_ref, g_ref, ga_ref, be_ref, xo_ref, xh_ref, rs_ref = refs
        z = DEEPNORM_ALPHA * x_ref[...] + (1.0 + g_ref[...]) * y_ref[...]
        mu = jnp.mean(z, axis=-1, keepdims=True)
        zc = z - mu
        var = jnp.mean(zc * zc, axis=-1, keepdims=True)
        rstd = lax.rsqrt(var + LN_EPS)
        xh = zc * rstd
        xo = xh * ga_ref[...] + be_ref[...]
        xo_ref[...] = xo
        xh_ref[...] = xh
        rs_ref[...] = rstd
        if emit_h:
            h_ref[...] = (xo * (1.0 + sc_ref[...]) + sh_ref[...]).astype(BF16)

    ins = [x, y, gate, gamma, beta] + ([sc, sh] if emit_h else [])
    in_specs = [_rows(tr, D), _rows(tr, D)] + [_vec(D)] * (len(ins) - 2)
    out_shape = [jax.ShapeDtypeStruct((S, D), F32), jax.ShapeDtypeStruct((S, D), F32), jax.ShapeDtypeStruct((S, 1), F32)]
    out_specs = [_rows(tr, D), _rows(tr, D), _rows(tr, 1)]
    if emit_h:
        out_shape.append(jax.ShapeDtypeStruct((S, D), BF16))
        out_specs.append(_rows(tr, D))
    return _call(body, name=name, grid=(S // tr,), in_specs=in_specs, out_specs=out_specs, out_shape=out_shape,
                 sem=("parallel",), args=ins, comm=comm)


def _loss_head(xf, tgt, name):
    S, D = xf.shape
    tr = min(256, S)

    def body(x_ref, t_ref, dx_ref, l_ref):
        e = x_ref[...] - t_ref[...]
        dx_ref[...] = e * (1.0 / D)

        @pl.when(pl.program_id(0) == 0)
        def _():
            l_ref[...] = jnp.zeros_like(l_ref)

        l_ref[...] += jnp.sum(e * e, axis=0, keepdims=True)

    return pl.pallas_call(
        body, name=name, grid=(S // tr,), in_specs=[_rows(tr, D), _rows(tr, D)],
        out_specs=[_rows(tr, D), _vec(D)],
        out_shape=[jax.ShapeDtypeStruct((S, D), F32), jax.ShapeDtypeStruct((1, D), F32)],
        compiler_params=_params(("arbitrary",)),
    )(xf, tgt)


def _ln_bwd(dxo, xh, rstd, gamma, y, gate, name, pre=None):
    S, D = dxo.shape
    tr = min(256, S)
    n_pre = 0 if pre is None else 3

    def body(dx_ref, xh_ref, rs_ref, ga_ref, y_ref, g_ref, *rest):
        dres_ref, dy_ref, dga_ref, dbe_ref, dg_ref = rest[n_pre:n_pre + 5]
        first = pl.program_id(0) == 0
        dxo_ = dx_ref[...]
        xh_ = xh_ref[...]
        if pre is not None:
            dh_ref, sc_ref, be_ref = rest[:3]
            dsc_ref, dsh_ref = rest[n_pre + 5:]
            dh_ = dh_ref[...]
            dxo_ = dxo_ + dh_ * (1.0 + sc_ref[...])

            @pl.when(first)
            def _():
                dsc_ref[...] = jnp.zeros_like(dsc_ref)
                dsh_ref[...] = jnp.zeros_like(dsh_ref)

            dsc_ref[...] += jnp.sum(dh_ * (xh_ * ga_ref[...] + be_ref[...]), axis=0, keepdims=True)
            dsh_ref[...] += jnp.sum(dh_, axis=0, keepdims=True)
        dxh = dxo_ * ga_ref[...]
        m1 = jnp.mean(dxh, axis=-1, keepdims=True)
        m2 = jnp.mean(dxh * xh_, axis=-1, keepdims=True)
        dz = rs_ref[...] * (dxh - m1 - xh_ * m2)
        dres_ref[...] = DEEPNORM_ALPHA * dz
        dy_ref[...] = ((1.0 + g_ref[...]) * dz).astype(BF16)

        @pl.when(first)
        def _():
            dga_ref[...] = jnp.zeros_like(dga_ref)
            dbe_ref[...] = jnp.zeros_like(dbe_ref)
            dg_ref[...] = jnp.zeros_like(dg_ref)

        dga_ref[...] += jnp.sum(dxo_ * xh_, axis=0, keepdims=True)
        dbe_ref[...] += jnp.sum(dxo_, axis=0, keepdims=True)
        dg_ref[...] += jnp.sum(dz * y_ref[...], axis=0, keepdims=True)

    extra_in = [] if pre is None else [_rows(tr, D), _vec(D), _vec(D)]
    return pl.pallas_call(
        body, name=name, grid=(S // tr,),
        in_specs=[_rows(tr, D), _rows(tr, D), _rows(tr, 1), _vec(D), _rows(tr, D), _vec(D)] + extra_in,
        out_specs=[_rows(tr, D), _rows(tr, D)] + [_vec(D)] * (3 + (0 if pre is None else 2)),
        out_shape=[jax.ShapeDtypeStruct((S, D), F32), jax.ShapeDtypeStruct((S, D), BF16)]
        + [jax.ShapeDtypeStruct((1, D), F32)] * (3 + (0 if pre is None else 2)),
        compiler_params=_params(("arbitrary",)),
    )(dxo, xh, rstd, gamma, y, gate, *(pre or ()))


def _mod_bwd(dh, x, sc, dres, name):
    S, D = x.shape
    tr = min(256, S)

    def body(dh_ref, x_ref, sc_ref, dr_ref, dx_ref, dsc_ref, dsh_ref):
        dh_ = dh_ref[...]
        dx_ref[...] = dr_ref[...] + dh_ * (1.0 + sc_ref[...])

        @pl.when(pl.program_id(0) == 0)
        def _():
            dsc_ref[...] = jnp.zeros_like(dsc_ref)
            dsh_ref[...] = jnp.zeros_like(dsh_ref)

        dsc_ref[...] += jnp.sum(dh_ * x_ref[...], axis=0, keepdims=True)
        dsh_ref[...] += jnp.sum(dh_, axis=0, keepdims=True)

    return pl.pallas_call(
        body, name=name, grid=(S // tr,),
        in_specs=[_rows(tr, D), _rows(tr, D), _vec(D), _rows(tr, D)],
        out_specs=[_rows(tr, D), _vec(D), _vec(D)],
        out_shape=[jax.ShapeDtypeStruct((S, D), F32), jax.ShapeDtypeStruct((1, D), F32), jax.ShapeDtypeStruct((1, D), F32)],
        compiler_params=_params(("arbitrary",)),
    )(dh, x, sc, dres)


def _log_sigmoid(z):
    return jnp.minimum(z, 0.0) - jnp.log(1.0 + jnp.exp(-jnp.abs(z)))


def _fox_gate_fwd(proj, b_f, n_heads, name):
    S, PW = proj.shape
    blk = min(256, S)
    last = PW // LANES - 1

    def body(fl_ref, b_ref, cum_ref):
        r = lax.broadcasted_iota(jnp.int32, (blk, blk), 0)
        c = lax.broadcasted_iota(jnp.int32, (blk, blk), 1)
        tril = (c <= r).astype(F32)
        carry = jnp.zeros((1, LANES), F32)
        for i in range(S // blk):
            lf = _log_sigmoid(fl_ref[i * blk:(i + 1) * blk, :] + b_ref[...])
            cum_ref[i * blk:(i + 1) * blk, :] = jnp.dot(tril, lf, preferred_element_type=F32, precision=HIGHEST) + carry
            carry = carry + jnp.sum(lf, axis=0, keepdims=True)

    return pl.pallas_call(
        body, name=name, grid=(1,),
        in_specs=[pl.BlockSpec((S, LANES), lambda i: (0, last)), pl.BlockSpec((1, LANES), lambda i: (0, 0))],
        out_specs=pl.BlockSpec((S, LANES), lambda i: (0, 0)),
        out_shape=jax.ShapeDtypeStruct((S, LANES), F32), compiler_params=_params(("arbitrary",)),
    )(proj, b_f)


def _fox_gate_bwd(dcum, proj, b_f, n_heads, name):
    S, PW = proj.shape
    blk = min(256, S)
    last = PW // LANES - 1
    nb = S // blk

    def body(dc_ref, fl_ref, b_ref, dfl_ref, db_ref):
        r = lax.broadcasted_iota(jnp.int32, (blk, blk), 0)
        c = lax.broadcasted_iota(jnp.int32, (blk, blk), 1)
        triu = (c >= r).astype(F32)
        lane = lax.broadcasted_iota(jnp.int32, (blk, LANES), 1)
        carry = jnp.zeros((1, LANES), F32)
        dbs = jnp.zeros((1, LANES), F32)
        for i in reversed(range(nb)):
            dc = dc_ref[i * blk:(i + 1) * blk, :]
            dlf = jnp.dot(triu, dc, preferred_element_type=F32, precision=HIGHEST) + carry
            carry = carry + jnp.sum(dc, axis=0, keepdims=True)
            z = fl_ref[i * blk:(i + 1) * blk, :] + b_ref[...]
            e = jnp.exp(-jnp.abs(z))
            sig_neg = jnp.where(z >= 0, e / (1.0 + e), 1.0 / (1.0 + e))
            dfl = jnp.where(lane < n_heads, dlf * sig_neg, 0.0)
            dfl_ref[i * blk:(i + 1) * blk, :] = dfl.astype(BF16)
            dbs = dbs + jnp.sum(dfl, axis=0, keepdims=True)
        db_ref[...] = dbs

    return pl.pallas_call(
        body, name=name, grid=(1,),
        in_specs=[pl.BlockSpec((S, LANES), lambda i: (0, 0)), pl.BlockSpec((S, LANES), lambda i: (0, last)),
                  pl.BlockSpec((1, LANES), lambda i: (0, 0))],
        out_specs=[pl.BlockSpec((S, LANES), lambda i: (0, 0)), pl.BlockSpec((1, LANES), lambda i: (0, 0))],
        out_shape=[jax.ShapeDtypeStruct((S, LANES), BF16), jax.ShapeDtypeStruct((1, LANES), F32)],
        compiler_params=_params(("arbitrary",)),
    )(dcum, proj, b_f)


def _fox_scores(q_ref, kb_ref, cq_ref, ck_ref, qi, tq, scale):
    kk = (qi + 1) * tq
    rows = slice(qi * tq, (qi + 1) * tq)
    qb = q_ref[rows, :].astype(BF16)
    s = lax.dot_general(qb, kb_ref[0:kk, :], _DN["nt"], preferred_element_type=F32) * scale
    s = s + (cq_ref[0, rows, :] - ck_ref[0, :, 0:kk])
    r = lax.broadcasted_iota(jnp.int32, (tq, kk), 0) + qi * tq
    c = lax.broadcasted_iota(jnp.int32, (tq, kk), 1)
    mask = c <= r
    return jnp.where(mask, s, NEG), mask, qb


def _fox_fwd(proj, cq, ck, n_heads, name, comm=None):
    S = proj.shape[0]
    H = n_heads
    tq = min(FOX_TQ, S)
    nq = S // tq
    scale = FDH ** -0.5

    def body(q_ref, k_ref, v_ref, cq_ref, ck_ref, o_ref, lse_ref, kb_ref, vb_ref):
        kb_ref[...] = k_ref[...].astype(BF16)
        vb_ref[...] = v_ref[...].astype(BF16)
        for qi in range(nq):
            kk = (qi + 1) * tq
            rows = slice(qi * tq, (qi + 1) * tq)
            s, _, _ = _fox_scores(q_ref, kb_ref, cq_ref, ck_ref, qi, tq, scale)
            m = jnp.max(s, axis=-1, keepdims=True)
            p = jnp.exp(s - m)
            l = jnp.sum(p, axis=-1, keepdims=True)
            p = p * (1.0 / l)
            o_ref[rows, :] = jnp.dot(p.astype(BF16), vb_ref[0:kk, :], preferred_element_type=F32).astype(BF16)
            lse_ref[0, rows, :] = m + jnp.log(l)

    col = lambda off: pl.BlockSpec((S, FDH), lambda h: (0, h + off))
    stat_c = pl.BlockSpec((1, S, 1), lambda h: (h, 0, 0))
    stat_r = pl.BlockSpec((1, 1, S), lambda h: (h, 0, 0))
    return _call(
        body, name=name, grid=(H,),
        in_specs=[col(0), col(H), col(2 * H), stat_c, stat_r],
        out_specs=[col(0), stat_c],
        out_shape=[jax.ShapeDtypeStruct((S, H * FDH), BF16), jax.ShapeDtypeStruct((H, S, 1), F32)],
        scratch_shapes=[pltpu.VMEM((S, FDH), BF16), pltpu.VMEM((S, FDH), BF16)],
        sem=("parallel",), args=(proj, proj, proj, cq, ck), comm=comm)


def _fox_bwd(proj, cq, ck, lse, do, n_heads, name, comm=None):
    S = proj.shape[0]
    H = n_heads
    tq = min(FOX_TQ, S)
    nq = S // tq
    scale = FDH ** -0.5

    def body(q_ref, k_ref, v_ref, cq_ref, ck_ref, lse_ref, do_ref, dq_ref, dk_ref, dv_ref, dcq_ref, dck_ref,
             kb_ref, vb_ref, dka_ref, dva_ref):
        kb_ref[...] = k_ref[...].astype(BF16)
        vb_ref[...] = v_ref[...].astype(BF16)
        dka_ref[...] = jnp.zeros_like(dka_ref)
        dva_ref[...] = jnp.zeros_like(dva_ref)
        dck_ref[...] = jnp.zeros_like(dck_ref)
        for qi in range(nq):
            kk = (qi + 1) * tq
            rows = slice(qi * tq, (qi + 1) * tq)
            s, mask, qb = _fox_scores(q_ref, kb_ref, cq_ref, ck_ref, qi, tq, scale)
            p = jnp.where(mask, jnp.exp(s - lse_ref[0, rows, :]), 0.0)
            dob = do_ref[rows, :]
            dp = lax.dot_general(dob, vb_ref[0:kk, :], _DN["nt"], preferred_element_type=F32)
            delta = jnp.sum(p * dp, axis=-1, keepdims=True)
            ds = p * (dp - delta)
            dcq_ref[0, rows, :] = jnp.sum(ds, axis=-1, keepdims=True)
            dck_ref[0, :, 0:kk] -= jnp.sum(ds, axis=0, keepdims=True)
            dsb = (ds * scale).astype(BF16)
            dq_ref[rows, :] = jnp.dot(dsb, kb_ref[0:kk, :], preferred_element_type=F32).astype(BF16)
            dka_ref[0:kk, :] += lax.dot_general(dsb, qb, _DN["tn"], preferred_element_type=F32)
            dva_ref[0:kk, :] += lax.dot_general(p.astype(BF16), dob, _DN["tn"], preferred_element_type=F32)
        dk_ref[...] = dka_ref[...].astype(BF16)
        dv_ref[...] = dva_ref[...].astype(BF16)

    col = lambda off: pl.BlockSpec((S, FDH), lambda h: (0, h + off))
    stat_c = pl.BlockSpec((1, S, 1), lambda h: (h, 0, 0))
    stat_r = pl.BlockSpec((1, 1, S), lambda h: (h, 0, 0))
    wide = jax.ShapeDtypeStruct((S, H * FDH), BF16)
    return _call(
        body, name=name, grid=(H,),
        in_specs=[col(0), col(H), col(2 * H), stat_c, stat_r, stat_c, col(0)],
        out_specs=[col(0), col(0), col(0), stat_c, stat_r],
        out_shape=[wide, wide, wide, jax.ShapeDtypeStruct((H, S, 1), F32), jax.ShapeDtypeStruct((H, 1, S), F32)],
        scratch_shapes=[pltpu.VMEM((S, FDH), BF16), pltpu.VMEM((S, FDH), BF16), pltpu.VMEM((S, FDH), F32), pltpu.VMEM((S, FDH), F32)],
        sem=("parallel",), args=(proj, proj, proj, cq, ck, lse, do), comm=comm)


def _rope_tables(pos, sign):
    inv = ROPE_THETA ** (-jnp.arange(0, ROPE_DIM, 2, dtype=F32) / ROPE_DIM)
    ang = pos.astype(F32)[:, None] * inv
    cos, sin = jnp.cos(ang), sign * jnp.sin(ang)
    l64 = jnp.arange(LANES) % SDH
    idx = l64 % (ROPE_DIM // 2)
    c = jnp.where(l64 < ROPE_DIM, cos[:, idx], 1.0)
    sa = jnp.where(l64 < ROPE_DIM // 2, -sin[:, idx], 0.0)
    sb = jnp.where((l64 >= ROPE_DIM // 2) & (l64 < ROPE_DIM), sin[:, idx], 0.0)
    rot = jnp.stack([c, sa, sb])
    ident = jnp.stack([jnp.ones_like(c), jnp.zeros_like(c), jnp.zeros_like(c)])
    return jnp.stack([rot, ident]).astype(F32)


def _rope(xin, tabs, n_rot, out_dtype, name, comm=None):
    S, W = xin.shape

    def body(x_ref, t_ref, o_ref):
        xv = x_ref[...]
        o = xv * t_ref[0, 0] + pltpu.roll(xv, LANES - ROPE_DIM // 2, 1) * t_ref[0, 1] + pltpu.roll(xv, ROPE_DIM // 2, 1) * t_ref[0, 2]
        o_ref[...] = o.astype(out_dtype)

    return _call(
        body, name=name, grid=(W // LANES,),
        in_specs=[pl.BlockSpec((S, LANES), lambda j: (0, j)),
                  pl.BlockSpec((1, 3, S, LANES), lambda j: (jnp.where(j < n_rot, 0, 1), 0, 0, 0))],
        out_specs=[pl.BlockSpec((S, LANES), lambda j: (0, j))],
        out_shape=[jax.ShapeDtypeStruct((S, W), out_dtype)], sem=("parallel",), args=(xin, tabs), comm=comm)[0]


SWA_PER_STEP = 8


def _swa_bias():
    r = jnp.arange(WIN)[:, None]
    c = jnp.arange(2 * WIN)[None, :]
    first = c <= r
    later = (c > r) & (c <= r + WIN)
    return jnp.where(jnp.stack([first, later]), 0.0, NEG).astype(F32)


def _swa_probs(q_ref, k_ref, sk_ref, b_ref, n, j, scale):
    st = pl.multiple_of(jnp.maximum(n - 1, 0) * WIN, WIN)
    qb = q_ref[0, j]
    kb = k_ref[0, pl.ds(st, 2 * WIN), :]
    gm = qb.shape[0]
    s = lax.dot_general(qb, kb, _DN["nt"], preferred_element_type=F32) * scale
    s = (s.reshape(gm // WIN, WIN, 2 * WIN) + b_ref[jnp.minimum(n, 1)][None]).reshape(gm, 2 * WIN)
    sink = sk_ref[0]
    m = jnp.maximum(jnp.max(s, axis=-1, keepdims=True), sink)
    e = jnp.exp(s - m)
    es = jnp.exp(sink - m)
    inv = 1.0 / (jnp.sum(e, axis=-1, keepdims=True) + es)
    return e * inv, es * inv, st, qb, kb


def _swa_specs(S, gm):
    blk = pl.BlockSpec((1, SWA_PER_STEP, gm, SDH), lambda g, n: (g, n, 0, 0))
    kv = pl.BlockSpec((1, S, SDH), lambda g, n: (g, 0, 0))
    col = pl.BlockSpec((1, gm, 1), lambda g, n: (g, 0, 0))
    bias = pl.BlockSpec((2, WIN, 2 * WIN), lambda g, n: (0, 0, 0))
    return blk, kv, col, bias


def _swa_fwd(q, k, v, sinks, name, comm=None):
    KH, nb, gm, _ = q.shape
    S = k.shape[1]
    scale = SDH ** -0.5

    def body(q_ref, k_ref, v_ref, sk_ref, b_ref, o_ref):
        for j in range(SWA_PER_STEP):
            p, _, st, _, _ = _swa_probs(q_ref, k_ref, sk_ref, b_ref, pl.program_id(1) * SWA_PER_STEP + j, j, scale)
            vb = v_ref[0, pl.ds(st, 2 * WIN), :]
            o_ref[0, j] = jnp.dot(p.astype(BF16), vb, preferred_element_type=F32).astype(BF16)

    blk, kv, col, bias = _swa_specs(S, gm)
    return _call(
        body, name=name, grid=(KH, nb // SWA_PER_STEP), in_specs=[blk, kv, kv, col, bias], out_specs=[blk],
        out_shape=[jax.ShapeDtypeStruct(q.shape, BF16)], sem=("parallel", "parallel"), args=(q, k, v, sinks, _swa_bias()), comm=comm)[0]


def _swa_bwd(q, k, v, sinks, do, name, comm=None):
    KH, nb, gm, _ = q.shape
    S = k.shape[1]
    scale = SDH ** -0.5

    def body(q_ref, k_ref, v_ref, sk_ref, b_ref, do_ref, dq_ref, dk_ref, dv_ref, dsk_ref):
        @pl.when(pl.program_id(1) == 0)
        def _():
            dk_ref[...] = jnp.zeros_like(dk_ref)
            dv_ref[...] = jnp.zeros_like(dv_ref)
            dsk_ref[...] = jnp.zeros_like(dsk_ref)

        blocks = []
        for j in range(SWA_PER_STEP):
            p, ps, st, qb, kb = _swa_probs(q_ref, k_ref, sk_ref, b_ref, pl.program_id(1) * SWA_PER_STEP + j, j, scale)
            vb = v_ref[0, pl.ds(st, 2 * WIN), :]
            dob = do_ref[0, j]
            dp = lax.dot_general(dob, vb, _DN["nt"], preferred_element_type=F32)
            delta = jnp.sum(p * dp, axis=-1, keepdims=True)
            dsb = (p * (dp - delta) * scale).astype(BF16)
            dq_ref[0, j] = jnp.dot(dsb, kb, preferred_element_type=F32)
            blocks.append((st, lax.dot_general(dsb, qb, _DN["tn"], preferred_element_type=F32),
                           lax.dot_general(p.astype(BF16), dob, _DN["tn"], preferred_element_type=F32), ps * delta))
        for st, dk, dv, dsk in blocks:
            dk_ref[0, pl.ds(st, 2 * WIN), :] += dk
            dv_ref[0, pl.ds(st, 2 * WIN), :] += dv
            dsk_ref[0] -= dsk

    blk, kv, col, bias = _swa_specs(S, gm)
    return _call(
        body, name=name, grid=(KH, nb // SWA_PER_STEP), in_specs=[blk, kv, kv, col, bias, blk], out_specs=[blk, kv, kv, col],
        out_shape=[jax.ShapeDtypeStruct(q.shape, F32), jax.ShapeDtypeStruct(k.shape, F32),
                   jax.ShapeDtypeStruct(k.shape, F32), jax.ShapeDtypeStruct(sinks.shape, F32)],
        sem=("parallel", "arbitrary"), args=(q, k, v, sinks, _swa_bias(), do), comm=comm)


def _shift_down(u, k):
    row = lax.broadcasted_iota(jnp.int32, u.shape, 0)
    return jnp.where(row >= k, pltpu.roll(u, k, 0), 0.0)


def _shift_up(u, k):
    n = u.shape[0]
    row = lax.broadcasted_iota(jnp.int32, u.shape, 0)
    return jnp.where(row < n - k, pltpu.roll(u, n - k, 0), 0.0)


def _conv3(u, w_ref, b_ref):
    return w_ref[0:1, :] * _shift_down(u, 2) + w_ref[1:2, :] * _shift_down(u, 1) + w_ref[2:3, :] * u + b_ref[...]


def _conv_gate(u, cw, cb, name, comm=None):
    S, F2 = u.shape
    Fh = F2 // 2
    tc = _tile(Fh, 256)
    nf = Fh // tc

    def body(ug_ref, uv_ref, wg_ref, wv_ref, bg_ref, bv_ref, a_ref):
        g = _conv3(ug_ref[...], wg_ref, bg_ref)
        val = _conv3(uv_ref[...], wv_ref, bv_ref)
        a_ref[...] = (g * (1.0 / (1.0 + jnp.exp(-g))) * val).astype(BF16)

    blk = lambda r, off: pl.BlockSpec((r, tc), lambda j: (0, j + off))
    return _call(
        body, name=name, grid=(nf,),
        in_specs=[blk(S, 0), blk(S, nf), blk(3, 0), blk(3, nf), blk(1, 0), blk(1, nf)], out_specs=[blk(S, 0)],
        out_shape=[jax.ShapeDtypeStruct((S, Fh), BF16)], sem=("parallel",), args=(u, u, cw, cw, cb, cb), comm=comm)[0]


def _conv_gate_bwd(u, da, cw, cb, name, comm=None):
    S, F2 = u.shape
    Fh = F2 // 2
    tc = _tile(Fh, 256)
    nf = Fh // tc

    def half(h, dx, uu, w_ref, du_ref, dw_ref, db_ref):
        up1, up2 = _shift_up(dx, 1), _shift_up(dx, 2)
        du = w_ref[2:3, :] * dx + w_ref[1:2, :] * up1 + w_ref[0:1, :] * up2
        du_ref[h] = du.astype(BF16)
        dw_ref[h, 0:1, :] = jnp.sum(up2 * uu, axis=0, keepdims=True)
        dw_ref[h, 1:2, :] = jnp.sum(up1 * uu, axis=0, keepdims=True)
        dw_ref[h, 2:3, :] = jnp.sum(dx * uu, axis=0, keepdims=True)
        db_ref[h] = jnp.sum(dx, axis=0, keepdims=True)

    def body(ug_ref, uv_ref, da_ref, wg_ref, wv_ref, bg_ref, bv_ref, du_ref, dw_ref, db_ref):
        ug = ug_ref[...]
        uv = uv_ref[...]
        g = _conv3(ug, wg_ref, bg_ref)
        val = _conv3(uv, wv_ref, bv_ref)
        sig = 1.0 / (1.0 + jnp.exp(-g))
        da_ = da_ref[...]
        dg = da_ * val * (sig * (1.0 + g * (1.0 - sig)))
        dval = da_ * (g * sig)
        half(0, dg, ug, wg_ref, du_ref, dw_ref, db_ref)
        half(1, dval, uv, wv_ref, du_ref, dw_ref, db_ref)

    blk = lambda r, off: pl.BlockSpec((r, tc), lambda j: (0, j + off))
    both = lambda r: pl.BlockSpec((2, r, tc), lambda j: (0, 0, j))
    return _call(
        body, name=name, grid=(nf,),
        in_specs=[blk(S, 0), blk(S, nf), blk(S, 0), blk(3, 0), blk(3, nf), blk(1, 0), blk(1, nf)],
        out_specs=[both(S), both(3), both(1)],
        out_shape=[jax.ShapeDtypeStruct((2, S, Fh), BF16), jax.ShapeDtypeStruct((2, 3, Fh), F32), jax.ShapeDtypeStruct((2, 1, Fh), F32)],
        sem=("parallel",), args=(u, u, da, cw, cw, cb, cb), comm=comm)


def _to_groups(t, kh):
    S, width = t.shape
    g = width // SDH // kh
    return t.reshape(S // WIN, WIN, kh, g, SDH).transpose(2, 0, 3, 1, 4).reshape(kh, S // WIN, g * WIN, SDH)


def _from_groups(t):
    kh, nb, gm, _ = t.shape
    g = gm // WIN
    return t.reshape(kh, nb, g, WIN, SDH).transpose(1, 3, 0, 2, 4).reshape(nb * WIN, kh * g * SDH)


class LocalWeights:
    def __init__(self, weights):
        self.weights, self.grads = weights, {}

    def w(self, name):
        return self.weights[name]

    def carry(self, stage, last=()):
        return None

    def carried(self, stage, comm):
        pass

    def grad(self, name, g):
        self.grads[name] = g


def _local_step(dm, x, tgt, pos, mod, sp, pp, h0=None):
    S, D, FH, QH, KH, Fh = dm
    m = [[mod[i:i + 1, j * D:(j + 1) * D] for j in range(6)] for i in range(DEPTH)]

    last = []

    def run(fn, *args, name, **kw):
        comm = pp.carry(name, last)
        args = [pp.w(arg[1]) if isinstance(arg, tuple) and arg[:1] == ("w",) else arg for arg in args]
        out = fn(*args, name=name, comm=comm, **kw)
        if comm is not None:
            pp.carried(name, comm)
        last[:] = list(out) if isinstance(out, (list, tuple)) else [out]
        return out

    sv = []
    xs = x
    h = _modulate(xs, m[0][1], m[0][0], "mod_in") if h0 is None else h0
    last[:] = [h]
    for i in range(DEPTH):
        sh1, sc1, g1, sh2, sc2, g2 = m[i]
        L = {}
        L["x_in"], L["h1"] = xs, h
        if i == 0:
            proj = run(_mm, h[None], ("w", "fox_w_in"), mode="nn", out_dtype=F32, name="fox_proj", tn=896)[0]
            cum = _fox_gate_fwd(proj, sp["fox_b_f"], FH, "fox_gate")
            cq = cum[:, :FH].T[:, :, None]
            ck = cum[:, :FH].T[:, None, :]
            o, lse = run(_fox_fwd, proj, cq, ck, FH, name="fox_attn")
            L.update(proj=proj, cq=cq, ck=ck, lse=lse, o=o)
            y = run(_mm, o[None], ("w", "fox_w_o"), mode="nn", out_dtype=F32, name="fox_out")[0]
        else:
            proj = run(_mm, h[None], ("w", "swa_w_in"), mode="nn", out_dtype=F32, name="swa_proj", tn=640)[0]
            tabs = _rope_tables(pos, 1.0)
            n_rot = (QH + KH) * SDH // LANES
            pr = run(_rope, proj, tabs, n_rot, BF16, name="swa_rope")
            qh = _to_groups(pr[:, :QH * SDH], KH)
            kh = pr[:, QH * SDH:(QH + KH) * SDH].reshape(S, KH, SDH).transpose(1, 0, 2)
            vh = pr[:, (QH + KH) * SDH:].reshape(S, KH, SDH).transpose(1, 0, 2)
            oh = run(_swa_fwd, qh, kh, vh, sp["sinks"], name="swa_attn")
            o = _from_groups(oh)
            L.update(qh=qh, kh=kh, vh=vh, o=o)
            y = run(_mm, o[None], ("w", "swa_w_o"), mode="nn", out_dtype=F32, name="swa_out")[0]
        L["y1"] = y
        x1, L["xh1"], L["rs1"], h2 = run(_ln_fwd, xs, y, g1, sp["ln_mix_g"][i], sp["ln_mix_b"][i], sc2, sh2, name=f"ln_mix{i}")
        L["x1"], L["h2"] = x1, h2
        u = run(_mm, h2[None], ("w", f"ffn_w_up{i}"), mode="nn", out_dtype=F32, name=f"ffn_up{i}", tm=1024, tn=1408)[0]
        a = run(_conv_gate, u, sp["conv_w"][i], sp["conv_b"][i], name=f"ffn_gate{i}")
        y2 = run(_mm, a[None], ("w", f"ffn_w_down{i}"), mode="nn", out_dtype=F32, name=f"ffn_down{i}", tk=5632, tm=512)[0]
        L.update(u=u, a=a, y2=y2)
        if i + 1 < DEPTH:
            xs, L["xh2"], L["rs2"], h = run(_ln_fwd, x1, y2, g2, sp["ln_ffn_g"][i], sp["ln_ffn_b"][i], m[i + 1][1], m[i + 1][0], name=f"ln_ffn{i}")
        else:
            xs, L["xh2"], L["rs2"] = run(_ln_fwd, x1, y2, g2, sp["ln_ffn_g"][i], sp["ln_ffn_b"][i], None, None, name=f"ln_ffn{i}")
        sv.append(L)

    dx, loss_cols = _loss_head(xs, tgt, "loss_head")

    gs = {k: [None] * DEPTH for k in ("conv_w", "conv_b", "ln_mix_g", "ln_mix_b", "ln_ffn_g", "ln_ffn_b")}
    dmp = [dict() for _ in range(DEPTH)]
    dres, pend = dx, None
    for i in reversed(range(DEPTH)):
        sh1, sc1, g1, sh2, sc2, g2 = m[i]
        L = sv[i]
        res = _ln_bwd(dres, L["xh2"], L["rs2"], sp["ln_ffn_g"][i], L["y2"], g2, f"ln_ffn_bwd{i}",
                      None if pend is None else (*pend, sp["ln_ffn_b"][i]))
        dres, dy, gs["ln_ffn_g"][i], gs["ln_ffn_b"][i], dmp[i]["g2"] = res[:5]
        if pend is not None:
            dmp[i + 1]["sc1"], dmp[i + 1]["sh1"] = res[5:]
        da = run(_mm, dy[None], pp.w(f"ffn_w_down{i}"), mode="nt", out_dtype=F32, name=f"ffn_down_dx{i}", tm=1024, tn=1408)[0]
        pp.grad(f"ffn_w_down{i}", run(_mm, L["a"][None], dy[None], mode="tn", out_dtype=BF16, name=f"ffn_down_dw{i}", tm=1408))
        du, dcw, dcb = run(_conv_gate_bwd, L["u"], da, sp["conv_w"][i], sp["conv_b"][i], name=f"ffn_gate_bwd{i}")
        gs["conv_w"][i] = dcw.transpose(1, 0, 2).reshape(3, 2 * Fh)
        gs["conv_b"][i] = dcb.transpose(1, 0, 2).reshape(1, 2 * Fh)
        dh2 = run(_mm, du, pp.w(f"ffn_w_up{i}"), mode="nt", out_dtype=F32, name=f"ffn_up_dx{i}", tk=2816)[0]
        pp.grad(f"ffn_w_up{i}", run(_mm, L["h2"][None], du, mode="tn", out_dtype=BF16, name=f"ffn_up_dw{i}", out_groups=N_CHIPS, tn=1408))
        dres, dy, gs["ln_mix_g"][i], gs["ln_mix_b"][i], dmp[i]["g1"], dmp[i]["sc2"], dmp[i]["sh2"] = _ln_bwd(
            dres, L["xh1"], L["rs1"], sp["ln_mix_g"][i], L["y1"], g1, f"ln_mix_bwd{i}", (dh2, sc2, sp["ln_mix_b"][i]))
        if i == 0:
            do = run(_mm, dy[None], pp.w("fox_w_o"), mode="nt", out_dtype=BF16, name="fox_out_dx")[0]
            pp.grad("fox_w_o", run(_mm, L["o"][None], dy[None], mode="tn", out_dtype=BF16, name="fox_out_dw"))
            dq, dk, dv, dcq, dck = run(_fox_bwd, L["proj"], L["cq"], L["ck"], L["lse"], do, FH, name="fox_attn_bwd")
            dcum = dcq[:, :, 0].T + dck[:, 0, :].T
            dcum = jnp.pad(dcum, ((0, 0), (0, LANES - FH)))
            dfl, db_f = _fox_gate_bwd(dcum, L["proj"], sp["fox_b_f"], FH, "fox_gate_bwd")
            gs["fox_b_f"] = db_f
            dproj = jnp.concatenate([dq, dk, dv, dfl], axis=1)
            pp.grad("fox_w_in", run(_mm, L["h1"][None], dproj[None], mode="tn", out_dtype=BF16, name="fox_proj_dw", tn=896))
            dh1 = run(_mm, dproj[None], pp.w("fox_w_in"), mode="nt", out_dtype=F32, name="fox_proj_dx", tk=6272, tm=512)[0]
        else:
            do = run(_mm, dy[None], pp.w("swa_w_o"), mode="nt", out_dtype=BF16, name="swa_out_dx")[0]
            pp.grad("swa_w_o", run(_mm, L["o"][None], dy[None], mode="tn", out_dtype=BF16, name="swa_out_dw"))
            dqh, dkh, dvh, dsk = run(_swa_bwd, L["qh"], L["kh"], L["vh"], sp["sinks"], _to_groups(do, KH), name="swa_attn_bwd")
            gs["sinks"] = jnp.sum(dsk.reshape(QH, WIN), axis=1)
            dpr = jnp.concatenate([_from_groups(dqh), dkh.transpose(1, 0, 2).reshape(S, KH * SDH),
                                   dvh.transpose(1, 0, 2).reshape(S, KH * SDH)], axis=1)
            n_rot = (QH + KH) * SDH // LANES
            dproj = _rope(dpr, _rope_tables(pos, -1.0), n_rot, BF16, "swa_rope_bwd")
            dh1 = run(_mm, dproj[None], pp.w("swa_w_in"), mode="nt", out_dtype=F32, name="swa_proj_dx", tk=640)[0]
            pp.grad("swa_w_in", run(_mm, L["h1"][None], dproj[None], mode="tn", out_dtype=BF16, name="swa_proj_dw", out_groups=N_CHIPS, tn=640))
        pend = (dh1, sc1)
    grad_x, dmp[0]["sc1"], dmp[0]["sh1"] = _mod_bwd(pend[0], sv[0]["x_in"], pend[1], dres, "mod_mix_bwd0")
    dmod = [jnp.concatenate([p["sh1"], p["sc1"], p["g1"], p["sh2"], p["sc2"], p["g2"]], axis=1) for p in dmp]
    return loss_cols, grad_x, gs, jnp.concatenate(dmod, axis=0)


def _allgather_small(v, name):
    m_per, n = v.shape

    def body(x_ref, out_ref, send_sems, recv_sems, local_sem):
        x, y, c, chips = _place()
        me, sibling = (x, y, c), (x, y, 1 - c)

        def rows(px, py, pc):
            return out_ref.at[pl.ds((4 * px + 2 * py + pc) * m_per, m_per), :]

        def copy(k, block, to, src=None):
            return _remote(rows(*block) if src is None else src, rows(*block), send_sems.at[k], recv_sems.at[k], to)

        mine = pltpu.make_async_copy(x_ref, rows(*me), local_sem)
        mine.start()
        first = [copy(0, me, sibling, src=x_ref)]
        first += [copy(1 + j, me, (*chip, c), src=x_ref) for j, chip in enumerate(chips)]
        for cp in first:
            cp.start()
        passed = [copy(4 + j, (*chip, c), sibling) for j, chip in enumerate(chips)]
        for j, chip in enumerate(chips):
            copy(1 + j, (*chip, c), me).wait_recv()
            passed[j].start()
        copy(0, sibling, me).wait_recv()
        for j, chip in enumerate(chips):
            copy(4 + j, (*chip, 1 - c), me).wait_recv()
        for cp in first + passed:
            cp.wait_send()
        mine.wait()

    return pl.pallas_call(
        body, name=name, out_shape=jax.ShapeDtypeStruct((N_DEV * m_per, n), v.dtype),
        in_specs=[pl.BlockSpec(memory_space=pltpu.VMEM)], out_specs=pl.BlockSpec(memory_space=pltpu.VMEM),
        scratch_shapes=[pltpu.SemaphoreType.DMA((7,)), pltpu.SemaphoreType.DMA((7,)), pltpu.SemaphoreType.DMA],
        compiler_params=pltpu.CompilerParams(vmem_limit_bytes=VMEM_LIMIT),
    )(v)


def _row_tile(r, pref=256):
    return _tile(r, pref, 16)


def _cast_bf16(w, layer, chip, name, after=()):
    _, R, C = w.shape
    tr = _row_tile(R)

    def body(s_ref, w_ref, *rest):
        rest[-1][...] = w_ref[...].astype(BF16)

    return pl.pallas_call(
        body, name=name,
        grid_spec=pltpu.PrefetchScalarGridSpec(
            num_scalar_prefetch=1, grid=(R // tr,),
            in_specs=[pl.BlockSpec((None, tr, C), lambda i, s: (layer, i, 0))] + _any_specs(len(after)),
            out_specs=pl.BlockSpec((None, tr, C), lambda i, s: (s[0], i, 0))),
        out_shape=jax.ShapeDtypeStruct((N_CHIPS, R, C), BF16), compiler_params=_params(("parallel",)),
    )(jnp.reshape(chip, (1,)).astype(jnp.int32), w, *after)


def _add_sibling(g, got, c, name):
    G, R, C = g.shape
    rh = R // 2
    tr = _row_tile(rh)
    nb = rh // tr

    def body(c_ref, g_ref, o_ref, p_ref):
        p_ref[...] = (g_ref[...].astype(F32) + o_ref[...].astype(F32)).astype(BF16)

    return pl.pallas_call(
        body, name=name,
        grid_spec=pltpu.PrefetchScalarGridSpec(
            num_scalar_prefetch=1, grid=(G, nb),
            in_specs=[pl.BlockSpec((1, tr, C), lambda s, i, c_ref: (s, c_ref[0] * nb + i, 0)),
                      pl.BlockSpec((1, tr, C), lambda s, i, c_ref: (s, i, 0))],
            out_specs=pl.BlockSpec((1, tr, C), lambda s, i, c_ref: (s, i, 0))),
        out_shape=jax.ShapeDtypeStruct((G, rh, C), BF16), compiler_params=_params(("parallel", "parallel")),
    )(jnp.reshape(c, (1,)).astype(jnp.int32), g, got)


def _sum_chips(part, landed, chip, c, name):
    G, rh, C = part.shape
    tr = _row_tile(rh)
    nb = rh // tr

    def body(p_ref, own_ref, *rest):
        acc = own_ref[...].astype(F32)
        for ref in rest[:G - 1]:
            acc = acc + ref[...].astype(F32)
        rest[G - 1][...] = acc

    slot = lambda k: pl.BlockSpec((None, tr, C), lambda i, p: ((p[0] + k) % G, i, 0))
    return pl.pallas_call(
        body, name=name,
        grid_spec=pltpu.PrefetchScalarGridSpec(
            num_scalar_prefetch=1, grid=(nb,), in_specs=[slot(k) for k in range(G)],
            out_specs=pl.BlockSpec((tr, C), lambda i, p: (p[1] * nb + i, 0))),
        out_shape=jax.ShapeDtypeStruct((2 * rh, C), F32), compiler_params=_params(("parallel",)),
    )(jnp.stack([chip, c]).astype(jnp.int32), part, *([landed] * (G - 1)))


def _adam_math(w, g, m, v):
    m = ADAM_B1 * m + (1.0 - ADAM_B1) * g
    v = ADAM_B2 * v + (1.0 - ADAM_B2) * (g * g)
    m_hat = m / (1.0 - ADAM_B1 ** ADAM_STEP)
    v_hat = v / (1.0 - ADAM_B2 ** ADAM_STEP)
    delta = -ADAM_LR * (m_hat / (jnp.sqrt(v_hat) + ADAM_EPS) + ADAM_WD * w)
    return delta, m, v


def _adamw(w, g, m, v, layer, prev, name, by_cols=False, after=()):
    L, R, C = w.shape
    tr = R if by_cols else _tile(R, 128, 8)
    tc = _tile(C, 256) if by_cols else C
    n_alias = len(prev)
    prev = tuple(prev) + tuple(after)
    n_prev = len(prev)

    def body(w_ref, g_ref, m_ref, v_ref, *rest):
        go_ref, d_ref, mo_ref, vo_ref = rest[n_prev:]
        gv = g_ref[...]
        go_ref[...] = gv
        d_ref[...], mo_ref[...], vo_ref[...] = _adam_math(w_ref[...], gv, m_ref[...], v_ref[...])

    lay = pl.BlockSpec((None, tr, tc), lambda i: (layer, i // (C // tc), i % (C // tc)))
    flat = pl.BlockSpec((tr, tc), lambda i: (i // (C // tc), i % (C // tc)))
    return _call(
        body, name=name, grid=((R // tr) * (C // tc),), in_specs=[lay, flat, lay, lay] + _any_specs(n_prev), out_specs=[lay] * 4,
        out_shape=[jax.ShapeDtypeStruct((L, R, C), F32)] * 4, aliases={4 + k: k for k in range(n_alias)},
        sem=("parallel",), args=(w, g, m, v, *prev))


def _cond_rows(c_row, cw, name, after=()):
    D = c_row.shape[1]
    nr, fc = cw.shape

    def body(c_ref, e_ref, *rest):
        o_ref = rest[-1]
        o_ref[...] = jnp.zeros_like(o_ref)
        cv = c_ref[...]
        o_ref[0:1, 0:D] = cv * (1.0 / (1.0 + jnp.exp(-cv)))
        o_ref[8:8 + nr, 0:fc] = e_ref[...]

    vmem = pl.BlockSpec(memory_space=pltpu.VMEM)
    return pl.pallas_call(body, name=name, in_specs=[vmem, vmem] + _any_specs(len(after)), out_specs=vmem,
                          out_shape=jax.ShapeDtypeStruct((16, max(D, fc)), F32))(c_row, cw, *after)


def _ada_fwd(cact, ada_w, ada_b, layer, chip, name):
    _, D, NC = ada_w.shape
    tn = _tile(NC, 1024)
    nj = NC // tn

    def body(idx_ref, c_ref, w_ref, b_ref, o_ref):
        acc = jnp.dot(c_ref[...].astype(BF16), w_ref[0].astype(BF16), preferred_element_type=F32)
        o_ref[...] = acc + b_ref[pl.ds(idx_ref[0], 1), :]

    return pl.pallas_call(
        body, name=name,
        grid_spec=pltpu.PrefetchScalarGridSpec(
            num_scalar_prefetch=1, grid=(nj,),
            in_specs=[pl.BlockSpec((8, D), lambda j, idx: (0, 0)),
                      pl.BlockSpec((1, D, tn), lambda j, idx: (idx[0], 0, j)),
                      pl.BlockSpec((DEPTH, tn), lambda j, idx: (0, idx[1] * nj + j))],
            out_specs=pl.BlockSpec((8, tn), lambda j, idx: (0, j))),
        out_shape=jax.ShapeDtypeStruct((8, NC), F32), compiler_params=_params(("parallel",)),
    )(jnp.stack([layer, chip]).astype(jnp.int32), cact, ada_w, ada_b)


def _ada_grad_adamw(cact_t, dmod, w, m, v, name):
    L, D, NC = w.shape
    tr = _tile(D, 128, 8)

    def body(c_ref, d_ref, w_ref, m_ref, v_ref, g_ref, dl_ref, mo_ref, vo_ref):
        g = jnp.dot(c_ref[...], d_ref[...], preferred_element_type=F32, precision=HIGHEST)
        g_ref[...] = g
        dl_ref[...], mo_ref[...], vo_ref[...] = _adam_math(w_ref[...], g, m_ref[...], v_ref[...])

    lay = pl.BlockSpec((None, tr, NC), lambda l, i: (l, i, 0))
    return _call(
        body, name=name, grid=(L, D // tr),
        in_specs=[pl.BlockSpec((tr, N_DEV), lambda l, i: (i, 0)), pl.BlockSpec((None, N_DEV, NC), lambda l, i: (l, 0, 0)), lay, lay, lay],
        out_specs=[lay] * 4, out_shape=[jax.ShapeDtypeStruct((L, D, NC), F32)] * 4,
        sem=("parallel", "parallel"), args=(cact_t, dmod, w, m, v))


def _sum_devices(gathered, name):
    n, R, C = gathered.shape

    def body(g_ref, o_ref):
        acc = g_ref[0]
        for j in range(1, n):
            acc = acc + g_ref[j]
        o_ref[...] = acc

    return pl.pallas_call(body, name=name, out_shape=jax.ShapeDtypeStruct((R, C), F32),
                          compiler_params=pltpu.CompilerParams(vmem_limit_bytes=VMEM_LIMIT))(gathered)


def _adamw_small(w, g, m, v, name):
    def body(w_ref, g_ref, m_ref, v_ref, d_ref, mo_ref, vo_ref):
        d_ref[...], mo_ref[...], vo_ref[...] = _adam_math(w_ref[...], g_ref[...], m_ref[...], v_ref[...])

    return pl.pallas_call(body, name=name, out_shape=[jax.ShapeDtypeStruct(w.shape, F32)] * 3)(w, g, m, v)


def _pad_rows(flat, unit=8 * LANES):
    n = flat.shape[0]
    total = -(-n // unit) * unit
    return jnp.pad(flat, (0, total - n)).reshape(total // LANES, LANES)


def _pad_lanes(v2d):
    return jnp.pad(v2d.reshape(1, -1), ((0, 0), (0, LANES - v2d.size)))


FORWARD = {
    "fox_proj": ((), ("ffn_w_up0",)),
    "ffn_up0": (("ffn_w_up0",), ("ffn_w_down0", "swa_w_in")),
    "ffn_down0": (("ffn_w_down0", "swa_w_in"), ("swa_w_o", "ffn_w_up1")),
    "swa_out": (("swa_w_o", "ffn_w_up1"), ("ffn_w_down1",)),
    "ffn_down1": (("ffn_w_down1",), ()),
}
PLAN = {
    "ffn_gate_bwd1": [("swap", "ffn_w_down1")],
    "ffn_up_dx1": [("scatter", "ffn_w_down1", 0, 1, 1)],
    "swa_out_dx": [("swap", "ffn_w_up1")],
    "swa_attn_bwd": [("scatter", "ffn_w_up1", 0, 4, 8), ("swap", "swa_w_o")],
    "swa_proj_dx": [("scatter", "swa_w_o", 0, 1, 1)],
    "ffn_down_dx0": [("scatter", "ffn_w_up1", 4, 6, 8), ("swap", "swa_w_in")],
    "ffn_down_dw0": [("scatter", "swa_w_in", 0, 1, 1)],
    "ffn_gate_bwd0": [("scatter", "ffn_w_up1", 6, 8, 8), ("swap", "ffn_w_down0")],
    "ffn_up_dx0": [("scatter", "ffn_w_down0", 0, 1, 1)],
    "fox_out_dx": [("swap", "ffn_w_up0")],
    "fox_attn_bwd": [("scatter", "ffn_w_up0", 0, 5, 8), ("swap", "fox_w_o")],
    "fox_proj_dw": [("scatter", "fox_w_o", 0, 1, 1), ("scatter", "ffn_w_up0", 5, 6, 8)],
    "fox_proj_dx": [("scatter", "ffn_w_up0", 6, 8, 8), ("swap", "fox_w_in")],
}


class Exchanges:
    def __init__(self, dm, slots, chip, c):
        self.dm, self.slots, self.chip, self.c = dm, dict(slots), chip, c
        self.raw, self.part, self.landed, self.grads, self.views, self.pending = {}, {}, {}, {}, {}, {}

    def gather_start(self, keys, name, after):
        self.first = (keys, _gather_comm([self.slots[k] for k in keys]))
        self.first_state, token = _split_start(self.first[1], name + "_start", after)
        return token

    def gather_finish(self, after, name):
        keys, comm = self.first
        _split_wait(comm, self.first_state, after, name + "_wait")
        pass_on = _forward_comm(comm.results)
        _run_comm(pass_on, name + "_pass")
        self.slots.update(zip(keys, pass_on.results))
        self.fence = list(pass_on.results)

    def before(self, stage, last):
        need, nxt = FORWARD[stage]
        if need:
            self.gather_finish(list(last), "gather_" + "_".join(need))
        if nxt:
            return self.gather_start(list(nxt), "gather_" + "_".join(nxt), list(last) + self.fence)
        return None

    def w(self, key):
        if key not in self.views:
            S, D, FH, QH, KH, Fh = self.dm
            full = self.slots[key]
            if key == "fox_w_in":
                cols = full.shape[2]
                full = jnp.pad(full.transpose(1, 0, 2).reshape(D, N_CHIPS * cols), ((0, 0), (0, 3 * D + LANES - N_CHIPS * cols)))[None]
            elif key in ("fox_w_o", "swa_w_o"):
                full = full.reshape(1, D, D)
            elif key.startswith("ffn_w_down"):
                full = full.reshape(1, Fh, D)
            self.views[key] = full
        return self.views[key]

    def carry(self, stage, last=()):
        todo = []
        token = self.before(stage, last) if stage in FORWARD else None
        if token is not None:
            todo.append(("order", [], Comm([token], [], {}, 1, lambda *refs: None, lambda *refs: None)))
        for kind, key, *chunk in PLAN.get(stage, ()):
            if kind == "swap":
                todo.append((kind, [key], _swap_comm([self.raw[key]])))
            elif kind == "scatter":
                todo.append((kind, [(key, *chunk)], _scatter_comm([self.part[key]], [self.landed.get(key)], [tuple(chunk)])))
        self.pending[stage] = todo
        return _merge([cm for _, _, cm in todo])

    def carried(self, stage, comm):
        for kind, keys, cm in self.pending.pop(stage):
            if kind == "swap":
                self.part[keys[0]] = _add_sibling(self.raw[keys[0]], cm.results[0], self.c, f"add_sibling_{keys[0]}")
            elif kind == "scatter":
                self.landed[keys[0][0]] = cm.results[0]

    def grad(self, key, g):
        S, D, FH, QH, KH, Fh = self.dm
        if key == "fox_w_in":
            cols = self.slots[key].shape[2]
            g = g[0][:, :N_CHIPS * cols].reshape(D, N_CHIPS, cols).transpose(1, 0, 2)
        elif key in ("fox_w_o", "swa_w_o"):
            g = g.reshape(N_CHIPS, D // N_CHIPS, D)
        elif key.startswith("ffn_w_down"):
            g = g.reshape(N_CHIPS, Fh // N_CHIPS, D)
        self.raw[key] = g

    def last_start(self, last, after):
        part = self.part[last]
        self.last = (last, _scatter_comm([part], [lax.empty(part.shape, part.dtype)], [(0, 1, 1)]))
        self.last_state, token = _split_start(self.last[1], "grads_last_start", after)
        return token

    def join_landed(self):
        keys = list(self.landed)
        join = _join_comm([_sum_chips(self.part[k], self.landed[k], self.chip, self.c, f"sum_chips_{k}") for k in keys])
        _run_comm(join, "grads_join")
        return dict(zip(keys, join.results))

    def last_finish(self, after):
        last, comm = self.last
        _split_wait(comm, self.last_state, after, "grads_last_wait")
        join = _join_comm([_sum_chips(self.part[last], comm.results[0], self.chip, self.c, f"sum_chips_{last}")])
        _run_comm(join, "grads_join_last")
        return join.results[0]


def _step(dm, a):
    S, D, FH, QH, KH, Fh = dm
    ix, iy, ic = lax.axis_index("x"), lax.axis_index("y"), lax.axis_index("c")
    chip = 2 * ix + iy
    dev = 2 * chip + ic
    F2c = a["ffn_w_up"].shape[2]
    NC = a["ada_w"].shape[2]

    names = ["fox_w_in", "fox_w_o", "swa_w_in", "swa_w_o", "ffn_w_up", "ffn_w_up", "ffn_w_down", "ffn_w_down"]
    layers = [0, 0, 0, 0, 0, 1, 0, 1]
    keys = ["fox_w_in", "fox_w_o", "swa_w_in", "swa_w_o", "ffn_w_up0", "ffn_w_up1", "ffn_w_down0", "ffn_w_down1"]
    cast = lambda t, after: _cast_bf16(a[names[t]], layers[t], chip, f"cast_{keys[t]}", after)
    pp = Exchanges(dm, {keys[t]: cast(t, ()) for t in (0, 1)}, chip, ic)

    e0 = _cond_rows(a["c"], a["ffn_conv_w"].reshape(DEPTH * 3, F2c), "silu_c", [pp.slots[k] for k in keys[:2]])
    g0 = _allgather_small(e0, "gather_cond").reshape(N_DEV, 16, e0.shape[1])
    cact = g0[:, 0, :D]
    conv_w = g0[0::2, 8:8 + DEPTH * 3, :F2c].transpose(1, 0, 2).reshape(DEPTH, 3, N_CHIPS * F2c)
    rows = _ada_fwd(cact, a["ada_w"], a["ada_b"], ic, chip, "ada_proj")
    g1 = _allgather_small(rows, "gather_mod").reshape(N_CHIPS, DEPTH, 8, NC)
    mod = lax.dynamic_index_in_dim(g1, dev, axis=2, keepdims=False).transpose(1, 0, 2).reshape(DEPTH, N_CHIPS * NC)

    token = pp.gather_start(keys[:2], "gather_fox", [mod])
    pp.slots.update({keys[t]: cast(t, (token,)) for t in range(2, len(keys))})
    h0 = _modulate(a["x"][0], mod[0:1, D:2 * D], mod[0:1, 0:D], "mod_in")
    pp.gather_finish([pp.slots[k] for k in keys[2:]] + [h0], "gather_fox")
    sp = {"fox_b_f": _pad_lanes(a["fox_b_f"]), "sinks": jnp.repeat(a["swa_sinks"].reshape(KH, QH // KH), WIN, axis=1)[:, :, None],
          "conv_w": [conv_w[i] for i in range(DEPTH)], "conv_b": [a["ffn_conv_b"][i:i + 1] for i in range(DEPTH)]}
    for nm in ("ln_mix_g", "ln_mix_b", "ln_ffn_g", "ln_ffn_b"):
        sp[nm] = [a[nm][i:i + 1] for i in range(DEPTH)]

    loss_cols, grad_x, gs, dmod = _local_step(dm, a["x"][0], a["loss_target"][0], a["positions"][0], mod, sp, pp, h0)
    loss = lax.psum(0.5 / D * jnp.sum(loss_cols), ("x", "y", "c"))
    out = {"loss": loss, "grad_x": grad_x[None]}

    pieces = [dmod.reshape(-1), gs["fox_b_f"].reshape(-1), _pad_lanes(gs["sinks"]).reshape(-1),
              jnp.stack(gs["conv_w"]).reshape(-1), jnp.stack(gs["conv_b"]).reshape(-1)]
    pieces += [jnp.stack(gs[nm]).reshape(-1) for nm in ("ln_mix_g", "ln_mix_b", "ln_ffn_g", "ln_ffn_b")]
    sizes = [p.shape[0] for p in pieces]
    packed = _pad_rows(jnp.concatenate(pieces))
    allp = _allgather_small(packed, "gather_small").reshape(N_DEV, packed.shape[0], LANES)
    tot = _sum_devices(allp, "sum_small").reshape(-1)
    offs = [sum(sizes[:k]) for k in range(len(sizes))]
    take = lambda k: tot[offs[k]:offs[k] + sizes[k]]
    g_small = {"ada_b": take(0).reshape(DEPTH, -1), "fox_b_f": take(1)[:FH].reshape(1, FH), "swa_sinks": take(2)[:QH].reshape(1, QH),
               "ffn_conv_w": lax.dynamic_slice_in_dim(take(3).reshape(DEPTH, 3, N_CHIPS * F2c), chip * F2c, F2c, axis=2),
               "ffn_conv_b": take(4).reshape(DEPTH, -1)}
    for k, nm in enumerate(("ln_mix_g", "ln_mix_b", "ln_ffn_g", "ln_ffn_b")):
        g_small[nm] = take(5 + k).reshape(DEPTH, D)
    small = list(g_small)
    pack = lambda pre: _pad_rows(jnp.concatenate([(a[pre + nm] if pre else a[nm]).reshape(-1) for nm in small]))
    gp = _pad_rows(jnp.concatenate([g_small[nm].reshape(-1) for nm in small]))
    ds_, ms_, vs_ = _adamw_small(pack(""), gp, pack("m_"), pack("v_"), "adamw_small")
    off = 0
    for nm in small:
        n_el = a[nm].size
        out["grad_" + nm] = g_small[nm]
        for pre, arr in (("delta_", ds_), ("new_m_", ms_), ("new_v_", vs_)):
            out[pre + nm] = arr.reshape(-1)[off:off + n_el].reshape(a[nm].shape)
        off += n_el

    dmod_all = allp.reshape(N_DEV, -1)[:, :DEPTH * N_CHIPS * NC].reshape(N_DEV, DEPTH, N_CHIPS * NC)
    dmod_mine = lax.dynamic_slice_in_dim(dmod_all, chip * NC, NC, axis=2).transpose(1, 0, 2)

    grads = pp.join_landed()
    token = pp.last_start("fox_w_in", [ds_, dmod_mine] + list(grads.values()))
    ada = _ada_grad_adamw(cact.T, dmod_mine + token[0, 0], a["ada_w"], a["m_ada_w"], a["v_ada_w"], "ada_grad")
    for pre, arr in zip(("grad_", "delta_", "new_m_", "new_v_"), ada):
        out[pre + "ada_w"] = arr
    upd = {}
    for k, nm, l in zip(keys[1:], names[1:], layers[1:]):
        upd[nm] = _adamw(a[nm], grads[k], a["m_" + nm], a["v_" + nm], l, upd.get(nm, ()), f"adamw_{k}", after=(token,))
    g_last = pp.last_finish([ada[1]] + [res[1] for res in upd.values()])
    tview = lambda t: jnp.swapaxes(t, 1, 2)
    res = _adamw(tview(a["fox_w_in"]), g_last.T, tview(a["m_fox_w_in"]), tview(a["v_fox_w_in"]), 0, (), "adamw_fox_w_in", by_cols=True)
    upd["fox_w_in"] = [tview(r) for r in res]
    for nm, res in upd.items():
        for pre, arr in zip(("grad_", "delta_", "new_m_", "new_v_"), res):
            out[pre + nm] = arr
    return out


_WEIGHTS = ["fox_w_in", "fox_b_f", "fox_w_o", "swa_w_in", "swa_sinks", "swa_w_o", "ada_w", "ada_b", "ffn_w_up", "ffn_conv_w",
            "ffn_conv_b", "ffn_w_down", "ln_mix_g", "ln_mix_b", "ln_ffn_g", "ln_ffn_b"]
_INPUTS = (["x", "c", "positions"] + _WEIGHTS + ["loss_target"] + ["m_" + w for w in _WEIGHTS] + ["v_" + w for w in _WEIGHTS])


def kernel(x, c, positions, fox_w_in, fox_b_f, fox_w_o, swa_w_in, swa_sinks, swa_w_o, ada_w, ada_b, ffn_w_up, ffn_conv_w, ffn_conv_b, ffn_w_down, ln_mix_g, ln_mix_b, ln_ffn_g, ln_ffn_b, loss_target, m_fox_w_in, m_fox_b_f, m_fox_w_o, m_swa_w_in, m_swa_sinks, m_swa_w_o, m_ada_w, m_ada_b, m_ffn_w_up, m_ffn_conv_w, m_ffn_conv_b, m_ffn_w_down, m_ln_mix_g, m_ln_mix_b, m_ln_ffn_g, m_ln_ffn_b, v_fox_w_in, v_fox_b_f, v_fox_w_o, v_swa_w_in, v_swa_sinks, v_swa_w_o, v_ada_w, v_ada_b, v_ffn_w_up, v_ffn_conv_w, v_ffn_conv_b, v_ffn_w_down, v_ln_mix_g, v_ln_mix_b, v_ln_ffn_g, v_ln_ffn_b):
    args = (x, c, positions, fox_w_in, fox_b_f, fox_w_o, swa_w_in, swa_sinks, swa_w_o, ada_w, ada_b, ffn_w_up, ffn_conv_w, ffn_conv_b, ffn_w_down, ln_mix_g, ln_mix_b, ln_ffn_g, ln_ffn_b, loss_target, m_fox_w_in, m_fox_b_f, m_fox_w_o, m_swa_w_in, m_swa_sinks, m_swa_w_o, m_ada_w, m_ada_b, m_ffn_w_up, m_ffn_conv_w, m_ffn_conv_b, m_ffn_w_down, m_ln_mix_g, m_ln_mix_b, m_ln_ffn_g, m_ln_ffn_b, v_fox_w_in, v_fox_b_f, v_fox_w_o, v_swa_w_in, v_swa_sinks, v_swa_w_o, v_ada_w, v_ada_b, v_ffn_w_up, v_ffn_conv_w, v_ffn_conv_b, v_ffn_w_down, v_ln_mix_g, v_ln_mix_b, v_ln_ffn_g, v_ln_ffn_b)
    out = _step(PROD, dict(zip(_INPUTS, args)))
    order = ["loss", "grad_x"] + [p + w for p in ("grad_", "delta_", "new_m_", "new_v_") for w in _WEIGHTS]
    return tuple(out[k] for k in order)
```

```python
import functools
from typing import NamedTuple

import jax
import jax.numpy as jnp
from jax import lax
from jax.experimental import pallas as pl
from jax.experimental.pallas import tpu as pltpu

F32 = jnp.float32
BF16 = jnp.bfloat16
MESH = pl.DeviceIdType.MESH
HIGHEST = lax.Precision.HIGHEST

N_CHIPS = 4
N_DEV = 8
LANES = 128
VMEM_LIMIT = 56 * 1024 * 1024

DEPTH = 2
DEEPNORM_ALPHA = (2.0 * DEPTH) ** 0.25
LN_EPS = 1e-5
ROPE_THETA = 500000.0
ADAM_LR, ADAM_B1, ADAM_B2, ADAM_EPS, ADAM_WD, ADAM_STEP = 0.001, 0.9, 0.999, 1e-08, 0.01, 10
NEG = -1e30


class Dims(NamedTuple):
    S: int
    D: int
    FH: int
    QH: int
    KH: int
    F: int


PROD = Dims(S=2048, D=2048, FH=16, QH=32, KH=4, F=5632)
FDH = 128
SDH = 64
WIN = 128
ROPE_DIM = 16
FOX_TQ = 256


def _params(sem=None, vmem=VMEM_LIMIT):
    return pltpu.CompilerParams(dimension_semantics=sem, vmem_limit_bytes=vmem)


def _tile(n, pref, unit=LANES):
    if n <= pref:
        return n
    t = (pref // unit) * unit
    while t > 0:
        if n % t == 0:
            return t
        t -= unit
    return n


class Comm:
    def __init__(self, args, out_shapes, aliases, n_sem, start, finish, members=()):
        self.args, self.out_shapes, self.aliases, self.n_sem = list(args), list(out_shapes), dict(aliases), n_sem
        self.start, self.finish = start, finish
        self.members = members
        self.results = None

    def set_results(self, res):
        self.results = list(res)
        for cm, o0 in self.members:
            cm.set_results(self.results[o0:o0 + len(cm.out_shapes)])


class _SemView:
    def __init__(self, sems, first):
        self.sems, self.first = sems, first

    @property
    def at(self):
        return self

    def __getitem__(self, k):
        return self.sems.at[self.first + k]


def _merge(comms):
    comms = [cm for cm in comms if cm is not None]
    if len(comms) < 2:
        return comms[0] if comms else None
    args, shapes, aliases, spans, n_sem = [], [], {}, [], 0
    for cm in comms:
        spans.append((len(args), len(shapes), n_sem))
        aliases.update({len(args) + a: len(shapes) + o for a, o in cm.aliases.items()})
        args += cm.args
        shapes += cm.out_shapes
        n_sem += cm.n_sem

    def each(step):
        def run(ar, ou, send, recv):
            for cm, (a0, o0, s0) in zip(comms, spans):
                getattr(cm, step)(ar[a0:a0 + len(cm.args)], ou[o0:o0 + len(cm.out_shapes)], _SemView(send, s0), _SemView(recv, s0))
        return run

    return Comm(args, shapes, aliases, n_sem, each("start"), each("finish"), [(cm, o0) for cm, (_, o0, _) in zip(comms, spans)])


def _place():
    x, y, c = lax.axis_index("x"), lax.axis_index("y"), lax.axis_index("c")
    chips = [(1 - x, y), (x, 1 - y), (1 - x, 1 - y)]
    return x, y, c, chips


def _remote(src, dst, send, recv, to):
    return pltpu.make_async_remote_copy(src_ref=src, dst_ref=dst, send_sem=send, recv_sem=recv, device_id=to, device_id_type=MESH)


def _any_specs(n):
    return [pl.BlockSpec(memory_space=pl.ANY)] * n


def _call(body, *, name, grid, in_specs, out_specs, out_shape, args, sem, scratch_shapes=(), aliases=None, comm=None):
    in_specs, out_specs, out_shape, scratch_shapes = list(in_specs), list(out_specs), list(out_shape), list(scratch_shapes)
    aliases = dict(aliases or {})
    if comm is None:
        return pl.pallas_call(body, name=name, grid=grid, in_specs=in_specs, out_specs=out_specs, out_shape=out_shape,
                              scratch_shapes=scratch_shapes, input_output_aliases=aliases, compiler_params=_params(sem))(*args)
    n_in, n_out, nc_in, nc_out, n_scr = len(in_specs), len(out_specs), len(comm.args), len(comm.out_shapes), len(scratch_shapes)

    def wrapped(*refs):
        ins, refs = refs[:n_in], refs[n_in:]
        cin, refs = refs[:nc_in], refs[nc_in:]
        outs, refs = refs[:n_out], refs[n_out:]
        cout, refs = refs[:nc_out], refs[nc_out:]
        scratch, (send, recv) = refs[:n_scr], refs[n_scr:]
        ids = [pl.program_id(k) for k in range(len(grid))]
        first = functools.reduce(jnp.logical_and, [i == 0 for i in ids])
        last = functools.reduce(jnp.logical_and, [i == g - 1 for i, g in zip(ids, grid)])

        @pl.when(first)
        def _():
            comm.start(cin, cout, send, recv)

        body(*ins, *outs, *scratch)

        @pl.when(last)
        def _():
            comm.finish(cin, cout, send, recv)

    res = pl.pallas_call(
        wrapped, name=name, grid=grid, in_specs=in_specs + _any_specs(nc_in), out_specs=out_specs + _any_specs(nc_out),
        out_shape=out_shape + comm.out_shapes,
        scratch_shapes=scratch_shapes + [pltpu.SemaphoreType.DMA((comm.n_sem,)), pltpu.SemaphoreType.DMA((comm.n_sem,))],
        input_output_aliases={**aliases, **{n_in + a: n_out + o for a, o in comm.aliases.items()}},
        compiler_params=_params(("arbitrary",) * len(grid)),
    )(*args, *comm.args)
    comm.set_results(res[n_out:])
    return list(res[:n_out])


def _run_comm(comm, name):
    nc_in, nc_out = len(comm.args), len(comm.out_shapes)

    def body(*refs):
        cin, cout, (send, recv) = refs[:nc_in], refs[nc_in:nc_in + nc_out], refs[nc_in + nc_out:]
        comm.start(cin, cout, send, recv)
        comm.finish(cin, cout, send, recv)

    res = pl.pallas_call(
        body, name=name, in_specs=_any_specs(nc_in), out_specs=_any_specs(nc_out), out_shape=comm.out_shapes,
        scratch_shapes=[pltpu.SemaphoreType.DMA((comm.n_sem,)), pltpu.SemaphoreType.DMA((comm.n_sem,))],
        input_output_aliases=comm.aliases,
    )(*comm.args)
    comm.set_results(res)


_HBM = pl.BlockSpec(memory_space=pltpu.HBM)
_SEM = pl.BlockSpec(memory_space=pltpu.SEMAPHORE)
_EFFECT = pltpu.SideEffectType.DATAFLOW_SIDE_EFFECTING


def _split_start(comm, name, after=()):
    n = len(comm.args)
    back = {o: a for a, o in comm.aliases.items()}
    assert len(back) == len(comm.out_shapes)

    n_after = len(after)

    def body(*refs):
        refs = refs[n + n_after:]
        send, recv, thru, token = refs[0], refs[1], refs[2:n + 2], refs[n + 2]
        comm.start(thru, [thru[back[o]] for o in range(len(back))], send, recv)
        token[...] = jnp.zeros_like(token)

    res = pl.pallas_call(
        body, name=name,
        out_shape=(pltpu.SemaphoreType.DMA((comm.n_sem,)), pltpu.SemaphoreType.DMA((comm.n_sem,)),
                   *[pltpu.HBM(a.shape, a.dtype) for a in comm.args], jax.ShapeDtypeStruct((8, LANES), F32)),
        in_specs=[_HBM] * n + _any_specs(n_after), out_specs=(_SEM, _SEM, *[_HBM] * n, pl.BlockSpec(memory_space=pltpu.VMEM)),
        input_output_aliases={i: 2 + i for i in range(n)},
        compiler_params=pltpu.CompilerParams(has_side_effects=_EFFECT),
    )(*[pltpu.with_memory_space_constraint(a, pltpu.HBM) for a in comm.args], *after)
    return (res[0], res[1], list(res[2:2 + n])), res[2 + n]


def _split_wait(comm, state, after, name):
    send, recv, thru = state
    n, n_after = len(thru), len(after)
    back = {o: a for a, o in comm.aliases.items()}

    def body(*refs):
        ins, send_ref, recv_ref = refs[:n], refs[n], refs[n + 1]
        comm.finish(ins, [ins[back[o]] for o in range(len(back))], send_ref, recv_ref)

    res = pl.pallas_call(
        body, name=name, out_shape=tuple(pltpu.HBM(a.shape, a.dtype) for a in thru),
        in_specs=[_HBM] * n + [_SEM, _SEM] + _any_specs(n_after), out_specs=[_HBM] * n,
        input_output_aliases={i: i for i in range(n)}, compiler_params=pltpu.CompilerParams(has_side_effects=_EFFECT),
    )(*thru, send, recv, *after)
    comm.set_results([res[back[o]] for o in range(len(back))])


def _forward_comm(slots):
    n = len(slots)

    def rows(t, who):
        rh = slots[t].shape[1] // 2
        return pl.ds(who * rh, rh)

    def copy(outs, send, recv, t, j, chip, who):
        x, y, c, _ = _place()
        blk = outs[t].at[2 * chip[0] + chip[1], rows(t, who)]
        return _remote(blk, blk, send.at[3 * t + j], recv.at[3 * t + j], (x, y, 1 - c))

    def start(args, outs, send, recv):
        _, _, c, chips = _place()
        for t in range(n):
            for j, chip in enumerate(chips):
                copy(outs, send, recv, t, j, chip, c).start()

    def finish(args, outs, send, recv):
        _, _, c, chips = _place()
        for t in range(n):
            for j, chip in enumerate(chips):
                copy(outs, send, recv, t, j, chip, 1 - c).wait_recv()
        for t in range(n):
            for j, chip in enumerate(chips):
                copy(outs, send, recv, t, j, chip, c).wait_send()

    shapes = [jax.ShapeDtypeStruct(w.shape, w.dtype) for w in slots]
    return Comm(slots, shapes, {t: t for t in range(n)}, 3 * n, start, finish)


def _gather_comm(slots):
    n = len(slots)

    def rows(t, who):
        rh = slots[t].shape[1] // 2
        return pl.ds(who * rh, rh)

    def start(args, outs, send, recv):
        x, y, c, chips = _place()
        for t in range(n):
            mine = outs[t].at[2 * x + y, rows(t, c)]
            for j, chip in enumerate(chips):
                _remote(mine, mine, send.at[3 * t + j], recv.at[3 * t + j], (*chip, c)).start()

    def finish(args, outs, send, recv):
        x, y, c, chips = _place()
        for t in range(n):
            for j, chip in enumerate(chips):
                blk = outs[t].at[2 * chip[0] + chip[1], rows(t, c)]
                _remote(blk, blk, send.at[3 * t + j], recv.at[3 * t + j], (*chip, c)).wait_recv()
        for t in range(n):
            mine = outs[t].at[2 * x + y, rows(t, c)]
            for j, chip in enumerate(chips):
                _remote(mine, mine, send.at[3 * t + j], recv.at[3 * t + j], (*chip, c)).wait_send()

    shapes = [jax.ShapeDtypeStruct(w.shape, w.dtype) for w in slots]
    return Comm(slots, shapes, {t: t for t in range(n)}, 3 * n, start, finish)


def _scatter_comm(parts, landed, chunks):
    n = len(parts)
    prev = [t for t in range(n) if landed[t] is not None]

    def rows(t):
        lo, hi, nch = chunks[t]
        rc = parts[t].shape[1] // nch
        return pl.ds(lo * rc, (hi - lo) * rc)

    def start(args, outs, send, recv):
        x, y, c, chips = _place()
        s = 2 * x + y
        for t in range(n):
            for j, chip in enumerate(chips):
                _remote(args[t].at[2 * chip[0] + chip[1], rows(t)], outs[t].at[s, rows(t)],
                        send.at[3 * t + j], recv.at[3 * t + j], (*chip, c)).start()

    def finish(args, outs, send, recv):
        x, y, c, chips = _place()
        for t in range(n):
            for j, chip in enumerate(chips):
                blk = outs[t].at[2 * chip[0] + chip[1], rows(t)]
                _remote(blk, blk, send.at[3 * t + j], recv.at[3 * t + j], (*chip, c)).wait_recv()
        for t in range(n):
            for j, chip in enumerate(chips):
                src = args[t].at[2 * chip[0] + chip[1], rows(t)]
                _remote(src, src, send.at[3 * t + j], recv.at[3 * t + j], (*chip, c)).wait_send()

    shapes = [jax.ShapeDtypeStruct(p.shape, p.dtype) for p in parts]
    return Comm(list(parts) + [landed[t] for t in prev], shapes, {n + i: t for i, t in enumerate(prev)}, 3 * n, start, finish)


def _swap_comm(gs):
    n = len(gs)

    def copy(args, outs, send, recv, t):
        _, _, c, _ = _place()
        rh = gs[t].shape[1] // 2
        x, y = lax.axis_index("x"), lax.axis_index("y")
        return _remote(args[t].at[:, pl.ds((1 - c) * rh, rh), :], outs[t], send.at[t], recv.at[t], (x, y, 1 - c))

    def start(args, outs, send, recv):
        for t in range(n):
            copy(args, outs, send, recv, t).start()

    def finish(args, outs, send, recv):
        for t in range(n):
            copy(args, outs, send, recv, t).wait()

    shapes = [jax.ShapeDtypeStruct((g.shape[0], g.shape[1] // 2, g.shape[2]), g.dtype) for g in gs]
    return Comm(gs, shapes, {}, n, start, finish)


def _join_comm(gs):
    n = len(gs)

    def half(outs, t, who):
        rh = gs[t].shape[0] // 2
        return outs[t].at[pl.ds(who * rh, rh), :]

    def start(args, outs, send, recv):
        x, y, c, _ = _place()
        for t in range(n):
            _remote(half(outs, t, c), half(outs, t, c), send.at[t], recv.at[t], (x, y, 1 - c)).start()

    def finish(args, outs, send, recv):
        x, y, c, _ = _place()
        for t in range(n):
            _remote(half(outs, t, 1 - c), half(outs, t, 1 - c), send.at[t], recv.at[t], (x, y, 1 - c)).wait_recv()
        for t in range(n):
            _remote(half(outs, t, c), half(outs, t, c), send.at[t], recv.at[t], (x, y, 1 - c)).wait_send()

    shapes = [jax.ShapeDtypeStruct(g.shape, g.dtype) for g in gs]
    return Comm(gs, shapes, {t: t for t in range(n)}, n, start, finish)


_DN = {"nn": (((1,), (0,)), ((), ())), "nt": (((1,), (1,)), ((), ())), "tn": (((0,), (0,)), ((), ()))}


def _mm(a, b, *, mode, out_dtype, name, out_groups=1, tm=1024, tn=1024, tk=2048, comm=None):
    ga, ra, ca = a.shape
    gb, rb, cb = b.shape
    if mode == "nn":
        M, K, N = ra, ga * ca, gb * cb
        assert rb == K and ga == 1 or (rb == K)
    elif mode == "nt":
        M, K, N = ra, ga * ca, rb
        assert gb * cb == K
    else:
        K, M, N = ra, ga * ca, gb * cb
        assert rb == K
    go = out_groups
    if mode == "nn":
        tk = _tile(ca, tk); assert rb % tk == 0 and (ga == 1 or True)
        tn = _tile(min(cb, N // go), tn); tm = _tile(M, tm, 8)
    elif mode == "nt":
        tk = _tile(ca, tk); tk = _tile(cb, tk) if cb % tk else tk; assert ca % tk == 0 and cb % tk == 0
        tn = _tile(N // go, tn); tm = _tile(M, tm, 8)
    else:
        tk = _tile(K, tk, 8); tm = _tile(ca, tm); tn = _tile(min(cb, N // go), tn)
    assert (N // go) % tn == 0 and M % tm == 0 and K % tk == 0, (name, M, N, K, tm, tn, tk)
    nk = K // tk
    kpa = max(ca // tk, 1)
    kpb = max(cb // tk, 1)
    npb = max(cb // tn, 1)
    npo = (N // go) // tn
    mpa = max(ca // tm, 1)

    if mode == "nn":
        a_spec = pl.BlockSpec((1, tm, tk), lambda j, i, k: (k // kpa, i, k % kpa))
        b_spec = pl.BlockSpec((1, tk, tn), lambda j, i, k: (j // npb, k, j % npb))
    elif mode == "nt":
        a_spec = pl.BlockSpec((1, tm, tk), lambda j, i, k: (k // kpa, i, k % kpa))
        b_spec = pl.BlockSpec((1, tn, tk), lambda j, i, k: (k // kpb, j, k % kpb))
    else:
        a_spec = pl.BlockSpec((1, tk, tm), lambda j, i, k: (i // mpa, k, i % mpa))
        b_spec = pl.BlockSpec((1, tk, tn), lambda j, i, k: (j // npb, k, j % npb))
    o_spec = pl.BlockSpec((1, tm, tn), lambda j, i, k: (j // npo, i, j % npo))
    dn = _DN[mode]

    def body(a_ref, b_ref, o_ref, *acc):
        p = lax.dot_general(a_ref[0], b_ref[0], dn, preferred_element_type=F32)
        if nk == 1:
            o_ref[0] = p.astype(out_dtype)
        else:
            k = pl.program_id(2)

            @pl.when(k == 0)
            def _():
                acc[0][...] = p

            @pl.when(k > 0)
            def _():
                acc[0][...] += p

            @pl.when(k == nk - 1)
            def _():
                o_ref[0] = acc[0][...].astype(out_dtype)

    return _call(
        body, name=name, grid=(N // tn, M // tm, nk), in_specs=[a_spec, b_spec], out_specs=[o_spec],
        out_shape=[jax.ShapeDtypeStruct((go, M, N // go), out_dtype)],
        scratch_shapes=[pltpu.VMEM((tm, tn), F32)] if nk > 1 else [],
        sem=("parallel", "parallel", "arbitrary"), args=(a, b), comm=comm)[0]


def _rows(tr, d):
    return pl.BlockSpec((tr, d), lambda i: (i, 0))


def _vec(d):
    return pl.BlockSpec((1, d), lambda i: (0, 0))


def _modulate(x, sc, sh, name):
    S, D = x.shape
    tr = min(256, S)

    def body(x_ref, sc_ref, sh_ref, h_ref):
        h_ref[...] = (x_ref[...] * (1.0 + sc_ref[...]) + sh_ref[...]).astype(BF16)

    return pl.pallas_call(
        body, name=name, grid=(S // tr,), in_specs=[_rows(tr, D), _vec(D), _vec(D)], out_specs=_rows(tr, D),
        out_shape=jax.ShapeDtypeStruct((S, D), BF16), compiler_params=_params(("parallel",)),
    )(x, sc, sh)


def _ln_fwd(x, y, gate, gamma, beta, sc, sh, name, comm=None):
    S, D = x.shape
    tr = min(256, S)
    emit_h = sc is not None

    def body(*refs):
        if emit_h:
            x_ref, y_ref, g_ref, ga_ref, be_ref, sc_ref, sh_ref, xo_ref, xh_ref, rs_ref, h_ref = refs
        else:
            x_ref, y_ref, g_ref, ga_ref, be_ref, xo_ref, xh_ref, rs_ref = refs
        z = DEEPNORM_ALPHA * x_ref[...] + (1.0 + g_ref[...]) * y_ref[...]
        mu = jnp.mean(z, axis=-1, keepdims=True)
        zc = z - mu
        var = jnp.mean(zc * zc, axis=-1, keepdims=True)
        rstd = lax.rsqrt(var + LN_EPS)
        xh = zc * rstd
        xo = xh * ga_ref[...] + be_ref[...]
        xo_ref[...] = xo
        xh_ref[...] = xh
        rs_ref[...] = rstd
        if emit_h:
            h_ref[...] = (xo * (1.0 + sc_ref[...]) + sh_ref[...]).astype(BF16)

    ins = [x, y, gate, gamma, beta] + ([sc, sh] if emit_h else [])
    in_specs = [_rows(tr, D), _rows(tr, D)] + [_vec(D)] * (len(ins) - 2)
    out_shape = [jax.ShapeDtypeStruct((S, D), F32), jax.ShapeDtypeStruct((S, D), F32), jax.ShapeDtypeStruct((S, 1), F32)]
    out_specs = [_rows(tr, D), _rows(tr, D), _rows(tr, 1)]
    if emit_h:
        out_shape.append(jax.ShapeDtypeStruct((S, D), BF16))
        out_specs.append(_rows(tr, D))
    return _call(body, name=name, grid=(S // tr,), in_specs=in_specs, out_specs=out_specs, out_shape=out_shape,
                 sem=("parallel",), args=ins, comm=comm)


def _loss_head(xf, tgt, name):
    S, D = xf.shape
    tr = min(256, S)

    def body(x_ref, t_ref, dx_ref, l_ref):
        e = x_ref[...] - t_ref[...]
        dx_ref[...] = e * (1.0 / D)

        @pl.when(pl.program_id(0) == 0)
        def _():
            l_ref[...] = jnp.zeros_like(l_ref)

        l_ref[...] += jnp.sum(e * e, axis=0, keepdims=True)

    return pl.pallas_call(
        body, name=name, grid=(S // tr,), in_specs=[_rows(tr, D), _rows(tr, D)],
        out_specs=[_rows(tr, D), _vec(D)],
        out_shape=[jax.ShapeDtypeStruct((S, D), F32), jax.ShapeDtypeStruct((1, D), F32)],
        compiler_params=_params(("arbitrary",)),
    )(xf, tgt)


def _ln_bwd(dxo, xh, rstd, gamma, y, gate, name, pre=None):
    S, D = dxo.shape
    tr = min(256, S)
    n_pre = 0 if pre is None else 3

    def body(dx_ref, xh_ref, rs_ref, ga_ref, y_ref, g_ref, *rest):
        dres_ref, dy_ref, dga_ref, dbe_ref, dg_ref = rest[n_pre:n_pre + 5]
        first = pl.program_id(0) == 0
        dxo_ = dx_ref[...]
        xh_ = xh_ref[...]
        if pre is not None:
            dh_ref, sc_ref, be_ref = rest[:3]
            dsc_ref, dsh_ref = rest[n_pre + 5:]
            dh_ = dh_ref[...]
            dxo_ = dxo_ + dh_ * (1.0 + sc_ref[...])

            @pl.when(first)
            def _():
                dsc_ref[...] = jnp.zeros_like(dsc_ref)
                dsh_ref[...] = jnp.zeros_like(dsh_ref)

            dsc_ref[...] += jnp.sum(dh_ * (xh_ * ga_ref[...] + be_ref[...]), axis=0, keepdims=True)
            dsh_ref[...] += jnp.sum(dh_, axis=0, keepdims=True)
        dxh = dxo_ * ga_ref[...]
        m1 = jnp.mean(dxh, axis=-1, keepdims=True)
        m2 = jnp.mean(dxh * xh_, axis=-1, keepdims=True)
        dz = rs_ref[...] * (dxh - m1 - xh_ * m2)
        dres_ref[...] = DEEPNORM_ALPHA * dz
        dy_ref[...] = ((1.0 + g_ref[...]) * dz).astype(BF16)

        @pl.when(first)
        def _():
            dga_ref[...] = jnp.zeros_like(dga_ref)
            dbe_ref[...] = jnp.zeros_like(dbe_ref)
            dg_ref[...] = jnp.zeros_like(dg_ref)

        dga_ref[...] += jnp.sum(dxo_ * xh_, axis=0, keepdims=True)
        dbe_ref[...] += jnp.sum(dxo_, axis=0, keepdims=True)
        dg_ref[...] += jnp.sum(dz * y_ref[...], axis=0, keepdims=True)

    extra_in = [] if pre is None else [_rows(tr, D), _vec(D), _vec(D)]
    return pl.pallas_call(
        body, name=name, grid=(S // tr,),
        in_specs=[_rows(tr, D), _rows(tr, D), _rows(tr, 1), _vec(D), _rows(tr, D), _vec(D)] + extra_in,
        out_specs=[_rows(tr, D), _rows(tr, D)] + [_vec(D)] * (3 + (0 if pre is None else 2)),
        out_shape=[jax.ShapeDtypeStruct((S, D), F32), jax.ShapeDtypeStruct((S, D), BF16)]
        + [jax.ShapeDtypeStruct((1, D), F32)] * (3 + (0 if pre is None else 2)),
        compiler_params=_params(("arbitrary",)),
    )(dxo, xh, rstd, gamma, y, gate, *(pre or ()))


def _mod_bwd(dh, x, sc, dres, name):
    S, D = x.shape
    tr = min(256, S)

    def body(dh_ref, x_ref, sc_ref, dr_ref, dx_ref, dsc_ref, dsh_ref):
        dh_ = dh_ref[...]
        dx_ref[...] = dr_ref[...] + dh_ * (1.0 + sc_ref[...])

        @pl.when(pl.program_id(0) == 0)
        def _():
            dsc_ref[...] = jnp.zeros_like(dsc_ref)
            dsh_ref[...] = jnp.zeros_like(dsh_ref)

        dsc_ref[...] += jnp.sum(dh_ * x_ref[...], axis=0, keepdims=True)
        dsh_ref[...] += jnp.sum(dh_, axis=0, keepdims=True)

    return pl.pallas_call(
        body, name=name, grid=(S // tr,),
        in_specs=[_rows(tr, D), _rows(tr, D), _vec(D), _rows(tr, D)],
        out_specs=[_rows(tr, D), _vec(D), _vec(D)],
        out_shape=[jax.ShapeDtypeStruct((S, D), F32), jax.ShapeDtypeStruct((1, D), F32), jax.ShapeDtypeStruct((1, D), F32)],
        compiler_params=_params(("arbitrary",)),
    )(dh, x, sc, dres)


def _log_sigmoid(z):
    return jnp.minimum(z, 0.0) - jnp.log(1.0 + jnp.exp(-jnp.abs(z)))


def _fox_gate_fwd(proj, b_f, n_heads, name):
    S, PW = proj.shape
    blk = min(256, S)
    last = PW // LANES - 1

    def body(fl_ref, b_ref, cum_ref):
        r = lax.broadcasted_iota(jnp.int32, (blk, blk), 0)
        c = lax.broadcasted_iota(jnp.int32, (blk, blk), 1)
        tril = (c <= r).astype(F32)
        carry = jnp.zeros((1, LANES), F32)
        for i in range(S // blk):
            lf = _log_sigmoid(fl_ref[i * blk:(i + 1) * blk, :] + b_ref[...])
            cum_ref[i * blk:(i + 1) * blk, :] = jnp.dot(tril, lf, preferred_element_type=F32, precision=HIGHEST) + carry
            carry = carry + jnp.sum(lf, axis=0, keepdims=True)

    return pl.pallas_call(
        body, name=name, grid=(1,),
        in_specs=[pl.BlockSpec((S, LANES), lambda i: (0, last)), pl.BlockSpec((1, LANES), lambda i: (0, 0))],
        out_specs=pl.BlockSpec((S, LANES), lambda i: (0, 0)),
        out_shape=jax.ShapeDtypeStruct((S, LANES), F32), compiler_params=_params(("arbitrary",)),
    )(proj, b_f)


def _fox_gate_bwd(dcum, proj, b_f, n_heads, name):
    S, PW = proj.shape
    blk = min(256, S)
    last = PW // LANES - 1
    nb = S // blk

    def body(dc_ref, fl_ref, b_ref, dfl_ref, db_ref):
        r = lax.broadcasted_iota(jnp.int32, (blk, blk), 0)
        c = lax.broadcasted_iota(jnp.int32, (blk, blk), 1)
        triu = (c >= r).astype(F32)
        lane = lax.broadcasted_iota(jnp.int32, (blk, LANES), 1)
        carry = jnp.zeros((1, LANES), F32)
        dbs = jnp.zeros((1, LANES), F32)
        for i in reversed(range(nb)):
            dc = dc_ref[i * blk:(i + 1) * blk, :]
            dlf = jnp.dot(triu, dc, preferred_element_type=F32, precision=HIGHEST) + carry
            carry = carry + jnp.sum(dc, axis=0, keepdims=True)
            z = fl_ref[i * blk:(i + 1) * blk, :] + b_ref[...]
            e = jnp.exp(-jnp.abs(z))
            sig_neg = jnp.where(z >= 0, e / (1.0 + e), 1.0 / (1.0 + e))
            dfl = jnp.where(lane < n_heads, dlf * sig_neg, 0.0)
            dfl_ref[i * blk:(i + 1) * blk, :] = dfl.astype(BF16)
            dbs = dbs + jnp.sum(dfl, axis=0, keepdims=True)
        db_ref[...] = dbs

    return pl.pallas_call(
        body, name=name, grid=(1,),
        in_specs=[pl.BlockSpec((S, LANES), lambda i: (0, 0)), pl.BlockSpec((S, LANES), lambda i: (0, last)),
                  pl.BlockSpec((1, LANES), lambda i: (0, 0))],
        out_specs=[pl.BlockSpec((S, LANES), lambda i: (0, 0)), pl.BlockSpec((1, LANES), lambda i: (0, 0))],
        out_shape=[jax.ShapeDtypeStruct((S, LANES), BF16), jax.ShapeDtypeStruct((1, LANES), F32)],
        compiler_params=_params(("arbitrary",)),
    )(dcum, proj, b_f)


def _fox_scores(q_ref, kb_ref, cq_ref, ck_ref, qi, tq, scale):
    kk = (qi + 1) * tq
    rows = slice(qi * tq, (qi + 1) * tq)
    qb = q_ref[rows, :].astype(BF16)
    s = lax.dot_general(qb, kb_ref[0:kk, :], _DN["nt"], preferred_element_type=F32) * scale
    s = s + (cq_ref[0, rows, :] - ck_ref[0, :, 0:kk])
    r = lax.broadcasted_iota(jnp.int32, (tq, kk), 0) + qi * tq
    c = lax.broadcasted_iota(jnp.int32, (tq, kk), 1)
    mask = c <= r
    return jnp.where(mask, s, NEG), mask, qb


def _fox_fwd(proj, cq, ck, n_heads, name, comm=None):
    S = proj.shape[0]
    H = n_heads
    tq = min(FOX_TQ, S)
    nq = S // tq
    scale = FDH ** -0.5

    def body(q_ref, k_ref, v_ref, cq_ref, ck_ref, o_ref, lse_ref, kb_ref, vb_ref):
        kb_ref[...] = k_ref[...].astype(BF16)
        vb_ref[...] = v_ref[...].astype(BF16)
        for qi in range(nq):
            kk = (qi + 1) * tq
            rows = slice(qi * tq, (qi + 1) * tq)
            s, _, _ = _fox_scores(q_ref, kb_ref, cq_ref, ck_ref, qi, tq, scale)
            m = jnp.max(s, axis=-1, keepdims=True)
            p = jnp.exp(s - m)
            l = jnp.sum(p, axis=-1, keepdims=True)
            p = p * (1.0 / l)
            o_ref[rows, :] = jnp.dot(p.astype(BF16), vb_ref[0:kk, :], preferred_element_type=F32).astype(BF16)
            lse_ref[0, rows, :] = m + jnp.log(l)

    col = lambda off: pl.BlockSpec((S, FDH), lambda h: (0, h + off))
    stat_c = pl.BlockSpec((1, S, 1), lambda h: (h, 0, 0))
    stat_r = pl.BlockSpec((1, 1, S), lambda h: (h, 0, 0))
    return _call(
        body, name=name, grid=(H,),
        in_specs=[col(0), col(H), col(2 * H), stat_c, stat_r],
        out_specs=[col(0), stat_c],
        out_shape=[jax.ShapeDtypeStruct((S, H * FDH), BF16), jax.ShapeDtypeStruct((H, S, 1), F32)],
        scratch_shapes=[pltpu.VMEM((S, FDH), BF16), pltpu.VMEM((S, FDH), BF16)],
        sem=("parallel",), args=(proj, proj, proj, cq, ck), comm=comm)


def _fox_bwd(proj, cq, ck, lse, do, n_heads, name, comm=None):
    S = proj.shape[0]
    H = n_heads
    tq = min(FOX_TQ, S)
    nq = S // tq
    scale = FDH ** -0.5

    def body(q_ref, k_ref, v_ref, cq_ref, ck_ref, lse_ref, do_ref, dq_ref, dk_ref, dv_ref, dcq_ref, dck_ref,
             kb_ref, vb_ref, dka_ref, dva_ref):
        kb_ref[...] = k_ref[...].astype(BF16)
        vb_ref[...] = v_ref[...].astype(BF16)
        dka_ref[...] = jnp.zeros_like(dka_ref)
        dva_ref[...] = jnp.zeros_like(dva_ref)
        dck_ref[...] = jnp.zeros_like(dck_ref)
        for qi in range(nq):
            kk = (qi + 1) * tq
            rows = slice(qi * tq, (qi + 1) * tq)
            s, mask, qb = _fox_scores(q_ref, kb_ref, cq_ref, ck_ref, qi, tq, scale)
            p = jnp.where(mask, jnp.exp(s - lse_ref[0, rows, :]), 0.0)
            dob = do_ref[rows, :]
            dp = lax.dot_general(dob, vb_ref[0:kk, :], _DN["nt"], preferred_element_type=F32)
            delta = jnp.sum(p * dp, axis=-1, keepdims=True)
            ds = p * (dp - delta)
            dcq_ref[0, rows, :] = jnp.sum(ds, axis=-1, keepdims=True)
            dck_ref[0, :, 0:kk] -= jnp.sum(ds, axis=0, keepdims=True)
            dsb = (ds * scale).astype(BF16)
            dq_ref[rows, :] = jnp.dot(dsb, kb_ref[0:kk, :], preferred_element_type=F32).astype(BF16)
            dka_ref[0:kk, :] += lax.dot_general(dsb, qb, _DN["tn"], preferred_element_type=F32)
            dva_ref[0:kk, :] += lax.dot_general(p.astype(BF16), dob, _DN["tn"], preferred_element_type=F32)
        dk_ref[...] = dka_ref[...].astype(BF16)
        dv_ref[...] = dva_ref[...].astype(BF16)

    col = lambda off: pl.BlockSpec((S, FDH), lambda h: (0, h + off))
    stat_c = pl.BlockSpec((1, S, 1), lambda h: (h, 0, 0))
    stat_r = pl.BlockSpec((1, 1, S), lambda h: (h, 0, 0))
    wide = jax.ShapeDtypeStruct((S, H * FDH), BF16)
    return _call(
        body, name=name, grid=(H,),
        in_specs=[col(0), col(H), col(2 * H), stat_c, stat_r, stat_c, col(0)],
        out_specs=[col(0), col(0), col(0), stat_c, stat_r],
        out_shape=[wide, wide, wide, jax.ShapeDtypeStruct((H, S, 1), F32), jax.ShapeDtypeStruct((H, 1, S), F32)],
        scratch_shapes=[pltpu.VMEM((S, FDH), BF16), pltpu.VMEM((S, FDH), BF16), pltpu.VMEM((S, FDH), F32), pltpu.VMEM((S, FDH), F32)],
        sem=("parallel",), args=(proj, proj, proj, cq, ck, lse, do), comm=comm)


def _rope_tables(pos, sign):
    inv = ROPE_THETA ** (-jnp.arange(0, ROPE_DIM, 2, dtype=F32) / ROPE_DIM)
    ang = pos.astype(F32)[:, None] * inv
    cos, sin = jnp.cos(ang), sign * jnp.sin(ang)
    l64 = jnp.arange(LANES) % SDH
    idx = l64 % (ROPE_DIM // 2)
    c = jnp.where(l64 < ROPE_DIM, cos[:, idx], 1.0)
    sa = jnp.where(l64 < ROPE_DIM // 2, -sin[:, idx], 0.0)
    sb = jnp.where((l64 >= ROPE_DIM // 2) & (l64 < ROPE_DIM), sin[:, idx], 0.0)
    rot = jnp.stack([c, sa, sb])
    ident = jnp.stack([jnp.ones_like(c), jnp.zeros_like(c), jnp.zeros_like(c)])
    return jnp.stack([rot, ident]).astype(F32)


def _rope(xin, tabs, n_rot, out_dtype, name, comm=None):
    S, W = xin.shape

    def body(x_ref, t_ref, o_ref):
        xv = x_ref[...]
        o = xv * t_ref[0, 0] + pltpu.roll(xv, LANES - ROPE_DIM // 2, 1) * t_ref[0, 1] + pltpu.roll(xv, ROPE_DIM // 2, 1) * t_ref[0, 2]
        o_ref[...] = o.astype(out_dtype)

    return _call(
        body, name=name, grid=(W // LANES,),
        in_specs=[pl.BlockSpec((S, LANES), lambda j: (0, j)),
                  pl.BlockSpec((1, 3, S, LANES), lambda j: (jnp.where(j < n_rot, 0, 1), 0, 0, 0))],
        out_specs=[pl.BlockSpec((S, LANES), lambda j: (0, j))],
        out_shape=[jax.ShapeDtypeStruct((S, W), out_dtype)], sem=("parallel",), args=(xin, tabs), comm=comm)[0]


SWA_PER_STEP = 8


def _swa_bias():
    r = jnp.arange(WIN)[:, None]
    c = jnp.arange(2 * WIN)[None, :]
    first = c <= r
    later = (c > r) & (c <= r + WIN)
    return jnp.where(jnp.stack([first, later]), 0.0, NEG).astype(F32)


def _swa_probs(q_ref, k_ref, sk_ref, b_ref, n, j, scale):
    st = pl.multiple_of(jnp.maximum(n - 1, 0) * WIN, WIN)
    qb = q_ref[0, j]
    kb = k_ref[0, pl.ds(st, 2 * WIN), :]
    gm = qb.shape[0]
    s = lax.dot_general(qb, kb, _DN["nt"], preferred_element_type=F32) * scale
    s = (s.reshape(gm // WIN, WIN, 2 * WIN) + b_ref[jnp.minimum(n, 1)][None]).reshape(gm, 2 * WIN)
    sink = sk_ref[0]
    m = jnp.maximum(jnp.max(s, axis=-1, keepdims=True), sink)
    e = jnp.exp(s - m)
    es = jnp.exp(sink - m)
    inv = 1.0 / (jnp.sum(e, axis=-1, keepdims=True) + es)
    return e * inv, es * inv, st, qb, kb


def _swa_specs(S, gm):
    blk = pl.BlockSpec((1, SWA_PER_STEP, gm, SDH), lambda g, n: (g, n, 0, 0))
    kv = pl.BlockSpec((1, S, SDH), lambda g, n: (g, 0, 0))
    col = pl.BlockSpec((1, gm, 1), lambda g, n: (g, 0, 0))
    bias = pl.BlockSpec((2, WIN, 2 * WIN), lambda g, n: (0, 0, 0))
    return blk, kv, col, bias


def _swa_fwd(q, k, v, sinks, name, comm=None):
    KH, nb, gm, _ = q.shape
    S = k.shape[1]
    scale = SDH ** -0.5

    def body(q_ref, k_ref, v_ref, sk_ref, b_ref, o_ref):
        for j in range(SWA_PER_STEP):
            p, _, st, _, _ = _swa_probs(q_ref, k_ref, sk_ref, b_ref, pl.program_id(1) * SWA_PER_STEP + j, j, scale)
            vb = v_ref[0, pl.ds(st, 2 * WIN), :]
            o_ref[0, j] = jnp.dot(p.astype(BF16), vb, preferred_element_type=F32).astype(BF16)

    blk, kv, col, bias = _swa_specs(S, gm)
    return _call(
        body, name=name, grid=(KH, nb // SWA_PER_STEP), in_specs=[blk, kv, kv, col, bias], out_specs=[blk],
        out_shape=[jax.ShapeDtypeStruct(q.shape, BF16)], sem=("parallel", "parallel"), args=(q, k, v, sinks, _swa_bias()), comm=comm)[0]


def _swa_bwd(q, k, v, sinks, do, name, comm=None):
    KH, nb, gm, _ = q.shape
    S = k.shape[1]
    scale = SDH ** -0.5

    def body(q_ref, k_ref, v_ref, sk_ref, b_ref, do_ref, dq_ref, dk_ref, dv_ref, dsk_ref):
        @pl.when(pl.program_id(1) == 0)
        def _():
            dk_ref[...] = jnp.zeros_like(dk_ref)
            dv_ref[...] = jnp.zeros_like(dv_ref)
            dsk_ref[...] = jnp.zeros_like(dsk_ref)

        blocks = []
        for j in range(SWA_PER_STEP):
            p, ps, st, qb, kb = _swa_probs(q_ref, k_ref, sk_ref, b_ref, pl.program_id(1) * SWA_PER_STEP + j, j, scale)
            vb = v_ref[0, pl.ds(st, 2 * WIN), :]
            dob = do_ref[0, j]
            dp = lax.dot_general(dob, vb, _DN["nt"], preferred_element_type=F32)
            delta = jnp.sum(p * dp, axis=-1, keepdims=True)
            dsb = (p * (dp - delta) * scale).astype(BF16)
            dq_ref[0, j] = jnp.dot(dsb, kb, preferred_element_type=F32)
            blocks.append((st, lax.dot_general(dsb, qb, _DN["tn"], preferred_element_type=F32),
                           lax.dot_general(p.astype(BF16), dob, _DN["tn"], preferred_element_type=F32), ps * delta))
        for st, dk, dv, dsk in blocks:
            dk_ref[0, pl.ds(st, 2 * WIN), :] += dk
            dv_ref[0, pl.ds(st, 2 * WIN), :] += dv
            dsk_ref[0] -= dsk

    blk, kv, col, bias = _swa_specs(S, gm)
    return _call(
        body, name=name, grid=(KH, nb // SWA_PER_STEP), in_specs=[blk, kv, kv, col, bias, blk], out_specs=[blk, kv, kv, col],
        out_shape=[jax.ShapeDtypeStruct(q.shape, F32), jax.ShapeDtypeStruct(k.shape, F32),
                   jax.ShapeDtypeStruct(k.shape, F32), jax.ShapeDtypeStruct(sinks.shape, F32)],
        sem=("parallel", "arbitrary"), args=(q, k, v, sinks, _swa_bias(), do), comm=comm)


def _shift_down(u, k):
    row = lax.broadcasted_iota(jnp.int32, u.shape, 0)
    return jnp.where(row >= k, pltpu.roll(u, k, 0), 0.0)


def _shift_up(u, k):
    n = u.shape[0]
    row = lax.broadcasted_iota(jnp.int32, u.shape, 0)
    return jnp.where(row < n - k, pltpu.roll(u, n - k, 0), 0.0)


def _conv3(u, w_ref, b_ref):
    return w_ref[0:1, :] * _shift_down(u, 2) + w_ref[1:2, :] * _shift_down(u, 1) + w_ref[2:3, :] * u + b_ref[...]


def _conv_gate(u, cw, cb, name, comm=None):
    S, F2 = u.shape
    Fh = F2 // 2
    tc = _tile(Fh, 256)
    nf = Fh // tc

    def body(ug_ref, uv_ref, wg_ref, wv_ref, bg_ref, bv_ref, a_ref):
        g = _conv3(ug_ref[...], wg_ref, bg_ref)
        val = _conv3(uv_ref[...], wv_ref, bv_ref)
        a_ref[...] = (g * (1.0 / (1.0 + jnp.exp(-g))) * val).astype(BF16)

    blk = lambda r, off: pl.BlockSpec((r, tc), lambda j: (0, j + off))
    return _call(
        body, name=name, grid=(nf,),
        in_specs=[blk(S, 0), blk(S, nf), blk(3, 0), blk(3, nf), blk(1, 0), blk(1, nf)], out_specs=[blk(S, 0)],
        out_shape=[jax.ShapeDtypeStruct((S, Fh), BF16)], sem=("parallel",), args=(u, u, cw, cw, cb, cb), comm=comm)[0]


def _conv_gate_bwd(u, da, cw, cb, name, comm=None):
    S, F2 = u.shape
    Fh = F2 // 2
    tc = _tile(Fh, 256)
    nf = Fh // tc

    def half(h, dx, uu, w_ref, du_ref, dw_ref, db_ref):
        up1, up2 = _shift_up(dx, 1), _shift_up(dx, 2)
        du = w_ref[2:3, :] * dx + w_ref[1:2, :] * up1 + w_ref[0:1, :] * up2
        du_ref[h] = du.astype(BF16)
        dw_ref[h, 0:1, :] = jnp.sum(up2 * uu, axis=0, keepdims=True)
        dw_ref[h, 1:2, :] = jnp.sum(up1 * uu, axis=0, keepdims=True)
        dw_ref[h, 2:3, :] = jnp.sum(dx * uu, axis=0, keepdims=True)
        db_ref[h] = jnp.sum(dx, axis=0, keepdims=True)

    def body(ug_ref, uv_ref, da_ref, wg_ref, wv_ref, bg_ref, bv_ref, du_ref, dw_ref, db_ref):
        ug = ug_ref[...]
        uv = uv_ref[...]
        g = _conv3(ug, wg_ref, bg_ref)
        val = _conv3(uv, wv_ref, bv_ref)
        sig = 1.0 / (1.0 + jnp.exp(-g))
        da_ = da_ref[...]
        dg = da_ * val * (sig * (1.0 + g * (1.0 - sig)))
        dval = da_ * (g * sig)
        half(0, dg, ug, wg_ref, du_ref, dw_ref, db_ref)
        half(1, dval, uv, wv_ref, du_ref, dw_ref, db_ref)

    blk = lambda r, off: pl.BlockSpec((r, tc), lambda j: (0, j + off))
    both = lambda r: pl.BlockSpec((2, r, tc), lambda j: (0, 0, j))
    return _call(
        body, name=name, grid=(nf,),
        in_specs=[blk(S, 0), blk(S, nf), blk(S, 0), blk(3, 0), blk(3, nf), blk(1, 0), blk(1, nf)],
        out_specs=[both(S), both(3), both(1)],
        out_shape=[jax.ShapeDtypeStruct((2, S, Fh), BF16), jax.ShapeDtypeStruct((2, 3, Fh), F32), jax.ShapeDtypeStruct((2, 1, Fh), F32)],
        sem=("parallel",), args=(u, u, da, cw, cw, cb, cb), comm=comm)


def _to_groups(t, kh):
    S, width = t.shape
    g = width // SDH // kh
    return t.reshape(S // WIN, WIN, kh, g, SDH).transpose(2, 0, 3, 1, 4).reshape(kh, S // WIN, g * WIN, SDH)


def _from_groups(t):
    kh, nb, gm, _ = t.shape
    g = gm // WIN
    return t.reshape(kh, nb, g, WIN, SDH).transpose(1, 3, 0, 2, 4).reshape(nb * WIN, kh * g * SDH)


class LocalWeights:
    def __init__(self, weights):
        self.weights, self.grads = weights, {}

    def w(self, name):
        return self.weights[name]

    def carry(self, stage, last=()):
        return None

    def carried(self, stage, comm):
        pass

    def grad(self, name, g):
        self.grads[name] = g


def _local_step(dm, x, tgt, pos, mod, sp, pp, h0=None):
    S, D, FH, QH, KH, Fh = dm
    m = [[mod[i:i + 1, j * D:(j + 1) * D] for j in range(6)] for i in range(DEPTH)]

    last = []

    def run(fn, *args, name, **kw):
        comm = pp.carry(name, last)
        args = [pp.w(arg[1]) if isinstance(arg, tuple) and arg[:1] == ("w",) else arg for arg in args]
        out = fn(*args, name=name, comm=comm, **kw)
        if comm is not None:
            pp.carried(name, comm)
        last[:] = list(out) if isinstance(out, (list, tuple)) else [out]
        return out

    sv = []
    xs = x
    h = _modulate(xs, m[0][1], m[0][0], "mod_in") if h0 is None else h0
    last[:] = [h]
    for i in range(DEPTH):
        sh1, sc1, g1, sh2, sc2, g2 = m[i]
        L = {}
        L["x_in"], L["h1"] = xs, h
        if i == 0:
            proj = run(_mm, h[None], ("w", "fox_w_in"), mode="nn", out_dtype=F32, name="fox_proj", tn=896)[0]
            cum = _fox_gate_fwd(proj, sp["fox_b_f"], FH, "fox_gate")
            cq = cum[:, :FH].T[:, :, None]
            ck = cum[:, :FH].T[:, None, :]
            o, lse = run(_fox_fwd, proj, cq, ck, FH, name="fox_attn")
            L.update(proj=proj, cq=cq, ck=ck, lse=lse, o=o)
            y = run(_mm, o[None], ("w", "fox_w_o"), mode="nn", out_dtype=F32, name="fox_out")[0]
        else:
            proj = run(_mm, h[None], ("w", "swa_w_in"), mode="nn", out_dtype=F32, name="swa_proj", tn=640)[0]
            tabs = _rope_tables(pos, 1.0)
            n_rot = (QH + KH) * SDH // LANES
            pr = run(_rope, proj, tabs, n_rot, BF16, name="swa_rope")
            qh = _to_groups(pr[:, :QH * SDH], KH)
            kh = pr[:, QH * SDH:(QH + KH) * SDH].reshape(S, KH, SDH).transpose(1, 0, 2)
            vh = pr[:, (QH + KH) * SDH:].reshape(S, KH, SDH).transpose(1, 0, 2)
            oh = run(_swa_fwd, qh, kh, vh, sp["sinks"], name="swa_attn")
            o = _from_groups(oh)
            L.update(qh=qh, kh=kh, vh=vh, o=o)
            y = run(_mm, o[None], ("w", "swa_w_o"), mode="nn", out_dtype=F32, name="swa_out")[0]
        L["y1"] = y
        x1, L["xh1"], L["rs1"], h2 = run(_ln_fwd, xs, y, g1, sp["ln_mix_g"][i], sp["ln_mix_b"][i], sc2, sh2, name=f"ln_mix{i}")
        L["x1"], L["h2"] = x1, h2
        u = run(_mm, h2[None], ("w", f"ffn_w_up{i}"), mode="nn", out_dtype=F32, name=f"ffn_up{i}", tm=1024, tn=1408)[0]
        a = run(_conv_gate, u, sp["conv_w"][i], sp["conv_b"][i], name=f"ffn_gate{i}")
        y2 = run(_mm, a[None], ("w", f"ffn_w_down{i}"), mode="nn", out_dtype=F32, name=f"ffn_down{i}", tk=5632, tm=512)[0]
        L.update(u=u, a=a, y2=y2)
        if i + 1 < DEPTH:
            xs, L["xh2"], L["rs2"], h = run(_ln_fwd, x1, y2, g2, sp["ln_ffn_g"][i], sp["ln_ffn_b"][i], m[i + 1][1], m[i + 1][0], name=f"ln_ffn{i}")
        else:
            xs, L["xh2"], L["rs2"] = run(_ln_fwd, x1, y2, g2, sp["ln_ffn_g"][i], sp["ln_ffn_b"][i], None, None, name=f"ln_ffn{i}")
        sv.append(L)

    dx, loss_cols = _loss_head(xs, tgt, "loss_head")

    gs = {k: [None] * DEPTH for k in ("conv_w", "conv_b", "ln_mix_g", "ln_mix_b", "ln_ffn_g", "ln_ffn_b")}
    dmp = [dict() for _ in range(DEPTH)]
    dres, pend = dx, None
    for i in reversed(range(DEPTH)):
        sh1, sc1, g1, sh2, sc2, g2 = m[i]
        L = sv[i]
        res = _ln_bwd(dres, L["xh2"], L["rs2"], sp["ln_ffn_g"][i], L["y2"], g2, f"ln_ffn_bwd{i}",
                      None if pend is None else (*pend, sp["ln_ffn_b"][i]))
        dres, dy, gs["ln_ffn_g"][i], gs["ln_ffn_b"][i], dmp[i]["g2"] = res[:5]
        if pend is not None:
            dmp[i + 1]["sc1"], dmp[i + 1]["sh1"] = res[5:]
        da = run(_mm, dy[None], pp.w(f"ffn_w_down{i}"), mode="nt", out_dtype=F32, name=f"ffn_down_dx{i}", tm=1024, tn=1408)[0]
        pp.grad(f"ffn_w_down{i}", run(_mm, L["a"][None], dy[None], mode="tn", out_dtype=BF16, name=f"ffn_down_dw{i}", tm=1408))
        du, dcw, dcb = run(_conv_gate_bwd, L["u"], da, sp["conv_w"][i], sp["conv_b"][i], name=f"ffn_gate_bwd{i}")
        gs["conv_w"][i] = dcw.transpose(1, 0, 2).reshape(3, 2 * Fh)
        gs["conv_b"][i] = dcb.transpose(1, 0, 2).reshape(1, 2 * Fh)
        pp.grad(f"ffn_w_up{i}", run(_mm, L["h2"][None], du, mode="tn", out_dtype=BF16, name=f"ffn_up_dw{i}", out_groups=N_CHIPS, tn=1408))
        dh2 = run(_mm, du, pp.w(f"ffn_w_up{i}"), mode="nt", out_dtype=F32, name=f"ffn_up_dx{i}", tk=2816)[0]
        dres, dy, gs["ln_mix_g"][i], gs["ln_mix_b"][i], dmp[i]["g1"], dmp[i]["sc2"], dmp[i]["sh2"] = _ln_bwd(
            dres, L["xh1"], L["rs1"], sp["ln_mix_g"][i], L["y1"], g1, f"ln_mix_bwd{i}", (dh2, sc2, sp["ln_mix_b"][i]))
        if i == 0:
            do = run(_mm, dy[None], pp.w("fox_w_o"), mode="nt", out_dtype=BF16, name="fox_out_dx")[0]
            pp.grad("fox_w_o", run(_mm, L["o"][None], dy[None], mode="tn", out_dtype=BF16, name="fox_out_dw"))
            dq, dk, dv, dcq, dck = run(_fox_bwd, L["proj"], L["cq"], L["ck"], L["lse"], do, FH, name="fox_attn_bwd")
            dcum = dcq[:, :, 0].T + dck[:, 0, :].T
            dcum = jnp.pad(dcum, ((0, 0), (0, LANES - FH)))
            dfl, db_f = _fox_gate_bwd(dcum, L["proj"], sp["fox_b_f"], FH, "fox_gate_bwd")
            gs["fox_b_f"] = db_f
            dproj = jnp.concatenate([dq, dk, dv, dfl], axis=1)
            pp.grad("fox_w_in", run(_mm, L["h1"][None], dproj[None], mode="tn", out_dtype=BF16, name="fox_proj_dw", tn=896))
            dh1 = run(_mm, dproj[None], pp.w("fox_w_in"), mode="nt", out_dtype=F32, name="fox_proj_dx", tk=6272, tm=512)[0]
        else:
            do = run(_mm, dy[None], pp.w("swa_w_o"), mode="nt", out_dtype=BF16, name="swa_out_dx")[0]
            pp.grad("swa_w_o", run(_mm, L["o"][None], dy[None], mode="tn", out_dtype=BF16, name="swa_out_dw"))
            dqh, dkh, dvh, dsk = run(_swa_bwd, L["qh"], L["kh"], L["vh"], sp["sinks"], _to_groups(do, KH), name="swa_attn_bwd")
            gs["sinks"] = jnp.sum(dsk.reshape(QH, WIN), axis=1)
            dpr = jnp.concatenate([_from_groups(dqh), dkh.transpose(1, 0, 2).reshape(S, KH * SDH),
                                   dvh.transpose(1, 0, 2).reshape(S, KH * SDH)], axis=1)
            n_rot = (QH + KH) * SDH // LANES
            dproj = _rope(dpr, _rope_tables(pos, -1.0), n_rot, BF16, "swa_rope_bwd")
            dh1 = run(_mm, dproj[None], pp.w("swa_w_in"), mode="nt", out_dtype=F32, name="swa_proj_dx", tk=640)[0]
            pp.grad("swa_w_in", run(_mm, L["h1"][None], dproj[None], mode="tn", out_dtype=BF16, name="swa_proj_dw", out_groups=N_CHIPS, tn=640))
        pend = (dh1, sc1)
    grad_x, dmp[0]["sc1"], dmp[0]["sh1"] = _mod_bwd(pend[0], sv[0]["x_in"], pend[1], dres, "mod_mix_bwd0")
    dmod = [jnp.concatenate([p["sh1"], p["sc1"], p["g1"], p["sh2"], p["sc2"], p["g2"]], axis=1) for p in dmp]
    return loss_cols, grad_x, gs, jnp.concatenate(dmod, axis=0)


def _allgather_small(v, name):
    m_per, n = v.shape

    def body(x_ref, out_ref, send_sems, recv_sems, local_sem):
        x, y, c, chips = _place()
        me, sibling = (x, y, c), (x, y, 1 - c)

        def rows(px, py, pc):
            return out_ref.at[pl.ds((4 * px + 2 * py + pc) * m_per, m_per), :]

        def copy(k, block, to, src=None):
            return _remote(rows(*block) if src is None else src, rows(*block), send_sems.at[k], recv_sems.at[k], to)

        mine = pltpu.make_async_copy(x_ref, rows(*me), local_sem)
        mine.start()
        first = [copy(0, me, sibling, src=x_ref)]
        first += [copy(1 + j, me, (*chip, c), src=x_ref) for j, chip in enumerate(chips)]
        for cp in first:
            cp.start()
        passed = [copy(4 + j, (*chip, c), sibling) for j, chip in enumerate(chips)]
        for j, chip in enumerate(chips):
            copy(1 + j, (*chip, c), me).wait_recv()
            passed[j].start()
        copy(0, sibling, me).wait_recv()
        for j, chip in enumerate(chips):
            copy(4 + j, (*chip, 1 - c), me).wait_recv()
        for cp in first + passed:
            cp.wait_send()
        mine.wait()

    return pl.pallas_call(
        body, name=name, out_shape=jax.ShapeDtypeStruct((N_DEV * m_per, n), v.dtype),
        in_specs=[pl.BlockSpec(memory_space=pltpu.VMEM)], out_specs=pl.BlockSpec(memory_space=pltpu.VMEM),
        scratch_shapes=[pltpu.SemaphoreType.DMA((7,)), pltpu.SemaphoreType.DMA((7,)), pltpu.SemaphoreType.DMA],
        compiler_params=pltpu.CompilerParams(vmem_limit_bytes=VMEM_LIMIT),
    )(v)


def _row_tile(r, pref=256):
    return _tile(r, pref, 16)


def _cast_bf16(w, layer, chip, name, after=()):
    _, R, C = w.shape
    tr = _row_tile(R)

    def body(s_ref, w_ref, *rest):
        rest[-1][...] = w_ref[...].astype(BF16)

    return pl.pallas_call(
        body, name=name,
        grid_spec=pltpu.PrefetchScalarGridSpec(
            num_scalar_prefetch=1, grid=(R // tr,),
            in_specs=[pl.BlockSpec((None, tr, C), lambda i, s: (layer, i, 0))] + _any_specs(len(after)),
            out_specs=pl.BlockSpec((None, tr, C), lambda i, s: (s[0], i, 0))),
        out_shape=jax.ShapeDtypeStruct((N_CHIPS, R, C), BF16), compiler_params=_params(("parallel",)),
    )(jnp.reshape(chip, (1,)).astype(jnp.int32), w, *after)


def _add_sibling(g, got, c, name):
    G, R, C = g.shape
    rh = R // 2
    tr = _row_tile(rh)
    nb = rh // tr

    def body(c_ref, g_ref, o_ref, p_ref):
        p_ref[...] = (g_ref[...].astype(F32) + o_ref[...].astype(F32)).astype(BF16)

    return pl.pallas_call(
        body, name=name,
        grid_spec=pltpu.PrefetchScalarGridSpec(
            num_scalar_prefetch=1, grid=(G, nb),
            in_specs=[pl.BlockSpec((1, tr, C), lambda s, i, c_ref: (s, c_ref[0] * nb + i, 0)),
                      pl.BlockSpec((1, tr, C), lambda s, i, c_ref: (s, i, 0))],
            out_specs=pl.BlockSpec((1, tr, C), lambda s, i, c_ref: (s, i, 0))),
        out_shape=jax.ShapeDtypeStruct((G, rh, C), BF16), compiler_params=_params(("parallel", "parallel")),
    )(jnp.reshape(c, (1,)).astype(jnp.int32), g, got)


def _sum_chips(part, landed, chip, c, name):
    G, rh, C = part.shape
    tr = _row_tile(rh)
    nb = rh // tr

    def body(p_ref, own_ref, *rest):
        acc = own_ref[...].astype(F32)
        for ref in rest[:G - 1]:
            acc = acc + ref[...].astype(F32)
        rest[G - 1][...] = acc

    slot = lambda k: pl.BlockSpec((None, tr, C), lambda i, p: ((p[0] + k) % G, i, 0))
    return pl.pallas_call(
        body, name=name,
        grid_spec=pltpu.PrefetchScalarGridSpec(
            num_scalar_prefetch=1, grid=(nb,), in_specs=[slot(k) for k in range(G)],
            out_specs=pl.BlockSpec((tr, C), lambda i, p: (p[1] * nb + i, 0))),
        out_shape=jax.ShapeDtypeStruct((2 * rh, C), F32), compiler_params=_params(("parallel",)),
    )(jnp.stack([chip, c]).astype(jnp.int32), part, *([landed] * (G - 1)))


def _adam_math(w, g, m, v):
    m = ADAM_B1 * m + (1.0 - ADAM_B1) * g
    v = ADAM_B2 * v + (1.0 - ADAM_B2) * (g * g)
    m_hat = m / (1.0 - ADAM_B1 ** ADAM_STEP)
    v_hat = v / (1.0 - ADAM_B2 ** ADAM_STEP)
    delta = -ADAM_LR * (m_hat / (jnp.sqrt(v_hat) + ADAM_EPS) + ADAM_WD * w)
    return delta, m, v


def _adamw(w, g, m, v, layer, prev, name, by_cols=False, after=()):
    L, R, C = w.shape
    tr = R if by_cols else _tile(R, 128, 8)
    tc = _tile(C, 256) if by_cols else C
    n_alias = len(prev)
    prev = tuple(prev) + tuple(after)
    n_prev = len(prev)

    def body(w_ref, g_ref, m_ref, v_ref, *rest):
        go_ref, d_ref, mo_ref, vo_ref = rest[n_prev:]
        gv = g_ref[...]
        go_ref[...] = gv
        d_ref[...], mo_ref[...], vo_ref[...] = _adam_math(w_ref[...], gv, m_ref[...], v_ref[...])

    lay = pl.BlockSpec((None, tr, tc), lambda i: (layer, i // (C // tc), i % (C // tc)))
    flat = pl.BlockSpec((tr, tc), lambda i: (i // (C // tc), i % (C // tc)))
    return _call(
        body, name=name, grid=((R // tr) * (C // tc),), in_specs=[lay, flat, lay, lay] + _any_specs(n_prev), out_specs=[lay] * 4,
        out_shape=[jax.ShapeDtypeStruct((L, R, C), F32)] * 4, aliases={4 + k: k for k in range(n_alias)},
        sem=("parallel",), args=(w, g, m, v, *prev))


def _cond_rows(c_row, cw, name, after=()):
    D = c_row.shape[1]
    nr, fc = cw.shape

    def body(c_ref, e_ref, *rest):
        o_ref = rest[-1]
        o_ref[...] = jnp.zeros_like(o_ref)
        cv = c_ref[...]
        o_ref[0:1, 0:D] = cv * (1.0 / (1.0 + jnp.exp(-cv)))
        o_ref[8:8 + nr, 0:fc] = e_ref[...]

    vmem = pl.BlockSpec(memory_space=pltpu.VMEM)
    return pl.pallas_call(body, name=name, in_specs=[vmem, vmem] + _any_specs(len(after)), out_specs=vmem,
                          out_shape=jax.ShapeDtypeStruct((16, max(D, fc)), F32))(c_row, cw, *after)


def _ada_fwd(cact, ada_w, ada_b, layer, chip, name):
    _, D, NC = ada_w.shape
    tn = _tile(NC, 1024)
    nj = NC // tn

    def body(idx_ref, c_ref, w_ref, b_ref, o_ref):
        acc = jnp.dot(c_ref[...].astype(BF16), w_ref[0].astype(BF16), preferred_element_type=F32)
        o_ref[...] = acc + b_ref[pl.ds(idx_ref[0], 1), :]

    return pl.pallas_call(
        body, name=name,
        grid_spec=pltpu.PrefetchScalarGridSpec(
            num_scalar_prefetch=1, grid=(nj,),
            in_specs=[pl.BlockSpec((8, D), lambda j, idx: (0, 0)),
                      pl.BlockSpec((1, D, tn), lambda j, idx: (idx[0], 0, j)),
                      pl.BlockSpec((DEPTH, tn), lambda j, idx: (0, idx[1] * nj + j))],
            out_specs=pl.BlockSpec((8, tn), lambda j, idx: (0, j))),
        out_shape=jax.ShapeDtypeStruct((8, NC), F32), compiler_params=_params(("parallel",)),
    )(jnp.stack([layer, chip]).astype(jnp.int32), cact, ada_w, ada_b)


def _ada_grad_adamw(cact_t, dmod, w, m, v, name):
    L, D, NC = w.shape
    tr = _tile(D, 128, 8)

    def body(c_ref, d_ref, w_ref, m_ref, v_ref, g_ref, dl_ref, mo_ref, vo_ref):
        g = jnp.dot(c_ref[...], d_ref[...], preferred_element_type=F32, precision=HIGHEST)
        g_ref[...] = g
        dl_ref[...], mo_ref[...], vo_ref[...] = _adam_math(w_ref[...], g, m_ref[...], v_ref[...])

    lay = pl.BlockSpec((None, tr, NC), lambda l, i: (l, i, 0))
    return _call(
        body, name=name, grid=(L, D // tr),
        in_specs=[pl.BlockSpec((tr, N_DEV), lambda l, i: (i, 0)), pl.BlockSpec((None, N_DEV, NC), lambda l, i: (l, 0, 0)), lay, lay, lay],
        out_specs=[lay] * 4, out_shape=[jax.ShapeDtypeStruct((L, D, NC), F32)] * 4,
        sem=("parallel", "parallel"), args=(cact_t, dmod, w, m, v))


def _sum_devices(gathered, name):
    n, R, C = gathered.shape

    def body(g_ref, o_ref):
        acc = g_ref[0]
        for j in range(1, n):
            acc = acc + g_ref[j]
        o_ref[...] = acc

    return pl.pallas_call(body, name=name, out_shape=jax.ShapeDtypeStruct((R, C), F32),
                          compiler_params=pltpu.CompilerParams(vmem_limit_bytes=VMEM_LIMIT))(gathered)


def _adamw_small(w, g, m, v, name):
    def body(w_ref, g_ref, m_ref, v_ref, d_ref, mo_ref, vo_ref):
        d_ref[...], mo_ref[...], vo_ref[...] = _adam_math(w_ref[...], g_ref[...], m_ref[...], v_ref[...])

    return pl.pallas_call(body, name=name, out_shape=[jax.ShapeDtypeStruct(w.shape, F32)] * 3)(w, g, m, v)


def _pad_rows(flat, unit=8 * LANES):
    n = flat.shape[0]
    total = -(-n // unit) * unit
    return jnp.pad(flat, (0, total - n)).reshape(total // LANES, LANES)


def _pad_lanes(v2d):
    return jnp.pad(v2d.reshape(1, -1), ((0, 0), (0, LANES - v2d.size)))


FORWARD = {
    "fox_proj": ((), ("ffn_w_up0",)),
    "ffn_up0": (("ffn_w_up0",), ("ffn_w_down0", "swa_w_in")),
    "ffn_down0": (("ffn_w_down0", "swa_w_in"), ("swa_w_o", "ffn_w_up1")),
    "swa_out": (("swa_w_o", "ffn_w_up1"), ("ffn_w_down1",)),
    "ffn_down1": (("ffn_w_down1",), ()),
}
PLAN = {
    "ffn_gate_bwd1": [("swap", "ffn_w_down1")],
    "ffn_up_dx1": [("scatter", "ffn_w_down1", 0, 1, 1), ("swap", "ffn_w_up1")],
    "swa_attn_bwd": [("scatter", "ffn_w_up1", 0, 4, 8), ("swap", "swa_w_o")],
    "swa_proj_dx": [("scatter", "swa_w_o", 0, 1, 1)],
    "ffn_down_dx0": [("scatter", "ffn_w_up1", 4, 6, 8), ("swap", "swa_w_in")],
    "ffn_down_dw0": [("scatter", "swa_w_in", 0, 1, 1)],
    "ffn_gate_bwd0": [("scatter", "ffn_w_up1", 6, 8, 8), ("swap", "ffn_w_down0")],
    "ffn_up_dx0": [("scatter", "ffn_w_down0", 0, 1, 1), ("swap", "ffn_w_up0")],
    "fox_attn_bwd": [("scatter", "ffn_w_up0", 0, 5, 8), ("swap", "fox_w_o")],
    "fox_proj_dw": [("scatter", "fox_w_o", 0, 1, 1), ("scatter", "ffn_w_up0", 5, 6, 8)],
    "fox_proj_dx": [("scatter", "ffn_w_up0", 6, 8, 8), ("swap", "fox_w_in")],
}


class Exchanges:
    def __init__(self, dm, slots, chip, c):
        self.dm, self.slots, self.chip, self.c = dm, dict(slots), chip, c
        self.raw, self.part, self.landed, self.grads, self.views, self.pending = {}, {}, {}, {}, {}, {}

    def gather_start(self, keys, name, after):
        self.first = (keys, _gather_comm([self.slots[k] for k in keys]))
        self.first_state, token = _split_start(self.first[1], name + "_start", after)
        return token

    def gather_finish(self, after, name):
        keys, comm = self.first
        _split_wait(comm, self.first_state, after, name + "_wait")
        pass_on = _forward_comm(comm.results)
        _run_comm(pass_on, name + "_pass")
        self.slots.update(zip(keys, pass_on.results))
        self.fence = list(pass_on.results)

    def before(self, stage, last):
        need, nxt = FORWARD[stage]
        if need:
            self.gather_finish(list(last), "gather_" + "_".join(need))
        if nxt:
            return self.gather_start(list(nxt), "gather_" + "_".join(nxt), list(last) + self.fence)
        return None

    def w(self, key):
        if key not in self.views:
            S, D, FH, QH, KH, Fh = self.dm
            full = self.slots[key]
            if key == "fox_w_in":
                cols = full.shape[2]
                full = jnp.pad(full.transpose(1, 0, 2).reshape(D, N_CHIPS * cols), ((0, 0), (0, 3 * D + LANES - N_CHIPS * cols)))[None]
            elif key in ("fox_w_o", "swa_w_o"):
                full = full.reshape(1, D, D)
            elif key.startswith("ffn_w_down"):
                full = full.reshape(1, Fh, D)
            self.views[key] = full
        return self.views[key]

    def carry(self, stage, last=()):
        todo = []
        token = self.before(stage, last) if stage in FORWARD else None
        if token is not None:
            todo.append(("order", [], Comm([token], [], {}, 1, lambda *refs: None, lambda *refs: None)))
        for kind, key, *chunk in PLAN.get(stage, ()):
            if kind == "swap":
                todo.append((kind, [key], _swap_comm([self.raw[key]])))
            elif kind == "scatter":
                todo.append((kind, [(key, *chunk)], _scatter_comm([self.part[key]], [self.landed.get(key)], [tuple(chunk)])))
        self.pending[stage] = todo
        return _merge([cm for _, _, cm in todo])

    def carried(self, stage, comm):
        for kind, keys, cm in self.pending.pop(stage):
            if kind == "swap":
                self.part[keys[0]] = _add_sibling(self.raw[keys[0]], cm.results[0], self.c, f"add_sibling_{keys[0]}")
            elif kind == "scatter":
                self.landed[keys[0][0]] = cm.results[0]

    def grad(self, key, g):
        S, D, FH, QH, KH, Fh = self.dm
        if key == "fox_w_in":
            cols = self.slots[key].shape[2]
            g = g[0][:, :N_CHIPS * cols].reshape(D, N_CHIPS, cols).transpose(1, 0, 2)
        elif key in ("fox_w_o", "swa_w_o"):
            g = g.reshape(N_CHIPS, D // N_CHIPS, D)
        elif key.startswith("ffn_w_down"):
            g = g.reshape(N_CHIPS, Fh // N_CHIPS, D)
        self.raw[key] = g

    def last_start(self, last, after):
        part = self.part[last]
        self.last = (last, _scatter_comm([part], [lax.empty(part.shape, part.dtype)], [(0, 1, 1)]))
        self.last_state, token = _split_start(self.last[1], "grads_last_start", after)
        return token

    def join_landed(self):
        keys = list(self.landed)
        join = _join_comm([_sum_chips(self.part[k], self.landed[k], self.chip, self.c, f"sum_chips_{k}") for k in keys])
        _run_comm(join, "grads_join")
        return dict(zip(keys, join.results))

    def last_finish(self, after):
        last, comm = self.last
        _split_wait(comm, self.last_state, after, "grads_last_wait")
        join = _join_comm([_sum_chips(self.part[last], comm.results[0], self.chip, self.c, f"sum_chips_{last}")])
        _run_comm(join, "grads_join_last")
        return join.results[0]


def _step(dm, a):
    S, D, FH, QH, KH, Fh = dm
    ix, iy, ic = lax.axis_index("x"), lax.axis_index("y"), lax.axis_index("c")
    chip = 2 * ix + iy
    dev = 2 * chip + ic
    F2c = a["ffn_w_up"].shape[2]
    NC = a["ada_w"].shape[2]

    names = ["fox_w_in", "fox_w_o", "swa_w_in", "swa_w_o", "ffn_w_up", "ffn_w_up", "ffn_w_down", "ffn_w_down"]
    layers = [0, 0, 0, 0, 0, 1, 0, 1]
    keys = ["fox_w_in", "fox_w_o", "swa_w_in", "swa_w_o", "ffn_w_up0", "ffn_w_up1", "ffn_w_down0", "ffn_w_down1"]
    cast = lambda t, after: _cast_bf16(a[names[t]], layers[t], chip, f"cast_{keys[t]}", after)
    pp = Exchanges(dm, {keys[t]: cast(t, ()) for t in (0, 1)}, chip, ic)

    e0 = _cond_rows(a["c"], a["ffn_conv_w"].reshape(DEPTH * 3, F2c), "silu_c", [pp.slots[k] for k in keys[:2]])
    g0 = _allgather_small(e0, "gather_cond").reshape(N_DEV, 16, e0.shape[1])
    cact = g0[:, 0, :D]
    conv_w = g0[0::2, 8:8 + DEPTH * 3, :F2c].transpose(1, 0, 2).reshape(DEPTH, 3, N_CHIPS * F2c)
    rows = _ada_fwd(cact, a["ada_w"], a["ada_b"], ic, chip, "ada_proj")
    g1 = _allgather_small(rows, "gather_mod").reshape(N_CHIPS, DEPTH, 8, NC)
    mod = lax.dynamic_index_in_dim(g1, dev, axis=2, keepdims=False).transpose(1, 0, 2).reshape(DEPTH, N_CHIPS * NC)

    token = pp.gather_start(keys[:2], "gather_fox", [mod])
    pp.slots.update({keys[t]: cast(t, (token,)) for t in range(2, len(keys))})
    h0 = _modulate(a["x"][0], mod[0:1, D:2 * D], mod[0:1, 0:D], "mod_in")
    pp.gather_finish([pp.slots[k] for k in keys[2:]] + [h0], "gather_fox")
    sp = {"fox_b_f": _pad_lanes(a["fox_b_f"]), "sinks": jnp.repeat(a["swa_sinks"].reshape(KH, QH // KH), WIN, axis=1)[:, :, None],
          "conv_w": [conv_w[i] for i in range(DEPTH)], "conv_b": [a["ffn_conv_b"][i:i + 1] for i in range(DEPTH)]}
    for nm in ("ln_mix_g", "ln_mix_b", "ln_ffn_g", "ln_ffn_b"):
        sp[nm] = [a[nm][i:i + 1] for i in range(DEPTH)]

    loss_cols, grad_x, gs, dmod = _local_step(dm, a["x"][0], a["loss_target"][0], a["positions"][0], mod, sp, pp, h0)
    loss = lax.psum(0.5 / D * jnp.sum(loss_cols), ("x", "y", "c"))
    out = {"loss": loss, "grad_x": grad_x[None]}

    pieces = [dmod.reshape(-1), gs["fox_b_f"].reshape(-1), _pad_lanes(gs["sinks"]).reshape(-1),
              jnp.stack(gs["conv_w"]).reshape(-1), jnp.stack(gs["conv_b"]).reshape(-1)]
    pieces += [jnp.stack(gs[nm]).reshape(-1) for nm in ("ln_mix_g", "ln_mix_b", "ln_ffn_g", "ln_ffn_b")]
    sizes = [p.shape[0] for p in pieces]
    packed = _pad_rows(jnp.concatenate(pieces))
    allp = _allgather_small(packed, "gather_small").reshape(N_DEV, packed.shape[0], LANES)
    tot = _sum_devices(allp, "sum_small").reshape(-1)
    offs = [sum(sizes[:k]) for k in range(len(sizes))]
    take = lambda k: tot[offs[k]:offs[k] + sizes[k]]
    g_small = {"ada_b": take(0).reshape(DEPTH, -1), "fox_b_f": take(1)[:FH].reshape(1, FH), "swa_sinks": take(2)[:QH].reshape(1, QH),
               "ffn_conv_w": lax.dynamic_slice_in_dim(take(3).reshape(DEPTH, 3, N_CHIPS * F2c), chip * F2c, F2c, axis=2),
               "ffn_conv_b": take(4).reshape(DEPTH, -1)}
    for k, nm in enumerate(("ln_mix_g", "ln_mix_b", "ln_ffn_g", "ln_ffn_b")):
        g_small[nm] = take(5 + k).reshape(DEPTH, D)
    small = list(g_small)
    pack = lambda pre: _pad_rows(jnp.concatenate([(a[pre + nm] if pre else a[nm]).reshape(-1) for nm in small]))
    gp = _pad_rows(jnp.concatenate([g_small[nm].reshape(-1) for nm in small]))
    ds_, ms_, vs_ = _adamw_small(pack(""), gp, pack("m_"), pack("v_"), "adamw_small")
    off = 0
    for nm in small:
        n_el = a[nm].size
        out["grad_" + nm] = g_small[nm]
        for pre, arr in (("delta_", ds_), ("new_m_", ms_), ("new_v_", vs_)):
            out[pre + nm] = arr.reshape(-1)[off:off + n_el].reshape(a[nm].shape)
        off += n_el

    dmod_all = allp.reshape(N_DEV, -1)[:, :DEPTH * N_CHIPS * NC].reshape(N_DEV, DEPTH, N_CHIPS * NC)
    dmod_mine = lax.dynamic_slice_in_dim(dmod_all, chip * NC, NC, axis=2).transpose(1, 0, 2)

    grads = pp.join_landed()
    token = pp.last_start("fox_w_in", [ds_, dmod_mine] + list(grads.values()))
    ada = _ada_grad_adamw(cact.T, dmod_mine + token[0, 0], a["ada_w"], a["m_ada_w"], a["v_ada_w"], "ada_grad")
    for pre, arr in zip(("grad_", "delta_", "new_m_", "new_v_"), ada):
        out[pre + "ada_w"] = arr
    upd = {}
    for k, nm, l in zip(keys[1:], names[1:], layers[1:]):
        upd[nm] = _adamw(a[nm], grads[k], a["m_" + nm], a["v_" + nm], l, upd.get(nm, ()), f"adamw_{k}", after=(token,))
    g_last = pp.last_finish([ada[1]] + [res[1] for res in upd.values()])
    tview = lambda t: jnp.swapaxes(t, 1, 2)
    res = _adamw(tview(a["fox_w_in"]), g_last.T, tview(a["m_fox_w_in"]), tview(a["v_fox_w_in"]), 0, (), "adamw_fox_w_in", by_cols=True)
    upd["fox_w_in"] = [tview(r) for r in res]
    for nm, res in upd.items():
        for pre, arr in zip(("grad_", "delta_", "new_m_", "new_v_"), res):
            out[pre + nm] = arr
    return out


_WEIGHTS = ["fox_w_in", "fox_b_f", "fox_w_o", "swa_w_in", "swa_sinks", "swa_w_o", "ada_w", "ada_b", "ffn_w_up", "ffn_conv_w",
            "ffn_conv_b", "ffn_w_down", "ln_mix_g", "ln_mix_b", "ln_ffn_g", "ln_ffn_b"]
_INPUTS = (["x", "c", "positions"] + _WEIGHTS + ["loss_target"] + ["m_" + w for w in _WEIGHTS] + ["v_" + w for w in _WEIGHTS])


def kernel(x, c, positions, fox_w_in, fox_b_f, fox_w_o, swa_w_in, swa_sinks, swa_w_o, ada_w, ada_b, ffn_w_up, ffn_conv_w, ffn_conv_b, ffn_w_down, ln_mix_g, ln_mix_b, ln_ffn_g, ln_ffn_b, loss_target, m_fox_w_in, m_fox_b_f, m_fox_w_o, m_swa_w_in, m_swa_sinks, m_swa_w_o, m_ada_w, m_ada_b, m_ffn_w_up, m_ffn_conv_w, m_ffn_conv_b, m_ffn_w_down, m_ln_mix_g, m_ln_mix_b, m_ln_ffn_g, m_ln_ffn_b, v_fox_w_in, v_fox_b_f, v_fox_w_o, v_swa_w_in, v_swa_sinks, v_swa_w_o, v_ada_w, v_ada_b, v_ffn_w_up, v_ffn_conv_w, v_ffn_conv_b, v_ffn_w_down, v_ln_mix_g, v_ln_mix_b, v_ln_ffn_g, v_ln_ffn_b):
    args = (x, c, positions, fox_w_in, fox_b_f, fox_w_o, swa_w_in, swa_sinks, swa_w_o, ada_w, ada_b, ffn_w_up, ffn_conv_w, ffn_conv_b, ffn_w_down, ln_mix_g, ln_mix_b, ln_ffn_g, ln_ffn_b, loss_target, m_fox_w_in, m_fox_b_f, m_fox_w_o, m_swa_w_in, m_swa_sinks, m_swa_w_o, m_ada_w, m_ada_b, m_ffn_w_up, m_ffn_conv_w, m_ffn_conv_b, m_ffn_w_down, m_ln_mix_g, m_ln_mix_b, m_ln_ffn_g, m_ln_ffn_b, v_fox_w_in, v_fox_b_f, v_fox_w_o, v_swa_w_in, v_swa_sinks, v_swa_w_o, v_ada_w, v_ada_b, v_ffn_w_up, v_ffn_conv_w, v_ffn_conv_b, v_ffn_w_down, v_ln_mix_g, v_ln_mix_b, v_ln_ffn_g, v_ln_ffn_b)
    out = _step(PROD, dict(zip(_INPUTS, args)))
    order = ["loss", "grad_x"] + [p + w for p in ("grad_", "delta_", "new_m_", "new_v_") for w in _WEIGHTS]
    return tuple(out[k] for k in order)
```

```python
import functools
from typing import NamedTuple

import jax
import jax.numpy as jnp
from jax import lax
from jax.experimental import pallas as pl
from jax.experimental.pallas import tpu as pltpu

F32 = jnp.float32
BF16 = jnp.bfloat16
MESH = pl.DeviceIdType.MESH
HIGHEST = lax.Precision.HIGHEST

N_CHIPS = 4
N_DEV = 8
LANES = 128
VMEM_LIMIT = 56 * 1024 * 1024

DEPTH = 2
DEEPNORM_ALPHA = (2.0 * DEPTH) ** 0.25
LN_EPS = 1e-5
ROPE_THETA = 500000.0
ADAM_LR, ADAM_B1, ADAM_B2, ADAM_EPS, ADAM_WD, ADAM_STEP = 0.001, 0.9, 0.999, 1e-08, 0.01, 10
NEG = -1e30


class Dims(NamedTuple):
    S: int
    D: int
    FH: int
    QH: int
    KH: int
    F: int


PROD = Dims(S=2048, D=2048, FH=16, QH=32, KH=4, F=5632)
FDH = 128
SDH = 64
WIN = 128
ROPE_DIM = 16
FOX_TQ = 256


def _params(sem=None, vmem=VMEM_LIMIT):
    return pltpu.CompilerParams(dimension_semantics=sem, vmem_limit_bytes=vmem)


def _tile(n, pref, unit=LANES):
    if n <= pref:
        return n
    t = (pref // unit) * unit
    while t > 0:
        if n % t == 0:
            return t
        t -= unit
    return n


class Comm:
    def __init__(self, args, out_shapes, aliases, n_sem, start, finish, members=()):
        self.args, self.out_shapes, self.aliases, self.n_sem = list(args), list(out_shapes), dict(aliases), n_sem
        self.start, self.finish = start, finish
        self.members = members
        self.results = None

    def set_results(self, res):
        self.results = list(res)
        for cm, o0 in self.members:
            cm.set_results(self.results[o0:o0 + len(cm.out_shapes)])


class _SemView:
    def __init__(self, sems, first):
        self.sems, self.first = sems, first

    @property
    def at(self):
        return self

    def __getitem__(self, k):
        return self.sems.at[self.first + k]


def _merge(comms):
    comms = [cm for cm in comms if cm is not None]
    if len(comms) < 2:
        return comms[0] if comms else None
    args, shapes, aliases, spans, n_sem = [], [], {}, [], 0
    for cm in comms:
        spans.append((len(args), len(shapes), n_sem))
        aliases.update({len(args) + a: len(shapes) + o for a, o in cm.aliases.items()})
        args += cm.args
        shapes += cm.out_shapes
        n_sem += cm.n_sem

    def each(step):
        def run(ar, ou, send, recv):
            for cm, (a0, o0, s0) in zip(comms, spans):
                getattr(cm, step)(ar[a0:a0 + len(cm.args)], ou[o0:o0 + len(cm.out_shapes)], _SemView(send, s0), _SemView(recv, s0))
        return run

    return Comm(args, shapes, aliases, n_sem, each("start"), each("finish"), [(cm, o0) for cm, (_, o0, _) in zip(comms, spans)])


def _place():
    x, y, c = lax.axis_index("x"), lax.axis_index("y"), lax.axis_index("c")
    chips = [(1 - x, y), (x, 1 - y), (1 - x, 1 - y)]
    return x, y, c, chips


def _remote(src, dst, send, recv, to):
    return pltpu.make_async_remote_copy(src_ref=src, dst_ref=dst, send_sem=send, recv_sem=recv, device_id=to, device_id_type=MESH)


def _any_specs(n):
    return [pl.BlockSpec(memory_space=pl.ANY)] * n


def _call(body, *, name, grid, in_specs, out_specs, out_shape, args, sem, scratch_shapes=(), aliases=None, comm=None):
    in_specs, out_specs, out_shape, scratch_shapes = list(in_specs), list(out_specs), list(out_shape), list(scratch_shapes)
    aliases = dict(aliases or {})
    if comm is None:
        return pl.pallas_call(body, name=name, grid=grid, in_specs=in_specs, out_specs=out_specs, out_shape=out_shape,
                              scratch_shapes=scratch_shapes, input_output_aliases=aliases, compiler_params=_params(sem))(*args)
    n_in, n_out, nc_in, nc_out, n_scr = len(in_specs), len(out_specs), len(comm.args), len(comm.out_shapes), len(scratch_shapes)

    def wrapped(*refs):
        ins, refs = refs[:n_in], refs[n_in:]
        cin, refs = refs[:nc_in], refs[nc_in:]
        outs, refs = refs[:n_out], refs[n_out:]
        cout, refs = refs[:nc_out], refs[nc_out:]
        scratch, (send, recv) = refs[:n_scr], refs[n_scr:]
        ids = [pl.program_id(k) for k in range(len(grid))]
        first = functools.reduce(jnp.logical_and, [i == 0 for i in ids])
        last = functools.reduce(jnp.logical_and, [i == g - 1 for i, g in zip(ids, grid)])

        @pl.when(first)
        def _():
            comm.start(cin, cout, send, recv)

        body(*ins, *outs, *scratch)

        @pl.when(last)
        def _():
            comm.finish(cin, cout, send, recv)

    res = pl.pallas_call(
        wrapped, name=name, grid=grid, in_specs=in_specs + _any_specs(nc_in), out_specs=out_specs + _any_specs(nc_out),
        out_shape=out_shape + comm.out_shapes,
        scratch_shapes=scratch_shapes + [pltpu.SemaphoreType.DMA((comm.n_sem,)), pltpu.SemaphoreType.DMA((comm.n_sem,))],
        input_output_aliases={**aliases, **{n_in + a: n_out + o for a, o in comm.aliases.items()}},
        compiler_params=_params(("arbitrary",) * len(grid)),
    )(*args, *comm.args)
    comm.set_results(res[n_out:])
    return list(res[:n_out])


def _run_comm(comm, name):
    nc_in, nc_out = len(comm.args), len(comm.out_shapes)

    def body(*refs):
        cin, cout, (send, recv) = refs[:nc_in], refs[nc_in:nc_in + nc_out], refs[nc_in + nc_out:]
        comm.start(cin, cout, send, recv)
        comm.finish(cin, cout, send, recv)

    res = pl.pallas_call(
        body, name=name, in_specs=_any_specs(nc_in), out_specs=_any_specs(nc_out), out_shape=comm.out_shapes,
        scratch_shapes=[pltpu.SemaphoreType.DMA((comm.n_sem,)), pltpu.SemaphoreType.DMA((comm.n_sem,))],
        input_output_aliases=comm.aliases,
    )(*comm.args)
    comm.set_results(res)


_HBM = pl.BlockSpec(memory_space=pltpu.HBM)
_SEM = pl.BlockSpec(memory_space=pltpu.SEMAPHORE)
_EFFECT = pltpu.SideEffectType.DATAFLOW_SIDE_EFFECTING


def _split_start(comm, name, after=()):
    n = len(comm.args)
    back = {o: a for a, o in comm.aliases.items()}
    assert len(back) == len(comm.out_shapes)

    n_after = len(after)

    def body(*refs):
        refs = refs[n + n_after:]
        send, recv, thru, token = refs[0], refs[1], refs[2:n + 2], refs[n + 2]
        comm.start(thru, [thru[back[o]] for o in range(len(back))], send, recv)
        token[...] = jnp.zeros_like(token)

    res = pl.pallas_call(
        body, name=name,
        out_shape=(pltpu.SemaphoreType.DMA((comm.n_sem,)), pltpu.SemaphoreType.DMA((comm.n_sem,)),
                   *[pltpu.HBM(a.shape, a.dtype) for a in comm.args], jax.ShapeDtypeStruct((8, LANES), F32)),
        in_specs=[_HBM] * n + _any_specs(n_after), out_specs=(_SEM, _SEM, *[_HBM] * n, pl.BlockSpec(memory_space=pltpu.VMEM)),
        input_output_aliases={i: 2 + i for i in range(n)},
        compiler_params=pltpu.CompilerParams(has_side_effects=_EFFECT),
    )(*[pltpu.with_memory_space_constraint(a, pltpu.HBM) for a in comm.args], *after)
    return (res[0], res[1], list(res[2:2 + n])), res[2 + n]


def _split_wait(comm, state, after, name):
    send, recv, thru = state
    n, n_after = len(thru), len(after)
    back = {o: a for a, o in comm.aliases.items()}

    def body(*refs):
        ins, send_ref, recv_ref = refs[:n], refs[n], refs[n + 1]
        comm.finish(ins, [ins[back[o]] for o in range(len(back))], send_ref, recv_ref)

    res = pl.pallas_call(
        body, name=name, out_shape=tuple(pltpu.HBM(a.shape, a.dtype) for a in thru),
        in_specs=[_HBM] * n + [_SEM, _SEM] + _any_specs(n_after), out_specs=[_HBM] * n,
        input_output_aliases={i: i for i in range(n)}, compiler_params=pltpu.CompilerParams(has_side_effects=_EFFECT),
    )(*thru, send, recv, *after)
    comm.set_results([res[back[o]] for o in range(len(back))])


def _forward_comm(slots):
    n = len(slots)

    def rows(t, who):
        rh = slots[t].shape[1] // 2
        return pl.ds(who * rh, rh)

    def copy(outs, send, recv, t, j, chip, who):
        x, y, c, _ = _place()
        blk = outs[t].at[2 * chip[0] + chip[1], rows(t, who)]
        return _remote(blk, blk, send.at[3 * t + j], recv.at[3 * t + j], (x, y, 1 - c))

    def start(args, outs, send, recv):
        _, _, c, chips = _place()
        for t in range(n):
            for j, chip in enumerate(chips):
                copy(outs, send, recv, t, j, chip, c).start()

    def finish(args, outs, send, recv):
        _, _, c, chips = _place()
        for t in range(n):
            for j, chip in enumerate(chips):
                copy(outs, send, recv, t, j, chip, 1 - c).wait_recv()
        for t in range(n):
            for j, chip in enumerate(chips):
                copy(outs, send, recv, t, j, chip, c).wait_send()

    shapes = [jax.ShapeDtypeStruct(w.shape, w.dtype) for w in slots]
    return Comm(slots, shapes, {t: t for t in range(n)}, 3 * n, start, finish)


def _gather_comm(slots):
    n = len(slots)

    def rows(t, who):
        rh = slots[t].shape[1] // 2
        return pl.ds(who * rh, rh)

    def start(args, outs, send, recv):
        x, y, c, chips = _place()
        for t in range(n):
            mine = outs[t].at[2 * x + y, rows(t, c)]
            for j, chip in enumerate(chips):
                _remote(mine, mine, send.at[3 * t + j], recv.at[3 * t + j], (*chip, c)).start()

    def finish(args, outs, send, recv):
        x, y, c, chips = _place()
        for t in range(n):
            for j, chip in enumerate(chips):
                blk = outs[t].at[2 * chip[0] + chip[1], rows(t, c)]
                _remote(blk, blk, send.at[3 * t + j], recv.at[3 * t + j], (*chip, c)).wait_recv()
        for t in range(n):
            mine = outs[t].at[2 * x + y, rows(t, c)]
            for j, chip in enumerate(chips):
                _remote(mine, mine, send.at[3 * t + j], recv.at[3 * t + j], (*chip, c)).wait_send()

    shapes = [jax.ShapeDtypeStruct(w.shape, w.dtype) for w in slots]
    return Comm(slots, shapes, {t: t for t in range(n)}, 3 * n, start, finish)


def _scatter_comm(parts, landed, chunks):
    n = len(parts)
    prev = [t for t in range(n) if landed[t] is not None]

    def rows(t):
        lo, hi, nch = chunks[t]
        rc = parts[t].shape[1] // nch
        return pl.ds(lo * rc, (hi - lo) * rc)

    def start(args, outs, send, recv):
        x, y, c, chips = _place()
        s = 2 * x + y
        for t in range(n):
            for j, chip in enumerate(chips):
                _remote(args[t].at[2 * chip[0] + chip[1], rows(t)], outs[t].at[s, rows(t)],
                        send.at[3 * t + j], recv.at[3 * t + j], (*chip, c)).start()

    def finish(args, outs, send, recv):
        x, y, c, chips = _place()
        for t in range(n):
            for j, chip in enumerate(chips):
                blk = outs[t].at[2 * chip[0] + chip[1], rows(t)]
                _remote(blk, blk, send.at[3 * t + j], recv.at[3 * t + j], (*chip, c)).wait_recv()
        for t in range(n):
            for j, chip in enumerate(chips):
                src = args[t].at[2 * chip[0] + chip[1], rows(t)]
                _remote(src, src, send.at[3 * t + j], recv.at[3 * t + j], (*chip, c)).wait_send()

    shapes = [jax.ShapeDtypeStruct(p.shape, p.dtype) for p in parts]
    return Comm(list(parts) + [landed[t] for t in prev], shapes, {n + i: t for i, t in enumerate(prev)}, 3 * n, start, finish)


def _swap_comm(gs):
    n = len(gs)

    def copy(args, outs, send, recv, t):
        _, _, c, _ = _place()
        rh = gs[t].shape[1] // 2
        x, y = lax.axis_index("x"), lax.axis_index("y")
        return _remote(args[t].at[:, pl.ds((1 - c) * rh, rh), :], outs[t], send.at[t], recv.at[t], (x, y, 1 - c))

    def start(args, outs, send, recv):
        for t in range(n):
            copy(args, outs, send, recv, t).start()

    def finish(args, outs, send, recv):
        for t in range(n):
            copy(args, outs, send, recv, t).wait()

    shapes = [jax.ShapeDtypeStruct((g.shape[0], g.shape[1] // 2, g.shape[2]), g.dtype) for g in gs]
    return Comm(gs, shapes, {}, n, start, finish)


def _join_comm(gs):
    n = len(gs)

    def half(outs, t, who):
        rh = gs[t].shape[0] // 2
        return outs[t].at[pl.ds(who * rh, rh), :]

    def start(args, outs, send, recv):
        x, y, c, _ = _place()
        for t in range(n):
            _remote(half(outs, t, c), half(outs, t, c), send.at[t], recv.at[t], (x, y, 1 - c)).start()

    def finish(args, outs, send, recv):
        x, y, c, _ = _place()
        for t in range(n):
            _remote(half(outs, t, 1 - c), half(outs, t, 1 - c), send.at[t], recv.at[t], (x, y, 1 - c)).wait_recv()
        for t in range(n):
            _remote(half(outs, t, c), half(outs, t, c), send.at[t], recv.at[t], (x, y, 1 - c)).wait_send()

    shapes = [jax.ShapeDtypeStruct(g.shape, g.dtype) for g in gs]
    return Comm(gs, shapes, {t: t for t in range(n)}, n, start, finish)


_DN = {"nn": (((1,), (0,)), ((), ())), "nt": (((1,), (1,)), ((), ())), "tn": (((0,), (0,)), ((), ()))}


def _mm(a, b, *, mode, out_dtype, name, out_groups=1, tm=1024, tn=1024, tk=2048, comm=None):
    ga, ra, ca = a.shape
    gb, rb, cb = b.shape
    if mode == "nn":
        M, K, N = ra, ga * ca, gb * cb
        assert rb == K and ga == 1 or (rb == K)
    elif mode == "nt":
        M, K, N = ra, ga * ca, rb
        assert gb * cb == K
    else:
        K, M, N = ra, ga * ca, gb * cb
        assert rb == K
    go = out_groups
    if mode == "nn":
        tk = _tile(ca, tk); assert rb % tk == 0 and (ga == 1 or True)
        tn = _tile(min(cb, N // go), tn); tm = _tile(M, tm, 8)
    elif mode == "nt":
        tk = _tile(ca, tk); tk = _tile(cb, tk) if cb % tk else tk; assert ca % tk == 0 and cb % tk == 0
        tn = _tile(N // go, tn); tm = _tile(M, tm, 8)
    else:
        tk = _tile(K, tk, 8); tm = _tile(ca, tm); tn = _tile(min(cb, N // go), tn)
    assert (N // go) % tn == 0 and M % tm == 0 and K % tk == 0, (name, M, N, K, tm, tn, tk)
    nk = K // tk
    kpa = max(ca // tk, 1)
    kpb = max(cb // tk, 1)
    npb = max(cb // tn, 1)
    npo = (N // go) // tn
    mpa = max(ca // tm, 1)

    if mode == "nn":
        a_spec = pl.BlockSpec((1, tm, tk), lambda j, i, k: (k // kpa, i, k % kpa))
        b_spec = pl.BlockSpec((1, tk, tn), lambda j, i, k: (j // npb, k, j % npb))
    elif mode == "nt":
        a_spec = pl.BlockSpec((1, tm, tk), lambda j, i, k: (k // kpa, i, k % kpa))
        b_spec = pl.BlockSpec((1, tn, tk), lambda j, i, k: (k // kpb, j, k % kpb))
    else:
        a_spec = pl.BlockSpec((1, tk, tm), lambda j, i, k: (i // mpa, k, i % mpa))
        b_spec = pl.BlockSpec((1, tk, tn), lambda j, i, k: (j // npb, k, j % npb))
    o_spec = pl.BlockSpec((1, tm, tn), lambda j, i, k: (j // npo, i, j % npo))
    dn = _DN[mode]

    def body(a_ref, b_ref, o_ref, *acc):
        p = lax.dot_general(a_ref[0], b_ref[0], dn, preferred_element_type=F32)
        if nk == 1:
            o_ref[0] = p.astype(out_dtype)
        else:
            k = pl.program_id(2)

            @pl.when(k == 0)
            def _():
                acc[0][...] = p

            @pl.when(k > 0)
            def _():
                acc[0][...] += p

            @pl.when(k == nk - 1)
            def _():
                o_ref[0] = acc[0][...].astype(out_dtype)

    return _call(
        body, name=name, grid=(N // tn, M // tm, nk), in_specs=[a_spec, b_spec], out_specs=[o_spec],
        out_shape=[jax.ShapeDtypeStruct((go, M, N // go), out_dtype)],
        scratch_shapes=[pltpu.VMEM((tm, tn), F32)] if nk > 1 else [],
        sem=("parallel", "parallel", "arbitrary"), args=(a, b), comm=comm)[0]


def _rows(tr, d):
    return pl.BlockSpec((tr, d), lambda i: (i, 0))


def _vec(d):
    return pl.BlockSpec((1, d), lambda i: (0, 0))


def _modulate(x, sc, sh, name):
    S, D = x.shape
    tr = min(256, S)

    def body(x_ref, sc_ref, sh_ref, h_ref):
        h_ref[...] = (x_ref[...] * (1.0 + sc_ref[...]) + sh_ref[...]).astype(BF16)

    return pl.pallas_call(
        body, name=name, grid=(S // tr,), in_specs=[_rows(tr, D), _vec(D), _vec(D)], out_specs=_rows(tr, D),
        out_shape=jax.ShapeDtypeStruct((S, D), BF16), compiler_params=_params(("parallel",)),
    )(x, sc, sh)


def _ln_fwd(x, y, gate, gamma, beta, sc, sh, name, comm=None):
    S, D = x.shape
    tr = min(256, S)
    emit_h = sc is not None

    def body(*refs):
        if emit_h:
            x_ref, y_ref, g_ref, ga_ref, be_ref, sc_ref, sh_ref, xo_ref, xh_ref, rs_ref, h_ref = refs
        else:
            x_ref, y_ref, g_ref, ga_ref, be_ref, xo_ref, xh_ref, rs_ref = refs
        z = DEEPNORM_ALPHA * x_ref[...] + (1.0 + g_ref[...]) * y_ref[...]
        mu = jnp.mean(z, axis=-1, keepdims=True)
        zc = z - mu
        var = jnp.mean(zc * zc, axis=-1, keepdims=True)
        rstd = lax.rsqrt(var + LN_EPS)
        xh = zc * rstd
        xo = xh * ga_ref[...] + be_ref[...]
        xo_ref[...] = xo
        xh_ref[...] = xh
        rs_ref[...] = rstd
        if emit_h:
            h_ref[...] = (xo * (1.0 + sc_ref[...]) + sh_ref[...]).astype(BF16)

    ins = [x, y, gate, gamma, beta] + ([sc, sh] if emit_h else [])
    in_specs = [_rows(tr, D), _rows(tr, D)] + [_vec(D)] * (len(ins) - 2)
    out_shape = [jax.ShapeDtypeStruct((S, D), F32), jax.ShapeDtypeStruct((S, D), F32), jax.ShapeDtypeStruct((S, 1), F32)]
    out_specs = [_rows(tr, D), _rows(tr, D), _rows(tr, 1)]
    if emit_h:
        out_shape.append(jax.ShapeDtypeStruct((S, D), BF16))
        out_specs.append(_rows(tr, D))
    return _call(body, name=name, grid=(S // tr,), in_specs=in_specs, out_specs=out_specs, out_shape=out_shape,
                 sem=("parallel",), args=ins, comm=comm)


def _loss_head(xf, tgt, name):
    S, D = xf.shape
    tr = min(256, S)

    def body(x_ref, t_ref, dx_ref, l_ref):
        e = x_ref[...] - t_ref[...]
        dx_ref[...] = e * (1.0 / D)

        @pl.when(pl.program_id(0) == 0)
        def _():
            l_ref[...] = jnp.zeros_like(l_ref)

        l_ref[...] += jnp.sum(e * e, axis=0, keepdims=True)

    return pl.pallas_call(
        body, name=name, grid=(S // tr,), in_specs=[_rows(tr, D), _rows(tr, D)],
        out_specs=[_rows(tr, D), _vec(D)],
        out_shape=[jax.ShapeDtypeStruct((S, D), F32), jax.ShapeDtypeStruct((1, D), F32)],
        compiler_params=_params(("arbitrary",)),
    )(xf, tgt)


def _ln_bwd(dxo, xh, rstd, gamma, y, gate, name, pre=None):
    S, D = dxo.shape
    tr = min(256, S)
    n_pre = 0 if pre is None else 3

    def body(dx_ref, xh_ref, rs_ref, ga_ref, y_ref, g_ref, *rest):
        dres_ref, dy_ref, dga_ref, dbe_ref, dg_ref = rest[n_pre:n_pre + 5]
        first = pl.program_id(0) == 0
        dxo_ = dx_ref[...]
        xh_ = xh_ref[...]
        if pre is not None:
            dh_ref, sc_ref, be_ref = rest[:3]
            dsc_ref, dsh_ref = rest[n_pre + 5:]
            dh_ = dh_ref[...]
            dxo_ = dxo_ + dh_ * (1.0 + sc_ref[...])

            @pl.when(first)
            def _():
                dsc_ref[...] = jnp.zeros_like(dsc_ref)
                dsh_ref[...] = jnp.zeros_like(dsh_ref)

            dsc_ref[...] += jnp.sum(dh_ * (xh_ * ga_ref[...] + be_ref[...]), axis=0, keepdims=True)
            dsh_ref[...] += jnp.sum(dh_, axis=0, keepdims=True)
        dxh = dxo_ * ga_ref[...]
        m1 = jnp.mean(dxh, axis=-1, keepdims=True)
        m2 = jnp.mean(dxh * xh_, axis=-1, keepdims=True)
        dz = rs_ref[...] * (dxh - m1 - xh_ * m2)
        dres_ref[...] = DEEPNORM_ALPHA * dz
        dy_ref[...] = ((1.0 + g_ref[...]) * dz).astype(BF16)

        @pl.when(first)
        def _():
            dga_ref[...] = jnp.zeros_like(dga_ref)
            dbe_ref[...] = jnp.zeros_like(dbe_ref)
            dg_ref[...] = jnp.zeros_like(dg_ref)

        dga_ref[...] += jnp.sum(dxo_ * xh_, axis=0, keepdims=True)
        dbe_ref[...] += jnp.sum(dxo_, axis=0, keepdims=True)
        dg_ref[...] += jnp.sum(dz * y_ref[...], axis=0, keepdims=True)

    extra_in = [] if pre is None else [_rows(tr, D), _vec(D), _vec(D)]
    return pl.pallas_call(
        body, name=name, grid=(S // tr,),
        in_specs=[_rows(tr, D), _rows(tr, D), _rows(tr, 1), _vec(D), _rows(tr, D), _vec(D)] + extra_in,
        out_specs=[_rows(tr, D), _rows(tr, D)] + [_vec(D)] * (3 + (0 if pre is None else 2)),
        out_shape=[jax.ShapeDtypeStruct((S, D), F32), jax.ShapeDtypeStruct((S, D), BF16)]
        + [jax.ShapeDtypeStruct((1, D), F32)] * (3 + (0 if pre is None else 2)),
        compiler_params=_params(("arbitrary",)),
    )(dxo, xh, rstd, gamma, y, gate, *(pre or ()))


def _mod_bwd(dh, x, sc, dres, name):
    S, D = x.shape
    tr = min(256, S)

    def body(dh_ref, x_ref, sc_ref, dr_ref, dx_ref, dsc_ref, dsh_ref):
        dh_ = dh_ref[...]
        dx_ref[...] = dr_ref[...] + dh_ * (1.0 + sc_ref[...])

        @pl.when(pl.program_id(0) == 0)
        def _():
            dsc_ref[...] = jnp.zeros_like(dsc_ref)
            dsh_ref[...] = jnp.zeros_like(dsh_ref)

        dsc_ref[...] += jnp.sum(dh_ * x_ref[...], axis=0, keepdims=True)
        dsh_ref[...] += jnp.sum(dh_, axis=0, keepdims=True)

    return pl.pallas_call(
        body, name=name, grid=(S // tr,),
        in_specs=[_rows(tr, D), _rows(tr, D), _vec(D), _rows(tr, D)],
        out_specs=[_rows(tr, D), _vec(D), _vec(D)],
        out_shape=[jax.ShapeDtypeStruct((S, D), F32), jax.ShapeDtypeStruct((1, D), F32), jax.ShapeDtypeStruct((1, D), F32)],
        compiler_params=_params(("arbitrary",)),
    )(dh, x, sc, dres)


def _log_sigmoid(z):
    return jnp.minimum(z, 0.0) - jnp.log(1.0 + jnp.exp(-jnp.abs(z)))


def _fox_gate_fwd(proj, b_f, n_heads, name):
    S, PW = proj.shape
    blk = min(256, S)
    last = PW // LANES - 1

    def body(fl_ref, b_ref, cum_ref):
        r = lax.broadcasted_iota(jnp.int32, (blk, blk), 0)
        c = lax.broadcasted_iota(jnp.int32, (blk, blk), 1)
        tril = (c <= r).astype(F32)
        carry = jnp.zeros((1, LANES), F32)
        for i in range(S // blk):
            lf = _log_sigmoid(fl_ref[i * blk:(i + 1) * blk, :] + b_ref[...])
            cum_ref[i * blk:(i + 1) * blk, :] = jnp.dot(tril, lf, preferred_element_type=F32, precision=HIGHEST) + carry
            carry = carry + jnp.sum(lf, axis=0, keepdims=True)

    return pl.pallas_call(
        body, name=name, grid=(1,),
        in_specs=[pl.BlockSpec((S, LANES), lambda i: (0, last)), pl.BlockSpec((1, LANES), lambda i: (0, 0))],
        out_specs=pl.BlockSpec((S, LANES), lambda i: (0, 0)),
        out_shape=jax.ShapeDtypeStruct((S, LANES), F32), compiler_params=_params(("arbitrary",)),
    )(proj, b_f)


def _fox_gate_bwd(dcum, proj, b_f, n_heads, name):
    S, PW = proj.shape
    blk = min(256, S)
    last = PW // LANES - 1
    nb = S // blk

    def body(dc_ref, fl_ref, b_ref, dfl_ref, db_ref):
        r = lax.broadcasted_iota(jnp.int32, (blk, blk), 0)
        c = lax.broadcasted_iota(jnp.int32, (blk, blk), 1)
        triu = (c >= r).astype(F32)
        lane = lax.broadcasted_iota(jnp.int32, (blk, LANES), 1)
        carry = jnp.zeros((1, LANES), F32)
        dbs = jnp.zeros((1, LANES), F32)
        for i in reversed(range(nb)):
            dc = dc_ref[i * blk:(i + 1) * blk, :]
            dlf = jnp.dot(triu, dc, preferred_element_type=F32, precision=HIGHEST) + carry
            carry = carry + jnp.sum(dc, axis=0, keepdims=True)
            z = fl_ref[i * blk:(i + 1) * blk, :] + b_ref[...]
            e = jnp.exp(-jnp.abs(z))
            sig_neg = jnp.where(z >= 0, e / (1.0 + e), 1.0 / (1.0 + e))
            dfl = jnp.where(lane < n_heads, dlf * sig_neg, 0.0)
            dfl_ref[i * blk:(i + 1) * blk, :] = dfl.astype(BF16)
            dbs = dbs + jnp.sum(dfl, axis=0, keepdims=True)
        db_ref[...] = dbs

    return pl.pallas_call(
        body, name=name, grid=(1,),
        in_specs=[pl.BlockSpec((S, LANES), lambda i: (0, 0)), pl.BlockSpec((S, LANES), lambda i: (0, last)),
                  pl.BlockSpec((1, LANES), lambda i: (0, 0))],
        out_specs=[pl.BlockSpec((S, LANES), lambda i: (0, 0)), pl.BlockSpec((1, LANES), lambda i: (0, 0))],
        out_shape=[jax.ShapeDtypeStruct((S, LANES), BF16), jax.ShapeDtypeStruct((1, LANES), F32)],
        compiler_params=_params(("arbitrary",)),
    )(dcum, proj, b_f)


def _fox_scores(q_ref, kb_ref, cq_ref, ck_ref, qi, tq, scale):
    kk = (qi + 1) * tq
    rows = slice(qi * tq, (qi + 1) * tq)
    qb = q_ref[rows, :].astype(BF16)
    s = lax.dot_general(qb, kb_ref[0:kk, :], _DN["nt"], preferred_element_type=F32) * scale
    s = s + (cq_ref[0, rows, :] - ck_ref[0, :, 0:kk])
    r = lax.broadcasted_iota(jnp.int32, (tq, kk), 0) + qi * tq
    c = lax.broadcasted_iota(jnp.int32, (tq, kk), 1)
    mask = c <= r
    return jnp.where(mask, s, NEG), mask, qb


def _fox_fwd(proj, cq, ck, n_heads, name, comm=None):
    S = proj.shape[0]
    H = n_heads
    tq = min(FOX_TQ, S)
    nq = S // tq
    scale = FDH ** -0.5

    def body(q_ref, k_ref, v_ref, cq_ref, ck_ref, o_ref, lse_ref, kb_ref, vb_ref):
        kb_ref[...] = k_ref[...].astype(BF16)
        vb_ref[...] = v_ref[...].astype(BF16)
        for qi in range(nq):
            kk = (qi + 1) * tq
            rows = slice(qi * tq, (qi + 1) * tq)
            s, _, _ = _fox_scores(q_ref, kb_ref, cq_ref, ck_ref, qi, tq, scale)
            m = jnp.max(s, axis=-1, keepdims=True)
            p = jnp.exp(s - m)
            l = jnp.sum(p, axis=-1, keepdims=True)
            p = p * (1.0 / l)
            o_ref[rows, :] = jnp.dot(p.astype(BF16), vb_ref[0:kk, :], preferred_element_type=F32).astype(BF16)
            lse_ref[0, rows, :] = m + jnp.log(l)

    col = lambda off: pl.BlockSpec((S, FDH), lambda h: (0, h + off))
    stat_c = pl.BlockSpec((1, S, 1), lambda h: (h, 0, 0))
    stat_r = pl.BlockSpec((1, 1, S), lambda h: (h, 0, 0))
    return _call(
        body, name=name, grid=(H,),
        in_specs=[col(0), col(H), col(2 * H), stat_c, stat_r],
        out_specs=[col(0), stat_c],
        out_shape=[jax.ShapeDtypeStruct((S, H * FDH), BF16), jax.ShapeDtypeStruct((H, S, 1), F32)],
        scratch_shapes=[pltpu.VMEM((S, FDH), BF16), pltpu.VMEM((S, FDH), BF16)],
        sem=("parallel",), args=(proj, proj, proj, cq, ck), comm=comm)


def _fox_bwd(proj, cq, ck, lse, do, n_heads, name, comm=None):
    S = proj.shape[0]
    H = n_heads
    tq = min(FOX_TQ, S)
    nq = S // tq
    scale = FDH ** -0.5

    def body(q_ref, k_ref, v_ref, cq_ref, ck_ref, lse_ref, do_ref, dq_ref, dk_ref, dv_ref, dcq_ref, dck_ref,
             kb_ref, vb_ref, dka_ref, dva_ref):
        kb_ref[...] = k_ref[...].astype(BF16)
        vb_ref[...] = v_ref[...].astype(BF16)
        dka_ref[...] = jnp.zeros_like(dka_ref)
        dva_ref[...] = jnp.zeros_like(dva_ref)
        dck_ref[...] = jnp.zeros_like(dck_ref)
        for qi in range(nq):
            kk = (qi + 1) * tq
            rows = slice(qi * tq, (qi + 1) * tq)
            s, mask, qb = _fox_scores(q_ref, kb_ref, cq_ref, ck_ref, qi, tq, scale)
            p = jnp.where(mask, jnp.exp(s - lse_ref[0, rows, :]), 0.0)
            dob = do_ref[rows, :]
            dp = lax.dot_general(dob, vb_ref[0:kk, :], _DN["nt"], preferred_element_type=F32)
            delta = jnp.sum(p * dp, axis=-1, keepdims=True)
            ds = p * (dp - delta)
            dcq_ref[0, rows, :] = jnp.sum(ds, axis=-1, keepdims=True)
            dck_ref[0, :, 0:kk] -= jnp.sum(ds, axis=0, keepdims=True)
            dsb = (ds * scale).astype(BF16)
            dq_ref[rows, :] = jnp.dot(dsb, kb_ref[0:kk, :], preferred_element_type=F32).astype(BF16)
            dka_ref[0:kk, :] += lax.dot_general(dsb, qb, _DN["tn"], preferred_element_type=F32)
            dva_ref[0:kk, :] += lax.dot_general(p.astype(BF16), dob, _DN["tn"], preferred_element_type=F32)
        dk_ref[...] = dka_ref[...].astype(BF16)
        dv_ref[...] = dva_ref[...].astype(BF16)

    col = lambda off: pl.BlockSpec((S, FDH), lambda h: (0, h + off))
    stat_c = pl.BlockSpec((1, S, 1), lambda h: (h, 0, 0))
    stat_r = pl.BlockSpec((1, 1, S), lambda h: (h, 0, 0))
    wide = jax.ShapeDtypeStruct((S, H * FDH), BF16)
    return _call(
        body, name=name, grid=(H,),
        in_specs=[col(0), col(H), col(2 * H), stat_c, stat_r, stat_c, col(0)],
        out_specs=[col(0), col(0), col(0), stat_c, stat_r],
        out_shape=[wide, wide, wide, jax.ShapeDtypeStruct((H, S, 1), F32), jax.ShapeDtypeStruct((H, 1, S), F32)],
        scratch_shapes=[pltpu.VMEM((S, FDH), BF16), pltpu.VMEM((S, FDH), BF16), pltpu.VMEM((S, FDH), F32), pltpu.VMEM((S, FDH), F32)],
        sem=("parallel",), args=(proj, proj, proj, cq, ck, lse, do), comm=comm)


def _rope_tables(pos, sign):
    inv = ROPE_THETA ** (-jnp.arange(0, ROPE_DIM, 2, dtype=F32) / ROPE_DIM)
    ang = pos.astype(F32)[:, None] * inv
    cos, sin = jnp.cos(ang), sign * jnp.sin(ang)
    l64 = jnp.arange(LANES) % SDH
    idx = l64 % (ROPE_DIM // 2)
    c = jnp.where(l64 < ROPE_DIM, cos[:, idx], 1.0)
    sa = jnp.where(l64 < ROPE_DIM // 2, -sin[:, idx], 0.0)
    sb = jnp.where((l64 >= ROPE_DIM // 2) & (l64 < ROPE_DIM), sin[:, idx], 0.0)
    rot = jnp.stack([c, sa, sb])
    ident = jnp.stack([jnp.ones_like(c), jnp.zeros_like(c), jnp.zeros_like(c)])
    return jnp.stack([rot, ident]).astype(F32)


def _rope(xin, tabs, n_rot, out_dtype, name, comm=None):
    S, W = xin.shape

    def body(x_ref, t_ref, o_ref):
        xv = x_ref[...]
        o = xv * t_ref[0, 0] + pltpu.roll(xv, LANES - ROPE_DIM // 2, 1) * t_ref[0, 1] + pltpu.roll(xv, ROPE_DIM // 2, 1) * t_ref[0, 2]
        o_ref[...] = o.astype(out_dtype)

    return _call(
        body, name=name, grid=(W // LANES,),
        in_specs=[pl.BlockSpec((S, LANES), lambda j: (0, j)),
                  pl.BlockSpec((1, 3, S, LANES), lambda j: (jnp.where(j < n_rot, 0, 1), 0, 0, 0))],
        out_specs=[pl.BlockSpec((S, LANES), lambda j: (0, j))],
        out_shape=[jax.ShapeDtypeStruct((S, W), out_dtype)], sem=("parallel",), args=(xin, tabs), comm=comm)[0]


SWA_PER_STEP = 16


def _swa_bias():
    r = jnp.arange(WIN)[:, None]
    c = jnp.arange(2 * WIN)[None, :]
    first = c <= r
    later = (c > r) & (c <= r + WIN)
    return jnp.where(jnp.stack([first, later]), 0.0, NEG).astype(F32)


def _swa_probs(q_ref, k_ref, sk_ref, b_ref, n, j, scale):
    st = pl.multiple_of(jnp.maximum(n - 1, 0) * WIN, WIN)
    qb = q_ref[0, j]
    kb = k_ref[0, pl.ds(st, 2 * WIN), :]
    gm = qb.shape[0]
    s = lax.dot_general(qb, kb, _DN["nt"], preferred_element_type=F32) * scale
    s = (s.reshape(gm // WIN, WIN, 2 * WIN) + b_ref[jnp.minimum(n, 1)][None]).reshape(gm, 2 * WIN)
    sink = sk_ref[0]
    m = jnp.maximum(jnp.max(s, axis=-1, keepdims=True), sink)
    e = jnp.exp(s - m)
    es = jnp.exp(sink - m)
    inv = 1.0 / (jnp.sum(e, axis=-1, keepdims=True) + es)
    return e * inv, es * inv, st, qb, kb


def _swa_specs(S, gm):
    blk = pl.BlockSpec((1, SWA_PER_STEP, gm, SDH), lambda g, n: (g, n, 0, 0))
    kv = pl.BlockSpec((1, S, SDH), lambda g, n: (g, 0, 0))
    col = pl.BlockSpec((1, gm, 1), lambda g, n: (g, 0, 0))
    bias = pl.BlockSpec((2, WIN, 2 * WIN), lambda g, n: (0, 0, 0))
    return blk, kv, col, bias


def _swa_fwd(q, k, v, sinks, name, comm=None):
    KH, nb, gm, _ = q.shape
    S = k.shape[1]
    scale = SDH ** -0.5

    def body(q_ref, k_ref, v_ref, sk_ref, b_ref, o_ref):
        for j in range(SWA_PER_STEP):
            p, _, st, _, _ = _swa_probs(q_ref, k_ref, sk_ref, b_ref, pl.program_id(1) * SWA_PER_STEP + j, j, scale)
            vb = v_ref[0, pl.ds(st, 2 * WIN), :]
            o_ref[0, j] = jnp.dot(p.astype(BF16), vb, preferred_element_type=F32).astype(BF16)

    blk, kv, col, bias = _swa_specs(S, gm)
    return _call(
        body, name=name, grid=(KH, nb // SWA_PER_STEP), in_specs=[blk, kv, kv, col, bias], out_specs=[blk],
        out_shape=[jax.ShapeDtypeStruct(q.shape, BF16)], sem=("parallel", "parallel"), args=(q, k, v, sinks, _swa_bias()), comm=comm)[0]


def _swa_bwd(q, k, v, sinks, do, name, comm=None):
    KH, nb, gm, _ = q.shape
    S = k.shape[1]
    scale = SDH ** -0.5

    def body(q_ref, k_ref, v_ref, sk_ref, b_ref, do_ref, dq_ref, dk_ref, dv_ref, dsk_ref):
        @pl.when(pl.program_id(1) == 0)
        def _():
            dk_ref[...] = jnp.zeros_like(dk_ref)
            dv_ref[...] = jnp.zeros_like(dv_ref)
            dsk_ref[...] = jnp.zeros_like(dsk_ref)

        blocks = []
        for j in range(SWA_PER_STEP):
            p, ps, st, qb, kb = _swa_probs(q_ref, k_ref, sk_ref, b_ref, pl.program_id(1) * SWA_PER_STEP + j, j, scale)
            vb = v_ref[0, pl.ds(st, 2 * WIN), :]
            dob = do_ref[0, j]
            dp = lax.dot_general(dob, vb, _DN["nt"], preferred_element_type=F32)
            delta = jnp.sum(p * dp, axis=-1, keepdims=True)
            dsb = (p * (dp - delta) * scale).astype(BF16)
            dq_ref[0, j] = jnp.dot(dsb, kb, preferred_element_type=F32)
            blocks.append((st, lax.dot_general(dsb, qb, _DN["tn"], preferred_element_type=F32),
                           lax.dot_general(p.astype(BF16), dob, _DN["tn"], preferred_element_type=F32), ps * delta))
        for st, dk, dv, dsk in blocks:
            dk_ref[0, pl.ds(st, 2 * WIN), :] += dk
            dv_ref[0, pl.ds(st, 2 * WIN), :] += dv
            dsk_ref[0] -= dsk

    blk, kv, col, bias = _swa_specs(S, gm)
    return _call(
        body, name=name, grid=(KH, nb // SWA_PER_STEP), in_specs=[blk, kv, kv, col, bias, blk], out_specs=[blk, kv, kv, col],
        out_shape=[jax.ShapeDtypeStruct(q.shape, F32), jax.ShapeDtypeStruct(k.shape, F32),
                   jax.ShapeDtypeStruct(k.shape, F32), jax.ShapeDtypeStruct(sinks.shape, F32)],
        sem=("parallel", "arbitrary"), args=(q, k, v, sinks, _swa_bias(), do), comm=comm)


def _shift_down(u, k):
    row = lax.broadcasted_iota(jnp.int32, u.shape, 0)
    return jnp.where(row >= k, pltpu.roll(u, k, 0), 0.0)


def _shift_up(u, k):
    n = u.shape[0]
    row = lax.broadcasted_iota(jnp.int32, u.shape, 0)
    return jnp.where(row < n - k, pltpu.roll(u, n - k, 0), 0.0)


def _conv3(u, w_ref, b_ref):
    return w_ref[0:1, :] * _shift_down(u, 2) + w_ref[1:2, :] * _shift_down(u, 1) + w_ref[2:3, :] * u + b_ref[...]


def _conv_gate(u, cw, cb, name, comm=None):
    S, F2 = u.shape
    Fh = F2 // 2
    tc = _tile(Fh, 256)
    nf = Fh // tc

    def body(ug_ref, uv_ref, wg_ref, wv_ref, bg_ref, bv_ref, a_ref):
        g = _conv3(ug_ref[...], wg_ref, bg_ref)
        val = _conv3(uv_ref[...], wv_ref, bv_ref)
        a_ref[...] = (g * (1.0 / (1.0 + jnp.exp(-g))) * val).astype(BF16)

    blk = lambda r, off: pl.BlockSpec((r, tc), lambda j: (0, j + off))
    return _call(
        body, name=name, grid=(nf,),
        in_specs=[blk(S, 0), blk(S, nf), blk(3, 0), blk(3, nf), blk(1, 0), blk(1, nf)], out_specs=[blk(S, 0)],
        out_shape=[jax.ShapeDtypeStruct((S, Fh), BF16)], sem=("parallel",), args=(u, u, cw, cw, cb, cb), comm=comm)[0]


def _conv_gate_bwd(u, da, cw, cb, name, comm=None):
    S, F2 = u.shape
    Fh = F2 // 2
    tc = _tile(Fh, 256)
    nf = Fh // tc

    def half(h, dx, uu, w_ref, du_ref, dw_ref, db_ref):
        up1, up2 = _shift_up(dx, 1), _shift_up(dx, 2)
        du = w_ref[2:3, :] * dx + w_ref[1:2, :] * up1 + w_ref[0:1, :] * up2
        du_ref[h] = du.astype(BF16)
        dw_ref[h, 0:1, :] = jnp.sum(up2 * uu, axis=0, keepdims=True)
        dw_ref[h, 1:2, :] = jnp.sum(up1 * uu, axis=0, keepdims=True)
        dw_ref[h, 2:3, :] = jnp.sum(dx * uu, axis=0, keepdims=True)
        db_ref[h] = jnp.sum(dx, axis=0, keepdims=True)

    def body(ug_ref, uv_ref, da_ref, wg_ref, wv_ref, bg_ref, bv_ref, du_ref, dw_ref, db_ref):
        ug = ug_ref[...]
        uv = uv_ref[...]
        g = _conv3(ug, wg_ref, bg_ref)
        val = _conv3(uv, wv_ref, bv_ref)
        sig = 1.0 / (1.0 + jnp.exp(-g))
        da_ = da_ref[...]
        dg = da_ * val * (sig * (1.0 + g * (1.0 - sig)))
        dval = da_ * (g * sig)
        half(0, dg, ug, wg_ref, du_ref, dw_ref, db_ref)
        half(1, dval, uv, wv_ref, du_ref, dw_ref, db_ref)

    blk = lambda r, off: pl.BlockSpec((r, tc), lambda j: (0, j + off))
    both = lambda r: pl.BlockSpec((2, r, tc), lambda j: (0, 0, j))
    return _call(
        body, name=name, grid=(nf,),
        in_specs=[blk(S, 0), blk(S, nf), blk(S, 0), blk(3, 0), blk(3, nf), blk(1, 0), blk(1, nf)],
        out_specs=[both(S), both(3), both(1)],
        out_shape=[jax.ShapeDtypeStruct((2, S, Fh), BF16), jax.ShapeDtypeStruct((2, 3, Fh), F32), jax.ShapeDtypeStruct((2, 1, Fh), F32)],
        sem=("parallel",), args=(u, u, da, cw, cw, cb, cb), comm=comm)


def _to_groups(t, kh):
    S, width = t.shape
    g = width // SDH // kh
    return t.reshape(S // WIN, WIN, kh, g, SDH).transpose(2, 0, 3, 1, 4).reshape(kh, S // WIN, g * WIN, SDH)


def _from_groups(t):
    kh, nb, gm, _ = t.shape
    g = gm // WIN
    return t.reshape(kh, nb, g, WIN, SDH).transpose(1, 3, 0, 2, 4).reshape(nb * WIN, kh * g * SDH)


class LocalWeights:
    def __init__(self, weights):
        self.weights, self.grads = weights, {}

    def w(self, name):
        return self.weights[name]

    def carry(self, stage, last=()):
        return None

    def carried(self, stage, comm):
        pass

    def grad(self, name, g):
        self.grads[name] = g


def _local_step(dm, x, tgt, pos, mod, sp, pp, h0=None):
    S, D, FH, QH, KH, Fh = dm
    m = [[mod[i:i + 1, j * D:(j + 1) * D] for j in range(6)] for i in range(DEPTH)]

    last = []

    def run(fn, *args, name, **kw):
        comm = pp.carry(name, last)
        args = [pp.w(arg[1]) if isinstance(arg, tuple) and arg[:1] == ("w",) else arg for arg in args]
        out = fn(*args, name=name, comm=comm, **kw)
        if comm is not None:
            pp.carried(name, comm)
        last[:] = list(out) if isinstance(out, (list, tuple)) else [out]
        return out

    sv = []
    xs = x
    h = _modulate(xs, m[0][1], m[0][0], "mod_in") if h0 is None else h0
    last[:] = [h]
    for i in range(DEPTH):
        sh1, sc1, g1, sh2, sc2, g2 = m[i]
        L = {}
        L["x_in"], L["h1"] = xs, h
        if i == 0:
            proj = run(_mm, h[None], ("w", "fox_w_in"), mode="nn", out_dtype=F32, name="fox_proj", tn=896)[0]
            cum = _fox_gate_fwd(proj, sp["fox_b_f"], FH, "fox_gate")
            cq = cum[:, :FH].T[:, :, None]
            ck = cum[:, :FH].T[:, None, :]
            o, lse = run(_fox_fwd, proj, cq, ck, FH, name="fox_attn")
            L.update(proj=proj, cq=cq, ck=ck, lse=lse, o=o)
            y = run(_mm, o[None], ("w", "fox_w_o"), mode="nn", out_dtype=F32, name="fox_out")[0]
        else:
            proj = run(_mm, h[None], ("w", "swa_w_in"), mode="nn", out_dtype=F32, name="swa_proj", tn=640)[0]
            tabs = _rope_tables(pos, 1.0)
            n_rot = (QH + KH) * SDH // LANES
            pr = run(_rope, proj, tabs, n_rot, BF16, name="swa_rope")
            qh = _to_groups(pr[:, :QH * SDH], KH)
            kh = pr[:, QH * SDH:(QH + KH) * SDH].reshape(S, KH, SDH).transpose(1, 0, 2)
            vh = pr[:, (QH + KH) * SDH:].reshape(S, KH, SDH).transpose(1, 0, 2)
            oh = run(_swa_fwd, qh, kh, vh, sp["sinks"], name="swa_attn")
            o = _from_groups(oh)
            L.update(qh=qh, kh=kh, vh=vh, o=o)
            y = run(_mm, o[None], ("w", "swa_w_o"), mode="nn", out_dtype=F32, name="swa_out")[0]
        L["y1"] = y
        x1, L["xh1"], L["rs1"], h2 = run(_ln_fwd, xs, y, g1, sp["ln_mix_g"][i], sp["ln_mix_b"][i], sc2, sh2, name=f"ln_mix{i}")
        L["x1"], L["h2"] = x1, h2
        u = run(_mm, h2[None], ("w", f"ffn_w_up{i}"), mode="nn", out_dtype=F32, name=f"ffn_up{i}", tm=1024, tn=1408)[0]
        a = run(_conv_gate, u, sp["conv_w"][i], sp["conv_b"][i], name=f"ffn_gate{i}")
        y2 = run(_mm, a[None], ("w", f"ffn_w_down{i}"), mode="nn", out_dtype=F32, name=f"ffn_down{i}", tk=5632, tm=512)[0]
        L.update(u=u, a=a, y2=y2)
        if i + 1 < DEPTH:
            xs, L["xh2"], L["rs2"], h = run(_ln_fwd, x1, y2, g2, sp["ln_ffn_g"][i], sp["ln_ffn_b"][i], m[i + 1][1], m[i + 1][0], name=f"ln_ffn{i}")
        else:
            xs, L["xh2"], L["rs2"] = run(_ln_fwd, x1, y2, g2, sp["ln_ffn_g"][i], sp["ln_ffn_b"][i], None, None, name=f"ln_ffn{i}")
        sv.append(L)

    dx, loss_cols = _loss_head(xs, tgt, "loss_head")

    gs = {k: [None] * DEPTH for k in ("conv_w", "conv_b", "ln_mix_g", "ln_mix_b", "ln_ffn_g", "ln_ffn_b")}
    dmp = [dict() for _ in range(DEPTH)]
    dres, pend = dx, None
    for i in reversed(range(DEPTH)):
        sh1, sc1, g1, sh2, sc2, g2 = m[i]
        L = sv[i]
        res = _ln_bwd(dres, L["xh2"], L["rs2"], sp["ln_ffn_g"][i], L["y2"], g2, f"ln_ffn_bwd{i}",
                      None if pend is None else (*pend, sp["ln_ffn_b"][i]))
        dres, dy, gs["ln_ffn_g"][i], gs["ln_ffn_b"][i], dmp[i]["g2"] = res[:5]
        if pend is not None:
            dmp[i + 1]["sc1"], dmp[i + 1]["sh1"] = res[5:]
        da = run(_mm, dy[None], pp.w(f"ffn_w_down{i}"), mode="nt", out_dtype=F32, name=f"ffn_down_dx{i}", tm=1024, tn=1408)[0]
        pp.grad(f"ffn_w_down{i}", run(_mm, L["a"][None], dy[None], mode="tn", out_dtype=BF16, name=f"ffn_down_dw{i}", tm=1408))
        du, dcw, dcb = run(_conv_gate_bwd, L["u"], da, sp["conv_w"][i], sp["conv_b"][i], name=f"ffn_gate_bwd{i}")
        gs["conv_w"][i] = dcw.transpose(1, 0, 2).reshape(3, 2 * Fh)
        gs["conv_b"][i] = dcb.transpose(1, 0, 2).reshape(1, 2 * Fh)
        pp.grad(f"ffn_w_up{i}", run(_mm, L["h2"][None], du, mode="tn", out_dtype=BF16, name=f"ffn_up_dw{i}", out_groups=N_CHIPS, tn=1408))
        dh2 = run(_mm, du, pp.w(f"ffn_w_up{i}"), mode="nt", out_dtype=F32, name=f"ffn_up_dx{i}", tk=2816)[0]
        dres, dy, gs["ln_mix_g"][i], gs["ln_mix_b"][i], dmp[i]["g1"], dmp[i]["sc2"], dmp[i]["sh2"] = _ln_bwd(
            dres, L["xh1"], L["rs1"], sp["ln_mix_g"][i], L["y1"], g1, f"ln_mix_bwd{i}", (dh2, sc2, sp["ln_mix_b"][i]))
        if i == 0:
            do = run(_mm, dy[None], pp.w("fox_w_o"), mode="nt", out_dtype=BF16, name="fox_out_dx")[0]
            pp.grad("fox_w_o", run(_mm, L["o"][None], dy[None], mode="tn", out_dtype=BF16, name="fox_out_dw"))
            dq, dk, dv, dcq, dck = run(_fox_bwd, L["proj"], L["cq"], L["ck"], L["lse"], do, FH, name="fox_attn_bwd")
            dcum = dcq[:, :, 0].T + dck[:, 0, :].T
            dcum = jnp.pad(dcum, ((0, 0), (0, LANES - FH)))
            dfl, db_f = _fox_gate_bwd(dcum, L["proj"], sp["fox_b_f"], FH, "fox_gate_bwd")
            gs["fox_b_f"] = db_f
            dproj = jnp.concatenate([dq, dk, dv, dfl], axis=1)
            pp.grad("fox_w_in", run(_mm, L["h1"][None], dproj[None], mode="tn", out_dtype=BF16, name="fox_proj_dw", tn=896))
            dh1 = run(_mm, dproj[None], pp.w("fox_w_in"), mode="nt", out_dtype=F32, name="fox_proj_dx", tk=6272, tm=512)[0]
        else:
            do = run(_mm, dy[None], pp.w("swa_w_o"), mode="nt", out_dtype=BF16, name="swa_out_dx")[0]
            pp.grad("swa_w_o", run(_mm, L["o"][None], dy[None], mode="tn", out_dtype=BF16, name="swa_out_dw"))
            dqh, dkh, dvh, dsk = run(_swa_bwd, L["qh"], L["kh"], L["vh"], sp["sinks"], _to_groups(do, KH), name="swa_attn_bwd")
            gs["sinks"] = jnp.sum(dsk.reshape(QH, WIN), axis=1)
            dpr = jnp.concatenate([_from_groups(dqh), dkh.transpose(1, 0, 2).reshape(S, KH * SDH),
                                   dvh.transpose(1, 0, 2).reshape(S, KH * SDH)], axis=1)
            n_rot = (QH + KH) * SDH // LANES
            dproj = _rope(dpr, _rope_tables(pos, -1.0), n_rot, BF16, "swa_rope_bwd")
            dh1 = run(_mm, dproj[None], pp.w("swa_w_in"), mode="nt", out_dtype=F32, name="swa_proj_dx", tk=640)[0]
            pp.grad("swa_w_in", run(_mm, L["h1"][None], dproj[None], mode="tn", out_dtype=BF16, name="swa_proj_dw", out_groups=N_CHIPS, tn=640))
        pend = (dh1, sc1)
    grad_x, dmp[0]["sc1"], dmp[0]["sh1"] = _mod_bwd(pend[0], sv[0]["x_in"], pend[1], dres, "mod_mix_bwd0")
    dmod = [jnp.concatenate([p["sh1"], p["sc1"], p["g1"], p["sh2"], p["sc2"], p["g2"]], axis=1) for p in dmp]
    return loss_cols, grad_x, gs, jnp.concatenate(dmod, axis=0)


def _allgather_small(v, name):
    m_per, n = v.shape

    def body(x_ref, out_ref, send_sems, recv_sems, local_sem):
        x, y, c, chips = _place()
        me, sibling = (x, y, c), (x, y, 1 - c)

        def rows(px, py, pc):
            return out_ref.at[pl.ds((4 * px + 2 * py + pc) * m_per, m_per), :]

        def copy(k, block, to, src=None):
            return _remote(rows(*block) if src is None else src, rows(*block), send_sems.at[k], recv_sems.at[k], to)

        mine = pltpu.make_async_copy(x_ref, rows(*me), local_sem)
        mine.start()
        first = [copy(0, me, sibling, src=x_ref)]
        first += [copy(1 + j, me, (*chip, c), src=x_ref) for j, chip in enumerate(chips)]
        for cp in first:
            cp.start()
        passed = [copy(4 + j, (*chip, c), sibling) for j, chip in enumerate(chips)]
        for j, chip in enumerate(chips):
            copy(1 + j, (*chip, c), me).wait_recv()
            passed[j].start()
        copy(0, sibling, me).wait_recv()
        for j, chip in enumerate(chips):
            copy(4 + j, (*chip, 1 - c), me).wait_recv()
        for cp in first + passed:
            cp.wait_send()
        mine.wait()

    return pl.pallas_call(
        body, name=name, out_shape=jax.ShapeDtypeStruct((N_DEV * m_per, n), v.dtype),
        in_specs=[pl.BlockSpec(memory_space=pltpu.VMEM)], out_specs=pl.BlockSpec(memory_space=pltpu.VMEM),
        scratch_shapes=[pltpu.SemaphoreType.DMA((7,)), pltpu.SemaphoreType.DMA((7,)), pltpu.SemaphoreType.DMA],
        compiler_params=pltpu.CompilerParams(vmem_limit_bytes=VMEM_LIMIT),
    )(v)


def _row_tile(r, pref=256):
    return _tile(r, pref, 16)


def _cast_bf16(w, layer, chip, name, after=()):
    _, R, C = w.shape
    tr = _row_tile(R)

    def body(s_ref, w_ref, *rest):
        rest[-1][...] = w_ref[...].astype(BF16)

    return pl.pallas_call(
        body, name=name,
        grid_spec=pltpu.PrefetchScalarGridSpec(
            num_scalar_prefetch=1, grid=(R // tr,),
            in_specs=[pl.BlockSpec((None, tr, C), lambda i, s: (layer, i, 0))] + _any_specs(len(after)),
            out_specs=pl.BlockSpec((None, tr, C), lambda i, s: (s[0], i, 0))),
        out_shape=jax.ShapeDtypeStruct((N_CHIPS, R, C), BF16), compiler_params=_params(("parallel",)),
    )(jnp.reshape(chip, (1,)).astype(jnp.int32), w, *after)


def _add_sibling(g, got, c, name):
    G, R, C = g.shape
    rh = R // 2
    tr = _row_tile(rh)
    nb = rh // tr

    def body(c_ref, g_ref, o_ref, p_ref):
        p_ref[...] = (g_ref[...].astype(F32) + o_ref[...].astype(F32)).astype(BF16)

    return pl.pallas_call(
        body, name=name,
        grid_spec=pltpu.PrefetchScalarGridSpec(
            num_scalar_prefetch=1, grid=(G, nb),
            in_specs=[pl.BlockSpec((1, tr, C), lambda s, i, c_ref: (s, c_ref[0] * nb + i, 0)),
                      pl.BlockSpec((1, tr, C), lambda s, i, c_ref: (s, i, 0))],
            out_specs=pl.BlockSpec((1, tr, C), lambda s, i, c_ref: (s, i, 0))),
        out_shape=jax.ShapeDtypeStruct((G, rh, C), BF16), compiler_params=_params(("parallel", "parallel")),
    )(jnp.reshape(c, (1,)).astype(jnp.int32), g, got)


def _sum_chips(part, landed, chip, c, name):
    G, rh, C = part.shape
    tr = _row_tile(rh)
    nb = rh // tr

    def body(p_ref, own_ref, *rest):
        acc = own_ref[...].astype(F32)
        for ref in rest[:G - 1]:
            acc = acc + ref[...].astype(F32)
        rest[G - 1][...] = acc

    slot = lambda k: pl.BlockSpec((None, tr, C), lambda i, p: ((p[0] + k) % G, i, 0))
    return pl.pallas_call(
        body, name=name,
        grid_spec=pltpu.PrefetchScalarGridSpec(
            num_scalar_prefetch=1, grid=(nb,), in_specs=[slot(k) for k in range(G)],
            out_specs=pl.BlockSpec((tr, C), lambda i, p: (p[1] * nb + i, 0))),
        out_shape=jax.ShapeDtypeStruct((2 * rh, C), F32), compiler_params=_params(("parallel",)),
    )(jnp.stack([chip, c]).astype(jnp.int32), part, *([landed] * (G - 1)))


def _adam_math(w, g, m, v):
    m = ADAM_B1 * m + (1.0 - ADAM_B1) * g
    v = ADAM_B2 * v + (1.0 - ADAM_B2) * (g * g)
    m_hat = m / (1.0 - ADAM_B1 ** ADAM_STEP)
    v_hat = v / (1.0 - ADAM_B2 ** ADAM_STEP)
    delta = -ADAM_LR * (m_hat / (jnp.sqrt(v_hat) + ADAM_EPS) + ADAM_WD * w)
    return delta, m, v


def _adamw(w, g, m, v, layer, prev, name, by_cols=False, after=()):
    L, R, C = w.shape
    tr = R if by_cols else _tile(R, 128, 8)
    tc = _tile(C, 256) if by_cols else C
    n_alias = len(prev)
    prev = tuple(prev) + tuple(after)
    n_prev = len(prev)

    def body(w_ref, g_ref, m_ref, v_ref, *rest):
        go_ref, d_ref, mo_ref, vo_ref = rest[n_prev:]
        gv = g_ref[...]
        go_ref[...] = gv
        d_ref[...], mo_ref[...], vo_ref[...] = _adam_math(w_ref[...], gv, m_ref[...], v_ref[...])

    lay = pl.BlockSpec((None, tr, tc), lambda i: (layer, i // (C // tc), i % (C // tc)))
    flat = pl.BlockSpec((tr, tc), lambda i: (i // (C // tc), i % (C // tc)))
    return _call(
        body, name=name, grid=((R // tr) * (C // tc),), in_specs=[lay, flat, lay, lay] + _any_specs(n_prev), out_specs=[lay] * 4,
        out_shape=[jax.ShapeDtypeStruct((L, R, C), F32)] * 4, aliases={4 + k: k for k in range(n_alias)},
        sem=("parallel",), args=(w, g, m, v, *prev))


def _cond_rows(c_row, cw, name, after=()):
    D = c_row.shape[1]
    nr, fc = cw.shape

    def body(c_ref, e_ref, *rest):
        o_ref = rest[-1]
        o_ref[...] = jnp.zeros_like(o_ref)
        cv = c_ref[...]
        o_ref[0:1, 0:D] = cv * (1.0 / (1.0 + jnp.exp(-cv)))
        o_ref[8:8 + nr, 0:fc] = e_ref[...]

    vmem = pl.BlockSpec(memory_space=pltpu.VMEM)
    return pl.pallas_call(body, name=name, in_specs=[vmem, vmem] + _any_specs(len(after)), out_specs=vmem,
                          out_shape=jax.ShapeDtypeStruct((16, max(D, fc)), F32))(c_row, cw, *after)


def _ada_fwd(cact, ada_w, ada_b, layer, chip, name):
    _, D, NC = ada_w.shape
    tn = _tile(NC, 1024)
    nj = NC // tn

    def body(idx_ref, c_ref, w_ref, b_ref, o_ref):
        acc = jnp.dot(c_ref[...].astype(BF16), w_ref[0].astype(BF16), preferred_element_type=F32)
        o_ref[...] = acc + b_ref[pl.ds(idx_ref[0], 1), :]

    return pl.pallas_call(
        body, name=name,
        grid_spec=pltpu.PrefetchScalarGridSpec(
            num_scalar_prefetch=1, grid=(nj,),
            in_specs=[pl.BlockSpec((8, D), lambda j, idx: (0, 0)),
                      pl.BlockSpec((1, D, tn), lambda j, idx: (idx[0], 0, j)),
                      pl.BlockSpec((DEPTH, tn), lambda j, idx: (0, idx[1] * nj + j))],
            out_specs=pl.BlockSpec((8, tn), lambda j, idx: (0, j))),
        out_shape=jax.ShapeDtypeStruct((8, NC), F32), compiler_params=_params(("parallel",)),
    )(jnp.stack([layer, chip]).astype(jnp.int32), cact, ada_w, ada_b)


def _ada_grad_adamw(cact_t, dmod, w, m, v, name):
    L, D, NC = w.shape
    tr = _tile(D, 128, 8)

    def body(c_ref, d_ref, w_ref, m_ref, v_ref, g_ref, dl_ref, mo_ref, vo_ref):
        g = jnp.dot(c_ref[...], d_ref[...], preferred_element_type=F32, precision=HIGHEST)
        g_ref[...] = g
        dl_ref[...], mo_ref[...], vo_ref[...] = _adam_math(w_ref[...], g, m_ref[...], v_ref[...])

    lay = pl.BlockSpec((None, tr, NC), lambda l, i: (l, i, 0))
    return _call(
        body, name=name, grid=(L, D // tr),
        in_specs=[pl.BlockSpec((tr, N_DEV), lambda l, i: (i, 0)), pl.BlockSpec((None, N_DEV, NC), lambda l, i: (l, 0, 0)), lay, lay, lay],
        out_specs=[lay] * 4, out_shape=[jax.ShapeDtypeStruct((L, D, NC), F32)] * 4,
        sem=("parallel", "parallel"), args=(cact_t, dmod, w, m, v))


def _sum_devices(gathered, name):
    n, R, C = gathered.shape

    def body(g_ref, o_ref):
        acc = g_ref[0]
        for j in range(1, n):
            acc = acc + g_ref[j]
        o_ref[...] = acc

    return pl.pallas_call(body, name=name, out_shape=jax.ShapeDtypeStruct((R, C), F32),
                          compiler_params=pltpu.CompilerParams(vmem_limit_bytes=VMEM_LIMIT))(gathered)


def _adamw_small(w, g, m, v, name):
    def body(w_ref, g_ref, m_ref, v_ref, d_ref, mo_ref, vo_ref):
        d_ref[...], mo_ref[...], vo_ref[...] = _adam_math(w_ref[...], g_ref[...], m_ref[...], v_ref[...])

    return pl.pallas_call(body, name=name, out_shape=[jax.ShapeDtypeStruct(w.shape, F32)] * 3)(w, g, m, v)


def _pad_rows(flat, unit=8 * LANES):
    n = flat.shape[0]
    total = -(-n // unit) * unit
    return jnp.pad(flat, (0, total - n)).reshape(total // LANES, LANES)


def _pad_lanes(v2d):
    return jnp.pad(v2d.reshape(1, -1), ((0, 0), (0, LANES - v2d.size)))


FORWARD = {
    "fox_proj": ((), ("ffn_w_up0",)),
    "ffn_up0": (("ffn_w_up0",), ("ffn_w_down0", "swa_w_in")),
    "ffn_down0": (("ffn_w_down0", "swa_w_in"), ("swa_w_o", "ffn_w_up1")),
    "swa_out": (("swa_w_o", "ffn_w_up1"), ("ffn_w_down1",)),
    "ffn_down1": (("ffn_w_down1",), ()),
}
PLAN = {
    "ffn_gate_bwd1": [("swap", "ffn_w_down1")],
    "ffn_up_dx1": [("scatter", "ffn_w_down1", 0, 1, 1), ("swap", "ffn_w_up1")],
    "swa_attn_bwd": [("scatter", "ffn_w_up1", 0, 4, 8), ("swap", "swa_w_o")],
    "swa_proj_dx": [("scatter", "swa_w_o", 0, 1, 1)],
    "ffn_down_dx0": [("scatter", "ffn_w_up1", 4, 6, 8), ("swap", "swa_w_in")],
    "ffn_down_dw0": [("scatter", "swa_w_in", 0, 1, 1)],
    "ffn_gate_bwd0": [("scatter", "ffn_w_up1", 6, 8, 8), ("swap", "ffn_w_down0")],
    "ffn_up_dx0": [("scatter", "ffn_w_down0", 0, 1, 1), ("swap", "ffn_w_up0")],
    "fox_attn_bwd": [("scatter", "ffn_w_up0", 0, 5, 8), ("swap", "fox_w_o")],
    "fox_proj_dw": [("scatter", "fox_w_o", 0, 1, 1), ("scatter", "ffn_w_up0", 5, 6, 8)],
    "fox_proj_dx": [("scatter", "ffn_w_up0", 6, 8, 8), ("swap", "fox_w_in")],
}


class Exchanges:
    def __init__(self, dm, slots, chip, c):
        self.dm, self.slots, self.chip, self.c = dm, dict(slots), chip, c
        self.raw, self.part, self.landed, self.grads, self.views, self.pending = {}, {}, {}, {}, {}, {}

    def gather_start(self, keys, name, after):
        self.first = (keys, _gather_comm([self.slots[k] for k in keys]))
        self.first_state, token = _split_start(self.first[1], name + "_start", after)
        return token

    def gather_finish(self, after, name):
        keys, comm = self.first
        _split_wait(comm, self.first_state, after, name + "_wait")
        pass_on = _forward_comm(comm.results)
        _run_comm(pass_on, name + "_pass")
        self.slots.update(zip(keys, pass_on.results))
        self.fence = list(pass_on.results)

    def before(self, stage, last):
        need, nxt = FORWARD[stage]
        if need:
            self.gather_finish(list(last), "gather_" + "_".join(need))
        if nxt:
            return self.gather_start(list(nxt), "gather_" + "_".join(nxt), list(last) + self.fence)
        return None

    def w(self, key):
        if key not in self.views:
            S, D, FH, QH, KH, Fh = self.dm
            full = self.slots[key]
            if key == "fox_w_in":
                cols = full.shape[2]
                full = jnp.pad(full.transpose(1, 0, 2).reshape(D, N_CHIPS * cols), ((0, 0), (0, 3 * D + LANES - N_CHIPS * cols)))[None]
            elif key in ("fox_w_o", "swa_w_o"):
                full = full.reshape(1, D, D)
            elif key.startswith("ffn_w_down"):
                full = full.reshape(1, Fh, D)
            self.views[key] = full
        return self.views[key]

    def carry(self, stage, last=()):
        todo = []
        token = self.before(stage, last) if stage in FORWARD else None
        if token is not None:
            todo.append(("order", [], Comm([token], [], {}, 1, lambda *refs: None, lambda *refs: None)))
        for kind, key, *chunk in PLAN.get(stage, ()):
            if kind == "swap":
                todo.append((kind, [key], _swap_comm([self.raw[key]])))
            elif kind == "scatter":
                todo.append((kind, [(key, *chunk)], _scatter_comm([self.part[key]], [self.landed.get(key)], [tuple(chunk)])))
        self.pending[stage] = todo
        return _merge([cm for _, _, cm in todo])

    def carried(self, stage, comm):
        for kind, keys, cm in self.pending.pop(stage):
            if kind == "swap":
                self.part[keys[0]] = _add_sibling(self.raw[keys[0]], cm.results[0], self.c, f"add_sibling_{keys[0]}")
            elif kind == "scatter":
                self.landed[keys[0][0]] = cm.results[0]

    def grad(self, key, g):
        S, D, FH, QH, KH, Fh = self.dm
        if key == "fox_w_in":
            cols = self.slots[key].shape[2]
            g = g[0][:, :N_CHIPS * cols].reshape(D, N_CHIPS, cols).transpose(1, 0, 2)
        elif key in ("fox_w_o", "swa_w_o"):
            g = g.reshape(N_CHIPS, D // N_CHIPS, D)
        elif key.startswith("ffn_w_down"):
            g = g.reshape(N_CHIPS, Fh // N_CHIPS, D)
        self.raw[key] = g

    def last_start(self, last, after):
        part = self.part[last]
        self.last = (last, _scatter_comm([part], [lax.empty(part.shape, part.dtype)], [(0, 1, 1)]))
        self.last_state, token = _split_start(self.last[1], "grads_last_start", after)
        return token

    def join_landed(self):
        keys = list(self.landed)
        join = _join_comm([_sum_chips(self.part[k], self.landed[k], self.chip, self.c, f"sum_chips_{k}") for k in keys])
        _run_comm(join, "grads_join")
        return dict(zip(keys, join.results))

    def last_finish(self, after):
        last, comm = self.last
        _split_wait(comm, self.last_state, after, "grads_last_wait")
        join = _join_comm([_sum_chips(self.part[last], comm.results[0], self.chip, self.c, f"sum_chips_{last}")])
        _run_comm(join, "grads_join_last")
        return join.results[0]


def _step(dm, a):
    S, D, FH, QH, KH, Fh = dm
    ix, iy, ic = lax.axis_index("x"), lax.axis_index("y"), lax.axis_index("c")
    chip = 2 * ix + iy
    dev = 2 * chip + ic
    F2c = a["ffn_w_up"].shape[2]
    NC = a["ada_w"].shape[2]

    names = ["fox_w_in", "fox_w_o", "swa_w_in", "swa_w_o", "ffn_w_up", "ffn_w_up", "ffn_w_down", "ffn_w_down"]
    layers = [0, 0, 0, 0, 0, 1, 0, 1]
    keys = ["fox_w_in", "fox_w_o", "swa_w_in", "swa_w_o", "ffn_w_up0", "ffn_w_up1", "ffn_w_down0", "ffn_w_down1"]
    cast = lambda t, after: _cast_bf16(a[names[t]], layers[t], chip, f"cast_{keys[t]}", after)
    pp = Exchanges(dm, {keys[t]: cast(t, ()) for t in (0, 1)}, chip, ic)

    e0 = _cond_rows(a["c"], a["ffn_conv_w"].reshape(DEPTH * 3, F2c), "silu_c", [pp.slots[k] for k in keys[:2]])
    g0 = _allgather_small(e0, "gather_cond").reshape(N_DEV, 16, e0.shape[1])
    cact = g0[:, 0, :D]
    conv_w = g0[0::2, 8:8 + DEPTH * 3, :F2c].transpose(1, 0, 2).reshape(DEPTH, 3, N_CHIPS * F2c)
    rows = _ada_fwd(cact, a["ada_w"], a["ada_b"], ic, chip, "ada_proj")
    g1 = _allgather_small(rows, "gather_mod").reshape(N_CHIPS, DEPTH, 8, NC)
    mod = lax.dynamic_index_in_dim(g1, dev, axis=2, keepdims=False).transpose(1, 0, 2).reshape(DEPTH, N_CHIPS * NC)

    token = pp.gather_start(keys[:2], "gather_fox", [mod])
    pp.slots.update({keys[t]: cast(t, (token,)) for t in range(2, len(keys))})
    h0 = _modulate(a["x"][0], mod[0:1, D:2 * D], mod[0:1, 0:D], "mod_in")
    pp.gather_finish([pp.slots[k] for k in keys[2:]] + [h0], "gather_fox")
    sp = {"fox_b_f": _pad_lanes(a["fox_b_f"]), "sinks": jnp.repeat(a["swa_sinks"].reshape(KH, QH // KH), WIN, axis=1)[:, :, None],
          "conv_w": [conv_w[i] for i in range(DEPTH)], "conv_b": [a["ffn_conv_b"][i:i + 1] for i in range(DEPTH)]}
    for nm in ("ln_mix_g", "ln_mix_b", "ln_ffn_g", "ln_ffn_b"):
        sp[nm] = [a[nm][i:i + 1] for i in range(DEPTH)]

    loss_cols, grad_x, gs, dmod = _local_step(dm, a["x"][0], a["loss_target"][0], a["positions"][0], mod, sp, pp, h0)
    loss = lax.psum(0.5 / D * jnp.sum(loss_cols), ("x", "y", "c"))
    out = {"loss": loss, "grad_x": grad_x[None]}

    pieces = [dmod.reshape(-1), gs["fox_b_f"].reshape(-1), _pad_lanes(gs["sinks"]).reshape(-1),
              jnp.stack(gs["conv_w"]).reshape(-1), jnp.stack(gs["conv_b"]).reshape(-1)]
    pieces += [jnp.stack(gs[nm]).reshape(-1) for nm in ("ln_mix_g", "ln_mix_b", "ln_ffn_g", "ln_ffn_b")]
    sizes = [p.shape[0] for p in pieces]
    packed = _pad_rows(jnp.concatenate(pieces))
    allp = _allgather_small(packed, "gather_small").reshape(N_DEV, packed.shape[0], LANES)
    tot = _sum_devices(allp, "sum_small").reshape(-1)
    offs = [sum(sizes[:k]) for k in range(len(sizes))]
    take = lambda k: tot[offs[k]:offs[k] + sizes[k]]
    g_small = {"ada_b": take(0).reshape(DEPTH, -1), "fox_b_f": take(1)[:FH].reshape(1, FH), "swa_sinks": take(2)[:QH].reshape(1, QH),
               "ffn_conv_w": lax.dynamic_slice_in_dim(take(3).reshape(DEPTH, 3, N_CHIPS * F2c), chip * F2c, F2c, axis=2),
               "ffn_conv_b": take(4).reshape(DEPTH, -1)}
    for k, nm in enumerate(("ln_mix_g", "ln_mix_b", "ln_ffn_g", "ln_ffn_b")):
        g_small[nm] = take(5 + k).reshape(DEPTH, D)
    small = list(g_small)
    pack = lambda pre: _pad_rows(jnp.concatenate([(a[pre + nm] if pre else a[nm]).reshape(-1) for nm in small]))
    gp = _pad_rows(jnp.concatenate([g_small[nm].reshape(-1) for nm in small]))
    ds_, ms_, vs_ = _adamw_small(pack(""), gp, pack("m_"), pack("v_"), "adamw_small")
    off = 0
    for nm in small:
        n_el = a[nm].size
        out["grad_" + nm] = g_small[nm]
        for pre, arr in (("delta_", ds_), ("new_m_", ms_), ("new_v_", vs_)):
            out[pre + nm] = arr.reshape(-1)[off:off + n_el].reshape(a[nm].shape)
        off += n_el

    dmod_all = allp.reshape(N_DEV, -1)[:, :DEPTH * N_CHIPS * NC].reshape(N_DEV, DEPTH, N_CHIPS * NC)
    dmod_mine = lax.dynamic_slice_in_dim(dmod_all, chip * NC, NC, axis=2).transpose(1, 0, 2)

    grads = pp.join_landed()
    token = pp.last_start("fox_w_in", [ds_, dmod_mine] + list(grads.values()))
    ada = _ada_grad_adamw(cact.T, dmod_mine + token[0, 0], a["ada_w"], a["m_ada_w"], a["v_ada_w"], "ada_grad")
    for pre, arr in zip(("grad_", "delta_", "new_m_", "new_v_"), ada):
        out[pre + "ada_w"] = arr
    upd = {}
    for k, nm, l in zip(keys[1:], names[1:], layers[1:]):
        upd[nm] = _adamw(a[nm], grads[k], a["m_" + nm], a["v_" + nm], l, upd.get(nm, ()), f"adamw_{k}", after=(token,))
    g_last = pp.last_finish([ada[1]] + [res[1] for res in upd.values()])
    tview = lambda t: jnp.swapaxes(t, 1, 2)
    res = _adamw(tview(a["fox_w_in"]), g_last.T, tview(a["m_fox_w_in"]), tview(a["v_fox_w_in"]), 0, (), "adamw_fox_w_in", by_cols=True)
    upd["fox_w_in"] = [tview(r) for r in res]
    for nm, res in upd.items():
        for pre, arr in zip(("grad_", "delta_", "new_m_", "new_v_"), res):
            out[pre + nm] = arr
    return out


_WEIGHTS = ["fox_w_in", "fox_b_f", "fox_w_o", "swa_w_in", "swa_sinks", "swa_w_o", "ada_w", "ada_b", "ffn_w_up", "ffn_conv_w",
            "ffn_conv_b", "ffn_w_down", "ln_mix_g", "ln_mix_b", "ln_ffn_g", "ln_ffn_b"]
_INPUTS = (["x", "c", "positions"] + _WEIGHTS + ["loss_target"] + ["m_" + w for w in _WEIGHTS] + ["v_" + w for w in _WEIGHTS])


def kernel(x, c, positions, fox_w_in, fox_b_f, fox_w_o, swa_w_in, swa_sinks, swa_w_o, ada_w, ada_b, ffn_w_up, ffn_conv_w, ffn_conv_b, ffn_w_down, ln_mix_g, ln_mix_b, ln_ffn_g, ln_ffn_b, loss_target, m_fox_w_in, m_fox_b_f, m_fox_w_o, m_swa_w_in, m_swa_sinks, m_swa_w_o, m_ada_w, m_ada_b, m_ffn_w_up, m_ffn_conv_w, m_ffn_conv_b, m_ffn_w_down, m_ln_mix_g, m_ln_mix_b, m_ln_ffn_g, m_ln_ffn_b, v_fox_w_in, v_fox_b_f, v_fox_w_o, v_swa_w_in, v_swa_sinks, v_swa_w_o, v_ada_w, v_ada_b, v_ffn_w_up, v_ffn_conv_w, v_ffn_conv_b, v_ffn_w_down, v_ln_mix_g, v_ln_mix_b, v_ln_ffn_g, v_ln_ffn_b):
    args = (x, c, positions, fox_w_in, fox_b_f, fox_w_o, swa_w_in, swa_sinks, swa_w_o, ada_w, ada_b, ffn_w_up, ffn_conv_w, ffn_conv_b, ffn_w_down, ln_mix_g, ln_mix_b, ln_ffn_g, ln_ffn_b, loss_target, m_fox_w_in, m_fox_b_f, m_fox_w_o, m_swa_w_in, m_swa_sinks, m_swa_w_o, m_ada_w, m_ada_b, m_ffn_w_up, m_ffn_conv_w, m_ffn_conv_b, m_ffn_w_down, m_ln_mix_g, m_ln_mix_b, m_ln_ffn_g, m_ln_ffn_b, v_fox_w_in, v_fox_b_f, v_fox_w_o, v_swa_w_in, v_swa_sinks, v_swa_w_o, v_ada_w, v_ada_b, v_ffn_w_up, v_ffn_conv_w, v_ffn_conv_b, v_ffn_w_down, v_ln_mix_g, v_ln_mix_b, v_ln_ffn_g, v_ln_ffn_b)
    out = _step(PROD, dict(zip(_INPUTS, args)))
    order = ["loss", "grad_x"] + [p + w for p in ("grad_", "delta_", "new_m_", "new_v_") for w in _WEIGHTS]
    return tuple(out[k] for k in order)
```
